```python
import math
import jax, jax.numpy as jnp
from jax import lax
import numpy as np

D_MODEL = 2048
BATCH = 8
SEQ = 4096
DEPTH = 2

MEM_LEN = 256
CHUNK = 128
GMLP_WIDTH = 1024
GMLP_GROUPS = 8
GMLP_GROUP_DIM = GMLP_WIDTH // GMLP_GROUPS
N_Q_HEADS = 16
N_KV_HEADS = 4
HEAD_DIM = 64
ATTN_WIDTH = N_Q_HEADS * HEAD_DIM
KV_WIDTH = N_KV_HEADS * HEAD_DIM
WINDOW = 128
ROPE_THETA = 10000.0
X_HEADS = 4
X_HEAD_DIM = 128
X_WIDTH = X_HEADS * X_HEAD_DIM
D_FF = 4 * D_MODEL
LN_EPS = 1e-5
ALPHA = (2 * DEPTH) ** 0.25
BETA = (8 * DEPTH) ** -0.25

OFF_U = GMLP_WIDTH
OFF_V = 2 * GMLP_WIDTH
OFF_Q = OFF_V + ATTN_WIDTH
OFF_K = OFF_Q + KV_WIDTH
OFF_VA = OFF_K + KV_WIDTH
OFF_GA = OFF_VA + D_MODEL
IN_WIDTH = OFF_GA + D_MODEL

kernel_name = "hybrid_gmlp_swa_sink_deepnorm_decoder"


def layer_norm(x, g, b):
    xf = x.astype(jnp.float32)
    mu = jnp.mean(xf, axis=-1, keepdims=True)
    var = jnp.mean(jnp.square(xf - mu), axis=-1, keepdims=True)
    y = (xf - mu) * lax.rsqrt(var + LN_EPS)
    return (y * g.astype(jnp.float32) + b.astype(jnp.float32)).astype(x.dtype)


def rope_tables(seq):
    inv = 1.0 / (ROPE_THETA ** (jnp.arange(0, HEAD_DIM, 2, dtype=jnp.float32) / HEAD_DIM))
    pos = jnp.arange(seq, dtype=jnp.float32)
    ang = pos[:, None] * inv[None, :]
    return jnp.cos(ang), jnp.sin(ang)


def apply_rope(x, cos, sin):
    xf = x.astype(jnp.float32)
    x1, x2 = jnp.split(xf, 2, axis=-1)
    c = cos[None, :, None, :]
    s = sin[None, :, None, :]
    return jnp.concatenate([x1 * c - x2 * s, x2 * c + x1 * s], axis=-1).astype(x.dtype)


def chunked_spatial_gating(u, v, ln_g, ln_b, w_s, b_s):
    bsz, seq, _ = u.shape
    n_chunks = seq // CHUNK
    v = layer_norm(v, ln_g, ln_b)
    vc = v.reshape(bsz, n_chunks, CHUNK, GMLP_GROUPS, GMLP_GROUP_DIM)
    causal = jnp.tril(jnp.ones((CHUNK, CHUNK), dtype=bool))
    w = jnp.where(causal[None], w_s, 0.0)
    mixed = jnp.einsum('gts,bnsgc->bntgc', w, vc) + b_s.T[None, None, :, :, None]
    return u * mixed.reshape(bsz, seq, GMLP_WIDTH)


def sliding_window_attention(q, k, v, sinks):
    bsz, seq, _, _ = q.shape
    n_blk = seq // WINDOW
    grp = N_Q_HEADS // N_KV_HEADS
    qb = q.reshape(bsz, n_blk, WINDOW, N_KV_HEADS, grp, HEAD_DIM)
    kb = k.reshape(bsz, n_blk, WINDOW, N_KV_HEADS, HEAD_DIM)
    vb = v.reshape(bsz, n_blk, WINDOW, N_KV_HEADS, HEAD_DIM)
    pad = ((0, 0), (1, 0), (0, 0), (0, 0), (0, 0))
    kk = jnp.concatenate([jnp.pad(kb, pad)[:, :-1], kb], axis=2)
    vv = jnp.concatenate([jnp.pad(vb, pad)[:, :-1], vb], axis=2)
    scores = jnp.einsum('bnqhgd,bnkhd->bnhgqk', qb, kk).astype(jnp.float32) * (HEAD_DIM ** -0.5)
    q_loc = jnp.arange(WINDOW)[:, None]
    k_loc = jnp.arange(2 * WINDOW)[None, :]
    band = (k_loc <= q_loc + WINDOW) & (k_loc > q_loc)
    blk = jnp.arange(n_blk)[:, None, None]
    valid = band[None] & (blk * WINDOW + k_loc[None] - WINDOW >= 0)
    scores = jnp.where(valid[None, :, None, None], scores, -jnp.inf)
    sink = sinks.astype(jnp.float32).reshape(N_KV_HEADS, grp)[None, None, :, :, None, None]
    m = jnp.maximum(jnp.max(scores, axis=-1, keepdims=True), sink)
    p = jnp.exp(scores - m)
    probs = (p / (jnp.sum(p, axis=-1, keepdims=True) + jnp.exp(sink - m))).astype(v.dtype)
    out = jnp.einsum('bnhgqk,bnkhd->bnqhgd', probs, vv)
    return out.reshape(bsz, seq, ATTN_WIDTH)


def hybrid_mixer(x, w_in, b_gate, ln_v_g, ln_v_b, w_s, b_s, sinks, w_br_a, w_br_b, w_o, cos, sin):
    bsz, seq, _ = x.shape
    proj = x @ w_in
    u, v, q, k, va, ga, gb = jnp.split(proj, [OFF_U, OFF_V, OFF_Q, OFF_K, OFF_VA, OFF_GA], axis=-1)
    ya = chunked_spatial_gating(jax.nn.gelu(u), jax.nn.gelu(v), ln_v_g, ln_v_b, w_s, b_s) @ w_br_a
    q = apply_rope(q.reshape(bsz, seq, N_Q_HEADS, HEAD_DIM), cos, sin)
    k = apply_rope(k.reshape(bsz, seq, N_KV_HEADS, HEAD_DIM), cos, sin)
    va = va.reshape(bsz, seq, N_KV_HEADS, HEAD_DIM)
    yb = sliding_window_attention(q, k, va, sinks) @ w_br_b
    merged = jax.nn.sigmoid(ga + b_gate[:D_MODEL]) * ya + jax.nn.sigmoid(gb + b_gate[D_MODEL:]) * yb
    return merged @ w_o


def memory_cross_attention(x, mem, w_xq, w_xkv, w_xo):
    bsz, seq, _ = x.shape
    q = (x @ w_xq).reshape(bsz, seq, X_HEADS, X_HEAD_DIM)
    k, v = jnp.split(mem @ w_xkv, 2, axis=-1)
    k = k.reshape(bsz, MEM_LEN, X_HEADS, X_HEAD_DIM)
    v = v.reshape(bsz, MEM_LEN, X_HEADS, X_HEAD_DIM)
    s = jnp.einsum('bqhd,bkhd->bhqk', q, k).astype(jnp.float32) * (X_HEAD_DIM ** -0.5)
    p = jax.nn.softmax(s, axis=-1).astype(v.dtype)
    o = jnp.einsum('bhqk,bkhd->bqhd', p, v).reshape(bsz, seq, X_WIDTH)
    return o @ w_xo


def squared_relu_mlp(x, w_up, w_down):
    return jnp.square(jax.nn.relu(x @ w_up)) @ w_down


def _fwd_setup_inputs(seed: int = 0) -> dict:
    key = jax.random.key(seed)
    ks = jax.random.split(key, 26)
    f32 = jnp.float32
    L, D = DEPTH, D_MODEL

    def nrm(k, shape, scale):
        return jax.random.normal(k, shape, f32) * scale

    return {
        "x": nrm(ks[0], (BATCH, SEQ, D), 1.0),
        "mem": nrm(ks[1], (BATCH, MEM_LEN, D), 1.0),
        "w_in": nrm(ks[2], (L, D, IN_WIDTH), D ** -0.5),
        "b_gate": nrm(ks[3], (L, 2 * D), 0.1),
        "ln_v_g": 1.0 + nrm(ks[4], (L, GMLP_WIDTH), 0.02),
        "ln_v_b": nrm(ks[5], (L, GMLP_WIDTH), 0.02),
        "w_s": nrm(ks[6], (L, GMLP_GROUPS, CHUNK, CHUNK), 0.05),
        "b_s": 1.0 + nrm(ks[7], (L, GMLP_GROUPS, CHUNK), 0.1),
        "sinks": nrm(ks[8], (L, N_Q_HEADS), 0.5),
        "w_br_a": nrm(ks[9], (L, GMLP_WIDTH, D), GMLP_WIDTH ** -0.5),
        "w_br_b": nrm(ks[10], (L, ATTN_WIDTH, D), ATTN_WIDTH ** -0.5),
        "w_o": nrm(ks[11], (L, D, D), BETA * D ** -0.5),
        "ln1_g": 1.0 + nrm(ks[12], (L, D), 0.02),
        "ln1_b": nrm(ks[13], (L, D), 0.02),
        "w_xq": nrm(ks[14], (L, D, X_WIDTH), D ** -0.5),
        "w_xkv": nrm(ks[15], (L, D, 2 * X_WIDTH), D ** -0.5),
        "w_xo": nrm(ks[16], (L, X_WIDTH, D), BETA * X_WIDTH ** -0.5),
        "ln2_g": 1.0 + nrm(ks[17], (L, D), 0.02),
        "ln2_b": nrm(ks[18], (L, D), 0.02),
        "w_up": nrm(ks[19], (L, D, D_FF), D ** -0.5),
        "w_down": nrm(ks[20], (L, D_FF, D), BETA * D_FF ** -0.5),
        "ln3_g": 1.0 + nrm(ks[21], (L, D), 0.02),
        "ln3_b": nrm(ks[22], (L, D), 0.02),
    }


def _fwd_reference(x, mem, w_in, b_gate, ln_v_g, ln_v_b, w_s, b_s, sinks, w_br_a, w_br_b, w_o,
              ln1_g, ln1_b, w_xq, w_xkv, w_xo, ln2_g, ln2_b, w_up, w_down, ln3_g, ln3_b):
    cos, sin = rope_tables(x.shape[1])
    for l in range(DEPTH):
        y = hybrid_mixer(x, w_in[l], b_gate[l], ln_v_g[l], ln_v_b[l], w_s[l], b_s[l], sinks[l],
                         w_br_a[l], w_br_b[l], w_o[l], cos, sin)
        x = layer_norm(ALPHA * x + y, ln1_g[l], ln1_b[l])
        y = memory_cross_attention(x, mem, w_xq[l], w_xkv[l], w_xo[l])
        x = layer_norm(ALPHA * x + y, ln2_g[l], ln2_b[l])
        y = squared_relu_mlp(x, w_up[l], w_down[l])
        x = layer_norm(ALPHA * x + y, ln3_g[l], ln3_b[l])
    return x


import jax as _jax
import jax.numpy as _jnp

TWIN_FORMAT = 'train_step'
FWD_PARAMS = ['x', 'mem', 'w_in', 'b_gate', 'ln_v_g', 'ln_v_b', 'w_s', 'b_s', 'sinks', 'w_br_a', 'w_br_b', 'w_o', 'ln1_g', 'ln1_b', 'w_xq', 'w_xkv', 'w_xo', 'ln2_g', 'ln2_b', 'w_up', 'w_down', 'ln3_g', 'ln3_b']
TWIN_WEIGHTS = ['w_in', 'b_gate', 'ln_v_g', 'ln_v_b', 'w_s', 'b_s', 'sinks', 'w_br_a', 'w_br_b', 'w_o', 'ln1_g', 'ln1_b', 'w_xq', 'w_xkv', 'w_xo', 'ln2_g', 'ln2_b', 'w_up', 'w_down', 'ln3_g', 'ln3_b']
TWIN_DIFF_INPUT = 'x'
TWIN_INPUTS = ['x', 'mem', 'w_in', 'b_gate', 'ln_v_g', 'ln_v_b', 'w_s', 'b_s', 'sinks', 'w_br_a', 'w_br_b', 'w_o', 'ln1_g', 'ln1_b', 'w_xq', 'w_xkv', 'w_xo', 'ln2_g', 'ln2_b', 'w_up', 'w_down', 'ln3_g', 'ln3_b', 'loss_target', 'm_w_in', 'm_b_gate', 'm_ln_v_g', 'm_ln_v_b', 'm_w_s', 'm_b_s', 'm_sinks', 'm_w_br_a', 'm_w_br_b', 'm_w_o', 'm_ln1_g', 'm_ln1_b', 'm_w_xq', 'm_w_xkv', 'm_w_xo', 'm_ln2_g', 'm_ln2_b', 'm_w_up', 'm_w_down', 'm_ln3_g', 'm_ln3_b', 'v_w_in', 'v_b_gate', 'v_ln_v_g', 'v_ln_v_b', 'v_w_s', 'v_b_s', 'v_sinks', 'v_w_br_a', 'v_w_br_b', 'v_w_o', 'v_ln1_g', 'v_ln1_b', 'v_w_xq', 'v_w_xkv', 'v_w_xo', 'v_ln2_g', 'v_ln2_b', 'v_w_up', 'v_w_down', 'v_ln3_g', 'v_ln3_b']
TWIN_OUTPUTS = ['loss', 'grad_x', 'grad_w_in', 'grad_b_gate', 'grad_ln_v_g', 'grad_ln_v_b', 'grad_w_s', 'grad_b_s', 'grad_sinks', 'grad_w_br_a', 'grad_w_br_b', 'grad_w_o', 'grad_ln1_g', 'grad_ln1_b', 'grad_w_xq', 'grad_w_xkv', 'grad_w_xo', 'grad_ln2_g', 'grad_ln2_b', 'grad_w_up', 'grad_w_down', 'grad_ln3_g', 'grad_ln3_b', 'delta_w_in', 'delta_b_gate', 'delta_ln_v_g', 'delta_ln_v_b', 'delta_w_s', 'delta_b_s', 'delta_sinks', 'delta_w_br_a', 'delta_w_br_b', 'delta_w_o', 'delta_ln1_g', 'delta_ln1_b', 'delta_w_xq', 'delta_w_xkv', 'delta_w_xo', 'delta_ln2_g', 'delta_ln2_b', 'delta_w_up', 'delta_w_down', 'delta_ln3_g', 'delta_ln3_b', 'new_m_w_in', 'new_m_b_gate', 'new_m_ln_v_g', 'new_m_ln_v_b', 'new_m_w_s', 'new_m_b_s', 'new_m_sinks', 'new_m_w_br_a', 'new_m_w_br_b', 'new_m_w_o', 'new_m_ln1_g', 'new_m_ln1_b', 'new_m_w_xq', 'new_m_w_xkv', 'new_m_w_xo', 'new_m_ln2_g', 'new_m_ln2_b', 'new_m_w_up', 'new_m_w_down', 'new_m_ln3_g', 'new_m_ln3_b', 'new_v_w_in', 'new_v_b_gate', 'new_v_ln_v_g', 'new_v_ln_v_b', 'new_v_w_s', 'new_v_b_s', 'new_v_sinks', 'new_v_w_br_a', 'new_v_w_br_b', 'new_v_w_o', 'new_v_ln1_g', 'new_v_ln1_b', 'new_v_w_xq', 'new_v_w_xkv', 'new_v_w_xo', 'new_v_ln2_g', 'new_v_ln2_b', 'new_v_w_up', 'new_v_w_down', 'new_v_ln3_g', 'new_v_ln3_b']
TWIN_LEAF_KINDS = {'loss': 'loss', 'grad_x': 'grad_x', 'grad_w_in': 'grad_w', 'grad_b_gate': 'grad_w', 'grad_ln_v_g': 'grad_w', 'grad_ln_v_b': 'grad_w', 'grad_w_s': 'grad_w', 'grad_b_s': 'grad_w', 'grad_sinks': 'grad_w', 'grad_w_br_a': 'grad_w', 'grad_w_br_b': 'grad_w', 'grad_w_o': 'grad_w', 'grad_ln1_g': 'grad_w', 'grad_ln1_b': 'grad_w', 'grad_w_xq': 'grad_w', 'grad_w_xkv': 'grad_w', 'grad_w_xo': 'grad_w', 'grad_ln2_g': 'grad_w', 'grad_ln2_b': 'grad_w', 'grad_w_up': 'grad_w', 'grad_w_down': 'grad_w', 'grad_ln3_g': 'grad_w', 'grad_ln3_b': 'grad_w', 'delta_w_in': 'delta_w', 'delta_b_gate': 'delta_w', 'delta_ln_v_g': 'delta_w', 'delta_ln_v_b': 'delta_w', 'delta_w_s': 'delta_w', 'delta_b_s': 'delta_w', 'delta_sinks': 'delta_w', 'delta_w_br_a': 'delta_w', 'delta_w_br_b': 'delta_w', 'delta_w_o': 'delta_w', 'delta_ln1_g': 'delta_w', 'delta_ln1_b': 'delta_w', 'delta_w_xq': 'delta_w', 'delta_w_xkv': 'delta_w', 'delta_w_xo': 'delta_w', 'delta_ln2_g': 'delta_w', 'delta_ln2_b': 'delta_w', 'delta_w_up': 'delta_w', 'delta_w_down': 'delta_w', 'delta_ln3_g': 'delta_w', 'delta_ln3_b': 'delta_w', 'new_m_w_in': 'new_m', 'new_m_b_gate': 'new_m', 'new_m_ln_v_g': 'new_m', 'new_m_ln_v_b': 'new_m', 'new_m_w_s': 'new_m', 'new_m_b_s': 'new_m', 'new_m_sinks': 'new_m', 'new_m_w_br_a': 'new_m', 'new_m_w_br_b': 'new_m', 'new_m_w_o': 'new_m', 'new_m_ln1_g': 'new_m', 'new_m_ln1_b': 'new_m', 'new_m_w_xq': 'new_m', 'new_m_w_xkv': 'new_m', 'new_m_w_xo': 'new_m', 'new_m_ln2_g': 'new_m', 'new_m_ln2_b': 'new_m', 'new_m_w_up': 'new_m', 'new_m_w_down': 'new_m', 'new_m_ln3_g': 'new_m', 'new_m_ln3_b': 'new_m', 'new_v_w_in': 'new_v', 'new_v_b_gate': 'new_v', 'new_v_ln_v_g': 'new_v', 'new_v_ln_v_b': 'new_v', 'new_v_w_s': 'new_v', 'new_v_b_s': 'new_v', 'new_v_sinks': 'new_v', 'new_v_w_br_a': 'new_v', 'new_v_w_br_b': 'new_v', 'new_v_w_o': 'new_v', 'new_v_ln1_g': 'new_v', 'new_v_ln1_b': 'new_v', 'new_v_w_xq': 'new_v', 'new_v_w_xkv': 'new_v', 'new_v_w_xo': 'new_v', 'new_v_ln2_g': 'new_v', 'new_v_ln2_b': 'new_v', 'new_v_w_up': 'new_v', 'new_v_w_down': 'new_v', 'new_v_ln3_g': 'new_v', 'new_v_ln3_b': 'new_v'}


def _forward(args):
    return _fwd_reference(*[args[k] for k in FWD_PARAMS])


def _output_shape():
    def fwd():
        inp = _fwd_setup_inputs(0)
        return _fwd_reference(*[inp[k] for k in FWD_PARAMS])
    out = _jax.eval_shape(fwd)
    return out.shape, out.dtype

N_MICROBATCH = 1
ADAM_LR = 0.001
ADAM_B1 = 0.9
ADAM_B2 = 0.999
ADAM_EPS = 1e-08
ADAM_WD = 0.01
ADAM_STEP = 10
PER_EXAMPLE_BATCH_AXIS = {'x': 0, 'mem': 0, 'loss_target': 0}
SHARED_INPUTS = []
_WEIGHT_DTYPES = {'w_in': _jnp.float32, 'b_gate': _jnp.float32, 'ln_v_g': _jnp.float32, 'ln_v_b': _jnp.float32, 'w_s': _jnp.float32, 'b_s': _jnp.float32, 'sinks': _jnp.float32, 'w_br_a': _jnp.float32, 'w_br_b': _jnp.float32, 'w_o': _jnp.float32, 'ln1_g': _jnp.float32, 'ln1_b': _jnp.float32, 'w_xq': _jnp.float32, 'w_xkv': _jnp.float32, 'w_xo': _jnp.float32, 'ln2_g': _jnp.float32, 'ln2_b': _jnp.float32, 'w_up': _jnp.float32, 'w_down': _jnp.float32, 'ln3_g': _jnp.float32, 'ln3_b': _jnp.float32}
MOMENT_SCALE = {'w_in': 8.328513e-03, 'b_gate': 5.309313e-03, 'ln_v_g': 6.500416e-03, 'ln_v_b': 6.577306e-03, 'w_s': 1.144777e-02, 'b_s': 1.621943e-02, 'sinks': 4.504710e-03, 'w_br_a': 1.697786e-02, 'w_br_b': 5.399014e-03, 'w_o': 3.536885e-02, 'ln1_g': 4.675808e-01, 'ln1_b': 2.853899e-01, 'w_xq': 6.742789e-03, 'w_xkv': 7.181964e-03, 'w_xo': 7.696344e-03, 'ln2_g': 4.675405e-01, 'ln2_b': 2.855069e-01, 'w_up': 2.098710e-02, 'w_down': 1.079966e-01, 'ln3_g': 1.137210e+01, 'ln3_b': 2.633542e+00}


def _to_microbatches(a, axis):
    t = _jnp.moveaxis(a, axis, 0)
    t = t.reshape((N_MICROBATCH, t.shape[0] // N_MICROBATCH) + t.shape[1:])
    return _jnp.moveaxis(t, 1, axis + 1)


def setup_inputs(seed: int = 0) -> dict:
    inp = _fwd_setup_inputs(seed)
    key = _jax.random.fold_in(_jax.random.key(seed), 7919)
    shape, _ = _output_shape()
    out = dict(inp)
    out["loss_target"] = _jax.random.normal(_jax.random.fold_in(key, 0), shape, _jnp.float32)
    for i, name in enumerate(TWIN_WEIGHTS):
        w = inp[name].astype(_jnp.float32)
        if MOMENT_SCALE is None:
            s = _jnp.sqrt(_jnp.mean(_jnp.square(w)) + 1e-30)
        else:
            s = MOMENT_SCALE[name]
        km, kv = _jax.random.split(_jax.random.fold_in(key, i + 1))
        out[name] = w
        out["m_" + name] = s * _jax.random.normal(km, w.shape, _jnp.float32)
        out["v_" + name] = (s * s) * _jax.random.uniform(kv, w.shape, _jnp.float32, 0.5, 1.5)
    if N_MICROBATCH > 1:
        for name, axis in PER_EXAMPLE_BATCH_AXIS.items():
            out[name] = _to_microbatches(out[name], axis)
    return {'x': out['x'], 'mem': out['mem'], 'w_in': out['w_in'], 'b_gate': out['b_gate'], 'ln_v_g': out['ln_v_g'], 'ln_v_b': out['ln_v_b'], 'w_s': out['w_s'], 'b_s': out['b_s'], 'sinks': out['sinks'], 'w_br_a': out['w_br_a'], 'w_br_b': out['w_br_b'], 'w_o': out['w_o'], 'ln1_g': out['ln1_g'], 'ln1_b': out['ln1_b'], 'w_xq': out['w_xq'], 'w_xkv': out['w_xkv'], 'w_xo': out['w_xo'], 'ln2_g': out['ln2_g'], 'ln2_b': out['ln2_b'], 'w_up': out['w_up'], 'w_down': out['w_down'], 'ln3_g': out['ln3_g'], 'ln3_b': out['ln3_b'], 'loss_target': out['loss_target'], 'm_w_in': out['m_w_in'], 'm_b_gate': out['m_b_gate'], 'm_ln_v_g': out['m_ln_v_g'], 'm_ln_v_b': out['m_ln_v_b'], 'm_w_s': out['m_w_s'], 'm_b_s': out['m_b_s'], 'm_sinks': out['m_sinks'], 'm_w_br_a': out['m_w_br_a'], 'm_w_br_b': out['m_w_br_b'], 'm_w_o': out['m_w_o'], 'm_ln1_g': out['m_ln1_g'], 'm_ln1_b': out['m_ln1_b'], 'm_w_xq': out['m_w_xq'], 'm_w_xkv': out['m_w_xkv'], 'm_w_xo': out['m_w_xo'], 'm_ln2_g': out['m_ln2_g'], 'm_ln2_b': out['m_ln2_b'], 'm_w_up': out['m_w_up'], 'm_w_down': out['m_w_down'], 'm_ln3_g': out['m_ln3_g'], 'm_ln3_b': out['m_ln3_b'], 'v_w_in': out['v_w_in'], 'v_b_gate': out['v_b_gate'], 'v_ln_v_g': out['v_ln_v_g'], 'v_ln_v_b': out['v_ln_v_b'], 'v_w_s': out['v_w_s'], 'v_b_s': out['v_b_s'], 'v_sinks': out['v_sinks'], 'v_w_br_a': out['v_w_br_a'], 'v_w_br_b': out['v_w_br_b'], 'v_w_o': out['v_w_o'], 'v_ln1_g': out['v_ln1_g'], 'v_ln1_b': out['v_ln1_b'], 'v_w_xq': out['v_w_xq'], 'v_w_xkv': out['v_w_xkv'], 'v_w_xo': out['v_w_xo'], 'v_ln2_g': out['v_ln2_g'], 'v_ln2_b': out['v_ln2_b'], 'v_w_up': out['v_w_up'], 'v_w_down': out['v_w_down'], 'v_ln3_g': out['v_ln3_g'], 'v_ln3_b': out['v_ln3_b']}


def _loss(weights, diff, rest, loss_target):
    with _jax.named_scope("forward"):
        args = {**rest, TWIN_DIFF_INPUT: diff, **{k: w.astype(_WEIGHT_DTYPES[k]) for k, w in weights.items()}}
        y = _forward(args)
    with _jax.named_scope("loss_head"):
        err = _jnp.square(y.astype(_jnp.float32) - loss_target)
        return 0.5 * _jnp.sum(_jnp.mean(err, axis=-1)) if err.ndim else 0.5 * err


def _adamw(w, g, m, v):
    m = ADAM_B1 * m + (1.0 - ADAM_B1) * g
    v = ADAM_B2 * v + (1.0 - ADAM_B2) * _jnp.square(g)
    m_hat = m / (1.0 - ADAM_B1 ** ADAM_STEP)
    v_hat = v / (1.0 - ADAM_B2 ** ADAM_STEP)
    delta = -ADAM_LR * (m_hat / (_jnp.sqrt(v_hat) + ADAM_EPS) + ADAM_WD * w)
    return delta, m, v


def reference(x, mem, w_in, b_gate, ln_v_g, ln_v_b, w_s, b_s, sinks, w_br_a, w_br_b, w_o, ln1_g, ln1_b, w_xq, w_xkv, w_xo, ln2_g, ln2_b, w_up, w_down, ln3_g, ln3_b, loss_target, m_w_in, m_b_gate, m_ln_v_g, m_ln_v_b, m_w_s, m_b_s, m_sinks, m_w_br_a, m_w_br_b, m_w_o, m_ln1_g, m_ln1_b, m_w_xq, m_w_xkv, m_w_xo, m_ln2_g, m_ln2_b, m_w_up, m_w_down, m_ln3_g, m_ln3_b, v_w_in, v_b_gate, v_ln_v_g, v_ln_v_b, v_w_s, v_b_s, v_sinks, v_w_br_a, v_w_br_b, v_w_o, v_ln1_g, v_ln1_b, v_w_xq, v_w_xkv, v_w_xo, v_ln2_g, v_ln2_b, v_w_up, v_w_down, v_ln3_g, v_ln3_b):
    given = dict(x=x, mem=mem, w_in=w_in, b_gate=b_gate, ln_v_g=ln_v_g, ln_v_b=ln_v_b, w_s=w_s, b_s=b_s, sinks=sinks, w_br_a=w_br_a, w_br_b=w_br_b, w_o=w_o, ln1_g=ln1_g, ln1_b=ln1_b, w_xq=w_xq, w_xkv=w_xkv, w_xo=w_xo, ln2_g=ln2_g, ln2_b=ln2_b, w_up=w_up, w_down=w_down, ln3_g=ln3_g, ln3_b=ln3_b, loss_target=loss_target, m_w_in=m_w_in, m_b_gate=m_b_gate, m_ln_v_g=m_ln_v_g, m_ln_v_b=m_ln_v_b, m_w_s=m_w_s, m_b_s=m_b_s, m_sinks=m_sinks, m_w_br_a=m_w_br_a, m_w_br_b=m_w_br_b, m_w_o=m_w_o, m_ln1_g=m_ln1_g, m_ln1_b=m_ln1_b, m_w_xq=m_w_xq, m_w_xkv=m_w_xkv, m_w_xo=m_w_xo, m_ln2_g=m_ln2_g, m_ln2_b=m_ln2_b, m_w_up=m_w_up, m_w_down=m_w_down, m_ln3_g=m_ln3_g, m_ln3_b=m_ln3_b, v_w_in=v_w_in, v_b_gate=v_b_gate, v_ln_v_g=v_ln_v_g, v_ln_v_b=v_ln_v_b, v_w_s=v_w_s, v_b_s=v_b_s, v_sinks=v_sinks, v_w_br_a=v_w_br_a, v_w_br_b=v_w_br_b, v_w_o=v_w_o, v_ln1_g=v_ln1_g, v_ln1_b=v_ln1_b, v_w_xq=v_w_xq, v_w_xkv=v_w_xkv, v_w_xo=v_w_xo, v_ln2_g=v_ln2_g, v_ln2_b=v_ln2_b, v_w_up=v_w_up, v_w_down=v_w_down, v_ln3_g=v_ln3_g, v_ln3_b=v_ln3_b)
    weights = {n: given[n] for n in TWIN_WEIGHTS}
    shared = {n: given[n] for n in SHARED_INPUTS}
    per_example = {n: given[n] for n in ['x', 'mem']}
    grad_fn = _jax.value_and_grad(_loss, argnums=(0, 1))

    def one_microbatch(ex, loss_target):
        ex = dict(ex)
        diff = ex.pop(TWIN_DIFF_INPUT)
        return grad_fn(weights, diff, {**shared, **ex}, loss_target)

    if N_MICROBATCH == 1:
        loss, (grad_w, grad_x) = one_microbatch(per_example, given["loss_target"])
    else:
        def body(carry, xs):
            loss_sum, grad_sum = carry
            l_k, (gw_k, gx_k) = one_microbatch(xs[0], xs[1])
            with _jax.named_scope("update"):
                return (loss_sum + l_k, _jax.tree.map(_jnp.add, grad_sum, gw_k)), gx_k

        init = (_jnp.zeros((), _jnp.float32), _jax.tree.map(_jnp.zeros_like, weights))
        (loss, grad_w), grad_x = _jax.lax.scan(body, init, (per_example, given["loss_target"]))
    with _jax.named_scope("update"):
        delta_w, new_m, new_v = {}, {}, {}
        for n in TWIN_WEIGHTS:
            delta_w[n], new_m[n], new_v[n] = _adamw(weights[n], grad_w[n], given["m_" + n], given["v_" + n])
    return (loss, grad_x, *[grad_w[n] for n in TWIN_WEIGHTS], *[delta_w[n] for n in TWIN_WEIGHTS],
            *[new_m[n] for n in TWIN_WEIGHTS], *[new_v[n] for n in TWIN_WEIGHTS])
```

```python
import collections
import functools

import jax
import jax.numpy as jnp
from jax import lax
from jax.experimental import pallas as pl
from jax.experimental.pallas import tpu as pltpu

F32 = jnp.float32
BF16 = jnp.bfloat16
MESH = pl.DeviceIdType.MESH

N_DEV = 8
DEPTH = 2
CHUNK = 128
HEAD_DIM = 64
ROPE_HALF = HEAD_DIM // 2
X_HEAD_DIM = 128
GROUP_DIM = 128
LANES = 128
ROPE_THETA = 10000.0
LN_EPS = 1e-5
ALPHA = (2 * DEPTH) ** 0.25
ADAM_LR = 0.001
ADAM_B1 = 0.9
ADAM_B2 = 0.999
ADAM_EPS = 1e-08
ADAM_WD = 0.01
ADAM_STEP = 10
VMEM_BYTES = 64 * 1024 * 1024
VMEM_TEMP_BYTES = 20 * 1024 * 1024

Dims = collections.namedtuple("Dims", "T D IN GW G AW KVW HQ HKV XW XH MEM DFF")


def _offsets(d):
    off_v = d.GW
    off_q = 2 * d.GW
    off_k = off_q + d.AW
    off_va = off_k + d.KVW
    off_ga = off_va + d.KVW
    off_gb = off_ga + d.D
    return off_v, off_q, off_k, off_va, off_ga, off_gb


def _tile(dim, pref, align=LANES):
    if dim <= pref:
        return dim
    t = pref - pref % align
    while t >= align:
        if dim % t == 0:
            return t
        t -= align
    return dim


def _nbytes(shape, dtype):
    n = 1
    for s in shape:
        n *= s
    return n * jnp.dtype(dtype).itemsize


def _params(sem, blocks, scratch=0):
    need = 2 * sum(_nbytes(s, t) for s, t in blocks) + scratch + VMEM_TEMP_BYTES
    limit = min(max(need, 32 * 1024 * 1024), VMEM_BYTES - 4 * 1024 * 1024)
    return pltpu.CompilerParams(dimension_semantics=sem, vmem_limit_bytes=limit)


def _dot(a, b, kind):
    dn = {"nn": (((1,), (0,)), ((), ())), "nt": (((1,), (1,)), ((), ())), "tn": (((0,), (0,)), ((), ()))}[kind]
    return lax.dot_general(a, b, dn, preferred_element_type=F32)


def _gelu(x):
    c = 0.7978845608028654
    t = jnp.tanh(c * (x + 0.044715 * (x * x * x)))
    return 0.5 * x * (1.0 + t)


def _gelu_grad(x):
    c = 0.7978845608028654
    x2 = x * x
    t = jnp.tanh(c * (x + 0.044715 * (x2 * x)))
    return 0.5 * (1.0 + t) + 0.5 * x * (1.0 - t * t) * (c * (1.0 + 3.0 * 0.044715 * x2))


def _ln_fwd(z, g, b):
    mu = jnp.mean(z, axis=-1, keepdims=True)
    zc = z - mu
    var = jnp.mean(zc * zc, axis=-1, keepdims=True)
    rstd = lax.rsqrt(var + LN_EPS)
    xhat = zc * rstd
    return xhat * g + b, xhat, rstd


def _ln_bwd_vals(dy, xhat, rstd, g):
    gd = dy * g
    m1 = jnp.mean(gd, axis=-1, keepdims=True)
    m2 = jnp.mean(gd * xhat, axis=-1, keepdims=True)
    return rstd * (gd - m1 - xhat * m2)


def _acc_store(ref, val, first):
    @pl.when(first)
    def _():
        ref[...] = val

    @pl.when(jnp.logical_not(first))
    def _():
        ref[...] += val


def _matmul(a, b, kind, *, name, tm=512, tn=1024, tk=2048, out_defs=(("mn", F32),), epilogue=None, extras=()):
    if kind == "nn":
        (M, K), (K2, N) = a.shape, b.shape
    elif kind == "nt":
        (M, K), (N, K2) = a.shape, b.shape
    else:
        (K, M), (K2, N) = a.shape, b.shape
    assert K == K2, (a.shape, b.shape, kind)
    tm, tn, tk = _tile(M, tm), _tile(N, tn), _tile(K, tk)
    nk = K // tk
    blocks = []

    def spec(shape, imap, dtype):
        blocks.append((shape, dtype))
        return pl.BlockSpec(shape, imap)

    if kind == "tn":
        a_spec = spec((tk, tm), lambda j, i, k: (k, i), a.dtype)
    else:
        a_spec = spec((tm, tk), lambda j, i, k: (i, k), a.dtype)
    if kind == "nt":
        b_spec = spec((tn, tk), lambda j, i, k: (j, k), b.dtype)
    else:
        b_spec = spec((tk, tn), lambda j, i, k: (k, j), b.dtype)
    in_specs = [a_spec, b_spec]
    operands = [a, b]
    for arr, ekind, off in extras:
        if ekind == "mn":
            assert off % tn == 0
            in_specs.append(spec((tm, tn), lambda j, i, k, o=off // tn: (i, j + o), arr.dtype))
        elif ekind == "row":
            assert off % tn == 0
            in_specs.append(spec((1, tn), lambda j, i, k, o=off // tn: (0, j + o), arr.dtype))
        else:
            in_specs.append(spec((tm, 1), lambda j, i, k: (i, 0), arr.dtype))
        operands.append(arr)
    out_shape, out_specs = [], []
    for okind, dt in out_defs:
        if okind == "mn":
            out_shape.append(jax.ShapeDtypeStruct((M, N), dt))
            out_specs.append(spec((tm, tn), lambda j, i, k: (i, j), dt))
        elif okind == "m1":
            assert N == tn
            out_shape.append(jax.ShapeDtypeStruct((M, 1), dt))
            out_specs.append(spec((tm, 1), lambda j, i, k: (i, 0), dt))
        else:
            out_shape.append(jax.ShapeDtypeStruct((1, N), F32))
            out_specs.append(spec((1, tn), lambda j, i, k: (0, j), F32))
    n_ex, n_out = len(extras), len(out_defs)
    ep = epilogue if epilogue is not None else (lambda acc: (acc,))

    def body(*refs):
        a_ref, b_ref = refs[0], refs[1]
        ex_refs = refs[2 : 2 + n_ex]
        out_refs = refs[2 + n_ex : 2 + n_ex + n_out]
        i = pl.program_id(1)
        k = pl.program_id(2)
        part = _dot(a_ref[...], b_ref[...], kind)

        def finish(acc):
            vals = ep(acc, *[r[...] for r in ex_refs])
            for (okind, dt), r, v in zip(out_defs, out_refs, vals):
                if okind == "acc":
                    _acc_store(r, v, i == 0)
                else:
                    r[...] = v.astype(dt)

        if nk == 1:
            finish(part)
        else:
            acc_ref = refs[-1]

            @pl.when(k == 0)
            def _():
                acc_ref[...] = part

            @pl.when(k > 0)
            def _():
                acc_ref[...] += part

            @pl.when(k == nk - 1)
            def _():
                finish(acc_ref[...])

    scratch = [pltpu.VMEM((tm, tn), F32)] if nk > 1 else []
    outs = pl.pallas_call(
        body,
        name=name,
        grid=(N // tn, M // tm, nk),
        in_specs=in_specs,
        out_specs=out_specs,
        out_shape=out_shape,
        scratch_shapes=scratch,
        compiler_params=_params(("arbitrary", "arbitrary", "arbitrary"), blocks, _nbytes((tm, tn), F32) if nk > 1 else 0),
    )(*operands)
    return outs if len(outs) > 1 else outs[0]


def _ep_resid_ln(acc, resid, g, b):
    y, xhat, rstd = _ln_fwd(ALPHA * resid + acc, g, b)
    return y, y, xhat, rstd


def _ep_resid(acc, resid):
    return (ALPHA * resid + acc,)


def _ep_relu2(acc):
    r = jnp.maximum(acc, 0.0)
    return acc, r * r


def _ep_relu2_bwd(acc, h):
    return (acc * (2.0 * jnp.maximum(h, 0.0)),)


def _ep_gate_bwd(acc, ga, gb, ya, yb, bga, bgb):
    sa = jax.nn.sigmoid(ga + bga)
    sb = jax.nn.sigmoid(gb + bgb)
    dga = acc * ya * (sa * (1.0 - sa))
    dgb = acc * yb * (sb * (1.0 - sb))
    return (acc * sa, acc * sb, dga, dgb, jnp.sum(dga, axis=0, keepdims=True), jnp.sum(dgb, axis=0, keepdims=True))


def _ln_bwd(dy, xhat, rstd, g, *, name):
    T, D = dy.shape
    tm = _tile(T, 256)

    def body(dy_ref, xh_ref, rs_ref, g_ref, dz_ref, dzb_ref, dg_ref, db_ref):
        i = pl.program_id(0)
        dyv, xh = dy_ref[...], xh_ref[...]
        dz = _ln_bwd_vals(dyv, xh, rs_ref[...], g_ref[...])
        dz_ref[...] = dz
        dzb_ref[...] = dz.astype(BF16)
        _acc_store(dg_ref, jnp.sum(dyv * xh, axis=0, keepdims=True), i == 0)
        _acc_store(db_ref, jnp.sum(dyv, axis=0, keepdims=True), i == 0)

    row = pl.BlockSpec((tm, D), lambda i: (i, 0))
    vec = pl.BlockSpec((1, D), lambda i: (0, 0))
    return pl.pallas_call(
        body,
        name=name,
        grid=(T // tm,),
        in_specs=[row, row, pl.BlockSpec((tm, 1), lambda i: (i, 0)), vec],
        out_specs=[row, row, vec, vec],
        out_shape=[
            jax.ShapeDtypeStruct((T, D), F32),
            jax.ShapeDtypeStruct((T, D), BF16),
            jax.ShapeDtypeStruct((1, D), F32),
            jax.ShapeDtypeStruct((1, D), F32),
        ],
        compiler_params=_params(("arbitrary",), [((tm, D), F32)] * 4),
    )(dy, xhat, rstd, g)


def _loss_head(y, target, *, name):
    T, D = y.shape
    tm = _tile(T, 256)

    def body(y_ref, t_ref, loss_ref, dy_ref):
        err = y_ref[...] - t_ref[...]
        dy_ref[...] = err * (1.0 / D)
        part = 0.5 * jnp.sum(jnp.sum(err * err, axis=1, keepdims=True) * (1.0 / D), axis=0, keepdims=True)
        _acc_store(loss_ref, part, pl.program_id(0) == 0)

    row = pl.BlockSpec((tm, D), lambda i: (i, 0))
    return pl.pallas_call(
        body,
        name=name,
        grid=(T // tm,),
        in_specs=[row, row],
        out_specs=[pl.BlockSpec((1, 1), lambda i: (0, 0)), row],
        out_shape=[jax.ShapeDtypeStruct((1, 1), F32), jax.ShapeDtypeStruct((T, D), F32)],
        compiler_params=_params(("arbitrary",), [((tm, D), F32)] * 3),
    )(y, target)


def _masked_mix_weights(wm_ref, G):
    r = lax.broadcasted_iota(jnp.int32, (CHUNK, CHUNK), 0)
    c = lax.broadcasted_iota(jnp.int32, (CHUNK, CHUNK), 1)
    return [jnp.where(c <= r, wm_ref[g], 0.0).astype(BF16) for g in range(G)]


def _gmlp_fwd(d, P, lnv_g, lnv_b, wm, bsT, *, name):
    T, GW, G = d.T, d.GW, d.G
    tm = _tile(T, 256)
    nc = tm // CHUNK

    def body(u_ref, v_ref, g_ref, b_ref, wm_ref, bs_ref, o_ref):
        gu = _gelu(u_ref[...])
        vn, _, _ = _ln_fwd(_gelu(v_ref[...]), g_ref[...], b_ref[...])
        vn = vn.astype(BF16)
        wl = _masked_mix_weights(wm_ref, G)
        for c in range(nc):
            rows = slice(c * CHUNK, (c + 1) * CHUNK)
            for g in range(G):
                cols = slice(g * GROUP_DIM, (g + 1) * GROUP_DIM)
                mixed = _dot(wl[g], vn[rows, cols], "nn") + bs_ref[:, g : g + 1]
                o_ref[rows, cols] = (gu[rows, cols] * mixed).astype(BF16)

    return pl.pallas_call(
        body,
        name=name,
        grid=(T // tm,),
        in_specs=[
            pl.BlockSpec((tm, GW), lambda i: (i, 0)),
            pl.BlockSpec((tm, GW), lambda i: (i, 1)),
            pl.BlockSpec((1, GW), lambda i: (0, 0)),
            pl.BlockSpec((1, GW), lambda i: (0, 0)),
            pl.BlockSpec((G, CHUNK, CHUNK), lambda i: (0, 0, 0)),
            pl.BlockSpec((CHUNK, G), lambda i: (0, 0)),
        ],
        out_specs=pl.BlockSpec((tm, GW), lambda i: (i, 0)),
        out_shape=jax.ShapeDtypeStruct((T, GW), BF16),
        compiler_params=_params(("arbitrary",), [((tm, GW), F32)] * 3),
    )(P, P, lnv_g, lnv_b, wm, bsT)


def _gmlp_bwd(d, P, dya_b, Ga, lnv_g, lnv_b, wm, bsT, *, name):
    T, D, GW, G = d.T, d.D, d.GW, d.G
    tm = _tile(T, 256)
    nc = tm // CHUNK

    def body(u_ref, v_ref, dya_ref, ga_ref, g_ref, b_ref, wm_ref, bs_ref, duv_ref, dwm_ref, dbs_ref, dlg_ref, dlb_ref, dgu_s, dvn_s):
        first = pl.program_id(0) == 0
        dsgu = _dot(dya_ref[...], ga_ref[...], "nn")
        u, v = u_ref[...], v_ref[...]
        gu = _gelu(u)
        lng = g_ref[...]
        vn, xh, rstd = _ln_fwd(_gelu(v), lng, b_ref[...])
        vn = vn.astype(BF16)
        wl = _masked_mix_weights(wm_ref, G)
        r = lax.broadcasted_iota(jnp.int32, (CHUNK, CHUNK), 0)
        cc = lax.broadcasted_iota(jnp.int32, (CHUNK, CHUNK), 1)
        lane_g = lax.broadcasted_iota(jnp.int32, (CHUNK, G), 1)
        dbs = jnp.zeros((CHUNK, G), F32)
        for g in range(G):
            cols = slice(g * GROUP_DIM, (g + 1) * GROUP_DIM)
            dw = jnp.zeros((CHUNK, CHUNK), F32)
            for c in range(nc):
                rows = slice(c * CHUNK, (c + 1) * CHUNK)
                mixed = _dot(wl[g], vn[rows, cols], "nn") + bs_ref[:, g : g + 1]
                ds = dsgu[rows, cols]
                dgu_s[rows, cols] = ds * mixed
                dmx = ds * gu[rows, cols]
                dbs = dbs + jnp.where(lane_g == g, jnp.sum(dmx, axis=1, keepdims=True), 0.0)
                dmx = dmx.astype(BF16)
                dw = dw + _dot(dmx, vn[rows, cols], "nt")
                dvn_s[rows, cols] = _dot(wl[g], dmx, "tn")
            _acc_store(dwm_ref.at[g], jnp.where(cc <= r, dw, 0.0), first)
        _acc_store(dbs_ref, dbs, first)
        dvn = dvn_s[...]
        _acc_store(dlg_ref, jnp.sum(dvn * xh, axis=0, keepdims=True), first)
        _acc_store(dlb_ref, jnp.sum(dvn, axis=0, keepdims=True), first)
        dgv = _ln_bwd_vals(dvn, xh, rstd, lng)
        duv_ref[:, :GW] = (dgu_s[...] * _gelu_grad(u)).astype(BF16)
        duv_ref[:, GW:] = (dgv * _gelu_grad(v)).astype(BF16)

    return pl.pallas_call(
        body,
        name=name,
        grid=(T // tm,),
        in_specs=[
            pl.BlockSpec((tm, GW), lambda i: (i, 0)),
            pl.BlockSpec((tm, GW), lambda i: (i, 1)),
            pl.BlockSpec((tm, D), lambda i: (i, 0)),
            pl.BlockSpec((D, GW), lambda i: (0, 0)),
            pl.BlockSpec((1, GW), lambda i: (0, 0)),
            pl.BlockSpec((1, GW), lambda i: (0, 0)),
            pl.BlockSpec((G, CHUNK, CHUNK), lambda i: (0, 0, 0)),
            pl.BlockSpec((CHUNK, G), lambda i: (0, 0)),
        ],
        out_specs=[
            pl.BlockSpec((tm, 2 * GW), lambda i: (i, 0)),
            pl.BlockSpec((G, CHUNK, CHUNK), lambda i: (0, 0, 0)),
            pl.BlockSpec((CHUNK, G), lambda i: (0, 0)),
            pl.BlockSpec((1, GW), lambda i: (0, 0)),
            pl.BlockSpec((1, GW), lambda i: (0, 0)),
        ],
        out_shape=[
            jax.ShapeDtypeStruct((T, 2 * GW), BF16),
            jax.ShapeDtypeStruct((G, CHUNK, CHUNK), F32),
            jax.ShapeDtypeStruct((CHUNK, G), F32),
            jax.ShapeDtypeStruct((1, GW), F32),
            jax.ShapeDtypeStruct((1, GW), F32),
        ],
        scratch_shapes=[pltpu.VMEM((tm, GW), F32), pltpu.VMEM((tm, GW), F32)],
        compiler_params=_params(
            ("arbitrary",), [((tm, GW), F32)] * 3 + [((tm, D), BF16), ((D, GW), BF16)], 2 * _nbytes((tm, GW), F32)
        ),
    )(P, P, dya_b, Ga, lnv_g, lnv_b, wm, bsT)


def _swap_halves(x):
    w = x.shape[1]
    lane = lax.broadcasted_iota(jnp.int32, x.shape, 1)
    return jnp.where((lane % HEAD_DIM) < ROPE_HALF, pltpu.roll(x, w - ROPE_HALF, 1), pltpu.roll(x, ROPE_HALF, 1))


def _lane_tile(t, width):
    reps = width // LANES
    return t if reps == 1 else jnp.tile(t, (1, reps))


def _rope(x, cos2, sin2):
    w = x.shape[1]
    return x * _lane_tile(cos2, w) + _swap_halves(x) * _lane_tile(sin2, w)


def _rope_bwd(g, cos2, sin2):
    w = g.shape[1]
    return g * _lane_tile(cos2, w) - _swap_halves(g) * _lane_tile(sin2, w)


def _swa_probs(qh, kch, kph, sink, mask_c, mask_p):
    scale = HEAD_DIM**-0.5
    s_c = jnp.where(mask_c, _dot(qh, kch, "nt") * scale, -jnp.inf)
    s_p = jnp.where(mask_p, _dot(qh, kph, "nt") * scale, -jnp.inf)
    m = jnp.maximum(jnp.maximum(jnp.max(s_c, axis=1, keepdims=True), jnp.max(s_p, axis=1, keepdims=True)), sink)
    e_c = jnp.exp(s_c - m)
    e_p = jnp.exp(s_p - m)
    e_s = jnp.exp(sink - m)
    den = jnp.sum(e_c, axis=1, keepdims=True) + jnp.sum(e_p, axis=1, keepdims=True) + e_s
    return e_c / den, e_p / den, e_s / den


def _swa_masks(i):
    r = lax.broadcasted_iota(jnp.int32, (CHUNK, CHUNK), 0)
    c = lax.broadcasted_iota(jnp.int32, (CHUNK, CHUNK), 1)
    return c <= r, jnp.logical_and(c > r, i > 0)


def _swa_specs(d):
    _, off_q, off_k, off_va, _, _ = _offsets(d)
    assert off_q % d.AW == 0 and off_k % d.KVW == 0 and off_va % d.KVW == 0
    prev = lambda i: jnp.maximum(i - 1, 0)
    return [
        pl.BlockSpec(memory_space=pltpu.SMEM),
        pl.BlockSpec((CHUNK, d.AW), lambda i, o=off_q // d.AW: (i, o)),
        pl.BlockSpec((CHUNK, d.KVW), lambda i, o=off_k // d.KVW: (i, o)),
        pl.BlockSpec((CHUNK, d.KVW), lambda i, o=off_k // d.KVW: (prev(i), o)),
        pl.BlockSpec((CHUNK, d.KVW), lambda i, o=off_va // d.KVW: (i, o)),
        pl.BlockSpec((CHUNK, d.KVW), lambda i, o=off_va // d.KVW: (prev(i), o)),
        pl.BlockSpec((CHUNK, LANES), lambda i: (i, 0)),
        pl.BlockSpec((CHUNK, LANES), lambda i: (i, 0)),
        pl.BlockSpec((CHUNK, LANES), lambda i: (prev(i), 0)),
        pl.BlockSpec((CHUNK, LANES), lambda i: (prev(i), 0)),
    ]


def _swa_fwd(d, P, cos2, sin2, sinks, *, name):
    T, AW, HQ, HKV = d.T, d.AW, d.HQ, d.HKV
    grp = HQ // HKV

    def body(sink_ref, q_ref, kc_ref, kp_ref, vc_ref, vp_ref, cc_ref, sc_ref, cp_ref, sp_ref, o_ref):
        i = pl.program_id(0)
        cc, sc, cp, sp = cc_ref[...], sc_ref[...], cp_ref[...], sp_ref[...]
        qr = _rope(q_ref[...], cc, sc).astype(BF16)
        kc = _rope(kc_ref[...], cc, sc).astype(BF16)
        kp = _rope(kp_ref[...], cp, sp).astype(BF16)
        vc = vc_ref[...].astype(BF16)
        vp = vp_ref[...].astype(BF16)
        mask_c, mask_p = _swa_masks(i)
        for h in range(HQ):
            hs = slice(h * HEAD_DIM, (h + 1) * HEAD_DIM)
            ks = slice((h // grp) * HEAD_DIM, (h // grp + 1) * HEAD_DIM)
            p_c, p_p, _ = _swa_probs(qr[:, hs], kc[:, ks], kp[:, ks], sink_ref[h], mask_c, mask_p)
            o = _dot(p_c.astype(BF16), vc[:, ks], "nn") + _dot(p_p.astype(BF16), vp[:, ks], "nn")
            o_ref[:, hs] = o.astype(BF16)

    return pl.pallas_call(
        body,
        name=name,
        grid=(T // CHUNK,),
        in_specs=_swa_specs(d),
        out_specs=pl.BlockSpec((CHUNK, AW), lambda i: (i, 0)),
        out_shape=jax.ShapeDtypeStruct((T, AW), BF16),
        compiler_params=_params(("arbitrary",), [((CHUNK, AW), F32)] * 3),
    )(sinks, P, P, P, P, P, cos2, sin2, cos2, sin2)


def _swa_bwd(d, P, dyb_b, Gb, cos2, sin2, sinks, *, name):
    T, D, AW, KVW, HQ, HKV = d.T, d.D, d.AW, d.KVW, d.HQ, d.HKV
    grp = HQ // HKV
    nb = T // CHUNK
    scale = HEAD_DIM**-0.5

    def body(sink_ref, q_ref, kc_ref, kp_ref, vc_ref, vp_ref, cc_ref, sc_ref, cp_ref, sp_ref, dyb_ref, gb_ref,
             dq_ref, dkv_ref, dsk_ref, acc_s, dq_s, dkc_s, dkp_s, dvc_s, dvp_s):
        i = pl.program_id(0)
        cc, sc, cp, sp = cc_ref[...], sc_ref[...], cp_ref[...], sp_ref[...]
        datt = _dot(dyb_ref[...], gb_ref[...], "nn").astype(BF16)
        qr = _rope(q_ref[...], cc, sc).astype(BF16)
        kc = _rope(kc_ref[...], cc, sc).astype(BF16)
        kp = _rope(kp_ref[...], cp, sp).astype(BF16)
        vc = vc_ref[...].astype(BF16)
        vp = vp_ref[...].astype(BF16)
        mask_c, mask_p = _swa_masks(i)
        lane = lax.broadcasted_iota(jnp.int32, (1, LANES), 1)
        dsk = jnp.zeros((1, LANES), F32)
        for kv in range(HKV):
            ks = slice(kv * HEAD_DIM, (kv + 1) * HEAD_DIM)
            kch, kph, vch, vph = kc[:, ks], kp[:, ks], vc[:, ks], vp[:, ks]
            dkc = jnp.zeros((CHUNK, HEAD_DIM), F32)
            dkp = jnp.zeros((CHUNK, HEAD_DIM), F32)
            dvc = jnp.zeros((CHUNK, HEAD_DIM), F32)
            dvp = jnp.zeros((CHUNK, HEAD_DIM), F32)
            for g in range(grp):
                h = kv * grp + g
                hs = slice(h * HEAD_DIM, (h + 1) * HEAD_DIM)
                qh, dah = qr[:, hs], datt[:, hs]
                p_c, p_p, p_s = _swa_probs(qh, kch, kph, sink_ref[h], mask_c, mask_p)
                dp_c = _dot(dah, vch, "nt")
                dp_p = _dot(dah, vph, "nt")
                rs = jnp.sum(p_c * dp_c, axis=1, keepdims=True) + jnp.sum(p_p * dp_p, axis=1, keepdims=True)
                ds_c = (p_c * (dp_c - rs) * scale).astype(BF16)
                ds_p = (p_p * (dp_p - rs) * scale).astype(BF16)
                dsk = dsk + jnp.where(lane == h, -jnp.sum(p_s * rs, axis=0, keepdims=True), 0.0)
                dq_s[:, hs] = _dot(ds_c, kch, "nn") + _dot(ds_p, kph, "nn")
                dkc = dkc + _dot(ds_c, qh, "tn")
                dkp = dkp + _dot(ds_p, qh, "tn")
                dvc = dvc + _dot(p_c.astype(BF16), dah, "tn")
                dvp = dvp + _dot(p_p.astype(BF16), dah, "tn")
            dkc_s[:, ks] = dkc
            dkp_s[:, ks] = dkp
            dvc_s[:, ks] = dvc
            dvp_s[:, ks] = dvp
        dq_ref[...] = _rope_bwd(dq_s[...], cc, sc).astype(BF16)
        cur = pl.ds(pl.multiple_of(i * CHUNK, CHUNK), CHUNK)
        acc_s[cur, :KVW] = _rope_bwd(dkc_s[...], cc, sc)
        acc_s[cur, KVW:] = dvc_s[...]

        @pl.when(i > 0)
        def _():
            prv = pl.ds(pl.multiple_of((i - 1) * CHUNK, CHUNK), CHUNK)
            acc_s[prv, :KVW] += _rope_bwd(dkp_s[...], cp, sp)
            acc_s[prv, KVW:] += dvp_s[...]

        _acc_store(dsk_ref, dsk, i == 0)

        @pl.when(i == nb - 1)
        def _():
            dkv_ref[...] = acc_s[...].astype(BF16)

    return pl.pallas_call(
        body,
        name=name,
        grid=(nb,),
        in_specs=_swa_specs(d) + [pl.BlockSpec((CHUNK, D), lambda i: (i, 0)), pl.BlockSpec((D, AW), lambda i: (0, 0))],
        out_specs=[
            pl.BlockSpec((CHUNK, AW), lambda i: (i, 0)),
            pl.BlockSpec((T, 2 * KVW), lambda i: (0, 0)),
            pl.BlockSpec((1, LANES), lambda i: (0, 0)),
        ],
        out_shape=[
            jax.ShapeDtypeStruct((T, AW), BF16),
            jax.ShapeDtypeStruct((T, 2 * KVW), BF16),
            jax.ShapeDtypeStruct((1, LANES), F32),
        ],
        scratch_shapes=[
            pltpu.VMEM((T, 2 * KVW), F32),
            pltpu.VMEM((CHUNK, AW), F32),
            pltpu.VMEM((CHUNK, KVW), F32),
            pltpu.VMEM((CHUNK, KVW), F32),
            pltpu.VMEM((CHUNK, KVW), F32),
            pltpu.VMEM((CHUNK, KVW), F32),
        ],
        compiler_params=_params(
            ("arbitrary",), [((CHUNK, AW), F32)] * 3 + [((D, AW), BF16), ((T, 2 * KVW), BF16)], _nbytes((T, 2 * KVW), F32)
        ),
    )(sinks, P, P, P, P, P, cos2, sin2, cos2, sin2, dyb_b, Gb)


def _branch_merge(d, sgu, att, Ga, Gb, P, b_gate, *, name):
    T, D, GW, AW = d.T, d.D, d.GW, d.AW
    _, _, _, _, off_ga, off_gb = _offsets(d)
    tm, tn = _tile(T, 1024), _tile(D, 512)
    assert off_ga % tn == 0 and off_gb % tn == 0

    def body(s_ref, a_ref, ga_w, gb_w, ga_ref, gb_ref, bga_ref, bgb_ref, ya_ref, yb_ref, mg_ref):
        ya = _dot(s_ref[...], ga_w[...], "nt")
        yb = _dot(a_ref[...], gb_w[...], "nt")
        ya_ref[...] = ya
        yb_ref[...] = yb
        sa = jax.nn.sigmoid(ga_ref[...] + bga_ref[...])
        sb = jax.nn.sigmoid(gb_ref[...] + bgb_ref[...])
        mg_ref[...] = (sa * ya + sb * yb).astype(BF16)

    tile = pl.BlockSpec((tm, tn), lambda j, i: (i, j))
    return pl.pallas_call(
        body,
        name=name,
        grid=(D // tn, T // tm),
        in_specs=[
            pl.BlockSpec((tm, GW), lambda j, i: (i, 0)),
            pl.BlockSpec((tm, AW), lambda j, i: (i, 0)),
            pl.BlockSpec((tn, GW), lambda j, i: (j, 0)),
            pl.BlockSpec((tn, AW), lambda j, i: (j, 0)),
            pl.BlockSpec((tm, tn), lambda j, i, o=off_ga // tn: (i, j + o)),
            pl.BlockSpec((tm, tn), lambda j, i, o=off_gb // tn: (i, j + o)),
            pl.BlockSpec((1, tn), lambda j, i: (0, j)),
            pl.BlockSpec((1, tn), lambda j, i, o=D // tn: (0, j + o)),
        ],
        out_specs=[tile, tile, tile],
        out_shape=[jax.ShapeDtypeStruct((T, D), F32), jax.ShapeDtypeStruct((T, D), F32), jax.ShapeDtypeStruct((T, D), BF16)],
        compiler_params=_params(
            ("arbitrary", "arbitrary"),
            [((tm, GW), BF16), ((tm, AW), BF16), ((tn, GW), BF16), ((tn, AW), BF16)] + [((tm, tn), F32)] * 5,
        ),
    )(sgu, att, Ga, Gb, P, P, b_gate, b_gate)


def _xattn_probs(qh, kh):
    s = _dot(qh, kh, "nt") * (X_HEAD_DIM**-0.5)
    e = jnp.exp(s - jnp.max(s, axis=1, keepdims=True))
    return e / jnp.sum(e, axis=1, keepdims=True)


def _xattn_fwd(d, X, Xb, kv, Gxq, Gxo, ln_g, ln_b, *, name):
    T, D, XW, XH, MEM = d.T, d.D, d.XW, d.XH, d.MEM
    tm = _tile(T, 256)

    def body(x_ref, xb_ref, kv_ref, wq_ref, wo_ref, g_ref, b_ref, y_ref, yb_ref, xh_ref, rs_ref, q_ref, o_ref, o_s):
        q = _dot(xb_ref[...], wq_ref[...], "nn").astype(BF16)
        q_ref[...] = q
        for h in range(XH):
            hs = slice(h * X_HEAD_DIM, (h + 1) * X_HEAD_DIM)
            p = _xattn_probs(q[:, hs], kv_ref[:, hs]).astype(BF16)
            o_s[:, hs] = _dot(p, kv_ref[:, XW + h * X_HEAD_DIM : XW + (h + 1) * X_HEAD_DIM], "nn").astype(BF16)
        o = o_s[...]
        o_ref[...] = o
        y, xhat, rstd = _ln_fwd(ALPHA * x_ref[...] + _dot(o, wo_ref[...], "nt"), g_ref[...], b_ref[...])
        y_ref[...] = y
        yb_ref[...] = y.astype(BF16)
        xh_ref[...] = xhat
        rs_ref[...] = rstd

    row = pl.BlockSpec((tm, D), lambda i: (i, 0))
    nar = pl.BlockSpec((tm, XW), lambda i: (i, 0))
    vec = pl.BlockSpec((1, D), lambda i: (0, 0))
    return pl.pallas_call(
        body,
        name=name,
        grid=(T // tm,),
        in_specs=[
            row,
            row,
            pl.BlockSpec((MEM, 2 * XW), lambda i: (0, 0)),
            pl.BlockSpec((D, XW), lambda i: (0, 0)),
            pl.BlockSpec((D, XW), lambda i: (0, 0)),
            vec,
            vec,
        ],
        out_specs=[row, row, row, pl.BlockSpec((tm, 1), lambda i: (i, 0)), nar, nar],
        out_shape=[
            jax.ShapeDtypeStruct((T, D), F32),
            jax.ShapeDtypeStruct((T, D), BF16),
            jax.ShapeDtypeStruct((T, D), F32),
            jax.ShapeDtypeStruct((T, 1), F32),
            jax.ShapeDtypeStruct((T, XW), BF16),
            jax.ShapeDtypeStruct((T, XW), BF16),
        ],
        scratch_shapes=[pltpu.VMEM((tm, XW), BF16)],
        compiler_params=_params(("arbitrary",), [((tm, D), F32)] * 4 + [((D, XW), BF16)] * 2),
    )(X, Xb, kv, Gxq, Gxo, ln_g, ln_b)


def _xattn_bwd(d, dz, dzb, q_b, kv, Gxq, Gxo, *, name):
    T, D, XW, XH, MEM = d.T, d.D, d.XW, d.XH, d.MEM
    tm = _tile(T, 256)
    scale = X_HEAD_DIM**-0.5

    def body(dz_ref, dzb_ref, q_ref, kv_ref, wq_ref, wo_ref, dx_ref, dq_ref, dkv_ref, dq_s):
        first = pl.program_id(0) == 0
        do = _dot(dzb_ref[...], wo_ref[...], "nn").astype(BF16)
        for h in range(XH):
            hs = slice(h * X_HEAD_DIM, (h + 1) * X_HEAD_DIM)
            vs = slice(XW + h * X_HEAD_DIM, XW + (h + 1) * X_HEAD_DIM)
            kh, vh, qh, doh = kv_ref[:, hs], kv_ref[:, vs], q_ref[:, hs], do[:, hs]
            p = _xattn_probs(qh, kh)
            dp = _dot(doh, vh, "nt")
            ds = (p * (dp - jnp.sum(p * dp, axis=1, keepdims=True)) * scale).astype(BF16)
            dq_s[:, hs] = _dot(ds, kh, "nn").astype(BF16)
            _acc_store(dkv_ref.at[h], _dot(ds, qh, "tn"), first)
            _acc_store(dkv_ref.at[XH + h], _dot(p.astype(BF16), doh, "tn"), first)
        dq = dq_s[...]
        dq_ref[...] = dq
        dx_ref[...] = ALPHA * dz_ref[...] + _dot(dq, wq_ref[...], "nt")

    row = pl.BlockSpec((tm, D), lambda i: (i, 0))
    nar = pl.BlockSpec((tm, XW), lambda i: (i, 0))
    return pl.pallas_call(
        body,
        name=name,
        grid=(T // tm,),
        in_specs=[
            row,
            row,
            nar,
            pl.BlockSpec((MEM, 2 * XW), lambda i: (0, 0)),
            pl.BlockSpec((D, XW), lambda i: (0, 0)),
            pl.BlockSpec((D, XW), lambda i: (0, 0)),
        ],
        out_specs=[row, nar, pl.BlockSpec((2 * XH, MEM, X_HEAD_DIM), lambda i: (0, 0, 0))],
        out_shape=[
            jax.ShapeDtypeStruct((T, D), F32),
            jax.ShapeDtypeStruct((T, XW), BF16),
            jax.ShapeDtypeStruct((2 * XH, MEM, X_HEAD_DIM), F32),
        ],
        scratch_shapes=[pltpu.VMEM((tm, XW), BF16)],
        compiler_params=_params(("arbitrary",), [((tm, D), F32)] * 3 + [((D, XW), BF16)] * 2),
    )(dz, dzb, q_b, kv, Gxq, Gxo)


def _resid_ln_outs():
    return (("mn", F32), ("mn", BF16), ("mn", F32), ("m1", F32))


def _layer_fwd(d, X, Xb, memb, G, S, cos2, sin2, tag):
    D = d.D
    P = _matmul(Xb, G["in"], "nt", name=f"proj_{tag}", tm=1024, tn=_tile(d.IN, 1280))
    sgu = _gmlp_fwd(d, P, S["ln_v_g"], S["ln_v_b"], S["w_s"], S["b_sT"], name=f"gmlp_fwd_{tag}")
    att = _swa_fwd(d, P, cos2, sin2, S["sinks"], name=f"swa_fwd_{tag}")
    ya, yb, merged = _branch_merge(d, sgu, att, G["a"], G["b"], P, S["b_gate"], name=f"merge_{tag}")
    X1, X1b, xh1, rs1 = _matmul(
        merged, G["o"], "nn", name=f"oproj_ln_{tag}", tn=D, tk=1024, out_defs=_resid_ln_outs(), epilogue=_ep_resid_ln,
        extras=((X, "mn", 0), (S["ln1_g"], "row", 0), (S["ln1_b"], "row", 0)),
    )
    kv = _matmul(memb, G["xkv"], "nn", name=f"xkv_{tag}", out_defs=(("mn", BF16),))
    X2, X2b, xh2, rs2, q_b, o_b = _xattn_fwd(d, X1, X1b, kv, G["xq"], G["xo"], S["ln2_g"], S["ln2_b"], name=f"xattn_fwd_{tag}")
    H, A = _matmul(X2b, G["up"], "nt", name=f"up_{tag}", tm=1024, out_defs=(("mn", F32), ("mn", BF16)), epilogue=_ep_relu2)
    X3, X3b, xh3, rs3 = _matmul(
        A, G["down"], "nn", name=f"down_ln_{tag}", tn=D, tk=1024, out_defs=_resid_ln_outs(), epilogue=_ep_resid_ln,
        extras=((X2, "mn", 0), (S["ln3_g"], "row", 0), (S["ln3_b"], "row", 0)),
    )
    saved = dict(Xb=Xb, P=P, sgu=sgu, att=att, ya=ya, yb=yb, merged=merged, X1b=X1b, xh1=xh1, rs1=rs1, kv=kv, q_b=q_b,
                 o_b=o_b, X2b=X2b, xh2=xh2, rs2=rs2, H=H, A=A, xh3=xh3, rs3=rs3)
    return X3, X3b, saved


def _layer_bwd(d, dXout, sv, memb, G, S, cos2, sin2, tag):
    D = d.D
    _, _, _, _, off_ga, off_gb = _offsets(d)
    bf = (("mn", BF16),)
    dz3, dz3b, dg3, db3 = _ln_bwd(dXout, sv["xh3"], sv["rs3"], S["ln3_g"], name=f"ln3_bwd_{tag}")
    dHb = _matmul(dz3b, G["down"], "nt", name=f"down_dx_{tag}", tm=1024, out_defs=bf, epilogue=_ep_relu2_bwd, extras=((sv["H"], "mn", 0),))
    g_down = _matmul(sv["A"], dz3b, "tn", name=f"down_dw_{tag}", tn=D)
    dX2o = _matmul(dHb, G["up"], "nn", name=f"up_dx_{tag}", tn=D, epilogue=_ep_resid, extras=((dz3, "mn", 0),))
    g_up = _matmul(dHb, sv["X2b"], "tn", name=f"up_dw_{tag}", tn=D)
    dz2, dz2b, dg2, db2 = _ln_bwd(dX2o, sv["xh2"], sv["rs2"], S["ln2_g"], name=f"ln2_bwd_{tag}")
    dX1o, dq_b, dkv = _xattn_bwd(d, dz2, dz2b, sv["q_b"], sv["kv"], G["xq"], G["xo"], name=f"xattn_bwd_{tag}")
    g_xo = _matmul(dz2b, sv["o_b"], "tn", name=f"xo_dw_{tag}")
    g_xq = _matmul(sv["X1b"], dq_b, "tn", name=f"xq_dw_{tag}")
    dkv_b = dkv.transpose(1, 0, 2).reshape(d.MEM, 2 * d.XW).astype(BF16)
    g_xkv = _matmul(memb, dkv_b, "tn", name=f"xkv_dw_{tag}")
    dz1, dz1b, dg1, db1 = _ln_bwd(dX1o, sv["xh1"], sv["rs1"], S["ln1_g"], name=f"ln1_bwd_{tag}")
    dya_b, dyb_b, dga_b, dgb_b, dbga, dbgb = _matmul(
        dz1b, G["o"], "nt", name=f"oproj_dx_{tag}", tm=1024, tn=512,
        out_defs=(("mn", BF16), ("mn", BF16), ("mn", BF16), ("mn", BF16), ("acc", F32), ("acc", F32)),
        epilogue=_ep_gate_bwd,
        extras=((sv["P"], "mn", off_ga), (sv["P"], "mn", off_gb), (sv["ya"], "mn", 0), (sv["yb"], "mn", 0),
                (S["b_gate"], "row", 0), (S["b_gate"], "row", D)),
    )
    g_o = _matmul(sv["merged"], dz1b, "tn", name=f"oproj_dw_{tag}", tn=D)
    duv_b, dwm, dbsT, dlvg, dlvb = _gmlp_bwd(d, sv["P"], dya_b, G["a"], S["ln_v_g"], S["ln_v_b"], S["w_s"], S["b_sT"], name=f"gmlp_bwd_{tag}")
    g_a = _matmul(dya_b, sv["sgu"], "tn", name=f"bra_dw_{tag}")
    dqr_b, dkvr_b, dsk = _swa_bwd(d, sv["P"], dyb_b, G["b"], cos2, sin2, S["sinks"], name=f"swa_bwd_{tag}")
    g_b = _matmul(dyb_b, sv["att"], "tn", name=f"brb_dw_{tag}")
    dP = jnp.concatenate([duv_b, dqr_b, dkvr_b, dga_b, dgb_b], axis=1)
    dXin = _matmul(dP, G["in"], "nn", name=f"proj_dx_{tag}", tn=D, tk=_tile(d.IN, 1920), epilogue=_ep_resid, extras=((dz1, "mn", 0),))
    g_in = _matmul(dP, sv["Xb"], "tn", name=f"proj_dw_{tag}", tn=D)
    big = {"in": g_in, "a": g_a, "b": g_b, "o": g_o, "xq": g_xq, "xkv": g_xkv, "xo": g_xo, "up": g_up, "down": g_down}
    small = dict(b_gate=jnp.concatenate([dbga, dbgb], axis=1), ln_v_g=dlvg, ln_v_b=dlvb, w_s=dwm, b_sT=dbsT, sinks=dsk,
                 ln1_g=dg1, ln1_b=db1, ln2_g=dg2, ln2_b=db2, ln3_g=dg3, ln3_b=db3)
    return dXin, big, small


def _rope_tables(T):
    inv = 1.0 / (ROPE_THETA ** (jnp.arange(0, HEAD_DIM, 2, dtype=F32) / HEAD_DIM))
    ang = jnp.arange(T, dtype=F32)[:, None] * inv[None, :]
    cos, sin = jnp.cos(ang), jnp.sin(ang)
    reps = LANES // HEAD_DIM
    return jnp.concatenate([cos, cos] * reps, axis=1), jnp.concatenate([-sin, sin] * reps, axis=1)


def _local_step(d, x, mem, target, Gs, Ss):
    cos2, sin2 = _rope_tables(d.T)
    memb = mem.astype(BF16)
    X, Xb = x, x.astype(BF16)
    saved = []
    for l in range(DEPTH):
        X, Xb, sv = _layer_fwd(d, X, Xb, memb, Gs[l], Ss[l], cos2, sin2, f"l{l}")
        saved.append(sv)
    loss, dX = _loss_head(X, target, name="loss_head")
    bigs, smalls = [None] * DEPTH, [None] * DEPTH
    for l in reversed(range(DEPTH)):
        dX, bigs[l], smalls[l] = _layer_bwd(d, dX, saved[l], memb, Gs[l], Ss[l], cos2, sin2, f"l{l}")
    return loss, dX, bigs, smalls


def _place():
    x, y, c = lax.axis_index("x"), lax.axis_index("y"), lax.axis_index("c")
    return x, y, c, [(1 - x, y), (x, 1 - y), (1 - x, 1 - y)]


def _all_gather(shards, *, name):
    n = len(shards)
    any_spec = pl.BlockSpec(memory_space=pl.ANY)

    def body(*refs):
        ins, outs = refs[:n], refs[n : 2 * n]
        send_sems, recv_sems, local_sems = refs[2 * n :]
        x, y, c, chips = _place()
        me, sibling = (x, y, c), (x, y, 1 - c)

        def rows(w, p):
            r = ins[w].shape[0]
            return outs[w].at[pl.ds((4 * p[0] + 2 * p[1] + p[2]) * r, r), :]

        def copy(w, k, block, to, src=None):
            return pltpu.make_async_remote_copy(
                src_ref=rows(w, block) if src is None else src, dst_ref=rows(w, block),
                send_sem=send_sems.at[7 * w + k], recv_sem=recv_sems.at[7 * w + k], device_id=to, device_id_type=MESH)

        mine = [pltpu.make_async_copy(ins[w], rows(w, me), local_sems.at[w]) for w in range(n)]
        for cp in mine:
            cp.start()
        first = []
        for w in range(n):
            first.append(copy(w, 0, me, sibling, src=ins[w]))
            first += [copy(w, 1 + j, me, (*chip, c), src=ins[w]) for j, chip in enumerate(chips)]
        for cp in first:
            cp.start()
        passed = []
        for j, chip in enumerate(chips):
            for w in range(n):
                copy(w, 1 + j, (*chip, c), me).wait_recv()
                fwd = copy(w, 4 + j, (*chip, c), sibling)
                fwd.start()
                passed.append(fwd)
        for w in range(n):
            copy(w, 0, sibling, me).wait_recv()
            for j, chip in enumerate(chips):
                copy(w, 4 + j, (*chip, 1 - c), me).wait_recv()
        for cp in first + passed:
            cp.wait_send()
        for cp in mine:
            cp.wait()

    return pl.pallas_call(
        body,
        name=name,
        in_specs=[any_spec] * n,
        out_specs=[any_spec] * n,
        out_shape=[jax.ShapeDtypeStruct((N_DEV * s.shape[0], s.shape[1]), s.dtype) for s in shards],
        scratch_shapes=[pltpu.SemaphoreType.DMA((7 * n,)), pltpu.SemaphoreType.DMA((7 * n,)), pltpu.SemaphoreType.DMA((n,))],
    )(*shards)


def _sibling_exchange(grads, *, name):
    n = len(grads)
    any_spec = pl.BlockSpec(memory_space=pl.ANY)

    def body(*refs):
        ins, outs = refs[:n], refs[n : 2 * n]
        send_sems, recv_sems = refs[2 * n :]
        x, y, c, _ = _place()
        copies = []
        for w in range(n):
            r = ins[w].shape[0] // N_DEV
            for k in range(4):
                kx, ky = k // 2, k % 2
                copies.append(pltpu.make_async_remote_copy(
                    src_ref=ins[w].at[pl.ds((4 * kx + 2 * ky + (1 - c)) * r, r), :], dst_ref=outs[w].at[pl.ds(k * r, r), :],
                    send_sem=send_sems.at[4 * w + k], recv_sem=recv_sems.at[4 * w + k], device_id=(x, y, 1 - c), device_id_type=MESH))
        for cp in copies:
            cp.start()
        for cp in copies:
            cp.wait()

    return pl.pallas_call(
        body,
        name=name,
        in_specs=[any_spec] * n,
        out_specs=[any_spec] * n,
        out_shape=[jax.ShapeDtypeStruct((g.shape[0] // 2, g.shape[1]), g.dtype) for g in grads],
        scratch_shapes=[pltpu.SemaphoreType.DMA((4 * n,)), pltpu.SemaphoreType.DMA((4 * n,))],
    )(*grads)


def _pair_sum(grad, got, place, *, name):
    R, C = grad.shape
    r = R // N_DEV
    tr = _tile(r, 256, 16)
    nt = r // tr

    def chip_of(s, pr):
        fx = jnp.where((s == 1) | (s == 3), 1, 0)
        fy = jnp.where(s >= 2, 1, 0)
        return pr[0] ^ fx, pr[1] ^ fy

    def grad_map(s, t, pr):
        kx, ky = chip_of(s, pr)
        return ((4 * kx + 2 * ky + pr[2]) * nt + t, 0)

    def got_map(s, t, pr):
        kx, ky = chip_of(s, pr)
        return ((2 * kx + ky) * nt + t, 0)

    def body(pr, g_ref, r_ref, own_ref, send_ref):
        s = pl.program_id(0)
        tot = g_ref[...] + r_ref[...]

        @pl.when(s == 0)
        def _():
            own_ref[...] = tot

        @pl.when(s > 0)
        def _():
            send_ref[...] = tot.astype(BF16)

    return pl.pallas_call(
        body,
        name=name,
        grid_spec=pltpu.PrefetchScalarGridSpec(
            num_scalar_prefetch=1,
            grid=(4, nt),
            in_specs=[pl.BlockSpec((tr, C), grad_map), pl.BlockSpec((tr, C), got_map)],
            out_specs=[
                pl.BlockSpec((tr, C), lambda s, t, pr: (jnp.where(s == 0, t, nt - 1), 0)),
                pl.BlockSpec((tr, C), lambda s, t, pr: (jnp.where(s == 0, 0, (s - 1) * nt + t), 0)),
            ],
        ),
        out_shape=[jax.ShapeDtypeStruct((r, C), F32), jax.ShapeDtypeStruct((3 * r, C), BF16)],
        compiler_params=_params(("arbitrary", "arbitrary"), [((tr, C), F32)] * 4),
    )(place, grad, got)


def _chip_exchange(sends, *, name):
    n = len(sends)
    any_spec = pl.BlockSpec(memory_space=pl.ANY)

    def body(*refs):
        ins, outs = refs[:n], refs[n : 2 * n]
        send_sems, recv_sems = refs[2 * n :]
        x, y, c, chips = _place()
        copies = []
        for w in range(n):
            r = ins[w].shape[0] // 3
            for j, chip in enumerate(chips):
                blk = pl.ds(j * r, r)
                copies.append(pltpu.make_async_remote_copy(
                    src_ref=ins[w].at[blk, :], dst_ref=outs[w].at[blk, :], send_sem=send_sems.at[3 * w + j],
                    recv_sem=recv_sems.at[3 * w + j], device_id=(*chip, c), device_id_type=MESH))
        for cp in copies:
            cp.start()
        for cp in copies:
            cp.wait()

    return pl.pallas_call(
        body,
        name=name,
        in_specs=[any_spec] * n,
        out_specs=[any_spec] * n,
        out_shape=[jax.ShapeDtypeStruct(s.shape, s.dtype) for s in sends],
        scratch_shapes=[pltpu.SemaphoreType.DMA((3 * n,)), pltpu.SemaphoreType.DMA((3 * n,))],
    )(*sends)


def _adam_vals(w, g, m, v):
    m = ADAM_B1 * m + (1.0 - ADAM_B1) * g
    v = ADAM_B2 * v + (1.0 - ADAM_B2) * (g * g)
    m_hat = m / (1.0 - ADAM_B1**ADAM_STEP)
    v_hat = v / (1.0 - ADAM_B2**ADAM_STEP)
    delta = -ADAM_LR * (m_hat / (jnp.sqrt(v_hat) + ADAM_EPS) + ADAM_WD * w)
    return delta, m, v


def _adam_big(w, m, v, own, got, *, name):
    r, C = w.shape
    tr = _tile(r, 256, 16)
    nt = r // tr

    def body(w_ref, m_ref, v_ref, own_ref, g1_ref, g2_ref, g3_ref, g_ref, d_ref, nm_ref, nv_ref):
        g = ((own_ref[...] + g1_ref[...].astype(F32)) + g2_ref[...].astype(F32)) + g3_ref[...].astype(F32)
        g_ref[...] = g
        d_ref[...], nm_ref[...], nv_ref[...] = _adam_vals(w_ref[...], g, m_ref[...], v_ref[...])

    row = pl.BlockSpec((tr, C), lambda t: (t, 0))
    return pl.pallas_call(
        body,
        name=name,
        grid=(nt,),
        in_specs=[row, row, row, row] + [pl.BlockSpec((tr, C), lambda t, j=j: (j * nt + t, 0)) for j in range(3)],
        out_specs=[row] * 4,
        out_shape=[jax.ShapeDtypeStruct((r, C), F32)] * 4,
        compiler_params=_params(("arbitrary",), [((tr, C), F32)] * 10),
    )(w, m, v, own, got, got, got)


def _adam_small(w, m, v, parts, *, name):
    R = w.shape[0]

    def body(w_ref, m_ref, v_ref, p_ref, g_ref, d_ref, nm_ref, nv_ref):
        g = p_ref[pl.ds(0, R), :]
        for k in range(1, N_DEV):
            g = g + p_ref[pl.ds(k * R, R), :]
        g_ref[...] = g
        d_ref[...], nm_ref[...], nv_ref[...] = _adam_vals(w_ref[...], g, m_ref[...], v_ref[...])

    return pl.pallas_call(
        body,
        name=name,
        out_shape=[jax.ShapeDtypeStruct((R, LANES), F32)] * 4,
        compiler_params=_params(None, [((R, LANES), F32)] * 16),
    )(w, m, v, parts)


SMALL_NAMES = ("b_gate", "ln_v_g", "ln_v_b", "w_s", "b_s", "sinks", "ln1_g", "ln1_b", "ln2_g", "ln2_b", "ln3_g", "ln3_b")
PACK_ALIGN = 8 * LANES


def _pack(arrays):
    flat = []
    for a in arrays:
        a = a.reshape(-1)
        flat.append(jnp.pad(a, (0, (-a.shape[0]) % PACK_ALIGN)))
    return jnp.concatenate(flat).reshape(-1, LANES)


def _unpack(packed, shapes):
    flat = packed.reshape(-1)
    out, pos = [], 0
    for s in shapes:
        n = 1
        for e in s:
            n *= e
        out.append(flat[pos : pos + n].reshape(s))
        pos += n + (-n) % PACK_ALIGN
    return out


BIG = (("in", "w_in", True), ("a", "w_br_a", True), ("b", "w_br_b", True), ("o", "w_o", False), ("xq", "w_xq", False),
       ("xkv", "w_xkv", False), ("xo", "w_xo", True), ("up", "w_up", True), ("down", "w_down", False))


def kernel(x, mem, w_in, b_gate, ln_v_g, ln_v_b, w_s, b_s, sinks, w_br_a, w_br_b, w_o, ln1_g, ln1_b, w_xq, w_xkv, w_xo, ln2_g, ln2_b, w_up, w_down, ln3_g, ln3_b, loss_target, m_w_in, m_b_gate, m_ln_v_g, m_ln_v_b, m_w_s, m_b_s, m_sinks, m_w_br_a, m_w_br_b, m_w_o, m_ln1_g, m_ln1_b, m_w_xq, m_w_xkv, m_w_xo, m_ln2_g, m_ln2_b, m_w_up, m_w_down, m_ln3_g, m_ln3_b, v_w_in, v_b_gate, v_ln_v_g, v_ln_v_b, v_w_s, v_b_s, v_sinks, v_w_br_a, v_w_br_b, v_w_o, v_ln1_g, v_ln1_b, v_w_xq, v_w_xkv, v_w_xo, v_ln2_g, v_ln2_b, v_w_up, v_w_down, v_ln3_g, v_ln3_b):
    given = dict(locals())
    T, D = x.shape[1], x.shape[2]
    IN, GW, AW, XW, DFF = w_in.shape[2] * N_DEV, ln_v_g.shape[1], w_br_b.shape[1], w_xq.shape[2], w_up.shape[2] * N_DEV
    KVW = (IN - 2 * GW - AW - 2 * D) // 2
    d = Dims(T=T, D=D, IN=IN, GW=GW, G=GW // GROUP_DIM, AW=AW, KVW=KVW, HQ=AW // HEAD_DIM, HKV=KVW // HEAD_DIM, XW=XW,
             XH=XW // X_HEAD_DIM, MEM=mem.shape[1], DFF=DFF)
    place = jnp.stack([lax.axis_index("x"), lax.axis_index("y"), lax.axis_index("c")]).astype(jnp.int32)

    Gs = []
    for l in range(DEPTH):
        shards = [(given[p][l].T if tr else given[p][l]).astype(BF16) for _, p, tr in BIG]
        full = _all_gather(shards, name=f"gather_weights_l{l}")
        Gs.append({g: f for (g, _, _), f in zip(BIG, full)})
    Ss = []
    for l in range(DEPTH):
        S = {n: given[n][l].reshape(1, -1) for n in SMALL_NAMES if n not in ("w_s", "b_s", "sinks")}
        S["w_s"], S["b_sT"], S["sinks"] = w_s[l], b_s[l].T, sinks[l]
        Ss.append(S)

    loss, dX, bigs, smalls = _local_step(d, x[0], mem[0], loss_target[0], Gs, Ss)
    loss = lax.psum(loss[0, 0], ("x", "y", "c"))

    res = {}
    for l in reversed(range(DEPTH)):
        grads = [bigs[l][g] for g, _, _ in BIG]
        gots = _sibling_exchange(grads, name=f"grad_sibling_l{l}")
        owns, sends = [], []
        for (g, _, _), gr, got in zip(BIG, grads, gots):
            own, send = _pair_sum(gr, got, place, name=f"grad_pair_sum_{g}_l{l}")
            owns.append(own)
            sends.append(send)
        recvs = _chip_exchange(sends, name=f"grad_chips_l{l}")
        for (g, p, tr), own, got in zip(BIG, owns, recvs):
            w, m, v = given[p][l], given["m_" + p][l], given["v_" + p][l]
            if tr:
                w, m, v = w.T, m.T, v.T
            outs = _adam_big(w, m, v, own, got, name=f"adamw_{g}_l{l}")
            res[(p, l)] = [o.T for o in outs] if tr else outs

    def small_list(src, l):
        return [src[l][n] if n != "b_s" else src[l]["b_sT"].T for n in SMALL_NAMES]

    shapes = [given[n].shape[1:] for n in SMALL_NAMES]
    part = _pack([a if n != "sinks" else a[0, : d.HQ] for l in range(DEPTH) for n, a in zip(SMALL_NAMES, small_list(smalls, l))])
    parts = _all_gather([part], name="gather_small_grads")[0]
    pw, pm, pv = (_pack([given[pre + n][l] for l in range(DEPTH) for n in SMALL_NAMES]) for pre in ("", "m_", "v_"))
    small_out = [_unpack(o, shapes * DEPTH) for o in _adam_small(pw, pm, pv, parts, name="adamw_small")]

    names = ["w_in", "b_gate", "ln_v_g", "ln_v_b", "w_s", "b_s", "sinks", "w_br_a", "w_br_b", "w_o", "ln1_g", "ln1_b", "w_xq",
             "w_xkv", "w_xo", "ln2_g", "ln2_b", "w_up", "w_down", "ln3_g", "ln3_b"]
    out = [loss, dX[None]]
    for kind in range(4):
        for n in names:
            if n in SMALL_NAMES:
                i = SMALL_NAMES.index(n)
                out.append(jnp.stack([small_out[kind][l * len(SMALL_NAMES) + i] for l in range(DEPTH)]))
            else:
                out.append(jnp.stack([res[(n, l)][kind] for l in range(DEPTH)]))
    return tuple(out)
```

```python
import collections
import functools

import jax
import jax.numpy as jnp
from jax import lax
from jax.experimental import pallas as pl
from jax.experimental.pallas import tpu as pltpu

F32 = jnp.float32
BF16 = jnp.bfloat16
MESH = pl.DeviceIdType.MESH

N_DEV = 8
DEPTH = 2
CHUNK = 128
HEAD_DIM = 64
ROPE_HALF = HEAD_DIM // 2
X_HEAD_DIM = 128
GROUP_DIM = 128
LANES = 128
ROPE_THETA = 10000.0
LN_EPS = 1e-5
ALPHA = (2 * DEPTH) ** 0.25
ADAM_LR = 0.001
ADAM_B1 = 0.9
ADAM_B2 = 0.999
ADAM_EPS = 1e-08
ADAM_WD = 0.01
ADAM_STEP = 10
VMEM_BYTES = 64 * 1024 * 1024
VMEM_TEMP_BYTES = 20 * 1024 * 1024

Dims = collections.namedtuple("Dims", "T D IN GW G AW KVW HQ HKV XW XH MEM DFF")


def _offsets(d):
    off_v = d.GW
    off_q = 2 * d.GW
    off_k = off_q + d.AW
    off_va = off_k + d.KVW
    off_ga = off_va + d.KVW
    off_gb = off_ga + d.D
    return off_v, off_q, off_k, off_va, off_ga, off_gb


def _tile(dim, pref, align=LANES):
    if dim <= pref:
        return dim
    t = pref - pref % align
    while t >= align:
        if dim % t == 0:
            return t
        t -= align
    return dim


def _nbytes(shape, dtype):
    n = 1
    for s in shape:
        n *= s
    return n * jnp.dtype(dtype).itemsize


def _params(sem, blocks, scratch=0):
    need = 2 * sum(_nbytes(s, t) for s, t in blocks) + scratch + VMEM_TEMP_BYTES
    limit = min(max(need, 32 * 1024 * 1024), VMEM_BYTES - 4 * 1024 * 1024)
    return pltpu.CompilerParams(dimension_semantics=sem, vmem_limit_bytes=limit)


def _dot(a, b, kind):
    dn = {"nn": (((1,), (0,)), ((), ())), "nt": (((1,), (1,)), ((), ())), "tn": (((0,), (0,)), ((), ()))}[kind]
    return lax.dot_general(a, b, dn, preferred_element_type=F32)


def _gelu(x):
    c = 0.7978845608028654
    t = jnp.tanh(c * (x + 0.044715 * (x * x * x)))
    return 0.5 * x * (1.0 + t)


def _gelu_grad(x):
    c = 0.7978845608028654
    x2 = x * x
    t = jnp.tanh(c * (x + 0.044715 * (x2 * x)))
    return 0.5 * (1.0 + t) + 0.5 * x * (1.0 - t * t) * (c * (1.0 + 3.0 * 0.044715 * x2))


def _ln_fwd(z, g, b):
    mu = jnp.mean(z, axis=-1, keepdims=True)
    zc = z - mu
    var = jnp.mean(zc * zc, axis=-1, keepdims=True)
    rstd = lax.rsqrt(var + LN_EPS)
    xhat = zc * rstd
    return xhat * g + b, xhat, rstd


def _ln_bwd_vals(dy, xhat, rstd, g):
    gd = dy * g
    m1 = jnp.mean(gd, axis=-1, keepdims=True)
    m2 = jnp.mean(gd * xhat, axis=-1, keepdims=True)
    return rstd * (gd - m1 - xhat * m2)


def _acc_store(ref, val, first):
    @pl.when(first)
    def _():
        ref[...] = val

    @pl.when(jnp.logical_not(first))
    def _():
        ref[...] += val


def _matmul(a, b, kind, *, name, tm=512, tn=1024, tk=2048, out_defs=(("mn", F32),), epilogue=None, extras=(), sched=None):
    if kind == "nn":
        (M, K), (K2, N) = a.shape, b.shape
    elif kind == "nt":
        (M, K), (N, K2) = a.shape, b.shape
    else:
        (K, M), (K2, N) = a.shape, b.shape
    assert K == K2, (a.shape, b.shape, kind)
    tm, tn, tk = _tile(M, tm), _tile(N, tn), _tile(K, tk)
    nk = K // tk
    blocks = []

    def spec(shape, imap, dtype):
        blocks.append((shape, dtype))
        return pl.BlockSpec(shape, imap)

    if kind == "tn":
        a_spec = spec((tk, tm), lambda j, i, k: (k, i), a.dtype)
    else:
        a_spec = spec((tm, tk), lambda j, i, k: (i, k), a.dtype)
    if kind == "nt":
        b_spec = spec((tn, tk), lambda j, i, k: (j, k), b.dtype)
    else:
        b_spec = spec((tk, tn), lambda j, i, k: (k, j), b.dtype)
    in_specs = [a_spec, b_spec]
    operands = [a, b]
    for arr, ekind, off in extras:
        if ekind == "mn":
            assert off % tn == 0
            in_specs.append(spec((tm, tn), lambda j, i, k, o=off // tn: (i, j + o), arr.dtype))
        elif ekind == "row":
            assert off % tn == 0
            in_specs.append(spec((1, tn), lambda j, i, k, o=off // tn: (0, j + o), arr.dtype))
        else:
            in_specs.append(spec((tm, 1), lambda j, i, k: (i, 0), arr.dtype))
        operands.append(arr)
    out_shape, out_specs = [], []
    for okind, dt in out_defs:
        if okind == "mn":
            out_shape.append(jax.ShapeDtypeStruct((M, N), dt))
            out_specs.append(spec((tm, tn), lambda j, i, k: (i, j), dt))
        elif okind == "m1":
            assert N == tn
            out_shape.append(jax.ShapeDtypeStruct((M, 1), dt))
            out_specs.append(spec((tm, 1), lambda j, i, k: (i, 0), dt))
        else:
            out_shape.append(jax.ShapeDtypeStruct((1, N), F32))
            out_specs.append(spec((1, tn), lambda j, i, k: (0, j), F32))
    n_ex, n_out = len(extras), len(out_defs)
    ep = epilogue if epilogue is not None else (lambda acc: (acc,))

    def body(*refs):
        a_ref, b_ref = refs[0], refs[1]
        ex_refs = refs[2 : 2 + n_ex]
        out_refs = refs[2 + n_ex : 2 + n_ex + n_out]
        i = pl.program_id(1)
        k = pl.program_id(2)
        part = _dot(a_ref[...], b_ref[...], kind)

        def finish(acc):
            vals = ep(acc, *[r[...] for r in ex_refs])
            for (okind, dt), r, v in zip(out_defs, out_refs, vals):
                if okind == "acc":
                    _acc_store(r, v, i == 0)
                else:
                    r[...] = v.astype(dt)

        if nk == 1:
            finish(part)
        else:
            acc_ref = refs[-1]

            @pl.when(k == 0)
            def _():
                acc_ref[...] = part

            @pl.when(k > 0)
            def _():
                acc_ref[...] += part

            @pl.when(k == nk - 1)
            def _():
                finish(acc_ref[...])

    scratch = [pltpu.VMEM((tm, tn), F32)] if nk > 1 else []
    outs = _hosted(
        body,
        name=name,
        sched=sched,
        grid=(N // tn, M // tm, nk),
        in_specs=in_specs,
        out_specs=out_specs,
        out_shape=out_shape,
        scratch_shapes=scratch,
        compiler_params=_params(("arbitrary", "arbitrary", "arbitrary"), blocks, _nbytes((tm, tn), F32) if nk > 1 else 0),
    )(*operands)
    return outs if len(outs) > 1 else outs[0]


def _ep_resid_ln(acc, resid, g, b):
    y, xhat, rstd = _ln_fwd(ALPHA * resid + acc, g, b)
    return y, y, xhat, rstd


def _ep_resid(acc, resid):
    return (ALPHA * resid + acc,)


def _ep_relu2(acc):
    r = jnp.maximum(acc, 0.0)
    return acc, r * r


def _ep_relu2_bwd(acc, h):
    return (acc * (2.0 * jnp.maximum(h, 0.0)),)


def _ep_gate_bwd(acc, ga, gb, ya, yb, bga, bgb):
    sa = jax.nn.sigmoid(ga + bga)
    sb = jax.nn.sigmoid(gb + bgb)
    dga = acc * ya * (sa * (1.0 - sa))
    dgb = acc * yb * (sb * (1.0 - sb))
    return (acc * sa, acc * sb, dga, dgb, jnp.sum(dga, axis=0, keepdims=True), jnp.sum(dgb, axis=0, keepdims=True))


def _ln_bwd(dy, xhat, rstd, g, *, name, sched=None):
    T, D = dy.shape
    tm = _tile(T, 256)

    def body(dy_ref, xh_ref, rs_ref, g_ref, dz_ref, dzb_ref, dg_ref, db_ref):
        i = pl.program_id(0)
        dyv, xh = dy_ref[...], xh_ref[...]
        dz = _ln_bwd_vals(dyv, xh, rs_ref[...], g_ref[...])
        dz_ref[...] = dz
        dzb_ref[...] = dz.astype(BF16)
        _acc_store(dg_ref, jnp.sum(dyv * xh, axis=0, keepdims=True), i == 0)
        _acc_store(db_ref, jnp.sum(dyv, axis=0, keepdims=True), i == 0)

    row = pl.BlockSpec((tm, D), lambda i: (i, 0))
    vec = pl.BlockSpec((1, D), lambda i: (0, 0))
    return _hosted(
        body,
        name=name,
        sched=sched,
        grid=(T // tm,),
        in_specs=[row, row, pl.BlockSpec((tm, 1), lambda i: (i, 0)), vec],
        out_specs=[row, row, vec, vec],
        out_shape=[
            jax.ShapeDtypeStruct((T, D), F32),
            jax.ShapeDtypeStruct((T, D), BF16),
            jax.ShapeDtypeStruct((1, D), F32),
            jax.ShapeDtypeStruct((1, D), F32),
        ],
        compiler_params=_params(("arbitrary",), [((tm, D), F32)] * 4),
    )(dy, xhat, rstd, g)


def _loss_head(y, target, *, name, sched=None):
    T, D = y.shape
    tm = _tile(T, 256)

    def body(y_ref, t_ref, loss_ref, dy_ref):
        err = y_ref[...] - t_ref[...]
        dy_ref[...] = err * (1.0 / D)
        part = 0.5 * jnp.sum(jnp.sum(err * err, axis=1, keepdims=True) * (1.0 / D), axis=0, keepdims=True)
        _acc_store(loss_ref, part, pl.program_id(0) == 0)

    row = pl.BlockSpec((tm, D), lambda i: (i, 0))
    return _hosted(
        body,
        name=name,
        sched=sched,
        grid=(T // tm,),
        in_specs=[row, row],
        out_specs=[pl.BlockSpec((1, 1), lambda i: (0, 0)), row],
        out_shape=[jax.ShapeDtypeStruct((1, 1), F32), jax.ShapeDtypeStruct((T, D), F32)],
        compiler_params=_params(("arbitrary",), [((tm, D), F32)] * 3),
    )(y, target)


def _masked_mix_weights(wm_ref, G):
    r = lax.broadcasted_iota(jnp.int32, (CHUNK, CHUNK), 0)
    c = lax.broadcasted_iota(jnp.int32, (CHUNK, CHUNK), 1)
    return [jnp.where(c <= r, wm_ref[g], 0.0).astype(BF16) for g in range(G)]


def _gmlp_fwd(d, P, lnv_g, lnv_b, wm, bsT, *, name, sched=None):
    T, GW, G = d.T, d.GW, d.G
    tm = _tile(T, 256)
    nc = tm // CHUNK

    def body(u_ref, v_ref, g_ref, b_ref, wm_ref, bs_ref, o_ref):
        gu = _gelu(u_ref[...])
        vn, _, _ = _ln_fwd(_gelu(v_ref[...]), g_ref[...], b_ref[...])
        vn = vn.astype(BF16)
        wl = _masked_mix_weights(wm_ref, G)
        for c in range(nc):
            rows = slice(c * CHUNK, (c + 1) * CHUNK)
            for g in range(G):
                cols = slice(g * GROUP_DIM, (g + 1) * GROUP_DIM)
                mixed = _dot(wl[g], vn[rows, cols], "nn") + bs_ref[:, g : g + 1]
                o_ref[rows, cols] = (gu[rows, cols] * mixed).astype(BF16)

    return _hosted(
        body,
        name=name,
        sched=sched,
        grid=(T // tm,),
        in_specs=[
            pl.BlockSpec((tm, GW), lambda i: (i, 0)),
            pl.BlockSpec((tm, GW), lambda i: (i, 1)),
            pl.BlockSpec((1, GW), lambda i: (0, 0)),
            pl.BlockSpec((1, GW), lambda i: (0, 0)),
            pl.BlockSpec((G, CHUNK, CHUNK), lambda i: (0, 0, 0)),
            pl.BlockSpec((CHUNK, G), lambda i: (0, 0)),
        ],
        out_specs=pl.BlockSpec((tm, GW), lambda i: (i, 0)),
        out_shape=jax.ShapeDtypeStruct((T, GW), BF16),
        compiler_params=_params(("arbitrary",), [((tm, GW), F32)] * 3),
    )(P, P, lnv_g, lnv_b, wm, bsT)


def _gmlp_bwd(d, P, dya_b, Ga, lnv_g, lnv_b, wm, bsT, *, name, sched=None):
    T, D, GW, G = d.T, d.D, d.GW, d.G
    tm = _tile(T, 256)
    nc = tm // CHUNK

    def body(u_ref, v_ref, dya_ref, ga_ref, g_ref, b_ref, wm_ref, bs_ref, duv_ref, dwm_ref, dbs_ref, dlg_ref, dlb_ref, dgu_s, dvn_s):
        first = pl.program_id(0) == 0
        dsgu = _dot(dya_ref[...], ga_ref[...], "nn")
        u, v = u_ref[...], v_ref[...]
        gu = _gelu(u)
        lng = g_ref[...]
        vn, xh, rstd = _ln_fwd(_gelu(v), lng, b_ref[...])
        vn = vn.astype(BF16)
        wl = _masked_mix_weights(wm_ref, G)
        r = lax.broadcasted_iota(jnp.int32, (CHUNK, CHUNK), 0)
        cc = lax.broadcasted_iota(jnp.int32, (CHUNK, CHUNK), 1)
        lane_g = lax.broadcasted_iota(jnp.int32, (CHUNK, G), 1)
        dbs = jnp.zeros((CHUNK, G), F32)
        for g in range(G):
            cols = slice(g * GROUP_DIM, (g + 1) * GROUP_DIM)
            dw = jnp.zeros((CHUNK, CHUNK), F32)
            for c in range(nc):
                rows = slice(c * CHUNK, (c + 1) * CHUNK)
                mixed = _dot(wl[g], vn[rows, cols], "nn") + bs_ref[:, g : g + 1]
                ds = dsgu[rows, cols]
                dgu_s[rows, cols] = ds * mixed
                dmx = ds * gu[rows, cols]
                dbs = dbs + jnp.where(lane_g == g, jnp.sum(dmx, axis=1, keepdims=True), 0.0)
                dmx = dmx.astype(BF16)
                dw = dw + _dot(dmx, vn[rows, cols], "nt")
                dvn_s[rows, cols] = _dot(wl[g], dmx, "tn")
            _acc_store(dwm_ref.at[g], jnp.where(cc <= r, dw, 0.0), first)
        _acc_store(dbs_ref, dbs, first)
        dvn = dvn_s[...]
        _acc_store(dlg_ref, jnp.sum(dvn * xh, axis=0, keepdims=True), first)
        _acc_store(dlb_ref, jnp.sum(dvn, axis=0, keepdims=True), first)
        dgv = _ln_bwd_vals(dvn, xh, rstd, lng)
        duv_ref[:, :GW] = (dgu_s[...] * _gelu_grad(u)).astype(BF16)
        duv_ref[:, GW:] = (dgv * _gelu_grad(v)).astype(BF16)

    return _hosted(
        body,
        name=name,
        sched=sched,
        grid=(T // tm,),
        in_specs=[
            pl.BlockSpec((tm, GW), lambda i: (i, 0)),
            pl.BlockSpec((tm, GW), lambda i: (i, 1)),
            pl.BlockSpec((tm, D), lambda i: (i, 0)),
            pl.BlockSpec((D, GW), lambda i: (0, 0)),
            pl.BlockSpec((1, GW), lambda i: (0, 0)),
            pl.BlockSpec((1, GW), lambda i: (0, 0)),
            pl.BlockSpec((G, CHUNK, CHUNK), lambda i: (0, 0, 0)),
            pl.BlockSpec((CHUNK, G), lambda i: (0, 0)),
        ],
        out_specs=[
            pl.BlockSpec((tm, 2 * GW), lambda i: (i, 0)),
            pl.BlockSpec((G, CHUNK, CHUNK), lambda i: (0, 0, 0)),
            pl.BlockSpec((CHUNK, G), lambda i: (0, 0)),
            pl.BlockSpec((1, GW), lambda i: (0, 0)),
            pl.BlockSpec((1, GW), lambda i: (0, 0)),
        ],
        out_shape=[
            jax.ShapeDtypeStruct((T, 2 * GW), BF16),
            jax.ShapeDtypeStruct((G, CHUNK, CHUNK), F32),
            jax.ShapeDtypeStruct((CHUNK, G), F32),
            jax.ShapeDtypeStruct((1, GW), F32),
            jax.ShapeDtypeStruct((1, GW), F32),
        ],
        scratch_shapes=[pltpu.VMEM((tm, GW), F32), pltpu.VMEM((tm, GW), F32)],
        compiler_params=_params(
            ("arbitrary",), [((tm, GW), F32)] * 3 + [((tm, D), BF16), ((D, GW), BF16)], 2 * _nbytes((tm, GW), F32)
        ),
    )(P, P, dya_b, Ga, lnv_g, lnv_b, wm, bsT)


def _swap_halves(x):
    w = x.shape[1]
    lane = lax.broadcasted_iota(jnp.int32, x.shape, 1)
    return jnp.where((lane % HEAD_DIM) < ROPE_HALF, pltpu.roll(x, w - ROPE_HALF, 1), pltpu.roll(x, ROPE_HALF, 1))


def _lane_tile(t, width):
    reps = width // LANES
    return t if reps == 1 else jnp.tile(t, (1, reps))


def _rope(x, cos2, sin2):
    w = x.shape[1]
    return x * _lane_tile(cos2, w) + _swap_halves(x) * _lane_tile(sin2, w)


def _rope_bwd(g, cos2, sin2):
    w = g.shape[1]
    return g * _lane_tile(cos2, w) - _swap_halves(g) * _lane_tile(sin2, w)


def _swa_probs(qh, kch, kph, sink, mask_c, mask_p):
    scale = HEAD_DIM**-0.5
    s_c = jnp.where(mask_c, _dot(qh, kch, "nt") * scale, -jnp.inf)
    s_p = jnp.where(mask_p, _dot(qh, kph, "nt") * scale, -jnp.inf)
    m = jnp.maximum(jnp.maximum(jnp.max(s_c, axis=1, keepdims=True), jnp.max(s_p, axis=1, keepdims=True)), sink)
    e_c = jnp.exp(s_c - m)
    e_p = jnp.exp(s_p - m)
    e_s = jnp.exp(sink - m)
    den = jnp.sum(e_c, axis=1, keepdims=True) + jnp.sum(e_p, axis=1, keepdims=True) + e_s
    return e_c / den, e_p / den, e_s / den


def _swa_masks(i):
    r = lax.broadcasted_iota(jnp.int32, (CHUNK, CHUNK), 0)
    c = lax.broadcasted_iota(jnp.int32, (CHUNK, CHUNK), 1)
    return c <= r, jnp.logical_and(c > r, i > 0)


def _swa_specs(d):
    _, off_q, off_k, off_va, _, _ = _offsets(d)
    assert off_q % d.AW == 0 and off_k % d.KVW == 0 and off_va % d.KVW == 0
    prev = lambda i: jnp.maximum(i - 1, 0)
    return [
        pl.BlockSpec(memory_space=pltpu.SMEM),
        pl.BlockSpec((CHUNK, d.AW), lambda i, o=off_q // d.AW: (i, o)),
        pl.BlockSpec((CHUNK, d.KVW), lambda i, o=off_k // d.KVW: (i, o)),
        pl.BlockSpec((CHUNK, d.KVW), lambda i, o=off_k // d.KVW: (prev(i), o)),
        pl.BlockSpec((CHUNK, d.KVW), lambda i, o=off_va // d.KVW: (i, o)),
        pl.BlockSpec((CHUNK, d.KVW), lambda i, o=off_va // d.KVW: (prev(i), o)),
        pl.BlockSpec((CHUNK, LANES), lambda i: (i, 0)),
        pl.BlockSpec((CHUNK, LANES), lambda i: (i, 0)),
        pl.BlockSpec((CHUNK, LANES), lambda i: (prev(i), 0)),
        pl.BlockSpec((CHUNK, LANES), lambda i: (prev(i), 0)),
    ]


def _swa_fwd(d, P, cos2, sin2, sinks, *, name, sched=None):
    T, AW, HQ, HKV = d.T, d.AW, d.HQ, d.HKV
    grp = HQ // HKV

    def body(sink_ref, q_ref, kc_ref, kp_ref, vc_ref, vp_ref, cc_ref, sc_ref, cp_ref, sp_ref, o_ref):
        i = pl.program_id(0)
        cc, sc, cp, sp = cc_ref[...], sc_ref[...], cp_ref[...], sp_ref[...]
        qr = _rope(q_ref[...], cc, sc).astype(BF16)
        kc = _rope(kc_ref[...], cc, sc).astype(BF16)
        kp = _rope(kp_ref[...], cp, sp).astype(BF16)
        vc = vc_ref[...].astype(BF16)
        vp = vp_ref[...].astype(BF16)
        mask_c, mask_p = _swa_masks(i)
        for h in range(HQ):
            hs = slice(h * HEAD_DIM, (h + 1) * HEAD_DIM)
            ks = slice((h // grp) * HEAD_DIM, (h // grp + 1) * HEAD_DIM)
            p_c, p_p, _ = _swa_probs(qr[:, hs], kc[:, ks], kp[:, ks], sink_ref[h], mask_c, mask_p)
            o = _dot(p_c.astype(BF16), vc[:, ks], "nn") + _dot(p_p.astype(BF16), vp[:, ks], "nn")
            o_ref[:, hs] = o.astype(BF16)

    return _hosted(
        body,
        name=name,
        sched=sched,
        grid=(T // CHUNK,),
        in_specs=_swa_specs(d),
        out_specs=pl.BlockSpec((CHUNK, AW), lambda i: (i, 0)),
        out_shape=jax.ShapeDtypeStruct((T, AW), BF16),
        compiler_params=_params(("arbitrary",), [((CHUNK, AW), F32)] * 3),
    )(sinks, P, P, P, P, P, cos2, sin2, cos2, sin2)


def _swa_bwd(d, P, dyb_b, Gb, cos2, sin2, sinks, *, name, sched=None):
    T, D, AW, KVW, HQ, HKV = d.T, d.D, d.AW, d.KVW, d.HQ, d.HKV
    grp = HQ // HKV
    nb = T // CHUNK
    scale = HEAD_DIM**-0.5

    def body(sink_ref, q_ref, kc_ref, kp_ref, vc_ref, vp_ref, cc_ref, sc_ref, cp_ref, sp_ref, dyb_ref, gb_ref,
             dq_ref, dkv_ref, dsk_ref, acc_s, dq_s, dkc_s, dkp_s, dvc_s, dvp_s):
        i = pl.program_id(0)
        cc, sc, cp, sp = cc_ref[...], sc_ref[...], cp_ref[...], sp_ref[...]
        datt = _dot(dyb_ref[...], gb_ref[...], "nn").astype(BF16)
        qr = _rope(q_ref[...], cc, sc).astype(BF16)
        kc = _rope(kc_ref[...], cc, sc).astype(BF16)
        kp = _rope(kp_ref[...], cp, sp).astype(BF16)
        vc = vc_ref[...].astype(BF16)
        vp = vp_ref[...].astype(BF16)
        mask_c, mask_p = _swa_masks(i)
        lane = lax.broadcasted_iota(jnp.int32, (1, LANES), 1)
        dsk = jnp.zeros((1, LANES), F32)
        for kv in range(HKV):
            ks = slice(kv * HEAD_DIM, (kv + 1) * HEAD_DIM)
            kch, kph, vch, vph = kc[:, ks], kp[:, ks], vc[:, ks], vp[:, ks]
            dkc = jnp.zeros((CHUNK, HEAD_DIM), F32)
            dkp = jnp.zeros((CHUNK, HEAD_DIM), F32)
            dvc = jnp.zeros((CHUNK, HEAD_DIM), F32)
            dvp = jnp.zeros((CHUNK, HEAD_DIM), F32)
            for g in range(grp):
                h = kv * grp + g
                hs = slice(h * HEAD_DIM, (h + 1) * HEAD_DIM)
                qh, dah = qr[:, hs], datt[:, hs]
                p_c, p_p, p_s = _swa_probs(qh, kch, kph, sink_ref[h], mask_c, mask_p)
                dp_c = _dot(dah, vch, "nt")
                dp_p = _dot(dah, vph, "nt")
                rs = jnp.sum(p_c * dp_c, axis=1, keepdims=True) + jnp.sum(p_p * dp_p, axis=1, keepdims=True)
                ds_c = (p_c * (dp_c - rs) * scale).astype(BF16)
                ds_p = (p_p * (dp_p - rs) * scale).astype(BF16)
                dsk = dsk + jnp.where(lane == h, -jnp.sum(p_s * rs, axis=0, keepdims=True), 0.0)
                dq_s[:, hs] = _dot(ds_c, kch, "nn") + _dot(ds_p, kph, "nn")
                dkc = dkc + _dot(ds_c, qh, "tn")
                dkp = dkp + _dot(ds_p, qh, "tn")
                dvc = dvc + _dot(p_c.astype(BF16), dah, "tn")
                dvp = dvp + _dot(p_p.astype(BF16), dah, "tn")
            dkc_s[:, ks] = dkc
            dkp_s[:, ks] = dkp
            dvc_s[:, ks] = dvc
            dvp_s[:, ks] = dvp
        dq_ref[...] = _rope_bwd(dq_s[...], cc, sc).astype(BF16)
        cur = pl.ds(pl.multiple_of(i * CHUNK, CHUNK), CHUNK)
        acc_s[cur, :KVW] = _rope_bwd(dkc_s[...], cc, sc)
        acc_s[cur, KVW:] = dvc_s[...]

        @pl.when(i > 0)
        def _():
            prv = pl.ds(pl.multiple_of((i - 1) * CHUNK, CHUNK), CHUNK)
            acc_s[prv, :KVW] += _rope_bwd(dkp_s[...], cp, sp)
            acc_s[prv, KVW:] += dvp_s[...]

        _acc_store(dsk_ref, dsk, i == 0)

        @pl.when(i == nb - 1)
        def _():
            dkv_ref[...] = acc_s[...].astype(BF16)

    return _hosted(
        body,
        name=name,
        sched=sched,
        grid=(nb,),
        in_specs=_swa_specs(d) + [pl.BlockSpec((CHUNK, D), lambda i: (i, 0)), pl.BlockSpec((D, AW), lambda i: (0, 0))],
        out_specs=[
            pl.BlockSpec((CHUNK, AW), lambda i: (i, 0)),
            pl.BlockSpec((T, 2 * KVW), lambda i: (0, 0)),
            pl.BlockSpec((1, LANES), lambda i: (0, 0)),
        ],
        out_shape=[
            jax.ShapeDtypeStruct((T, AW), BF16),
            jax.ShapeDtypeStruct((T, 2 * KVW), BF16),
            jax.ShapeDtypeStruct((1, LANES), F32),
        ],
        scratch_shapes=[
            pltpu.VMEM((T, 2 * KVW), F32),
            pltpu.VMEM((CHUNK, AW), F32),
            pltpu.VMEM((CHUNK, KVW), F32),
            pltpu.VMEM((CHUNK, KVW), F32),
            pltpu.VMEM((CHUNK, KVW), F32),
            pltpu.VMEM((CHUNK, KVW), F32),
        ],
        compiler_params=_params(
            ("arbitrary",), [((CHUNK, AW), F32)] * 3 + [((D, AW), BF16), ((T, 2 * KVW), BF16)], _nbytes((T, 2 * KVW), F32)
        ),
    )(sinks, P, P, P, P, P, cos2, sin2, cos2, sin2, dyb_b, Gb)


def _branch_merge(d, sgu, att, Ga, Gb, P, b_gate, *, name, sched=None):
    T, D, GW, AW = d.T, d.D, d.GW, d.AW
    _, _, _, _, off_ga, off_gb = _offsets(d)
    tm, tn = _tile(T, 1024), _tile(D, 512)
    assert off_ga % tn == 0 and off_gb % tn == 0

    def body(s_ref, a_ref, ga_w, gb_w, ga_ref, gb_ref, bga_ref, bgb_ref, ya_ref, yb_ref, mg_ref):
        ya = _dot(s_ref[...], ga_w[...], "nt")
        yb = _dot(a_ref[...], gb_w[...], "nt")
        ya_ref[...] = ya
        yb_ref[...] = yb
        sa = jax.nn.sigmoid(ga_ref[...] + bga_ref[...])
        sb = jax.nn.sigmoid(gb_ref[...] + bgb_ref[...])
        mg_ref[...] = (sa * ya + sb * yb).astype(BF16)

    tile = pl.BlockSpec((tm, tn), lambda j, i: (i, j))
    return _hosted(
        body,
        name=name,
        sched=sched,
        grid=(D // tn, T // tm),
        in_specs=[
            pl.BlockSpec((tm, GW), lambda j, i: (i, 0)),
            pl.BlockSpec((tm, AW), lambda j, i: (i, 0)),
            pl.BlockSpec((tn, GW), lambda j, i: (j, 0)),
            pl.BlockSpec((tn, AW), lambda j, i: (j, 0)),
            pl.BlockSpec((tm, tn), lambda j, i, o=off_ga // tn: (i, j + o)),
            pl.BlockSpec((tm, tn), lambda j, i, o=off_gb // tn: (i, j + o)),
            pl.BlockSpec((1, tn), lambda j, i: (0, j)),
            pl.BlockSpec((1, tn), lambda j, i, o=D // tn: (0, j + o)),
        ],
        out_specs=[tile, tile, tile],
        out_shape=[jax.ShapeDtypeStruct((T, D), F32), jax.ShapeDtypeStruct((T, D), F32), jax.ShapeDtypeStruct((T, D), BF16)],
        compiler_params=_params(
            ("arbitrary", "arbitrary"),
            [((tm, GW), BF16), ((tm, AW), BF16), ((tn, GW), BF16), ((tn, AW), BF16)] + [((tm, tn), F32)] * 5,
        ),
    )(sgu, att, Ga, Gb, P, P, b_gate, b_gate)


def _xattn_probs(qh, kh):
    s = _dot(qh, kh, "nt") * (X_HEAD_DIM**-0.5)
    e = jnp.exp(s - jnp.max(s, axis=1, keepdims=True))
    return e / jnp.sum(e, axis=1, keepdims=True)


def _xattn_fwd(d, X, Xb, kv, Gxq, Gxo, ln_g, ln_b, *, name, sched=None):
    T, D, XW, XH, MEM = d.T, d.D, d.XW, d.XH, d.MEM
    tm = _tile(T, 256)

    def body(x_ref, xb_ref, kv_ref, wq_ref, wo_ref, g_ref, b_ref, y_ref, yb_ref, xh_ref, rs_ref, q_ref, o_ref, o_s):
        q = _dot(xb_ref[...], wq_ref[...], "nn").astype(BF16)
        q_ref[...] = q
        for h in range(XH):
            hs = slice(h * X_HEAD_DIM, (h + 1) * X_HEAD_DIM)
            p = _xattn_probs(q[:, hs], kv_ref[:, hs]).astype(BF16)
            o_s[:, hs] = _dot(p, kv_ref[:, XW + h * X_HEAD_DIM : XW + (h + 1) * X_HEAD_DIM], "nn").astype(BF16)
        o = o_s[...]
        o_ref[...] = o
        y, xhat, rstd = _ln_fwd(ALPHA * x_ref[...] + _dot(o, wo_ref[...], "nt"), g_ref[...], b_ref[...])
        y_ref[...] = y
        yb_ref[...] = y.astype(BF16)
        xh_ref[...] = xhat
        rs_ref[...] = rstd

    row = pl.BlockSpec((tm, D), lambda i: (i, 0))
    nar = pl.BlockSpec((tm, XW), lambda i: (i, 0))
    vec = pl.BlockSpec((1, D), lambda i: (0, 0))
    return _hosted(
        body,
        name=name,
        sched=sched,
        grid=(T // tm,),
        in_specs=[
            row,
            row,
            pl.BlockSpec((MEM, 2 * XW), lambda i: (0, 0)),
            pl.BlockSpec((D, XW), lambda i: (0, 0)),
            pl.BlockSpec((D, XW), lambda i: (0, 0)),
            vec,
            vec,
        ],
        out_specs=[row, row, row, pl.BlockSpec((tm, 1), lambda i: (i, 0)), nar, nar],
        out_shape=[
            jax.ShapeDtypeStruct((T, D), F32),
            jax.ShapeDtypeStruct((T, D), BF16),
            jax.ShapeDtypeStruct((T, D), F32),
            jax.ShapeDtypeStruct((T, 1), F32),
            jax.ShapeDtypeStruct((T, XW), BF16),
            jax.ShapeDtypeStruct((T, XW), BF16),
        ],
        scratch_shapes=[pltpu.VMEM((tm, XW), BF16)],
        compiler_params=_params(("arbitrary",), [((tm, D), F32)] * 4 + [((D, XW), BF16)] * 2),
    )(X, Xb, kv, Gxq, Gxo, ln_g, ln_b)


def _xattn_bwd(d, dz, dzb, q_b, kv, Gxq, Gxo, *, name, sched=None):
    T, D, XW, XH, MEM = d.T, d.D, d.XW, d.XH, d.MEM
    tm = _tile(T, 256)
    scale = X_HEAD_DIM**-0.5

    def body(dz_ref, dzb_ref, q_ref, kv_ref, wq_ref, wo_ref, dx_ref, dq_ref, dkv_ref, dq_s):
        first = pl.program_id(0) == 0
        do = _dot(dzb_ref[...], wo_ref[...], "nn").astype(BF16)
        for h in range(XH):
            hs = slice(h * X_HEAD_DIM, (h + 1) * X_HEAD_DIM)
            vs = slice(XW + h * X_HEAD_DIM, XW + (h + 1) * X_HEAD_DIM)
            kh, vh, qh, doh = kv_ref[:, hs], kv_ref[:, vs], q_ref[:, hs], do[:, hs]
            p = _xattn_probs(qh, kh)
            dp = _dot(doh, vh, "nt")
            ds = (p * (dp - jnp.sum(p * dp, axis=1, keepdims=True)) * scale).astype(BF16)
            dq_s[:, hs] = _dot(ds, kh, "nn").astype(BF16)
            _acc_store(dkv_ref.at[h], _dot(ds, qh, "tn"), first)
            _acc_store(dkv_ref.at[XH + h], _dot(p.astype(BF16), doh, "tn"), first)
        dq = dq_s[...]
        dq_ref[...] = dq
        dx_ref[...] = ALPHA * dz_ref[...] + _dot(dq, wq_ref[...], "nt")

    row = pl.BlockSpec((tm, D), lambda i: (i, 0))
    nar = pl.BlockSpec((tm, XW), lambda i: (i, 0))
    return _hosted(
        body,
        name=name,
        sched=sched,
        grid=(T // tm,),
        in_specs=[
            row,
            row,
            nar,
            pl.BlockSpec((MEM, 2 * XW), lambda i: (0, 0)),
            pl.BlockSpec((D, XW), lambda i: (0, 0)),
            pl.BlockSpec((D, XW), lambda i: (0, 0)),
        ],
        out_specs=[row, nar, pl.BlockSpec((2 * XH, MEM, X_HEAD_DIM), lambda i: (0, 0, 0))],
        out_shape=[
            jax.ShapeDtypeStruct((T, D), F32),
            jax.ShapeDtypeStruct((T, XW), BF16),
            jax.ShapeDtypeStruct((2 * XH, MEM, X_HEAD_DIM), F32),
        ],
        scratch_shapes=[pltpu.VMEM((tm, XW), BF16)],
        compiler_params=_params(("arbitrary",), [((tm, D), F32)] * 3 + [((D, XW), BF16)] * 2),
    )(dz, dzb, q_b, kv, Gxq, Gxo)


def _resid_ln_outs():
    return (("mn", F32), ("mn", BF16), ("mn", F32), ("m1", F32))


def _layer_fwd(d, X, Xb, memb, G, S, cos2, sin2, tag, sched=None):
    D = d.D
    mm = functools.partial(_matmul, sched=sched)
    P = mm(Xb, G["in"], "nt", name=f"proj_{tag}", tm=1024, tn=_tile(d.IN, 1280))
    sgu = _gmlp_fwd(d, P, S["ln_v_g"], S["ln_v_b"], S["w_s"], S["b_sT"], name=f"gmlp_fwd_{tag}", sched=sched)
    att = _swa_fwd(d, P, cos2, sin2, S["sinks"], name=f"swa_fwd_{tag}", sched=sched)
    ya, yb, merged = _branch_merge(d, sgu, att, G["a"], G["b"], P, S["b_gate"], name=f"merge_{tag}", sched=sched)
    X1, X1b, xh1, rs1 = mm(
        merged, G["o"], "nn", name=f"oproj_ln_{tag}", tn=D, tk=1024, out_defs=_resid_ln_outs(), epilogue=_ep_resid_ln,
        extras=((X, "mn", 0), (S["ln1_g"], "row", 0), (S["ln1_b"], "row", 0)),
    )
    kv = mm(memb, G["xkv"], "nn", name=f"xkv_{tag}", out_defs=(("mn", BF16),))
    X2, X2b, xh2, rs2, q_b, o_b = _xattn_fwd(d, X1, X1b, kv, G["xq"], G["xo"], S["ln2_g"], S["ln2_b"], name=f"xattn_fwd_{tag}", sched=sched)
    H, A = mm(X2b, G["up"], "nt", name=f"up_{tag}", tm=1024, out_defs=(("mn", F32), ("mn", BF16)), epilogue=_ep_relu2)
    X3, X3b, xh3, rs3 = mm(
        A, G["down"], "nn", name=f"down_ln_{tag}", tn=D, tk=1024, out_defs=_resid_ln_outs(), epilogue=_ep_resid_ln,
        extras=((X2, "mn", 0), (S["ln3_g"], "row", 0), (S["ln3_b"], "row", 0)),
    )
    saved = dict(Xb=Xb, P=P, sgu=sgu, att=att, ya=ya, yb=yb, merged=merged, X1b=X1b, xh1=xh1, rs1=rs1, kv=kv, q_b=q_b,
                 o_b=o_b, X2b=X2b, xh2=xh2, rs2=rs2, H=H, A=A, xh3=xh3, rs3=rs3)
    return X3, X3b, saved


def _layer_bwd(d, dXout, sv, memb, G, S, cos2, sin2, tag, sched=None, on_grad=None):
    D = d.D
    mm = functools.partial(_matmul, sched=sched)
    emit = on_grad if on_grad is not None else (lambda n, g: None)
    _, _, _, _, off_ga, off_gb = _offsets(d)
    bf = (("mn", BF16),)
    dz3, dz3b, dg3, db3 = _ln_bwd(dXout, sv["xh3"], sv["rs3"], S["ln3_g"], name=f"ln3_bwd_{tag}", sched=sched)
    dHb = mm(dz3b, G["down"], "nt", name=f"down_dx_{tag}", tm=1024, out_defs=bf, epilogue=_ep_relu2_bwd, extras=((sv["H"], "mn", 0),))
    g_down = mm(sv["A"], dz3b, "tn", name=f"down_dw_{tag}", tn=D)
    emit("down", g_down)
    dX2o = mm(dHb, G["up"], "nn", name=f"up_dx_{tag}", tn=D, epilogue=_ep_resid, extras=((dz3, "mn", 0),))
    g_up = mm(dHb, sv["X2b"], "tn", name=f"up_dw_{tag}", tn=D)
    emit("up", g_up)
    dz2, dz2b, dg2, db2 = _ln_bwd(dX2o, sv["xh2"], sv["rs2"], S["ln2_g"], name=f"ln2_bwd_{tag}", sched=sched)
    dX1o, dq_b, dkv = _xattn_bwd(d, dz2, dz2b, sv["q_b"], sv["kv"], G["xq"], G["xo"], name=f"xattn_bwd_{tag}", sched=sched)
    g_xo = mm(dz2b, sv["o_b"], "tn", name=f"xo_dw_{tag}")
    emit("xo", g_xo)
    g_xq = mm(sv["X1b"], dq_b, "tn", name=f"xq_dw_{tag}")
    emit("xq", g_xq)
    dkv_b = dkv.transpose(1, 0, 2).reshape(d.MEM, 2 * d.XW).astype(BF16)
    g_xkv = mm(memb, dkv_b, "tn", name=f"xkv_dw_{tag}")
    emit("xkv", g_xkv)
    dz1, dz1b, dg1, db1 = _ln_bwd(dX1o, sv["xh1"], sv["rs1"], S["ln1_g"], name=f"ln1_bwd_{tag}", sched=sched)
    dya_b, dyb_b, dga_b, dgb_b, dbga, dbgb = mm(
        dz1b, G["o"], "nt", name=f"oproj_dx_{tag}", tm=1024, tn=512,
        out_defs=(("mn", BF16), ("mn", BF16), ("mn", BF16), ("mn", BF16), ("acc", F32), ("acc", F32)),
        epilogue=_ep_gate_bwd,
        extras=((sv["P"], "mn", off_ga), (sv["P"], "mn", off_gb), (sv["ya"], "mn", 0), (sv["yb"], "mn", 0),
                (S["b_gate"], "row", 0), (S["b_gate"], "row", D)),
    )
    g_o = mm(sv["merged"], dz1b, "tn", name=f"oproj_dw_{tag}", tn=D)
    emit("o", g_o)
    duv_b, dwm, dbsT, dlvg, dlvb = _gmlp_bwd(d, sv["P"], dya_b, G["a"], S["ln_v_g"], S["ln_v_b"], S["w_s"], S["b_sT"], name=f"gmlp_bwd_{tag}", sched=sched)
    g_a = mm(dya_b, sv["sgu"], "tn", name=f"bra_dw_{tag}")
    emit("a", g_a)
    dqr_b, dkvr_b, dsk = _swa_bwd(d, sv["P"], dyb_b, G["b"], cos2, sin2, S["sinks"], name=f"swa_bwd_{tag}", sched=sched)
    g_b = mm(dyb_b, sv["att"], "tn", name=f"brb_dw_{tag}")
    emit("b", g_b)
    dP = jnp.concatenate([duv_b, dqr_b, dkvr_b, dga_b, dgb_b], axis=1)
    dXin = mm(dP, G["in"], "nn", name=f"proj_dx_{tag}", tn=D, tk=_tile(d.IN, 1920), epilogue=_ep_resid, extras=((dz1, "mn", 0),))
    g_in = mm(dP, sv["Xb"], "tn", name=f"proj_dw_{tag}", tn=D)
    emit("in", g_in)
    big = {"in": g_in, "a": g_a, "b": g_b, "o": g_o, "xq": g_xq, "xkv": g_xkv, "xo": g_xo, "up": g_up, "down": g_down}
    small = dict(b_gate=jnp.concatenate([dbga, dbgb], axis=1), ln_v_g=dlvg, ln_v_b=dlvb, w_s=dwm, b_sT=dbsT, sinks=dsk,
                 ln1_g=dg1, ln1_b=db1, ln2_g=dg2, ln2_b=db2, ln3_g=dg3, ln3_b=db3)
    return dXin, big, small


def _rope_tables(T):
    inv = 1.0 / (ROPE_THETA ** (jnp.arange(0, HEAD_DIM, 2, dtype=F32) / HEAD_DIM))
    ang = jnp.arange(T, dtype=F32)[:, None] * inv[None, :]
    cos, sin = jnp.cos(ang), jnp.sin(ang)
    reps = LANES // HEAD_DIM
    return jnp.concatenate([cos, cos] * reps, axis=1), jnp.concatenate([-sin, sin] * reps, axis=1)


def _local_step(d, x, mem, target, Gs, Ss, sched=None, on_grad=None):
    cos2, sin2 = _rope_tables(d.T)
    memb = mem.astype(BF16)
    X, Xb = x, x.astype(BF16)
    saved = []
    for l in range(DEPTH):
        X, Xb, sv = _layer_fwd(d, X, Xb, memb, Gs[l], Ss[l], cos2, sin2, f"l{l}", sched)
        saved.append(sv)
    loss, dX = _loss_head(X, target, name="loss_head", sched=sched)
    bigs, smalls = [None] * DEPTH, [None] * DEPTH
    for l in reversed(range(DEPTH)):
        emit = None if on_grad is None else functools.partial(on_grad, l)
        dX, bigs[l], smalls[l] = _layer_bwd(d, dX, saved[l], memb, Gs[l], Ss[l], cos2, sin2, f"l{l}", sched, emit)
    return loss, dX, bigs, smalls


def _place():
    x, y, c = lax.axis_index("x"), lax.axis_index("y"), lax.axis_index("c")
    return x, y, c, [(1 - x, y), (x, 1 - y), (1 - x, 1 - y)]


def _all_gather(shards, *, name, sched=None):
    n = len(shards)
    any_spec = pl.BlockSpec(memory_space=pl.ANY)

    def body(*refs):
        ins, outs = refs[:n], refs[n : 2 * n]
        send_sems, recv_sems, local_sems = refs[2 * n :]
        x, y, c, chips = _place()
        me, sibling = (x, y, c), (x, y, 1 - c)

        def rows(w, p):
            r = ins[w].shape[0]
            return outs[w].at[pl.ds((4 * p[0] + 2 * p[1] + p[2]) * r, r), :]

        def copy(w, k, block, to, src=None):
            return pltpu.make_async_remote_copy(
                src_ref=rows(w, block) if src is None else src, dst_ref=rows(w, block),
                send_sem=send_sems.at[7 * w + k], recv_sem=recv_sems.at[7 * w + k], device_id=to, device_id_type=MESH)

        mine = [pltpu.make_async_copy(ins[w], rows(w, me), local_sems.at[w]) for w in range(n)]
        for cp in mine:
            cp.start()
        first = []
        for w in range(n):
            first.append(copy(w, 0, me, sibling, src=ins[w]))
            first += [copy(w, 1 + j, me, (*chip, c), src=ins[w]) for j, chip in enumerate(chips)]
        for cp in first:
            cp.start()
        passed = []
        for j, chip in enumerate(chips):
            for w in range(n):
                copy(w, 1 + j, (*chip, c), me).wait_recv()
                fwd = copy(w, 4 + j, (*chip, c), sibling)
                fwd.start()
                passed.append(fwd)
        for w in range(n):
            copy(w, 0, sibling, me).wait_recv()
            for j, chip in enumerate(chips):
                copy(w, 4 + j, (*chip, 1 - c), me).wait_recv()
        for cp in first + passed:
            cp.wait_send()
        for cp in mine:
            cp.wait()

    return _hosted(
        body,
        name=name,
        sched=sched,
        in_specs=[any_spec] * n,
        out_specs=[any_spec] * n,
        out_shape=[jax.ShapeDtypeStruct((N_DEV * s.shape[0], s.shape[1]), s.dtype) for s in shards],
        scratch_shapes=[pltpu.SemaphoreType.DMA((7 * n,)), pltpu.SemaphoreType.DMA((7 * n,)), pltpu.SemaphoreType.DMA((n,))],
    )(*shards)


class _Task:
    def __init__(self, ins, out_shapes, n_remote, n_local, plan, done, aliases=None):
        self.ins, self.out_shapes, self.n_remote, self.n_local = list(ins), list(out_shapes), n_remote, n_local
        self.plan, self.done, self.aliases = plan, done, dict(aliases or {})


class _Sched:
    def __init__(self):
        self.pending = {}

    def at(self, host, task):
        self.pending.setdefault(host, []).append(task)

    def take(self, host):
        return self.pending.pop(host, [])


def _hosted(body, *, name, sched=None, tasks=(), grid=(), in_specs=None, out_specs=None, out_shape=(), scratch_shapes=(), compiler_params=None):
    tasks = list(tasks) + (sched.take(name) if sched is not None else [])
    scratch_shapes = list(scratch_shapes)
    kwargs = dict(name=name)
    if grid:
        kwargs["grid"] = grid
    if compiler_params is not None:
        kwargs["compiler_params"] = compiler_params
    if not tasks:
        if in_specs is not None:
            kwargs.update(in_specs=in_specs, out_specs=out_specs)
        return pl.pallas_call(body, out_shape=out_shape, scratch_shapes=scratch_shapes, **kwargs)
    assert in_specs is not None and out_specs is not None, name
    single = not isinstance(out_shape, (list, tuple))
    core_shape = [out_shape] if single else list(out_shape)
    core_specs = [out_specs] if single else list(out_specs)
    in_specs = list(in_specs)
    n_in, n_out, n_scr = len(in_specs), len(core_shape), len(scratch_shapes)
    t_ins = [a for t in tasks for a in t.ins]
    t_outs = [s for t in tasks for s in t.out_shapes]
    n_rem = sum(t.n_remote for t in tasks)
    n_loc = sum(t.n_local for t in tasks)
    aliases, pos_in, pos_out = {}, n_in, n_out
    for t in tasks:
        for a, b in t.aliases.items():
            aliases[pos_in + a] = pos_out + b
        pos_in += len(t.ins)
        pos_out += len(t.out_shapes)
    any_spec = pl.BlockSpec(memory_space=pl.ANY)

    def wrapped(*refs):
        core_in, t_in = refs[:n_in], refs[n_in : n_in + len(t_ins)]
        o0 = n_in + len(t_ins)
        core_out, t_out = refs[o0 : o0 + n_out], refs[o0 + n_out : o0 + n_out + len(t_outs)]
        s0 = o0 + n_out + len(t_outs)
        core_scr = refs[s0 : s0 + n_scr]
        send_sems, recv_sems, local_sems = refs[s0 + n_scr :]
        remote, local = [], []
        pi = po = pr = pq = 0
        for t in tasks:
            r, q = t.plan(t_in[pi : pi + len(t.ins)], t_out[po : po + len(t.out_shapes)], send_sems, recv_sems, local_sems, pr, pq)
            assert len(r) == t.n_remote and len(q) == t.n_local
            remote += r
            local += q
            pi, po, pr, pq = pi + len(t.ins), po + len(t.out_shapes), pr + t.n_remote, pq + t.n_local

        def start():
            for cp in local + remote:
                cp.start()

        def finish():
            for cp in remote + local:
                cp.wait()

        if grid:
            ids = [pl.program_id(a) for a in range(len(grid))]
            first = functools.reduce(jnp.logical_and, [i == 0 for i in ids])
            last = functools.reduce(jnp.logical_and, [i == g - 1 for i, g in zip(ids, grid)])
            pl.when(first)(start)
            body(*core_in, *core_out, *core_scr)
            pl.when(last)(finish)
        else:
            start()
            body(*core_in, *core_out, *core_scr)
            finish()

    call = pl.pallas_call(
        wrapped,
        in_specs=in_specs + [any_spec] * len(t_ins),
        out_specs=core_specs + [any_spec] * len(t_outs),
        out_shape=core_shape + t_outs,
        scratch_shapes=scratch_shapes
        + [pltpu.SemaphoreType.DMA((n_rem,)), pltpu.SemaphoreType.DMA((n_rem,)), pltpu.SemaphoreType.DMA((max(n_loc, 1),))],
        input_output_aliases=aliases,
        **kwargs,
    )

    def run(*operands):
        outs = call(*operands, *t_ins)
        po = n_out
        for t in tasks:
            t.done(list(outs[po : po + len(t.out_shapes)]))
            po += len(t.out_shapes)
        return outs[0] if single else list(outs[:n_out])

    return run


def _comm_only(tasks, *, name):
    _hosted(lambda: None, name=name, tasks=tasks, in_specs=[], out_shape=[], out_specs=[])()


def _rows(ref, block, n_blocks):
    r = ref.shape[0] // n_blocks
    return ref.at[pl.ds(block * r, r), :]


def _remote(src, dst, send_sems, recv_sems, k, to):
    return pltpu.make_async_remote_copy(src_ref=src, dst_ref=dst, send_sem=send_sems.at[k], recv_sem=recv_sems.at[k],
                                        device_id=to, device_id_type=MESH)


def _gather_stage1(shards, done):
    n = len(shards)

    def plan(ins, outs, send_sems, recv_sems, local_sems, pr, pq):
        x, y, c, chips = _place()
        mine = 4 * x + 2 * y + c
        remote, local = [], []
        for w in range(n):
            dst = _rows(outs[w], mine, N_DEV)
            local.append(pltpu.make_async_copy(ins[w], dst, local_sems.at[pq + w]))
            for k, to in enumerate([(x, y, 1 - c)] + [(*chip, c) for chip in chips]):
                remote.append(_remote(ins[w], dst, send_sems, recv_sems, pr + 4 * w + k, to))
        return remote, local

    shapes = [jax.ShapeDtypeStruct((N_DEV * s.shape[0], s.shape[1]), s.dtype) for s in shards]
    return _Task(shards, shapes, 4 * n, n, plan, done)


def _gather_stage2(partial, done):
    n = len(partial)

    def plan(ins, outs, send_sems, recv_sems, local_sems, pr, pq):
        x, y, c, chips = _place()
        remote = []
        for w in range(n):
            for j, chip in enumerate(chips):
                rows = _rows(outs[w], 4 * chip[0] + 2 * chip[1] + c, N_DEV)
                remote.append(_remote(rows, rows, send_sems, recv_sems, pr + 3 * w + j, (x, y, 1 - c)))
        return remote, []

    shapes = [jax.ShapeDtypeStruct(p.shape, p.dtype) for p in partial]
    return _Task(partial, shapes, 3 * n, 0, plan, done, aliases={w: w for w in range(n)})


def _sibling_stage(grads, done):
    n = len(grads)

    def plan(ins, outs, send_sems, recv_sems, local_sems, pr, pq):
        x, y, c, _ = _place()
        remote = []
        for w in range(n):
            for k in range(4):
                src = _rows(ins[w], 4 * (k // 2) + 2 * (k % 2) + (1 - c), N_DEV)
                remote.append(_remote(src, _rows(outs[w], k, 4), send_sems, recv_sems, pr + 4 * w + k, (x, y, 1 - c)))
        return remote, []

    shapes = [jax.ShapeDtypeStruct((g.shape[0] // 2, g.shape[1]), g.dtype) for g in grads]
    return _Task(grads, shapes, 4 * n, 0, plan, done)


def _chips_stage(sends, done):
    n = len(sends)

    def plan(ins, outs, send_sems, recv_sems, local_sems, pr, pq):
        x, y, c, chips = _place()
        remote = []
        for w in range(n):
            for j, chip in enumerate(chips):
                remote.append(_remote(_rows(ins[w], j, 3), _rows(outs[w], j, 3), send_sems, recv_sems, pr + 3 * w + j, (*chip, c)))
        return remote, []

    shapes = [jax.ShapeDtypeStruct(s.shape, s.dtype) for s in sends]
    return _Task(sends, shapes, 3 * n, 0, plan, done)


def _pair_sum(grad, got, place, *, name):
    R, C = grad.shape
    r = R // N_DEV
    tr = _tile(r, 256, 16)
    nt = r // tr

    def chip_of(s, pr):
        fx = jnp.where((s == 1) | (s == 3), 1, 0)
        fy = jnp.where(s >= 2, 1, 0)
        return pr[0] ^ fx, pr[1] ^ fy

    def grad_map(s, t, pr):
        kx, ky = chip_of(s, pr)
        return ((4 * kx + 2 * ky + pr[2]) * nt + t, 0)

    def got_map(s, t, pr):
        kx, ky = chip_of(s, pr)
        return ((2 * kx + ky) * nt + t, 0)

    def body(pr, g_ref, r_ref, own_ref, send_ref):
        s = pl.program_id(0)
        tot = g_ref[...] + r_ref[...]

        @pl.when(s == 0)
        def _():
            own_ref[...] = tot

        @pl.when(s > 0)
        def _():
            send_ref[...] = tot.astype(BF16)

    return pl.pallas_call(
        body,
        name=name,
        grid_spec=pltpu.PrefetchScalarGridSpec(
            num_scalar_prefetch=1,
            grid=(4, nt),
            in_specs=[pl.BlockSpec((tr, C), grad_map), pl.BlockSpec((tr, C), got_map)],
            out_specs=[
                pl.BlockSpec((tr, C), lambda s, t, pr: (jnp.where(s == 0, t, nt - 1), 0)),
                pl.BlockSpec((tr, C), lambda s, t, pr: (jnp.where(s == 0, 0, (s - 1) * nt + t), 0)),
            ],
        ),
        out_shape=[jax.ShapeDtypeStruct((r, C), F32), jax.ShapeDtypeStruct((3 * r, C), BF16)],
        compiler_params=_params(("arbitrary", "arbitrary"), [((tr, C), F32)] * 4),
    )(place, grad, got)


def _adam_vals(w, g, m, v):
    m = ADAM_B1 * m + (1.0 - ADAM_B1) * g
    v = ADAM_B2 * v + (1.0 - ADAM_B2) * (g * g)
    m_hat = m / (1.0 - ADAM_B1**ADAM_STEP)
    v_hat = v / (1.0 - ADAM_B2**ADAM_STEP)
    delta = -ADAM_LR * (m_hat / (jnp.sqrt(v_hat) + ADAM_EPS) + ADAM_WD * w)
    return delta, m, v


def _adam_big(w, m, v, own, got, *, name, sched=None):
    r, C = w.shape
    tr = _tile(r, 256, 16)
    nt = r // tr

    def body(w_ref, m_ref, v_ref, own_ref, g1_ref, g2_ref, g3_ref, g_ref, d_ref, nm_ref, nv_ref):
        g = ((own_ref[...] + g1_ref[...].astype(F32)) + g2_ref[...].astype(F32)) + g3_ref[...].astype(F32)
        g_ref[...] = g
        d_ref[...], nm_ref[...], nv_ref[...] = _adam_vals(w_ref[...], g, m_ref[...], v_ref[...])

    row = pl.BlockSpec((tr, C), lambda t: (t, 0))
    return _hosted(
        body,
        name=name,
        sched=sched,
        grid=(nt,),
        in_specs=[row, row, row, row] + [pl.BlockSpec((tr, C), lambda t, j=j: (j * nt + t, 0)) for j in range(3)],
        out_specs=[row] * 4,
        out_shape=[jax.ShapeDtypeStruct((r, C), F32)] * 4,
        compiler_params=_params(("arbitrary",), [((tr, C), F32)] * 10),
    )(w, m, v, own, got, got, got)


def _adam_small(w, m, v, parts, *, name, sched=None):
    R = w.shape[0]

    def body(w_ref, m_ref, v_ref, p_ref, g_ref, d_ref, nm_ref, nv_ref):
        g = p_ref[pl.ds(0, R), :]
        for k in range(1, N_DEV):
            g = g + p_ref[pl.ds(k * R, R), :]
        g_ref[...] = g
        d_ref[...], nm_ref[...], nv_ref[...] = _adam_vals(w_ref[...], g, m_ref[...], v_ref[...])

    return _hosted(
        body,
        name=name,
        sched=sched,
        out_shape=[jax.ShapeDtypeStruct((R, LANES), F32)] * 4,
        compiler_params=_params(None, [((R, LANES), F32)] * 16),
    )(w, m, v, parts)


SMALL_NAMES = ("b_gate", "ln_v_g", "ln_v_b", "w_s", "b_s", "sinks", "ln1_g", "ln1_b", "ln2_g", "ln2_b", "ln3_g", "ln3_b")
PACK_ALIGN = 8 * LANES


def _pack(arrays):
    flat = []
    for a in arrays:
        a = a.reshape(-1)
        flat.append(jnp.pad(a, (0, (-a.shape[0]) % PACK_ALIGN)))
    return jnp.concatenate(flat).reshape(-1, LANES)


def _unpack(packed, shapes):
    flat = packed.reshape(-1)
    out, pos = [], 0
    for s in shapes:
        n = 1
        for e in s:
            n *= e
        out.append(flat[pos : pos + n].reshape(s))
        pos += n + (-n) % PACK_ALIGN
    return out


BIG = (("in", "w_in", True), ("a", "w_br_a", True), ("b", "w_br_b", True), ("o", "w_o", False), ("xq", "w_xq", False),
       ("xkv", "w_xkv", False), ("xo", "w_xo", True), ("up", "w_up", True), ("down", "w_down", False))


SMALL_BIG = ("a", "b", "o", "xq", "xkv", "xo")
LAYER1_GATHER = (
    (("in",), "proj_l0", "oproj_ln_l0"),
    (SMALL_BIG, "swa_fwd_l0", "oproj_ln_l0"),
    (("up",), "up_l0", "down_ln_l0"),
    (("down",), "down_ln_l0", "proj_l1"),
)
GRAD_HOSTS = {
    1: {"down": ("xattn_bwd_l1", "proj_dx_l1"), "up": ("xattn_bwd_l1", "proj_dw_l1"),
        **{g: ("proj_dx_l1", "down_dx_l0") for g in SMALL_BIG}, "in": ("down_dx_l0", "up_dx_l0")},
    0: {"down": ("xattn_bwd_l0", "swa_bwd_l0"), "up": ("xattn_bwd_l0", "proj_dx_l0"),
        **{g: ("proj_dx_l0", "proj_dw_l0") for g in SMALL_BIG}, "in": (None, None)},
}


def kernel(x, mem, w_in, b_gate, ln_v_g, ln_v_b, w_s, b_s, sinks, w_br_a, w_br_b, w_o, ln1_g, ln1_b, w_xq, w_xkv, w_xo, ln2_g, ln2_b, w_up, w_down, ln3_g, ln3_b, loss_target, m_w_in, m_b_gate, m_ln_v_g, m_ln_v_b, m_w_s, m_b_s, m_sinks, m_w_br_a, m_w_br_b, m_w_o, m_ln1_g, m_ln1_b, m_w_xq, m_w_xkv, m_w_xo, m_ln2_g, m_ln2_b, m_w_up, m_w_down, m_ln3_g, m_ln3_b, v_w_in, v_b_gate, v_ln_v_g, v_ln_v_b, v_w_s, v_b_s, v_sinks, v_w_br_a, v_w_br_b, v_w_o, v_ln1_g, v_ln1_b, v_w_xq, v_w_xkv, v_w_xo, v_ln2_g, v_ln2_b, v_w_up, v_w_down, v_ln3_g, v_ln3_b):
    given = dict(locals())
    T, D = x.shape[1], x.shape[2]
    IN, GW, AW, XW, DFF = w_in.shape[2] * N_DEV, ln_v_g.shape[1], w_br_b.shape[1], w_xq.shape[2], w_up.shape[2] * N_DEV
    KVW = (IN - 2 * GW - AW - 2 * D) // 2
    d = Dims(T=T, D=D, IN=IN, GW=GW, G=GW // GROUP_DIM, AW=AW, KVW=KVW, HQ=AW // HEAD_DIM, HKV=KVW // HEAD_DIM, XW=XW,
             XH=XW // X_HEAD_DIM, MEM=mem.shape[1], DFF=DFF)
    place = jnp.stack([lax.axis_index("x"), lax.axis_index("y"), lax.axis_index("c")]).astype(jnp.int32)

    sched = _Sched()
    big_of = {g: (p, tr) for g, p, tr in BIG}

    def shard(l, g):
        p, tr = big_of[g]
        return (given[p][l].T if tr else given[p][l]).astype(BF16)

    Gs = [dict(zip([g for g, _, _ in BIG], _all_gather([shard(0, g) for g, _, _ in BIG], name="gather_weights_l0"))), {}]
    for names, host1, host2 in LAYER1_GATHER:
        def after1(part, names=names, host2=host2):
            sched.at(host2, _gather_stage2(part, lambda full, names=names: Gs[1].update(zip(names, full))))

        sched.at(host1, _gather_stage1([shard(1, g) for g in names], after1))
    Ss = []
    for l in range(DEPTH):
        S = {n: given[n][l].reshape(1, -1) for n in SMALL_NAMES if n not in ("w_s", "b_s", "sinks")}
        S["w_s"], S["b_sT"], S["sinks"] = w_s[l], b_s[l].T, sinks[l]
        Ss.append(S)

    owns, recvs = {}, {}

    def on_grad(l, g, grad):
        sib_host, chips_host = GRAD_HOSTS[l][g]

        def after_sibling(got):
            owns[(l, g)], send = _pair_sum(grad, got[0], place, name=f"grad_pair_sum_{g}_l{l}")
            task = _chips_stage([send], lambda r: recvs.__setitem__((l, g), r[0]))
            if chips_host is None:
                _comm_only([task], name=f"grad_chips_{g}_l{l}")
            else:
                sched.at(chips_host, task)

        task = _sibling_stage([grad], after_sibling)
        if sib_host is None:
            _comm_only([task], name=f"grad_sibling_{g}_l{l}")
        else:
            sched.at(sib_host, task)

    loss, dX, bigs, smalls = _local_step(d, x[0], mem[0], loss_target[0], Gs, Ss, sched, on_grad)
    assert not sched.pending, sorted(sched.pending)
    loss = lax.psum(loss[0, 0], ("x", "y", "c"))

    res = {}
    for l in reversed(range(DEPTH)):
        for g, p, tr in BIG:
            w, m, v = given[p][l], given["m_" + p][l], given["v_" + p][l]
            if tr:
                w, m, v = w.T, m.T, v.T
            outs = _adam_big(w, m, v, owns[(l, g)], recvs[(l, g)], name=f"adamw_{g}_l{l}")
            res[(p, l)] = [o.T for o in outs] if tr else outs

    def small_list(src, l):
        return [src[l][n] if n != "b_s" else src[l]["b_sT"].T for n in SMALL_NAMES]

    shapes = [given[n].shape[1:] for n in SMALL_NAMES]
    part = _pack([a if n != "sinks" else a[0, : d.HQ] for l in range(DEPTH) for n, a in zip(SMALL_NAMES, small_list(smalls, l))])
    parts = _all_gather([part], name="gather_small_grads")[0]
    pw, pm, pv = (_pack([given[pre + n][l] for l in range(DEPTH) for n in SMALL_NAMES]) for pre in ("", "m_", "v_"))
    small_out = [_unpack(o, shapes * DEPTH) for o in _adam_small(pw, pm, pv, parts, name="adamw_small")]

    names = ["w_in", "b_gate", "ln_v_g", "ln_v_b", "w_s", "b_s", "sinks", "w_br_a", "w_br_b", "w_o", "ln1_g", "ln1_b", "w_xq",
             "w_xkv", "w_xo", "ln2_g", "ln2_b", "w_up", "w_down", "ln3_g", "ln3_b"]
    out = [loss, dX[None]]
    for kind in range(4):
        for n in names:
            if n in SMALL_NAMES:
                i = SMALL_NAMES.index(n)
                out.append(jnp.stack([small_out[kind][l * len(SMALL_NAMES) + i] for l in range(DEPTH)]))
            else:
                out.append(jnp.stack([res[(n, l)][kind] for l in range(DEPTH)]))
    return tuple(out)
```

```python
import collections
import functools

import jax
import jax.numpy as jnp
from jax import lax
from jax.experimental import pallas as pl
from jax.experimental.pallas import tpu as pltpu

F32 = jnp.float32
BF16 = jnp.bfloat16
MESH = pl.DeviceIdType.MESH

N_DEV = 8
DEPTH = 2
CHUNK = 128
HEAD_DIM = 64
ROPE_HALF = HEAD_DIM // 2
X_HEAD_DIM = 128
GROUP_DIM = 128
LANES = 128
ROPE_THETA = 10000.0
LN_EPS = 1e-5
ALPHA = (2 * DEPTH) ** 0.25
ADAM_LR = 0.001
ADAM_B1 = 0.9
ADAM_B2 = 0.999
ADAM_EPS = 1e-08
ADAM_WD = 0.01
ADAM_STEP = 10
VMEM_BYTES = 64 * 1024 * 1024
VMEM_TEMP_BYTES = 20 * 1024 * 1024

Dims = collections.namedtuple("Dims", "T D IN GW G AW KVW HQ HKV XW XH MEM DFF")


def _offsets(d):
    off_v = d.GW
    off_q = 2 * d.GW
    off_k = off_q + d.AW
    off_va = off_k + d.KVW
    off_ga = off_va + d.KVW
    off_gb = off_ga + d.D
    return off_v, off_q, off_k, off_va, off_ga, off_gb


def _tile(dim, pref, align=LANES):
    if dim <= pref:
        return dim
    t = pref - pref % align
    while t >= align:
        if dim % t == 0:
            return t
        t -= align
    return dim


def _nbytes(shape, dtype):
    n = 1
    for s in shape:
        n *= s
    return n * jnp.dtype(dtype).itemsize


def _params(sem, blocks, scratch=0):
    need = 2 * sum(_nbytes(s, t) for s, t in blocks) + scratch + VMEM_TEMP_BYTES
    limit = min(max(need, 32 * 1024 * 1024), VMEM_BYTES - 4 * 1024 * 1024)
    return pltpu.CompilerParams(dimension_semantics=sem, vmem_limit_bytes=limit)


def _dot(a, b, kind):
    dn = {"nn": (((1,), (0,)), ((), ())), "nt": (((1,), (1,)), ((), ())), "tn": (((0,), (0,)), ((), ()))}[kind]
    return lax.dot_general(a, b, dn, preferred_element_type=F32)


def _gelu(x):
    c = 0.7978845608028654
    t = jnp.tanh(c * (x + 0.044715 * (x * x * x)))
    return 0.5 * x * (1.0 + t)


def _gelu_grad(x):
    c = 0.7978845608028654
    x2 = x * x
    t = jnp.tanh(c * (x + 0.044715 * (x2 * x)))
    return 0.5 * (1.0 + t) + 0.5 * x * (1.0 - t * t) * (c * (1.0 + 3.0 * 0.044715 * x2))


def _ln_fwd(z, g, b):
    mu = jnp.mean(z, axis=-1, keepdims=True)
    zc = z - mu
    var = jnp.mean(zc * zc, axis=-1, keepdims=True)
    rstd = lax.rsqrt(var + LN_EPS)
    xhat = zc * rstd
    return xhat * g + b, xhat, rstd


def _ln_bwd_vals(dy, xhat, rstd, g):
    gd = dy * g
    m1 = jnp.mean(gd, axis=-1, keepdims=True)
    m2 = jnp.mean(gd * xhat, axis=-1, keepdims=True)
    return rstd * (gd - m1 - xhat * m2)


def _acc_store(ref, val, first):
    @pl.when(first)
    def _():
        ref[...] = val

    @pl.when(jnp.logical_not(first))
    def _():
        ref[...] += val


def _matmul(a, b, kind, *, name, tm=512, tn=1024, tk=2048, out_defs=(("mn", F32),), epilogue=None, extras=(), sched=None):
    if kind == "nn":
        (M, K), (K2, N) = a.shape, b.shape
    elif kind == "nt":
        (M, K), (N, K2) = a.shape, b.shape
    else:
        (K, M), (K2, N) = a.shape, b.shape
    assert K == K2, (a.shape, b.shape, kind)
    tm, tn, tk = _tile(M, tm), _tile(N, tn), _tile(K, tk)
    nk = K // tk
    blocks = []

    def spec(shape, imap, dtype):
        blocks.append((shape, dtype))
        return pl.BlockSpec(shape, imap)

    if kind == "tn":
        a_spec = spec((tk, tm), lambda j, i, k: (k, i), a.dtype)
    else:
        a_spec = spec((tm, tk), lambda j, i, k: (i, k), a.dtype)
    if kind == "nt":
        b_spec = spec((tn, tk), lambda j, i, k: (j, k), b.dtype)
    else:
        b_spec = spec((tk, tn), lambda j, i, k: (k, j), b.dtype)
    in_specs = [a_spec, b_spec]
    operands = [a, b]
    for arr, ekind, off in extras:
        if ekind == "mn":
            assert off % tn == 0
            in_specs.append(spec((tm, tn), lambda j, i, k, o=off // tn: (i, j + o), arr.dtype))
        elif ekind == "row":
            assert off % tn == 0
            in_specs.append(spec((1, tn), lambda j, i, k, o=off // tn: (0, j + o), arr.dtype))
        else:
            in_specs.append(spec((tm, 1), lambda j, i, k: (i, 0), arr.dtype))
        operands.append(arr)
    out_shape, out_specs = [], []
    for okind, dt in out_defs:
        if okind == "mn":
            out_shape.append(jax.ShapeDtypeStruct((M, N), dt))
            out_specs.append(spec((tm, tn), lambda j, i, k: (i, j), dt))
        elif okind == "m1":
            assert N == tn
            out_shape.append(jax.ShapeDtypeStruct((M, 1), dt))
            out_specs.append(spec((tm, 1), lambda j, i, k: (i, 0), dt))
        else:
            out_shape.append(jax.ShapeDtypeStruct((1, N), F32))
            out_specs.append(spec((1, tn), lambda j, i, k: (0, j), F32))
    n_ex, n_out = len(extras), len(out_defs)
    ep = epilogue if epilogue is not None else (lambda acc: (acc,))

    def body(*refs):
        a_ref, b_ref = refs[0], refs[1]
        ex_refs = refs[2 : 2 + n_ex]
        out_refs = refs[2 + n_ex : 2 + n_ex + n_out]
        i = pl.program_id(1)
        k = pl.program_id(2)
        part = _dot(a_ref[...], b_ref[...], kind)

        def finish(acc):
            vals = ep(acc, *[r[...] for r in ex_refs])
            for (okind, dt), r, v in zip(out_defs, out_refs, vals):
                if okind == "acc":
                    _acc_store(r, v, i == 0)
                else:
                    r[...] = v.astype(dt)

        if nk == 1:
            finish(part)
        else:
            acc_ref = refs[-1]

            @pl.when(k == 0)
            def _():
                acc_ref[...] = part

            @pl.when(k > 0)
            def _():
                acc_ref[...] += part

            @pl.when(k == nk - 1)
            def _():
                finish(acc_ref[...])

    scratch = [pltpu.VMEM((tm, tn), F32)] if nk > 1 else []
    outs = _hosted(
        body,
        name=name,
        sched=sched,
        grid=(N // tn, M // tm, nk),
        in_specs=in_specs,
        out_specs=out_specs,
        out_shape=out_shape,
        scratch_shapes=scratch,
        compiler_params=_params(("arbitrary", "arbitrary", "arbitrary"), blocks, _nbytes((tm, tn), F32) if nk > 1 else 0),
    )(*operands)
    return outs if len(outs) > 1 else outs[0]


def _ep_resid_ln(acc, resid, g, b):
    y, xhat, rstd = _ln_fwd(ALPHA * resid + acc, g, b)
    return y, y, xhat, rstd


def _ep_resid(acc, resid):
    return (ALPHA * resid + acc,)


def _ep_relu2(acc):
    r = jnp.maximum(acc, 0.0)
    return acc, r * r


def _ep_relu2_bwd(acc, h):
    return (acc * (2.0 * jnp.maximum(h, 0.0)),)


def _ep_gate_bwd(acc, ga, gb, ya, yb, bga, bgb):
    sa = jax.nn.sigmoid(ga + bga)
    sb = jax.nn.sigmoid(gb + bgb)
    dga = acc * ya * (sa * (1.0 - sa))
    dgb = acc * yb * (sb * (1.0 - sb))
    return (acc * sa, acc * sb, dga, dgb, jnp.sum(dga, axis=0, keepdims=True), jnp.sum(dgb, axis=0, keepdims=True))


def _ln_bwd(dy, xhat, rstd, g, *, name, sched=None):
    T, D = dy.shape
    tm = _tile(T, 256)

    def body(dy_ref, xh_ref, rs_ref, g_ref, dz_ref, dzb_ref, dg_ref, db_ref):
        i = pl.program_id(0)
        dyv, xh = dy_ref[...], xh_ref[...]
        dz = _ln_bwd_vals(dyv, xh, rs_ref[...], g_ref[...])
        dz_ref[...] = dz
        dzb_ref[...] = dz.astype(BF16)
        _acc_store(dg_ref, jnp.sum(dyv * xh, axis=0, keepdims=True), i == 0)
        _acc_store(db_ref, jnp.sum(dyv, axis=0, keepdims=True), i == 0)

    row = pl.BlockSpec((tm, D), lambda i: (i, 0))
    vec = pl.BlockSpec((1, D), lambda i: (0, 0))
    return _hosted(
        body,
        name=name,
        sched=sched,
        grid=(T // tm,),
        in_specs=[row, row, pl.BlockSpec((tm, 1), lambda i: (i, 0)), vec],
        out_specs=[row, row, vec, vec],
        out_shape=[
            jax.ShapeDtypeStruct((T, D), F32),
            jax.ShapeDtypeStruct((T, D), BF16),
            jax.ShapeDtypeStruct((1, D), F32),
            jax.ShapeDtypeStruct((1, D), F32),
        ],
        compiler_params=_params(("arbitrary",), [((tm, D), F32)] * 4),
    )(dy, xhat, rstd, g)


def _loss_head(y, target, *, name, sched=None):
    T, D = y.shape
    tm = _tile(T, 256)

    def body(y_ref, t_ref, loss_ref, dy_ref):
        err = y_ref[...] - t_ref[...]
        dy_ref[...] = err * (1.0 / D)
        part = 0.5 * jnp.sum(jnp.sum(err * err, axis=1, keepdims=True) * (1.0 / D), axis=0, keepdims=True)
        _acc_store(loss_ref, part, pl.program_id(0) == 0)

    row = pl.BlockSpec((tm, D), lambda i: (i, 0))
    return _hosted(
        body,
        name=name,
        sched=sched,
        grid=(T // tm,),
        in_specs=[row, row],
        out_specs=[pl.BlockSpec((1, 1), lambda i: (0, 0)), row],
        out_shape=[jax.ShapeDtypeStruct((1, 1), F32), jax.ShapeDtypeStruct((T, D), F32)],
        compiler_params=_params(("arbitrary",), [((tm, D), F32)] * 3),
    )(y, target)


def _masked_mix_weights(wm_ref, G):
    r = lax.broadcasted_iota(jnp.int32, (CHUNK, CHUNK), 0)
    c = lax.broadcasted_iota(jnp.int32, (CHUNK, CHUNK), 1)
    return [jnp.where(c <= r, wm_ref[g], 0.0).astype(BF16) for g in range(G)]


def _gmlp_fwd(d, P, lnv_g, lnv_b, wm, bsT, *, name, sched=None):
    T, GW, G = d.T, d.GW, d.G
    tm = _tile(T, 256)
    nc = tm // CHUNK

    def body(u_ref, v_ref, g_ref, b_ref, wm_ref, bs_ref, o_ref):
        gu = _gelu(u_ref[...])
        vn, _, _ = _ln_fwd(_gelu(v_ref[...]), g_ref[...], b_ref[...])
        vn = vn.astype(BF16)
        wl = _masked_mix_weights(wm_ref, G)
        for c in range(nc):
            rows = slice(c * CHUNK, (c + 1) * CHUNK)
            for g in range(G):
                cols = slice(g * GROUP_DIM, (g + 1) * GROUP_DIM)
                mixed = _dot(wl[g], vn[rows, cols], "nn") + bs_ref[:, g : g + 1]
                o_ref[rows, cols] = (gu[rows, cols] * mixed).astype(BF16)

    return _hosted(
        body,
        name=name,
        sched=sched,
        grid=(T // tm,),
        in_specs=[
            pl.BlockSpec((tm, GW), lambda i: (i, 0)),
            pl.BlockSpec((tm, GW), lambda i: (i, 1)),
            pl.BlockSpec((1, GW), lambda i: (0, 0)),
            pl.BlockSpec((1, GW), lambda i: (0, 0)),
            pl.BlockSpec((G, CHUNK, CHUNK), lambda i: (0, 0, 0)),
            pl.BlockSpec((CHUNK, G), lambda i: (0, 0)),
        ],
        out_specs=pl.BlockSpec((tm, GW), lambda i: (i, 0)),
        out_shape=jax.ShapeDtypeStruct((T, GW), BF16),
        compiler_params=_params(("arbitrary",), [((tm, GW), F32)] * 3),
    )(P, P, lnv_g, lnv_b, wm, bsT)


def _gmlp_bwd(d, P, dya_b, Ga, lnv_g, lnv_b, wm, bsT, *, name, sched=None):
    T, D, GW, G = d.T, d.D, d.GW, d.G
    tm = _tile(T, 256)
    nc = tm // CHUNK

    def body(u_ref, v_ref, dya_ref, ga_ref, g_ref, b_ref, wm_ref, bs_ref, duv_ref, dwm_ref, dbs_ref, dlg_ref, dlb_ref, dgu_s, dvn_s):
        first = pl.program_id(0) == 0
        dsgu = _dot(dya_ref[...], ga_ref[...], "nn")
        u, v = u_ref[...], v_ref[...]
        gu = _gelu(u)
        lng = g_ref[...]
        vn, xh, rstd = _ln_fwd(_gelu(v), lng, b_ref[...])
        vn = vn.astype(BF16)
        wl = _masked_mix_weights(wm_ref, G)
        r = lax.broadcasted_iota(jnp.int32, (CHUNK, CHUNK), 0)
        cc = lax.broadcasted_iota(jnp.int32, (CHUNK, CHUNK), 1)
        lane_g = lax.broadcasted_iota(jnp.int32, (CHUNK, G), 1)
        dbs = jnp.zeros((CHUNK, G), F32)
        for g in range(G):
            cols = slice(g * GROUP_DIM, (g + 1) * GROUP_DIM)
            dw = jnp.zeros((CHUNK, CHUNK), F32)
            for c in range(nc):
                rows = slice(c * CHUNK, (c + 1) * CHUNK)
                mixed = _dot(wl[g], vn[rows, cols], "nn") + bs_ref[:, g : g + 1]
                ds = dsgu[rows, cols]
                dgu_s[rows, cols] = ds * mixed
                dmx = ds * gu[rows, cols]
                dbs = dbs + jnp.where(lane_g == g, jnp.sum(dmx, axis=1, keepdims=True), 0.0)
                dmx = dmx.astype(BF16)
                dw = dw + _dot(dmx, vn[rows, cols], "nt")
                dvn_s[rows, cols] = _dot(wl[g], dmx, "tn")
            _acc_store(dwm_ref.at[g], jnp.where(cc <= r, dw, 0.0), first)
        _acc_store(dbs_ref, dbs, first)
        dvn = dvn_s[...]
        _acc_store(dlg_ref, jnp.sum(dvn * xh, axis=0, keepdims=True), first)
        _acc_store(dlb_ref, jnp.sum(dvn, axis=0, keepdims=True), first)
        dgv = _ln_bwd_vals(dvn, xh, rstd, lng)
        duv_ref[:, :GW] = (dgu_s[...] * _gelu_grad(u)).astype(BF16)
        duv_ref[:, GW:] = (dgv * _gelu_grad(v)).astype(BF16)

    return _hosted(
        body,
        name=name,
        sched=sched,
        grid=(T // tm,),
        in_specs=[
            pl.BlockSpec((tm, GW), lambda i: (i, 0)),
            pl.BlockSpec((tm, GW), lambda i: (i, 1)),
            pl.BlockSpec((tm, D), lambda i: (i, 0)),
            pl.BlockSpec((D, GW), lambda i: (0, 0)),
            pl.BlockSpec((1, GW), lambda i: (0, 0)),
            pl.BlockSpec((1, GW), lambda i: (0, 0)),
            pl.BlockSpec((G, CHUNK, CHUNK), lambda i: (0, 0, 0)),
            pl.BlockSpec((CHUNK, G), lambda i: (0, 0)),
        ],
        out_specs=[
            pl.BlockSpec((tm, 2 * GW), lambda i: (i, 0)),
            pl.BlockSpec((G, CHUNK, CHUNK), lambda i: (0, 0, 0)),
            pl.BlockSpec((CHUNK, G), lambda i: (0, 0)),
            pl.BlockSpec((1, GW), lambda i: (0, 0)),
            pl.BlockSpec((1, GW), lambda i: (0, 0)),
        ],
        out_shape=[
            jax.ShapeDtypeStruct((T, 2 * GW), BF16),
            jax.ShapeDtypeStruct((G, CHUNK, CHUNK), F32),
            jax.ShapeDtypeStruct((CHUNK, G), F32),
            jax.ShapeDtypeStruct((1, GW), F32),
            jax.ShapeDtypeStruct((1, GW), F32),
        ],
        scratch_shapes=[pltpu.VMEM((tm, GW), F32), pltpu.VMEM((tm, GW), F32)],
        compiler_params=_params(
            ("arbitrary",), [((tm, GW), F32)] * 3 + [((tm, D), BF16), ((D, GW), BF16)], 2 * _nbytes((tm, GW), F32)
        ),
    )(P, P, dya_b, Ga, lnv_g, lnv_b, wm, bsT)


def _swap_halves(x):
    w = x.shape[1]
    lane = lax.broadcasted_iota(jnp.int32, x.shape, 1)
    return jnp.where((lane % HEAD_DIM) < ROPE_HALF, pltpu.roll(x, w - ROPE_HALF, 1), pltpu.roll(x, ROPE_HALF, 1))


def _lane_tile(t, width):
    reps = width // LANES
    return t if reps == 1 else jnp.tile(t, (1, reps))


def _rope(x, cos2, sin2):
    w = x.shape[1]
    return x * _lane_tile(cos2, w) + _swap_halves(x) * _lane_tile(sin2, w)


def _rope_bwd(g, cos2, sin2):
    w = g.shape[1]
    return g * _lane_tile(cos2, w) - _swap_halves(g) * _lane_tile(sin2, w)


PAIR = LANES // HEAD_DIM


def _swa_valid(i):
    s = lax.broadcasted_iota(jnp.int32, (2 * CHUNK, CHUNK), 0)
    t = lax.broadcasted_iota(jnp.int32, (2 * CHUNK, CHUNK), 1)
    cur = jnp.logical_and(s >= CHUNK, s - CHUNK <= t)
    prev = jnp.logical_and(jnp.logical_and(s < CHUNK, s > t), i > 0)
    return jnp.logical_or(cur, prev)


def _half_variants(x, odd):
    lane = lax.broadcasted_iota(jnp.int32, x.shape, 1)
    if odd:
        hi = jnp.where(lane >= HEAD_DIM, x, jnp.zeros_like(x))
        return pltpu.roll(hi, HEAD_DIM, 1), hi
    lo = jnp.where(lane < HEAD_DIM, x, jnp.zeros_like(x))
    return lo, pltpu.roll(lo, HEAD_DIM, 1)


def _swa_probs_t(kx, qp, sink, valid):
    s = jnp.where(valid, _dot(kx, qp, "nt") * (HEAD_DIM**-0.5), -jnp.inf)
    m = jnp.maximum(jnp.max(s, axis=0, keepdims=True), sink)
    e = jnp.exp(s - m)
    e_s = jnp.exp(sink - m)
    den = jnp.sum(e, axis=0, keepdims=True) + e_s
    return e / den, e_s / den


def _swa_load(q_ref, kc_ref, kp_ref, vc_ref, vp_ref, cc, sc, cp, sp):
    qr = _rope(q_ref[...], cc, sc).astype(BF16)
    kcat = jnp.concatenate([_rope(kp_ref[...], cp, sp), _rope(kc_ref[...], cc, sc)], axis=0)
    vcat = jnp.concatenate([vp_ref[...], vc_ref[...]], axis=0)
    return qr, kcat, vcat


def _swa_specs(d):
    _, off_q, off_k, off_va, _, _ = _offsets(d)
    assert off_q % d.AW == 0 and off_k % d.KVW == 0 and off_va % d.KVW == 0
    prev = lambda i: jnp.maximum(i - 1, 0)
    return [
        pl.BlockSpec(memory_space=pltpu.SMEM),
        pl.BlockSpec((CHUNK, d.AW), lambda i, o=off_q // d.AW: (i, o)),
        pl.BlockSpec((CHUNK, d.KVW), lambda i, o=off_k // d.KVW: (i, o)),
        pl.BlockSpec((CHUNK, d.KVW), lambda i, o=off_k // d.KVW: (prev(i), o)),
        pl.BlockSpec((CHUNK, d.KVW), lambda i, o=off_va // d.KVW: (i, o)),
        pl.BlockSpec((CHUNK, d.KVW), lambda i, o=off_va // d.KVW: (prev(i), o)),
        pl.BlockSpec((CHUNK, LANES), lambda i: (i, 0)),
        pl.BlockSpec((CHUNK, LANES), lambda i: (i, 0)),
        pl.BlockSpec((CHUNK, LANES), lambda i: (prev(i), 0)),
        pl.BlockSpec((CHUNK, LANES), lambda i: (prev(i), 0)),
    ]


def _swa_fwd(d, P, cos2, sin2, sinks, *, name, sched=None):
    T, AW, HQ, HKV = d.T, d.AW, d.HQ, d.HKV
    grp = HQ // HKV
    assert grp % PAIR == 0 and HKV % PAIR == 0

    def body(sink_ref, q_ref, kc_ref, kp_ref, vc_ref, vp_ref, cc_ref, sc_ref, cp_ref, sp_ref, o_ref):
        i = pl.program_id(0)
        qr, kcat, vcat = _swa_load(q_ref, kc_ref, kp_ref, vc_ref, vp_ref, cc_ref[...], sc_ref[...], cp_ref[...], sp_ref[...])
        valid = _swa_valid(i)
        for kv in range(HKV):
            col = slice((kv // PAIR) * LANES, (kv // PAIR + 1) * LANES)
            kx = [t.astype(BF16) for t in _half_variants(kcat[:, col], kv % PAIR)]
            vx = [t.astype(BF16) for t in _half_variants(vcat[:, col], kv % PAIR)]
            for pp in range(grp // PAIR):
                p = kv * (grp // PAIR) + pp
                qs = slice(p * LANES, (p + 1) * LANES)
                out = jnp.zeros((CHUNK, LANES), F32)
                for half in range(PAIR):
                    pt, _ = _swa_probs_t(kx[half], qr[:, qs], sink_ref[PAIR * p + half], valid)
                    out = out + _dot(pt.astype(BF16), vx[half], "tn")
                o_ref[:, qs] = out.astype(BF16)

    return _hosted(
        body,
        name=name,
        sched=sched,
        grid=(T // CHUNK,),
        in_specs=_swa_specs(d),
        out_specs=pl.BlockSpec((CHUNK, AW), lambda i: (i, 0)),
        out_shape=jax.ShapeDtypeStruct((T, AW), BF16),
        compiler_params=_params(("arbitrary",), [((CHUNK, AW), F32)] * 3),
    )(sinks, P, P, P, P, P, cos2, sin2, cos2, sin2)


def _swa_bwd(d, P, dyb_b, Gb, cos2, sin2, sinks, *, name, sched=None):
    T, D, AW, KVW, HQ, HKV = d.T, d.D, d.AW, d.KVW, d.HQ, d.HKV
    grp = HQ // HKV
    nb = T // CHUNK
    scale = HEAD_DIM**-0.5
    assert grp % PAIR == 0 and HKV % PAIR == 0

    def body(sink_ref, q_ref, kc_ref, kp_ref, vc_ref, vp_ref, cc_ref, sc_ref, cp_ref, sp_ref, dyb_ref, gb_ref,
             dq_ref, dkv_ref, dsk_ref, acc_s, dq_s, dk_s, dv_s):
        i = pl.program_id(0)
        cc, sc, cp, sp = cc_ref[...], sc_ref[...], cp_ref[...], sp_ref[...]
        datt = _dot(dyb_ref[...], gb_ref[...], "nn").astype(BF16)
        qr, kcat, vcat = _swa_load(q_ref, kc_ref, kp_ref, vc_ref, vp_ref, cc, sc, cp, sp)
        valid = _swa_valid(i)
        lane1 = lax.broadcasted_iota(jnp.int32, (1, LANES), 1)
        lane2 = lax.broadcasted_iota(jnp.int32, (2 * CHUNK, LANES), 1)
        dsk = jnp.zeros((1, LANES), F32)
        for kvp in range(HKV // PAIR):
            col = slice(kvp * LANES, (kvp + 1) * LANES)
            dk_col = jnp.zeros((2 * CHUNK, LANES), F32)
            dv_col = jnp.zeros((2 * CHUNK, LANES), F32)
            for odd in range(PAIR):
                kv = kvp * PAIR + odd
                kx = [t.astype(BF16) for t in _half_variants(kcat[:, col], odd)]
                vx = [t.astype(BF16) for t in _half_variants(vcat[:, col], odd)]
                dk_half = [jnp.zeros((2 * CHUNK, LANES), F32) for _ in range(PAIR)]
                dv_half = [jnp.zeros((2 * CHUNK, LANES), F32) for _ in range(PAIR)]
                for pp in range(grp // PAIR):
                    p = kv * (grp // PAIR) + pp
                    qs = slice(p * LANES, (p + 1) * LANES)
                    qp, dop = qr[:, qs], datt[:, qs]
                    dq = jnp.zeros((CHUNK, LANES), F32)
                    for half in range(PAIR):
                        h = PAIR * p + half
                        pt, p_s = _swa_probs_t(kx[half], qp, sink_ref[h], valid)
                        dpt = _dot(vx[half], dop, "nt")
                        rs = jnp.sum(pt * dpt, axis=0, keepdims=True)
                        dst = (pt * (dpt - rs) * scale).astype(BF16)
                        dsk = dsk + jnp.where(lane1 == h, -jnp.sum(p_s * rs, axis=1, keepdims=True), 0.0)
                        dq = dq + _dot(dst, kx[half], "tn")
                        dk_half[half] = dk_half[half] + _dot(dst, qp, "nn")
                        dv_half[half] = dv_half[half] + _dot(pt.astype(BF16), dop, "nn")
                    dq_s[:, qs] = dq
                for acc_half, is_k in ((dk_half, True), (dv_half, False)):
                    lo = jnp.where(lane2 < HEAD_DIM, acc_half[0], 0.0)
                    hi = jnp.where(lane2 >= HEAD_DIM, acc_half[1], 0.0)
                    both = (pltpu.roll(lo, HEAD_DIM, 1) + hi) if odd else (lo + pltpu.roll(hi, HEAD_DIM, 1))
                    if is_k:
                        dk_col = dk_col + both
                    else:
                        dv_col = dv_col + both
            dk_s[:, col] = dk_col
            dv_s[:, col] = dv_col
        dq_ref[...] = _rope_bwd(dq_s[...], cc, sc).astype(BF16)
        cur = pl.ds(pl.multiple_of(i * CHUNK, CHUNK), CHUNK)
        acc_s[cur, :KVW] = _rope_bwd(dk_s[CHUNK:, :], cc, sc)
        acc_s[cur, KVW:] = dv_s[CHUNK:, :]

        @pl.when(i > 0)
        def _():
            prv = pl.ds(pl.multiple_of((i - 1) * CHUNK, CHUNK), CHUNK)
            acc_s[prv, :KVW] += _rope_bwd(dk_s[:CHUNK, :], cp, sp)
            acc_s[prv, KVW:] += dv_s[:CHUNK, :]

        _acc_store(dsk_ref, dsk, i == 0)

        @pl.when(i == nb - 1)
        def _():
            dkv_ref[...] = acc_s[...].astype(BF16)

    return _hosted(
        body,
        name=name,
        sched=sched,
        grid=(nb,),
        in_specs=_swa_specs(d) + [pl.BlockSpec((CHUNK, D), lambda i: (i, 0)), pl.BlockSpec((D, AW), lambda i: (0, 0))],
        out_specs=[
            pl.BlockSpec((CHUNK, AW), lambda i: (i, 0)),
            pl.BlockSpec((T, 2 * KVW), lambda i: (0, 0)),
            pl.BlockSpec((1, LANES), lambda i: (0, 0)),
        ],
        out_shape=[
            jax.ShapeDtypeStruct((T, AW), BF16),
            jax.ShapeDtypeStruct((T, 2 * KVW), BF16),
            jax.ShapeDtypeStruct((1, LANES), F32),
        ],
        scratch_shapes=[
            pltpu.VMEM((T, 2 * KVW), F32),
            pltpu.VMEM((CHUNK, AW), F32),
            pltpu.VMEM((2 * CHUNK, KVW), F32),
            pltpu.VMEM((2 * CHUNK, KVW), F32),
        ],
        compiler_params=_params(
            ("arbitrary",), [((CHUNK, AW), F32)] * 3 + [((D, AW), BF16), ((T, 2 * KVW), BF16)], _nbytes((T, 2 * KVW), F32)
        ),
    )(sinks, P, P, P, P, P, cos2, sin2, cos2, sin2, dyb_b, Gb)


def _branch_merge(d, sgu, att, Ga, Gb, P, b_gate, *, name, sched=None):
    T, D, GW, AW = d.T, d.D, d.GW, d.AW
    _, _, _, _, off_ga, off_gb = _offsets(d)
    tm, tn = _tile(T, 1024), _tile(D, 512)
    assert off_ga % tn == 0 and off_gb % tn == 0

    def body(s_ref, a_ref, ga_w, gb_w, ga_ref, gb_ref, bga_ref, bgb_ref, ya_ref, yb_ref, mg_ref):
        ya = _dot(s_ref[...], ga_w[...], "nt")
        yb = _dot(a_ref[...], gb_w[...], "nt")
        ya_ref[...] = ya
        yb_ref[...] = yb
        sa = jax.nn.sigmoid(ga_ref[...] + bga_ref[...])
        sb = jax.nn.sigmoid(gb_ref[...] + bgb_ref[...])
        mg_ref[...] = (sa * ya + sb * yb).astype(BF16)

    tile = pl.BlockSpec((tm, tn), lambda j, i: (i, j))
    return _hosted(
        body,
        name=name,
        sched=sched,
        grid=(D // tn, T // tm),
        in_specs=[
            pl.BlockSpec((tm, GW), lambda j, i: (i, 0)),
            pl.BlockSpec((tm, AW), lambda j, i: (i, 0)),
            pl.BlockSpec((tn, GW), lambda j, i: (j, 0)),
            pl.BlockSpec((tn, AW), lambda j, i: (j, 0)),
            pl.BlockSpec((tm, tn), lambda j, i, o=off_ga // tn: (i, j + o)),
            pl.BlockSpec((tm, tn), lambda j, i, o=off_gb // tn: (i, j + o)),
            pl.BlockSpec((1, tn), lambda j, i: (0, j)),
            pl.BlockSpec((1, tn), lambda j, i, o=D // tn: (0, j + o)),
        ],
        out_specs=[tile, tile, tile],
        out_shape=[jax.ShapeDtypeStruct((T, D), F32), jax.ShapeDtypeStruct((T, D), F32), jax.ShapeDtypeStruct((T, D), BF16)],
        compiler_params=_params(
            ("arbitrary", "arbitrary"),
            [((tm, GW), BF16), ((tm, AW), BF16), ((tn, GW), BF16), ((tn, AW), BF16)] + [((tm, tn), F32)] * 5,
        ),
    )(sgu, att, Ga, Gb, P, P, b_gate, b_gate)


def _xattn_probs(qh, kh):
    s = _dot(qh, kh, "nt") * (X_HEAD_DIM**-0.5)
    e = jnp.exp(s - jnp.max(s, axis=1, keepdims=True))
    return e / jnp.sum(e, axis=1, keepdims=True)


def _xattn_fwd(d, X, Xb, kv, Gxq, Gxo, ln_g, ln_b, *, name, sched=None):
    T, D, XW, XH, MEM = d.T, d.D, d.XW, d.XH, d.MEM
    tm = _tile(T, 256)

    def body(x_ref, xb_ref, kv_ref, wq_ref, wo_ref, g_ref, b_ref, y_ref, yb_ref, xh_ref, rs_ref, q_ref, o_ref, o_s):
        q = _dot(xb_ref[...], wq_ref[...], "nn").astype(BF16)
        q_ref[...] = q
        for h in range(XH):
            hs = slice(h * X_HEAD_DIM, (h + 1) * X_HEAD_DIM)
            p = _xattn_probs(q[:, hs], kv_ref[:, hs]).astype(BF16)
            o_s[:, hs] = _dot(p, kv_ref[:, XW + h * X_HEAD_DIM : XW + (h + 1) * X_HEAD_DIM], "nn").astype(BF16)
        o = o_s[...]
        o_ref[...] = o
        y, xhat, rstd = _ln_fwd(ALPHA * x_ref[...] + _dot(o, wo_ref[...], "nt"), g_ref[...], b_ref[...])
        y_ref[...] = y
        yb_ref[...] = y.astype(BF16)
        xh_ref[...] = xhat
        rs_ref[...] = rstd

    row = pl.BlockSpec((tm, D), lambda i: (i, 0))
    nar = pl.BlockSpec((tm, XW), lambda i: (i, 0))
    vec = pl.BlockSpec((1, D), lambda i: (0, 0))
    return _hosted(
        body,
        name=name,
        sched=sched,
        grid=(T // tm,),
        in_specs=[
            row,
            row,
            pl.BlockSpec((MEM, 2 * XW), lambda i: (0, 0)),
            pl.BlockSpec((D, XW), lambda i: (0, 0)),
            pl.BlockSpec((D, XW), lambda i: (0, 0)),
            vec,
            vec,
        ],
        out_specs=[row, row, row, pl.BlockSpec((tm, 1), lambda i: (i, 0)), nar, nar],
        out_shape=[
            jax.ShapeDtypeStruct((T, D), F32),
            jax.ShapeDtypeStruct((T, D), BF16),
            jax.ShapeDtypeStruct((T, D), F32),
            jax.ShapeDtypeStruct((T, 1), F32),
            jax.ShapeDtypeStruct((T, XW), BF16),
            jax.ShapeDtypeStruct((T, XW), BF16),
        ],
        scratch_shapes=[pltpu.VMEM((tm, XW), BF16)],
        compiler_params=_params(("arbitrary",), [((tm, D), F32)] * 4 + [((D, XW), BF16)] * 2),
    )(X, Xb, kv, Gxq, Gxo, ln_g, ln_b)


def _xattn_bwd(d, dz, dzb, q_b, kv, Gxq, Gxo, *, name, sched=None):
    T, D, XW, XH, MEM = d.T, d.D, d.XW, d.XH, d.MEM
    tm = _tile(T, 256)
    scale = X_HEAD_DIM**-0.5

    def body(dz_ref, dzb_ref, q_ref, kv_ref, wq_ref, wo_ref, dx_ref, dq_ref, dkv_ref, dq_s):
        first = pl.program_id(0) == 0
        do = _dot(dzb_ref[...], wo_ref[...], "nn").astype(BF16)
        for h in range(XH):
            hs = slice(h * X_HEAD_DIM, (h + 1) * X_HEAD_DIM)
            vs = slice(XW + h * X_HEAD_DIM, XW + (h + 1) * X_HEAD_DIM)
            kh, vh, qh, doh = kv_ref[:, hs], kv_ref[:, vs], q_ref[:, hs], do[:, hs]
            p = _xattn_probs(qh, kh)
            dp = _dot(doh, vh, "nt")
            ds = (p * (dp - jnp.sum(p * dp, axis=1, keepdims=True)) * scale).astype(BF16)
            dq_s[:, hs] = _dot(ds, kh, "nn").astype(BF16)
            _acc_store(dkv_ref.at[h], _dot(ds, qh, "tn"), first)
            _acc_store(dkv_ref.at[XH + h], _dot(p.astype(BF16), doh, "tn"), first)
        dq = dq_s[...]
        dq_ref[...] = dq
        dx_ref[...] = ALPHA * dz_ref[...] + _dot(dq, wq_ref[...], "nt")

    row = pl.BlockSpec((tm, D), lambda i: (i, 0))
    nar = pl.BlockSpec((tm, XW), lambda i: (i, 0))
    return _hosted(
        body,
        name=name,
        sched=sched,
        grid=(T // tm,),
        in_specs=[
            row,
            row,
            nar,
            pl.BlockSpec((MEM, 2 * XW), lambda i: (0, 0)),
            pl.BlockSpec((D, XW), lambda i: (0, 0)),
            pl.BlockSpec((D, XW), lambda i: (0, 0)),
        ],
        out_specs=[row, nar, pl.BlockSpec((2 * XH, MEM, X_HEAD_DIM), lambda i: (0, 0, 0))],
        out_shape=[
            jax.ShapeDtypeStruct((T, D), F32),
            jax.ShapeDtypeStruct((T, XW), BF16),
            jax.ShapeDtypeStruct((2 * XH, MEM, X_HEAD_DIM), F32),
        ],
        scratch_shapes=[pltpu.VMEM((tm, XW), BF16)],
        compiler_params=_params(("arbitrary",), [((tm, D), F32)] * 3 + [((D, XW), BF16)] * 2),
    )(dz, dzb, q_b, kv, Gxq, Gxo)


def _resid_ln_outs():
    return (("mn", F32), ("mn", BF16), ("mn", F32), ("m1", F32))


def _layer_fwd(d, X, Xb, memb, G, S, cos2, sin2, tag, sched=None):
    D = d.D
    mm = functools.partial(_matmul, sched=sched)
    P = mm(Xb, G["in"], "nt", name=f"proj_{tag}", tm=1024, tn=_tile(d.IN, 1280))
    sgu = _gmlp_fwd(d, P, S["ln_v_g"], S["ln_v_b"], S["w_s"], S["b_sT"], name=f"gmlp_fwd_{tag}", sched=sched)
    att = _swa_fwd(d, P, cos2, sin2, S["sinks"], name=f"swa_fwd_{tag}", sched=sched)
    ya, yb, merged = _branch_merge(d, sgu, att, G["a"], G["b"], P, S["b_gate"], name=f"merge_{tag}", sched=sched)
    X1, X1b, xh1, rs1 = mm(
        merged, G["o"], "nn", name=f"oproj_ln_{tag}", tn=D, tk=1024, out_defs=_resid_ln_outs(), epilogue=_ep_resid_ln,
        extras=((X, "mn", 0), (S["ln1_g"], "row", 0), (S["ln1_b"], "row", 0)),
    )
    kv = mm(memb, G["xkv"], "nn", name=f"xkv_{tag}", out_defs=(("mn", BF16),))
    X2, X2b, xh2, rs2, q_b, o_b = _xattn_fwd(d, X1, X1b, kv, G["xq"], G["xo"], S["ln2_g"], S["ln2_b"], name=f"xattn_fwd_{tag}", sched=sched)
    H, A = mm(X2b, G["up"], "nt", name=f"up_{tag}", tm=1024, out_defs=(("mn", F32), ("mn", BF16)), epilogue=_ep_relu2)
    X3, X3b, xh3, rs3 = mm(
        A, G["down"], "nn", name=f"down_ln_{tag}", tn=D, tk=1024, out_defs=_resid_ln_outs(), epilogue=_ep_resid_ln,
        extras=((X2, "mn", 0), (S["ln3_g"], "row", 0), (S["ln3_b"], "row", 0)),
    )
    saved = dict(Xb=Xb, P=P, sgu=sgu, att=att, ya=ya, yb=yb, merged=merged, X1b=X1b, xh1=xh1, rs1=rs1, kv=kv, q_b=q_b,
                 o_b=o_b, X2b=X2b, xh2=xh2, rs2=rs2, H=H, A=A, xh3=xh3, rs3=rs3)
    return X3, X3b, saved


def _layer_bwd(d, dXout, sv, memb, G, S, cos2, sin2, tag, sched=None, on_grad=None):
    D = d.D
    mm = functools.partial(_matmul, sched=sched)
    emit = on_grad if on_grad is not None else (lambda n, g: None)
    _, _, _, _, off_ga, off_gb = _offsets(d)
    bf = (("mn", BF16),)
    dz3, dz3b, dg3, db3 = _ln_bwd(dXout, sv["xh3"], sv["rs3"], S["ln3_g"], name=f"ln3_bwd_{tag}", sched=sched)
    dHb = mm(dz3b, G["down"], "nt", name=f"down_dx_{tag}", tm=1024, out_defs=bf, epilogue=_ep_relu2_bwd, extras=((sv["H"], "mn", 0),))
    g_down = mm(sv["A"], dz3b, "tn", name=f"down_dw_{tag}", tn=D)
    emit("down", g_down)
    dX2o = mm(dHb, G["up"], "nn", name=f"up_dx_{tag}", tn=D, epilogue=_ep_resid, extras=((dz3, "mn", 0),))
    g_up = mm(dHb, sv["X2b"], "tn", name=f"up_dw_{tag}", tn=D)
    emit("up", g_up)
    dz2, dz2b, dg2, db2 = _ln_bwd(dX2o, sv["xh2"], sv["rs2"], S["ln2_g"], name=f"ln2_bwd_{tag}", sched=sched)
    dX1o, dq_b, dkv = _xattn_bwd(d, dz2, dz2b, sv["q_b"], sv["kv"], G["xq"], G["xo"], name=f"xattn_bwd_{tag}", sched=sched)
    g_xo = mm(dz2b, sv["o_b"], "tn", name=f"xo_dw_{tag}")
    emit("xo", g_xo)
    g_xq = mm(sv["X1b"], dq_b, "tn", name=f"xq_dw_{tag}")
    emit("xq", g_xq)
    dkv_b = dkv.transpose(1, 0, 2).reshape(d.MEM, 2 * d.XW).astype(BF16)
    g_xkv = mm(memb, dkv_b, "tn", name=f"xkv_dw_{tag}")
    emit("xkv", g_xkv)
    dz1, dz1b, dg1, db1 = _ln_bwd(dX1o, sv["xh1"], sv["rs1"], S["ln1_g"], name=f"ln1_bwd_{tag}", sched=sched)
    dya_b, dyb_b, dga_b, dgb_b, dbga, dbgb = mm(
        dz1b, G["o"], "nt", name=f"oproj_dx_{tag}", tm=1024, tn=512,
        out_defs=(("mn", BF16), ("mn", BF16), ("mn", BF16), ("mn", BF16), ("acc", F32), ("acc", F32)),
        epilogue=_ep_gate_bwd,
        extras=((sv["P"], "mn", off_ga), (sv["P"], "mn", off_gb), (sv["ya"], "mn", 0), (sv["yb"], "mn", 0),
                (S["b_gate"], "row", 0), (S["b_gate"], "row", D)),
    )
    g_o = mm(sv["merged"], dz1b, "tn", name=f"oproj_dw_{tag}", tn=D)
    emit("o", g_o)
    duv_b, dwm, dbsT, dlvg, dlvb = _gmlp_bwd(d, sv["P"], dya_b, G["a"], S["ln_v_g"], S["ln_v_b"], S["w_s"], S["b_sT"], name=f"gmlp_bwd_{tag}", sched=sched)
    g_a = mm(dya_b, sv["sgu"], "tn", name=f"bra_dw_{tag}")
    emit("a", g_a)
    dqr_b, dkvr_b, dsk = _swa_bwd(d, sv["P"], dyb_b, G["b"], cos2, sin2, S["sinks"], name=f"swa_bwd_{tag}", sched=sched)
    g_b = mm(dyb_b, sv["att"], "tn", name=f"brb_dw_{tag}")
    emit("b", g_b)
    dP = jnp.concatenate([duv_b, dqr_b, dkvr_b, dga_b, dgb_b], axis=1)
    g_in = mm(dP, sv["Xb"], "tn", name=f"proj_dw_{tag}", tn=D)
    emit("in", g_in)
    dXin = mm(dP, G["in"], "nn", name=f"proj_dx_{tag}", tn=D, tk=_tile(d.IN, 1920), epilogue=_ep_resid, extras=((dz1, "mn", 0),))
    big = {"in": g_in, "a": g_a, "b": g_b, "o": g_o, "xq": g_xq, "xkv": g_xkv, "xo": g_xo, "up": g_up, "down": g_down}
    small = dict(b_gate=jnp.concatenate([dbga, dbgb], axis=1), ln_v_g=dlvg, ln_v_b=dlvb, w_s=dwm, b_sT=dbsT, sinks=dsk,
                 ln1_g=dg1, ln1_b=db1, ln2_g=dg2, ln2_b=db2, ln3_g=dg3, ln3_b=db3)
    return dXin, big, small


def _rope_tables(T):
    inv = 1.0 / (ROPE_THETA ** (jnp.arange(0, HEAD_DIM, 2, dtype=F32) / HEAD_DIM))
    ang = jnp.arange(T, dtype=F32)[:, None] * inv[None, :]
    cos, sin = jnp.cos(ang), jnp.sin(ang)
    reps = LANES // HEAD_DIM
    return jnp.concatenate([cos, cos] * reps, axis=1), jnp.concatenate([-sin, sin] * reps, axis=1)


def _local_step(d, x, mem, target, Gs, Ss, sched=None, on_grad=None):
    cos2, sin2 = _rope_tables(d.T)
    memb = mem.astype(BF16)
    X, Xb = x, x.astype(BF16)
    saved = []
    for l in range(DEPTH):
        X, Xb, sv = _layer_fwd(d, X, Xb, memb, Gs[l], Ss[l], cos2, sin2, f"l{l}", sched)
        saved.append(sv)
    loss, dX = _loss_head(X, target, name="loss_head", sched=sched)
    bigs, smalls = [None] * DEPTH, [None] * DEPTH
    for l in reversed(range(DEPTH)):
        emit = None if on_grad is None else functools.partial(on_grad, l)
        dX, bigs[l], smalls[l] = _layer_bwd(d, dX, saved[l], memb, Gs[l], Ss[l], cos2, sin2, f"l{l}", sched, emit)
    return loss, dX, bigs, smalls


def _place():
    x, y, c = lax.axis_index("x"), lax.axis_index("y"), lax.axis_index("c")
    return x, y, c, [(1 - x, y), (x, 1 - y), (1 - x, 1 - y)]


def _all_gather(shards, *, name, sched=None):
    n = len(shards)
    any_spec = pl.BlockSpec(memory_space=pl.ANY)

    def body(*refs):
        ins, outs = refs[:n], refs[n : 2 * n]
        send_sems, recv_sems, local_sems = refs[2 * n :]
        x, y, c, chips = _place()
        me, sibling = (x, y, c), (x, y, 1 - c)

        def rows(w, p):
            r = ins[w].shape[0]
            return outs[w].at[pl.ds((4 * p[0] + 2 * p[1] + p[2]) * r, r), :]

        def copy(w, k, block, to, src=None):
            return pltpu.make_async_remote_copy(
                src_ref=rows(w, block) if src is None else src, dst_ref=rows(w, block),
                send_sem=send_sems.at[7 * w + k], recv_sem=recv_sems.at[7 * w + k], device_id=to, device_id_type=MESH)

        mine = [pltpu.make_async_copy(ins[w], rows(w, me), local_sems.at[w]) for w in range(n)]
        for cp in mine:
            cp.start()
        first = []
        for w in range(n):
            first.append(copy(w, 0, me, sibling, src=ins[w]))
            first += [copy(w, 1 + j, me, (*chip, c), src=ins[w]) for j, chip in enumerate(chips)]
        for cp in first:
            cp.start()
        passed = []
        for j, chip in enumerate(chips):
            for w in range(n):
                copy(w, 1 + j, (*chip, c), me).wait_recv()
                fwd = copy(w, 4 + j, (*chip, c), sibling)
                fwd.start()
                passed.append(fwd)
        for w in range(n):
            copy(w, 0, sibling, me).wait_recv()
            for j, chip in enumerate(chips):
                copy(w, 4 + j, (*chip, 1 - c), me).wait_recv()
        for cp in first + passed:
            cp.wait_send()
        for cp in mine:
            cp.wait()

    return _hosted(
        body,
        name=name,
        sched=sched,
        in_specs=[any_spec] * n,
        out_specs=[any_spec] * n,
        out_shape=[jax.ShapeDtypeStruct((N_DEV * s.shape[0], s.shape[1]), s.dtype) for s in shards],
        scratch_shapes=[pltpu.SemaphoreType.DMA((7 * n,)), pltpu.SemaphoreType.DMA((7 * n,)), pltpu.SemaphoreType.DMA((n,))],
    )(*shards)


class _Task:
    def __init__(self, ins, out_shapes, n_remote, n_local, plan, done, aliases=None):
        self.ins, self.out_shapes, self.n_remote, self.n_local = list(ins), list(out_shapes), n_remote, n_local
        self.plan, self.done, self.aliases = plan, done, dict(aliases or {})


class _Sched:
    def __init__(self):
        self.pending = {}

    def at(self, host, task):
        self.pending.setdefault(host, []).append(task)

    def take(self, host):
        return self.pending.pop(host, [])


def _hosted(body, *, name, sched=None, tasks=(), grid=(), in_specs=None, out_specs=None, out_shape=(), scratch_shapes=(),
            compiler_params=None, input_output_aliases=None):
    tasks = list(tasks) + (sched.take(name) if sched is not None else [])
    scratch_shapes = list(scratch_shapes)
    kwargs = dict(name=name)
    core_aliases = dict(input_output_aliases or {})
    if grid:
        kwargs["grid"] = grid
    if compiler_params is not None:
        kwargs["compiler_params"] = compiler_params
    if not tasks:
        if in_specs is not None:
            kwargs.update(in_specs=in_specs, out_specs=out_specs)
        return pl.pallas_call(body, out_shape=out_shape, scratch_shapes=scratch_shapes, input_output_aliases=core_aliases, **kwargs)
    assert in_specs is not None and out_specs is not None, name
    single = not isinstance(out_shape, (list, tuple))
    core_shape = [out_shape] if single else list(out_shape)
    core_specs = [out_specs] if single else list(out_specs)
    in_specs = list(in_specs)
    n_in, n_out, n_scr = len(in_specs), len(core_shape), len(scratch_shapes)
    t_ins = [a for t in tasks for a in t.ins]
    t_outs = [s for t in tasks for s in t.out_shapes]
    n_rem = sum(t.n_remote for t in tasks)
    n_loc = sum(t.n_local for t in tasks)
    aliases, pos_in, pos_out = dict(core_aliases), n_in, n_out
    for t in tasks:
        for a, b in t.aliases.items():
            aliases[pos_in + a] = pos_out + b
        pos_in += len(t.ins)
        pos_out += len(t.out_shapes)
    any_spec = pl.BlockSpec(memory_space=pl.ANY)

    def wrapped(*refs):
        core_in, t_in = refs[:n_in], refs[n_in : n_in + len(t_ins)]
        o0 = n_in + len(t_ins)
        core_out, t_out = refs[o0 : o0 + n_out], refs[o0 + n_out : o0 + n_out + len(t_outs)]
        s0 = o0 + n_out + len(t_outs)
        core_scr = refs[s0 : s0 + n_scr]
        send_sems, recv_sems, local_sems = refs[s0 + n_scr :]
        remote, local = [], []
        pi = po = pr = pq = 0
        for t in tasks:
            r, q = t.plan(t_in[pi : pi + len(t.ins)], t_out[po : po + len(t.out_shapes)], send_sems, recv_sems, local_sems, pr, pq)
            assert len(r) == t.n_remote and len(q) == t.n_local
            remote += r
            local += q
            pi, po, pr, pq = pi + len(t.ins), po + len(t.out_shapes), pr + t.n_remote, pq + t.n_local

        def start():
            for cp in local + remote:
                cp.start()

        def finish():
            for cp in remote + local:
                cp.wait()

        if grid:
            ids = [pl.program_id(a) for a in range(len(grid))]
            first = functools.reduce(jnp.logical_and, [i == 0 for i in ids])
            last = functools.reduce(jnp.logical_and, [i == g - 1 for i, g in zip(ids, grid)])
            pl.when(first)(start)
            body(*core_in, *core_out, *core_scr)
            pl.when(last)(finish)
        else:
            start()
            body(*core_in, *core_out, *core_scr)
            finish()

    call = pl.pallas_call(
        wrapped,
        in_specs=in_specs + [any_spec] * len(t_ins),
        out_specs=core_specs + [any_spec] * len(t_outs),
        out_shape=core_shape + t_outs,
        scratch_shapes=scratch_shapes
        + [pltpu.SemaphoreType.DMA((n_rem,)), pltpu.SemaphoreType.DMA((n_rem,)), pltpu.SemaphoreType.DMA((max(n_loc, 1),))],
        input_output_aliases=aliases,
        **kwargs,
    )

    def run(*operands):
        outs = call(*operands, *t_ins)
        po = n_out
        for t in tasks:
            t.done(list(outs[po : po + len(t.out_shapes)]))
            po += len(t.out_shapes)
        return outs[0] if single else list(outs[:n_out])

    return run


def _comm_only(tasks, *, name):
    _hosted(lambda: None, name=name, tasks=tasks, in_specs=[], out_shape=[], out_specs=[])()


def _rows(ref, block, n_blocks):
    r = ref.shape[0] // n_blocks
    return ref.at[pl.ds(block * r, r), :]


def _remote(src, dst, send_sems, recv_sems, k, to):
    return pltpu.make_async_remote_copy(src_ref=src, dst_ref=dst, send_sem=send_sems.at[k], recv_sem=recv_sems.at[k],
                                        device_id=to, device_id_type=MESH)


def _gather_stage1(shards, done, part=(0, 1), into=None):
    n = len(shards)
    k, n_parts = part

    def plan(ins, outs, send_sems, recv_sems, local_sems, pr, pq):
        x, y, c, chips = _place()
        mine = 4 * x + 2 * y + c
        remote, local = [], []
        for w in range(n):
            r = shards[w].shape[0]
            rp = r // n_parts
            src = ins[w].at[pl.ds(k * rp, rp), :]
            dst = outs[w].at[pl.ds(mine * r + k * rp, rp), :]
            local.append(pltpu.make_async_copy(src, dst, local_sems.at[pq + w]))
            for j, to in enumerate([(x, y, 1 - c)] + [(*chip, c) for chip in chips]):
                remote.append(_remote(src, dst, send_sems, recv_sems, pr + 4 * w + j, to))
        return remote, local

    shapes = [jax.ShapeDtypeStruct((N_DEV * s.shape[0], s.shape[1]), s.dtype) for s in shards]
    if into is None:
        return _Task(shards, shapes, 4 * n, n, plan, done)
    return _Task(list(shards) + list(into), shapes, 4 * n, n, plan, done, aliases={n + w: w for w in range(n)})


def _gather_stage2(partial, done):
    n = len(partial)

    def plan(ins, outs, send_sems, recv_sems, local_sems, pr, pq):
        x, y, c, chips = _place()
        remote = []
        for w in range(n):
            for j, chip in enumerate(chips):
                rows = _rows(outs[w], 4 * chip[0] + 2 * chip[1] + c, N_DEV)
                remote.append(_remote(rows, rows, send_sems, recv_sems, pr + 3 * w + j, (x, y, 1 - c)))
        return remote, []

    shapes = [jax.ShapeDtypeStruct(p.shape, p.dtype) for p in partial]
    return _Task(partial, shapes, 3 * n, 0, plan, done, aliases={w: w for w in range(n)})


def _sibling_stage(grads, done):
    n = len(grads)

    def plan(ins, outs, send_sems, recv_sems, local_sems, pr, pq):
        x, y, c, _ = _place()
        remote = []
        for w in range(n):
            for k in range(4):
                src = _rows(ins[w], 4 * (k // 2) + 2 * (k % 2) + (1 - c), N_DEV)
                remote.append(_remote(src, _rows(outs[w], k, 4), send_sems, recv_sems, pr + 4 * w + k, (x, y, 1 - c)))
        return remote, []

    shapes = [jax.ShapeDtypeStruct((g.shape[0] // 2, g.shape[1]), g.dtype) for g in grads]
    return _Task(grads, shapes, 4 * n, 0, plan, done)


def _chips_stage(sends, done, part=(0, 1), into=None):
    n = len(sends)
    k, n_parts = part

    def plan(ins, outs, send_sems, recv_sems, local_sems, pr, pq):
        x, y, c, chips = _place()
        remote = []
        for w in range(n):
            r = sends[w].shape[0] // 3
            rp = r // n_parts
            for j, chip in enumerate(chips):
                rows = pl.ds(j * r + k * rp, rp)
                remote.append(_remote(ins[w].at[rows, :], outs[w].at[rows, :], send_sems, recv_sems, pr + 3 * w + j, (*chip, c)))
        return remote, []

    shapes = [jax.ShapeDtypeStruct(s.shape, s.dtype) for s in sends]
    if into is None:
        return _Task(sends, shapes, 3 * n, 0, plan, done)
    return _Task(list(sends) + list(into), shapes, 3 * n, 0, plan, done, aliases={n + w: w for w in range(n)})


def _pair_sum(grad, got, place, *, name):
    R, C = grad.shape
    r = R // N_DEV
    tr = _tile(r, 256, 16)
    nt = r // tr

    def chip_of(s, pr):
        fx = jnp.where((s == 1) | (s == 3), 1, 0)
        fy = jnp.where(s >= 2, 1, 0)
        return pr[0] ^ fx, pr[1] ^ fy

    def grad_map(s, t, pr):
        kx, ky = chip_of(s, pr)
        return ((4 * kx + 2 * ky + pr[2]) * nt + t, 0)

    def got_map(s, t, pr):
        kx, ky = chip_of(s, pr)
        return ((2 * kx + ky) * nt + t, 0)

    def body(pr, g_ref, r_ref, own_ref, send_ref):
        s = pl.program_id(0)
        tot = g_ref[...] + r_ref[...]

        @pl.when(s == 0)
        def _():
            own_ref[...] = tot

        @pl.when(s > 0)
        def _():
            send_ref[...] = tot.astype(BF16)

    return pl.pallas_call(
        body,
        name=name,
        grid_spec=pltpu.PrefetchScalarGridSpec(
            num_scalar_prefetch=1,
            grid=(4, nt),
            in_specs=[pl.BlockSpec((tr, C), grad_map), pl.BlockSpec((tr, C), got_map)],
            out_specs=[
                pl.BlockSpec((tr, C), lambda s, t, pr: (jnp.where(s == 0, t, nt - 1), 0)),
                pl.BlockSpec((tr, C), lambda s, t, pr: (jnp.where(s == 0, 0, (s - 1) * nt + t), 0)),
            ],
        ),
        out_shape=[jax.ShapeDtypeStruct((r, C), F32), jax.ShapeDtypeStruct((3 * r, C), BF16)],
        compiler_params=_params(("arbitrary", "arbitrary"), [((tr, C), F32)] * 4),
    )(place, grad, got)


def _adam_vals(w, g, m, v):
    m = ADAM_B1 * m + (1.0 - ADAM_B1) * g
    v = ADAM_B2 * v + (1.0 - ADAM_B2) * (g * g)
    m_hat = m / (1.0 - ADAM_B1**ADAM_STEP)
    v_hat = v / (1.0 - ADAM_B2**ADAM_STEP)
    delta = -ADAM_LR * (m_hat / (jnp.sqrt(v_hat) + ADAM_EPS) + ADAM_WD * w)
    return delta, m, v


def _adam_big(w3, m3, v3, own, got, layer, transposed, prev, *, name, sched=None):
    _, A, B = w3.shape
    ta = _tile(A, 256)
    nt = A // ta
    b_pad = (-B) % LANES

    def body(*refs):
        w_ref, m_ref, v_ref, own_ref, g1_ref, g2_ref, g3_ref = refs[:7]
        g_ref, d_ref, nm_ref, nv_ref = refs[-4:]
        g = ((own_ref[...] + g1_ref[...].astype(F32)) + g2_ref[...].astype(F32)) + g3_ref[...].astype(F32)
        if transposed:
            if b_pad:
                g = jnp.concatenate([g, jnp.zeros((b_pad, ta), F32)], axis=0)
            g = g.T[:, :B]
        g_ref[...] = g
        d_ref[...], nm_ref[...], nv_ref[...] = _adam_vals(w_ref[...], g, m_ref[...], v_ref[...])

    nat = pl.BlockSpec((None, ta, B), lambda t: (layer, t, 0))
    if transposed:
        parts = [pl.BlockSpec((B, ta), lambda t: (0, t))] + [pl.BlockSpec((B, ta), lambda t, j=j: (j, t)) for j in range(3)]
    else:
        parts = [pl.BlockSpec((ta, B), lambda t: (t, 0))] + [pl.BlockSpec((ta, B), lambda t, j=j: (j * nt + t, 0)) for j in range(3)]
    keep = [] if prev is None else list(prev)
    outs = _hosted(
        body,
        name=name,
        sched=sched,
        grid=(nt,),
        in_specs=[nat, nat, nat] + parts + [pl.BlockSpec(memory_space=pl.ANY)] * len(keep),
        out_specs=[nat] * 4,
        out_shape=[jax.ShapeDtypeStruct(w3.shape, F32)] * 4,
        input_output_aliases={7 + k: k for k in range(len(keep))},
        compiler_params=_params(("arbitrary",), [((ta, B), F32)] * 10),
    )(w3, m3, v3, own, got, got, got, *keep)
    return list(outs)


def _adam_small(w, m, v, parts, *, name, sched=None):
    R = w.shape[0]

    def body(w_ref, m_ref, v_ref, p_ref, g_ref, d_ref, nm_ref, nv_ref):
        g = p_ref[pl.ds(0, R), :]
        for k in range(1, N_DEV):
            g = g + p_ref[pl.ds(k * R, R), :]
        g_ref[...] = g
        d_ref[...], nm_ref[...], nv_ref[...] = _adam_vals(w_ref[...], g, m_ref[...], v_ref[...])

    return _hosted(
        body,
        name=name,
        sched=sched,
        out_shape=[jax.ShapeDtypeStruct((R, LANES), F32)] * 4,
        compiler_params=_params(None, [((R, LANES), F32)] * 16),
    )(w, m, v, parts)


SMALL_NAMES = ("b_gate", "ln_v_g", "ln_v_b", "w_s", "b_s", "sinks", "ln1_g", "ln1_b", "ln2_g", "ln2_b", "ln3_g", "ln3_b")
PACK_ALIGN = 8 * LANES


def _pack(arrays):
    flat = []
    for a in arrays:
        a = a.reshape(-1)
        flat.append(jnp.pad(a, (0, (-a.shape[0]) % PACK_ALIGN)))
    return jnp.concatenate(flat).reshape(-1, LANES)


def _unpack(packed, shapes):
    flat = packed.reshape(-1)
    out, pos = [], 0
    for s in shapes:
        n = 1
        for e in s:
            n *= e
        out.append(flat[pos : pos + n].reshape(s))
        pos += n + (-n) % PACK_ALIGN
    return out


BIG = (("in", "w_in", True), ("a", "w_br_a", True), ("b", "w_br_b", True), ("o", "w_o", False), ("xq", "w_xq", False),
       ("xkv", "w_xkv", False), ("xo", "w_xo", True), ("up", "w_up", True), ("down", "w_down", False))


SMALL_BIG = ("a", "b", "o", "xq", "xkv", "xo")
GATHER_PLAN = (
    (0, ("in",), None, None),
    (0, SMALL_BIG, ("proj_l0",), "gmlp_fwd_l0"),
    (0, ("up",), ("swa_fwd_l0", "merge_l0"), "oproj_ln_l0"),
    (0, ("down",), ("oproj_ln_l0", "xattn_fwd_l0"), "up_l0"),
    (1, ("in",), ("up_l0",), "down_ln_l0"),
    (1, SMALL_BIG, ("down_ln_l0",), "proj_l1"),
    (1, ("up",), ("proj_l1",), "gmlp_fwd_l1"),
    (1, ("down",), ("swa_fwd_l1", "merge_l1"), "oproj_ln_l1"),
)
GRAD_HOSTS = {
    1: {"down": ("xattn_bwd_l1", "proj_dw_l1"), "up": ("xattn_bwd_l1", "proj_dx_l1"),
        **{g: ("proj_dw_l1", "down_dx_l0") for g in SMALL_BIG}, "in": ("proj_dx_l1", "up_dx_l0")},
    0: {"down": ("xattn_bwd_l0", "swa_bwd_l0"), "up": ("xattn_bwd_l0", "proj_dw_l0"),
        **{g: ("proj_dw_l0", "proj_dx_l0") for g in SMALL_BIG},
        "in": ("proj_dx_l0", ("adamw_up_l1", "adamw_down_l1", "adamw_in_l1"))},
}
ADAM_ORDER = [(1, "up"), (1, "down"), (1, "in")] + [(1, g) for g in SMALL_BIG] + [(0, "up"), (0, "down")] + [(0, g) for g in SMALL_BIG] + [(0, "in")]


def kernel(x, mem, w_in, b_gate, ln_v_g, ln_v_b, w_s, b_s, sinks, w_br_a, w_br_b, w_o, ln1_g, ln1_b, w_xq, w_xkv, w_xo, ln2_g, ln2_b, w_up, w_down, ln3_g, ln3_b, loss_target, m_w_in, m_b_gate, m_ln_v_g, m_ln_v_b, m_w_s, m_b_s, m_sinks, m_w_br_a, m_w_br_b, m_w_o, m_ln1_g, m_ln1_b, m_w_xq, m_w_xkv, m_w_xo, m_ln2_g, m_ln2_b, m_w_up, m_w_down, m_ln3_g, m_ln3_b, v_w_in, v_b_gate, v_ln_v_g, v_ln_v_b, v_w_s, v_b_s, v_sinks, v_w_br_a, v_w_br_b, v_w_o, v_ln1_g, v_ln1_b, v_w_xq, v_w_xkv, v_w_xo, v_ln2_g, v_ln2_b, v_w_up, v_w_down, v_ln3_g, v_ln3_b):
    given = dict(locals())
    T, D = x.shape[1], x.shape[2]
    IN, GW, AW, XW, DFF = w_in.shape[2] * N_DEV, ln_v_g.shape[1], w_br_b.shape[1], w_xq.shape[2], w_up.shape[2] * N_DEV
    KVW = (IN - 2 * GW - AW - 2 * D) // 2
    d = Dims(T=T, D=D, IN=IN, GW=GW, G=GW // GROUP_DIM, AW=AW, KVW=KVW, HQ=AW // HEAD_DIM, HKV=KVW // HEAD_DIM, XW=XW,
             XH=XW // X_HEAD_DIM, MEM=mem.shape[1], DFF=DFF)
    place = jnp.stack([lax.axis_index("x"), lax.axis_index("y"), lax.axis_index("c")]).astype(jnp.int32)

    sched = _Sched()
    big_of = {g: (p, tr) for g, p, tr in BIG}

    def shard(l, g):
        p, tr = big_of[g]
        return (given[p][l].T if tr else given[p][l]).astype(BF16)

    Gs = [{}, {}]

    def plan_gather(l, names, hosts1, host2):
        shards = [shard(l, g) for g in names]
        if hosts1 is None:
            Gs[l].update(zip(names, _all_gather(shards, name=f"gather_{names[0]}_l{l}")))
            return

        def register(k, into):
            def done(outs):
                if k + 1 < len(hosts1):
                    register(k + 1, outs)
                else:
                    sched.at(host2, _gather_stage2(outs, lambda full: Gs[l].update(zip(names, full))))

            sched.at(hosts1[k], _gather_stage1(shards, done, part=(k, len(hosts1)), into=into))

        register(0, None)

    for entry in GATHER_PLAN:
        plan_gather(*entry)
    Ss = []
    for l in range(DEPTH):
        S = {n: given[n][l].reshape(1, -1) for n in SMALL_NAMES if n not in ("w_s", "b_s", "sinks")}
        S["w_s"], S["b_sT"], S["sinks"] = w_s[l], b_s[l].T, sinks[l]
        Ss.append(S)

    owns, recvs = {}, {}

    def on_grad(l, g, grad):
        sib_host, chips_host = GRAD_HOSTS[l][g]

        def after_sibling(got):
            owns[(l, g)], send = _pair_sum(grad, got[0], place, name=f"grad_pair_sum_{g}_l{l}")
            hosts = chips_host if isinstance(chips_host, tuple) else (chips_host,)

            def register(k, into):
                def done(r):
                    if k + 1 < len(hosts):
                        register(k + 1, r)
                    else:
                        recvs[(l, g)] = r[0]

                task = _chips_stage([send], done, part=(k, len(hosts)), into=into)
                if hosts[k] is None:
                    _comm_only([task], name=f"grad_chips_{g}_l{l}")
                else:
                    sched.at(hosts[k], task)

            register(0, None)

        task = _sibling_stage([grad], after_sibling)
        if sib_host is None:
            _comm_only([task], name=f"grad_sibling_{g}_l{l}")
        else:
            sched.at(sib_host, task)

    loss, dX, bigs, smalls = _local_step(d, x[0], mem[0], loss_target[0], Gs, Ss, sched, on_grad)
    loss = lax.psum(loss[0, 0], ("x", "y", "c"))

    res = {p: None for _, p, _ in BIG}
    for l, g in ADAM_ORDER:
        p, tr = big_of[g]
        res[p] = _adam_big(given[p], given["m_" + p], given["v_" + p], owns[(l, g)], recvs[(l, g)], l, tr, res[p],
                           name=f"adamw_{g}_l{l}", sched=sched)
    assert not sched.pending, sorted(sched.pending)

    def small_list(src, l):
        return [src[l][n] if n != "b_s" else src[l]["b_sT"].T for n in SMALL_NAMES]

    shapes = [given[n].shape[1:] for n in SMALL_NAMES]
    part = _pack([a if n != "sinks" else a[0, : d.HQ] for l in range(DEPTH) for n, a in zip(SMALL_NAMES, small_list(smalls, l))])
    parts = _all_gather([part], name="gather_small_grads")[0]
    pw, pm, pv = (_pack([given[pre + n][l] for l in range(DEPTH) for n in SMALL_NAMES]) for pre in ("", "m_", "v_"))
    small_out = [_unpack(o, shapes * DEPTH) for o in _adam_small(pw, pm, pv, parts, name="adamw_small")]

    names = ["w_in", "b_gate", "ln_v_g", "ln_v_b", "w_s", "b_s", "sinks", "w_br_a", "w_br_b", "w_o", "ln1_g", "ln1_b", "w_xq",
             "w_xkv", "w_xo", "ln2_g", "ln2_b", "w_up", "w_down", "ln3_g", "ln3_b"]
    out = [loss, dX[None]]
    for kind in range(4):
        for n in names:
            if n in SMALL_NAMES:
                i = SMALL_NAMES.index(n)
                out.append(jnp.stack([small_out[kind][l * len(SMALL_NAMES) + i] for l in range(DEPTH)]))
            else:
                out.append(res[n][kind])
    return tuple(out)
```

```python
import collections
import functools

import jax
import jax.numpy as jnp
from jax import lax
from jax.experimental import pallas as pl
from jax.experimental.pallas import tpu as pltpu

F32 = jnp.float32
BF16 = jnp.bfloat16
MESH = pl.DeviceIdType.MESH

N_DEV = 8
DEPTH = 2
CHUNK = 128
HEAD_DIM = 64
ROPE_HALF = HEAD_DIM // 2
X_HEAD_DIM = 128
GROUP_DIM = 128
LANES = 128
ROPE_THETA = 10000.0
LN_EPS = 1e-5
ALPHA = (2 * DEPTH) ** 0.25
ADAM_LR = 0.001
ADAM_B1 = 0.9
ADAM_B2 = 0.999
ADAM_EPS = 1e-08
ADAM_WD = 0.01
ADAM_STEP = 10
VMEM_BYTES = 64 * 1024 * 1024
VMEM_TEMP_BYTES = 20 * 1024 * 1024

Dims = collections.namedtuple("Dims", "T D IN GW G AW KVW HQ HKV XW XH MEM DFF")


def _offsets(d):
    off_v = d.GW
    off_q = 2 * d.GW
    off_k = off_q + d.AW
    off_va = off_k + d.KVW
    off_ga = off_va + d.KVW
    off_gb = off_ga + d.D
    return off_v, off_q, off_k, off_va, off_ga, off_gb


def _tile(dim, pref, align=LANES):
    if dim <= pref:
        return dim
    t = pref - pref % align
    while t >= align:
        if dim % t == 0:
            return t
        t -= align
    return dim


def _nbytes(shape, dtype):
    n = 1
    for s in shape:
        n *= s
    return n * jnp.dtype(dtype).itemsize


def _params(sem, blocks, scratch=0):
    need = 2 * sum(_nbytes(s, t) for s, t in blocks) + scratch + VMEM_TEMP_BYTES
    limit = min(max(need, 32 * 1024 * 1024), VMEM_BYTES - 4 * 1024 * 1024)
    return pltpu.CompilerParams(dimension_semantics=sem, vmem_limit_bytes=limit)


def _dot(a, b, kind):
    dn = {"nn": (((1,), (0,)), ((), ())), "nt": (((1,), (1,)), ((), ())), "tn": (((0,), (0,)), ((), ()))}[kind]
    return lax.dot_general(a, b, dn, preferred_element_type=F32)


def _gelu(x):
    c = 0.7978845608028654
    t = jnp.tanh(c * (x + 0.044715 * (x * x * x)))
    return 0.5 * x * (1.0 + t)


def _gelu_grad(x):
    c = 0.7978845608028654
    x2 = x * x
    t = jnp.tanh(c * (x + 0.044715 * (x2 * x)))
    return 0.5 * (1.0 + t) + 0.5 * x * (1.0 - t * t) * (c * (1.0 + 3.0 * 0.044715 * x2))


def _ln_fwd(z, g, b):
    mu = jnp.mean(z, axis=-1, keepdims=True)
    zc = z - mu
    var = jnp.mean(zc * zc, axis=-1, keepdims=True)
    rstd = lax.rsqrt(var + LN_EPS)
    xhat = zc * rstd
    return xhat * g + b, xhat, rstd


def _ln_bwd_vals(dy, xhat, rstd, g):
    gd = dy * g
    m1 = jnp.mean(gd, axis=-1, keepdims=True)
    m2 = jnp.mean(gd * xhat, axis=-1, keepdims=True)
    return rstd * (gd - m1 - xhat * m2)


def _acc_store(ref, val, first):
    @pl.when(first)
    def _():
        ref[...] = val

    @pl.when(jnp.logical_not(first))
    def _():
        ref[...] += val


def _matmul(a, b, kind, *, name, tm=512, tn=1024, tk=2048, out_defs=(("mn", F32),), epilogue=None, extras=(), sched=None):
    if kind == "nn":
        (M, K), (K2, N) = a.shape, b.shape
    elif kind == "nt":
        (M, K), (N, K2) = a.shape, b.shape
    else:
        (K, M), (K2, N) = a.shape, b.shape
    assert K == K2, (a.shape, b.shape, kind)
    tm, tn, tk = _tile(M, tm), _tile(N, tn), _tile(K, tk)
    nk = K // tk
    blocks = []

    def spec(shape, imap, dtype):
        blocks.append((shape, dtype))
        return pl.BlockSpec(shape, imap)

    if kind == "tn":
        a_spec = spec((tk, tm), lambda j, i, k: (k, i), a.dtype)
    else:
        a_spec = spec((tm, tk), lambda j, i, k: (i, k), a.dtype)
    if kind == "nt":
        b_spec = spec((tn, tk), lambda j, i, k: (j, k), b.dtype)
    else:
        b_spec = spec((tk, tn), lambda j, i, k: (k, j), b.dtype)
    in_specs = [a_spec, b_spec]
    operands = [a, b]
    for arr, ekind, off in extras:
        if ekind == "mn":
            assert off % tn == 0
            in_specs.append(spec((tm, tn), lambda j, i, k, o=off // tn: (i, j + o), arr.dtype))
        elif ekind == "row":
            assert off % tn == 0
            in_specs.append(spec((1, tn), lambda j, i, k, o=off // tn: (0, j + o), arr.dtype))
        else:
            in_specs.append(spec((tm, 1), lambda j, i, k: (i, 0), arr.dtype))
        operands.append(arr)
    out_shape, out_specs = [], []
    for okind, dt in out_defs:
        if okind == "mn":
            out_shape.append(jax.ShapeDtypeStruct((M, N), dt))
            out_specs.append(spec((tm, tn), lambda j, i, k: (i, j), dt))
        elif okind == "m1":
            assert N == tn
            out_shape.append(jax.ShapeDtypeStruct((M, 1), dt))
            out_specs.append(spec((tm, 1), lambda j, i, k: (i, 0), dt))
        else:
            out_shape.append(jax.ShapeDtypeStruct((1, N), F32))
            out_specs.append(spec((1, tn), lambda j, i, k: (0, j), F32))
    n_ex, n_out = len(extras), len(out_defs)
    ep = epilogue if epilogue is not None else (lambda acc: (acc,))

    def body(*refs):
        a_ref, b_ref = refs[0], refs[1]
        ex_refs = refs[2 : 2 + n_ex]
        out_refs = refs[2 + n_ex : 2 + n_ex + n_out]
        i = pl.program_id(1)
        k = pl.program_id(2)
        part = _dot(a_ref[...], b_ref[...], kind)

        def finish(acc):
            vals = ep(acc, *[r[...] for r in ex_refs])
            for (okind, dt), r, v in zip(out_defs, out_refs, vals):
                if okind == "acc":
                    _acc_store(r, v, i == 0)
                else:
                    r[...] = v.astype(dt)

        if nk == 1:
            finish(part)
        else:
            acc_ref = refs[-1]

            @pl.when(k == 0)
            def _():
                acc_ref[...] = part

            @pl.when(k > 0)
            def _():
                acc_ref[...] += part

            @pl.when(k == nk - 1)
            def _():
                finish(acc_ref[...])

    scratch = [pltpu.VMEM((tm, tn), F32)] if nk > 1 else []
    outs = _hosted(
        body,
        name=name,
        sched=sched,
        grid=(N // tn, M // tm, nk),
        in_specs=in_specs,
        out_specs=out_specs,
        out_shape=out_shape,
        scratch_shapes=scratch,
        compiler_params=_params(("arbitrary", "arbitrary", "arbitrary"), blocks, _nbytes((tm, tn), F32) if nk > 1 else 0),
    )(*operands)
    return outs if len(outs) > 1 else outs[0]


def _ep_resid_ln(acc, resid, g, b):
    y, xhat, rstd = _ln_fwd(ALPHA * resid + acc, g, b)
    return y, y, xhat, rstd


def _ep_resid(acc, resid):
    return (ALPHA * resid + acc,)


def _ep_relu2(acc):
    r = jnp.maximum(acc, 0.0)
    return acc, r * r


def _ep_relu2_bwd(acc, h):
    return (acc * (2.0 * jnp.maximum(h, 0.0)),)


def _ep_gate_bwd(acc, ga, gb, ya, yb, bga, bgb):
    sa = jax.nn.sigmoid(ga + bga)
    sb = jax.nn.sigmoid(gb + bgb)
    dga = acc * ya * (sa * (1.0 - sa))
    dgb = acc * yb * (sb * (1.0 - sb))
    return (acc * sa, acc * sb, dga, dgb, jnp.sum(dga, axis=0, keepdims=True), jnp.sum(dgb, axis=0, keepdims=True))


def _ln_bwd(dy, xhat, rstd, g, *, name, sched=None):
    T, D = dy.shape
    tm = _tile(T, 256)

    def body(dy_ref, xh_ref, rs_ref, g_ref, dz_ref, dzb_ref, dg_ref, db_ref):
        i = pl.program_id(0)
        dyv, xh = dy_ref[...], xh_ref[...]
        dz = _ln_bwd_vals(dyv, xh, rs_ref[...], g_ref[...])
        dz_ref[...] = dz
        dzb_ref[...] = dz.astype(BF16)
        _acc_store(dg_ref, jnp.sum(dyv * xh, axis=0, keepdims=True), i == 0)
        _acc_store(db_ref, jnp.sum(dyv, axis=0, keepdims=True), i == 0)

    row = pl.BlockSpec((tm, D), lambda i: (i, 0))
    vec = pl.BlockSpec((1, D), lambda i: (0, 0))
    return _hosted(
        body,
        name=name,
        sched=sched,
        grid=(T // tm,),
        in_specs=[row, row, pl.BlockSpec((tm, 1), lambda i: (i, 0)), vec],
        out_specs=[row, row, vec, vec],
        out_shape=[
            jax.ShapeDtypeStruct((T, D), F32),
            jax.ShapeDtypeStruct((T, D), BF16),
            jax.ShapeDtypeStruct((1, D), F32),
            jax.ShapeDtypeStruct((1, D), F32),
        ],
        compiler_params=_params(("arbitrary",), [((tm, D), F32)] * 4),
    )(dy, xhat, rstd, g)


def _loss_head(y, target, *, name, sched=None):
    T, D = y.shape
    tm = _tile(T, 256)

    def body(y_ref, t_ref, loss_ref, dy_ref):
        err = y_ref[...] - t_ref[...]
        dy_ref[...] = err * (1.0 / D)
        part = 0.5 * jnp.sum(jnp.sum(err * err, axis=1, keepdims=True) * (1.0 / D), axis=0, keepdims=True)
        _acc_store(loss_ref, part, pl.program_id(0) == 0)

    row = pl.BlockSpec((tm, D), lambda i: (i, 0))
    return _hosted(
        body,
        name=name,
        sched=sched,
        grid=(T // tm,),
        in_specs=[row, row],
        out_specs=[pl.BlockSpec((1, 1), lambda i: (0, 0)), row],
        out_shape=[jax.ShapeDtypeStruct((1, 1), F32), jax.ShapeDtypeStruct((T, D), F32)],
        compiler_params=_params(("arbitrary",), [((tm, D), F32)] * 3),
    )(y, target)


def _masked_mix_weights(wm_ref, G):
    r = lax.broadcasted_iota(jnp.int32, (CHUNK, CHUNK), 0)
    c = lax.broadcasted_iota(jnp.int32, (CHUNK, CHUNK), 1)
    return [jnp.where(c <= r, wm_ref[g], 0.0).astype(BF16) for g in range(G)]


def _gmlp_fwd(d, P, lnv_g, lnv_b, wm, bsT, *, name, sched=None):
    T, GW, G = d.T, d.GW, d.G
    tm = _tile(T, 256)
    nc = tm // CHUNK

    def body(u_ref, v_ref, g_ref, b_ref, wm_ref, bs_ref, o_ref):
        gu = _gelu(u_ref[...])
        vn, _, _ = _ln_fwd(_gelu(v_ref[...]), g_ref[...], b_ref[...])
        vn = vn.astype(BF16)
        wl = _masked_mix_weights(wm_ref, G)
        for c in range(nc):
            rows = slice(c * CHUNK, (c + 1) * CHUNK)
            for g in range(G):
                cols = slice(g * GROUP_DIM, (g + 1) * GROUP_DIM)
                mixed = _dot(wl[g], vn[rows, cols], "nn") + bs_ref[:, g : g + 1]
                o_ref[rows, cols] = (gu[rows, cols] * mixed).astype(BF16)

    return _hosted(
        body,
        name=name,
        sched=sched,
        grid=(T // tm,),
        in_specs=[
            pl.BlockSpec((tm, GW), lambda i: (i, 0)),
            pl.BlockSpec((tm, GW), lambda i: (i, 1)),
            pl.BlockSpec((1, GW), lambda i: (0, 0)),
            pl.BlockSpec((1, GW), lambda i: (0, 0)),
            pl.BlockSpec((G, CHUNK, CHUNK), lambda i: (0, 0, 0)),
            pl.BlockSpec((CHUNK, G), lambda i: (0, 0)),
        ],
        out_specs=pl.BlockSpec((tm, GW), lambda i: (i, 0)),
        out_shape=jax.ShapeDtypeStruct((T, GW), BF16),
        compiler_params=_params(("arbitrary",), [((tm, GW), F32)] * 3),
    )(P, P, lnv_g, lnv_b, wm, bsT)


def _gmlp_bwd(d, P, dya_b, Ga, lnv_g, lnv_b, wm, bsT, *, name, sched=None):
    T, D, GW, G = d.T, d.D, d.GW, d.G
    tm = _tile(T, 256)
    nc = tm // CHUNK

    def body(u_ref, v_ref, dya_ref, ga_ref, g_ref, b_ref, wm_ref, bs_ref, duv_ref, dwm_ref, dbs_ref, dlg_ref, dlb_ref, dgu_s, dvn_s):
        first = pl.program_id(0) == 0
        dsgu = _dot(dya_ref[...], ga_ref[...], "nn")
        u, v = u_ref[...], v_ref[...]
        gu = _gelu(u)
        lng = g_ref[...]
        vn, xh, rstd = _ln_fwd(_gelu(v), lng, b_ref[...])
        vn = vn.astype(BF16)
        wl = _masked_mix_weights(wm_ref, G)
        r = lax.broadcasted_iota(jnp.int32, (CHUNK, CHUNK), 0)
        cc = lax.broadcasted_iota(jnp.int32, (CHUNK, CHUNK), 1)
        lane_g = lax.broadcasted_iota(jnp.int32, (CHUNK, G), 1)
        dbs = jnp.zeros((CHUNK, G), F32)
        for g in range(G):
            cols = slice(g * GROUP_DIM, (g + 1) * GROUP_DIM)
            dw = jnp.zeros((CHUNK, CHUNK), F32)
            for c in range(nc):
                rows = slice(c * CHUNK, (c + 1) * CHUNK)
                mixed = _dot(wl[g], vn[rows, cols], "nn") + bs_ref[:, g : g + 1]
                ds = dsgu[rows, cols]
                dgu_s[rows, cols] = ds * mixed
                dmx = ds * gu[rows, cols]
                dbs = dbs + jnp.where(lane_g == g, jnp.sum(dmx, axis=1, keepdims=True), 0.0)
                dmx = dmx.astype(BF16)
                dw = dw + _dot(dmx, vn[rows, cols], "nt")
                dvn_s[rows, cols] = _dot(wl[g], dmx, "tn")
            _acc_store(dwm_ref.at[g], jnp.where(cc <= r, dw, 0.0), first)
        _acc_store(dbs_ref, dbs, first)
        dvn = dvn_s[...]
        _acc_store(dlg_ref, jnp.sum(dvn * xh, axis=0, keepdims=True), first)
        _acc_store(dlb_ref, jnp.sum(dvn, axis=0, keepdims=True), first)
        dgv = _ln_bwd_vals(dvn, xh, rstd, lng)
        duv_ref[:, :GW] = (dgu_s[...] * _gelu_grad(u)).astype(BF16)
        duv_ref[:, GW:] = (dgv * _gelu_grad(v)).astype(BF16)

    return _hosted(
        body,
        name=name,
        sched=sched,
        grid=(T // tm,),
        in_specs=[
            pl.BlockSpec((tm, GW), lambda i: (i, 0)),
            pl.BlockSpec((tm, GW), lambda i: (i, 1)),
            pl.BlockSpec((tm, D), lambda i: (i, 0)),
            pl.BlockSpec((D, GW), lambda i: (0, 0)),
            pl.BlockSpec((1, GW), lambda i: (0, 0)),
            pl.BlockSpec((1, GW), lambda i: (0, 0)),
            pl.BlockSpec((G, CHUNK, CHUNK), lambda i: (0, 0, 0)),
            pl.BlockSpec((CHUNK, G), lambda i: (0, 0)),
        ],
        out_specs=[
            pl.BlockSpec((tm, 2 * GW), lambda i: (i, 0)),
            pl.BlockSpec((G, CHUNK, CHUNK), lambda i: (0, 0, 0)),
            pl.BlockSpec((CHUNK, G), lambda i: (0, 0)),
            pl.BlockSpec((1, GW), lambda i: (0, 0)),
            pl.BlockSpec((1, GW), lambda i: (0, 0)),
        ],
        out_shape=[
            jax.ShapeDtypeStruct((T, 2 * GW), BF16),
            jax.ShapeDtypeStruct((G, CHUNK, CHUNK), F32),
            jax.ShapeDtypeStruct((CHUNK, G), F32),
            jax.ShapeDtypeStruct((1, GW), F32),
            jax.ShapeDtypeStruct((1, GW), F32),
        ],
        scratch_shapes=[pltpu.VMEM((tm, GW), F32), pltpu.VMEM((tm, GW), F32)],
        compiler_params=_params(
            ("arbitrary",), [((tm, GW), F32)] * 3 + [((tm, D), BF16), ((D, GW), BF16)], 2 * _nbytes((tm, GW), F32)
        ),
    )(P, P, dya_b, Ga, lnv_g, lnv_b, wm, bsT)


def _swap_halves(x):
    w = x.shape[1]
    lane = lax.broadcasted_iota(jnp.int32, x.shape, 1)
    return jnp.where((lane % HEAD_DIM) < ROPE_HALF, pltpu.roll(x, w - ROPE_HALF, 1), pltpu.roll(x, ROPE_HALF, 1))


def _lane_tile(t, width):
    reps = width // LANES
    return t if reps == 1 else jnp.tile(t, (1, reps))


def _rope(x, cos2, sin2):
    w = x.shape[1]
    return x * _lane_tile(cos2, w) + _swap_halves(x) * _lane_tile(sin2, w)


def _rope_bwd(g, cos2, sin2):
    w = g.shape[1]
    return g * _lane_tile(cos2, w) - _swap_halves(g) * _lane_tile(sin2, w)


PAIR = LANES // HEAD_DIM


def _swa_valid(i):
    s = lax.broadcasted_iota(jnp.int32, (2 * CHUNK, CHUNK), 0)
    t = lax.broadcasted_iota(jnp.int32, (2 * CHUNK, CHUNK), 1)
    cur = jnp.logical_and(s >= CHUNK, s - CHUNK <= t)
    prev = jnp.logical_and(jnp.logical_and(s < CHUNK, s > t), i > 0)
    return jnp.logical_or(cur, prev)


def _half_variants(x, odd):
    lane = lax.broadcasted_iota(jnp.int32, x.shape, 1)
    if odd:
        hi = jnp.where(lane >= HEAD_DIM, x, jnp.zeros_like(x))
        return pltpu.roll(hi, HEAD_DIM, 1), hi
    lo = jnp.where(lane < HEAD_DIM, x, jnp.zeros_like(x))
    return lo, pltpu.roll(lo, HEAD_DIM, 1)


def _swa_probs_t(kx, qp, sink, valid):
    s = jnp.where(valid, _dot(kx, qp, "nt") * (HEAD_DIM**-0.5), -jnp.inf)
    m = jnp.maximum(jnp.max(s, axis=0, keepdims=True), sink)
    e = jnp.exp(s - m)
    e_s = jnp.exp(sink - m)
    den = jnp.sum(e, axis=0, keepdims=True) + e_s
    return e / den, e_s / den


def _swa_load(q_ref, kc_ref, kp_ref, vc_ref, vp_ref, cc, sc, cp, sp):
    qr = _rope(q_ref[...], cc, sc).astype(BF16)
    kcat = jnp.concatenate([_rope(kp_ref[...], cp, sp), _rope(kc_ref[...], cc, sc)], axis=0)
    vcat = jnp.concatenate([vp_ref[...], vc_ref[...]], axis=0)
    return qr, kcat, vcat


def _swa_specs(d):
    _, off_q, off_k, off_va, _, _ = _offsets(d)
    assert off_q % d.AW == 0 and off_k % d.KVW == 0 and off_va % d.KVW == 0
    prev = lambda i: jnp.maximum(i - 1, 0)
    return [
        pl.BlockSpec(memory_space=pltpu.SMEM),
        pl.BlockSpec((CHUNK, d.AW), lambda i, o=off_q // d.AW: (i, o)),
        pl.BlockSpec((CHUNK, d.KVW), lambda i, o=off_k // d.KVW: (i, o)),
        pl.BlockSpec((CHUNK, d.KVW), lambda i, o=off_k // d.KVW: (prev(i), o)),
        pl.BlockSpec((CHUNK, d.KVW), lambda i, o=off_va // d.KVW: (i, o)),
        pl.BlockSpec((CHUNK, d.KVW), lambda i, o=off_va // d.KVW: (prev(i), o)),
        pl.BlockSpec((CHUNK, LANES), lambda i: (i, 0)),
        pl.BlockSpec((CHUNK, LANES), lambda i: (i, 0)),
        pl.BlockSpec((CHUNK, LANES), lambda i: (prev(i), 0)),
        pl.BlockSpec((CHUNK, LANES), lambda i: (prev(i), 0)),
    ]


def _swa_fwd(d, P, cos2, sin2, sinks, *, name, sched=None):
    T, AW, HQ, HKV = d.T, d.AW, d.HQ, d.HKV
    grp = HQ // HKV
    assert grp % PAIR == 0 and HKV % PAIR == 0

    def body(sink_ref, q_ref, kc_ref, kp_ref, vc_ref, vp_ref, cc_ref, sc_ref, cp_ref, sp_ref, o_ref):
        i = pl.program_id(0)
        qr, kcat, vcat = _swa_load(q_ref, kc_ref, kp_ref, vc_ref, vp_ref, cc_ref[...], sc_ref[...], cp_ref[...], sp_ref[...])
        valid = _swa_valid(i)
        for kv in range(HKV):
            col = slice((kv // PAIR) * LANES, (kv // PAIR + 1) * LANES)
            kx = [t.astype(BF16) for t in _half_variants(kcat[:, col], kv % PAIR)]
            vx = [t.astype(BF16) for t in _half_variants(vcat[:, col], kv % PAIR)]
            for pp in range(grp // PAIR):
                p = kv * (grp // PAIR) + pp
                qs = slice(p * LANES, (p + 1) * LANES)
                out = jnp.zeros((CHUNK, LANES), F32)
                for half in range(PAIR):
                    pt, _ = _swa_probs_t(kx[half], qr[:, qs], sink_ref[PAIR * p + half], valid)
                    out = out + _dot(pt.astype(BF16), vx[half], "tn")
                o_ref[:, qs] = out.astype(BF16)

    return _hosted(
        body,
        name=name,
        sched=sched,
        grid=(T // CHUNK,),
        in_specs=_swa_specs(d),
        out_specs=pl.BlockSpec((CHUNK, AW), lambda i: (i, 0)),
        out_shape=jax.ShapeDtypeStruct((T, AW), BF16),
        compiler_params=_params(("arbitrary",), [((CHUNK, AW), F32)] * 3),
    )(sinks, P, P, P, P, P, cos2, sin2, cos2, sin2)


def _swa_bwd(d, P, dyb_b, Gb, cos2, sin2, sinks, *, name, sched=None):
    T, D, AW, KVW, HQ, HKV = d.T, d.D, d.AW, d.KVW, d.HQ, d.HKV
    grp = HQ // HKV
    nb = T // CHUNK
    scale = HEAD_DIM**-0.5
    assert grp % PAIR == 0 and HKV % PAIR == 0

    def body(sink_ref, q_ref, kc_ref, kp_ref, vc_ref, vp_ref, cc_ref, sc_ref, cp_ref, sp_ref, dyb_ref, gb_ref,
             dq_ref, dkv_ref, dsk_ref, acc_s, dq_s, dk_s, dv_s):
        i = pl.program_id(0)
        cc, sc, cp, sp = cc_ref[...], sc_ref[...], cp_ref[...], sp_ref[...]
        datt = _dot(dyb_ref[...], gb_ref[...], "nn").astype(BF16)
        qr, kcat, vcat = _swa_load(q_ref, kc_ref, kp_ref, vc_ref, vp_ref, cc, sc, cp, sp)
        valid = _swa_valid(i)
        lane1 = lax.broadcasted_iota(jnp.int32, (1, LANES), 1)
        lane2 = lax.broadcasted_iota(jnp.int32, (2 * CHUNK, LANES), 1)
        dsk = jnp.zeros((1, LANES), F32)
        for kvp in range(HKV // PAIR):
            col = slice(kvp * LANES, (kvp + 1) * LANES)
            dk_col = jnp.zeros((2 * CHUNK, LANES), F32)
            dv_col = jnp.zeros((2 * CHUNK, LANES), F32)
            for odd in range(PAIR):
                kv = kvp * PAIR + odd
                kx = [t.astype(BF16) for t in _half_variants(kcat[:, col], odd)]
                vx = [t.astype(BF16) for t in _half_variants(vcat[:, col], odd)]
                dk_half = [jnp.zeros((2 * CHUNK, LANES), F32) for _ in range(PAIR)]
                dv_half = [jnp.zeros((2 * CHUNK, LANES), F32) for _ in range(PAIR)]
                for pp in range(grp // PAIR):
                    p = kv * (grp // PAIR) + pp
                    qs = slice(p * LANES, (p + 1) * LANES)
                    qp, dop = qr[:, qs], datt[:, qs]
                    dq = jnp.zeros((CHUNK, LANES), F32)
                    for half in range(PAIR):
                        h = PAIR * p + half
                        pt, p_s = _swa_probs_t(kx[half], qp, sink_ref[h], valid)
                        dpt = _dot(vx[half], dop, "nt")
                        rs = jnp.sum(pt * dpt, axis=0, keepdims=True)
                        dst = (pt * (dpt - rs) * scale).astype(BF16)
                        dsk = dsk + jnp.where(lane1 == h, -jnp.sum(p_s * rs, axis=1, keepdims=True), 0.0)
                        dq = dq + _dot(dst, kx[half], "tn")
                        dk_half[half] = dk_half[half] + _dot(dst, qp, "nn")
                        dv_half[half] = dv_half[half] + _dot(pt.astype(BF16), dop, "nn")
                    dq_s[:, qs] = dq
                for acc_half, is_k in ((dk_half, True), (dv_half, False)):
                    lo = jnp.where(lane2 < HEAD_DIM, acc_half[0], 0.0)
                    hi = jnp.where(lane2 >= HEAD_DIM, acc_half[1], 0.0)
                    both = (pltpu.roll(lo, HEAD_DIM, 1) + hi) if odd else (lo + pltpu.roll(hi, HEAD_DIM, 1))
                    if is_k:
                        dk_col = dk_col + both
                    else:
                        dv_col = dv_col + both
            dk_s[:, col] = dk_col
            dv_s[:, col] = dv_col
        dq_ref[...] = _rope_bwd(dq_s[...], cc, sc).astype(BF16)
        cur = pl.ds(pl.multiple_of(i * CHUNK, CHUNK), CHUNK)
        acc_s[cur, :KVW] = _rope_bwd(dk_s[CHUNK:, :], cc, sc)
        acc_s[cur, KVW:] = dv_s[CHUNK:, :]

        @pl.when(i > 0)
        def _():
            prv = pl.ds(pl.multiple_of((i - 1) * CHUNK, CHUNK), CHUNK)
            acc_s[prv, :KVW] += _rope_bwd(dk_s[:CHUNK, :], cp, sp)
            acc_s[prv, KVW:] += dv_s[:CHUNK, :]

        _acc_store(dsk_ref, dsk, i == 0)

        @pl.when(i == nb - 1)
        def _():
            dkv_ref[...] = acc_s[...].astype(BF16)

    return _hosted(
        body,
        name=name,
        sched=sched,
        grid=(nb,),
        in_specs=_swa_specs(d) + [pl.BlockSpec((CHUNK, D), lambda i: (i, 0)), pl.BlockSpec((D, AW), lambda i: (0, 0))],
        out_specs=[
            pl.BlockSpec((CHUNK, AW), lambda i: (i, 0)),
            pl.BlockSpec((T, 2 * KVW), lambda i: (0, 0)),
            pl.BlockSpec((1, LANES), lambda i: (0, 0)),
        ],
        out_shape=[
            jax.ShapeDtypeStruct((T, AW), BF16),
            jax.ShapeDtypeStruct((T, 2 * KVW), BF16),
            jax.ShapeDtypeStruct((1, LANES), F32),
        ],
        scratch_shapes=[
            pltpu.VMEM((T, 2 * KVW), F32),
            pltpu.VMEM((CHUNK, AW), F32),
            pltpu.VMEM((2 * CHUNK, KVW), F32),
            pltpu.VMEM((2 * CHUNK, KVW), F32),
        ],
        compiler_params=_params(
            ("arbitrary",), [((CHUNK, AW), F32)] * 3 + [((D, AW), BF16), ((T, 2 * KVW), BF16)], _nbytes((T, 2 * KVW), F32)
        ),
    )(sinks, P, P, P, P, P, cos2, sin2, cos2, sin2, dyb_b, Gb)


def _branch_merge(d, sgu, att, Ga, Gb, P, b_gate, *, name, sched=None):
    T, D, GW, AW = d.T, d.D, d.GW, d.AW
    _, _, _, _, off_ga, off_gb = _offsets(d)
    tm, tn = _tile(T, 1024), _tile(D, 512)
    assert off_ga % tn == 0 and off_gb % tn == 0

    def body(s_ref, a_ref, ga_w, gb_w, ga_ref, gb_ref, bga_ref, bgb_ref, ya_ref, yb_ref, mg_ref):
        ya = _dot(s_ref[...], ga_w[...], "nt")
        yb = _dot(a_ref[...], gb_w[...], "nt")
        ya_ref[...] = ya
        yb_ref[...] = yb
        sa = jax.nn.sigmoid(ga_ref[...] + bga_ref[...])
        sb = jax.nn.sigmoid(gb_ref[...] + bgb_ref[...])
        mg_ref[...] = (sa * ya + sb * yb).astype(BF16)

    tile = pl.BlockSpec((tm, tn), lambda j, i: (i, j))
    return _hosted(
        body,
        name=name,
        sched=sched,
        grid=(D // tn, T // tm),
        in_specs=[
            pl.BlockSpec((tm, GW), lambda j, i: (i, 0)),
            pl.BlockSpec((tm, AW), lambda j, i: (i, 0)),
            pl.BlockSpec((tn, GW), lambda j, i: (j, 0)),
            pl.BlockSpec((tn, AW), lambda j, i: (j, 0)),
            pl.BlockSpec((tm, tn), lambda j, i, o=off_ga // tn: (i, j + o)),
            pl.BlockSpec((tm, tn), lambda j, i, o=off_gb // tn: (i, j + o)),
            pl.BlockSpec((1, tn), lambda j, i: (0, j)),
            pl.BlockSpec((1, tn), lambda j, i, o=D // tn: (0, j + o)),
        ],
        out_specs=[tile, tile, tile],
        out_shape=[jax.ShapeDtypeStruct((T, D), F32), jax.ShapeDtypeStruct((T, D), F32), jax.ShapeDtypeStruct((T, D), BF16)],
        compiler_params=_params(
            ("arbitrary", "arbitrary"),
            [((tm, GW), BF16), ((tm, AW), BF16), ((tn, GW), BF16), ((tn, AW), BF16)] + [((tm, tn), F32)] * 5,
        ),
    )(sgu, att, Ga, Gb, P, P, b_gate, b_gate)


def _xattn_probs(qh, kh):
    s = _dot(qh, kh, "nt") * (X_HEAD_DIM**-0.5)
    e = jnp.exp(s - jnp.max(s, axis=1, keepdims=True))
    return e / jnp.sum(e, axis=1, keepdims=True)


def _xattn_fwd(d, X, Xb, kv, Gxq, Gxo, ln_g, ln_b, *, name, sched=None):
    T, D, XW, XH, MEM = d.T, d.D, d.XW, d.XH, d.MEM
    tm = _tile(T, 256)

    def body(x_ref, xb_ref, kv_ref, wq_ref, wo_ref, g_ref, b_ref, y_ref, yb_ref, xh_ref, rs_ref, q_ref, o_ref, o_s):
        q = _dot(xb_ref[...], wq_ref[...], "nn").astype(BF16)
        q_ref[...] = q
        for h in range(XH):
            hs = slice(h * X_HEAD_DIM, (h + 1) * X_HEAD_DIM)
            p = _xattn_probs(q[:, hs], kv_ref[:, hs]).astype(BF16)
            o_s[:, hs] = _dot(p, kv_ref[:, XW + h * X_HEAD_DIM : XW + (h + 1) * X_HEAD_DIM], "nn").astype(BF16)
        o = o_s[...]
        o_ref[...] = o
        y, xhat, rstd = _ln_fwd(ALPHA * x_ref[...] + _dot(o, wo_ref[...], "nt"), g_ref[...], b_ref[...])
        y_ref[...] = y
        yb_ref[...] = y.astype(BF16)
        xh_ref[...] = xhat
        rs_ref[...] = rstd

    row = pl.BlockSpec((tm, D), lambda i: (i, 0))
    nar = pl.BlockSpec((tm, XW), lambda i: (i, 0))
    vec = pl.BlockSpec((1, D), lambda i: (0, 0))
    return _hosted(
        body,
        name=name,
        sched=sched,
        grid=(T // tm,),
        in_specs=[
            row,
            row,
            pl.BlockSpec((MEM, 2 * XW), lambda i: (0, 0)),
            pl.BlockSpec((D, XW), lambda i: (0, 0)),
            pl.BlockSpec((D, XW), lambda i: (0, 0)),
            vec,
            vec,
        ],
        out_specs=[row, row, row, pl.BlockSpec((tm, 1), lambda i: (i, 0)), nar, nar],
        out_shape=[
            jax.ShapeDtypeStruct((T, D), F32),
            jax.ShapeDtypeStruct((T, D), BF16),
            jax.ShapeDtypeStruct((T, D), F32),
            jax.ShapeDtypeStruct((T, 1), F32),
            jax.ShapeDtypeStruct((T, XW), BF16),
            jax.ShapeDtypeStruct((T, XW), BF16),
        ],
        scratch_shapes=[pltpu.VMEM((tm, XW), BF16)],
        compiler_params=_params(("arbitrary",), [((tm, D), F32)] * 4 + [((D, XW), BF16)] * 2),
    )(X, Xb, kv, Gxq, Gxo, ln_g, ln_b)


def _xattn_bwd(d, dz, dzb, q_b, kv, Gxq, Gxo, *, name, sched=None):
    T, D, XW, XH, MEM = d.T, d.D, d.XW, d.XH, d.MEM
    tm = _tile(T, 256)
    scale = X_HEAD_DIM**-0.5

    def body(dz_ref, dzb_ref, q_ref, kv_ref, wq_ref, wo_ref, dx_ref, dq_ref, dkv_ref, dq_s):
        first = pl.program_id(0) == 0
        do = _dot(dzb_ref[...], wo_ref[...], "nn").astype(BF16)
        for h in range(XH):
            hs = slice(h * X_HEAD_DIM, (h + 1) * X_HEAD_DIM)
            vs = slice(XW + h * X_HEAD_DIM, XW + (h + 1) * X_HEAD_DIM)
            kh, vh, qh, doh = kv_ref[:, hs], kv_ref[:, vs], q_ref[:, hs], do[:, hs]
            p = _xattn_probs(qh, kh)
            dp = _dot(doh, vh, "nt")
            ds = (p * (dp - jnp.sum(p * dp, axis=1, keepdims=True)) * scale).astype(BF16)
            dq_s[:, hs] = _dot(ds, kh, "nn").astype(BF16)
            _acc_store(dkv_ref.at[h], _dot(ds, qh, "tn"), first)
            _acc_store(dkv_ref.at[XH + h], _dot(p.astype(BF16), doh, "tn"), first)
        dq = dq_s[...]
        dq_ref[...] = dq
        dx_ref[...] = ALPHA * dz_ref[...] + _dot(dq, wq_ref[...], "nt")

    row = pl.BlockSpec((tm, D), lambda i: (i, 0))
    nar = pl.BlockSpec((tm, XW), lambda i: (i, 0))
    return _hosted(
        body,
        name=name,
        sched=sched,
        grid=(T // tm,),
        in_specs=[
            row,
            row,
            nar,
            pl.BlockSpec((MEM, 2 * XW), lambda i: (0, 0)),
            pl.BlockSpec((D, XW), lambda i: (0, 0)),
            pl.BlockSpec((D, XW), lambda i: (0, 0)),
        ],
        out_specs=[row, nar, pl.BlockSpec((2 * XH, MEM, X_HEAD_DIM), lambda i: (0, 0, 0))],
        out_shape=[
            jax.ShapeDtypeStruct((T, D), F32),
            jax.ShapeDtypeStruct((T, XW), BF16),
            jax.ShapeDtypeStruct((2 * XH, MEM, X_HEAD_DIM), F32),
        ],
        scratch_shapes=[pltpu.VMEM((tm, XW), BF16)],
        compiler_params=_params(("arbitrary",), [((tm, D), F32)] * 3 + [((D, XW), BF16)] * 2),
    )(dz, dzb, q_b, kv, Gxq, Gxo)


def _resid_ln_outs():
    return (("mn", F32), ("mn", BF16), ("mn", F32), ("m1", F32))


def _layer_fwd(d, X, Xb, memb, G, S, cos2, sin2, tag, sched=None):
    D = d.D
    mm = functools.partial(_matmul, sched=sched)
    P = mm(Xb, G["in"], "nt", name=f"proj_{tag}", tm=1024, tn=_tile(d.IN, 1280))
    sgu = _gmlp_fwd(d, P, S["ln_v_g"], S["ln_v_b"], S["w_s"], S["b_sT"], name=f"gmlp_fwd_{tag}", sched=sched)
    att = _swa_fwd(d, P, cos2, sin2, S["sinks"], name=f"swa_fwd_{tag}", sched=sched)
    ya, yb, merged = _branch_merge(d, sgu, att, G["a"], G["b"], P, S["b_gate"], name=f"merge_{tag}", sched=sched)
    X1, X1b, xh1, rs1 = mm(
        merged, G["o"], "nn", name=f"oproj_ln_{tag}", tn=D, tk=1024, out_defs=_resid_ln_outs(), epilogue=_ep_resid_ln,
        extras=((X, "mn", 0), (S["ln1_g"], "row", 0), (S["ln1_b"], "row", 0)),
    )
    kv = mm(memb, G["xkv"], "nn", name=f"xkv_{tag}", out_defs=(("mn", BF16),))
    X2, X2b, xh2, rs2, q_b, o_b = _xattn_fwd(d, X1, X1b, kv, G["xq"], G["xo"], S["ln2_g"], S["ln2_b"], name=f"xattn_fwd_{tag}", sched=sched)
    H, A = mm(X2b, G["up"], "nt", name=f"up_{tag}", tm=1024, out_defs=(("mn", F32), ("mn", BF16)), epilogue=_ep_relu2)
    X3, X3b, xh3, rs3 = mm(
        A, G["down"], "nn", name=f"down_ln_{tag}", tn=D, tk=1024, out_defs=_resid_ln_outs(), epilogue=_ep_resid_ln,
        extras=((X2, "mn", 0), (S["ln3_g"], "row", 0), (S["ln3_b"], "row", 0)),
    )
    saved = dict(Xb=Xb, P=P, sgu=sgu, att=att, ya=ya, yb=yb, merged=merged, X1b=X1b, xh1=xh1, rs1=rs1, kv=kv, q_b=q_b,
                 o_b=o_b, X2b=X2b, xh2=xh2, rs2=rs2, H=H, A=A, xh3=xh3, rs3=rs3)
    return X3, X3b, saved


def _layer_bwd(d, dXout, sv, memb, G, S, cos2, sin2, tag, sched=None, on_grad=None):
    D = d.D
    mm = functools.partial(_matmul, sched=sched)
    emit = on_grad if on_grad is not None else (lambda n, g: None)
    _, _, _, _, off_ga, off_gb = _offsets(d)
    bf = (("mn", BF16),)
    dz3, dz3b, dg3, db3 = _ln_bwd(dXout, sv["xh3"], sv["rs3"], S["ln3_g"], name=f"ln3_bwd_{tag}", sched=sched)
    dHb = mm(dz3b, G["down"], "nt", name=f"down_dx_{tag}", tm=1024, out_defs=bf, epilogue=_ep_relu2_bwd, extras=((sv["H"], "mn", 0),))
    g_down = mm(sv["A"], dz3b, "tn", name=f"down_dw_{tag}", tn=D)
    emit("down", g_down)
    dX2o = mm(dHb, G["up"], "nn", name=f"up_dx_{tag}", tn=D, epilogue=_ep_resid, extras=((dz3, "mn", 0),))
    g_up = mm(dHb, sv["X2b"], "tn", name=f"up_dw_{tag}", tn=D)
    emit("up", g_up)
    dz2, dz2b, dg2, db2 = _ln_bwd(dX2o, sv["xh2"], sv["rs2"], S["ln2_g"], name=f"ln2_bwd_{tag}", sched=sched)
    dX1o, dq_b, dkv = _xattn_bwd(d, dz2, dz2b, sv["q_b"], sv["kv"], G["xq"], G["xo"], name=f"xattn_bwd_{tag}", sched=sched)
    g_xo = mm(dz2b, sv["o_b"], "tn", name=f"xo_dw_{tag}")
    emit("xo", g_xo)
    g_xq = mm(sv["X1b"], dq_b, "tn", name=f"xq_dw_{tag}")
    emit("xq", g_xq)
    dkv_b = dkv.transpose(1, 0, 2).reshape(d.MEM, 2 * d.XW).astype(BF16)
    g_xkv = mm(memb, dkv_b, "tn", name=f"xkv_dw_{tag}")
    emit("xkv", g_xkv)
    dz1, dz1b, dg1, db1 = _ln_bwd(dX1o, sv["xh1"], sv["rs1"], S["ln1_g"], name=f"ln1_bwd_{tag}", sched=sched)
    dya_b, dyb_b, dga_b, dgb_b, dbga, dbgb = mm(
        dz1b, G["o"], "nt", name=f"oproj_dx_{tag}", tm=1024, tn=512,
        out_defs=(("mn", BF16), ("mn", BF16), ("mn", BF16), ("mn", BF16), ("acc", F32), ("acc", F32)),
        epilogue=_ep_gate_bwd,
        extras=((sv["P"], "mn", off_ga), (sv["P"], "mn", off_gb), (sv["ya"], "mn", 0), (sv["yb"], "mn", 0),
                (S["b_gate"], "row", 0), (S["b_gate"], "row", D)),
    )
    g_o = mm(sv["merged"], dz1b, "tn", name=f"oproj_dw_{tag}", tn=D)
    emit("o", g_o)
    duv_b, dwm, dbsT, dlvg, dlvb = _gmlp_bwd(d, sv["P"], dya_b, G["a"], S["ln_v_g"], S["ln_v_b"], S["w_s"], S["b_sT"], name=f"gmlp_bwd_{tag}", sched=sched)
    g_a = mm(dya_b, sv["sgu"], "tn", name=f"bra_dw_{tag}")
    emit("a", g_a)
    dqr_b, dkvr_b, dsk = _swa_bwd(d, sv["P"], dyb_b, G["b"], cos2, sin2, S["sinks"], name=f"swa_bwd_{tag}", sched=sched)
    g_b = mm(dyb_b, sv["att"], "tn", name=f"brb_dw_{tag}")
    emit("b", g_b)
    small = dict(b_gate=jnp.concatenate([dbga, dbgb], axis=1), ln_v_g=dlvg, ln_v_b=dlvb, w_s=dwm, b_sT=dbsT, sinks=dsk,
                 ln1_g=dg1, ln1_b=db1, ln2_g=dg2, ln2_b=db2, ln3_g=dg3, ln3_b=db3)
    emit("small", small)
    dP = jnp.concatenate([duv_b, dqr_b, dkvr_b, dga_b, dgb_b], axis=1)
    g_in = mm(dP, sv["Xb"], "tn", name=f"proj_dw_{tag}", tn=D)
    emit("in", g_in)
    dXin = mm(dP, G["in"], "nn", name=f"proj_dx_{tag}", tn=D, tk=_tile(d.IN, 1920), epilogue=_ep_resid, extras=((dz1, "mn", 0),))
    big = {"in": g_in, "a": g_a, "b": g_b, "o": g_o, "xq": g_xq, "xkv": g_xkv, "xo": g_xo, "up": g_up, "down": g_down}
    return dXin, big, small


def _rope_tables(T):
    inv = 1.0 / (ROPE_THETA ** (jnp.arange(0, HEAD_DIM, 2, dtype=F32) / HEAD_DIM))
    ang = jnp.arange(T, dtype=F32)[:, None] * inv[None, :]
    cos, sin = jnp.cos(ang), jnp.sin(ang)
    reps = LANES // HEAD_DIM
    return jnp.concatenate([cos, cos] * reps, axis=1), jnp.concatenate([-sin, sin] * reps, axis=1)


def _local_step(d, x, mem, target, Gs, Ss, sched=None, on_grad=None):
    cos2, sin2 = _rope_tables(d.T)
    memb = mem.astype(BF16)
    X, Xb = x, x.astype(BF16)
    saved = []
    for l in range(DEPTH):
        X, Xb, sv = _layer_fwd(d, X, Xb, memb, Gs[l], Ss[l], cos2, sin2, f"l{l}", sched)
        saved.append(sv)
    loss, dX = _loss_head(X, target, name="loss_head", sched=sched)
    bigs, smalls = [None] * DEPTH, [None] * DEPTH
    for l in reversed(range(DEPTH)):
        emit = None if on_grad is None else functools.partial(on_grad, l)
        dX, bigs[l], smalls[l] = _layer_bwd(d, dX, saved[l], memb, Gs[l], Ss[l], cos2, sin2, f"l{l}", sched, emit)
    return loss, dX, bigs, smalls


def _place():
    x, y, c = lax.axis_index("x"), lax.axis_index("y"), lax.axis_index("c")
    return x, y, c, [(1 - x, y), (x, 1 - y), (1 - x, 1 - y)]


def _all_gather(shards, *, name, sched=None):
    n = len(shards)
    any_spec = pl.BlockSpec(memory_space=pl.ANY)

    def body(*refs):
        ins, outs = refs[:n], refs[n : 2 * n]
        send_sems, recv_sems, local_sems = refs[2 * n :]
        x, y, c, chips = _place()
        me, sibling = (x, y, c), (x, y, 1 - c)

        def rows(w, p):
            r = ins[w].shape[0]
            return outs[w].at[pl.ds((4 * p[0] + 2 * p[1] + p[2]) * r, r), :]

        def copy(w, k, block, to, src=None):
            return pltpu.make_async_remote_copy(
                src_ref=rows(w, block) if src is None else src, dst_ref=rows(w, block),
                send_sem=send_sems.at[7 * w + k], recv_sem=recv_sems.at[7 * w + k], device_id=to, device_id_type=MESH)

        mine = [pltpu.make_async_copy(ins[w], rows(w, me), local_sems.at[w]) for w in range(n)]
        for cp in mine:
            cp.start()
        first = []
        for w in range(n):
            first.append(copy(w, 0, me, sibling, src=ins[w]))
            first += [copy(w, 1 + j, me, (*chip, c), src=ins[w]) for j, chip in enumerate(chips)]
        for cp in first:
            cp.start()
        passed = []
        for j, chip in enumerate(chips):
            for w in range(n):
                copy(w, 1 + j, (*chip, c), me).wait_recv()
                fwd = copy(w, 4 + j, (*chip, c), sibling)
                fwd.start()
                passed.append(fwd)
        for w in range(n):
            copy(w, 0, sibling, me).wait_recv()
            for j, chip in enumerate(chips):
                copy(w, 4 + j, (*chip, 1 - c), me).wait_recv()
        for cp in first + passed:
            cp.wait_send()
        for cp in mine:
            cp.wait()

    return _hosted(
        body,
        name=name,
        sched=sched,
        in_specs=[any_spec] * n,
        out_specs=[any_spec] * n,
        out_shape=[jax.ShapeDtypeStruct((N_DEV * s.shape[0], s.shape[1]), s.dtype) for s in shards],
        scratch_shapes=[pltpu.SemaphoreType.DMA((7 * n,)), pltpu.SemaphoreType.DMA((7 * n,)), pltpu.SemaphoreType.DMA((n,))],
    )(*shards)


class _Task:
    def __init__(self, ins, out_shapes, n_remote, n_local, plan, done, aliases=None):
        self.ins, self.out_shapes, self.n_remote, self.n_local = list(ins), list(out_shapes), n_remote, n_local
        self.plan, self.done, self.aliases = plan, done, dict(aliases or {})


class _Sched:
    def __init__(self):
        self.pending = {}

    def at(self, host, task):
        self.pending.setdefault(host, []).append(task)

    def take(self, host):
        return self.pending.pop(host, [])


def _hosted(body, *, name, sched=None, tasks=(), grid=(), in_specs=None, out_specs=None, out_shape=(), scratch_shapes=(),
            compiler_params=None, input_output_aliases=None):
    tasks = list(tasks) + (sched.take(name) if sched is not None else [])
    scratch_shapes = list(scratch_shapes)
    kwargs = dict(name=name)
    core_aliases = dict(input_output_aliases or {})
    if grid:
        kwargs["grid"] = grid
    if compiler_params is not None:
        kwargs["compiler_params"] = compiler_params
    if not tasks:
        if in_specs is not None:
            kwargs.update(in_specs=in_specs, out_specs=out_specs)
        return pl.pallas_call(body, out_shape=out_shape, scratch_shapes=scratch_shapes, input_output_aliases=core_aliases, **kwargs)
    assert in_specs is not None and out_specs is not None, name
    single = not isinstance(out_shape, (list, tuple))
    core_shape = [out_shape] if single else list(out_shape)
    core_specs = [out_specs] if single else list(out_specs)
    in_specs = list(in_specs)
    n_in, n_out, n_scr = len(in_specs), len(core_shape), len(scratch_shapes)
    t_ins = [a for t in tasks for a in t.ins]
    t_outs = [s for t in tasks for s in t.out_shapes]
    n_rem = sum(t.n_remote for t in tasks)
    n_loc = sum(t.n_local for t in tasks)
    aliases, pos_in, pos_out = dict(core_aliases), n_in, n_out
    for t in tasks:
        for a, b in t.aliases.items():
            aliases[pos_in + a] = pos_out + b
        pos_in += len(t.ins)
        pos_out += len(t.out_shapes)
    any_spec = pl.BlockSpec(memory_space=pl.ANY)

    def wrapped(*refs):
        core_in, t_in = refs[:n_in], refs[n_in : n_in + len(t_ins)]
        o0 = n_in + len(t_ins)
        core_out, t_out = refs[o0 : o0 + n_out], refs[o0 + n_out : o0 + n_out + len(t_outs)]
        s0 = o0 + n_out + len(t_outs)
        core_scr = refs[s0 : s0 + n_scr]
        send_sems, recv_sems, local_sems = refs[s0 + n_scr :]
        remote, local = [], []
        pi = po = pr = pq = 0
        for t in tasks:
            r, q = t.plan(t_in[pi : pi + len(t.ins)], t_out[po : po + len(t.out_shapes)], send_sems, recv_sems, local_sems, pr, pq)
            assert len(r) == t.n_remote and len(q) == t.n_local
            remote += r
            local += q
            pi, po, pr, pq = pi + len(t.ins), po + len(t.out_shapes), pr + t.n_remote, pq + t.n_local

        def start():
            for cp in local + remote:
                cp.start()

        def finish():
            for cp in remote + local:
                cp.wait()

        if grid:
            ids = [pl.program_id(a) for a in range(len(grid))]
            first = functools.reduce(jnp.logical_and, [i == 0 for i in ids])
            last = functools.reduce(jnp.logical_and, [i == g - 1 for i, g in zip(ids, grid)])
            pl.when(first)(start)
            body(*core_in, *core_out, *core_scr)
            pl.when(last)(finish)
        else:
            start()
            body(*core_in, *core_out, *core_scr)
            finish()

    call = pl.pallas_call(
        wrapped,
        in_specs=in_specs + [any_spec] * len(t_ins),
        out_specs=core_specs + [any_spec] * len(t_outs),
        out_shape=core_shape + t_outs,
        scratch_shapes=scratch_shapes
        + [pltpu.SemaphoreType.DMA((n_rem,)), pltpu.SemaphoreType.DMA((n_rem,)), pltpu.SemaphoreType.DMA((max(n_loc, 1),))],
        input_output_aliases=aliases,
        **kwargs,
    )

    def run(*operands):
        outs = call(*operands, *t_ins)
        po = n_out
        for t in tasks:
            t.done(list(outs[po : po + len(t.out_shapes)]))
            po += len(t.out_shapes)
        return outs[0] if single else list(outs[:n_out])

    return run


def _comm_only(tasks, *, name):
    _hosted(lambda: None, name=name, tasks=tasks, in_specs=[], out_shape=[], out_specs=[])()


def _rows(ref, block, n_blocks):
    r = ref.shape[0] // n_blocks
    return ref.at[pl.ds(block * r, r), :]


def _remote(src, dst, send_sems, recv_sems, k, to):
    return pltpu.make_async_remote_copy(src_ref=src, dst_ref=dst, send_sem=send_sems.at[k], recv_sem=recv_sems.at[k],
                                        device_id=to, device_id_type=MESH)


def _gather_stage1(shards, done, part=(0, 1), into=None):
    n = len(shards)
    k, n_parts = part

    def plan(ins, outs, send_sems, recv_sems, local_sems, pr, pq):
        x, y, c, chips = _place()
        mine = 4 * x + 2 * y + c
        remote, local = [], []
        for w in range(n):
            r = shards[w].shape[0]
            rp = r // n_parts
            src = ins[w].at[pl.ds(k * rp, rp), :]
            dst = outs[w].at[pl.ds(mine * r + k * rp, rp), :]
            local.append(pltpu.make_async_copy(src, dst, local_sems.at[pq + w]))
            for j, to in enumerate([(x, y, 1 - c)] + [(*chip, c) for chip in chips]):
                remote.append(_remote(src, dst, send_sems, recv_sems, pr + 4 * w + j, to))
        return remote, local

    shapes = [jax.ShapeDtypeStruct((N_DEV * s.shape[0], s.shape[1]), s.dtype) for s in shards]
    if into is None:
        return _Task(shards, shapes, 4 * n, n, plan, done)
    return _Task(list(shards) + list(into), shapes, 4 * n, n, plan, done, aliases={n + w: w for w in range(n)})


def _gather_stage2(partial, done):
    n = len(partial)

    def plan(ins, outs, send_sems, recv_sems, local_sems, pr, pq):
        x, y, c, chips = _place()
        remote = []
        for w in range(n):
            for j, chip in enumerate(chips):
                rows = _rows(outs[w], 4 * chip[0] + 2 * chip[1] + c, N_DEV)
                remote.append(_remote(rows, rows, send_sems, recv_sems, pr + 3 * w + j, (x, y, 1 - c)))
        return remote, []

    shapes = [jax.ShapeDtypeStruct(p.shape, p.dtype) for p in partial]
    return _Task(partial, shapes, 3 * n, 0, plan, done, aliases={w: w for w in range(n)})


def _sibling_stage(grads, done):
    n = len(grads)

    def plan(ins, outs, send_sems, recv_sems, local_sems, pr, pq):
        x, y, c, _ = _place()
        remote = []
        for w in range(n):
            for k in range(4):
                src = _rows(ins[w], 4 * (k // 2) + 2 * (k % 2) + (1 - c), N_DEV)
                remote.append(_remote(src, _rows(outs[w], k, 4), send_sems, recv_sems, pr + 4 * w + k, (x, y, 1 - c)))
        return remote, []

    shapes = [jax.ShapeDtypeStruct((g.shape[0] // 2, g.shape[1]), g.dtype) for g in grads]
    return _Task(grads, shapes, 4 * n, 0, plan, done)


def _chips_stage(sends, done, part=(0, 1), into=None):
    n = len(sends)
    k, n_parts = part

    def plan(ins, outs, send_sems, recv_sems, local_sems, pr, pq):
        x, y, c, chips = _place()
        remote = []
        for w in range(n):
            r = sends[w].shape[0] // 3
            rp = r // n_parts
            for j, chip in enumerate(chips):
                rows = pl.ds(j * r + k * rp, rp)
                remote.append(_remote(ins[w].at[rows, :], outs[w].at[rows, :], send_sems, recv_sems, pr + 3 * w + j, (*chip, c)))
        return remote, []

    shapes = [jax.ShapeDtypeStruct(s.shape, s.dtype) for s in sends]
    if into is None:
        return _Task(sends, shapes, 3 * n, 0, plan, done)
    return _Task(list(sends) + list(into), shapes, 3 * n, 0, plan, done, aliases={n + w: w for w in range(n)})


def _pair_sum(grad, got, place, *, name):
    R, C = grad.shape
    r = R // N_DEV
    tr = _tile(r, 256, 16)
    nt = r // tr

    def chip_of(s, pr):
        fx = jnp.where((s == 1) | (s == 3), 1, 0)
        fy = jnp.where(s >= 2, 1, 0)
        return pr[0] ^ fx, pr[1] ^ fy

    def grad_map(s, t, pr):
        kx, ky = chip_of(s, pr)
        return ((4 * kx + 2 * ky + pr[2]) * nt + t, 0)

    def got_map(s, t, pr):
        kx, ky = chip_of(s, pr)
        return ((2 * kx + ky) * nt + t, 0)

    def body(pr, g_ref, r_ref, own_ref, send_ref):
        s = pl.program_id(0)
        tot = g_ref[...] + r_ref[...]

        @pl.when(s == 0)
        def _():
            own_ref[...] = tot

        @pl.when(s > 0)
        def _():
            send_ref[...] = tot.astype(BF16)

    return pl.pallas_call(
        body,
        name=name,
        grid_spec=pltpu.PrefetchScalarGridSpec(
            num_scalar_prefetch=1,
            grid=(4, nt),
            in_specs=[pl.BlockSpec((tr, C), grad_map), pl.BlockSpec((tr, C), got_map)],
            out_specs=[
                pl.BlockSpec((tr, C), lambda s, t, pr: (jnp.where(s == 0, t, nt - 1), 0)),
                pl.BlockSpec((tr, C), lambda s, t, pr: (jnp.where(s == 0, 0, (s - 1) * nt + t), 0)),
            ],
        ),
        out_shape=[jax.ShapeDtypeStruct((r, C), F32), jax.ShapeDtypeStruct((3 * r, C), BF16)],
        compiler_params=_params(("arbitrary", "arbitrary"), [((tr, C), F32)] * 4),
    )(place, grad, got)


def _adam_vals(w, g, m, v):
    m = ADAM_B1 * m + (1.0 - ADAM_B1) * g
    v = ADAM_B2 * v + (1.0 - ADAM_B2) * (g * g)
    m_hat = m / (1.0 - ADAM_B1**ADAM_STEP)
    v_hat = v / (1.0 - ADAM_B2**ADAM_STEP)
    delta = -ADAM_LR * (m_hat / (jnp.sqrt(v_hat) + ADAM_EPS) + ADAM_WD * w)
    return delta, m, v


def _adam_big(w3, m3, v3, own, got, layer, transposed, prev, *, name, sched=None):
    _, A, B = w3.shape
    ta = _tile(A, 256, 16)
    nt = A // ta
    b_pad = (-B) % LANES

    def body(*refs):
        w_ref, m_ref, v_ref, own_ref, g1_ref, g2_ref, g3_ref = refs[:7]
        g_ref, d_ref, nm_ref, nv_ref = refs[-4:]
        g = ((own_ref[...] + g1_ref[...].astype(F32)) + g2_ref[...].astype(F32)) + g3_ref[...].astype(F32)
        if transposed:
            if b_pad:
                g = jnp.concatenate([g, jnp.zeros((b_pad, ta), F32)], axis=0)
            g = g.T[:, :B]
        g_ref[...] = g
        d_ref[...], nm_ref[...], nv_ref[...] = _adam_vals(w_ref[...], g, m_ref[...], v_ref[...])

    nat = pl.BlockSpec((None, ta, B), lambda t: (layer, t, 0))
    if transposed:
        parts = [pl.BlockSpec((B, ta), lambda t: (0, t))] + [pl.BlockSpec((B, ta), lambda t, j=j: (j, t)) for j in range(3)]
    else:
        parts = [pl.BlockSpec((ta, B), lambda t: (t, 0))] + [pl.BlockSpec((ta, B), lambda t, j=j: (j * nt + t, 0)) for j in range(3)]
    keep = [] if prev is None else list(prev)
    outs = _hosted(
        body,
        name=name,
        sched=sched,
        grid=(nt,),
        in_specs=[nat, nat, nat] + parts + [pl.BlockSpec(memory_space=pl.ANY)] * len(keep),
        out_specs=[nat] * 4,
        out_shape=[jax.ShapeDtypeStruct(w3.shape, F32)] * 4,
        input_output_aliases={7 + k: k for k in range(len(keep))},
        compiler_params=_params(("arbitrary",), [((ta, B), F32)] * 10),
    )(w3, m3, v3, own, got, got, got, *keep)
    return list(outs)


def _adam_small(w, m, v, parts, *, name, sched=None):
    R = w.shape[0]

    def body(w_ref, m_ref, v_ref, p_ref, g_ref, d_ref, nm_ref, nv_ref):
        g = p_ref[pl.ds(0, R), :]
        for k in range(1, N_DEV):
            g = g + p_ref[pl.ds(k * R, R), :]
        g_ref[...] = g
        d_ref[...], nm_ref[...], nv_ref[...] = _adam_vals(w_ref[...], g, m_ref[...], v_ref[...])

    return _hosted(
        body,
        name=name,
        sched=sched,
        out_shape=[jax.ShapeDtypeStruct((R, LANES), F32)] * 4,
        compiler_params=_params(None, [((R, LANES), F32)] * 16),
    )(w, m, v, parts)


SMALL_NAMES = ("b_gate", "ln_v_g", "ln_v_b", "w_s", "b_s", "sinks", "ln1_g", "ln1_b", "ln2_g", "ln2_b", "ln3_g", "ln3_b")
PACK_ALIGN = 8 * LANES


def _pack(arrays):
    flat = []
    for a in arrays:
        a = a.reshape(-1)
        flat.append(jnp.pad(a, (0, (-a.shape[0]) % PACK_ALIGN)))
    return jnp.concatenate(flat).reshape(-1, LANES)


def _unpack(packed, shapes):
    flat = packed.reshape(-1)
    out, pos = [], 0
    for s in shapes:
        n = 1
        for e in s:
            n *= e
        out.append(flat[pos : pos + n].reshape(s))
        pos += n + (-n) % PACK_ALIGN
    return out


BIG = (("in", "w_in", True), ("a", "w_br_a", True), ("b", "w_br_b", True), ("o", "w_o", False), ("xq", "w_xq", False),
       ("xkv", "w_xkv", False), ("xo", "w_xo", True), ("up", "w_up", True), ("down", "w_down", False))


SMALL_BIG = ("a", "b", "o", "xq", "xkv", "xo")
GATHER_PLAN = (
    (0, ("in",), None, None),
    (0, SMALL_BIG, ("proj_l0",), "gmlp_fwd_l0"),
    (0, ("up",), ("swa_fwd_l0", "merge_l0"), "oproj_ln_l0"),
    (0, ("down",), ("oproj_ln_l0", "xattn_fwd_l0"), "up_l0"),
    (1, ("in",), ("up_l0",), "down_ln_l0"),
    (1, SMALL_BIG, ("down_ln_l0",), "proj_l1"),
    (1, ("up",), ("proj_l1",), "gmlp_fwd_l1"),
    (1, ("down",), ("swa_fwd_l1", "merge_l1"), "oproj_ln_l1"),
)
EARLY_SMALL_BIG = ("o", "xq", "xkv", "xo")
LATE_SMALL_BIG = ("a", "b")
GRAD_HOSTS = {
    1: {"down": ("xattn_bwd_l1", "swa_bwd_l1"), "up": ("xattn_bwd_l1", "proj_dw_l1"),
        **{g: ("gmlp_bwd_l1", "proj_dx_l1") for g in EARLY_SMALL_BIG}, **{g: ("proj_dw_l1", "proj_dx_l1") for g in LATE_SMALL_BIG},
        "in": ("proj_dx_l1", "down_dx_l0")},
    0: {"down": ("xattn_bwd_l0", ("oproj_dx_l0", "gmlp_bwd_l0")), "up": ("xattn_bwd_l0", "swa_bwd_l0"),
        **{g: ("gmlp_bwd_l0", "proj_dw_l0") for g in EARLY_SMALL_BIG}, **{g: ("proj_dw_l0", "proj_dx_l0") for g in LATE_SMALL_BIG},
        "in": (None, "proj_dx_l0")},
}
SMALL_GRAD_HOSTS = ("proj_dw_l0", "proj_dx_l0")


def kernel(x, mem, w_in, b_gate, ln_v_g, ln_v_b, w_s, b_s, sinks, w_br_a, w_br_b, w_o, ln1_g, ln1_b, w_xq, w_xkv, w_xo, ln2_g, ln2_b, w_up, w_down, ln3_g, ln3_b, loss_target, m_w_in, m_b_gate, m_ln_v_g, m_ln_v_b, m_w_s, m_b_s, m_sinks, m_w_br_a, m_w_br_b, m_w_o, m_ln1_g, m_ln1_b, m_w_xq, m_w_xkv, m_w_xo, m_ln2_g, m_ln2_b, m_w_up, m_w_down, m_ln3_g, m_ln3_b, v_w_in, v_b_gate, v_ln_v_g, v_ln_v_b, v_w_s, v_b_s, v_sinks, v_w_br_a, v_w_br_b, v_w_o, v_ln1_g, v_ln1_b, v_w_xq, v_w_xkv, v_w_xo, v_ln2_g, v_ln2_b, v_w_up, v_w_down, v_ln3_g, v_ln3_b):
    given = dict(locals())
    T, D = x.shape[1], x.shape[2]
    IN, GW, AW, XW, DFF = w_in.shape[2] * N_DEV, ln_v_g.shape[1], w_br_b.shape[1], w_xq.shape[2], w_up.shape[2] * N_DEV
    KVW = (IN - 2 * GW - AW - 2 * D) // 2
    d = Dims(T=T, D=D, IN=IN, GW=GW, G=GW // GROUP_DIM, AW=AW, KVW=KVW, HQ=AW // HEAD_DIM, HKV=KVW // HEAD_DIM, XW=XW,
             XH=XW // X_HEAD_DIM, MEM=mem.shape[1], DFF=DFF)
    place = jnp.stack([lax.axis_index("x"), lax.axis_index("y"), lax.axis_index("c")]).astype(jnp.int32)

    sched = _Sched()
    big_of = {g: (p, tr) for g, p, tr in BIG}

    def shard(l, g):
        p, tr = big_of[g]
        return (given[p][l].T if tr else given[p][l]).astype(BF16)

    Gs = [{}, {}]

    def plan_gather(l, names, hosts1, host2):
        shards = [shard(l, g) for g in names]
        if hosts1 is None:
            Gs[l].update(zip(names, _all_gather(shards, name=f"gather_{names[0]}_l{l}")))
            return

        def register(k, into):
            def done(outs):
                if k + 1 < len(hosts1):
                    register(k + 1, outs)
                else:
                    sched.at(host2, _gather_stage2(outs, lambda full: Gs[l].update(zip(names, full))))

            sched.at(hosts1[k], _gather_stage1(shards, done, part=(k, len(hosts1)), into=into))

        register(0, None)

    for entry in GATHER_PLAN:
        plan_gather(*entry)
    Ss = []
    for l in range(DEPTH):
        S = {n: given[n][l].reshape(1, -1) for n in SMALL_NAMES if n not in ("w_s", "b_s", "sinks")}
        S["w_s"], S["b_sT"], S["sinks"] = w_s[l], b_s[l].T, sinks[l]
        Ss.append(S)

    owns, recvs = {}, {}

    smalls_seen = {}
    gathered_small = []

    def pack_small(src):
        parts = []
        for l in range(DEPTH):
            for n in SMALL_NAMES:
                a = src[l]["b_sT"].T if n == "b_s" else src[l][n]
                parts.append(a[0, : d.HQ] if n == "sinks" else a)
        return _pack(parts)

    def on_small(l, small):
        smalls_seen[l] = small
        if len(smalls_seen) == DEPTH:
            host1, host2 = SMALL_GRAD_HOSTS
            sched.at(host1, _gather_stage1([pack_small(smalls_seen)], lambda part: sched.at(
                host2, _gather_stage2(part, lambda full: gathered_small.append(full[0])))))

    def on_grad(l, g, grad):
        if g == "small":
            return on_small(l, grad)
        sib_host, chips_host = GRAD_HOSTS[l][g]

        def after_sibling(got):
            owns[(l, g)], send = _pair_sum(grad, got[0], place, name=f"grad_pair_sum_{g}_l{l}")
            hosts = chips_host if isinstance(chips_host, tuple) else (chips_host,)

            def register(k, into):
                def done(r):
                    if k + 1 < len(hosts):
                        register(k + 1, r)
                    else:
                        recvs[(l, g)] = r[0]

                task = _chips_stage([send], done, part=(k, len(hosts)), into=into)
                if hosts[k] is None:
                    _comm_only([task], name=f"grad_chips_{g}_l{l}")
                else:
                    sched.at(hosts[k], task)

            register(0, None)

        task = _sibling_stage([grad], after_sibling)
        if sib_host is None:
            _comm_only([task], name=f"grad_sibling_{g}_l{l}")
        else:
            sched.at(sib_host, task)

    loss, dX, bigs, smalls = _local_step(d, x[0], mem[0], loss_target[0], Gs, Ss, sched, on_grad)
    loss = lax.psum(loss[0, 0], ("x", "y", "c"))

    assert not sched.pending, sorted(sched.pending)

    def viewed(p, tr):
        return tr and given[p].shape[2] % LANES != 0

    res = {p: None for _, p, _ in BIG}
    for l in reversed(range(DEPTH)):
        for g, p, tr in BIG:
            view = viewed(p, tr)
            w3, m3, v3 = ((jnp.swapaxes(a, 1, 2) if view else a) for a in (given[p], given["m_" + p], given["v_" + p]))
            res[p] = _adam_big(w3, m3, v3, owns[(l, g)], recvs[(l, g)], l, tr and not view, res[p], name=f"adamw_{g}_l{l}")
    for g, p, tr in BIG:
        if viewed(p, tr):
            res[p] = [jnp.swapaxes(a, 1, 2) for a in res[p]]

    shapes = [given[n].shape[1:] for n in SMALL_NAMES]
    pw, pm, pv = (_pack([given[pre + n][l] for l in range(DEPTH) for n in SMALL_NAMES]) for pre in ("", "m_", "v_"))
    small_out = [_unpack(o, shapes * DEPTH) for o in _adam_small(pw, pm, pv, gathered_small[0], name="adamw_small")]

    names = ["w_in", "b_gate", "ln_v_g", "ln_v_b", "w_s", "b_s", "sinks", "w_br_a", "w_br_b", "w_o", "ln1_g", "ln1_b", "w_xq",
             "w_xkv", "w_xo", "ln2_g", "ln2_b", "w_up", "w_down", "ln3_g", "ln3_b"]
    out = [loss, dX[None]]
    for kind in range(4):
        for n in names:
            if n in SMALL_NAMES:
                i = SMALL_NAMES.index(n)
                out.append(jnp.stack([small_out[kind][l * len(SMALL_NAMES) + i] for l in range(DEPTH)]))
            else:
                out.append(res[n][kind])
    return tuple(out)
```

```python
import collections
import functools

import jax
import jax.numpy as jnp
from jax import lax
from jax.experimental import pallas as pl
from jax.experimental.pallas import tpu as pltpu

F32 = jnp.float32
BF16 = jnp.bfloat16
MESH = pl.DeviceIdType.MESH

N_DEV = 8
DEPTH = 2
CHUNK = 128
HEAD_DIM = 64
ROPE_HALF = HEAD_DIM // 2
X_HEAD_DIM = 128
GROUP_DIM = 128
LANES = 128
ROPE_THETA = 10000.0
LN_EPS = 1e-5
ALPHA = (2 * DEPTH) ** 0.25
ADAM_LR = 0.001
ADAM_B1 = 0.9
ADAM_B2 = 0.999
ADAM_EPS = 1e-08
ADAM_WD = 0.01
ADAM_STEP = 10
VMEM_BYTES = 64 * 1024 * 1024
VMEM_TEMP_BYTES = 20 * 1024 * 1024

Dims = collections.namedtuple("Dims", "T D IN GW G AW KVW HQ HKV XW XH MEM DFF")


def _offsets(d):
    off_v = d.GW
    off_q = 2 * d.GW
    off_k = off_q + d.AW
    off_va = off_k + d.KVW
    off_ga = off_va + d.KVW
    off_gb = off_ga + d.D
    return off_v, off_q, off_k, off_va, off_ga, off_gb


def _tile(dim, pref, align=LANES):
    if dim <= pref:
        return dim
    t = pref - pref % align
    while t >= align:
        if dim % t == 0:
            return t
        t -= align
    return dim


def _nbytes(shape, dtype):
    n = 1
    for s in shape:
        n *= s
    return n * jnp.dtype(dtype).itemsize


def _params(sem, blocks, scratch=0):
    need = 2 * sum(_nbytes(s, t) for s, t in blocks) + scratch + VMEM_TEMP_BYTES
    limit = min(max(need, 32 * 1024 * 1024), VMEM_BYTES - 4 * 1024 * 1024)
    return pltpu.CompilerParams(dimension_semantics=sem, vmem_limit_bytes=limit)


def _dot(a, b, kind):
    dn = {"nn": (((1,), (0,)), ((), ())), "nt": (((1,), (1,)), ((), ())), "tn": (((0,), (0,)), ((), ()))}[kind]
    return lax.dot_general(a, b, dn, preferred_element_type=F32)


def _gelu(x):
    c = 0.7978845608028654
    t = jnp.tanh(c * (x + 0.044715 * (x * x * x)))
    return 0.5 * x * (1.0 + t)


def _gelu_grad(x):
    c = 0.7978845608028654
    x2 = x * x
    t = jnp.tanh(c * (x + 0.044715 * (x2 * x)))
    return 0.5 * (1.0 + t) + 0.5 * x * (1.0 - t * t) * (c * (1.0 + 3.0 * 0.044715 * x2))


def _ln_fwd(z, g, b):
    mu = jnp.mean(z, axis=-1, keepdims=True)
    zc = z - mu
    var = jnp.mean(zc * zc, axis=-1, keepdims=True)
    rstd = lax.rsqrt(var + LN_EPS)
    xhat = zc * rstd
    return xhat * g + b, xhat, rstd


def _ln_bwd_vals(dy, xhat, rstd, g):
    gd = dy * g
    m1 = jnp.mean(gd, axis=-1, keepdims=True)
    m2 = jnp.mean(gd * xhat, axis=-1, keepdims=True)
    return rstd * (gd - m1 - xhat * m2)


def _acc_store(ref, val, first):
    @pl.when(first)
    def _():
        ref[...] = val

    @pl.when(jnp.logical_not(first))
    def _():
        ref[...] += val


def _matmul(a, b, kind, *, name, tm=512, tn=1024, tk=2048, out_defs=(("mn", F32),), epilogue=None, extras=(), sched=None):
    if kind == "nn":
        (M, K), (K2, N) = a.shape, b.shape
    elif kind == "nt":
        (M, K), (N, K2) = a.shape, b.shape
    else:
        (K, M), (K2, N) = a.shape, b.shape
    assert K == K2, (a.shape, b.shape, kind)
    tm, tn, tk = _tile(M, tm), _tile(N, tn), _tile(K, tk)
    nk = K // tk
    blocks = []

    def spec(shape, imap, dtype):
        blocks.append((shape, dtype))
        return pl.BlockSpec(shape, imap)

    if kind == "tn":
        a_spec = spec((tk, tm), lambda j, i, k: (k, i), a.dtype)
    else:
        a_spec = spec((tm, tk), lambda j, i, k: (i, k), a.dtype)
    if kind == "nt":
        b_spec = spec((tn, tk), lambda j, i, k: (j, k), b.dtype)
    else:
        b_spec = spec((tk, tn), lambda j, i, k: (k, j), b.dtype)
    in_specs = [a_spec, b_spec]
    operands = [a, b]
    for arr, ekind, off in extras:
        if ekind == "mn":
            assert off % tn == 0
            in_specs.append(spec((tm, tn), lambda j, i, k, o=off // tn: (i, j + o), arr.dtype))
        elif ekind == "row":
            assert off % tn == 0
            in_specs.append(spec((1, tn), lambda j, i, k, o=off // tn: (0, j + o), arr.dtype))
        else:
            in_specs.append(spec((tm, 1), lambda j, i, k: (i, 0), arr.dtype))
        operands.append(arr)
    out_shape, out_specs = [], []
    for okind, dt in out_defs:
        if okind == "mn":
            out_shape.append(jax.ShapeDtypeStruct((M, N), dt))
            out_specs.append(spec((tm, tn), lambda j, i, k: (i, j), dt))
        elif okind == "m1":
            assert N == tn
            out_shape.append(jax.ShapeDtypeStruct((M, 1), dt))
            out_specs.append(spec((tm, 1), lambda j, i, k: (i, 0), dt))
        else:
            out_shape.append(jax.ShapeDtypeStruct((1, N), F32))
            out_specs.append(spec((1, tn), lambda j, i, k: (0, j), F32))
    n_ex, n_out = len(extras), len(out_defs)
    ep = epilogue if epilogue is not None else (lambda acc: (acc,))

    def body(*refs):
        a_ref, b_ref = refs[0], refs[1]
        ex_refs = refs[2 : 2 + n_ex]
        out_refs = refs[2 + n_ex : 2 + n_ex + n_out]
        i = pl.program_id(1)
        k = pl.program_id(2)
        part = _dot(a_ref[...], b_ref[...], kind)

        def finish(acc):
            vals = ep(acc, *[r[...] for r in ex_refs])
            for (okind, dt), r, v in zip(out_defs, out_refs, vals):
                if okind == "acc":
                    _acc_store(r, v, i == 0)
                else:
                    r[...] = v.astype(dt)

        if nk == 1:
            finish(part)
        else:
            acc_ref = refs[-1]

            @pl.when(k == 0)
            def _():
                acc_ref[...] = part

            @pl.when(k > 0)
            def _():
                acc_ref[...] += part

            @pl.when(k == nk - 1)
            def _():
                finish(acc_ref[...])

    scratch = [pltpu.VMEM((tm, tn), F32)] if nk > 1 else []
    outs = _hosted(
        body,
        name=name,
        sched=sched,
        grid=(N // tn, M // tm, nk),
        in_specs=in_specs,
        out_specs=out_specs,
        out_shape=out_shape,
        scratch_shapes=scratch,
        compiler_params=_params(("arbitrary", "arbitrary", "arbitrary"), blocks, _nbytes((tm, tn), F32) if nk > 1 else 0),
    )(*operands)
    return outs if len(outs) > 1 else outs[0]


def _ep_resid_ln(acc, resid, g, b):
    y, xhat, rstd = _ln_fwd(ALPHA * resid + acc, g, b)
    return y, y, xhat, rstd


def _ep_resid(acc, resid):
    return (ALPHA * resid + acc,)


def _ep_relu2(acc):
    r = jnp.maximum(acc, 0.0)
    return acc, r * r


def _ep_relu2_bwd(acc, h):
    return (acc * (2.0 * jnp.maximum(h, 0.0)),)


def _ep_gate_bwd(acc, ga, gb, ya, yb, bga, bgb):
    sa = jax.nn.sigmoid(ga + bga)
    sb = jax.nn.sigmoid(gb + bgb)
    dga = acc * ya * (sa * (1.0 - sa))
    dgb = acc * yb * (sb * (1.0 - sb))
    return (acc * sa, acc * sb, dga, dgb, jnp.sum(dga, axis=0, keepdims=True), jnp.sum(dgb, axis=0, keepdims=True))


def _ln_bwd(dy, xhat, rstd, g, *, name, sched=None):
    T, D = dy.shape
    tm = _tile(T, 256)

    def body(dy_ref, xh_ref, rs_ref, g_ref, dz_ref, dzb_ref, dg_ref, db_ref):
        i = pl.program_id(0)
        dyv, xh = dy_ref[...], xh_ref[...]
        dz = _ln_bwd_vals(dyv, xh, rs_ref[...], g_ref[...])
        dz_ref[...] = dz
        dzb_ref[...] = dz.astype(BF16)
        _acc_store(dg_ref, jnp.sum(dyv * xh, axis=0, keepdims=True), i == 0)
        _acc_store(db_ref, jnp.sum(dyv, axis=0, keepdims=True), i == 0)

    row = pl.BlockSpec((tm, D), lambda i: (i, 0))
    vec = pl.BlockSpec((1, D), lambda i: (0, 0))
    return _hosted(
        body,
        name=name,
        sched=sched,
        grid=(T // tm,),
        in_specs=[row, row, pl.BlockSpec((tm, 1), lambda i: (i, 0)), vec],
        out_specs=[row, row, vec, vec],
        out_shape=[
            jax.ShapeDtypeStruct((T, D), F32),
            jax.ShapeDtypeStruct((T, D), BF16),
            jax.ShapeDtypeStruct((1, D), F32),
            jax.ShapeDtypeStruct((1, D), F32),
        ],
        compiler_params=_params(("arbitrary",), [((tm, D), F32)] * 4),
    )(dy, xhat, rstd, g)


def _loss_head(y, target, *, name, sched=None):
    T, D = y.shape
    tm = _tile(T, 256)

    def body(y_ref, t_ref, loss_ref, dy_ref):
        err = y_ref[...] - t_ref[...]
        dy_ref[...] = err * (1.0 / D)
        part = 0.5 * jnp.sum(jnp.sum(err * err, axis=1, keepdims=True) * (1.0 / D), axis=0, keepdims=True)
        _acc_store(loss_ref, part, pl.program_id(0) == 0)

    row = pl.BlockSpec((tm, D), lambda i: (i, 0))
    return _hosted(
        body,
        name=name,
        sched=sched,
        grid=(T // tm,),
        in_specs=[row, row],
        out_specs=[pl.BlockSpec((1, 1), lambda i: (0, 0)), row],
        out_shape=[jax.ShapeDtypeStruct((1, 1), F32), jax.ShapeDtypeStruct((T, D), F32)],
        compiler_params=_params(("arbitrary",), [((tm, D), F32)] * 3),
    )(y, target)


def _masked_mix_weights(wm_ref, G):
    r = lax.broadcasted_iota(jnp.int32, (CHUNK, CHUNK), 0)
    c = lax.broadcasted_iota(jnp.int32, (CHUNK, CHUNK), 1)
    return [jnp.where(c <= r, wm_ref[g], 0.0).astype(BF16) for g in range(G)]


def _gmlp_fwd(d, P, lnv_g, lnv_b, wm, bsT, *, name, sched=None):
    T, GW, G = d.T, d.GW, d.G
    tm = _tile(T, 256)
    nc = tm // CHUNK

    def body(u_ref, v_ref, g_ref, b_ref, wm_ref, bs_ref, o_ref):
        gu = _gelu(u_ref[...])
        vn, _, _ = _ln_fwd(_gelu(v_ref[...]), g_ref[...], b_ref[...])
        vn = vn.astype(BF16)
        wl = _masked_mix_weights(wm_ref, G)
        for c in range(nc):
            rows = slice(c * CHUNK, (c + 1) * CHUNK)
            for g in range(G):
                cols = slice(g * GROUP_DIM, (g + 1) * GROUP_DIM)
                mixed = _dot(wl[g], vn[rows, cols], "nn") + bs_ref[:, g : g + 1]
                o_ref[rows, cols] = (gu[rows, cols] * mixed).astype(BF16)

    return _hosted(
        body,
        name=name,
        sched=sched,
        grid=(T // tm,),
        in_specs=[
            pl.BlockSpec((tm, GW), lambda i: (i, 0)),
            pl.BlockSpec((tm, GW), lambda i: (i, 1)),
            pl.BlockSpec((1, GW), lambda i: (0, 0)),
            pl.BlockSpec((1, GW), lambda i: (0, 0)),
            pl.BlockSpec((G, CHUNK, CHUNK), lambda i: (0, 0, 0)),
            pl.BlockSpec((CHUNK, G), lambda i: (0, 0)),
        ],
        out_specs=pl.BlockSpec((tm, GW), lambda i: (i, 0)),
        out_shape=jax.ShapeDtypeStruct((T, GW), BF16),
        compiler_params=_params(("arbitrary",), [((tm, GW), F32)] * 3),
    )(P, P, lnv_g, lnv_b, wm, bsT)


def _gmlp_bwd(d, P, dya_b, Ga, lnv_g, lnv_b, wm, bsT, *, name, sched=None):
    T, D, GW, G = d.T, d.D, d.GW, d.G
    tm = _tile(T, 256)
    nc = tm // CHUNK

    def body(u_ref, v_ref, dya_ref, ga_ref, g_ref, b_ref, wm_ref, bs_ref, duv_ref, dwm_ref, dbs_ref, dlg_ref, dlb_ref, dgu_s, dvn_s):
        first = pl.program_id(0) == 0
        dsgu = _dot(dya_ref[...], ga_ref[...], "nn")
        u, v = u_ref[...], v_ref[...]
        gu = _gelu(u)
        lng = g_ref[...]
        vn, xh, rstd = _ln_fwd(_gelu(v), lng, b_ref[...])
        vn = vn.astype(BF16)
        wl = _masked_mix_weights(wm_ref, G)
        r = lax.broadcasted_iota(jnp.int32, (CHUNK, CHUNK), 0)
        cc = lax.broadcasted_iota(jnp.int32, (CHUNK, CHUNK), 1)
        lane_g = lax.broadcasted_iota(jnp.int32, (CHUNK, G), 1)
        dbs = jnp.zeros((CHUNK, G), F32)
        for g in range(G):
            cols = slice(g * GROUP_DIM, (g + 1) * GROUP_DIM)
            dw = jnp.zeros((CHUNK, CHUNK), F32)
            for c in range(nc):
                rows = slice(c * CHUNK, (c + 1) * CHUNK)
                mixed = _dot(wl[g], vn[rows, cols], "nn") + bs_ref[:, g : g + 1]
                ds = dsgu[rows, cols]
                dgu_s[rows, cols] = ds * mixed
                dmx = ds * gu[rows, cols]
                dbs = dbs + jnp.where(lane_g == g, jnp.sum(dmx, axis=1, keepdims=True), 0.0)
                dmx = dmx.astype(BF16)
                dw = dw + _dot(dmx, vn[rows, cols], "nt")
                dvn_s[rows, cols] = _dot(wl[g], dmx, "tn")
            _acc_store(dwm_ref.at[g], jnp.where(cc <= r, dw, 0.0), first)
        _acc_store(dbs_ref, dbs, first)
        dvn = dvn_s[...]
        _acc_store(dlg_ref, jnp.sum(dvn * xh, axis=0, keepdims=True), first)
        _acc_store(dlb_ref, jnp.sum(dvn, axis=0, keepdims=True), first)
        dgv = _ln_bwd_vals(dvn, xh, rstd, lng)
        duv_ref[:, :GW] = (dgu_s[...] * _gelu_grad(u)).astype(BF16)
        duv_ref[:, GW:] = (dgv * _gelu_grad(v)).astype(BF16)

    return _hosted(
        body,
        name=name,
        sched=sched,
        grid=(T // tm,),
        in_specs=[
            pl.BlockSpec((tm, GW), lambda i: (i, 0)),
            pl.BlockSpec((tm, GW), lambda i: (i, 1)),
            pl.BlockSpec((tm, D), lambda i: (i, 0)),
            pl.BlockSpec((D, GW), lambda i: (0, 0)),
            pl.BlockSpec((1, GW), lambda i: (0, 0)),
            pl.BlockSpec((1, GW), lambda i: (0, 0)),
            pl.BlockSpec((G, CHUNK, CHUNK), lambda i: (0, 0, 0)),
            pl.BlockSpec((CHUNK, G), lambda i: (0, 0)),
        ],
        out_specs=[
            pl.BlockSpec((tm, 2 * GW), lambda i: (i, 0)),
            pl.BlockSpec((G, CHUNK, CHUNK), lambda i: (0, 0, 0)),
            pl.BlockSpec((CHUNK, G), lambda i: (0, 0)),
            pl.BlockSpec((1, GW), lambda i: (0, 0)),
            pl.BlockSpec((1, GW), lambda i: (0, 0)),
        ],
        out_shape=[
            jax.ShapeDtypeStruct((T, 2 * GW), BF16),
            jax.ShapeDtypeStruct((G, CHUNK, CHUNK), F32),
            jax.ShapeDtypeStruct((CHUNK, G), F32),
            jax.ShapeDtypeStruct((1, GW), F32),
            jax.ShapeDtypeStruct((1, GW), F32),
        ],
        scratch_shapes=[pltpu.VMEM((tm, GW), F32), pltpu.VMEM((tm, GW), F32)],
        compiler_params=_params(
            ("arbitrary",), [((tm, GW), F32)] * 3 + [((tm, D), BF16), ((D, GW), BF16)], 2 * _nbytes((tm, GW), F32)
        ),
    )(P, P, dya_b, Ga, lnv_g, lnv_b, wm, bsT)


def _swap_halves(x):
    w = x.shape[1]
    lane = lax.broadcasted_iota(jnp.int32, x.shape, 1)
    return jnp.where((lane % HEAD_DIM) < ROPE_HALF, pltpu.roll(x, w - ROPE_HALF, 1), pltpu.roll(x, ROPE_HALF, 1))


def _lane_tile(t, width):
    reps = width // LANES
    return t if reps == 1 else jnp.tile(t, (1, reps))


def _rope(x, cos2, sin2):
    w = x.shape[1]
    return x * _lane_tile(cos2, w) + _swap_halves(x) * _lane_tile(sin2, w)


def _rope_bwd(g, cos2, sin2):
    w = g.shape[1]
    return g * _lane_tile(cos2, w) - _swap_halves(g) * _lane_tile(sin2, w)


PAIR = LANES // HEAD_DIM


def _swa_valid(i):
    s = lax.broadcasted_iota(jnp.int32, (2 * CHUNK, CHUNK), 0)
    t = lax.broadcasted_iota(jnp.int32, (2 * CHUNK, CHUNK), 1)
    cur = jnp.logical_and(s >= CHUNK, s - CHUNK <= t)
    prev = jnp.logical_and(jnp.logical_and(s < CHUNK, s > t), i > 0)
    return jnp.logical_or(cur, prev)


def _half_variants(x, odd):
    lane = lax.broadcasted_iota(jnp.int32, x.shape, 1)
    if odd:
        hi = jnp.where(lane >= HEAD_DIM, x, jnp.zeros_like(x))
        return pltpu.roll(hi, HEAD_DIM, 1), hi
    lo = jnp.where(lane < HEAD_DIM, x, jnp.zeros_like(x))
    return lo, pltpu.roll(lo, HEAD_DIM, 1)


def _swa_probs_t(kx, qp, sink, valid):
    s = jnp.where(valid, _dot(kx, qp, "nt") * (HEAD_DIM**-0.5), -jnp.inf)
    m = jnp.maximum(jnp.max(s, axis=0, keepdims=True), sink)
    e = jnp.exp(s - m)
    e_s = jnp.exp(sink - m)
    inv = 1.0 / (jnp.sum(e, axis=0, keepdims=True) + e_s)
    return e * inv, e_s * inv


def _swa_load(q_ref, kc_ref, kp_ref, vc_ref, vp_ref, cc, sc, cp, sp):
    qr = _rope(q_ref[...], cc, sc).astype(BF16)
    kcat = jnp.concatenate([_rope(kp_ref[...], cp, sp), _rope(kc_ref[...], cc, sc)], axis=0)
    vcat = jnp.concatenate([vp_ref[...], vc_ref[...]], axis=0)
    return qr, kcat, vcat


def _swa_specs(d):
    _, off_q, off_k, off_va, _, _ = _offsets(d)
    assert off_q % d.AW == 0 and off_k % d.KVW == 0 and off_va % d.KVW == 0
    prev = lambda i: jnp.maximum(i - 1, 0)
    return [
        pl.BlockSpec(memory_space=pltpu.SMEM),
        pl.BlockSpec((CHUNK, d.AW), lambda i, o=off_q // d.AW: (i, o)),
        pl.BlockSpec((CHUNK, d.KVW), lambda i, o=off_k // d.KVW: (i, o)),
        pl.BlockSpec((CHUNK, d.KVW), lambda i, o=off_k // d.KVW: (prev(i), o)),
        pl.BlockSpec((CHUNK, d.KVW), lambda i, o=off_va // d.KVW: (i, o)),
        pl.BlockSpec((CHUNK, d.KVW), lambda i, o=off_va // d.KVW: (prev(i), o)),
        pl.BlockSpec((CHUNK, LANES), lambda i: (i, 0)),
        pl.BlockSpec((CHUNK, LANES), lambda i: (i, 0)),
        pl.BlockSpec((CHUNK, LANES), lambda i: (prev(i), 0)),
        pl.BlockSpec((CHUNK, LANES), lambda i: (prev(i), 0)),
    ]


def _swa_fwd(d, P, cos2, sin2, sinks, *, name, sched=None):
    T, AW, HQ, HKV = d.T, d.AW, d.HQ, d.HKV
    grp = HQ // HKV
    assert grp % PAIR == 0 and HKV % PAIR == 0

    def body(sink_ref, q_ref, kc_ref, kp_ref, vc_ref, vp_ref, cc_ref, sc_ref, cp_ref, sp_ref, o_ref):
        i = pl.program_id(0)
        qr, kcat, vcat = _swa_load(q_ref, kc_ref, kp_ref, vc_ref, vp_ref, cc_ref[...], sc_ref[...], cp_ref[...], sp_ref[...])
        valid = _swa_valid(i)
        for kv in range(HKV):
            col = slice((kv // PAIR) * LANES, (kv // PAIR + 1) * LANES)
            kx = [t.astype(BF16) for t in _half_variants(kcat[:, col], kv % PAIR)]
            vx = [t.astype(BF16) for t in _half_variants(vcat[:, col], kv % PAIR)]
            for pp in range(grp // PAIR):
                p = kv * (grp // PAIR) + pp
                qs = slice(p * LANES, (p + 1) * LANES)
                out = jnp.zeros((CHUNK, LANES), F32)
                for half in range(PAIR):
                    pt, _ = _swa_probs_t(kx[half], qr[:, qs], sink_ref[PAIR * p + half], valid)
                    out = out + _dot(pt.astype(BF16), vx[half], "tn")
                o_ref[:, qs] = out.astype(BF16)

    return _hosted(
        body,
        name=name,
        sched=sched,
        grid=(T // CHUNK,),
        in_specs=_swa_specs(d),
        out_specs=pl.BlockSpec((CHUNK, AW), lambda i: (i, 0)),
        out_shape=jax.ShapeDtypeStruct((T, AW), BF16),
        compiler_params=_params(("arbitrary",), [((CHUNK, AW), F32)] * 3),
    )(sinks, P, P, P, P, P, cos2, sin2, cos2, sin2)


def _swa_bwd(d, P, dyb_b, Gb, cos2, sin2, sinks, *, name, sched=None):
    T, D, AW, KVW, HQ, HKV = d.T, d.D, d.AW, d.KVW, d.HQ, d.HKV
    grp = HQ // HKV
    nb = T // CHUNK
    scale = HEAD_DIM**-0.5
    assert grp % PAIR == 0 and HKV % PAIR == 0

    def body(sink_ref, q_ref, kc_ref, kp_ref, vc_ref, vp_ref, cc_ref, sc_ref, cp_ref, sp_ref, dyb_ref, gb_ref,
             dq_ref, dkv_ref, dsk_ref, acc_s, dq_s, dk_s, dv_s):
        i = pl.program_id(0)
        cc, sc, cp, sp = cc_ref[...], sc_ref[...], cp_ref[...], sp_ref[...]
        datt = _dot(dyb_ref[...], gb_ref[...], "nn").astype(BF16)
        qr, kcat, vcat = _swa_load(q_ref, kc_ref, kp_ref, vc_ref, vp_ref, cc, sc, cp, sp)
        valid = _swa_valid(i)
        lane1 = lax.broadcasted_iota(jnp.int32, (1, LANES), 1)
        lane2 = lax.broadcasted_iota(jnp.int32, (2 * CHUNK, LANES), 1)
        dsk = jnp.zeros((1, LANES), F32)
        for kvp in range(HKV // PAIR):
            col = slice(kvp * LANES, (kvp + 1) * LANES)
            dk_col = jnp.zeros((2 * CHUNK, LANES), F32)
            dv_col = jnp.zeros((2 * CHUNK, LANES), F32)
            for odd in range(PAIR):
                kv = kvp * PAIR + odd
                kx = [t.astype(BF16) for t in _half_variants(kcat[:, col], odd)]
                vx = [t.astype(BF16) for t in _half_variants(vcat[:, col], odd)]
                dk_half = [jnp.zeros((2 * CHUNK, LANES), F32) for _ in range(PAIR)]
                dv_half = [jnp.zeros((2 * CHUNK, LANES), F32) for _ in range(PAIR)]
                for pp in range(grp // PAIR):
                    p = kv * (grp // PAIR) + pp
                    qs = slice(p * LANES, (p + 1) * LANES)
                    qp, dop = qr[:, qs], datt[:, qs]
                    dq = jnp.zeros((CHUNK, LANES), F32)
                    for half in range(PAIR):
                        h = PAIR * p + half
                        pt, p_s = _swa_probs_t(kx[half], qp, sink_ref[h], valid)
                        dpt = _dot(vx[half], dop, "nt")
                        rs = jnp.sum(pt * dpt, axis=0, keepdims=True)
                        dst = (pt * (dpt - rs) * scale).astype(BF16)
                        dsk = dsk + jnp.where(lane1 == h, -jnp.sum(p_s * rs, axis=1, keepdims=True), 0.0)
                        dq = dq + _dot(dst, kx[half], "tn")
                        dk_half[half] = dk_half[half] + _dot(dst, qp, "nn")
                        dv_half[half] = dv_half[half] + _dot(pt.astype(BF16), dop, "nn")
                    dq_s[:, qs] = dq
                for acc_half, is_k in ((dk_half, True), (dv_half, False)):
                    lo = jnp.where(lane2 < HEAD_DIM, acc_half[0], 0.0)
                    hi = jnp.where(lane2 >= HEAD_DIM, acc_half[1], 0.0)
                    both = (pltpu.roll(lo, HEAD_DIM, 1) + hi) if odd else (lo + pltpu.roll(hi, HEAD_DIM, 1))
                    if is_k:
                        dk_col = dk_col + both
                    else:
                        dv_col = dv_col + both
            dk_s[:, col] = dk_col
            dv_s[:, col] = dv_col
        dq_ref[...] = _rope_bwd(dq_s[...], cc, sc).astype(BF16)
        cur = pl.ds(pl.multiple_of(i * CHUNK, CHUNK), CHUNK)
        acc_s[cur, :KVW] = _rope_bwd(dk_s[CHUNK:, :], cc, sc)
        acc_s[cur, KVW:] = dv_s[CHUNK:, :]

        @pl.when(i > 0)
        def _():
            prv = pl.ds(pl.multiple_of((i - 1) * CHUNK, CHUNK), CHUNK)
            acc_s[prv, :KVW] += _rope_bwd(dk_s[:CHUNK, :], cp, sp)
            acc_s[prv, KVW:] += dv_s[:CHUNK, :]

        _acc_store(dsk_ref, dsk, i == 0)

        @pl.when(i == nb - 1)
        def _():
            dkv_ref[...] = acc_s[...].astype(BF16)

    return _hosted(
        body,
        name=name,
        sched=sched,
        grid=(nb,),
        in_specs=_swa_specs(d) + [pl.BlockSpec((CHUNK, D), lambda i: (i, 0)), pl.BlockSpec((D, AW), lambda i: (0, 0))],
        out_specs=[
            pl.BlockSpec((CHUNK, AW), lambda i: (i, 0)),
            pl.BlockSpec((T, 2 * KVW), lambda i: (0, 0)),
            pl.BlockSpec((1, LANES), lambda i: (0, 0)),
        ],
        out_shape=[
            jax.ShapeDtypeStruct((T, AW), BF16),
            jax.ShapeDtypeStruct((T, 2 * KVW), BF16),
            jax.ShapeDtypeStruct((1, LANES), F32),
        ],
        scratch_shapes=[
            pltpu.VMEM((T, 2 * KVW), F32),
            pltpu.VMEM((CHUNK, AW), F32),
            pltpu.VMEM((2 * CHUNK, KVW), F32),
            pltpu.VMEM((2 * CHUNK, KVW), F32),
        ],
        compiler_params=_params(
            ("arbitrary",), [((CHUNK, AW), F32)] * 3 + [((D, AW), BF16), ((T, 2 * KVW), BF16)], _nbytes((T, 2 * KVW), F32)
        ),
    )(sinks, P, P, P, P, P, cos2, sin2, cos2, sin2, dyb_b, Gb)


def _branch_merge(d, sgu, att, Ga, Gb, P, b_gate, *, name, sched=None):
    T, D, GW, AW = d.T, d.D, d.GW, d.AW
    _, _, _, _, off_ga, off_gb = _offsets(d)
    tm, tn = _tile(T, 1024), _tile(D, 512)
    assert off_ga % tn == 0 and off_gb % tn == 0

    def body(s_ref, a_ref, ga_w, gb_w, ga_ref, gb_ref, bga_ref, bgb_ref, ya_ref, yb_ref, mg_ref):
        ya = _dot(s_ref[...], ga_w[...], "nt")
        yb = _dot(a_ref[...], gb_w[...], "nt")
        ya_ref[...] = ya
        yb_ref[...] = yb
        sa = jax.nn.sigmoid(ga_ref[...] + bga_ref[...])
        sb = jax.nn.sigmoid(gb_ref[...] + bgb_ref[...])
        mg_ref[...] = (sa * ya + sb * yb).astype(BF16)

    tile = pl.BlockSpec((tm, tn), lambda j, i: (i, j))
    return _hosted(
        body,
        name=name,
        sched=sched,
        grid=(D // tn, T // tm),
        in_specs=[
            pl.BlockSpec((tm, GW), lambda j, i: (i, 0)),
            pl.BlockSpec((tm, AW), lambda j, i: (i, 0)),
            pl.BlockSpec((tn, GW), lambda j, i: (j, 0)),
            pl.BlockSpec((tn, AW), lambda j, i: (j, 0)),
            pl.BlockSpec((tm, tn), lambda j, i, o=off_ga // tn: (i, j + o)),
            pl.BlockSpec((tm, tn), lambda j, i, o=off_gb // tn: (i, j + o)),
            pl.BlockSpec((1, tn), lambda j, i: (0, j)),
            pl.BlockSpec((1, tn), lambda j, i, o=D // tn: (0, j + o)),
        ],
        out_specs=[tile, tile, tile],
        out_shape=[jax.ShapeDtypeStruct((T, D), F32), jax.ShapeDtypeStruct((T, D), F32), jax.ShapeDtypeStruct((T, D), BF16)],
        compiler_params=_params(
            ("arbitrary", "arbitrary"),
            [((tm, GW), BF16), ((tm, AW), BF16), ((tn, GW), BF16), ((tn, AW), BF16)] + [((tm, tn), F32)] * 5,
        ),
    )(sgu, att, Ga, Gb, P, P, b_gate, b_gate)


def _xattn_probs(qh, kh):
    s = _dot(qh, kh, "nt") * (X_HEAD_DIM**-0.5)
    e = jnp.exp(s - jnp.max(s, axis=1, keepdims=True))
    return e * (1.0 / jnp.sum(e, axis=1, keepdims=True))


def _xattn_fwd(d, X, Xb, kv, Gxq, Gxo, ln_g, ln_b, *, name, sched=None):
    T, D, XW, XH, MEM = d.T, d.D, d.XW, d.XH, d.MEM
    tm = _tile(T, 256)

    def body(x_ref, xb_ref, kv_ref, wq_ref, wo_ref, g_ref, b_ref, y_ref, yb_ref, xh_ref, rs_ref, q_ref, o_ref, o_s):
        q = _dot(xb_ref[...], wq_ref[...], "nn").astype(BF16)
        q_ref[...] = q
        for h in range(XH):
            hs = slice(h * X_HEAD_DIM, (h + 1) * X_HEAD_DIM)
            p = _xattn_probs(q[:, hs], kv_ref[:, hs]).astype(BF16)
            o_s[:, hs] = _dot(p, kv_ref[:, XW + h * X_HEAD_DIM : XW + (h + 1) * X_HEAD_DIM], "nn").astype(BF16)
        o = o_s[...]
        o_ref[...] = o
        y, xhat, rstd = _ln_fwd(ALPHA * x_ref[...] + _dot(o, wo_ref[...], "nt"), g_ref[...], b_ref[...])
        y_ref[...] = y
        yb_ref[...] = y.astype(BF16)
        xh_ref[...] = xhat
        rs_ref[...] = rstd

    row = pl.BlockSpec((tm, D), lambda i: (i, 0))
    nar = pl.BlockSpec((tm, XW), lambda i: (i, 0))
    vec = pl.BlockSpec((1, D), lambda i: (0, 0))
    return _hosted(
        body,
        name=name,
        sched=sched,
        grid=(T // tm,),
        in_specs=[
            row,
            row,
            pl.BlockSpec((MEM, 2 * XW), lambda i: (0, 0)),
            pl.BlockSpec((D, XW), lambda i: (0, 0)),
            pl.BlockSpec((D, XW), lambda i: (0, 0)),
            vec,
            vec,
        ],
        out_specs=[row, row, row, pl.BlockSpec((tm, 1), lambda i: (i, 0)), nar, nar],
        out_shape=[
            jax.ShapeDtypeStruct((T, D), F32),
            jax.ShapeDtypeStruct((T, D), BF16),
            jax.ShapeDtypeStruct((T, D), F32),
            jax.ShapeDtypeStruct((T, 1), F32),
            jax.ShapeDtypeStruct((T, XW), BF16),
            jax.ShapeDtypeStruct((T, XW), BF16),
        ],
        scratch_shapes=[pltpu.VMEM((tm, XW), BF16)],
        compiler_params=_params(("arbitrary",), [((tm, D), F32)] * 4 + [((D, XW), BF16)] * 2),
    )(X, Xb, kv, Gxq, Gxo, ln_g, ln_b)


def _xattn_bwd(d, dz, dzb, q_b, kv, Gxq, Gxo, *, name, sched=None):
    T, D, XW, XH, MEM = d.T, d.D, d.XW, d.XH, d.MEM
    tm = _tile(T, 256)
    scale = X_HEAD_DIM**-0.5

    def body(dz_ref, dzb_ref, q_ref, kv_ref, wq_ref, wo_ref, dx_ref, dq_ref, dkv_ref, dq_s):
        first = pl.program_id(0) == 0
        do = _dot(dzb_ref[...], wo_ref[...], "nn").astype(BF16)
        for h in range(XH):
            hs = slice(h * X_HEAD_DIM, (h + 1) * X_HEAD_DIM)
            vs = slice(XW + h * X_HEAD_DIM, XW + (h + 1) * X_HEAD_DIM)
            kh, vh, qh, doh = kv_ref[:, hs], kv_ref[:, vs], q_ref[:, hs], do[:, hs]
            p = _xattn_probs(qh, kh)
            dp = _dot(doh, vh, "nt")
            ds = (p * (dp - jnp.sum(p * dp, axis=1, keepdims=True)) * scale).astype(BF16)
            dq_s[:, hs] = _dot(ds, kh, "nn").astype(BF16)
            _acc_store(dkv_ref.at[h], _dot(ds, qh, "tn"), first)
            _acc_store(dkv_ref.at[XH + h], _dot(p.astype(BF16), doh, "tn"), first)
        dq = dq_s[...]
        dq_ref[...] = dq
        dx_ref[...] = ALPHA * dz_ref[...] + _dot(dq, wq_ref[...], "nt")

    row = pl.BlockSpec((tm, D), lambda i: (i, 0))
    nar = pl.BlockSpec((tm, XW), lambda i: (i, 0))
    return _hosted(
        body,
        name=name,
        sched=sched,
        grid=(T // tm,),
        in_specs=[
            row,
            row,
            nar,
            pl.BlockSpec((MEM, 2 * XW), lambda i: (0, 0)),
            pl.BlockSpec((D, XW), lambda i: (0, 0)),
            pl.BlockSpec((D, XW), lambda i: (0, 0)),
        ],
        out_specs=[row, nar, pl.BlockSpec((2 * XH, MEM, X_HEAD_DIM), lambda i: (0, 0, 0))],
        out_shape=[
            jax.ShapeDtypeStruct((T, D), F32),
            jax.ShapeDtypeStruct((T, XW), BF16),
            jax.ShapeDtypeStruct((2 * XH, MEM, X_HEAD_DIM), F32),
        ],
        scratch_shapes=[pltpu.VMEM((tm, XW), BF16)],
        compiler_params=_params(("arbitrary",), [((tm, D), F32)] * 3 + [((D, XW), BF16)] * 2),
    )(dz, dzb, q_b, kv, Gxq, Gxo)


def _resid_ln_outs():
    return (("mn", F32), ("mn", BF16), ("mn", F32), ("m1", F32))


def _layer_fwd(d, X, Xb, memb, G, S, cos2, sin2, tag, sched=None):
    D = d.D
    mm = functools.partial(_matmul, sched=sched)
    P = mm(Xb, G["in"], "nt", name=f"proj_{tag}", tm=1024, tn=_tile(d.IN, 1280))
    sgu = _gmlp_fwd(d, P, S["ln_v_g"], S["ln_v_b"], S["w_s"], S["b_sT"], name=f"gmlp_fwd_{tag}", sched=sched)
    att = _swa_fwd(d, P, cos2, sin2, S["sinks"], name=f"swa_fwd_{tag}", sched=sched)
    ya, yb, merged = _branch_merge(d, sgu, att, G["a"], G["b"], P, S["b_gate"], name=f"merge_{tag}", sched=sched)
    X1, X1b, xh1, rs1 = mm(
        merged, G["o"], "nn", name=f"oproj_ln_{tag}", tn=D, tk=1024, out_defs=_resid_ln_outs(), epilogue=_ep_resid_ln,
        extras=((X, "mn", 0), (S["ln1_g"], "row", 0), (S["ln1_b"], "row", 0)),
    )
    kv = mm(memb, G["xkv"], "nn", name=f"xkv_{tag}", out_defs=(("mn", BF16),))
    X2, X2b, xh2, rs2, q_b, o_b = _xattn_fwd(d, X1, X1b, kv, G["xq"], G["xo"], S["ln2_g"], S["ln2_b"], name=f"xattn_fwd_{tag}", sched=sched)
    H, A = mm(X2b, G["up"], "nt", name=f"up_{tag}", tm=1024, out_defs=(("mn", F32), ("mn", BF16)), epilogue=_ep_relu2)
    X3, X3b, xh3, rs3 = mm(
        A, G["down"], "nn", name=f"down_ln_{tag}", tn=D, tk=1024, out_defs=_resid_ln_outs(), epilogue=_ep_resid_ln,
        extras=((X2, "mn", 0), (S["ln3_g"], "row", 0), (S["ln3_b"], "row", 0)),
    )
    saved = dict(Xb=Xb, P=P, sgu=sgu, att=att, ya=ya, yb=yb, merged=merged, X1b=X1b, xh1=xh1, rs1=rs1, kv=kv, q_b=q_b,
                 o_b=o_b, X2b=X2b, xh2=xh2, rs2=rs2, H=H, A=A, xh3=xh3, rs3=rs3)
    return X3, X3b, saved


def _layer_bwd(d, dXout, sv, memb, G, S, cos2, sin2, tag, sched=None, on_grad=None):
    D = d.D
    mm = functools.partial(_matmul, sched=sched)
    emit = on_grad if on_grad is not None else (lambda n, g: None)
    wide = dict(tm=1024, tn=D, tk=1024)
    _, _, _, _, off_ga, off_gb = _offsets(d)
    bf = (("mn", BF16),)
    dz3, dz3b, dg3, db3 = _ln_bwd(dXout, sv["xh3"], sv["rs3"], S["ln3_g"], name=f"ln3_bwd_{tag}", sched=sched)
    dHb = mm(dz3b, G["down"], "nt", name=f"down_dx_{tag}", tm=1024, out_defs=bf, epilogue=_ep_relu2_bwd, extras=((sv["H"], "mn", 0),))
    g_down = mm(sv["A"], dz3b, "tn", name=f"down_dw_{tag}", **wide)
    emit("down", g_down)
    dX2o = mm(dHb, G["up"], "nn", name=f"up_dx_{tag}", tn=D, epilogue=_ep_resid, extras=((dz3, "mn", 0),))
    g_up = mm(dHb, sv["X2b"], "tn", name=f"up_dw_{tag}", **wide)
    emit("up", g_up)
    dz2, dz2b, dg2, db2 = _ln_bwd(dX2o, sv["xh2"], sv["rs2"], S["ln2_g"], name=f"ln2_bwd_{tag}", sched=sched)
    dX1o, dq_b, dkv = _xattn_bwd(d, dz2, dz2b, sv["q_b"], sv["kv"], G["xq"], G["xo"], name=f"xattn_bwd_{tag}", sched=sched)
    g_xo = mm(dz2b, sv["o_b"], "tn", name=f"xo_dw_{tag}")
    emit("xo", g_xo)
    g_xq = mm(sv["X1b"], dq_b, "tn", name=f"xq_dw_{tag}")
    emit("xq", g_xq)
    dkv_b = dkv.transpose(1, 0, 2).reshape(d.MEM, 2 * d.XW).astype(BF16)
    g_xkv = mm(memb, dkv_b, "tn", name=f"xkv_dw_{tag}")
    emit("xkv", g_xkv)
    dz1, dz1b, dg1, db1 = _ln_bwd(dX1o, sv["xh1"], sv["rs1"], S["ln1_g"], name=f"ln1_bwd_{tag}", sched=sched)
    dya_b, dyb_b, dga_b, dgb_b, dbga, dbgb = mm(
        dz1b, G["o"], "nt", name=f"oproj_dx_{tag}", tm=1024, tn=512,
        out_defs=(("mn", BF16), ("mn", BF16), ("mn", BF16), ("mn", BF16), ("acc", F32), ("acc", F32)),
        epilogue=_ep_gate_bwd,
        extras=((sv["P"], "mn", off_ga), (sv["P"], "mn", off_gb), (sv["ya"], "mn", 0), (sv["yb"], "mn", 0),
                (S["b_gate"], "row", 0), (S["b_gate"], "row", D)),
    )
    g_o = mm(sv["merged"], dz1b, "tn", name=f"oproj_dw_{tag}", **wide)
    emit("o", g_o)
    duv_b, dwm, dbsT, dlvg, dlvb = _gmlp_bwd(d, sv["P"], dya_b, G["a"], S["ln_v_g"], S["ln_v_b"], S["w_s"], S["b_sT"], name=f"gmlp_bwd_{tag}", sched=sched)
    g_a = mm(dya_b, sv["sgu"], "tn", name=f"bra_dw_{tag}")
    emit("a", g_a)
    dqr_b, dkvr_b, dsk = _swa_bwd(d, sv["P"], dyb_b, G["b"], cos2, sin2, S["sinks"], name=f"swa_bwd_{tag}", sched=sched)
    g_b = mm(dyb_b, sv["att"], "tn", name=f"brb_dw_{tag}")
    emit("b", g_b)
    small = dict(b_gate=jnp.concatenate([dbga, dbgb], axis=1), ln_v_g=dlvg, ln_v_b=dlvb, w_s=dwm, b_sT=dbsT, sinks=dsk,
                 ln1_g=dg1, ln1_b=db1, ln2_g=dg2, ln2_b=db2, ln3_g=dg3, ln3_b=db3)
    emit("small", small)
    dP = jnp.concatenate([duv_b, dqr_b, dkvr_b, dga_b, dgb_b], axis=1)
    g_in = mm(dP, sv["Xb"], "tn", name=f"proj_dw_{tag}", **wide)
    emit("in", g_in)
    dXin = mm(dP, G["in"], "nn", name=f"proj_dx_{tag}", tn=D, tk=_tile(d.IN, 1920), epilogue=_ep_resid, extras=((dz1, "mn", 0),))
    big = {"in": g_in, "a": g_a, "b": g_b, "o": g_o, "xq": g_xq, "xkv": g_xkv, "xo": g_xo, "up": g_up, "down": g_down}
    return dXin, big, small


def _rope_tables(T):
    inv = 1.0 / (ROPE_THETA ** (jnp.arange(0, HEAD_DIM, 2, dtype=F32) / HEAD_DIM))
    ang = jnp.arange(T, dtype=F32)[:, None] * inv[None, :]
    cos, sin = jnp.cos(ang), jnp.sin(ang)
    reps = LANES // HEAD_DIM
    return jnp.concatenate([cos, cos] * reps, axis=1), jnp.concatenate([-sin, sin] * reps, axis=1)


def _local_step(d, x, mem, target, Gs, Ss, sched=None, on_grad=None):
    cos2, sin2 = _rope_tables(d.T)
    memb = mem.astype(BF16)
    X, Xb = x, x.astype(BF16)
    saved = []
    for l in range(DEPTH):
        X, Xb, sv = _layer_fwd(d, X, Xb, memb, Gs[l], Ss[l], cos2, sin2, f"l{l}", sched)
        saved.append(sv)
    loss, dX = _loss_head(X, target, name="loss_head", sched=sched)
    bigs, smalls = [None] * DEPTH, [None] * DEPTH
    for l in reversed(range(DEPTH)):
        emit = None if on_grad is None else functools.partial(on_grad, l)
        dX, bigs[l], smalls[l] = _layer_bwd(d, dX, saved[l], memb, Gs[l], Ss[l], cos2, sin2, f"l{l}", sched, emit)
    return loss, dX, bigs, smalls


def _place():
    x, y, c = lax.axis_index("x"), lax.axis_index("y"), lax.axis_index("c")
    return x, y, c, [(1 - x, y), (x, 1 - y), (1 - x, 1 - y)]


def _all_gather(shards, *, name, sched=None):
    n = len(shards)
    any_spec = pl.BlockSpec(memory_space=pl.ANY)

    def body(*refs):
        ins, outs = refs[:n], refs[n : 2 * n]
        send_sems, recv_sems, local_sems = refs[2 * n :]
        x, y, c, chips = _place()
        me, sibling = (x, y, c), (x, y, 1 - c)

        def rows(w, p):
            r = ins[w].shape[0]
            return outs[w].at[pl.ds((4 * p[0] + 2 * p[1] + p[2]) * r, r), :]

        def copy(w, k, block, to, src=None):
            return pltpu.make_async_remote_copy(
                src_ref=rows(w, block) if src is None else src, dst_ref=rows(w, block),
                send_sem=send_sems.at[7 * w + k], recv_sem=recv_sems.at[7 * w + k], device_id=to, device_id_type=MESH)

        mine = [pltpu.make_async_copy(ins[w], rows(w, me), local_sems.at[w]) for w in range(n)]
        for cp in mine:
            cp.start()
        first = []
        for w in range(n):
            first.append(copy(w, 0, me, sibling, src=ins[w]))
            first += [copy(w, 1 + j, me, (*chip, c), src=ins[w]) for j, chip in enumerate(chips)]
        for cp in first:
            cp.start()
        passed = []
        for j, chip in enumerate(chips):
            for w in range(n):
                copy(w, 1 + j, (*chip, c), me).wait_recv()
                fwd = copy(w, 4 + j, (*chip, c), sibling)
                fwd.start()
                passed.append(fwd)
        for w in range(n):
            copy(w, 0, sibling, me).wait_recv()
            for j, chip in enumerate(chips):
                copy(w, 4 + j, (*chip, 1 - c), me).wait_recv()
        for cp in first + passed:
            cp.wait_send()
        for cp in mine:
            cp.wait()

    return _hosted(
        body,
        name=name,
        sched=sched,
        in_specs=[any_spec] * n,
        out_specs=[any_spec] * n,
        out_shape=[jax.ShapeDtypeStruct((N_DEV * s.shape[0], s.shape[1]), s.dtype) for s in shards],
        scratch_shapes=[pltpu.SemaphoreType.DMA((7 * n,)), pltpu.SemaphoreType.DMA((7 * n,)), pltpu.SemaphoreType.DMA((n,))],
    )(*shards)


class _Task:
    def __init__(self, ins, out_shapes, n_remote, n_local, plan, done, aliases=None):
        self.ins, self.out_shapes, self.n_remote, self.n_local = list(ins), list(out_shapes), n_remote, n_local
        self.plan, self.done, self.aliases = plan, done, dict(aliases or {})


class _Sched:
    def __init__(self):
        self.pending = {}

    def at(self, host, task):
        self.pending.setdefault(host, []).append(task)

    def take(self, host):
        return self.pending.pop(host, [])


def _hosted(body, *, name, sched=None, tasks=(), grid=(), in_specs=None, out_specs=None, out_shape=(), scratch_shapes=(),
            compiler_params=None, input_output_aliases=None):
    tasks = list(tasks) + (sched.take(name) if sched is not None else [])
    scratch_shapes = list(scratch_shapes)
    kwargs = dict(name=name)
    core_aliases = dict(input_output_aliases or {})
    if grid:
        kwargs["grid"] = grid
    if compiler_params is not None:
        kwargs["compiler_params"] = compiler_params
    if not tasks:
        if in_specs is not None:
            kwargs.update(in_specs=in_specs, out_specs=out_specs)
        return pl.pallas_call(body, out_shape=out_shape, scratch_shapes=scratch_shapes, input_output_aliases=core_aliases, **kwargs)
    assert in_specs is not None and out_specs is not None, name
    single = not isinstance(out_shape, (list, tuple))
    core_shape = [out_shape] if single else list(out_shape)
    core_specs = [out_specs] if single else list(out_specs)
    in_specs = list(in_specs)
    n_in, n_out, n_scr = len(in_specs), len(core_shape), len(scratch_shapes)
    t_ins = [a for t in tasks for a in t.ins]
    t_outs = [s for t in tasks for s in t.out_shapes]
    n_rem = sum(t.n_remote for t in tasks)
    n_loc = sum(t.n_local for t in tasks)
    aliases, pos_in, pos_out = dict(core_aliases), n_in, n_out
    for t in tasks:
        for a, b in t.aliases.items():
            aliases[pos_in + a] = pos_out + b
        pos_in += len(t.ins)
        pos_out += len(t.out_shapes)
    any_spec = pl.BlockSpec(memory_space=pl.ANY)

    def wrapped(*refs):
        core_in, t_in = refs[:n_in], refs[n_in : n_in + len(t_ins)]
        o0 = n_in + len(t_ins)
        core_out, t_out = refs[o0 : o0 + n_out], refs[o0 + n_out : o0 + n_out + len(t_outs)]
        s0 = o0 + n_out + len(t_outs)
        core_scr = refs[s0 : s0 + n_scr]
        send_sems, recv_sems, local_sems = refs[s0 + n_scr :]
        remote, local = [], []
        pi = po = pr = pq = 0
        for t in tasks:
            r, q = t.plan(t_in[pi : pi + len(t.ins)], t_out[po : po + len(t.out_shapes)], send_sems, recv_sems, local_sems, pr, pq)
            assert len(r) == t.n_remote and len(q) == t.n_local
            remote += r
            local += q
            pi, po, pr, pq = pi + len(t.ins), po + len(t.out_shapes), pr + t.n_remote, pq + t.n_local

        def start():
            for cp in local + remote:
                cp.start()

        def finish():
            for cp in remote + local:
                cp.wait()

        if grid:
            ids = [pl.program_id(a) for a in range(len(grid))]
            first = functools.reduce(jnp.logical_and, [i == 0 for i in ids])
            last = functools.reduce(jnp.logical_and, [i == g - 1 for i, g in zip(ids, grid)])
            pl.when(first)(start)
            body(*core_in, *core_out, *core_scr)
            pl.when(last)(finish)
        else:
            start()
            body(*core_in, *core_out, *core_scr)
            finish()

    call = pl.pallas_call(
        wrapped,
        in_specs=in_specs + [any_spec] * len(t_ins),
        out_specs=core_specs + [any_spec] * len(t_outs),
        out_shape=core_shape + t_outs,
        scratch_shapes=scratch_shapes
        + [pltpu.SemaphoreType.DMA((n_rem,)), pltpu.SemaphoreType.DMA((n_rem,)), pltpu.SemaphoreType.DMA((max(n_loc, 1),))],
        input_output_aliases=aliases,
        **kwargs,
    )

    def run(*operands):
        outs = call(*operands, *t_ins)
        po = n_out
        for t in tasks:
            t.done(list(outs[po : po + len(t.out_shapes)]))
            po += len(t.out_shapes)
        return outs[0] if single else list(outs[:n_out])

    return run


def _comm_only(tasks, *, name):
    _hosted(lambda: None, name=name, tasks=tasks, in_specs=[], out_shape=[], out_specs=[])()


def _rows(ref, block, n_blocks):
    r = ref.shape[0] // n_blocks
    return ref.at[pl.ds(block * r, r), :]


def _remote(src, dst, send_sems, recv_sems, k, to):
    return pltpu.make_async_remote_copy(src_ref=src, dst_ref=dst, send_sem=send_sems.at[k], recv_sem=recv_sems.at[k],
                                        device_id=to, device_id_type=MESH)


def _gather_stage1(shards, done, part=(0, 1), into=None):
    n = len(shards)
    k, n_parts = part

    def plan(ins, outs, send_sems, recv_sems, local_sems, pr, pq):
        x, y, c, chips = _place()
        mine = 4 * x + 2 * y + c
        remote, local = [], []
        for w in range(n):
            r = shards[w].shape[0]
            rp = r // n_parts
            src = ins[w].at[pl.ds(k * rp, rp), :]
            dst = outs[w].at[pl.ds(mine * r + k * rp, rp), :]
            local.append(pltpu.make_async_copy(src, dst, local_sems.at[pq + w]))
            for j, to in enumerate([(x, y, 1 - c)] + [(*chip, c) for chip in chips]):
                remote.append(_remote(src, dst, send_sems, recv_sems, pr + 4 * w + j, to))
        return remote, local

    shapes = [jax.ShapeDtypeStruct((N_DEV * s.shape[0], s.shape[1]), s.dtype) for s in shards]
    if into is None:
        return _Task(shards, shapes, 4 * n, n, plan, done)
    return _Task(list(shards) + list(into), shapes, 4 * n, n, plan, done, aliases={n + w: w for w in range(n)})


def _gather_stage2(partial, done):
    n = len(partial)

    def plan(ins, outs, send_sems, recv_sems, local_sems, pr, pq):
        x, y, c, chips = _place()
        remote = []
        for w in range(n):
            for j, chip in enumerate(chips):
                rows = _rows(outs[w], 4 * chip[0] + 2 * chip[1] + c, N_DEV)
                remote.append(_remote(rows, rows, send_sems, recv_sems, pr + 3 * w + j, (x, y, 1 - c)))
        return remote, []

    shapes = [jax.ShapeDtypeStruct(p.shape, p.dtype) for p in partial]
    return _Task(partial, shapes, 3 * n, 0, plan, done, aliases={w: w for w in range(n)})


def _sibling_stage(grads, done):
    n = len(grads)

    def plan(ins, outs, send_sems, recv_sems, local_sems, pr, pq):
        x, y, c, _ = _place()
        remote = []
        for w in range(n):
            for k in range(4):
                src = _rows(ins[w], 4 * (k // 2) + 2 * (k % 2) + (1 - c), N_DEV)
                remote.append(_remote(src, _rows(outs[w], k, 4), send_sems, recv_sems, pr + 4 * w + k, (x, y, 1 - c)))
        return remote, []

    shapes = [jax.ShapeDtypeStruct((g.shape[0] // 2, g.shape[1]), g.dtype) for g in grads]
    return _Task(grads, shapes, 4 * n, 0, plan, done)


def _chips_stage(sends, done, part=(0, 1), into=None):
    n = len(sends)
    k, n_parts = part

    def plan(ins, outs, send_sems, recv_sems, local_sems, pr, pq):
        x, y, c, chips = _place()
        remote = []
        for w in range(n):
            r = sends[w].shape[0] // 3
            rp = r // n_parts
            for j, chip in enumerate(chips):
                rows = pl.ds(j * r + k * rp, rp)
                remote.append(_remote(ins[w].at[rows, :], outs[w].at[rows, :], send_sems, recv_sems, pr + 3 * w + j, (*chip, c)))
        return remote, []

    shapes = [jax.ShapeDtypeStruct(s.shape, s.dtype) for s in sends]
    if into is None:
        return _Task(sends, shapes, 3 * n, 0, plan, done)
    return _Task(list(sends) + list(into), shapes, 3 * n, 0, plan, done, aliases={n + w: w for w in range(n)})


def _pair_sum(grad, got, place, *, name):
    R, C = grad.shape
    r = R // N_DEV
    tr = _tile(r, 256, 16)
    nt = r // tr

    def chip_of(s, pr):
        fx = jnp.where((s == 1) | (s == 3), 1, 0)
        fy = jnp.where(s >= 2, 1, 0)
        return pr[0] ^ fx, pr[1] ^ fy

    def grad_map(s, t, pr):
        kx, ky = chip_of(s, pr)
        return ((4 * kx + 2 * ky + pr[2]) * nt + t, 0)

    def got_map(s, t, pr):
        kx, ky = chip_of(s, pr)
        return ((2 * kx + ky) * nt + t, 0)

    def body(pr, g_ref, r_ref, own_ref, send_ref):
        s = pl.program_id(0)
        tot = g_ref[...] + r_ref[...]

        @pl.when(s == 0)
        def _():
            own_ref[...] = tot

        @pl.when(s > 0)
        def _():
            send_ref[...] = tot.astype(BF16)

    return pl.pallas_call(
        body,
        name=name,
        grid_spec=pltpu.PrefetchScalarGridSpec(
            num_scalar_prefetch=1,
            grid=(4, nt),
            in_specs=[pl.BlockSpec((tr, C), grad_map), pl.BlockSpec((tr, C), got_map)],
            out_specs=[
                pl.BlockSpec((tr, C), lambda s, t, pr: (jnp.where(s == 0, t, nt - 1), 0)),
                pl.BlockSpec((tr, C), lambda s, t, pr: (jnp.where(s == 0, 0, (s - 1) * nt + t), 0)),
            ],
        ),
        out_shape=[jax.ShapeDtypeStruct((r, C), F32), jax.ShapeDtypeStruct((3 * r, C), BF16)],
        compiler_params=_params(("arbitrary", "arbitrary"), [((tr, C), F32)] * 4),
    )(place, grad, got)


def _adam_vals(w, g, m, v):
    m = ADAM_B1 * m + (1.0 - ADAM_B1) * g
    v = ADAM_B2 * v + (1.0 - ADAM_B2) * (g * g)
    m_hat = m / (1.0 - ADAM_B1**ADAM_STEP)
    v_hat = v / (1.0 - ADAM_B2**ADAM_STEP)
    delta = -ADAM_LR * (m_hat / (jnp.sqrt(v_hat) + ADAM_EPS) + ADAM_WD * w)
    return delta, m, v


def _adam_big(w3, m3, v3, own, got, layer, transposed, prev, *, name, sched=None):
    _, A, B = w3.shape
    ta = _tile(A, 256, 16)
    nt = A // ta
    b_pad = (-B) % LANES

    def body(*refs):
        w_ref, m_ref, v_ref, own_ref, g1_ref, g2_ref, g3_ref = refs[:7]
        g_ref, d_ref, nm_ref, nv_ref = refs[-4:]
        g = ((own_ref[...] + g1_ref[...].astype(F32)) + g2_ref[...].astype(F32)) + g3_ref[...].astype(F32)
        if transposed:
            if b_pad:
                g = jnp.concatenate([g, jnp.zeros((b_pad, ta), F32)], axis=0)
            g = g.T[:, :B]
        g_ref[...] = g
        d_ref[...], nm_ref[...], nv_ref[...] = _adam_vals(w_ref[...], g, m_ref[...], v_ref[...])

    nat = pl.BlockSpec((None, ta, B), lambda t: (layer, t, 0))
    if transposed:
        parts = [pl.BlockSpec((B, ta), lambda t: (0, t))] + [pl.BlockSpec((B, ta), lambda t, j=j: (j, t)) for j in range(3)]
    else:
        parts = [pl.BlockSpec((ta, B), lambda t: (t, 0))] + [pl.BlockSpec((ta, B), lambda t, j=j: (j * nt + t, 0)) for j in range(3)]
    keep = [] if prev is None else list(prev)
    outs = _hosted(
        body,
        name=name,
        sched=sched,
        grid=(nt,),
        in_specs=[nat, nat, nat] + parts + [pl.BlockSpec(memory_space=pl.ANY)] * len(keep),
        out_specs=[nat] * 4,
        out_shape=[jax.ShapeDtypeStruct(w3.shape, F32)] * 4,
        input_output_aliases={7 + k: k for k in range(len(keep))},
        compiler_params=_params(("arbitrary",), [((ta, B), F32)] * 10),
    )(w3, m3, v3, own, got, got, got, *keep)
    return list(outs)


def _adam_small(w, m, v, parts, *, name, sched=None):
    R = w.shape[0]

    def body(w_ref, m_ref, v_ref, p_ref, g_ref, d_ref, nm_ref, nv_ref):
        g = p_ref[pl.ds(0, R), :]
        for k in range(1, N_DEV):
            g = g + p_ref[pl.ds(k * R, R), :]
        g_ref[...] = g
        d_ref[...], nm_ref[...], nv_ref[...] = _adam_vals(w_ref[...], g, m_ref[...], v_ref[...])

    return _hosted(
        body,
        name=name,
        sched=sched,
        out_shape=[jax.ShapeDtypeStruct((R, LANES), F32)] * 4,
        compiler_params=_params(None, [((R, LANES), F32)] * 16),
    )(w, m, v, parts)


SMALL_NAMES = ("b_gate", "ln_v_g", "ln_v_b", "w_s", "b_s", "sinks", "ln1_g", "ln1_b", "ln2_g", "ln2_b", "ln3_g", "ln3_b")
PACK_ALIGN = 8 * LANES


def _pack(arrays):
    flat = []
    for a in arrays:
        a = a.reshape(-1)
        flat.append(jnp.pad(a, (0, (-a.shape[0]) % PACK_ALIGN)))
    return jnp.concatenate(flat).reshape(-1, LANES)


def _unpack(packed, shapes):
    flat = packed.reshape(-1)
    out, pos = [], 0
    for s in shapes:
        n = 1
        for e in s:
            n *= e
        out.append(flat[pos : pos + n].reshape(s))
        pos += n + (-n) % PACK_ALIGN
    return out


BIG = (("in", "w_in", True), ("a", "w_br_a", True), ("b", "w_br_b", True), ("o", "w_o", False), ("xq", "w_xq", False),
       ("xkv", "w_xkv", False), ("xo", "w_xo", True), ("up", "w_up", True), ("down", "w_down", False))


SMALL_BIG = ("a", "b", "o", "xq", "xkv", "xo")
GATHER_PLAN = (
    (0, ("in",), None, None),
    (0, SMALL_BIG, ("proj_l0",), "gmlp_fwd_l0"),
    (0, ("up",), ("swa_fwd_l0", "merge_l0"), "oproj_ln_l0"),
    (0, ("down",), ("oproj_ln_l0", "xattn_fwd_l0"), "up_l0"),
    (1, ("in",), ("up_l0",), "down_ln_l0"),
    (1, SMALL_BIG, ("down_ln_l0",), "proj_l1"),
    (1, ("up",), ("down_ln_l0", "proj_l1"), "gmlp_fwd_l1"),
    (1, ("down",), ("swa_fwd_l1", "merge_l1"), "oproj_ln_l1"),
)
EARLY_SMALL_BIG = ("o", "xq", "xkv", "xo")
LATE_SMALL_BIG = ("a", "b")
GRAD_HOSTS = {
    1: {"down": ("xattn_bwd_l1", "swa_bwd_l1"), "up": ("xattn_bwd_l1", "proj_dw_l1"),
        **{g: ("gmlp_bwd_l1", "proj_dx_l1") for g in EARLY_SMALL_BIG}, **{g: ("proj_dw_l1", "proj_dx_l1") for g in LATE_SMALL_BIG},
        "in": ("proj_dx_l1", "down_dx_l0")},
    0: {"down": ("xattn_bwd_l0", ("oproj_dx_l0", "gmlp_bwd_l0")), "up": ("xattn_bwd_l0", "swa_bwd_l0"),
        **{g: ("gmlp_bwd_l0", "proj_dw_l0") for g in EARLY_SMALL_BIG}, **{g: ("proj_dw_l0", "proj_dx_l0") for g in LATE_SMALL_BIG},
        "in": (None, "proj_dx_l0")},
}
SMALL_GRAD_HOSTS = ("proj_dw_l0", "proj_dx_l0")


def kernel(x, mem, w_in, b_gate, ln_v_g, ln_v_b, w_s, b_s, sinks, w_br_a, w_br_b, w_o, ln1_g, ln1_b, w_xq, w_xkv, w_xo, ln2_g, ln2_b, w_up, w_down, ln3_g, ln3_b, loss_target, m_w_in, m_b_gate, m_ln_v_g, m_ln_v_b, m_w_s, m_b_s, m_sinks, m_w_br_a, m_w_br_b, m_w_o, m_ln1_g, m_ln1_b, m_w_xq, m_w_xkv, m_w_xo, m_ln2_g, m_ln2_b, m_w_up, m_w_down, m_ln3_g, m_ln3_b, v_w_in, v_b_gate, v_ln_v_g, v_ln_v_b, v_w_s, v_b_s, v_sinks, v_w_br_a, v_w_br_b, v_w_o, v_ln1_g, v_ln1_b, v_w_xq, v_w_xkv, v_w_xo, v_ln2_g, v_ln2_b, v_w_up, v_w_down, v_ln3_g, v_ln3_b):
    given = dict(locals())
    T, D = x.shape[1], x.shape[2]
    IN, GW, AW, XW, DFF = w_in.shape[2] * N_DEV, ln_v_g.shape[1], w_br_b.shape[1], w_xq.shape[2], w_up.shape[2] * N_DEV
    KVW = (IN - 2 * GW - AW - 2 * D) // 2
    d = Dims(T=T, D=D, IN=IN, GW=GW, G=GW // GROUP_DIM, AW=AW, KVW=KVW, HQ=AW // HEAD_DIM, HKV=KVW // HEAD_DIM, XW=XW,
             XH=XW // X_HEAD_DIM, MEM=mem.shape[1], DFF=DFF)
    place = jnp.stack([lax.axis_index("x"), lax.axis_index("y"), lax.axis_index("c")]).astype(jnp.int32)

    sched = _Sched()
    big_of = {g: (p, tr) for g, p, tr in BIG}

    def shard(l, g):
        p, tr = big_of[g]
        return (given[p][l].T if tr else given[p][l]).astype(BF16)

    Gs = [{}, {}]

    def plan_gather(l, names, hosts1, host2):
        shards = [shard(l, g) for g in names]
        if hosts1 is None:
            Gs[l].update(zip(names, _all_gather(shards, name=f"gather_{names[0]}_l{l}")))
            return

        def register(k, into):
            def done(outs):
                if k + 1 < len(hosts1):
                    register(k + 1, outs)
                else:
                    sched.at(host2, _gather_stage2(outs, lambda full: Gs[l].update(zip(names, full))))

            sched.at(hosts1[k], _gather_stage1(shards, done, part=(k, len(hosts1)), into=into))

        register(0, None)

    for entry in GATHER_PLAN:
        plan_gather(*entry)
    Ss = []
    for l in range(DEPTH):
        S = {n: given[n][l].reshape(1, -1) for n in SMALL_NAMES if n not in ("w_s", "b_s", "sinks")}
        S["w_s"], S["b_sT"], S["sinks"] = w_s[l], b_s[l].T, sinks[l]
        Ss.append(S)

    owns, recvs = {}, {}

    smalls_seen = {}
    gathered_small = []

    def pack_small(src):
        parts = []
        for l in range(DEPTH):
            for n in SMALL_NAMES:
                a = src[l]["b_sT"].T if n == "b_s" else src[l][n]
                parts.append(a[0, : d.HQ] if n == "sinks" else a)
        return _pack(parts)

    def on_small(l, small):
        smalls_seen[l] = small
        if len(smalls_seen) == DEPTH:
            host1, host2 = SMALL_GRAD_HOSTS
            sched.at(host1, _gather_stage1([pack_small(smalls_seen)], lambda part: sched.at(
                host2, _gather_stage2(part, lambda full: gathered_small.append(full[0])))))

    def on_grad(l, g, grad):
        if g == "small":
            return on_small(l, grad)
        sib_host, chips_host = GRAD_HOSTS[l][g]

        def after_sibling(got):
            owns[(l, g)], send = _pair_sum(grad, got[0], place, name=f"grad_pair_sum_{g}_l{l}")
            hosts = chips_host if isinstance(chips_host, tuple) else (chips_host,)

            def register(k, into):
                def done(r):
                    if k + 1 < len(hosts):
                        register(k + 1, r)
                    else:
                        recvs[(l, g)] = r[0]

                task = _chips_stage([send], done, part=(k, len(hosts)), into=into)
                if hosts[k] is None:
                    _comm_only([task], name=f"grad_chips_{g}_l{l}")
                else:
                    sched.at(hosts[k], task)

            register(0, None)

        task = _sibling_stage([grad], after_sibling)
        if sib_host is None:
            _comm_only([task], name=f"grad_sibling_{g}_l{l}")
        else:
            sched.at(sib_host, task)

    loss, dX, bigs, smalls = _local_step(d, x[0], mem[0], loss_target[0], Gs, Ss, sched, on_grad)
    loss = lax.psum(loss[0, 0], ("x", "y", "c"))

    assert not sched.pending, sorted(sched.pending)

    def viewed(p, tr):
        return tr and given[p].shape[2] % LANES != 0

    res = {p: None for _, p, _ in BIG}
    for l in reversed(range(DEPTH)):
        for g, p, tr in BIG:
            view = viewed(p, tr)
            w3, m3, v3 = ((jnp.swapaxes(a, 1, 2) if view else a) for a in (given[p], given["m_" + p], given["v_" + p]))
            res[p] = _adam_big(w3, m3, v3, owns[(l, g)], recvs[(l, g)], l, tr and not view, res[p], name=f"adamw_{g}_l{l}")
    for g, p, tr in BIG:
        if viewed(p, tr):
            res[p] = [jnp.swapaxes(a, 1, 2) for a in res[p]]

    shapes = [given[n].shape[1:] for n in SMALL_NAMES]
    pw, pm, pv = (_pack([given[pre + n][l] for l in range(DEPTH) for n in SMALL_NAMES]) for pre in ("", "m_", "v_"))
    small_out = [_unpack(o, shapes * DEPTH) for o in _adam_small(pw, pm, pv, gathered_small[0], name="adamw_small")]

    names = ["w_in", "b_gate", "ln_v_g", "ln_v_b", "w_s", "b_s", "sinks", "w_br_a", "w_br_b", "w_o", "ln1_g", "ln1_b", "w_xq",
             "w_xkv", "w_xo", "ln2_g", "ln2_b", "w_up", "w_down", "ln3_g", "ln3_b"]
    out = [loss, dX[None]]
    for kind in range(4):
        for n in names:
            if n in SMALL_NAMES:
                i = SMALL_NAMES.index(n)
                out.append(jnp.stack([small_out[kind][l * len(SMALL_NAMES) + i] for l in range(DEPTH)]))
            else:
                out.append(res[n][kind])
    return tuple(out)
```

```python
import collections
import functools

import jax
import jax.numpy as jnp
from jax import lax
from jax.experimental import pallas as pl
from jax.experimental.pallas import tpu as pltpu

F32 = jnp.float32
BF16 = jnp.bfloat16
MESH = pl.DeviceIdType.MESH

N_DEV = 8
DEPTH = 2
CHUNK = 128
HEAD_DIM = 64
ROPE_HALF = HEAD_DIM // 2
X_HEAD_DIM = 128
GROUP_DIM = 128
LANES = 128
ROPE_THETA = 10000.0
LN_EPS = 1e-5
ALPHA = (2 * DEPTH) ** 0.25
ADAM_LR = 0.001
ADAM_B1 = 0.9
ADAM_B2 = 0.999
ADAM_EPS = 1e-08
ADAM_WD = 0.01
ADAM_STEP = 10
VMEM_BYTES = 64 * 1024 * 1024
VMEM_TEMP_BYTES = 20 * 1024 * 1024

Dims = collections.namedtuple("Dims", "T D IN GW G AW KVW HQ HKV XW XH MEM DFF")


def _offsets(d):
    off_v = d.GW
    off_q = 2 * d.GW
    off_k = off_q + d.AW
    off_va = off_k + d.KVW
    off_ga = off_va + d.KVW
    off_gb = off_ga + d.D
    return off_v, off_q, off_k, off_va, off_ga, off_gb


def _tile(dim, pref, align=LANES):
    if dim <= pref:
        return dim
    t = pref - pref % align
    while t >= align:
        if dim % t == 0:
            return t
        t -= align
    return dim


def _nbytes(shape, dtype):
    n = 1
    for s in shape:
        n *= s
    return n * jnp.dtype(dtype).itemsize


def _params(sem, blocks, scratch=0):
    need = 2 * sum(_nbytes(s, t) for s, t in blocks) + scratch + VMEM_TEMP_BYTES
    limit = min(max(need, 32 * 1024 * 1024), VMEM_BYTES - 4 * 1024 * 1024)
    return pltpu.CompilerParams(dimension_semantics=sem, vmem_limit_bytes=limit)


def _dot(a, b, kind):
    dn = {"nn": (((1,), (0,)), ((), ())), "nt": (((1,), (1,)), ((), ())), "tn": (((0,), (0,)), ((), ()))}[kind]
    return lax.dot_general(a, b, dn, preferred_element_type=F32)


def _gelu(x):
    c = 0.7978845608028654
    t = jnp.tanh(c * (x + 0.044715 * (x * x * x)))
    return 0.5 * x * (1.0 + t)


def _gelu_grad(x):
    c = 0.7978845608028654
    x2 = x * x
    t = jnp.tanh(c * (x + 0.044715 * (x2 * x)))
    return 0.5 * (1.0 + t) + 0.5 * x * (1.0 - t * t) * (c * (1.0 + 3.0 * 0.044715 * x2))


def _ln_fwd(z, g, b):
    mu = jnp.mean(z, axis=-1, keepdims=True)
    zc = z - mu
    var = jnp.mean(zc * zc, axis=-1, keepdims=True)
    rstd = lax.rsqrt(var + LN_EPS)
    xhat = zc * rstd
    return xhat * g + b, xhat, rstd


def _ln_bwd_vals(dy, xhat, rstd, g):
    gd = dy * g
    m1 = jnp.mean(gd, axis=-1, keepdims=True)
    m2 = jnp.mean(gd * xhat, axis=-1, keepdims=True)
    return rstd * (gd - m1 - xhat * m2)


def _acc_store(ref, val, first):
    @pl.when(first)
    def _():
        ref[...] = val

    @pl.when(jnp.logical_not(first))
    def _():
        ref[...] += val


def _matmul(a, b, kind, *, name, tm=512, tn=1024, tk=2048, out_defs=(("mn", F32),), epilogue=None, extras=(), sched=None):
    if kind == "nn":
        (M, K), (K2, N) = a.shape, b.shape
    elif kind == "nt":
        (M, K), (N, K2) = a.shape, b.shape
    else:
        (K, M), (K2, N) = a.shape, b.shape
    assert K == K2, (a.shape, b.shape, kind)
    tm, tn, tk = _tile(M, tm), _tile(N, tn), _tile(K, tk)
    nk = K // tk
    blocks = []

    def spec(shape, imap, dtype):
        blocks.append((shape, dtype))
        return pl.BlockSpec(shape, imap)

    if kind == "tn":
        a_spec = spec((tk, tm), lambda j, i, k: (k, i), a.dtype)
    else:
        a_spec = spec((tm, tk), lambda j, i, k: (i, k), a.dtype)
    if kind == "nt":
        b_spec = spec((tn, tk), lambda j, i, k: (j, k), b.dtype)
    else:
        b_spec = spec((tk, tn), lambda j, i, k: (k, j), b.dtype)
    in_specs = [a_spec, b_spec]
    operands = [a, b]
    for arr, ekind, off in extras:
        if ekind == "mn":
            assert off % tn == 0
            in_specs.append(spec((tm, tn), lambda j, i, k, o=off // tn: (i, j + o), arr.dtype))
        elif ekind == "row":
            assert off % tn == 0
            in_specs.append(spec((1, tn), lambda j, i, k, o=off // tn: (0, j + o), arr.dtype))
        else:
            in_specs.append(spec((tm, 1), lambda j, i, k: (i, 0), arr.dtype))
        operands.append(arr)
    out_shape, out_specs = [], []
    for okind, dt in out_defs:
        if okind == "mn":
            out_shape.append(jax.ShapeDtypeStruct((M, N), dt))
            out_specs.append(spec((tm, tn), lambda j, i, k: (i, j), dt))
        elif okind == "m1":
            assert N == tn
            out_shape.append(jax.ShapeDtypeStruct((M, 1), dt))
            out_specs.append(spec((tm, 1), lambda j, i, k: (i, 0), dt))
        else:
            out_shape.append(jax.ShapeDtypeStruct((1, N), F32))
            out_specs.append(spec((1, tn), lambda j, i, k: (0, j), F32))
    n_ex, n_out = len(extras), len(out_defs)
    ep = epilogue if epilogue is not None else (lambda acc: (acc,))
    in_place = nk > 1 and epilogue is None and tuple(out_defs) == (("mn", F32),)

    def body(*refs):
        a_ref, b_ref = refs[0], refs[1]
        ex_refs = refs[2 : 2 + n_ex]
        out_refs = refs[2 + n_ex : 2 + n_ex + n_out]
        i = pl.program_id(1)
        k = pl.program_id(2)

        def product():
            return _dot(a_ref[...], b_ref[...], kind)

        def finish(acc):
            vals = ep(acc, *[r[...] for r in ex_refs])
            for (okind, dt), r, v in zip(out_defs, out_refs, vals):
                if okind == "acc":
                    _acc_store(r, v, i == 0)
                else:
                    r[...] = v.astype(dt)

        if nk == 1:
            finish(product())
            return
        acc_ref = out_refs[0] if in_place else refs[-1]

        @pl.when(k == 0)
        def _():
            acc_ref[...] = product()

        if in_place:
            @pl.when(k > 0)
            def _():
                acc_ref[...] += product()
        else:
            if nk > 2:
                @pl.when(jnp.logical_and(k > 0, k < nk - 1))
                def _():
                    acc_ref[...] += product()

            @pl.when(k == nk - 1)
            def _():
                finish(acc_ref[...] + product())

    scratch = [pltpu.VMEM((tm, tn), F32)] if nk > 1 and not in_place else []
    outs = _hosted(
        body,
        name=name,
        sched=sched,
        grid=(N // tn, M // tm, nk),
        in_specs=in_specs,
        out_specs=out_specs,
        out_shape=out_shape,
        scratch_shapes=scratch,
        compiler_params=_params(("arbitrary", "arbitrary", "arbitrary"), blocks, _nbytes((tm, tn), F32) if scratch else 0),
    )(*operands)
    return outs if len(outs) > 1 else outs[0]


def _ep_resid_ln(acc, resid, g, b):
    y, xhat, rstd = _ln_fwd(ALPHA * resid + acc, g, b)
    return y, y, xhat, rstd


def _ep_resid(acc, resid):
    return (ALPHA * resid + acc,)


def _ep_relu2(acc):
    r = jnp.maximum(acc, 0.0)
    return acc, r * r


def _ep_relu2_bwd(acc, h):
    return (acc * (2.0 * jnp.maximum(h, 0.0)),)


def _ep_gate_bwd(acc, ga, gb, ya, yb, bga, bgb):
    sa = jax.nn.sigmoid(ga + bga)
    sb = jax.nn.sigmoid(gb + bgb)
    dga = acc * ya * (sa * (1.0 - sa))
    dgb = acc * yb * (sb * (1.0 - sb))
    return (acc * sa, acc * sb, dga, dgb, jnp.sum(dga, axis=0, keepdims=True), jnp.sum(dgb, axis=0, keepdims=True))


def _ln_bwd(dy, xhat, rstd, g, *, name, sched=None):
    T, D = dy.shape
    tm = _tile(T, 256)

    def body(dy_ref, xh_ref, rs_ref, g_ref, dz_ref, dzb_ref, dg_ref, db_ref):
        i = pl.program_id(0)
        dyv, xh = dy_ref[...], xh_ref[...]
        dz = _ln_bwd_vals(dyv, xh, rs_ref[...], g_ref[...])
        dz_ref[...] = dz
        dzb_ref[...] = dz.astype(BF16)
        _acc_store(dg_ref, jnp.sum(dyv * xh, axis=0, keepdims=True), i == 0)
        _acc_store(db_ref, jnp.sum(dyv, axis=0, keepdims=True), i == 0)

    row = pl.BlockSpec((tm, D), lambda i: (i, 0))
    vec = pl.BlockSpec((1, D), lambda i: (0, 0))
    return _hosted(
        body,
        name=name,
        sched=sched,
        grid=(T // tm,),
        in_specs=[row, row, pl.BlockSpec((tm, 1), lambda i: (i, 0)), vec],
        out_specs=[row, row, vec, vec],
        out_shape=[
            jax.ShapeDtypeStruct((T, D), F32),
            jax.ShapeDtypeStruct((T, D), BF16),
            jax.ShapeDtypeStruct((1, D), F32),
            jax.ShapeDtypeStruct((1, D), F32),
        ],
        compiler_params=_params(("arbitrary",), [((tm, D), F32)] * 4),
    )(dy, xhat, rstd, g)


def _loss_head(y, target, *, name, sched=None):
    T, D = y.shape
    tm = _tile(T, 256)

    def body(y_ref, t_ref, loss_ref, dy_ref):
        err = y_ref[...] - t_ref[...]
        dy_ref[...] = err * (1.0 / D)
        part = 0.5 * jnp.sum(jnp.sum(err * err, axis=1, keepdims=True) * (1.0 / D), axis=0, keepdims=True)
        _acc_store(loss_ref, part, pl.program_id(0) == 0)

    row = pl.BlockSpec((tm, D), lambda i: (i, 0))
    return _hosted(
        body,
        name=name,
        sched=sched,
        grid=(T // tm,),
        in_specs=[row, row],
        out_specs=[pl.BlockSpec((1, 1), lambda i: (0, 0)), row],
        out_shape=[jax.ShapeDtypeStruct((1, 1), F32), jax.ShapeDtypeStruct((T, D), F32)],
        compiler_params=_params(("arbitrary",), [((tm, D), F32)] * 3),
    )(y, target)


def _masked_mix_weights(wm_ref, G):
    r = lax.broadcasted_iota(jnp.int32, (CHUNK, CHUNK), 0)
    c = lax.broadcasted_iota(jnp.int32, (CHUNK, CHUNK), 1)
    return [jnp.where(c <= r, wm_ref[g], 0.0).astype(BF16) for g in range(G)]


def _gmlp_fwd(d, P, lnv_g, lnv_b, wm, bsT, *, name, sched=None):
    T, GW, G = d.T, d.GW, d.G
    tm = _tile(T, 256)
    nc = tm // CHUNK

    def body(u_ref, v_ref, g_ref, b_ref, wm_ref, bs_ref, o_ref):
        gu = _gelu(u_ref[...])
        vn, _, _ = _ln_fwd(_gelu(v_ref[...]), g_ref[...], b_ref[...])
        vn = vn.astype(BF16)
        wl = _masked_mix_weights(wm_ref, G)
        for c in range(nc):
            rows = slice(c * CHUNK, (c + 1) * CHUNK)
            for g in range(G):
                cols = slice(g * GROUP_DIM, (g + 1) * GROUP_DIM)
                mixed = _dot(wl[g], vn[rows, cols], "nn") + bs_ref[:, g : g + 1]
                o_ref[rows, cols] = (gu[rows, cols] * mixed).astype(BF16)

    return _hosted(
        body,
        name=name,
        sched=sched,
        grid=(T // tm,),
        in_specs=[
            pl.BlockSpec((tm, GW), lambda i: (i, 0)),
            pl.BlockSpec((tm, GW), lambda i: (i, 1)),
            pl.BlockSpec((1, GW), lambda i: (0, 0)),
            pl.BlockSpec((1, GW), lambda i: (0, 0)),
            pl.BlockSpec((G, CHUNK, CHUNK), lambda i: (0, 0, 0)),
            pl.BlockSpec((CHUNK, G), lambda i: (0, 0)),
        ],
        out_specs=pl.BlockSpec((tm, GW), lambda i: (i, 0)),
        out_shape=jax.ShapeDtypeStruct((T, GW), BF16),
        compiler_params=_params(("arbitrary",), [((tm, GW), F32)] * 3),
    )(P, P, lnv_g, lnv_b, wm, bsT)


def _gmlp_bwd(d, P, dya_b, Ga, lnv_g, lnv_b, wm, bsT, *, name, sched=None):
    T, D, GW, G = d.T, d.D, d.GW, d.G
    tm = _tile(T, 256)
    nc = tm // CHUNK

    def body(u_ref, v_ref, dya_ref, ga_ref, g_ref, b_ref, wm_ref, bs_ref, duv_ref, dwm_ref, dbs_ref, dlg_ref, dlb_ref, dgu_s, dvn_s):
        first = pl.program_id(0) == 0
        dsgu = _dot(dya_ref[...], ga_ref[...], "nn")
        u, v = u_ref[...], v_ref[...]
        gu = _gelu(u)
        lng = g_ref[...]
        vn, xh, rstd = _ln_fwd(_gelu(v), lng, b_ref[...])
        vn = vn.astype(BF16)
        wl = _masked_mix_weights(wm_ref, G)
        r = lax.broadcasted_iota(jnp.int32, (CHUNK, CHUNK), 0)
        cc = lax.broadcasted_iota(jnp.int32, (CHUNK, CHUNK), 1)
        lane_g = lax.broadcasted_iota(jnp.int32, (CHUNK, G), 1)
        dbs = jnp.zeros((CHUNK, G), F32)
        for g in range(G):
            cols = slice(g * GROUP_DIM, (g + 1) * GROUP_DIM)
            dw = jnp.zeros((CHUNK, CHUNK), F32)
            for c in range(nc):
                rows = slice(c * CHUNK, (c + 1) * CHUNK)
                mixed = _dot(wl[g], vn[rows, cols], "nn") + bs_ref[:, g : g + 1]
                ds = dsgu[rows, cols]
                dgu_s[rows, cols] = ds * mixed
                dmx = ds * gu[rows, cols]
                dbs = dbs + jnp.where(lane_g == g, jnp.sum(dmx, axis=1, keepdims=True), 0.0)
                dmx = dmx.astype(BF16)
                dw = dw + _dot(dmx, vn[rows, cols], "nt")
                dvn_s[rows, cols] = _dot(wl[g], dmx, "tn")
            _acc_store(dwm_ref.at[g], jnp.where(cc <= r, dw, 0.0), first)
        _acc_store(dbs_ref, dbs, first)
        dvn = dvn_s[...]
        _acc_store(dlg_ref, jnp.sum(dvn * xh, axis=0, keepdims=True), first)
        _acc_store(dlb_ref, jnp.sum(dvn, axis=0, keepdims=True), first)
        dgv = _ln_bwd_vals(dvn, xh, rstd, lng)
        duv_ref[:, :GW] = (dgu_s[...] * _gelu_grad(u)).astype(BF16)
        duv_ref[:, GW:] = (dgv * _gelu_grad(v)).astype(BF16)

    return _hosted(
        body,
        name=name,
        sched=sched,
        grid=(T // tm,),
        in_specs=[
            pl.BlockSpec((tm, GW), lambda i: (i, 0)),
            pl.BlockSpec((tm, GW), lambda i: (i, 1)),
            pl.BlockSpec((tm, D), lambda i: (i, 0)),
            pl.BlockSpec((D, GW), lambda i: (0, 0)),
            pl.BlockSpec((1, GW), lambda i: (0, 0)),
            pl.BlockSpec((1, GW), lambda i: (0, 0)),
            pl.BlockSpec((G, CHUNK, CHUNK), lambda i: (0, 0, 0)),
            pl.BlockSpec((CHUNK, G), lambda i: (0, 0)),
        ],
        out_specs=[
            pl.BlockSpec((tm, 2 * GW), lambda i: (i, 0)),
            pl.BlockSpec((G, CHUNK, CHUNK), lambda i: (0, 0, 0)),
            pl.BlockSpec((CHUNK, G), lambda i: (0, 0)),
            pl.BlockSpec((1, GW), lambda i: (0, 0)),
            pl.BlockSpec((1, GW), lambda i: (0, 0)),
        ],
        out_shape=[
            jax.ShapeDtypeStruct((T, 2 * GW), BF16),
            jax.ShapeDtypeStruct((G, CHUNK, CHUNK), F32),
            jax.ShapeDtypeStruct((CHUNK, G), F32),
            jax.ShapeDtypeStruct((1, GW), F32),
            jax.ShapeDtypeStruct((1, GW), F32),
        ],
        scratch_shapes=[pltpu.VMEM((tm, GW), F32), pltpu.VMEM((tm, GW), F32)],
        compiler_params=_params(
            ("arbitrary",), [((tm, GW), F32)] * 3 + [((tm, D), BF16), ((D, GW), BF16)], 2 * _nbytes((tm, GW), F32)
        ),
    )(P, P, dya_b, Ga, lnv_g, lnv_b, wm, bsT)


def _swap_halves(x):
    w = x.shape[1]
    lane = lax.broadcasted_iota(jnp.int32, x.shape, 1)
    return jnp.where((lane % HEAD_DIM) < ROPE_HALF, pltpu.roll(x, w - ROPE_HALF, 1), pltpu.roll(x, ROPE_HALF, 1))


def _lane_tile(t, width):
    reps = width // LANES
    return t if reps == 1 else jnp.tile(t, (1, reps))


def _rope(x, cos2, sin2):
    w = x.shape[1]
    return x * _lane_tile(cos2, w) + _swap_halves(x) * _lane_tile(sin2, w)


def _rope_bwd(g, cos2, sin2):
    w = g.shape[1]
    return g * _lane_tile(cos2, w) - _swap_halves(g) * _lane_tile(sin2, w)


PAIR = LANES // HEAD_DIM


def _swa_valid(i):
    s = lax.broadcasted_iota(jnp.int32, (2 * CHUNK, CHUNK), 0)
    t = lax.broadcasted_iota(jnp.int32, (2 * CHUNK, CHUNK), 1)
    cur = jnp.logical_and(s >= CHUNK, s - CHUNK <= t)
    prev = jnp.logical_and(jnp.logical_and(s < CHUNK, s > t), i > 0)
    return jnp.logical_or(cur, prev)


def _half_variants(x, odd):
    lane = lax.broadcasted_iota(jnp.int32, x.shape, 1)
    if odd:
        hi = jnp.where(lane >= HEAD_DIM, x, jnp.zeros_like(x))
        return pltpu.roll(hi, HEAD_DIM, 1), hi
    lo = jnp.where(lane < HEAD_DIM, x, jnp.zeros_like(x))
    return lo, pltpu.roll(lo, HEAD_DIM, 1)


def _swa_probs_t(kx, qp, sink, valid):
    s = jnp.where(valid, _dot(kx, qp, "nt") * (HEAD_DIM**-0.5), -jnp.inf)
    m = jnp.maximum(jnp.max(s, axis=0, keepdims=True), sink)
    e = jnp.exp(s - m)
    e_s = jnp.exp(sink - m)
    den = jnp.sum(e, axis=0, keepdims=True) + e_s
    return e / den, e_s / den


def _swa_load(q_ref, kc_ref, kp_ref, vc_ref, vp_ref, cc, sc, cp, sp):
    qr = _rope(q_ref[...], cc, sc).astype(BF16)
    kcat = jnp.concatenate([_rope(kp_ref[...], cp, sp), _rope(kc_ref[...], cc, sc)], axis=0)
    vcat = jnp.concatenate([vp_ref[...], vc_ref[...]], axis=0)
    return qr, kcat, vcat


def _swa_specs(d):
    _, off_q, off_k, off_va, _, _ = _offsets(d)
    assert off_q % d.AW == 0 and off_k % d.KVW == 0 and off_va % d.KVW == 0
    prev = lambda i: jnp.maximum(i - 1, 0)
    return [
        pl.BlockSpec(memory_space=pltpu.SMEM),
        pl.BlockSpec((CHUNK, d.AW), lambda i, o=off_q // d.AW: (i, o)),
        pl.BlockSpec((CHUNK, d.KVW), lambda i, o=off_k // d.KVW: (i, o)),
        pl.BlockSpec((CHUNK, d.KVW), lambda i, o=off_k // d.KVW: (prev(i), o)),
        pl.BlockSpec((CHUNK, d.KVW), lambda i, o=off_va // d.KVW: (i, o)),
        pl.BlockSpec((CHUNK, d.KVW), lambda i, o=off_va // d.KVW: (prev(i), o)),
        pl.BlockSpec((CHUNK, LANES), lambda i: (i, 0)),
        pl.BlockSpec((CHUNK, LANES), lambda i: (i, 0)),
        pl.BlockSpec((CHUNK, LANES), lambda i: (prev(i), 0)),
        pl.BlockSpec((CHUNK, LANES), lambda i: (prev(i), 0)),
    ]


def _swa_fwd(d, P, cos2, sin2, sinks, *, name, sched=None):
    T, AW, HQ, HKV = d.T, d.AW, d.HQ, d.HKV
    grp = HQ // HKV
    assert grp % PAIR == 0 and HKV % PAIR == 0

    def body(sink_ref, q_ref, kc_ref, kp_ref, vc_ref, vp_ref, cc_ref, sc_ref, cp_ref, sp_ref, o_ref):
        i = pl.program_id(0)
        qr, kcat, vcat = _swa_load(q_ref, kc_ref, kp_ref, vc_ref, vp_ref, cc_ref[...], sc_ref[...], cp_ref[...], sp_ref[...])
        valid = _swa_valid(i)
        for kv in range(HKV):
            col = slice((kv // PAIR) * LANES, (kv // PAIR + 1) * LANES)
            kx = [t.astype(BF16) for t in _half_variants(kcat[:, col], kv % PAIR)]
            vx = [t.astype(BF16) for t in _half_variants(vcat[:, col], kv % PAIR)]
            for pp in range(grp // PAIR):
                p = kv * (grp // PAIR) + pp
                qs = slice(p * LANES, (p + 1) * LANES)
                out = jnp.zeros((CHUNK, LANES), F32)
                for half in range(PAIR):
                    pt, _ = _swa_probs_t(kx[half], qr[:, qs], sink_ref[PAIR * p + half], valid)
                    out = out + _dot(pt.astype(BF16), vx[half], "tn")
                o_ref[:, qs] = out.astype(BF16)

    return _hosted(
        body,
        name=name,
        sched=sched,
        grid=(T // CHUNK,),
        in_specs=_swa_specs(d),
        out_specs=pl.BlockSpec((CHUNK, AW), lambda i: (i, 0)),
        out_shape=jax.ShapeDtypeStruct((T, AW), BF16),
        compiler_params=_params(("arbitrary",), [((CHUNK, AW), F32)] * 3),
    )(sinks, P, P, P, P, P, cos2, sin2, cos2, sin2)


def _swa_bwd(d, P, dyb_b, Gb, cos2, sin2, sinks, *, name, sched=None):
    T, D, AW, KVW, HQ, HKV = d.T, d.D, d.AW, d.KVW, d.HQ, d.HKV
    grp = HQ // HKV
    nb = T // CHUNK
    scale = HEAD_DIM**-0.5
    assert grp % PAIR == 0 and HKV % PAIR == 0

    def body(sink_ref, q_ref, kc_ref, kp_ref, vc_ref, vp_ref, cc_ref, sc_ref, cp_ref, sp_ref, dyb_ref, gb_ref,
             dq_ref, dkv_ref, dsk_ref, acc_s, dq_s, dk_s, dv_s):
        i = pl.program_id(0)
        cc, sc, cp, sp = cc_ref[...], sc_ref[...], cp_ref[...], sp_ref[...]
        datt = _dot(dyb_ref[...], gb_ref[...], "nn").astype(BF16)
        qr, kcat, vcat = _swa_load(q_ref, kc_ref, kp_ref, vc_ref, vp_ref, cc, sc, cp, sp)
        valid = _swa_valid(i)
        lane1 = lax.broadcasted_iota(jnp.int32, (1, LANES), 1)
        lane2 = lax.broadcasted_iota(jnp.int32, (2 * CHUNK, LANES), 1)
        dsk = jnp.zeros((1, LANES), F32)
        for kvp in range(HKV // PAIR):
            col = slice(kvp * LANES, (kvp + 1) * LANES)
            dk_col = jnp.zeros((2 * CHUNK, LANES), F32)
            dv_col = jnp.zeros((2 * CHUNK, LANES), F32)
            for odd in range(PAIR):
                kv = kvp * PAIR + odd
                kx = [t.astype(BF16) for t in _half_variants(kcat[:, col], odd)]
                vx = [t.astype(BF16) for t in _half_variants(vcat[:, col], odd)]
                dk_half = [jnp.zeros((2 * CHUNK, LANES), F32) for _ in range(PAIR)]
                dv_half = [jnp.zeros((2 * CHUNK, LANES), F32) for _ in range(PAIR)]
                for pp in range(grp // PAIR):
                    p = kv * (grp // PAIR) + pp
                    qs = slice(p * LANES, (p + 1) * LANES)
                    qp, dop = qr[:, qs], datt[:, qs]
                    dq = jnp.zeros((CHUNK, LANES), F32)
                    for half in range(PAIR):
                        h = PAIR * p + half
                        pt, p_s = _swa_probs_t(kx[half], qp, sink_ref[h], valid)
                        dpt = _dot(vx[half], dop, "nt")
                        rs = jnp.sum(pt * dpt, axis=0, keepdims=True)
                        dst = (pt * (dpt - rs) * scale).astype(BF16)
                        dsk = dsk + jnp.where(lane1 == h, -jnp.sum(p_s * rs, axis=1, keepdims=True), 0.0)
                        dq = dq + _dot(dst, kx[half], "tn")
                        dk_half[half] = dk_half[half] + _dot(dst, qp, "nn")
                        dv_half[half] = dv_half[half] + _dot(pt.astype(BF16), dop, "nn")
                    dq_s[:, qs] = dq
                for acc_half, is_k in ((dk_half, True), (dv_half, False)):
                    lo = jnp.where(lane2 < HEAD_DIM, acc_half[0], 0.0)
                    hi = jnp.where(lane2 >= HEAD_DIM, acc_half[1], 0.0)
                    both = (pltpu.roll(lo, HEAD_DIM, 1) + hi) if odd else (lo + pltpu.roll(hi, HEAD_DIM, 1))
                    if is_k:
                        dk_col = dk_col + both
                    else:
                        dv_col = dv_col + both
            dk_s[:, col] = dk_col
            dv_s[:, col] = dv_col
        dq_ref[...] = _rope_bwd(dq_s[...], cc, sc).astype(BF16)
        cur = pl.ds(pl.multiple_of(i * CHUNK, CHUNK), CHUNK)
        acc_s[cur, :KVW] = _rope_bwd(dk_s[CHUNK:, :], cc, sc)
        acc_s[cur, KVW:] = dv_s[CHUNK:, :]

        @pl.when(i > 0)
        def _():
            prv = pl.ds(pl.multiple_of((i - 1) * CHUNK, CHUNK), CHUNK)
            acc_s[prv, :KVW] += _rope_bwd(dk_s[:CHUNK, :], cp, sp)
            acc_s[prv, KVW:] += dv_s[:CHUNK, :]

        _acc_store(dsk_ref, dsk, i == 0)

        @pl.when(i == nb - 1)
        def _():
            dkv_ref[...] = acc_s[...].astype(BF16)

    return _hosted(
        body,
        name=name,
        sched=sched,
        grid=(nb,),
        in_specs=_swa_specs(d) + [pl.BlockSpec((CHUNK, D), lambda i: (i, 0)), pl.BlockSpec((D, AW), lambda i: (0, 0))],
        out_specs=[
            pl.BlockSpec((CHUNK, AW), lambda i: (i, 0)),
            pl.BlockSpec((T, 2 * KVW), lambda i: (0, 0)),
            pl.BlockSpec((1, LANES), lambda i: (0, 0)),
        ],
        out_shape=[
            jax.ShapeDtypeStruct((T, AW), BF16),
            jax.ShapeDtypeStruct((T, 2 * KVW), BF16),
            jax.ShapeDtypeStruct((1, LANES), F32),
        ],
        scratch_shapes=[
            pltpu.VMEM((T, 2 * KVW), F32),
            pltpu.VMEM((CHUNK, AW), F32),
            pltpu.VMEM((2 * CHUNK, KVW), F32),
            pltpu.VMEM((2 * CHUNK, KVW), F32),
        ],
        compiler_params=_params(
            ("arbitrary",), [((CHUNK, AW), F32)] * 3 + [((D, AW), BF16), ((T, 2 * KVW), BF16)], _nbytes((T, 2 * KVW), F32)
        ),
    )(sinks, P, P, P, P, P, cos2, sin2, cos2, sin2, dyb_b, Gb)


def _branch_merge(d, sgu, att, Ga, Gb, P, b_gate, *, name, sched=None):
    T, D, GW, AW = d.T, d.D, d.GW, d.AW
    _, _, _, _, off_ga, off_gb = _offsets(d)
    tm, tn = _tile(T, 1024), _tile(D, 512)
    assert off_ga % tn == 0 and off_gb % tn == 0

    def body(s_ref, a_ref, ga_w, gb_w, ga_ref, gb_ref, bga_ref, bgb_ref, ya_ref, yb_ref, mg_ref):
        ya = _dot(s_ref[...], ga_w[...], "nt")
        yb = _dot(a_ref[...], gb_w[...], "nt")
        ya_ref[...] = ya
        yb_ref[...] = yb
        sa = jax.nn.sigmoid(ga_ref[...] + bga_ref[...])
        sb = jax.nn.sigmoid(gb_ref[...] + bgb_ref[...])
        mg_ref[...] = (sa * ya + sb * yb).astype(BF16)

    tile = pl.BlockSpec((tm, tn), lambda j, i: (i, j))
    return _hosted(
        body,
        name=name,
        sched=sched,
        grid=(D // tn, T // tm),
        in_specs=[
            pl.BlockSpec((tm, GW), lambda j, i: (i, 0)),
            pl.BlockSpec((tm, AW), lambda j, i: (i, 0)),
            pl.BlockSpec((tn, GW), lambda j, i: (j, 0)),
            pl.BlockSpec((tn, AW), lambda j, i: (j, 0)),
            pl.BlockSpec((tm, tn), lambda j, i, o=off_ga // tn: (i, j + o)),
            pl.BlockSpec((tm, tn), lambda j, i, o=off_gb // tn: (i, j + o)),
            pl.BlockSpec((1, tn), lambda j, i: (0, j)),
            pl.BlockSpec((1, tn), lambda j, i, o=D // tn: (0, j + o)),
        ],
        out_specs=[tile, tile, tile],
        out_shape=[jax.ShapeDtypeStruct((T, D), F32), jax.ShapeDtypeStruct((T, D), F32), jax.ShapeDtypeStruct((T, D), BF16)],
        compiler_params=_params(
            ("arbitrary", "arbitrary"),
            [((tm, GW), BF16), ((tm, AW), BF16), ((tn, GW), BF16), ((tn, AW), BF16)] + [((tm, tn), F32)] * 5,
        ),
    )(sgu, att, Ga, Gb, P, P, b_gate, b_gate)


def _xattn_probs(qh, kh):
    s = _dot(qh, kh, "nt") * (X_HEAD_DIM**-0.5)
    e = jnp.exp(s - jnp.max(s, axis=1, keepdims=True))
    return e / jnp.sum(e, axis=1, keepdims=True)


def _xattn_fwd(d, X, Xb, kv, Gxq, Gxo, ln_g, ln_b, *, name, sched=None):
    T, D, XW, XH, MEM = d.T, d.D, d.XW, d.XH, d.MEM
    tm = _tile(T, 256)

    def body(x_ref, xb_ref, kv_ref, wq_ref, wo_ref, g_ref, b_ref, y_ref, yb_ref, xh_ref, rs_ref, q_ref, o_ref, o_s):
        q = _dot(xb_ref[...], wq_ref[...], "nn").astype(BF16)
        q_ref[...] = q
        for h in range(XH):
            hs = slice(h * X_HEAD_DIM, (h + 1) * X_HEAD_DIM)
            p = _xattn_probs(q[:, hs], kv_ref[:, hs]).astype(BF16)
            o_s[:, hs] = _dot(p, kv_ref[:, XW + h * X_HEAD_DIM : XW + (h + 1) * X_HEAD_DIM], "nn").astype(BF16)
        o = o_s[...]
        o_ref[...] = o
        y, xhat, rstd = _ln_fwd(ALPHA * x_ref[...] + _dot(o, wo_ref[...], "nt"), g_ref[...], b_ref[...])
        y_ref[...] = y
        yb_ref[...] = y.astype(BF16)
        xh_ref[...] = xhat
        rs_ref[...] = rstd

    row = pl.BlockSpec((tm, D), lambda i: (i, 0))
    nar = pl.BlockSpec((tm, XW), lambda i: (i, 0))
    vec = pl.BlockSpec((1, D), lambda i: (0, 0))
    return _hosted(
        body,
        name=name,
        sched=sched,
        grid=(T // tm,),
        in_specs=[
            row,
            row,
            pl.BlockSpec((MEM, 2 * XW), lambda i: (0, 0)),
            pl.BlockSpec((D, XW), lambda i: (0, 0)),
            pl.BlockSpec((D, XW), lambda i: (0, 0)),
            vec,
            vec,
        ],
        out_specs=[row, row, row, pl.BlockSpec((tm, 1), lambda i: (i, 0)), nar, nar],
        out_shape=[
            jax.ShapeDtypeStruct((T, D), F32),
            jax.ShapeDtypeStruct((T, D), BF16),
            jax.ShapeDtypeStruct((T, D), F32),
            jax.ShapeDtypeStruct((T, 1), F32),
            jax.ShapeDtypeStruct((T, XW), BF16),
            jax.ShapeDtypeStruct((T, XW), BF16),
        ],
        scratch_shapes=[pltpu.VMEM((tm, XW), BF16)],
        compiler_params=_params(("arbitrary",), [((tm, D), F32)] * 4 + [((D, XW), BF16)] * 2),
    )(X, Xb, kv, Gxq, Gxo, ln_g, ln_b)


def _xattn_bwd(d, dz, dzb, q_b, kv, Gxq, Gxo, *, name, sched=None):
    T, D, XW, XH, MEM = d.T, d.D, d.XW, d.XH, d.MEM
    tm = _tile(T, 256)
    scale = X_HEAD_DIM**-0.5

    def body(dz_ref, dzb_ref, q_ref, kv_ref, wq_ref, wo_ref, dx_ref, dq_ref, dkv_ref, dq_s):
        first = pl.program_id(0) == 0
        do = _dot(dzb_ref[...], wo_ref[...], "nn").astype(BF16)
        for h in range(XH):
            hs = slice(h * X_HEAD_DIM, (h + 1) * X_HEAD_DIM)
            vs = slice(XW + h * X_HEAD_DIM, XW + (h + 1) * X_HEAD_DIM)
            kh, vh, qh, doh = kv_ref[:, hs], kv_ref[:, vs], q_ref[:, hs], do[:, hs]
            p = _xattn_probs(qh, kh)
            dp = _dot(doh, vh, "nt")
            ds = (p * (dp - jnp.sum(p * dp, axis=1, keepdims=True)) * scale).astype(BF16)
            dq_s[:, hs] = _dot(ds, kh, "nn").astype(BF16)
            _acc_store(dkv_ref.at[h], _dot(ds, qh, "tn"), first)
            _acc_store(dkv_ref.at[XH + h], _dot(p.astype(BF16), doh, "tn"), first)
        dq = dq_s[...]
        dq_ref[...] = dq
        dx_ref[...] = ALPHA * dz_ref[...] + _dot(dq, wq_ref[...], "nt")

    row = pl.BlockSpec((tm, D), lambda i: (i, 0))
    nar = pl.BlockSpec((tm, XW), lambda i: (i, 0))
    return _hosted(
        body,
        name=name,
        sched=sched,
        grid=(T // tm,),
        in_specs=[
            row,
            row,
            nar,
            pl.BlockSpec((MEM, 2 * XW), lambda i: (0, 0)),
            pl.BlockSpec((D, XW), lambda i: (0, 0)),
            pl.BlockSpec((D, XW), lambda i: (0, 0)),
        ],
        out_specs=[row, nar, pl.BlockSpec((2 * XH, MEM, X_HEAD_DIM), lambda i: (0, 0, 0))],
        out_shape=[
            jax.ShapeDtypeStruct((T, D), F32),
            jax.ShapeDtypeStruct((T, XW), BF16),
            jax.ShapeDtypeStruct((2 * XH, MEM, X_HEAD_DIM), F32),
        ],
        scratch_shapes=[pltpu.VMEM((tm, XW), BF16)],
        compiler_params=_params(("arbitrary",), [((tm, D), F32)] * 3 + [((D, XW), BF16)] * 2),
    )(dz, dzb, q_b, kv, Gxq, Gxo)


def _resid_ln_outs():
    return (("mn", F32), ("mn", BF16), ("mn", F32), ("m1", F32))


def _layer_fwd(d, X, Xb, memb, G, S, cos2, sin2, tag, sched=None):
    D = d.D
    mm = functools.partial(_matmul, sched=sched)
    P = mm(Xb, G["in"], "nt", name=f"proj_{tag}", tm=1024, tn=_tile(d.IN, 1280))
    sgu = _gmlp_fwd(d, P, S["ln_v_g"], S["ln_v_b"], S["w_s"], S["b_sT"], name=f"gmlp_fwd_{tag}", sched=sched)
    att = _swa_fwd(d, P, cos2, sin2, S["sinks"], name=f"swa_fwd_{tag}", sched=sched)
    ya, yb, merged = _branch_merge(d, sgu, att, G["a"], G["b"], P, S["b_gate"], name=f"merge_{tag}", sched=sched)
    X1, X1b, xh1, rs1 = mm(
        merged, G["o"], "nn", name=f"oproj_ln_{tag}", tn=D, tk=1024, out_defs=_resid_ln_outs(), epilogue=_ep_resid_ln,
        extras=((X, "mn", 0), (S["ln1_g"], "row", 0), (S["ln1_b"], "row", 0)),
    )
    kv = mm(memb, G["xkv"], "nn", name=f"xkv_{tag}", out_defs=(("mn", BF16),))
    X2, X2b, xh2, rs2, q_b, o_b = _xattn_fwd(d, X1, X1b, kv, G["xq"], G["xo"], S["ln2_g"], S["ln2_b"], name=f"xattn_fwd_{tag}", sched=sched)
    H, A = mm(X2b, G["up"], "nt", name=f"up_{tag}", tm=1024, out_defs=(("mn", F32), ("mn", BF16)), epilogue=_ep_relu2)
    X3, X3b, xh3, rs3 = mm(
        A, G["down"], "nn", name=f"down_ln_{tag}", tn=D, tk=1024, out_defs=_resid_ln_outs(), epilogue=_ep_resid_ln,
        extras=((X2, "mn", 0), (S["ln3_g"], "row", 0), (S["ln3_b"], "row", 0)),
    )
    saved = dict(Xb=Xb, P=P, sgu=sgu, att=att, ya=ya, yb=yb, merged=merged, X1b=X1b, xh1=xh1, rs1=rs1, kv=kv, q_b=q_b,
                 o_b=o_b, X2b=X2b, xh2=xh2, rs2=rs2, H=H, A=A, xh3=xh3, rs3=rs3)
    return X3, X3b, saved


def _layer_bwd(d, dXout, sv, memb, G, S, cos2, sin2, tag, sched=None, on_grad=None):
    D = d.D
    mm = functools.partial(_matmul, sched=sched)
    emit = on_grad if on_grad is not None else (lambda n, g: None)
    wide = dict(tn=D)
    _, _, _, _, off_ga, off_gb = _offsets(d)
    bf = (("mn", BF16),)
    dz3, dz3b, dg3, db3 = _ln_bwd(dXout, sv["xh3"], sv["rs3"], S["ln3_g"], name=f"ln3_bwd_{tag}", sched=sched)
    dHb = mm(dz3b, G["down"], "nt", name=f"down_dx_{tag}", tm=1024, out_defs=bf, epilogue=_ep_relu2_bwd, extras=((sv["H"], "mn", 0),))
    g_down = mm(sv["A"], dz3b, "tn", name=f"down_dw_{tag}", **wide)
    emit("down", g_down)
    dX2o = mm(dHb, G["up"], "nn", name=f"up_dx_{tag}", tn=D, epilogue=_ep_resid, extras=((dz3, "mn", 0),))
    g_up = mm(dHb, sv["X2b"], "tn", name=f"up_dw_{tag}", **wide)
    emit("up", g_up)
    dz2, dz2b, dg2, db2 = _ln_bwd(dX2o, sv["xh2"], sv["rs2"], S["ln2_g"], name=f"ln2_bwd_{tag}", sched=sched)
    dX1o, dq_b, dkv = _xattn_bwd(d, dz2, dz2b, sv["q_b"], sv["kv"], G["xq"], G["xo"], name=f"xattn_bwd_{tag}", sched=sched)
    g_xo = mm(dz2b, sv["o_b"], "tn", name=f"xo_dw_{tag}")
    emit("xo", g_xo)
    g_xq = mm(sv["X1b"], dq_b, "tn", name=f"xq_dw_{tag}")
    emit("xq", g_xq)
    dkv_b = dkv.transpose(1, 0, 2).reshape(d.MEM, 2 * d.XW).astype(BF16)
    g_xkv = mm(memb, dkv_b, "tn", name=f"xkv_dw_{tag}")
    emit("xkv", g_xkv)
    dz1, dz1b, dg1, db1 = _ln_bwd(dX1o, sv["xh1"], sv["rs1"], S["ln1_g"], name=f"ln1_bwd_{tag}", sched=sched)
    dya_b, dyb_b, dga_b, dgb_b, dbga, dbgb = mm(
        dz1b, G["o"], "nt", name=f"oproj_dx_{tag}", tm=1024, tn=512,
        out_defs=(("mn", BF16), ("mn", BF16), ("mn", BF16), ("mn", BF16), ("acc", F32), ("acc", F32)),
        epilogue=_ep_gate_bwd,
        extras=((sv["P"], "mn", off_ga), (sv["P"], "mn", off_gb), (sv["ya"], "mn", 0), (sv["yb"], "mn", 0),
                (S["b_gate"], "row", 0), (S["b_gate"], "row", D)),
    )
    g_o = mm(sv["merged"], dz1b, "tn", name=f"oproj_dw_{tag}", **wide)
    emit("o", g_o)
    duv_b, dwm, dbsT, dlvg, dlvb = _gmlp_bwd(d, sv["P"], dya_b, G["a"], S["ln_v_g"], S["ln_v_b"], S["w_s"], S["b_sT"], name=f"gmlp_bwd_{tag}", sched=sched)
    g_a = mm(dya_b, sv["sgu"], "tn", name=f"bra_dw_{tag}")
    emit("a", g_a)
    dqr_b, dkvr_b, dsk = _swa_bwd(d, sv["P"], dyb_b, G["b"], cos2, sin2, S["sinks"], name=f"swa_bwd_{tag}", sched=sched)
    g_b = mm(dyb_b, sv["att"], "tn", name=f"brb_dw_{tag}")
    emit("b", g_b)
    small = dict(b_gate=jnp.concatenate([dbga, dbgb], axis=1), ln_v_g=dlvg, ln_v_b=dlvb, w_s=dwm, b_sT=dbsT, sinks=dsk,
                 ln1_g=dg1, ln1_b=db1, ln2_g=dg2, ln2_b=db2, ln3_g=dg3, ln3_b=db3)
    emit("small", small)
    dP = jnp.concatenate([duv_b, dqr_b, dkvr_b, dga_b, dgb_b], axis=1)
    g_in = mm(dP, sv["Xb"], "tn", name=f"proj_dw_{tag}", **wide)
    emit("in", g_in)
    dXin = mm(dP, G["in"], "nn", name=f"proj_dx_{tag}", tn=D, tk=_tile(d.IN, 1920), epilogue=_ep_resid, extras=((dz1, "mn", 0),))
    big = {"in": g_in, "a": g_a, "b": g_b, "o": g_o, "xq": g_xq, "xkv": g_xkv, "xo": g_xo, "up": g_up, "down": g_down}
    return dXin, big, small


def _rope_tables(T):
    inv = 1.0 / (ROPE_THETA ** (jnp.arange(0, HEAD_DIM, 2, dtype=F32) / HEAD_DIM))
    ang = jnp.arange(T, dtype=F32)[:, None] * inv[None, :]
    cos, sin = jnp.cos(ang), jnp.sin(ang)
    reps = LANES // HEAD_DIM
    return jnp.concatenate([cos, cos] * reps, axis=1), jnp.concatenate([-sin, sin] * reps, axis=1)


def _local_step(d, x, mem, target, Gs, Ss, sched=None, on_grad=None):
    cos2, sin2 = _rope_tables(d.T)
    memb = mem.astype(BF16)
    X, Xb = x, x.astype(BF16)
    saved = []
    for l in range(DEPTH):
        X, Xb, sv = _layer_fwd(d, X, Xb, memb, Gs[l], Ss[l], cos2, sin2, f"l{l}", sched)
        saved.append(sv)
    loss, dX = _loss_head(X, target, name="loss_head", sched=sched)
    bigs, smalls = [None] * DEPTH, [None] * DEPTH
    for l in reversed(range(DEPTH)):
        emit = None if on_grad is None else functools.partial(on_grad, l)
        dX, bigs[l], smalls[l] = _layer_bwd(d, dX, saved[l], memb, Gs[l], Ss[l], cos2, sin2, f"l{l}", sched, emit)
    return loss, dX, bigs, smalls


def _place():
    x, y, c = lax.axis_index("x"), lax.axis_index("y"), lax.axis_index("c")
    return x, y, c, [(1 - x, y), (x, 1 - y), (1 - x, 1 - y)]


def _all_gather(shards, *, name, sched=None):
    n = len(shards)
    any_spec = pl.BlockSpec(memory_space=pl.ANY)

    def body(*refs):
        ins, outs = refs[:n], refs[n : 2 * n]
        send_sems, recv_sems, local_sems = refs[2 * n :]
        x, y, c, chips = _place()
        me, sibling = (x, y, c), (x, y, 1 - c)

        def rows(w, p):
            r = ins[w].shape[0]
            return outs[w].at[pl.ds((4 * p[0] + 2 * p[1] + p[2]) * r, r), :]

        def copy(w, k, block, to, src=None):
            return pltpu.make_async_remote_copy(
                src_ref=rows(w, block) if src is None else src, dst_ref=rows(w, block),
                send_sem=send_sems.at[7 * w + k], recv_sem=recv_sems.at[7 * w + k], device_id=to, device_id_type=MESH)

        mine = [pltpu.make_async_copy(ins[w], rows(w, me), local_sems.at[w]) for w in range(n)]
        for cp in mine:
            cp.start()
        first = []
        for w in range(n):
            first.append(copy(w, 0, me, sibling, src=ins[w]))
            first += [copy(w, 1 + j, me, (*chip, c), src=ins[w]) for j, chip in enumerate(chips)]
        for cp in first:
            cp.start()
        passed = []
        for j, chip in enumerate(chips):
            for w in range(n):
                copy(w, 1 + j, (*chip, c), me).wait_recv()
                fwd = copy(w, 4 + j, (*chip, c), sibling)
                fwd.start()
                passed.append(fwd)
        for w in range(n):
            copy(w, 0, sibling, me).wait_recv()
            for j, chip in enumerate(chips):
                copy(w, 4 + j, (*chip, 1 - c), me).wait_recv()
        for cp in first + passed:
            cp.wait_send()
        for cp in mine:
            cp.wait()

    return _hosted(
        body,
        name=name,
        sched=sched,
        in_specs=[any_spec] * n,
        out_specs=[any_spec] * n,
        out_shape=[jax.ShapeDtypeStruct((N_DEV * s.shape[0], s.shape[1]), s.dtype) for s in shards],
        scratch_shapes=[pltpu.SemaphoreType.DMA((7 * n,)), pltpu.SemaphoreType.DMA((7 * n,)), pltpu.SemaphoreType.DMA((n,))],
    )(*shards)


class _Task:
    def __init__(self, ins, out_shapes, n_remote, n_local, plan, done, aliases=None):
        self.ins, self.out_shapes, self.n_remote, self.n_local = list(ins), list(out_shapes), n_remote, n_local
        self.plan, self.done, self.aliases = plan, done, dict(aliases or {})


class _Sched:
    def __init__(self):
        self.pending = {}

    def at(self, host, task):
        self.pending.setdefault(host, []).append(task)

    def take(self, host):
        return self.pending.pop(host, [])


def _hosted(body, *, name, sched=None, tasks=(), grid=(), in_specs=None, out_specs=None, out_shape=(), scratch_shapes=(),
            compiler_params=None, input_output_aliases=None):
    tasks = list(tasks) + (sched.take(name) if sched is not None else [])
    scratch_shapes = list(scratch_shapes)
    kwargs = dict(name=name)
    core_aliases = dict(input_output_aliases or {})
    if grid:
        kwargs["grid"] = grid
    if compiler_params is not None:
        kwargs["compiler_params"] = compiler_params
    if not tasks:
        if in_specs is not None:
            kwargs.update(in_specs=in_specs, out_specs=out_specs)
        return pl.pallas_call(body, out_shape=out_shape, scratch_shapes=scratch_shapes, input_output_aliases=core_aliases, **kwargs)
    assert in_specs is not None and out_specs is not None, name
    single = not isinstance(out_shape, (list, tuple))
    core_shape = [out_shape] if single else list(out_shape)
    core_specs = [out_specs] if single else list(out_specs)
    in_specs = list(in_specs)
    n_in, n_out, n_scr = len(in_specs), len(core_shape), len(scratch_shapes)
    t_ins = [a for t in tasks for a in t.ins]
    t_outs = [s for t in tasks for s in t.out_shapes]
    n_rem = sum(t.n_remote for t in tasks)
    n_loc = sum(t.n_local for t in tasks)
    aliases, pos_in, pos_out = dict(core_aliases), n_in, n_out
    for t in tasks:
        for a, b in t.aliases.items():
            aliases[pos_in + a] = pos_out + b
        pos_in += len(t.ins)
        pos_out += len(t.out_shapes)
    any_spec = pl.BlockSpec(memory_space=pl.ANY)

    def wrapped(*refs):
        core_in, t_in = refs[:n_in], refs[n_in : n_in + len(t_ins)]
        o0 = n_in + len(t_ins)
        core_out, t_out = refs[o0 : o0 + n_out], refs[o0 + n_out : o0 + n_out + len(t_outs)]
        s0 = o0 + n_out + len(t_outs)
        core_scr = refs[s0 : s0 + n_scr]
        send_sems, recv_sems, local_sems = refs[s0 + n_scr :]
        remote, local = [], []
        pi = po = pr = pq = 0
        for t in tasks:
            r, q = t.plan(t_in[pi : pi + len(t.ins)], t_out[po : po + len(t.out_shapes)], send_sems, recv_sems, local_sems, pr, pq)
            assert len(r) == t.n_remote and len(q) == t.n_local
            remote += r
            local += q
            pi, po, pr, pq = pi + len(t.ins), po + len(t.out_shapes), pr + t.n_remote, pq + t.n_local

        def start():
            for cp in local + remote:
                cp.start()

        def finish():
            for cp in remote + local:
                cp.wait()

        if grid:
            ids = [pl.program_id(a) for a in range(len(grid))]
            first = functools.reduce(jnp.logical_and, [i == 0 for i in ids])
            last = functools.reduce(jnp.logical_and, [i == g - 1 for i, g in zip(ids, grid)])
            pl.when(first)(start)
            body(*core_in, *core_out, *core_scr)
            pl.when(last)(finish)
        else:
            start()
            body(*core_in, *core_out, *core_scr)
            finish()

    call = pl.pallas_call(
        wrapped,
        in_specs=in_specs + [any_spec] * len(t_ins),
        out_specs=core_specs + [any_spec] * len(t_outs),
        out_shape=core_shape + t_outs,
        scratch_shapes=scratch_shapes
        + [pltpu.SemaphoreType.DMA((n_rem,)), pltpu.SemaphoreType.DMA((n_rem,)), pltpu.SemaphoreType.DMA((max(n_loc, 1),))],
        input_output_aliases=aliases,
        **kwargs,
    )

    def run(*operands):
        outs = call(*operands, *t_ins)
        po = n_out
        for t in tasks:
            t.done(list(outs[po : po + len(t.out_shapes)]))
            po += len(t.out_shapes)
        return outs[0] if single else list(outs[:n_out])

    return run


def _comm_only(tasks, *, name):
    _hosted(lambda: None, name=name, tasks=tasks, in_specs=[], out_shape=[], out_specs=[])()


def _rows(ref, block, n_blocks):
    r = ref.shape[0] // n_blocks
    return ref.at[pl.ds(block * r, r), :]


def _remote(src, dst, send_sems, recv_sems, k, to):
    return pltpu.make_async_remote_copy(src_ref=src, dst_ref=dst, send_sem=send_sems.at[k], recv_sem=recv_sems.at[k],
                                        device_id=to, device_id_type=MESH)


def _gather_stage1(shards, done, part=(0, 1), into=None):
    n = len(shards)
    k, n_parts = part

    def plan(ins, outs, send_sems, recv_sems, local_sems, pr, pq):
        x, y, c, chips = _place()
        mine = 4 * x + 2 * y + c
        remote, local = [], []
        for w in range(n):
            r = shards[w].shape[0]
            rp = r // n_parts
            src = ins[w].at[pl.ds(k * rp, rp), :]
            dst = outs[w].at[pl.ds(mine * r + k * rp, rp), :]
            local.append(pltpu.make_async_copy(src, dst, local_sems.at[pq + w]))
            for j, to in enumerate([(x, y, 1 - c)] + [(*chip, c) for chip in chips]):
                remote.append(_remote(src, dst, send_sems, recv_sems, pr + 4 * w + j, to))
        return remote, local

    shapes = [jax.ShapeDtypeStruct((N_DEV * s.shape[0], s.shape[1]), s.dtype) for s in shards]
    if into is None:
        return _Task(shards, shapes, 4 * n, n, plan, done)
    return _Task(list(shards) + list(into), shapes, 4 * n, n, plan, done, aliases={n + w: w for w in range(n)})


def _gather_stage2(partial, done):
    n = len(partial)

    def plan(ins, outs, send_sems, recv_sems, local_sems, pr, pq):
        x, y, c, chips = _place()
        remote = []
        for w in range(n):
            for j, chip in enumerate(chips):
                rows = _rows(outs[w], 4 * chip[0] + 2 * chip[1] + c, N_DEV)
                remote.append(_remote(rows, rows, send_sems, recv_sems, pr + 3 * w + j, (x, y, 1 - c)))
        return remote, []

    shapes = [jax.ShapeDtypeStruct(p.shape, p.dtype) for p in partial]
    return _Task(partial, shapes, 3 * n, 0, plan, done, aliases={w: w for w in range(n)})


def _sibling_stage(grads, done):
    n = len(grads)

    def plan(ins, outs, send_sems, recv_sems, local_sems, pr, pq):
        x, y, c, _ = _place()
        remote = []
        for w in range(n):
            for k in range(4):
                src = _rows(ins[w], 4 * (k // 2) + 2 * (k % 2) + (1 - c), N_DEV)
                remote.append(_remote(src, _rows(outs[w], k, 4), send_sems, recv_sems, pr + 4 * w + k, (x, y, 1 - c)))
        return remote, []

    shapes = [jax.ShapeDtypeStruct((g.shape[0] // 2, g.shape[1]), g.dtype) for g in grads]
    return _Task(grads, shapes, 4 * n, 0, plan, done)


def _chips_stage(sends, done, part=(0, 1), into=None):
    n = len(sends)
    k, n_parts = part

    def plan(ins, outs, send_sems, recv_sems, local_sems, pr, pq):
        x, y, c, chips = _place()
        remote = []
        for w in range(n):
            r = sends[w].shape[0] // 3
            rp = r // n_parts
            for j, chip in enumerate(chips):
                rows = pl.ds(j * r + k * rp, rp)
                remote.append(_remote(ins[w].at[rows, :], outs[w].at[rows, :], send_sems, recv_sems, pr + 3 * w + j, (*chip, c)))
        return remote, []

    shapes = [jax.ShapeDtypeStruct(s.shape, s.dtype) for s in sends]
    if into is None:
        return _Task(sends, shapes, 3 * n, 0, plan, done)
    return _Task(list(sends) + list(into), shapes, 3 * n, 0, plan, done, aliases={n + w: w for w in range(n)})


def _pair_sum(grad, got, place, *, name):
    R, C = grad.shape
    r = R // N_DEV
    tr = _tile(r, 256, 16)
    nt = r // tr

    def chip_of(s, pr):
        fx = jnp.where((s == 1) | (s == 3), 1, 0)
        fy = jnp.where(s >= 2, 1, 0)
        return pr[0] ^ fx, pr[1] ^ fy

    def grad_map(s, t, pr):
        kx, ky = chip_of(s, pr)
        return ((4 * kx + 2 * ky + pr[2]) * nt + t, 0)

    def got_map(s, t, pr):
        kx, ky = chip_of(s, pr)
        return ((2 * kx + ky) * nt + t, 0)

    def body(pr, g_ref, r_ref, own_ref, send_ref):
        s = pl.program_id(0)
        tot = g_ref[...] + r_ref[...]

        @pl.when(s == 0)
        def _():
            own_ref[...] = tot

        @pl.when(s > 0)
        def _():
            send_ref[...] = tot.astype(BF16)

    return pl.pallas_call(
        body,
        name=name,
        grid_spec=pltpu.PrefetchScalarGridSpec(
            num_scalar_prefetch=1,
            grid=(4, nt),
            in_specs=[pl.BlockSpec((tr, C), grad_map), pl.BlockSpec((tr, C), got_map)],
            out_specs=[
                pl.BlockSpec((tr, C), lambda s, t, pr: (jnp.where(s == 0, t, nt - 1), 0)),
                pl.BlockSpec((tr, C), lambda s, t, pr: (jnp.where(s == 0, 0, (s - 1) * nt + t), 0)),
            ],
        ),
        out_shape=[jax.ShapeDtypeStruct((r, C), F32), jax.ShapeDtypeStruct((3 * r, C), BF16)],
        compiler_params=_params(("arbitrary", "arbitrary"), [((tr, C), F32)] * 4),
    )(place, grad, got)


def _adam_vals(w, g, m, v):
    m = ADAM_B1 * m + (1.0 - ADAM_B1) * g
    v = ADAM_B2 * v + (1.0 - ADAM_B2) * (g * g)
    m_hat = m / (1.0 - ADAM_B1**ADAM_STEP)
    v_hat = v / (1.0 - ADAM_B2**ADAM_STEP)
    delta = -ADAM_LR * (m_hat / (jnp.sqrt(v_hat) + ADAM_EPS) + ADAM_WD * w)
    return delta, m, v


def _adam_big(w3, m3, v3, own, got, layer, transposed, prev, *, name, sched=None):
    _, A, B = w3.shape
    ta = _tile(A, 256, 16)
    nt = A // ta
    b_pad = (-B) % LANES

    def body(*refs):
        w_ref, m_ref, v_ref, own_ref, g1_ref, g2_ref, g3_ref = refs[:7]
        g_ref, d_ref, nm_ref, nv_ref = refs[-4:]
        g = ((own_ref[...] + g1_ref[...].astype(F32)) + g2_ref[...].astype(F32)) + g3_ref[...].astype(F32)
        if transposed:
            if b_pad:
                g = jnp.concatenate([g, jnp.zeros((b_pad, ta), F32)], axis=0)
            g = g.T[:, :B]
        g_ref[...] = g
        d_ref[...], nm_ref[...], nv_ref[...] = _adam_vals(w_ref[...], g, m_ref[...], v_ref[...])

    nat = pl.BlockSpec((None, ta, B), lambda t: (layer, t, 0))
    if transposed:
        parts = [pl.BlockSpec((B, ta), lambda t: (0, t))] + [pl.BlockSpec((B, ta), lambda t, j=j: (j, t)) for j in range(3)]
    else:
        parts = [pl.BlockSpec((ta, B), lambda t: (t, 0))] + [pl.BlockSpec((ta, B), lambda t, j=j: (j * nt + t, 0)) for j in range(3)]
    keep = [] if prev is None else list(prev)
    outs = _hosted(
        body,
        name=name,
        sched=sched,
        grid=(nt,),
        in_specs=[nat, nat, nat] + parts + [pl.BlockSpec(memory_space=pl.ANY)] * len(keep),
        out_specs=[nat] * 4,
        out_shape=[jax.ShapeDtypeStruct(w3.shape, F32)] * 4,
        input_output_aliases={7 + k: k for k in range(len(keep))},
        compiler_params=_params(("arbitrary",), [((ta, B), F32)] * 10),
    )(w3, m3, v3, own, got, got, got, *keep)
    return list(outs)


def _adam_small(w, m, v, parts, *, name, sched=None):
    R = w.shape[0]

    def body(w_ref, m_ref, v_ref, p_ref, g_ref, d_ref, nm_ref, nv_ref):
        g = p_ref[pl.ds(0, R), :]
        for k in range(1, N_DEV):
            g = g + p_ref[pl.ds(k * R, R), :]
        g_ref[...] = g
        d_ref[...], nm_ref[...], nv_ref[...] = _adam_vals(w_ref[...], g, m_ref[...], v_ref[...])

    return _hosted(
        body,
        name=name,
        sched=sched,
        out_shape=[jax.ShapeDtypeStruct((R, LANES), F32)] * 4,
        compiler_params=_params(None, [((R, LANES), F32)] * 16),
    )(w, m, v, parts)


SMALL_NAMES = ("b_gate", "ln_v_g", "ln_v_b", "w_s", "b_s", "sinks", "ln1_g", "ln1_b", "ln2_g", "ln2_b", "ln3_g", "ln3_b")
PACK_ALIGN = 8 * LANES


def _pack(arrays):
    flat = []
    for a in arrays:
        a = a.reshape(-1)
        flat.append(jnp.pad(a, (0, (-a.shape[0]) % PACK_ALIGN)))
    return jnp.concatenate(flat).reshape(-1, LANES)


def _unpack(packed, shapes):
    flat = packed.reshape(-1)
    out, pos = [], 0
    for s in shapes:
        n = 1
        for e in s:
            n *= e
        out.append(flat[pos : pos + n].reshape(s))
        pos += n + (-n) % PACK_ALIGN
    return out


BIG = (("in", "w_in", True), ("a", "w_br_a", True), ("b", "w_br_b", True), ("o", "w_o", False), ("xq", "w_xq", False),
       ("xkv", "w_xkv", False), ("xo", "w_xo", True), ("up", "w_up", True), ("down", "w_down", False))


SMALL_BIG = ("a", "b", "o", "xq", "xkv", "xo")
GATHER_PLAN = (
    (0, ("in",), None, None),
    (0, SMALL_BIG, ("proj_l0",), "gmlp_fwd_l0"),
    (0, ("up",), ("swa_fwd_l0", "merge_l0"), "oproj_ln_l0"),
    (0, ("down",), ("oproj_ln_l0", "xattn_fwd_l0"), "up_l0"),
    (1, ("in",), ("up_l0",), "down_ln_l0"),
    (1, SMALL_BIG, ("down_ln_l0",), "proj_l1"),
    (1, ("up",), ("down_ln_l0", "proj_l1"), "gmlp_fwd_l1"),
    (1, ("down",), ("swa_fwd_l1", "merge_l1"), "oproj_ln_l1"),
)
EARLY_SMALL_BIG = ("o", "xq", "xkv", "xo")
LATE_SMALL_BIG = ("a", "b")
GRAD_HOSTS = {
    1: {"down": ("xattn_bwd_l1", "swa_bwd_l1"), "up": ("xattn_bwd_l1", "proj_dw_l1"),
        **{g: ("gmlp_bwd_l1", "proj_dx_l1") for g in EARLY_SMALL_BIG}, **{g: ("proj_dw_l1", "proj_dx_l1") for g in LATE_SMALL_BIG},
        "in": ("proj_dx_l1", "down_dx_l0")},
    0: {"down": ("xattn_bwd_l0", ("oproj_dx_l0", "gmlp_bwd_l0")), "up": ("xattn_bwd_l0", "swa_bwd_l0"),
        **{g: ("gmlp_bwd_l0", "proj_dw_l0") for g in EARLY_SMALL_BIG}, **{g: ("proj_dw_l0", "proj_dx_l0") for g in LATE_SMALL_BIG},
        "in": (None, "proj_dx_l0")},
}
SMALL_GRAD_HOSTS = ("proj_dw_l0", "proj_dx_l0")


def kernel(x, mem, w_in, b_gate, ln_v_g, ln_v_b, w_s, b_s, sinks, w_br_a, w_br_b, w_o, ln1_g, ln1_b, w_xq, w_xkv, w_xo, ln2_g, ln2_b, w_up, w_down, ln3_g, ln3_b, loss_target, m_w_in, m_b_gate, m_ln_v_g, m_ln_v_b, m_w_s, m_b_s, m_sinks, m_w_br_a, m_w_br_b, m_w_o, m_ln1_g, m_ln1_b, m_w_xq, m_w_xkv, m_w_xo, m_ln2_g, m_ln2_b, m_w_up, m_w_down, m_ln3_g, m_ln3_b, v_w_in, v_b_gate, v_ln_v_g, v_ln_v_b, v_w_s, v_b_s, v_sinks, v_w_br_a, v_w_br_b, v_w_o, v_ln1_g, v_ln1_b, v_w_xq, v_w_xkv, v_w_xo, v_ln2_g, v_ln2_b, v_w_up, v_w_down, v_ln3_g, v_ln3_b):
    given = dict(locals())
    T, D = x.shape[1], x.shape[2]
    IN, GW, AW, XW, DFF = w_in.shape[2] * N_DEV, ln_v_g.shape[1], w_br_b.shape[1], w_xq.shape[2], w_up.shape[2] * N_DEV
    KVW = (IN - 2 * GW - AW - 2 * D) // 2
    d = Dims(T=T, D=D, IN=IN, GW=GW, G=GW // GROUP_DIM, AW=AW, KVW=KVW, HQ=AW // HEAD_DIM, HKV=KVW // HEAD_DIM, XW=XW,
             XH=XW // X_HEAD_DIM, MEM=mem.shape[1], DFF=DFF)
    place = jnp.stack([lax.axis_index("x"), lax.axis_index("y"), lax.axis_index("c")]).astype(jnp.int32)

    sched = _Sched()
    big_of = {g: (p, tr) for g, p, tr in BIG}

    def shard(l, g):
        p, tr = big_of[g]
        return (given[p][l].T if tr else given[p][l]).astype(BF16)

    Gs = [{}, {}]

    def plan_gather(l, names, hosts1, host2):
        shards = [shard(l, g) for g in names]
        if hosts1 is None:
            Gs[l].update(zip(names, _all_gather(shards, name=f"gather_{names[0]}_l{l}")))
            return

        def register(k, into):
            def done(outs):
                if k + 1 < len(hosts1):
                    register(k + 1, outs)
                else:
                    sched.at(host2, _gather_stage2(outs, lambda full: Gs[l].update(zip(names, full))))

            sched.at(hosts1[k], _gather_stage1(shards, done, part=(k, len(hosts1)), into=into))

        register(0, None)

    for entry in GATHER_PLAN:
        plan_gather(*entry)
    Ss = []
    for l in range(DEPTH):
        S = {n: given[n][l].reshape(1, -1) for n in SMALL_NAMES if n not in ("w_s", "b_s", "sinks")}
        S["w_s"], S["b_sT"], S["sinks"] = w_s[l], b_s[l].T, sinks[l]
        Ss.append(S)

    owns, recvs = {}, {}

    smalls_seen = {}
    gathered_small = []

    def pack_small(src):
        parts = []
        for l in range(DEPTH):
            for n in SMALL_NAMES:
                a = src[l]["b_sT"].T if n == "b_s" else src[l][n]
                parts.append(a[0, : d.HQ] if n == "sinks" else a)
        return _pack(parts)

    def on_small(l, small):
        smalls_seen[l] = small
        if len(smalls_seen) == DEPTH:
            host1, host2 = SMALL_GRAD_HOSTS
            sched.at(host1, _gather_stage1([pack_small(smalls_seen)], lambda part: sched.at(
                host2, _gather_stage2(part, lambda full: gathered_small.append(full[0])))))

    def on_grad(l, g, grad):
        if g == "small":
            return on_small(l, grad)
        sib_host, chips_host = GRAD_HOSTS[l][g]

        def after_sibling(got):
            owns[(l, g)], send = _pair_sum(grad, got[0], place, name=f"grad_pair_sum_{g}_l{l}")
            hosts = chips_host if isinstance(chips_host, tuple) else (chips_host,)

            def register(k, into):
                def done(r):
                    if k + 1 < len(hosts):
                        register(k + 1, r)
                    else:
                        recvs[(l, g)] = r[0]

                task = _chips_stage([send], done, part=(k, len(hosts)), into=into)
                if hosts[k] is None:
                    _comm_only([task], name=f"grad_chips_{g}_l{l}")
                else:
                    sched.at(hosts[k], task)

            register(0, None)

        task = _sibling_stage([grad], after_sibling)
        if sib_host is None:
            _comm_only([task], name=f"grad_sibling_{g}_l{l}")
        else:
            sched.at(sib_host, task)

    loss, dX, bigs, smalls = _local_step(d, x[0], mem[0], loss_target[0], Gs, Ss, sched, on_grad)
    loss = lax.psum(loss[0, 0], ("x", "y", "c"))

    assert not sched.pending, sorted(sched.pending)

    def viewed(p, tr):
        return tr and given[p].shape[2] % LANES != 0

    res = {p: None for _, p, _ in BIG}
    for l in reversed(range(DEPTH)):
        for g, p, tr in BIG:
            view = viewed(p, tr)
            w3, m3, v3 = ((jnp.swapaxes(a, 1, 2) if view else a) for a in (given[p], given["m_" + p], given["v_" + p]))
            res[p] = _adam_big(w3, m3, v3, owns[(l, g)], recvs[(l, g)], l, tr and not view, res[p], name=f"adamw_{g}_l{l}")
    for g, p, tr in BIG:
        if viewed(p, tr):
            res[p] = [jnp.swapaxes(a, 1, 2) for a in res[p]]

    shapes = [given[n].shape[1:] for n in SMALL_NAMES]
    pw, pm, pv = (_pack([given[pre + n][l] for l in range(DEPTH) for n in SMALL_NAMES]) for pre in ("", "m_", "v_"))
    small_out = [_unpack(o, shapes * DEPTH) for o in _adam_small(pw, pm, pv, gathered_small[0], name="adamw_small")]

    names = ["w_in", "b_gate", "ln_v_g", "ln_v_b", "w_s", "b_s", "sinks", "w_br_a", "w_br_b", "w_o", "ln1_g", "ln1_b", "w_xq",
             "w_xkv", "w_xo", "ln2_g", "ln2_b", "w_up", "w_down", "ln3_g", "ln3_b"]
    out = [loss, dX[None]]
    for kind in range(4):
        for n in names:
            if n in SMALL_NAMES:
                i = SMALL_NAMES.index(n)
                out.append(jnp.stack([small_out[kind][l * len(SMALL_NAMES) + i] for l in range(DEPTH)]))
            else:
                out.append(res[n][kind])
    return tuple(out)
```

```python
import collections
import functools

import jax
import jax.numpy as jnp
from jax import lax
from jax.experimental import pallas as pl
from jax.experimental.pallas import tpu as pltpu

F32 = jnp.float32
BF16 = jnp.bfloat16
MESH = pl.DeviceIdType.MESH

N_DEV = 8
DEPTH = 2
CHUNK = 128
HEAD_DIM = 64
ROPE_HALF = HEAD_DIM // 2
X_HEAD_DIM = 128
GROUP_DIM = 128
LANES = 128
ROPE_THETA = 10000.0
LN_EPS = 1e-5
ALPHA = (2 * DEPTH) ** 0.25
ADAM_LR = 0.001
ADAM_B1 = 0.9
ADAM_B2 = 0.999
ADAM_EPS = 1e-08
ADAM_WD = 0.01
ADAM_STEP = 10
VMEM_BYTES = 64 * 1024 * 1024
VMEM_TEMP_BYTES = 20 * 1024 * 1024

Dims = collections.namedtuple("Dims", "T D IN GW G AW KVW HQ HKV XW XH MEM DFF")


def _offsets(d):
    off_v = d.GW
    off_q = 2 * d.GW
    off_k = off_q + d.AW
    off_va = off_k + d.KVW
    off_ga = off_va + d.KVW
    off_gb = off_ga + d.D
    return off_v, off_q, off_k, off_va, off_ga, off_gb


def _tile(dim, pref, align=LANES):
    if dim <= pref:
        return dim
    t = pref - pref % align
    while t >= align:
        if dim % t == 0:
            return t
        t -= align
    return dim


def _nbytes(shape, dtype):
    n = 1
    for s in shape:
        n *= s
    return n * jnp.dtype(dtype).itemsize


def _params(sem, blocks, scratch=0):
    need = 2 * sum(_nbytes(s, t) for s, t in blocks) + scratch + VMEM_TEMP_BYTES
    limit = min(max(need, 32 * 1024 * 1024), VMEM_BYTES - 4 * 1024 * 1024)
    return pltpu.CompilerParams(dimension_semantics=sem, vmem_limit_bytes=limit)


def _dot(a, b, kind):
    dn = {"nn": (((1,), (0,)), ((), ())), "nt": (((1,), (1,)), ((), ())), "tn": (((0,), (0,)), ((), ()))}[kind]
    return lax.dot_general(a, b, dn, preferred_element_type=F32)


def _gelu(x):
    c = 0.7978845608028654
    t = jnp.tanh(c * (x + 0.044715 * (x * x * x)))
    return 0.5 * x * (1.0 + t)


def _gelu_grad(x):
    c = 0.7978845608028654
    x2 = x * x
    t = jnp.tanh(c * (x + 0.044715 * (x2 * x)))
    return 0.5 * (1.0 + t) + 0.5 * x * (1.0 - t * t) * (c * (1.0 + 3.0 * 0.044715 * x2))


def _ln_fwd(z, g, b):
    mu = jnp.mean(z, axis=-1, keepdims=True)
    zc = z - mu
    var = jnp.mean(zc * zc, axis=-1, keepdims=True)
    rstd = lax.rsqrt(var + LN_EPS)
    xhat = zc * rstd
    return xhat * g + b, xhat, rstd


def _ln_bwd_vals(dy, xhat, rstd, g):
    gd = dy * g
    m1 = jnp.mean(gd, axis=-1, keepdims=True)
    m2 = jnp.mean(gd * xhat, axis=-1, keepdims=True)
    return rstd * (gd - m1 - xhat * m2)


def _acc_store(ref, val, first):
    @pl.when(first)
    def _():
        ref[...] = val

    @pl.when(jnp.logical_not(first))
    def _():
        ref[...] += val


def _matmul(a, b, kind, *, name, tm=512, tn=1024, tk=2048, out_defs=(("mn", F32),), epilogue=None, extras=(), sched=None):
    if kind == "nn":
        (M, K), (K2, N) = a.shape, b.shape
    elif kind == "nt":
        (M, K), (N, K2) = a.shape, b.shape
    else:
        (K, M), (K2, N) = a.shape, b.shape
    assert K == K2, (a.shape, b.shape, kind)
    tm, tn, tk = _tile(M, tm), _tile(N, tn), _tile(K, tk)
    nk = K // tk
    blocks = []

    def spec(shape, imap, dtype):
        blocks.append((shape, dtype))
        return pl.BlockSpec(shape, imap)

    if kind == "tn":
        a_spec = spec((tk, tm), lambda j, i, k: (k, i), a.dtype)
    else:
        a_spec = spec((tm, tk), lambda j, i, k: (i, k), a.dtype)
    if kind == "nt":
        b_spec = spec((tn, tk), lambda j, i, k: (j, k), b.dtype)
    else:
        b_spec = spec((tk, tn), lambda j, i, k: (k, j), b.dtype)
    in_specs = [a_spec, b_spec]
    operands = [a, b]
    for arr, ekind, off in extras:
        if ekind == "mn":
            assert off % tn == 0
            in_specs.append(spec((tm, tn), lambda j, i, k, o=off // tn: (i, j + o), arr.dtype))
        elif ekind == "row":
            assert off % tn == 0
            in_specs.append(spec((1, tn), lambda j, i, k, o=off // tn: (0, j + o), arr.dtype))
        else:
            in_specs.append(spec((tm, 1), lambda j, i, k: (i, 0), arr.dtype))
        operands.append(arr)
    out_shape, out_specs = [], []
    for od in out_defs:
        okind, dt = od[0], od[1]
        if okind == "mn":
            out_shape.append(jax.ShapeDtypeStruct((M, N), dt))
            out_specs.append(spec((tm, tn), lambda j, i, k: (i, j), dt))
        elif okind == "cols":
            assert od[3] % tn == 0
            out_shape.append(jax.ShapeDtypeStruct((M, od[2]), dt))
            out_specs.append(spec((tm, tn), lambda j, i, k, o=od[3] // tn: (i, j + o), dt))
        elif okind == "m1":
            assert N == tn
            out_shape.append(jax.ShapeDtypeStruct((M, 1), dt))
            out_specs.append(spec((tm, 1), lambda j, i, k: (i, 0), dt))
        else:
            out_shape.append(jax.ShapeDtypeStruct((1, N), F32))
            out_specs.append(spec((1, tn), lambda j, i, k: (0, j), F32))
    n_ex, n_out = len(extras), len(out_defs)
    ep = epilogue if epilogue is not None else (lambda acc: (acc,))
    in_place = nk > 1 and epilogue is None and tuple(out_defs) == (("mn", F32),)

    def body(*refs):
        a_ref, b_ref = refs[0], refs[1]
        ex_refs = refs[2 : 2 + n_ex]
        out_refs = refs[2 + n_ex : 2 + n_ex + n_out]
        i = pl.program_id(1)
        k = pl.program_id(2)

        def product():
            return _dot(a_ref[...], b_ref[...], kind)

        def finish(acc):
            vals = ep(acc, *[r[...] for r in ex_refs])
            for od, r, v in zip(out_defs, out_refs, vals):
                if od[0] == "acc":
                    _acc_store(r, v, i == 0)
                else:
                    r[...] = v.astype(od[1])

        if nk == 1:
            finish(product())
            return
        acc_ref = out_refs[0] if in_place else refs[-1]

        @pl.when(k == 0)
        def _():
            acc_ref[...] = product()

        if in_place:
            @pl.when(k > 0)
            def _():
                acc_ref[...] += product()
        else:
            if nk > 2:
                @pl.when(jnp.logical_and(k > 0, k < nk - 1))
                def _():
                    acc_ref[...] += product()

            @pl.when(k == nk - 1)
            def _():
                finish(acc_ref[...] + product())

    scratch = [pltpu.VMEM((tm, tn), F32)] if nk > 1 and not in_place else []
    outs = _hosted(
        body,
        name=name,
        sched=sched,
        grid=(N // tn, M // tm, nk),
        in_specs=in_specs,
        out_specs=out_specs,
        out_shape=out_shape,
        scratch_shapes=scratch,
        compiler_params=_params(("arbitrary", "arbitrary", "arbitrary"), blocks, _nbytes((tm, tn), F32) if scratch else 0),
    )(*operands)
    return outs if len(outs) > 1 else outs[0]


def _ep_resid_ln(acc, resid, g, b):
    y, xhat, rstd = _ln_fwd(ALPHA * resid + acc, g, b)
    return y, y, xhat, rstd


def _ep_resid(acc, resid):
    return (ALPHA * resid + acc,)


def _ep_relu2(acc):
    r = jnp.maximum(acc, 0.0)
    return acc, r * r


def _ep_relu2_bwd(acc, h):
    return (acc * (2.0 * jnp.maximum(h, 0.0)),)


def _ep_gate_bwd(acc, ga, gb, ya, yb, bga, bgb):
    sa = jax.nn.sigmoid(ga + bga)
    sb = jax.nn.sigmoid(gb + bgb)
    dga = acc * ya * (sa * (1.0 - sa))
    dgb = acc * yb * (sb * (1.0 - sb))
    return (acc * sa, acc * sb, dga, dgb, jnp.sum(dga, axis=0, keepdims=True), jnp.sum(dgb, axis=0, keepdims=True))


def _ln_bwd(dy, xhat, rstd, g, *, name, sched=None, target=None):
    T, D = dy.shape
    tm = _tile(T, 256)
    head = target is not None

    def body(*refs):
        dy_ref, xh_ref, rs_ref, g_ref = refs[:4]
        dz_ref, dzb_ref, dg_ref, db_ref = refs[4 + head : 8 + head]
        first = pl.program_id(0) == 0
        dyv, xh = dy_ref[...], xh_ref[...]
        if head:
            err = dyv - refs[4][...]
            dyv = err * (1.0 / D)
            part = 0.5 * jnp.sum(jnp.sum(err * err, axis=1, keepdims=True) * (1.0 / D), axis=0, keepdims=True)
            _acc_store(refs[-1], part, first)
        dz = _ln_bwd_vals(dyv, xh, rs_ref[...], g_ref[...])
        dz_ref[...] = dz
        dzb_ref[...] = dz.astype(BF16)
        _acc_store(dg_ref, jnp.sum(dyv * xh, axis=0, keepdims=True), first)
        _acc_store(db_ref, jnp.sum(dyv, axis=0, keepdims=True), first)

    row = pl.BlockSpec((tm, D), lambda i: (i, 0))
    vec = pl.BlockSpec((1, D), lambda i: (0, 0))
    one = pl.BlockSpec((1, 1), lambda i: (0, 0))
    return _hosted(
        body,
        name=name,
        sched=sched,
        grid=(T // tm,),
        in_specs=[row, row, pl.BlockSpec((tm, 1), lambda i: (i, 0)), vec] + [row] * head,
        out_specs=[row, row, vec, vec] + [one] * head,
        out_shape=[
            jax.ShapeDtypeStruct((T, D), F32),
            jax.ShapeDtypeStruct((T, D), BF16),
            jax.ShapeDtypeStruct((1, D), F32),
            jax.ShapeDtypeStruct((1, D), F32),
        ] + [jax.ShapeDtypeStruct((1, 1), F32)] * head,
        compiler_params=_params(("arbitrary",), [((tm, D), F32)] * (4 + head)),
    )(*([dy, xhat, rstd, g] + [target] * head))


def _masked_mix_weights(wm_ref, G):
    r = lax.broadcasted_iota(jnp.int32, (CHUNK, CHUNK), 0)
    c = lax.broadcasted_iota(jnp.int32, (CHUNK, CHUNK), 1)
    return [jnp.where(c <= r, wm_ref[g], 0.0).astype(BF16) for g in range(G)]


def _gmlp_fwd(d, P, lnv_g, lnv_b, wm, bsT, *, name, sched=None):
    T, GW, G = d.T, d.GW, d.G
    tm = _tile(T, 256)
    nc = tm // CHUNK

    def body(u_ref, v_ref, g_ref, b_ref, wm_ref, bs_ref, o_ref):
        gu = _gelu(u_ref[...])
        vn, _, _ = _ln_fwd(_gelu(v_ref[...]), g_ref[...], b_ref[...])
        vn = vn.astype(BF16)
        wl = _masked_mix_weights(wm_ref, G)
        for c in range(nc):
            rows = slice(c * CHUNK, (c + 1) * CHUNK)
            for g in range(G):
                cols = slice(g * GROUP_DIM, (g + 1) * GROUP_DIM)
                mixed = _dot(wl[g], vn[rows, cols], "nn") + bs_ref[:, g : g + 1]
                o_ref[rows, cols] = (gu[rows, cols] * mixed).astype(BF16)

    return _hosted(
        body,
        name=name,
        sched=sched,
        grid=(T // tm,),
        in_specs=[
            pl.BlockSpec((tm, GW), lambda i: (i, 0)),
            pl.BlockSpec((tm, GW), lambda i: (i, 1)),
            pl.BlockSpec((1, GW), lambda i: (0, 0)),
            pl.BlockSpec((1, GW), lambda i: (0, 0)),
            pl.BlockSpec((G, CHUNK, CHUNK), lambda i: (0, 0, 0)),
            pl.BlockSpec((CHUNK, G), lambda i: (0, 0)),
        ],
        out_specs=pl.BlockSpec((tm, GW), lambda i: (i, 0)),
        out_shape=jax.ShapeDtypeStruct((T, GW), BF16),
        compiler_params=_params(("arbitrary",), [((tm, GW), F32)] * 3),
    )(P, P, lnv_g, lnv_b, wm, bsT)


def _gmlp_bwd(d, P, dya_b, Ga, lnv_g, lnv_b, wm, bsT, dP, *, name, sched=None):
    T, D, GW, G = d.T, d.D, d.GW, d.G
    tm = _tile(T, 256)
    nc = tm // CHUNK

    def body(u_ref, v_ref, dya_ref, ga_ref, g_ref, b_ref, wm_ref, bs_ref, dp_in, duv_ref, dwm_ref, dbs_ref, dlg_ref, dlb_ref, dgu_s, dvn_s):
        first = pl.program_id(0) == 0
        dsgu = _dot(dya_ref[...], ga_ref[...], "nn")
        u, v = u_ref[...], v_ref[...]
        gu = _gelu(u)
        lng = g_ref[...]
        vn, xh, rstd = _ln_fwd(_gelu(v), lng, b_ref[...])
        vn = vn.astype(BF16)
        wl = _masked_mix_weights(wm_ref, G)
        r = lax.broadcasted_iota(jnp.int32, (CHUNK, CHUNK), 0)
        cc = lax.broadcasted_iota(jnp.int32, (CHUNK, CHUNK), 1)
        lane_g = lax.broadcasted_iota(jnp.int32, (CHUNK, G), 1)
        dbs = jnp.zeros((CHUNK, G), F32)
        for g in range(G):
            cols = slice(g * GROUP_DIM, (g + 1) * GROUP_DIM)
            dw = jnp.zeros((CHUNK, CHUNK), F32)
            for c in range(nc):
                rows = slice(c * CHUNK, (c + 1) * CHUNK)
                mixed = _dot(wl[g], vn[rows, cols], "nn") + bs_ref[:, g : g + 1]
                ds = dsgu[rows, cols]
                dgu_s[rows, cols] = ds * mixed
                dmx = ds * gu[rows, cols]
                dbs = dbs + jnp.where(lane_g == g, jnp.sum(dmx, axis=1, keepdims=True), 0.0)
                dmx = dmx.astype(BF16)
                dw = dw + _dot(dmx, vn[rows, cols], "nt")
                dvn_s[rows, cols] = _dot(wl[g], dmx, "tn")
            _acc_store(dwm_ref.at[g], jnp.where(cc <= r, dw, 0.0), first)
        _acc_store(dbs_ref, dbs, first)
        dvn = dvn_s[...]
        _acc_store(dlg_ref, jnp.sum(dvn * xh, axis=0, keepdims=True), first)
        _acc_store(dlb_ref, jnp.sum(dvn, axis=0, keepdims=True), first)
        dgv = _ln_bwd_vals(dvn, xh, rstd, lng)
        duv_ref[:, :GW] = (dgu_s[...] * _gelu_grad(u)).astype(BF16)
        duv_ref[:, GW:] = (dgv * _gelu_grad(v)).astype(BF16)

    return _hosted(
        body,
        name=name,
        sched=sched,
        grid=(T // tm,),
        in_specs=[
            pl.BlockSpec((tm, GW), lambda i: (i, 0)),
            pl.BlockSpec((tm, GW), lambda i: (i, 1)),
            pl.BlockSpec((tm, D), lambda i: (i, 0)),
            pl.BlockSpec((D, GW), lambda i: (0, 0)),
            pl.BlockSpec((1, GW), lambda i: (0, 0)),
            pl.BlockSpec((1, GW), lambda i: (0, 0)),
            pl.BlockSpec((G, CHUNK, CHUNK), lambda i: (0, 0, 0)),
            pl.BlockSpec((CHUNK, G), lambda i: (0, 0)),
            pl.BlockSpec(memory_space=pl.ANY),
        ],
        out_specs=[
            pl.BlockSpec((tm, 2 * GW), lambda i: (i, 0)),
            pl.BlockSpec((G, CHUNK, CHUNK), lambda i: (0, 0, 0)),
            pl.BlockSpec((CHUNK, G), lambda i: (0, 0)),
            pl.BlockSpec((1, GW), lambda i: (0, 0)),
            pl.BlockSpec((1, GW), lambda i: (0, 0)),
        ],
        out_shape=[
            jax.ShapeDtypeStruct(dP.shape, BF16),
            jax.ShapeDtypeStruct((G, CHUNK, CHUNK), F32),
            jax.ShapeDtypeStruct((CHUNK, G), F32),
            jax.ShapeDtypeStruct((1, GW), F32),
            jax.ShapeDtypeStruct((1, GW), F32),
        ],
        scratch_shapes=[pltpu.VMEM((tm, GW), F32), pltpu.VMEM((tm, GW), F32)],
        input_output_aliases={8: 0},
        compiler_params=_params(
            ("arbitrary",), [((tm, GW), F32)] * 3 + [((tm, D), BF16), ((D, GW), BF16)], 2 * _nbytes((tm, GW), F32)
        ),
    )(P, P, dya_b, Ga, lnv_g, lnv_b, wm, bsT, dP)


def _swap_halves(x):
    w = x.shape[1]
    lane = lax.broadcasted_iota(jnp.int32, x.shape, 1)
    return jnp.where((lane % HEAD_DIM) < ROPE_HALF, pltpu.roll(x, w - ROPE_HALF, 1), pltpu.roll(x, ROPE_HALF, 1))


def _lane_tile(t, width):
    reps = width // LANES
    return t if reps == 1 else jnp.tile(t, (1, reps))


def _rope(x, cos2, sin2):
    w = x.shape[1]
    return x * _lane_tile(cos2, w) + _swap_halves(x) * _lane_tile(sin2, w)


def _rope_bwd(g, cos2, sin2):
    w = g.shape[1]
    return g * _lane_tile(cos2, w) - _swap_halves(g) * _lane_tile(sin2, w)


PAIR = LANES // HEAD_DIM


def _swa_valid(i):
    s = lax.broadcasted_iota(jnp.int32, (2 * CHUNK, CHUNK), 0)
    t = lax.broadcasted_iota(jnp.int32, (2 * CHUNK, CHUNK), 1)
    cur = jnp.logical_and(s >= CHUNK, s - CHUNK <= t)
    prev = jnp.logical_and(jnp.logical_and(s < CHUNK, s > t), i > 0)
    return jnp.logical_or(cur, prev)


def _half_variants(x, odd):
    lane = lax.broadcasted_iota(jnp.int32, x.shape, 1)
    if odd:
        hi = jnp.where(lane >= HEAD_DIM, x, jnp.zeros_like(x))
        return pltpu.roll(hi, HEAD_DIM, 1), hi
    lo = jnp.where(lane < HEAD_DIM, x, jnp.zeros_like(x))
    return lo, pltpu.roll(lo, HEAD_DIM, 1)


def _swa_probs_t(kx, qp, sink, valid):
    s = jnp.where(valid, _dot(kx, qp, "nt") * (HEAD_DIM**-0.5), -jnp.inf)
    m = jnp.maximum(jnp.max(s, axis=0, keepdims=True), sink)
    e = jnp.exp(s - m)
    e_s = jnp.exp(sink - m)
    den = jnp.sum(e, axis=0, keepdims=True) + e_s
    return e / den, e_s / den


def _swa_load(q_ref, kc_ref, kp_ref, vc_ref, vp_ref, cc, sc, cp, sp):
    qr = _rope(q_ref[...], cc, sc).astype(BF16)
    kcat = jnp.concatenate([_rope(kp_ref[...], cp, sp), _rope(kc_ref[...], cc, sc)], axis=0)
    vcat = jnp.concatenate([vp_ref[...], vc_ref[...]], axis=0)
    return qr, kcat, vcat


def _swa_specs(d):
    _, off_q, off_k, off_va, _, _ = _offsets(d)
    assert off_q % d.AW == 0 and off_k % d.KVW == 0 and off_va % d.KVW == 0
    prev = lambda i: jnp.maximum(i - 1, 0)
    return [
        pl.BlockSpec(memory_space=pltpu.SMEM),
        pl.BlockSpec((CHUNK, d.AW), lambda i, o=off_q // d.AW: (i, o)),
        pl.BlockSpec((CHUNK, d.KVW), lambda i, o=off_k // d.KVW: (i, o)),
        pl.BlockSpec((CHUNK, d.KVW), lambda i, o=off_k // d.KVW: (prev(i), o)),
        pl.BlockSpec((CHUNK, d.KVW), lambda i, o=off_va // d.KVW: (i, o)),
        pl.BlockSpec((CHUNK, d.KVW), lambda i, o=off_va // d.KVW: (prev(i), o)),
        pl.BlockSpec((CHUNK, LANES), lambda i: (i, 0)),
        pl.BlockSpec((CHUNK, LANES), lambda i: (i, 0)),
        pl.BlockSpec((CHUNK, LANES), lambda i: (prev(i), 0)),
        pl.BlockSpec((CHUNK, LANES), lambda i: (prev(i), 0)),
    ]


def _swa_fwd(d, P, cos2, sin2, sinks, *, name, sched=None):
    T, AW, HQ, HKV = d.T, d.AW, d.HQ, d.HKV
    grp = HQ // HKV
    assert grp % PAIR == 0 and HKV % PAIR == 0

    def body(sink_ref, q_ref, kc_ref, kp_ref, vc_ref, vp_ref, cc_ref, sc_ref, cp_ref, sp_ref, o_ref):
        i = pl.program_id(0)
        qr, kcat, vcat = _swa_load(q_ref, kc_ref, kp_ref, vc_ref, vp_ref, cc_ref[...], sc_ref[...], cp_ref[...], sp_ref[...])
        valid = _swa_valid(i)
        for kv in range(HKV):
            col = slice((kv // PAIR) * LANES, (kv // PAIR + 1) * LANES)
            kx = [t.astype(BF16) for t in _half_variants(kcat[:, col], kv % PAIR)]
            vx = [t.astype(BF16) for t in _half_variants(vcat[:, col], kv % PAIR)]
            for pp in range(grp // PAIR):
                p = kv * (grp // PAIR) + pp
                qs = slice(p * LANES, (p + 1) * LANES)
                out = jnp.zeros((CHUNK, LANES), F32)
                for half in range(PAIR):
                    pt, _ = _swa_probs_t(kx[half], qr[:, qs], sink_ref[PAIR * p + half], valid)
                    out = out + _dot(pt.astype(BF16), vx[half], "tn")
                o_ref[:, qs] = out.astype(BF16)

    return _hosted(
        body,
        name=name,
        sched=sched,
        grid=(T // CHUNK,),
        in_specs=_swa_specs(d),
        out_specs=pl.BlockSpec((CHUNK, AW), lambda i: (i, 0)),
        out_shape=jax.ShapeDtypeStruct((T, AW), BF16),
        compiler_params=_params(("arbitrary",), [((CHUNK, AW), F32)] * 3),
    )(sinks, P, P, P, P, P, cos2, sin2, cos2, sin2)


def _swa_bwd(d, P, dyb_b, Gb, cos2, sin2, sinks, dP, *, name, sched=None):
    T, D, AW, KVW, HQ, HKV = d.T, d.D, d.AW, d.KVW, d.HQ, d.HKV
    grp = HQ // HKV
    nb = T // CHUNK
    scale = HEAD_DIM**-0.5
    off_q = _offsets(d)[1]
    assert grp % PAIR == 0 and HKV % PAIR == 0 and off_q % AW == 0

    def body(sink_ref, q_ref, kc_ref, kp_ref, vc_ref, vp_ref, cc_ref, sc_ref, cp_ref, sp_ref, dyb_ref, gb_ref, dp_in,
             dq_ref, dkv_ref, dsk_ref, acc_s, dq_s, dk_s, dv_s):
        i = pl.program_id(0)
        cc, sc, cp, sp = cc_ref[...], sc_ref[...], cp_ref[...], sp_ref[...]
        datt = _dot(dyb_ref[...], gb_ref[...], "nn").astype(BF16)
        qr, kcat, vcat = _swa_load(q_ref, kc_ref, kp_ref, vc_ref, vp_ref, cc, sc, cp, sp)
        valid = _swa_valid(i)
        lane1 = lax.broadcasted_iota(jnp.int32, (1, LANES), 1)
        lane2 = lax.broadcasted_iota(jnp.int32, (2 * CHUNK, LANES), 1)
        dsk = jnp.zeros((1, LANES), F32)
        for kvp in range(HKV // PAIR):
            col = slice(kvp * LANES, (kvp + 1) * LANES)
            dk_col = jnp.zeros((2 * CHUNK, LANES), F32)
            dv_col = jnp.zeros((2 * CHUNK, LANES), F32)
            for odd in range(PAIR):
                kv = kvp * PAIR + odd
                kx = [t.astype(BF16) for t in _half_variants(kcat[:, col], odd)]
                vx = [t.astype(BF16) for t in _half_variants(vcat[:, col], odd)]
                dk_half = [jnp.zeros((2 * CHUNK, LANES), F32) for _ in range(PAIR)]
                dv_half = [jnp.zeros((2 * CHUNK, LANES), F32) for _ in range(PAIR)]
                for pp in range(grp // PAIR):
                    p = kv * (grp // PAIR) + pp
                    qs = slice(p * LANES, (p + 1) * LANES)
                    qp, dop = qr[:, qs], datt[:, qs]
                    dq = jnp.zeros((CHUNK, LANES), F32)
                    for half in range(PAIR):
                        h = PAIR * p + half
                        pt, p_s = _swa_probs_t(kx[half], qp, sink_ref[h], valid)
                        dpt = _dot(vx[half], dop, "nt")
                        rs = jnp.sum(pt * dpt, axis=0, keepdims=True)
                        dst = (pt * (dpt - rs) * scale).astype(BF16)
                        dsk = dsk + jnp.where(lane1 == h, -jnp.sum(p_s * rs, axis=1, keepdims=True), 0.0)
                        dq = dq + _dot(dst, kx[half], "tn")
                        dk_half[half] = dk_half[half] + _dot(dst, qp, "nn")
                        dv_half[half] = dv_half[half] + _dot(pt.astype(BF16), dop, "nn")
                    dq_s[:, qs] = dq
                for acc_half, is_k in ((dk_half, True), (dv_half, False)):
                    lo = jnp.where(lane2 < HEAD_DIM, acc_half[0], 0.0)
                    hi = jnp.where(lane2 >= HEAD_DIM, acc_half[1], 0.0)
                    both = (pltpu.roll(lo, HEAD_DIM, 1) + hi) if odd else (lo + pltpu.roll(hi, HEAD_DIM, 1))
                    if is_k:
                        dk_col = dk_col + both
                    else:
                        dv_col = dv_col + both
            dk_s[:, col] = dk_col
            dv_s[:, col] = dv_col
        dq_ref[...] = _rope_bwd(dq_s[...], cc, sc).astype(BF16)
        cur = pl.ds(pl.multiple_of(i * CHUNK, CHUNK), CHUNK)
        acc_s[cur, :KVW] = _rope_bwd(dk_s[CHUNK:, :], cc, sc)
        acc_s[cur, KVW:] = dv_s[CHUNK:, :]

        @pl.when(i > 0)
        def _():
            prv = pl.ds(pl.multiple_of((i - 1) * CHUNK, CHUNK), CHUNK)
            acc_s[prv, :KVW] += _rope_bwd(dk_s[:CHUNK, :], cp, sp)
            acc_s[prv, KVW:] += dv_s[:CHUNK, :]

        _acc_store(dsk_ref, dsk, i == 0)

        @pl.when(i == nb - 1)
        def _():
            dkv_ref[...] = acc_s[...].astype(BF16)

    return _hosted(
        body,
        name=name,
        sched=sched,
        grid=(nb,),
        in_specs=_swa_specs(d) + [pl.BlockSpec((CHUNK, D), lambda i: (i, 0)), pl.BlockSpec((D, AW), lambda i: (0, 0)),
                                   pl.BlockSpec(memory_space=pl.ANY)],
        out_specs=[
            pl.BlockSpec((CHUNK, AW), lambda i, o=off_q // AW: (i, o)),
            pl.BlockSpec((T, 2 * KVW), lambda i: (0, 0)),
            pl.BlockSpec((1, LANES), lambda i: (0, 0)),
        ],
        out_shape=[
            jax.ShapeDtypeStruct(dP.shape, BF16),
            jax.ShapeDtypeStruct((T, 2 * KVW), BF16),
            jax.ShapeDtypeStruct((1, LANES), F32),
        ],
        input_output_aliases={12: 0},
        scratch_shapes=[
            pltpu.VMEM((T, 2 * KVW), F32),
            pltpu.VMEM((CHUNK, AW), F32),
            pltpu.VMEM((2 * CHUNK, KVW), F32),
            pltpu.VMEM((2 * CHUNK, KVW), F32),
        ],
        compiler_params=_params(
            ("arbitrary",), [((CHUNK, AW), F32)] * 3 + [((D, AW), BF16), ((T, 2 * KVW), BF16)], _nbytes((T, 2 * KVW), F32)
        ),
    )(sinks, P, P, P, P, P, cos2, sin2, cos2, sin2, dyb_b, Gb, dP)


def _place_cols(dst, src, off, *, name, sched=None):
    T, w = src.shape
    tm, tw = _tile(T, 512), _tile(w, 512)
    assert off % tw == 0

    def body(src_ref, dst_in, out_ref):
        out_ref[...] = src_ref[...]

    return _hosted(
        body,
        name=name,
        sched=sched,
        grid=(T // tm, w // tw),
        in_specs=[pl.BlockSpec((tm, tw), lambda i, j: (i, j)), pl.BlockSpec(memory_space=pl.ANY)],
        out_specs=pl.BlockSpec((tm, tw), lambda i, j, o=off // tw: (i, j + o)),
        out_shape=jax.ShapeDtypeStruct(dst.shape, dst.dtype),
        input_output_aliases={1: 0},
        compiler_params=_params(("arbitrary", "arbitrary"), [((tm, tw), dst.dtype)] * 2),
    )(src, dst)


def _branch_merge(d, sgu, att, Ga, Gb, P, b_gate, *, name, sched=None):
    T, D, GW, AW = d.T, d.D, d.GW, d.AW
    _, _, _, _, off_ga, off_gb = _offsets(d)
    tm, tn = _tile(T, 1024), _tile(D, 512)
    assert off_ga % tn == 0 and off_gb % tn == 0

    def body(s_ref, a_ref, ga_w, gb_w, ga_ref, gb_ref, bga_ref, bgb_ref, ya_ref, yb_ref, mg_ref):
        ya = _dot(s_ref[...], ga_w[...], "nt")
        yb = _dot(a_ref[...], gb_w[...], "nt")
        ya_ref[...] = ya
        yb_ref[...] = yb
        sa = jax.nn.sigmoid(ga_ref[...] + bga_ref[...])
        sb = jax.nn.sigmoid(gb_ref[...] + bgb_ref[...])
        mg_ref[...] = (sa * ya + sb * yb).astype(BF16)

    tile = pl.BlockSpec((tm, tn), lambda j, i: (i, j))
    return _hosted(
        body,
        name=name,
        sched=sched,
        grid=(D // tn, T // tm),
        in_specs=[
            pl.BlockSpec((tm, GW), lambda j, i: (i, 0)),
            pl.BlockSpec((tm, AW), lambda j, i: (i, 0)),
            pl.BlockSpec((tn, GW), lambda j, i: (j, 0)),
            pl.BlockSpec((tn, AW), lambda j, i: (j, 0)),
            pl.BlockSpec((tm, tn), lambda j, i, o=off_ga // tn: (i, j + o)),
            pl.BlockSpec((tm, tn), lambda j, i, o=off_gb // tn: (i, j + o)),
            pl.BlockSpec((1, tn), lambda j, i: (0, j)),
            pl.BlockSpec((1, tn), lambda j, i, o=D // tn: (0, j + o)),
        ],
        out_specs=[tile, tile, tile],
        out_shape=[jax.ShapeDtypeStruct((T, D), F32), jax.ShapeDtypeStruct((T, D), F32), jax.ShapeDtypeStruct((T, D), BF16)],
        compiler_params=_params(
            ("arbitrary", "arbitrary"),
            [((tm, GW), BF16), ((tm, AW), BF16), ((tn, GW), BF16), ((tn, AW), BF16)] + [((tm, tn), F32)] * 5,
        ),
    )(sgu, att, Ga, Gb, P, P, b_gate, b_gate)


def _xattn_probs(qh, kh):
    s = _dot(qh, kh, "nt") * (X_HEAD_DIM**-0.5)
    e = jnp.exp(s - jnp.max(s, axis=1, keepdims=True))
    return e / jnp.sum(e, axis=1, keepdims=True)


def _xattn_fwd(d, X, Xb, kv, Gxq, Gxo, ln_g, ln_b, *, name, sched=None):
    T, D, XW, XH, MEM = d.T, d.D, d.XW, d.XH, d.MEM
    tm = _tile(T, 256)

    def body(x_ref, xb_ref, kv_ref, wq_ref, wo_ref, g_ref, b_ref, y_ref, yb_ref, xh_ref, rs_ref, q_ref, o_ref, o_s):
        q = _dot(xb_ref[...], wq_ref[...], "nn").astype(BF16)
        q_ref[...] = q
        for h in range(XH):
            hs = slice(h * X_HEAD_DIM, (h + 1) * X_HEAD_DIM)
            p = _xattn_probs(q[:, hs], kv_ref[:, hs]).astype(BF16)
            o_s[:, hs] = _dot(p, kv_ref[:, XW + h * X_HEAD_DIM : XW + (h + 1) * X_HEAD_DIM], "nn").astype(BF16)
        o = o_s[...]
        o_ref[...] = o
        y, xhat, rstd = _ln_fwd(ALPHA * x_ref[...] + _dot(o, wo_ref[...], "nt"), g_ref[...], b_ref[...])
        y_ref[...] = y
        yb_ref[...] = y.astype(BF16)
        xh_ref[...] = xhat
        rs_ref[...] = rstd

    row = pl.BlockSpec((tm, D), lambda i: (i, 0))
    nar = pl.BlockSpec((tm, XW), lambda i: (i, 0))
    vec = pl.BlockSpec((1, D), lambda i: (0, 0))
    return _hosted(
        body,
        name=name,
        sched=sched,
        grid=(T // tm,),
        in_specs=[
            row,
            row,
            pl.BlockSpec((MEM, 2 * XW), lambda i: (0, 0)),
            pl.BlockSpec((D, XW), lambda i: (0, 0)),
            pl.BlockSpec((D, XW), lambda i: (0, 0)),
            vec,
            vec,
        ],
        out_specs=[row, row, row, pl.BlockSpec((tm, 1), lambda i: (i, 0)), nar, nar],
        out_shape=[
            jax.ShapeDtypeStruct((T, D), F32),
            jax.ShapeDtypeStruct((T, D), BF16),
            jax.ShapeDtypeStruct((T, D), F32),
            jax.ShapeDtypeStruct((T, 1), F32),
            jax.ShapeDtypeStruct((T, XW), BF16),
            jax.ShapeDtypeStruct((T, XW), BF16),
        ],
        scratch_shapes=[pltpu.VMEM((tm, XW), BF16)],
        compiler_params=_params(("arbitrary",), [((tm, D), F32)] * 4 + [((D, XW), BF16)] * 2),
    )(X, Xb, kv, Gxq, Gxo, ln_g, ln_b)


def _xattn_bwd(d, dz, dzb, q_b, kv, Gxq, Gxo, *, name, sched=None):
    T, D, XW, XH, MEM = d.T, d.D, d.XW, d.XH, d.MEM
    tm = _tile(T, 256)
    scale = X_HEAD_DIM**-0.5

    def body(dz_ref, dzb_ref, q_ref, kv_ref, wq_ref, wo_ref, dx_ref, dq_ref, dkv_ref, dq_s):
        first = pl.program_id(0) == 0
        do = _dot(dzb_ref[...], wo_ref[...], "nn").astype(BF16)
        for h in range(XH):
            hs = slice(h * X_HEAD_DIM, (h + 1) * X_HEAD_DIM)
            vs = slice(XW + h * X_HEAD_DIM, XW + (h + 1) * X_HEAD_DIM)
            kh, vh, qh, doh = kv_ref[:, hs], kv_ref[:, vs], q_ref[:, hs], do[:, hs]
            p = _xattn_probs(qh, kh)
            dp = _dot(doh, vh, "nt")
            ds = (p * (dp - jnp.sum(p * dp, axis=1, keepdims=True)) * scale).astype(BF16)
            dq_s[:, hs] = _dot(ds, kh, "nn").astype(BF16)
            _acc_store(dkv_ref.at[h], _dot(ds, qh, "tn"), first)
            _acc_store(dkv_ref.at[XH + h], _dot(p.astype(BF16), doh, "tn"), first)
        dq = dq_s[...]
        dq_ref[...] = dq
        dx_ref[...] = ALPHA * dz_ref[...] + _dot(dq, wq_ref[...], "nt")

    row = pl.BlockSpec((tm, D), lambda i: (i, 0))
    nar = pl.BlockSpec((tm, XW), lambda i: (i, 0))
    return _hosted(
        body,
        name=name,
        sched=sched,
        grid=(T // tm,),
        in_specs=[
            row,
            row,
            nar,
            pl.BlockSpec((MEM, 2 * XW), lambda i: (0, 0)),
            pl.BlockSpec((D, XW), lambda i: (0, 0)),
            pl.BlockSpec((D, XW), lambda i: (0, 0)),
        ],
        out_specs=[row, nar, pl.BlockSpec((2 * XH, MEM, X_HEAD_DIM), lambda i: (0, 0, 0))],
        out_shape=[
            jax.ShapeDtypeStruct((T, D), F32),
            jax.ShapeDtypeStruct((T, XW), BF16),
            jax.ShapeDtypeStruct((2 * XH, MEM, X_HEAD_DIM), F32),
        ],
        scratch_shapes=[pltpu.VMEM((tm, XW), BF16)],
        compiler_params=_params(("arbitrary",), [((tm, D), F32)] * 3 + [((D, XW), BF16)] * 2),
    )(dz, dzb, q_b, kv, Gxq, Gxo)


def _resid_ln_outs():
    return (("mn", F32), ("mn", BF16), ("mn", F32), ("m1", F32))


def _layer_fwd(d, X, Xb, memb, G, S, cos2, sin2, tag, sched=None):
    D = d.D
    mm = functools.partial(_matmul, sched=sched)
    P = mm(Xb, G["in"], "nt", name=f"proj_{tag}", tm=1024, tn=_tile(d.IN, 1280))
    sgu = _gmlp_fwd(d, P, S["ln_v_g"], S["ln_v_b"], S["w_s"], S["b_sT"], name=f"gmlp_fwd_{tag}", sched=sched)
    att = _swa_fwd(d, P, cos2, sin2, S["sinks"], name=f"swa_fwd_{tag}", sched=sched)
    ya, yb, merged = _branch_merge(d, sgu, att, G["a"], G["b"], P, S["b_gate"], name=f"merge_{tag}", sched=sched)
    X1, X1b, xh1, rs1 = mm(
        merged, G["o"], "nn", name=f"oproj_ln_{tag}", tn=D, tk=1024, out_defs=_resid_ln_outs(), epilogue=_ep_resid_ln,
        extras=((X, "mn", 0), (S["ln1_g"], "row", 0), (S["ln1_b"], "row", 0)),
    )
    kv = mm(memb, G["xkv"], "nn", name=f"xkv_{tag}", out_defs=(("mn", BF16),))
    X2, X2b, xh2, rs2, q_b, o_b = _xattn_fwd(d, X1, X1b, kv, G["xq"], G["xo"], S["ln2_g"], S["ln2_b"], name=f"xattn_fwd_{tag}", sched=sched)
    H, A = mm(X2b, G["up"], "nt", name=f"up_{tag}", tm=1024, out_defs=(("mn", F32), ("mn", BF16)), epilogue=_ep_relu2)
    X3, X3b, xh3, rs3 = mm(
        A, G["down"], "nn", name=f"down_ln_{tag}", tn=D, tk=1024, out_defs=_resid_ln_outs(), epilogue=_ep_resid_ln,
        extras=((X2, "mn", 0), (S["ln3_g"], "row", 0), (S["ln3_b"], "row", 0)),
    )
    saved = dict(Xb=Xb, P=P, sgu=sgu, att=att, ya=ya, yb=yb, merged=merged, X1b=X1b, xh1=xh1, rs1=rs1, kv=kv, q_b=q_b,
                 o_b=o_b, X2b=X2b, xh2=xh2, rs2=rs2, H=H, A=A, xh3=xh3, rs3=rs3)
    return X3, X3b, saved


def _layer_bwd(d, dXout, sv, memb, G, S, cos2, sin2, tag, sched=None, on_grad=None, target=None):
    D = d.D
    mm = functools.partial(_matmul, sched=sched)
    emit = on_grad if on_grad is not None else (lambda n, g: None)
    wide = dict(tn=D)
    _, _, off_k, _, off_ga, off_gb = _offsets(d)
    bf = (("mn", BF16),)
    dz3, dz3b, dg3, db3, *loss = _ln_bwd(dXout, sv["xh3"], sv["rs3"], S["ln3_g"], name=f"ln3_bwd_{tag}", sched=sched, target=target)
    dHb = mm(dz3b, G["down"], "nt", name=f"down_dx_{tag}", tm=1024, out_defs=bf, epilogue=_ep_relu2_bwd, extras=((sv["H"], "mn", 0),))
    g_down = mm(sv["A"], dz3b, "tn", name=f"down_dw_{tag}", **wide)
    emit("down", g_down)
    dX2o = mm(dHb, G["up"], "nn", name=f"up_dx_{tag}", tn=D, epilogue=_ep_resid, extras=((dz3, "mn", 0),))
    g_up = mm(dHb, sv["X2b"], "tn", name=f"up_dw_{tag}", **wide)
    emit("up", g_up)
    dz2, dz2b, dg2, db2 = _ln_bwd(dX2o, sv["xh2"], sv["rs2"], S["ln2_g"], name=f"ln2_bwd_{tag}", sched=sched)
    dX1o, dq_b, dkv = _xattn_bwd(d, dz2, dz2b, sv["q_b"], sv["kv"], G["xq"], G["xo"], name=f"xattn_bwd_{tag}", sched=sched)
    g_xo = mm(dz2b, sv["o_b"], "tn", name=f"xo_dw_{tag}")
    emit("xo", g_xo)
    g_xq = mm(sv["X1b"], dq_b, "tn", name=f"xq_dw_{tag}")
    emit("xq", g_xq)
    dkv_b = dkv.transpose(1, 0, 2).reshape(d.MEM, 2 * d.XW).astype(BF16)
    g_xkv = mm(memb, dkv_b, "tn", name=f"xkv_dw_{tag}")
    emit("xkv", g_xkv)
    dz1, dz1b, dg1, db1 = _ln_bwd(dX1o, sv["xh1"], sv["rs1"], S["ln1_g"], name=f"ln1_bwd_{tag}", sched=sched)
    dya_b, dyb_b, dP, dgb_b, dbga, dbgb = mm(
        dz1b, G["o"], "nt", name=f"oproj_dx_{tag}", tm=1024, tn=512,
        out_defs=(("mn", BF16), ("mn", BF16), ("cols", BF16, d.IN, off_ga), ("mn", BF16), ("acc", F32), ("acc", F32)),
        epilogue=_ep_gate_bwd,
        extras=((sv["P"], "mn", off_ga), (sv["P"], "mn", off_gb), (sv["ya"], "mn", 0), (sv["yb"], "mn", 0),
                (S["b_gate"], "row", 0), (S["b_gate"], "row", D)),
    )
    g_o = mm(sv["merged"], dz1b, "tn", name=f"oproj_dw_{tag}", **wide)
    emit("o", g_o)
    dP = _place_cols(dP, dgb_b, off_gb, name=f"place_dgb_{tag}")
    dP, dwm, dbsT, dlvg, dlvb = _gmlp_bwd(d, sv["P"], dya_b, G["a"], S["ln_v_g"], S["ln_v_b"], S["w_s"], S["b_sT"], dP, name=f"gmlp_bwd_{tag}", sched=sched)
    g_a = mm(dya_b, sv["sgu"], "tn", name=f"bra_dw_{tag}")
    emit("a", g_a)
    dP, dkvr_b, dsk = _swa_bwd(d, sv["P"], dyb_b, G["b"], cos2, sin2, S["sinks"], dP, name=f"swa_bwd_{tag}", sched=sched)
    g_b = mm(dyb_b, sv["att"], "tn", name=f"brb_dw_{tag}")
    emit("b", g_b)
    dP = _place_cols(dP, dkvr_b, off_k, name=f"place_dkv_{tag}")
    small = dict(b_gate=jnp.concatenate([dbga, dbgb], axis=1), ln_v_g=dlvg, ln_v_b=dlvb, w_s=dwm, b_sT=dbsT, sinks=dsk,
                 ln1_g=dg1, ln1_b=db1, ln2_g=dg2, ln2_b=db2, ln3_g=dg3, ln3_b=db3)
    emit("small", small)
    g_in = mm(dP, sv["Xb"], "tn", name=f"proj_dw_{tag}", **wide)
    emit("in", g_in)
    dXin = mm(dP, G["in"], "nn", name=f"proj_dx_{tag}", tn=D, tk=_tile(d.IN, 1920), epilogue=_ep_resid, extras=((dz1, "mn", 0),))
    big = {"in": g_in, "a": g_a, "b": g_b, "o": g_o, "xq": g_xq, "xkv": g_xkv, "xo": g_xo, "up": g_up, "down": g_down}
    return dXin, big, small, (loss[0] if loss else None)


def _rope_tables(T):
    inv = 1.0 / (ROPE_THETA ** (jnp.arange(0, HEAD_DIM, 2, dtype=F32) / HEAD_DIM))
    ang = jnp.arange(T, dtype=F32)[:, None] * inv[None, :]
    cos, sin = jnp.cos(ang), jnp.sin(ang)
    reps = LANES // HEAD_DIM
    return jnp.concatenate([cos, cos] * reps, axis=1), jnp.concatenate([-sin, sin] * reps, axis=1)


def _local_step(d, x, mem, target, Gs, Ss, sched=None, on_grad=None):
    cos2, sin2 = _rope_tables(d.T)
    memb = mem.astype(BF16)
    X, Xb = x, x.astype(BF16)
    saved = []
    for l in range(DEPTH):
        X, Xb, sv = _layer_fwd(d, X, Xb, memb, Gs[l], Ss[l], cos2, sin2, f"l{l}", sched)
        saved.append(sv)
    bigs, smalls = [None] * DEPTH, [None] * DEPTH
    dX, loss = X, None
    for l in reversed(range(DEPTH)):
        emit = None if on_grad is None else functools.partial(on_grad, l)
        head = target if l == DEPTH - 1 else None
        dX, bigs[l], smalls[l], got = _layer_bwd(d, dX, saved[l], memb, Gs[l], Ss[l], cos2, sin2, f"l{l}", sched, emit, head)
        loss = got if got is not None else loss
    return loss, dX, bigs, smalls


def _place():
    x, y, c = lax.axis_index("x"), lax.axis_index("y"), lax.axis_index("c")
    return x, y, c, [(1 - x, y), (x, 1 - y), (1 - x, 1 - y)]


def _all_gather(shards, *, name, sched=None):
    n = len(shards)
    any_spec = pl.BlockSpec(memory_space=pl.ANY)

    def body(*refs):
        ins, outs = refs[:n], refs[n : 2 * n]
        send_sems, recv_sems, local_sems = refs[2 * n :]
        x, y, c, chips = _place()
        me, sibling = (x, y, c), (x, y, 1 - c)

        def rows(w, p):
            r = ins[w].shape[0]
            return outs[w].at[pl.ds((4 * p[0] + 2 * p[1] + p[2]) * r, r), :]

        def copy(w, k, block, to, src=None):
            return pltpu.make_async_remote_copy(
                src_ref=rows(w, block) if src is None else src, dst_ref=rows(w, block),
                send_sem=send_sems.at[7 * w + k], recv_sem=recv_sems.at[7 * w + k], device_id=to, device_id_type=MESH)

        mine = [pltpu.make_async_copy(ins[w], rows(w, me), local_sems.at[w]) for w in range(n)]
        for cp in mine:
            cp.start()
        first = []
        for w in range(n):
            first.append(copy(w, 0, me, sibling, src=ins[w]))
            first += [copy(w, 1 + j, me, (*chip, c), src=ins[w]) for j, chip in enumerate(chips)]
        for cp in first:
            cp.start()
        passed = []
        for j, chip in enumerate(chips):
            for w in range(n):
                copy(w, 1 + j, (*chip, c), me).wait_recv()
                fwd = copy(w, 4 + j, (*chip, c), sibling)
                fwd.start()
                passed.append(fwd)
        for w in range(n):
            copy(w, 0, sibling, me).wait_recv()
            for j, chip in enumerate(chips):
                copy(w, 4 + j, (*chip, 1 - c), me).wait_recv()
        for cp in first + passed:
            cp.wait_send()
        for cp in mine:
            cp.wait()

    return _hosted(
        body,
        name=name,
        sched=sched,
        in_specs=[any_spec] * n,
        out_specs=[any_spec] * n,
        out_shape=[jax.ShapeDtypeStruct((N_DEV * s.shape[0], s.shape[1]), s.dtype) for s in shards],
        scratch_shapes=[pltpu.SemaphoreType.DMA((7 * n,)), pltpu.SemaphoreType.DMA((7 * n,)), pltpu.SemaphoreType.DMA((n,))],
    )(*shards)


class _Task:
    def __init__(self, ins, out_shapes, n_remote, n_local, plan, done, aliases=None):
        self.ins, self.out_shapes, self.n_remote, self.n_local = list(ins), list(out_shapes), n_remote, n_local
        self.plan, self.done, self.aliases = plan, done, dict(aliases or {})


class _Sched:
    def __init__(self):
        self.pending = {}

    def at(self, host, task):
        self.pending.setdefault(host, []).append(task)

    def take(self, host):
        return self.pending.pop(host, [])


def _hosted(body, *, name, sched=None, tasks=(), grid=(), in_specs=None, out_specs=None, out_shape=(), scratch_shapes=(),
            compiler_params=None, input_output_aliases=None):
    tasks = list(tasks) + (sched.take(name) if sched is not None else [])
    scratch_shapes = list(scratch_shapes)
    kwargs = dict(name=name)
    core_aliases = dict(input_output_aliases or {})
    if grid:
        kwargs["grid"] = grid
    if compiler_params is not None:
        kwargs["compiler_params"] = compiler_params
    if not tasks:
        if in_specs is not None:
            kwargs.update(in_specs=in_specs, out_specs=out_specs)
        return pl.pallas_call(body, out_shape=out_shape, scratch_shapes=scratch_shapes, input_output_aliases=core_aliases, **kwargs)
    assert in_specs is not None and out_specs is not None, name
    single = not isinstance(out_shape, (list, tuple))
    core_shape = [out_shape] if single else list(out_shape)
    core_specs = [out_specs] if single else list(out_specs)
    in_specs = list(in_specs)
    n_in, n_out, n_scr = len(in_specs), len(core_shape), len(scratch_shapes)
    t_ins = [a for t in tasks for a in t.ins]
    t_outs = [s for t in tasks for s in t.out_shapes]
    n_rem = sum(t.n_remote for t in tasks)
    n_loc = sum(t.n_local for t in tasks)
    aliases, pos_in, pos_out = dict(core_aliases), n_in, n_out
    for t in tasks:
        for a, b in t.aliases.items():
            aliases[pos_in + a] = pos_out + b
        pos_in += len(t.ins)
        pos_out += len(t.out_shapes)
    any_spec = pl.BlockSpec(memory_space=pl.ANY)

    def wrapped(*refs):
        core_in, t_in = refs[:n_in], refs[n_in : n_in + len(t_ins)]
        o0 = n_in + len(t_ins)
        core_out, t_out = refs[o0 : o0 + n_out], refs[o0 + n_out : o0 + n_out + len(t_outs)]
        s0 = o0 + n_out + len(t_outs)
        core_scr = refs[s0 : s0 + n_scr]
        send_sems, recv_sems, local_sems = refs[s0 + n_scr :]
        remote, local = [], []
        pi = po = pr = pq = 0
        for t in tasks:
            r, q = t.plan(t_in[pi : pi + len(t.ins)], t_out[po : po + len(t.out_shapes)], send_sems, recv_sems, local_sems, pr, pq)
            assert len(r) == t.n_remote and len(q) == t.n_local
            remote += r
            local += q
            pi, po, pr, pq = pi + len(t.ins), po + len(t.out_shapes), pr + t.n_remote, pq + t.n_local

        def start():
            for cp in local + remote:
                cp.start()

        def finish():
            for cp in remote + local:
                cp.wait()

        if grid:
            ids = [pl.program_id(a) for a in range(len(grid))]
            first = functools.reduce(jnp.logical_and, [i == 0 for i in ids])
            last = functools.reduce(jnp.logical_and, [i == g - 1 for i, g in zip(ids, grid)])
            pl.when(first)(start)
            body(*core_in, *core_out, *core_scr)
            pl.when(last)(finish)
        else:
            start()
            body(*core_in, *core_out, *core_scr)
            finish()

    call = pl.pallas_call(
        wrapped,
        in_specs=in_specs + [any_spec] * len(t_ins),
        out_specs=core_specs + [any_spec] * len(t_outs),
        out_shape=core_shape + t_outs,
        scratch_shapes=scratch_shapes
        + [pltpu.SemaphoreType.DMA((n_rem,)), pltpu.SemaphoreType.DMA((n_rem,)), pltpu.SemaphoreType.DMA((max(n_loc, 1),))],
        input_output_aliases=aliases,
        **kwargs,
    )

    def run(*operands):
        outs = call(*operands, *t_ins)
        po = n_out
        for t in tasks:
            t.done(list(outs[po : po + len(t.out_shapes)]))
            po += len(t.out_shapes)
        return outs[0] if single else list(outs[:n_out])

    return run


def _comm_only(tasks, *, name):
    _hosted(lambda: None, name=name, tasks=tasks, in_specs=[], out_shape=[], out_specs=[])()


def _rows(ref, block, n_blocks):
    r = ref.shape[0] // n_blocks
    return ref.at[pl.ds(block * r, r), :]


def _remote(src, dst, send_sems, recv_sems, k, to):
    return pltpu.make_async_remote_copy(src_ref=src, dst_ref=dst, send_sem=send_sems.at[k], recv_sem=recv_sems.at[k],
                                        device_id=to, device_id_type=MESH)


def _gather_stage1(shards, done, part=(0, 1), into=None):
    n = len(shards)
    k, n_parts = part

    def plan(ins, outs, send_sems, recv_sems, local_sems, pr, pq):
        x, y, c, chips = _place()
        mine = 4 * x + 2 * y + c
        remote, local = [], []
        for w in range(n):
            r = shards[w].shape[0]
            rp = r // n_parts
            src = ins[w].at[pl.ds(k * rp, rp), :]
            dst = outs[w].at[pl.ds(mine * r + k * rp, rp), :]
            local.append(pltpu.make_async_copy(src, dst, local_sems.at[pq + w]))
            for j, to in enumerate([(x, y, 1 - c)] + [(*chip, c) for chip in chips]):
                remote.append(_remote(src, dst, send_sems, recv_sems, pr + 4 * w + j, to))
        return remote, local

    shapes = [jax.ShapeDtypeStruct((N_DEV * s.shape[0], s.shape[1]), s.dtype) for s in shards]
    if into is None:
        return _Task(shards, shapes, 4 * n, n, plan, done)
    return _Task(list(shards) + list(into), shapes, 4 * n, n, plan, done, aliases={n + w: w for w in range(n)})


def _gather_stage2(partial, done):
    n = len(partial)

    def plan(ins, outs, send_sems, recv_sems, local_sems, pr, pq):
        x, y, c, chips = _place()
        remote = []
        for w in range(n):
            for j, chip in enumerate(chips):
                rows = _rows(outs[w], 4 * chip[0] + 2 * chip[1] + c, N_DEV)
                remote.append(_remote(rows, rows, send_sems, recv_sems, pr + 3 * w + j, (x, y, 1 - c)))
        return remote, []

    shapes = [jax.ShapeDtypeStruct(p.shape, p.dtype) for p in partial]
    return _Task(partial, shapes, 3 * n, 0, plan, done, aliases={w: w for w in range(n)})


def _sibling_stage(grads, done):
    n = len(grads)

    def plan(ins, outs, send_sems, recv_sems, local_sems, pr, pq):
        x, y, c, _ = _place()
        remote = []
        for w in range(n):
            for k in range(4):
                src = _rows(ins[w], 4 * (k // 2) + 2 * (k % 2) + (1 - c), N_DEV)
                remote.append(_remote(src, _rows(outs[w], k, 4), send_sems, recv_sems, pr + 4 * w + k, (x, y, 1 - c)))
        return remote, []

    shapes = [jax.ShapeDtypeStruct((g.shape[0] // 2, g.shape[1]), g.dtype) for g in grads]
    return _Task(grads, shapes, 4 * n, 0, plan, done)


def _chips_stage(sends, done, part=(0, 1), into=None):
    n = len(sends)
    k, n_parts = part

    def plan(ins, outs, send_sems, recv_sems, local_sems, pr, pq):
        x, y, c, chips = _place()
        remote = []
        for w in range(n):
            r = sends[w].shape[0] // 3
            rp = r // n_parts
            for j, chip in enumerate(chips):
                rows = pl.ds(j * r + k * rp, rp)
                remote.append(_remote(ins[w].at[rows, :], outs[w].at[rows, :], send_sems, recv_sems, pr + 3 * w + j, (*chip, c)))
        return remote, []

    shapes = [jax.ShapeDtypeStruct(s.shape, s.dtype) for s in sends]
    if into is None:
        return _Task(sends, shapes, 3 * n, 0, plan, done)
    return _Task(list(sends) + list(into), shapes, 3 * n, 0, plan, done, aliases={n + w: w for w in range(n)})


def _pair_sum(grad, got, place, *, name):
    R, C = grad.shape
    r = R // N_DEV
    tr = _tile(r, 256, 16)
    nt = r // tr

    def chip_of(s, pr):
        fx = jnp.where((s == 1) | (s == 3), 1, 0)
        fy = jnp.where(s >= 2, 1, 0)
        return pr[0] ^ fx, pr[1] ^ fy

    def grad_map(s, t, pr):
        kx, ky = chip_of(s, pr)
        return ((4 * kx + 2 * ky + pr[2]) * nt + t, 0)

    def got_map(s, t, pr):
        kx, ky = chip_of(s, pr)
        return ((2 * kx + ky) * nt + t, 0)

    def body(pr, g_ref, r_ref, own_ref, send_ref):
        s = pl.program_id(0)
        tot = g_ref[...] + r_ref[...]

        @pl.when(s == 0)
        def _():
            own_ref[...] = tot

        @pl.when(s > 0)
        def _():
            send_ref[...] = tot.astype(BF16)

    return pl.pallas_call(
        body,
        name=name,
        grid_spec=pltpu.PrefetchScalarGridSpec(
            num_scalar_prefetch=1,
            grid=(4, nt),
            in_specs=[pl.BlockSpec((tr, C), grad_map), pl.BlockSpec((tr, C), got_map)],
            out_specs=[
                pl.BlockSpec((tr, C), lambda s, t, pr: (jnp.where(s == 0, t, nt - 1), 0)),
                pl.BlockSpec((tr, C), lambda s, t, pr: (jnp.where(s == 0, 0, (s - 1) * nt + t), 0)),
            ],
        ),
        out_shape=[jax.ShapeDtypeStruct((r, C), F32), jax.ShapeDtypeStruct((3 * r, C), BF16)],
        compiler_params=_params(("arbitrary", "arbitrary"), [((tr, C), F32)] * 4),
    )(place, grad, got)


def _adam_vals(w, g, m, v):
    m = ADAM_B1 * m + (1.0 - ADAM_B1) * g
    v = ADAM_B2 * v + (1.0 - ADAM_B2) * (g * g)
    m_hat = m / (1.0 - ADAM_B1**ADAM_STEP)
    v_hat = v / (1.0 - ADAM_B2**ADAM_STEP)
    delta = -ADAM_LR * (m_hat / (jnp.sqrt(v_hat) + ADAM_EPS) + ADAM_WD * w)
    return delta, m, v


def _adam_big(w3, m3, v3, own, got, layer, transposed, prev, *, name, sched=None):
    _, A, B = w3.shape
    ta = _tile(A, 256, 16)
    nt = A // ta
    b_pad = (-B) % LANES

    def body(*refs):
        w_ref, m_ref, v_ref, own_ref, g1_ref, g2_ref, g3_ref = refs[:7]
        g_ref, d_ref, nm_ref, nv_ref = refs[-4:]
        g = ((own_ref[...] + g1_ref[...].astype(F32)) + g2_ref[...].astype(F32)) + g3_ref[...].astype(F32)
        if transposed:
            if b_pad:
                g = jnp.concatenate([g, jnp.zeros((b_pad, ta), F32)], axis=0)
            g = g.T[:, :B]
        g_ref[...] = g
        d_ref[...], nm_ref[...], nv_ref[...] = _adam_vals(w_ref[...], g, m_ref[...], v_ref[...])

    nat = pl.BlockSpec((None, ta, B), lambda t: (layer, t, 0))
    if transposed:
        parts = [pl.BlockSpec((B, ta), lambda t: (0, t))] + [pl.BlockSpec((B, ta), lambda t, j=j: (j, t)) for j in range(3)]
    else:
        parts = [pl.BlockSpec((ta, B), lambda t: (t, 0))] + [pl.BlockSpec((ta, B), lambda t, j=j: (j * nt + t, 0)) for j in range(3)]
    keep = [] if prev is None else list(prev)
    outs = _hosted(
        body,
        name=name,
        sched=sched,
        grid=(nt,),
        in_specs=[nat, nat, nat] + parts + [pl.BlockSpec(memory_space=pl.ANY)] * len(keep),
        out_specs=[nat] * 4,
        out_shape=[jax.ShapeDtypeStruct(w3.shape, F32)] * 4,
        input_output_aliases={7 + k: k for k in range(len(keep))},
        compiler_params=_params(("arbitrary",), [((ta, B), F32)] * 10),
    )(w3, m3, v3, own, got, got, got, *keep)
    return list(outs)


def _adam_small(w, m, v, parts, *, name, sched=None):
    R = w.shape[0]

    def body(w_ref, m_ref, v_ref, p_ref, g_ref, d_ref, nm_ref, nv_ref):
        g = p_ref[pl.ds(0, R), :]
        for k in range(1, N_DEV):
            g = g + p_ref[pl.ds(k * R, R), :]
        g_ref[...] = g
        d_ref[...], nm_ref[...], nv_ref[...] = _adam_vals(w_ref[...], g, m_ref[...], v_ref[...])

    return _hosted(
        body,
        name=name,
        sched=sched,
        out_shape=[jax.ShapeDtypeStruct((R, LANES), F32)] * 4,
        compiler_params=_params(None, [((R, LANES), F32)] * 16),
    )(w, m, v, parts)


SMALL_NAMES = ("b_gate", "ln_v_g", "ln_v_b", "w_s", "b_s", "sinks", "ln1_g", "ln1_b", "ln2_g", "ln2_b", "ln3_g", "ln3_b")
PACK_ALIGN = 8 * LANES


def _pack(arrays):
    flat = []
    for a in arrays:
        a = a.reshape(-1)
        flat.append(jnp.pad(a, (0, (-a.shape[0]) % PACK_ALIGN)))
    return jnp.concatenate(flat).reshape(-1, LANES)


def _unpack(packed, shapes):
    flat = packed.reshape(-1)
    out, pos = [], 0
    for s in shapes:
        n = 1
        for e in s:
            n *= e
        out.append(flat[pos : pos + n].reshape(s))
        pos += n + (-n) % PACK_ALIGN
    return out


BIG = (("in", "w_in", True), ("a", "w_br_a", True), ("b", "w_br_b", True), ("o", "w_o", False), ("xq", "w_xq", False),
       ("xkv", "w_xkv", False), ("xo", "w_xo", True), ("up", "w_up", True), ("down", "w_down", False))


SMALL_BIG = ("a", "b", "o", "xq", "xkv", "xo")
GATHER_PLAN = (
    (0, ("in",), None, None),
    (0, SMALL_BIG, ("proj_l0",), "gmlp_fwd_l0"),
    (0, ("up",), ("swa_fwd_l0", "merge_l0"), "oproj_ln_l0"),
    (0, ("down",), ("oproj_ln_l0", "xattn_fwd_l0"), "up_l0"),
    (1, ("in",), ("up_l0",), "down_ln_l0"),
    (1, SMALL_BIG, ("down_ln_l0",), "proj_l1"),
    (1, ("up",), ("down_ln_l0", "proj_l1"), "gmlp_fwd_l1"),
    (1, ("down",), ("swa_fwd_l1", "merge_l1"), "oproj_ln_l1"),
)
EARLY_SMALL_BIG = ("o", "xq", "xkv", "xo")
LATE_SMALL_BIG = ("a", "b")
GRAD_HOSTS = {
    1: {"down": ("xattn_bwd_l1", "swa_bwd_l1"), "up": ("xattn_bwd_l1", "proj_dw_l1"),
        **{g: ("gmlp_bwd_l1", "proj_dx_l1") for g in EARLY_SMALL_BIG}, **{g: ("proj_dw_l1", "proj_dx_l1") for g in LATE_SMALL_BIG},
        "in": ("proj_dx_l1", "down_dx_l0")},
    0: {"down": ("xattn_bwd_l0", ("oproj_dx_l0", "gmlp_bwd_l0")), "up": ("xattn_bwd_l0", "swa_bwd_l0"),
        **{g: ("gmlp_bwd_l0", "proj_dw_l0") for g in EARLY_SMALL_BIG}, **{g: ("proj_dw_l0", "proj_dx_l0") for g in LATE_SMALL_BIG},
        "in": (None, "proj_dx_l0")},
}
SMALL_GRAD_HOSTS = ("proj_dw_l0", "proj_dx_l0")


def kernel(x, mem, w_in, b_gate, ln_v_g, ln_v_b, w_s, b_s, sinks, w_br_a, w_br_b, w_o, ln1_g, ln1_b, w_xq, w_xkv, w_xo, ln2_g, ln2_b, w_up, w_down, ln3_g, ln3_b, loss_target, m_w_in, m_b_gate, m_ln_v_g, m_ln_v_b, m_w_s, m_b_s, m_sinks, m_w_br_a, m_w_br_b, m_w_o, m_ln1_g, m_ln1_b, m_w_xq, m_w_xkv, m_w_xo, m_ln2_g, m_ln2_b, m_w_up, m_w_down, m_ln3_g, m_ln3_b, v_w_in, v_b_gate, v_ln_v_g, v_ln_v_b, v_w_s, v_b_s, v_sinks, v_w_br_a, v_w_br_b, v_w_o, v_ln1_g, v_ln1_b, v_w_xq, v_w_xkv, v_w_xo, v_ln2_g, v_ln2_b, v_w_up, v_w_down, v_ln3_g, v_ln3_b):
    given = dict(locals())
    T, D = x.shape[1], x.shape[2]
    IN, GW, AW, XW, DFF = w_in.shape[2] * N_DEV, ln_v_g.shape[1], w_br_b.shape[1], w_xq.shape[2], w_up.shape[2] * N_DEV
    KVW = (IN - 2 * GW - AW - 2 * D) // 2
    d = Dims(T=T, D=D, IN=IN, GW=GW, G=GW // GROUP_DIM, AW=AW, KVW=KVW, HQ=AW // HEAD_DIM, HKV=KVW // HEAD_DIM, XW=XW,
             XH=XW // X_HEAD_DIM, MEM=mem.shape[1], DFF=DFF)
    place = jnp.stack([lax.axis_index("x"), lax.axis_index("y"), lax.axis_index("c")]).astype(jnp.int32)

    sched = _Sched()
    big_of = {g: (p, tr) for g, p, tr in BIG}

    def shard(l, g):
        p, tr = big_of[g]
        return (given[p][l].T if tr else given[p][l]).astype(BF16)

    Gs = [{}, {}]

    def plan_gather(l, names, hosts1, host2):
        shards = [shard(l, g) for g in names]
        if hosts1 is None:
            Gs[l].update(zip(names, _all_gather(shards, name=f"gather_{names[0]}_l{l}")))
            return

        def register(k, into):
            def done(outs):
                if k + 1 < len(hosts1):
                    register(k + 1, outs)
                else:
                    sched.at(host2, _gather_stage2(outs, lambda full: Gs[l].update(zip(names, full))))

            sched.at(hosts1[k], _gather_stage1(shards, done, part=(k, len(hosts1)), into=into))

        register(0, None)

    for entry in GATHER_PLAN:
        plan_gather(*entry)
    Ss = []
    for l in range(DEPTH):
        S = {n: given[n][l].reshape(1, -1) for n in SMALL_NAMES if n not in ("w_s", "b_s", "sinks")}
        S["w_s"], S["b_sT"], S["sinks"] = w_s[l], b_s[l].T, sinks[l]
        Ss.append(S)

    owns, recvs = {}, {}

    smalls_seen = {}
    gathered_small = []

    def pack_small(src):
        parts = []
        for l in range(DEPTH):
            for n in SMALL_NAMES:
                a = src[l]["b_sT"].T if n == "b_s" else src[l][n]
                parts.append(a[0, : d.HQ] if n == "sinks" else a)
        return _pack(parts)

    def on_small(l, small):
        smalls_seen[l] = small
        if len(smalls_seen) == DEPTH:
            host1, host2 = SMALL_GRAD_HOSTS
            sched.at(host1, _gather_stage1([pack_small(smalls_seen)], lambda part: sched.at(
                host2, _gather_stage2(part, lambda full: gathered_small.append(full[0])))))

    def on_grad(l, g, grad):
        if g == "small":
            return on_small(l, grad)
        sib_host, chips_host = GRAD_HOSTS[l][g]

        def after_sibling(got):
            owns[(l, g)], send = _pair_sum(grad, got[0], place, name=f"grad_pair_sum_{g}_l{l}")
            hosts = chips_host if isinstance(chips_host, tuple) else (chips_host,)

            def register(k, into):
                def done(r):
                    if k + 1 < len(hosts):
                        register(k + 1, r)
                    else:
                        recvs[(l, g)] = r[0]

                task = _chips_stage([send], done, part=(k, len(hosts)), into=into)
                if hosts[k] is None:
                    _comm_only([task], name=f"grad_chips_{g}_l{l}")
                else:
                    sched.at(hosts[k], task)

            register(0, None)

        task = _sibling_stage([grad], after_sibling)
        if sib_host is None:
            _comm_only([task], name=f"grad_sibling_{g}_l{l}")
        else:
            sched.at(sib_host, task)

    loss, dX, bigs, smalls = _local_step(d, x[0], mem[0], loss_target[0], Gs, Ss, sched, on_grad)
    loss = lax.psum(loss[0, 0], ("x", "y", "c"))

    assert not sched.pending, sorted(sched.pending)

    def viewed(p, tr):
        return tr and given[p].shape[2] % LANES != 0

    res = {p: None for _, p, _ in BIG}
    for l in reversed(range(DEPTH)):
        for g, p, tr in BIG:
            view = viewed(p, tr)
            w3, m3, v3 = ((jnp.swapaxes(a, 1, 2) if view else a) for a in (given[p], given["m_" + p], given["v_" + p]))
            res[p] = _adam_big(w3, m3, v3, owns[(l, g)], recvs[(l, g)], l, tr and not view, res[p], name=f"adamw_{g}_l{l}")
    for g, p, tr in BIG:
        if viewed(p, tr):
            res[p] = [jnp.swapaxes(a, 1, 2) for a in res[p]]

    shapes = [given[n].shape[1:] for n in SMALL_NAMES]
    pw, pm, pv = (_pack([given[pre + n][l] for l in range(DEPTH) for n in SMALL_NAMES]) for pre in ("", "m_", "v_"))
    small_out = [_unpack(o, shapes * DEPTH) for o in _adam_small(pw, pm, pv, gathered_small[0], name="adamw_small")]

    names = ["w_in", "b_gate", "ln_v_g", "ln_v_b", "w_s", "b_s", "sinks", "w_br_a", "w_br_b", "w_o", "ln1_g", "ln1_b", "w_xq",
             "w_xkv", "w_xo", "ln2_g", "ln2_b", "w_up", "w_down", "ln3_g", "ln3_b"]
    out = [loss, dX[None]]
    for kind in range(4):
        for n in names:
            if n in SMALL_NAMES:
                i = SMALL_NAMES.index(n)
                out.append(jnp.stack([small_out[kind][l * len(SMALL_NAMES) + i] for l in range(DEPTH)]))
            else:
                out.append(res[n][kind])
    return tuple(out)
```

```python
import collections
import functools

import jax
import jax.numpy as jnp
from jax import lax
from jax.experimental import pallas as pl
from jax.experimental.pallas import tpu as pltpu

F32 = jnp.float32
BF16 = jnp.bfloat16
MESH = pl.DeviceIdType.MESH

N_DEV = 8
DEPTH = 2
CHUNK = 128
HEAD_DIM = 64
ROPE_HALF = HEAD_DIM // 2
X_HEAD_DIM = 128
GROUP_DIM = 128
LANES = 128
ROPE_THETA = 10000.0
LN_EPS = 1e-5
ALPHA = (2 * DEPTH) ** 0.25
ADAM_LR = 0.001
ADAM_B1 = 0.9
ADAM_B2 = 0.999
ADAM_EPS = 1e-08
ADAM_WD = 0.01
ADAM_STEP = 10
VMEM_BYTES = 64 * 1024 * 1024
VMEM_TEMP_BYTES = 20 * 1024 * 1024

Dims = collections.namedtuple("Dims", "T D IN GW G AW KVW HQ HKV XW XH MEM DFF")


def _offsets(d):
    off_v = d.GW
    off_q = 2 * d.GW
    off_k = off_q + d.AW
    off_va = off_k + d.KVW
    off_ga = off_va + d.KVW
    off_gb = off_ga + d.D
    return off_v, off_q, off_k, off_va, off_ga, off_gb


def _tile(dim, pref, align=LANES):
    if dim <= pref:
        return dim
    t = pref - pref % align
    while t >= align:
        if dim % t == 0:
            return t
        t -= align
    return dim


def _nbytes(shape, dtype):
    n = 1
    for s in shape:
        n *= s
    return n * jnp.dtype(dtype).itemsize


def _params(sem, blocks, scratch=0):
    need = 2 * sum(_nbytes(s, t) for s, t in blocks) + scratch + VMEM_TEMP_BYTES
    limit = min(max(need, 32 * 1024 * 1024), VMEM_BYTES - 4 * 1024 * 1024)
    return pltpu.CompilerParams(dimension_semantics=sem, vmem_limit_bytes=limit)


def _dot(a, b, kind):
    dn = {"nn": (((1,), (0,)), ((), ())), "nt": (((1,), (1,)), ((), ())), "tn": (((0,), (0,)), ((), ()))}[kind]
    return lax.dot_general(a, b, dn, preferred_element_type=F32)


def _gelu(x):
    c = 0.7978845608028654
    t = jnp.tanh(c * (x + 0.044715 * (x * x * x)))
    return 0.5 * x * (1.0 + t)


def _gelu_grad(x):
    c = 0.7978845608028654
    x2 = x * x
    t = jnp.tanh(c * (x + 0.044715 * (x2 * x)))
    return 0.5 * (1.0 + t) + 0.5 * x * (1.0 - t * t) * (c * (1.0 + 3.0 * 0.044715 * x2))


def _ln_fwd(z, g, b):
    mu = jnp.mean(z, axis=-1, keepdims=True)
    zc = z - mu
    var = jnp.mean(zc * zc, axis=-1, keepdims=True)
    rstd = lax.rsqrt(var + LN_EPS)
    xhat = zc * rstd
    return xhat * g + b, xhat, rstd


def _ln_bwd_vals(dy, xhat, rstd, g):
    gd = dy * g
    m1 = jnp.mean(gd, axis=-1, keepdims=True)
    m2 = jnp.mean(gd * xhat, axis=-1, keepdims=True)
    return rstd * (gd - m1 - xhat * m2)


def _acc_store(ref, val, first):
    @pl.when(first)
    def _():
        ref[...] = val

    @pl.when(jnp.logical_not(first))
    def _():
        ref[...] += val


def _matmul(a, b, kind, *, name, tm=512, tn=1024, tk=2048, out_defs=(("mn", F32),), epilogue=None, extras=(), sched=None):
    if kind == "nn":
        (M, K), (K2, N) = a.shape, b.shape
    elif kind == "nt":
        (M, K), (N, K2) = a.shape, b.shape
    else:
        (K, M), (K2, N) = a.shape, b.shape
    assert K == K2, (a.shape, b.shape, kind)
    tm, tn, tk = _tile(M, tm), _tile(N, tn), _tile(K, tk)
    nk = K // tk
    blocks = []

    def spec(shape, imap, dtype):
        blocks.append((shape, dtype))
        return pl.BlockSpec(shape, imap)

    if kind == "tn":
        a_spec = spec((tk, tm), lambda j, i, k: (k, i), a.dtype)
    else:
        a_spec = spec((tm, tk), lambda j, i, k: (i, k), a.dtype)
    if kind == "nt":
        b_spec = spec((tn, tk), lambda j, i, k: (j, k), b.dtype)
    else:
        b_spec = spec((tk, tn), lambda j, i, k: (k, j), b.dtype)
    in_specs = [a_spec, b_spec]
    operands = [a, b]
    for arr, ekind, off in extras:
        if ekind == "mn":
            assert off % tn == 0
            in_specs.append(spec((tm, tn), lambda j, i, k, o=off // tn: (i, j + o), arr.dtype))
        elif ekind == "row":
            assert off % tn == 0
            in_specs.append(spec((1, tn), lambda j, i, k, o=off // tn: (0, j + o), arr.dtype))
        else:
            in_specs.append(spec((tm, 1), lambda j, i, k: (i, 0), arr.dtype))
        operands.append(arr)
    out_shape, out_specs = [], []
    for od in out_defs:
        okind, dt = od[0], od[1]
        if okind == "mn":
            out_shape.append(jax.ShapeDtypeStruct((M, N), dt))
            out_specs.append(spec((tm, tn), lambda j, i, k: (i, j), dt))
        elif okind == "cols":
            assert od[3] % tn == 0
            out_shape.append(jax.ShapeDtypeStruct((M, od[2]), dt))
            out_specs.append(spec((tm, tn), lambda j, i, k, o=od[3] // tn: (i, j + o), dt))
        elif okind == "m1":
            assert N == tn
            out_shape.append(jax.ShapeDtypeStruct((M, 1), dt))
            out_specs.append(spec((tm, 1), lambda j, i, k: (i, 0), dt))
        else:
            out_shape.append(jax.ShapeDtypeStruct((1, N), F32))
            out_specs.append(spec((1, tn), lambda j, i, k: (0, j), F32))
    n_ex, n_out = len(extras), len(out_defs)
    ep = epilogue if epilogue is not None else (lambda acc: (acc,))
    in_place = nk > 1 and epilogue is None and tuple(out_defs) == (("mn", F32),)

    def body(*refs):
        a_ref, b_ref = refs[0], refs[1]
        ex_refs = refs[2 : 2 + n_ex]
        out_refs = refs[2 + n_ex : 2 + n_ex + n_out]
        i = pl.program_id(1)
        k = pl.program_id(2)

        def product():
            return _dot(a_ref[...], b_ref[...], kind)

        def finish(acc):
            vals = ep(acc, *[r[...] for r in ex_refs])
            for od, r, v in zip(out_defs, out_refs, vals):
                if od[0] == "acc":
                    _acc_store(r, v, i == 0)
                else:
                    r[...] = v.astype(od[1])

        if nk == 1:
            finish(product())
            return
        acc_ref = out_refs[0] if in_place else refs[-1]

        @pl.when(k == 0)
        def _():
            acc_ref[...] = product()

        if in_place:
            @pl.when(k > 0)
            def _():
                acc_ref[...] += product()
        else:
            if nk > 2:
                @pl.when(jnp.logical_and(k > 0, k < nk - 1))
                def _():
                    acc_ref[...] += product()

            @pl.when(k == nk - 1)
            def _():
                finish(acc_ref[...] + product())

    scratch = [pltpu.VMEM((tm, tn), F32)] if nk > 1 and not in_place else []
    outs = _hosted(
        body,
        name=name,
        sched=sched,
        grid=(N // tn, M // tm, nk),
        in_specs=in_specs,
        out_specs=out_specs,
        out_shape=out_shape,
        scratch_shapes=scratch,
        compiler_params=_params(("arbitrary", "arbitrary", "arbitrary"), blocks, _nbytes((tm, tn), F32) if scratch else 0),
    )(*operands)
    return outs if len(outs) > 1 else outs[0]


def _ep_resid_ln(acc, resid, g, b):
    y, xhat, rstd = _ln_fwd(ALPHA * resid + acc, g, b)
    return y, y, xhat, rstd


def _ep_resid(acc, resid):
    return (ALPHA * resid + acc,)


def _ep_resid_ln_bwd(acc, resid, xhat, rstd, g):
    dy = ALPHA * resid + acc
    dz = _ln_bwd_vals(dy, xhat, rstd, g)
    return dz, dz, jnp.sum(dy * xhat, axis=0, keepdims=True), jnp.sum(dy, axis=0, keepdims=True)


def _ep_relu2(acc):
    r = jnp.maximum(acc, 0.0)
    return acc, r * r


def _ep_relu2_bwd(acc, h):
    return (acc * (2.0 * jnp.maximum(h, 0.0)),)


def _ep_gate_bwd(acc, ga, gb, ya, yb, bga, bgb):
    sa = jax.nn.sigmoid(ga + bga)
    sb = jax.nn.sigmoid(gb + bgb)
    dga = acc * ya * (sa * (1.0 - sa))
    dgb = acc * yb * (sb * (1.0 - sb))
    return (acc * sa, acc * sb, dga, dgb, jnp.sum(dga, axis=0, keepdims=True), jnp.sum(dgb, axis=0, keepdims=True))


def _ln_bwd(dy, xhat, rstd, g, *, name, sched=None, target=None):
    T, D = dy.shape
    tm = _tile(T, 256)
    head = target is not None

    def body(*refs):
        dy_ref, xh_ref, rs_ref, g_ref = refs[:4]
        dz_ref, dzb_ref, dg_ref, db_ref = refs[4 + head : 8 + head]
        first = pl.program_id(0) == 0
        dyv, xh = dy_ref[...], xh_ref[...]
        if head:
            err = dyv - refs[4][...]
            dyv = err * (1.0 / D)
            part = 0.5 * jnp.sum(jnp.sum(err * err, axis=1, keepdims=True) * (1.0 / D), axis=0, keepdims=True)
            _acc_store(refs[-1], part, first)
        dz = _ln_bwd_vals(dyv, xh, rs_ref[...], g_ref[...])
        dz_ref[...] = dz
        dzb_ref[...] = dz.astype(BF16)
        _acc_store(dg_ref, jnp.sum(dyv * xh, axis=0, keepdims=True), first)
        _acc_store(db_ref, jnp.sum(dyv, axis=0, keepdims=True), first)

    row = pl.BlockSpec((tm, D), lambda i: (i, 0))
    vec = pl.BlockSpec((1, D), lambda i: (0, 0))
    one = pl.BlockSpec((1, 1), lambda i: (0, 0))
    return _hosted(
        body,
        name=name,
        sched=sched,
        grid=(T // tm,),
        in_specs=[row, row, pl.BlockSpec((tm, 1), lambda i: (i, 0)), vec] + [row] * head,
        out_specs=[row, row, vec, vec] + [one] * head,
        out_shape=[
            jax.ShapeDtypeStruct((T, D), F32),
            jax.ShapeDtypeStruct((T, D), BF16),
            jax.ShapeDtypeStruct((1, D), F32),
            jax.ShapeDtypeStruct((1, D), F32),
        ] + [jax.ShapeDtypeStruct((1, 1), F32)] * head,
        compiler_params=_params(("arbitrary",), [((tm, D), F32)] * (4 + head)),
    )(*([dy, xhat, rstd, g] + [target] * head))


def _masked_mix_weights(wm_ref, G):
    r = lax.broadcasted_iota(jnp.int32, (CHUNK, CHUNK), 0)
    c = lax.broadcasted_iota(jnp.int32, (CHUNK, CHUNK), 1)
    return [jnp.where(c <= r, wm_ref[g], 0.0).astype(BF16) for g in range(G)]


def _gmlp_fwd(d, P, lnv_g, lnv_b, wm, bsT, *, name, sched=None):
    T, GW, G = d.T, d.GW, d.G
    tm = _tile(T, 256)
    nc = tm // CHUNK

    def body(u_ref, v_ref, g_ref, b_ref, wm_ref, bs_ref, o_ref):
        gu = _gelu(u_ref[...])
        vn, _, _ = _ln_fwd(_gelu(v_ref[...]), g_ref[...], b_ref[...])
        vn = vn.astype(BF16)
        wl = _masked_mix_weights(wm_ref, G)
        for c in range(nc):
            rows = slice(c * CHUNK, (c + 1) * CHUNK)
            for g in range(G):
                cols = slice(g * GROUP_DIM, (g + 1) * GROUP_DIM)
                mixed = _dot(wl[g], vn[rows, cols], "nn") + bs_ref[:, g : g + 1]
                o_ref[rows, cols] = (gu[rows, cols] * mixed).astype(BF16)

    return _hosted(
        body,
        name=name,
        sched=sched,
        grid=(T // tm,),
        in_specs=[
            pl.BlockSpec((tm, GW), lambda i: (i, 0)),
            pl.BlockSpec((tm, GW), lambda i: (i, 1)),
            pl.BlockSpec((1, GW), lambda i: (0, 0)),
            pl.BlockSpec((1, GW), lambda i: (0, 0)),
            pl.BlockSpec((G, CHUNK, CHUNK), lambda i: (0, 0, 0)),
            pl.BlockSpec((CHUNK, G), lambda i: (0, 0)),
        ],
        out_specs=pl.BlockSpec((tm, GW), lambda i: (i, 0)),
        out_shape=jax.ShapeDtypeStruct((T, GW), BF16),
        compiler_params=_params(("arbitrary",), [((tm, GW), F32)] * 3),
    )(P, P, lnv_g, lnv_b, wm, bsT)


def _gmlp_bwd(d, P, dya_b, Ga, lnv_g, lnv_b, wm, bsT, dP, *, name, sched=None):
    T, D, GW, G = d.T, d.D, d.GW, d.G
    tm = _tile(T, 256)
    nc = tm // CHUNK

    def body(u_ref, v_ref, dya_ref, ga_ref, g_ref, b_ref, wm_ref, bs_ref, dp_in, duv_ref, dwm_ref, dbs_ref, dlg_ref, dlb_ref, dgu_s, dvn_s):
        first = pl.program_id(0) == 0
        dsgu = _dot(dya_ref[...], ga_ref[...], "nn")
        u, v = u_ref[...], v_ref[...]
        gu = _gelu(u)
        lng = g_ref[...]
        vn, xh, rstd = _ln_fwd(_gelu(v), lng, b_ref[...])
        vn = vn.astype(BF16)
        wl = _masked_mix_weights(wm_ref, G)
        r = lax.broadcasted_iota(jnp.int32, (CHUNK, CHUNK), 0)
        cc = lax.broadcasted_iota(jnp.int32, (CHUNK, CHUNK), 1)
        lane_g = lax.broadcasted_iota(jnp.int32, (CHUNK, G), 1)
        dbs = jnp.zeros((CHUNK, G), F32)
        for g in range(G):
            cols = slice(g * GROUP_DIM, (g + 1) * GROUP_DIM)
            dw = jnp.zeros((CHUNK, CHUNK), F32)
            for c in range(nc):
                rows = slice(c * CHUNK, (c + 1) * CHUNK)
                mixed = _dot(wl[g], vn[rows, cols], "nn") + bs_ref[:, g : g + 1]
                ds = dsgu[rows, cols]
                dgu_s[rows, cols] = ds * mixed
                dmx = ds * gu[rows, cols]
                dbs = dbs + jnp.where(lane_g == g, jnp.sum(dmx, axis=1, keepdims=True), 0.0)
                dmx = dmx.astype(BF16)
                dw = dw + _dot(dmx, vn[rows, cols], "nt")
                dvn_s[rows, cols] = _dot(wl[g], dmx, "tn")
            _acc_store(dwm_ref.at[g], jnp.where(cc <= r, dw, 0.0), first)
        _acc_store(dbs_ref, dbs, first)
        dvn = dvn_s[...]
        _acc_store(dlg_ref, jnp.sum(dvn * xh, axis=0, keepdims=True), first)
        _acc_store(dlb_ref, jnp.sum(dvn, axis=0, keepdims=True), first)
        dgv = _ln_bwd_vals(dvn, xh, rstd, lng)
        duv_ref[:, :GW] = (dgu_s[...] * _gelu_grad(u)).astype(BF16)
        duv_ref[:, GW:] = (dgv * _gelu_grad(v)).astype(BF16)

    return _hosted(
        body,
        name=name,
        sched=sched,
        grid=(T // tm,),
        in_specs=[
            pl.BlockSpec((tm, GW), lambda i: (i, 0)),
            pl.BlockSpec((tm, GW), lambda i: (i, 1)),
            pl.BlockSpec((tm, D), lambda i: (i, 0)),
            pl.BlockSpec((D, GW), lambda i: (0, 0)),
            pl.BlockSpec((1, GW), lambda i: (0, 0)),
            pl.BlockSpec((1, GW), lambda i: (0, 0)),
            pl.BlockSpec((G, CHUNK, CHUNK), lambda i: (0, 0, 0)),
            pl.BlockSpec((CHUNK, G), lambda i: (0, 0)),
            pl.BlockSpec(memory_space=pl.ANY),
        ],
        out_specs=[
            pl.BlockSpec((tm, 2 * GW), lambda i: (i, 0)),
            pl.BlockSpec((G, CHUNK, CHUNK), lambda i: (0, 0, 0)),
            pl.BlockSpec((CHUNK, G), lambda i: (0, 0)),
            pl.BlockSpec((1, GW), lambda i: (0, 0)),
            pl.BlockSpec((1, GW), lambda i: (0, 0)),
        ],
        out_shape=[
            jax.ShapeDtypeStruct(dP.shape, BF16),
            jax.ShapeDtypeStruct((G, CHUNK, CHUNK), F32),
            jax.ShapeDtypeStruct((CHUNK, G), F32),
            jax.ShapeDtypeStruct((1, GW), F32),
            jax.ShapeDtypeStruct((1, GW), F32),
        ],
        scratch_shapes=[pltpu.VMEM((tm, GW), F32), pltpu.VMEM((tm, GW), F32)],
        input_output_aliases={8: 0},
        compiler_params=_params(
            ("arbitrary",), [((tm, GW), F32)] * 3 + [((tm, D), BF16), ((D, GW), BF16)], 2 * _nbytes((tm, GW), F32)
        ),
    )(P, P, dya_b, Ga, lnv_g, lnv_b, wm, bsT, dP)


def _swap_halves(x):
    w = x.shape[1]
    lane = lax.broadcasted_iota(jnp.int32, x.shape, 1)
    return jnp.where((lane % HEAD_DIM) < ROPE_HALF, pltpu.roll(x, w - ROPE_HALF, 1), pltpu.roll(x, ROPE_HALF, 1))


def _lane_tile(t, width):
    reps = width // LANES
    return t if reps == 1 else jnp.tile(t, (1, reps))


def _rope(x, cos2, sin2):
    w = x.shape[1]
    return x * _lane_tile(cos2, w) + _swap_halves(x) * _lane_tile(sin2, w)


def _rope_bwd(g, cos2, sin2):
    w = g.shape[1]
    return g * _lane_tile(cos2, w) - _swap_halves(g) * _lane_tile(sin2, w)


PAIR = LANES // HEAD_DIM


def _swa_valid(i):
    s = lax.broadcasted_iota(jnp.int32, (2 * CHUNK, CHUNK), 0)
    t = lax.broadcasted_iota(jnp.int32, (2 * CHUNK, CHUNK), 1)
    cur = jnp.logical_and(s >= CHUNK, s - CHUNK <= t)
    prev = jnp.logical_and(jnp.logical_and(s < CHUNK, s > t), i > 0)
    return jnp.logical_or(cur, prev)


def _half_variants(x, odd):
    lane = lax.broadcasted_iota(jnp.int32, x.shape, 1)
    if odd:
        hi = jnp.where(lane >= HEAD_DIM, x, jnp.zeros_like(x))
        return pltpu.roll(hi, HEAD_DIM, 1), hi
    lo = jnp.where(lane < HEAD_DIM, x, jnp.zeros_like(x))
    return lo, pltpu.roll(lo, HEAD_DIM, 1)


def _swa_probs_t(kx, qp, sink, valid):
    s = jnp.where(valid, _dot(kx, qp, "nt") * (HEAD_DIM**-0.5), -jnp.inf)
    m = jnp.maximum(jnp.max(s, axis=0, keepdims=True), sink)
    e = jnp.exp(s - m)
    e_s = jnp.exp(sink - m)
    den = jnp.sum(e, axis=0, keepdims=True) + e_s
    return e / den, e_s / den


def _swa_load(q_ref, kc_ref, kp_ref, vc_ref, vp_ref, cc, sc, cp, sp):
    qr = _rope(q_ref[...], cc, sc).astype(BF16)
    kcat = jnp.concatenate([_rope(kp_ref[...], cp, sp), _rope(kc_ref[...], cc, sc)], axis=0)
    vcat = jnp.concatenate([vp_ref[...], vc_ref[...]], axis=0)
    return qr, kcat, vcat


def _swa_specs(d):
    _, off_q, off_k, off_va, _, _ = _offsets(d)
    assert off_q % d.AW == 0 and off_k % d.KVW == 0 and off_va % d.KVW == 0
    prev = lambda i: jnp.maximum(i - 1, 0)
    return [
        pl.BlockSpec(memory_space=pltpu.SMEM),
        pl.BlockSpec((CHUNK, d.AW), lambda i, o=off_q // d.AW: (i, o)),
        pl.BlockSpec((CHUNK, d.KVW), lambda i, o=off_k // d.KVW: (i, o)),
        pl.BlockSpec((CHUNK, d.KVW), lambda i, o=off_k // d.KVW: (prev(i), o)),
        pl.BlockSpec((CHUNK, d.KVW), lambda i, o=off_va // d.KVW: (i, o)),
        pl.BlockSpec((CHUNK, d.KVW), lambda i, o=off_va // d.KVW: (prev(i), o)),
        pl.BlockSpec((CHUNK, LANES), lambda i: (i, 0)),
        pl.BlockSpec((CHUNK, LANES), lambda i: (i, 0)),
        pl.BlockSpec((CHUNK, LANES), lambda i: (prev(i), 0)),
        pl.BlockSpec((CHUNK, LANES), lambda i: (prev(i), 0)),
    ]


def _swa_fwd(d, P, cos2, sin2, sinks, *, name, sched=None):
    T, AW, HQ, HKV = d.T, d.AW, d.HQ, d.HKV
    grp = HQ // HKV
    assert grp % PAIR == 0 and HKV % PAIR == 0

    def body(sink_ref, q_ref, kc_ref, kp_ref, vc_ref, vp_ref, cc_ref, sc_ref, cp_ref, sp_ref, o_ref):
        i = pl.program_id(0)
        qr, kcat, vcat = _swa_load(q_ref, kc_ref, kp_ref, vc_ref, vp_ref, cc_ref[...], sc_ref[...], cp_ref[...], sp_ref[...])
        valid = _swa_valid(i)
        for kv in range(HKV):
            col = slice((kv // PAIR) * LANES, (kv // PAIR + 1) * LANES)
            kx = [t.astype(BF16) for t in _half_variants(kcat[:, col], kv % PAIR)]
            vx = [t.astype(BF16) for t in _half_variants(vcat[:, col], kv % PAIR)]
            for pp in range(grp // PAIR):
                p = kv * (grp // PAIR) + pp
                qs = slice(p * LANES, (p + 1) * LANES)
                out = jnp.zeros((CHUNK, LANES), F32)
                for half in range(PAIR):
                    pt, _ = _swa_probs_t(kx[half], qr[:, qs], sink_ref[PAIR * p + half], valid)
                    out = out + _dot(pt.astype(BF16), vx[half], "tn")
                o_ref[:, qs] = out.astype(BF16)

    return _hosted(
        body,
        name=name,
        sched=sched,
        grid=(T // CHUNK,),
        in_specs=_swa_specs(d),
        out_specs=pl.BlockSpec((CHUNK, AW), lambda i: (i, 0)),
        out_shape=jax.ShapeDtypeStruct((T, AW), BF16),
        compiler_params=_params(("arbitrary",), [((CHUNK, AW), F32)] * 3),
    )(sinks, P, P, P, P, P, cos2, sin2, cos2, sin2)


def _swa_bwd(d, P, dyb_b, Gb, cos2, sin2, sinks, dP, *, name, sched=None):
    T, D, AW, KVW, HQ, HKV = d.T, d.D, d.AW, d.KVW, d.HQ, d.HKV
    grp = HQ // HKV
    nb = T // CHUNK
    scale = HEAD_DIM**-0.5
    off_q = _offsets(d)[1]
    assert grp % PAIR == 0 and HKV % PAIR == 0 and off_q % AW == 0

    def body(sink_ref, q_ref, kc_ref, kp_ref, vc_ref, vp_ref, cc_ref, sc_ref, cp_ref, sp_ref, dyb_ref, gb_ref, dp_in,
             dq_ref, dkv_ref, dsk_ref, acc_s, dq_s, dk_s, dv_s):
        i = pl.program_id(0)
        cc, sc, cp, sp = cc_ref[...], sc_ref[...], cp_ref[...], sp_ref[...]
        datt = _dot(dyb_ref[...], gb_ref[...], "nn").astype(BF16)
        qr, kcat, vcat = _swa_load(q_ref, kc_ref, kp_ref, vc_ref, vp_ref, cc, sc, cp, sp)
        valid = _swa_valid(i)
        lane1 = lax.broadcasted_iota(jnp.int32, (1, LANES), 1)
        lane2 = lax.broadcasted_iota(jnp.int32, (2 * CHUNK, LANES), 1)
        dsk = jnp.zeros((1, LANES), F32)
        for kvp in range(HKV // PAIR):
            col = slice(kvp * LANES, (kvp + 1) * LANES)
            dk_col = jnp.zeros((2 * CHUNK, LANES), F32)
            dv_col = jnp.zeros((2 * CHUNK, LANES), F32)
            for odd in range(PAIR):
                kv = kvp * PAIR + odd
                kx = [t.astype(BF16) for t in _half_variants(kcat[:, col], odd)]
                vx = [t.astype(BF16) for t in _half_variants(vcat[:, col], odd)]
                dk_half = [jnp.zeros((2 * CHUNK, LANES), F32) for _ in range(PAIR)]
                dv_half = [jnp.zeros((2 * CHUNK, LANES), F32) for _ in range(PAIR)]
                for pp in range(grp // PAIR):
                    p = kv * (grp // PAIR) + pp
                    qs = slice(p * LANES, (p + 1) * LANES)
                    qp, dop = qr[:, qs], datt[:, qs]
                    dq = jnp.zeros((CHUNK, LANES), F32)
                    for half in range(PAIR):
                        h = PAIR * p + half
                        pt, p_s = _swa_probs_t(kx[half], qp, sink_ref[h], valid)
                        dpt = _dot(vx[half], dop, "nt")
                        rs = jnp.sum(pt * dpt, axis=0, keepdims=True)
                        dst = (pt * (dpt - rs) * scale).astype(BF16)
                        dsk = dsk + jnp.where(lane1 == h, -jnp.sum(p_s * rs, axis=1, keepdims=True), 0.0)
                        dq = dq + _dot(dst, kx[half], "tn")
                        dk_half[half] = dk_half[half] + _dot(dst, qp, "nn")
                        dv_half[half] = dv_half[half] + _dot(pt.astype(BF16), dop, "nn")
                    dq_s[:, qs] = dq
                for acc_half, is_k in ((dk_half, True), (dv_half, False)):
                    lo = jnp.where(lane2 < HEAD_DIM, acc_half[0], 0.0)
                    hi = jnp.where(lane2 >= HEAD_DIM, acc_half[1], 0.0)
                    both = (pltpu.roll(lo, HEAD_DIM, 1) + hi) if odd else (lo + pltpu.roll(hi, HEAD_DIM, 1))
                    if is_k:
                        dk_col = dk_col + both
                    else:
                        dv_col = dv_col + both
            dk_s[:, col] = dk_col
            dv_s[:, col] = dv_col
        dq_ref[...] = _rope_bwd(dq_s[...], cc, sc).astype(BF16)
        cur = pl.ds(pl.multiple_of(i * CHUNK, CHUNK), CHUNK)
        acc_s[cur, :KVW] = _rope_bwd(dk_s[CHUNK:, :], cc, sc)
        acc_s[cur, KVW:] = dv_s[CHUNK:, :]

        @pl.when(i > 0)
        def _():
            prv = pl.ds(pl.multiple_of((i - 1) * CHUNK, CHUNK), CHUNK)
            acc_s[prv, :KVW] += _rope_bwd(dk_s[:CHUNK, :], cp, sp)
            acc_s[prv, KVW:] += dv_s[:CHUNK, :]

        _acc_store(dsk_ref, dsk, i == 0)

        @pl.when(i == nb - 1)
        def _():
            dkv_ref[...] = acc_s[...].astype(BF16)

    return _hosted(
        body,
        name=name,
        sched=sched,
        grid=(nb,),
        in_specs=_swa_specs(d) + [pl.BlockSpec((CHUNK, D), lambda i: (i, 0)), pl.BlockSpec((D, AW), lambda i: (0, 0)),
                                   pl.BlockSpec(memory_space=pl.ANY)],
        out_specs=[
            pl.BlockSpec((CHUNK, AW), lambda i, o=off_q // AW: (i, o)),
            pl.BlockSpec((T, 2 * KVW), lambda i: (0, 0)),
            pl.BlockSpec((1, LANES), lambda i: (0, 0)),
        ],
        out_shape=[
            jax.ShapeDtypeStruct(dP.shape, BF16),
            jax.ShapeDtypeStruct((T, 2 * KVW), BF16),
            jax.ShapeDtypeStruct((1, LANES), F32),
        ],
        input_output_aliases={12: 0},
        scratch_shapes=[
            pltpu.VMEM((T, 2 * KVW), F32),
            pltpu.VMEM((CHUNK, AW), F32),
            pltpu.VMEM((2 * CHUNK, KVW), F32),
            pltpu.VMEM((2 * CHUNK, KVW), F32),
        ],
        compiler_params=_params(
            ("arbitrary",), [((CHUNK, AW), F32)] * 3 + [((D, AW), BF16), ((T, 2 * KVW), BF16)], _nbytes((T, 2 * KVW), F32)
        ),
    )(sinks, P, P, P, P, P, cos2, sin2, cos2, sin2, dyb_b, Gb, dP)


def _place_cols(dst, src, off, *, name, sched=None):
    T, w = src.shape
    tm, tw = _tile(T, 512), _tile(w, 512)
    assert off % tw == 0

    def body(src_ref, dst_in, out_ref):
        out_ref[...] = src_ref[...]

    return _hosted(
        body,
        name=name,
        sched=sched,
        grid=(T // tm, w // tw),
        in_specs=[pl.BlockSpec((tm, tw), lambda i, j: (i, j)), pl.BlockSpec(memory_space=pl.ANY)],
        out_specs=pl.BlockSpec((tm, tw), lambda i, j, o=off // tw: (i, j + o)),
        out_shape=jax.ShapeDtypeStruct(dst.shape, dst.dtype),
        input_output_aliases={1: 0},
        compiler_params=_params(("arbitrary", "arbitrary"), [((tm, tw), dst.dtype)] * 2),
    )(src, dst)


def _branch_merge(d, sgu, att, Ga, Gb, P, b_gate, *, name, sched=None):
    T, D, GW, AW = d.T, d.D, d.GW, d.AW
    _, _, _, _, off_ga, off_gb = _offsets(d)
    tm, tn = _tile(T, 1024), _tile(D, 512)
    assert off_ga % tn == 0 and off_gb % tn == 0

    def body(s_ref, a_ref, ga_w, gb_w, ga_ref, gb_ref, bga_ref, bgb_ref, ya_ref, yb_ref, mg_ref):
        ya = _dot(s_ref[...], ga_w[...], "nt")
        yb = _dot(a_ref[...], gb_w[...], "nt")
        ya_ref[...] = ya
        yb_ref[...] = yb
        sa = jax.nn.sigmoid(ga_ref[...] + bga_ref[...])
        sb = jax.nn.sigmoid(gb_ref[...] + bgb_ref[...])
        mg_ref[...] = (sa * ya + sb * yb).astype(BF16)

    tile = pl.BlockSpec((tm, tn), lambda j, i: (i, j))
    return _hosted(
        body,
        name=name,
        sched=sched,
        grid=(D // tn, T // tm),
        in_specs=[
            pl.BlockSpec((tm, GW), lambda j, i: (i, 0)),
            pl.BlockSpec((tm, AW), lambda j, i: (i, 0)),
            pl.BlockSpec((tn, GW), lambda j, i: (j, 0)),
            pl.BlockSpec((tn, AW), lambda j, i: (j, 0)),
            pl.BlockSpec((tm, tn), lambda j, i, o=off_ga // tn: (i, j + o)),
            pl.BlockSpec((tm, tn), lambda j, i, o=off_gb // tn: (i, j + o)),
            pl.BlockSpec((1, tn), lambda j, i: (0, j)),
            pl.BlockSpec((1, tn), lambda j, i, o=D // tn: (0, j + o)),
        ],
        out_specs=[tile, tile, tile],
        out_shape=[jax.ShapeDtypeStruct((T, D), F32), jax.ShapeDtypeStruct((T, D), F32), jax.ShapeDtypeStruct((T, D), BF16)],
        compiler_params=_params(
            ("arbitrary", "arbitrary"),
            [((tm, GW), BF16), ((tm, AW), BF16), ((tn, GW), BF16), ((tn, AW), BF16)] + [((tm, tn), F32)] * 5,
        ),
    )(sgu, att, Ga, Gb, P, P, b_gate, b_gate)


def _xattn_probs(qh, kh):
    s = _dot(qh, kh, "nt") * (X_HEAD_DIM**-0.5)
    e = jnp.exp(s - jnp.max(s, axis=1, keepdims=True))
    return e / jnp.sum(e, axis=1, keepdims=True)


def _xattn_fwd(d, X, Xb, kv, Gxq, Gxo, ln_g, ln_b, *, name, sched=None):
    T, D, XW, XH, MEM = d.T, d.D, d.XW, d.XH, d.MEM
    tm = _tile(T, 256)

    def body(x_ref, xb_ref, kv_ref, wq_ref, wo_ref, g_ref, b_ref, y_ref, yb_ref, xh_ref, rs_ref, q_ref, o_ref, o_s):
        q = _dot(xb_ref[...], wq_ref[...], "nn").astype(BF16)
        q_ref[...] = q
        for h in range(XH):
            hs = slice(h * X_HEAD_DIM, (h + 1) * X_HEAD_DIM)
            p = _xattn_probs(q[:, hs], kv_ref[:, hs]).astype(BF16)
            o_s[:, hs] = _dot(p, kv_ref[:, XW + h * X_HEAD_DIM : XW + (h + 1) * X_HEAD_DIM], "nn").astype(BF16)
        o = o_s[...]
        o_ref[...] = o
        y, xhat, rstd = _ln_fwd(ALPHA * x_ref[...] + _dot(o, wo_ref[...], "nt"), g_ref[...], b_ref[...])
        y_ref[...] = y
        yb_ref[...] = y.astype(BF16)
        xh_ref[...] = xhat
        rs_ref[...] = rstd

    row = pl.BlockSpec((tm, D), lambda i: (i, 0))
    nar = pl.BlockSpec((tm, XW), lambda i: (i, 0))
    vec = pl.BlockSpec((1, D), lambda i: (0, 0))
    return _hosted(
        body,
        name=name,
        sched=sched,
        grid=(T // tm,),
        in_specs=[
            row,
            row,
            pl.BlockSpec((MEM, 2 * XW), lambda i: (0, 0)),
            pl.BlockSpec((D, XW), lambda i: (0, 0)),
            pl.BlockSpec((D, XW), lambda i: (0, 0)),
            vec,
            vec,
        ],
        out_specs=[row, row, row, pl.BlockSpec((tm, 1), lambda i: (i, 0)), nar, nar],
        out_shape=[
            jax.ShapeDtypeStruct((T, D), F32),
            jax.ShapeDtypeStruct((T, D), BF16),
            jax.ShapeDtypeStruct((T, D), F32),
            jax.ShapeDtypeStruct((T, 1), F32),
            jax.ShapeDtypeStruct((T, XW), BF16),
            jax.ShapeDtypeStruct((T, XW), BF16),
        ],
        scratch_shapes=[pltpu.VMEM((tm, XW), BF16)],
        compiler_params=_params(("arbitrary",), [((tm, D), F32)] * 4 + [((D, XW), BF16)] * 2),
    )(X, Xb, kv, Gxq, Gxo, ln_g, ln_b)


def _xattn_bwd(d, dz, dzb, q_b, kv, Gxq, Gxo, xhat, rstd, ln_g, *, name, sched=None):
    T, D, XW, XH, MEM = d.T, d.D, d.XW, d.XH, d.MEM
    tm = _tile(T, 256)
    scale = X_HEAD_DIM**-0.5

    def body(dz_ref, dzb_ref, q_ref, kv_ref, wq_ref, wo_ref, xh_ref, rs_ref, g_ref, dx_ref, dxb_ref, dg_ref, db_ref, dq_ref, dkv_ref, dq_s):
        first = pl.program_id(0) == 0
        do = _dot(dzb_ref[...], wo_ref[...], "nn").astype(BF16)
        for h in range(XH):
            hs = slice(h * X_HEAD_DIM, (h + 1) * X_HEAD_DIM)
            vs = slice(XW + h * X_HEAD_DIM, XW + (h + 1) * X_HEAD_DIM)
            kh, vh, qh, doh = kv_ref[:, hs], kv_ref[:, vs], q_ref[:, hs], do[:, hs]
            p = _xattn_probs(qh, kh)
            dp = _dot(doh, vh, "nt")
            ds = (p * (dp - jnp.sum(p * dp, axis=1, keepdims=True)) * scale).astype(BF16)
            dq_s[:, hs] = _dot(ds, kh, "nn").astype(BF16)
            _acc_store(dkv_ref.at[h], _dot(ds, qh, "tn"), first)
            _acc_store(dkv_ref.at[XH + h], _dot(p.astype(BF16), doh, "tn"), first)
        dq = dq_s[...]
        dq_ref[...] = dq
        dy = ALPHA * dz_ref[...] + _dot(dq, wq_ref[...], "nt")
        xh = xh_ref[...]
        dz_in = _ln_bwd_vals(dy, xh, rs_ref[...], g_ref[...])
        dx_ref[...] = dz_in
        dxb_ref[...] = dz_in.astype(BF16)
        _acc_store(dg_ref, jnp.sum(dy * xh, axis=0, keepdims=True), first)
        _acc_store(db_ref, jnp.sum(dy, axis=0, keepdims=True), first)

    row = pl.BlockSpec((tm, D), lambda i: (i, 0))
    nar = pl.BlockSpec((tm, XW), lambda i: (i, 0))
    vec = pl.BlockSpec((1, D), lambda i: (0, 0))
    return _hosted(
        body,
        name=name,
        sched=sched,
        grid=(T // tm,),
        in_specs=[
            row,
            row,
            nar,
            pl.BlockSpec((MEM, 2 * XW), lambda i: (0, 0)),
            pl.BlockSpec((D, XW), lambda i: (0, 0)),
            pl.BlockSpec((D, XW), lambda i: (0, 0)),
            row,
            pl.BlockSpec((tm, 1), lambda i: (i, 0)),
            vec,
        ],
        out_specs=[row, row, vec, vec, nar, pl.BlockSpec((2 * XH, MEM, X_HEAD_DIM), lambda i: (0, 0, 0))],
        out_shape=[
            jax.ShapeDtypeStruct((T, D), F32),
            jax.ShapeDtypeStruct((T, D), BF16),
            jax.ShapeDtypeStruct((1, D), F32),
            jax.ShapeDtypeStruct((1, D), F32),
            jax.ShapeDtypeStruct((T, XW), BF16),
            jax.ShapeDtypeStruct((2 * XH, MEM, X_HEAD_DIM), F32),
        ],
        scratch_shapes=[pltpu.VMEM((tm, XW), BF16)],
        compiler_params=_params(("arbitrary",), [((tm, D), F32)] * 5 + [((D, XW), BF16)] * 2),
    )(dz, dzb, q_b, kv, Gxq, Gxo, xhat, rstd, ln_g)


def _resid_ln_outs():
    return (("mn", F32), ("mn", BF16), ("mn", F32), ("m1", F32))


def _layer_fwd(d, X, Xb, memb, G, S, cos2, sin2, tag, sched=None):
    D = d.D
    mm = functools.partial(_matmul, sched=sched)
    P = mm(Xb, G["in"], "nt", name=f"proj_{tag}", tm=1024, tn=_tile(d.IN, 1280))
    sgu = _gmlp_fwd(d, P, S["ln_v_g"], S["ln_v_b"], S["w_s"], S["b_sT"], name=f"gmlp_fwd_{tag}", sched=sched)
    att = _swa_fwd(d, P, cos2, sin2, S["sinks"], name=f"swa_fwd_{tag}", sched=sched)
    ya, yb, merged = _branch_merge(d, sgu, att, G["a"], G["b"], P, S["b_gate"], name=f"merge_{tag}", sched=sched)
    X1, X1b, xh1, rs1 = mm(
        merged, G["o"], "nn", name=f"oproj_ln_{tag}", tn=D, tk=1024, out_defs=_resid_ln_outs(), epilogue=_ep_resid_ln,
        extras=((X, "mn", 0), (S["ln1_g"], "row", 0), (S["ln1_b"], "row", 0)),
    )
    kv = mm(memb, G["xkv"], "nn", name=f"xkv_{tag}", out_defs=(("mn", BF16),))
    X2, X2b, xh2, rs2, q_b, o_b = _xattn_fwd(d, X1, X1b, kv, G["xq"], G["xo"], S["ln2_g"], S["ln2_b"], name=f"xattn_fwd_{tag}", sched=sched)
    H, A = mm(X2b, G["up"], "nt", name=f"up_{tag}", tm=1024, out_defs=(("mn", F32), ("mn", BF16)), epilogue=_ep_relu2)
    X3, X3b, xh3, rs3 = mm(
        A, G["down"], "nn", name=f"down_ln_{tag}", tn=D, tk=1024, out_defs=_resid_ln_outs(), epilogue=_ep_resid_ln,
        extras=((X2, "mn", 0), (S["ln3_g"], "row", 0), (S["ln3_b"], "row", 0)),
    )
    saved = dict(Xb=Xb, P=P, sgu=sgu, att=att, ya=ya, yb=yb, merged=merged, X1b=X1b, xh1=xh1, rs1=rs1, kv=kv, q_b=q_b,
                 o_b=o_b, X2b=X2b, xh2=xh2, rs2=rs2, H=H, A=A, xh3=xh3, rs3=rs3)
    return X3, X3b, saved


def _layer_bwd(d, dXout, sv, memb, G, S, cos2, sin2, tag, sched=None, on_grad=None, target=None):
    D = d.D
    mm = functools.partial(_matmul, sched=sched)
    emit = on_grad if on_grad is not None else (lambda n, g: None)
    wide = dict(tn=D)
    _, _, off_k, _, off_ga, off_gb = _offsets(d)
    bf = (("mn", BF16),)
    dz3, dz3b, dg3, db3, *loss = _ln_bwd(dXout, sv["xh3"], sv["rs3"], S["ln3_g"], name=f"ln3_bwd_{tag}", sched=sched, target=target)
    dHb = mm(dz3b, G["down"], "nt", name=f"down_dx_{tag}", tm=1024, out_defs=bf, epilogue=_ep_relu2_bwd, extras=((sv["H"], "mn", 0),))
    g_down = mm(sv["A"], dz3b, "tn", name=f"down_dw_{tag}", **wide)
    emit("down", g_down)
    ln_bwd_outs = (("mn", F32), ("mn", BF16), ("acc", F32), ("acc", F32))
    dz2, dz2b, dg2, db2 = mm(
        dHb, G["up"], "nn", name=f"up_dx_{tag}", tn=D, tk=1024, out_defs=ln_bwd_outs, epilogue=_ep_resid_ln_bwd,
        extras=((dz3, "mn", 0), (sv["xh2"], "mn", 0), (sv["rs2"], "m1", 0), (S["ln2_g"], "row", 0)),
    )
    g_up = mm(dHb, sv["X2b"], "tn", name=f"up_dw_{tag}", **wide)
    emit("up", g_up)
    dz1, dz1b, dg1, db1, dq_b, dkv = _xattn_bwd(d, dz2, dz2b, sv["q_b"], sv["kv"], G["xq"], G["xo"], sv["xh1"], sv["rs1"], S["ln1_g"],
                                               name=f"xattn_bwd_{tag}", sched=sched)
    g_xo = mm(dz2b, sv["o_b"], "tn", name=f"xo_dw_{tag}")
    emit("xo", g_xo)
    g_xq = mm(sv["X1b"], dq_b, "tn", name=f"xq_dw_{tag}")
    emit("xq", g_xq)
    dkv_b = dkv.transpose(1, 0, 2).reshape(d.MEM, 2 * d.XW).astype(BF16)
    g_xkv = mm(memb, dkv_b, "tn", name=f"xkv_dw_{tag}")
    emit("xkv", g_xkv)
    dya_b, dyb_b, dP, dgb_b, dbga, dbgb = mm(
        dz1b, G["o"], "nt", name=f"oproj_dx_{tag}", tm=1024, tn=512,
        out_defs=(("mn", BF16), ("mn", BF16), ("cols", BF16, d.IN, off_ga), ("mn", BF16), ("acc", F32), ("acc", F32)),
        epilogue=_ep_gate_bwd,
        extras=((sv["P"], "mn", off_ga), (sv["P"], "mn", off_gb), (sv["ya"], "mn", 0), (sv["yb"], "mn", 0),
                (S["b_gate"], "row", 0), (S["b_gate"], "row", D)),
    )
    g_o = mm(sv["merged"], dz1b, "tn", name=f"oproj_dw_{tag}", **wide)
    emit("o", g_o)
    dP = _place_cols(dP, dgb_b, off_gb, name=f"place_dgb_{tag}")
    dP, dwm, dbsT, dlvg, dlvb = _gmlp_bwd(d, sv["P"], dya_b, G["a"], S["ln_v_g"], S["ln_v_b"], S["w_s"], S["b_sT"], dP, name=f"gmlp_bwd_{tag}", sched=sched)
    g_a = mm(dya_b, sv["sgu"], "tn", name=f"bra_dw_{tag}")
    emit("a", g_a)
    dP, dkvr_b, dsk = _swa_bwd(d, sv["P"], dyb_b, G["b"], cos2, sin2, S["sinks"], dP, name=f"swa_bwd_{tag}", sched=sched)
    g_b = mm(dyb_b, sv["att"], "tn", name=f"brb_dw_{tag}")
    emit("b", g_b)
    dP = _place_cols(dP, dkvr_b, off_k, name=f"place_dkv_{tag}")
    small = dict(b_gate=jnp.concatenate([dbga, dbgb], axis=1), ln_v_g=dlvg, ln_v_b=dlvb, w_s=dwm, b_sT=dbsT, sinks=dsk,
                 ln1_g=dg1, ln1_b=db1, ln2_g=dg2, ln2_b=db2, ln3_g=dg3, ln3_b=db3)
    emit("small", small)
    g_in = mm(dP, sv["Xb"], "tn", name=f"proj_dw_{tag}", **wide)
    emit("in", g_in)
    dXin = mm(dP, G["in"], "nn", name=f"proj_dx_{tag}", tn=D, tk=_tile(d.IN, 1920), epilogue=_ep_resid, extras=((dz1, "mn", 0),))
    big = {"in": g_in, "a": g_a, "b": g_b, "o": g_o, "xq": g_xq, "xkv": g_xkv, "xo": g_xo, "up": g_up, "down": g_down}
    return dXin, big, small, (loss[0] if loss else None)


def _rope_tables(T):
    inv = 1.0 / (ROPE_THETA ** (jnp.arange(0, HEAD_DIM, 2, dtype=F32) / HEAD_DIM))
    ang = jnp.arange(T, dtype=F32)[:, None] * inv[None, :]
    cos, sin = jnp.cos(ang), jnp.sin(ang)
    reps = LANES // HEAD_DIM
    return jnp.concatenate([cos, cos] * reps, axis=1), jnp.concatenate([-sin, sin] * reps, axis=1)


def _local_step(d, x, mem, target, Gs, Ss, sched=None, on_grad=None):
    cos2, sin2 = _rope_tables(d.T)
    memb = mem.astype(BF16)
    X, Xb = x, x.astype(BF16)
    saved = []
    for l in range(DEPTH):
        X, Xb, sv = _layer_fwd(d, X, Xb, memb, Gs[l], Ss[l], cos2, sin2, f"l{l}", sched)
        saved.append(sv)
    bigs, smalls = [None] * DEPTH, [None] * DEPTH
    dX, loss = X, None
    for l in reversed(range(DEPTH)):
        emit = None if on_grad is None else functools.partial(on_grad, l)
        head = target if l == DEPTH - 1 else None
        dX, bigs[l], smalls[l], got = _layer_bwd(d, dX, saved[l], memb, Gs[l], Ss[l], cos2, sin2, f"l{l}", sched, emit, head)
        loss = got if got is not None else loss
    return loss, dX, bigs, smalls


def _place():
    x, y, c = lax.axis_index("x"), lax.axis_index("y"), lax.axis_index("c")
    return x, y, c, [(1 - x, y), (x, 1 - y), (1 - x, 1 - y)]


def _all_gather(shards, *, name, sched=None):
    n = len(shards)
    any_spec = pl.BlockSpec(memory_space=pl.ANY)

    def body(*refs):
        ins, outs = refs[:n], refs[n : 2 * n]
        send_sems, recv_sems, local_sems = refs[2 * n :]
        x, y, c, chips = _place()
        me, sibling = (x, y, c), (x, y, 1 - c)

        def rows(w, p):
            r = ins[w].shape[0]
            return outs[w].at[pl.ds((4 * p[0] + 2 * p[1] + p[2]) * r, r), :]

        def copy(w, k, block, to, src=None):
            return pltpu.make_async_remote_copy(
                src_ref=rows(w, block) if src is None else src, dst_ref=rows(w, block),
                send_sem=send_sems.at[7 * w + k], recv_sem=recv_sems.at[7 * w + k], device_id=to, device_id_type=MESH)

        mine = [pltpu.make_async_copy(ins[w], rows(w, me), local_sems.at[w]) for w in range(n)]
        for cp in mine:
            cp.start()
        first = []
        for w in range(n):
            first.append(copy(w, 0, me, sibling, src=ins[w]))
            first += [copy(w, 1 + j, me, (*chip, c), src=ins[w]) for j, chip in enumerate(chips)]
        for cp in first:
            cp.start()
        passed = []
        for j, chip in enumerate(chips):
            for w in range(n):
                copy(w, 1 + j, (*chip, c), me).wait_recv()
                fwd = copy(w, 4 + j, (*chip, c), sibling)
                fwd.start()
                passed.append(fwd)
        for w in range(n):
            copy(w, 0, sibling, me).wait_recv()
            for j, chip in enumerate(chips):
                copy(w, 4 + j, (*chip, 1 - c), me).wait_recv()
        for cp in first + passed:
            cp.wait_send()
        for cp in mine:
            cp.wait()

    return _hosted(
        body,
        name=name,
        sched=sched,
        in_specs=[any_spec] * n,
        out_specs=[any_spec] * n,
        out_shape=[jax.ShapeDtypeStruct((N_DEV * s.shape[0], s.shape[1]), s.dtype) for s in shards],
        scratch_shapes=[pltpu.SemaphoreType.DMA((7 * n,)), pltpu.SemaphoreType.DMA((7 * n,)), pltpu.SemaphoreType.DMA((n,))],
    )(*shards)


class _Task:
    def __init__(self, ins, out_shapes, n_remote, n_local, plan, done, aliases=None):
        self.ins, self.out_shapes, self.n_remote, self.n_local = list(ins), list(out_shapes), n_remote, n_local
        self.plan, self.done, self.aliases = plan, done, dict(aliases or {})


class _Sched:
    def __init__(self):
        self.pending = {}

    def at(self, host, task):
        self.pending.setdefault(host, []).append(task)

    def take(self, host):
        return self.pending.pop(host, [])


def _in_hbm(out_shape):
    if isinstance(out_shape, (list, tuple)):
        return [_in_hbm(s) for s in out_shape]
    return pltpu.HBM(out_shape.shape, out_shape.dtype)


def _hosted(body, *, name, sched=None, tasks=(), grid=(), in_specs=None, out_specs=None, out_shape=(), scratch_shapes=(),
            compiler_params=None, input_output_aliases=None):
    tasks = list(tasks) + (sched.take(name) if sched is not None else [])
    scratch_shapes = list(scratch_shapes)
    kwargs = dict(name=name)
    core_aliases = dict(input_output_aliases or {})
    if grid:
        kwargs["grid"] = grid
    if compiler_params is not None:
        kwargs["compiler_params"] = compiler_params
    if not tasks:
        if in_specs is not None:
            kwargs.update(in_specs=in_specs, out_specs=out_specs)
        return pl.pallas_call(body, out_shape=_in_hbm(out_shape), scratch_shapes=scratch_shapes, input_output_aliases=core_aliases, **kwargs)
    assert in_specs is not None and out_specs is not None, name
    single = not isinstance(out_shape, (list, tuple))
    core_shape = [out_shape] if single else list(out_shape)
    core_specs = [out_specs] if single else list(out_specs)
    in_specs = list(in_specs)
    n_in, n_out, n_scr = len(in_specs), len(core_shape), len(scratch_shapes)
    t_ins = [a for t in tasks for a in t.ins]
    t_outs = [s for t in tasks for s in t.out_shapes]
    n_rem = sum(t.n_remote for t in tasks)
    n_loc = sum(t.n_local for t in tasks)
    aliases, pos_in, pos_out = dict(core_aliases), n_in, n_out
    for t in tasks:
        for a, b in t.aliases.items():
            aliases[pos_in + a] = pos_out + b
        pos_in += len(t.ins)
        pos_out += len(t.out_shapes)
    any_spec = pl.BlockSpec(memory_space=pl.ANY)

    def wrapped(*refs):
        core_in, t_in = refs[:n_in], refs[n_in : n_in + len(t_ins)]
        o0 = n_in + len(t_ins)
        core_out, t_out = refs[o0 : o0 + n_out], refs[o0 + n_out : o0 + n_out + len(t_outs)]
        s0 = o0 + n_out + len(t_outs)
        core_scr = refs[s0 : s0 + n_scr]
        send_sems, recv_sems, local_sems = refs[s0 + n_scr :]
        remote, local = [], []
        pi = po = pr = pq = 0
        for t in tasks:
            r, q = t.plan(t_in[pi : pi + len(t.ins)], t_out[po : po + len(t.out_shapes)], send_sems, recv_sems, local_sems, pr, pq)
            assert len(r) == t.n_remote and len(q) == t.n_local
            remote += r
            local += q
            pi, po, pr, pq = pi + len(t.ins), po + len(t.out_shapes), pr + t.n_remote, pq + t.n_local

        def start():
            for cp in local + remote:
                cp.start()

        def finish():
            for cp in remote + local:
                cp.wait()

        if grid:
            ids = [pl.program_id(a) for a in range(len(grid))]
            first = functools.reduce(jnp.logical_and, [i == 0 for i in ids])
            last = functools.reduce(jnp.logical_and, [i == g - 1 for i, g in zip(ids, grid)])
            pl.when(first)(start)
            body(*core_in, *core_out, *core_scr)
            pl.when(last)(finish)
        else:
            start()
            body(*core_in, *core_out, *core_scr)
            finish()

    call = pl.pallas_call(
        wrapped,
        in_specs=in_specs + [any_spec] * len(t_ins),
        out_specs=core_specs + [any_spec] * len(t_outs),
        out_shape=_in_hbm(core_shape + t_outs),
        scratch_shapes=scratch_shapes
        + [pltpu.SemaphoreType.DMA((n_rem,)), pltpu.SemaphoreType.DMA((n_rem,)), pltpu.SemaphoreType.DMA((max(n_loc, 1),))],
        input_output_aliases=aliases,
        **kwargs,
    )

    def run(*operands):
        outs = call(*operands, *t_ins)
        po = n_out
        for t in tasks:
            t.done(list(outs[po : po + len(t.out_shapes)]))
            po += len(t.out_shapes)
        return outs[0] if single else list(outs[:n_out])

    return run


def _comm_only(tasks, *, name):
    _hosted(lambda: None, name=name, tasks=tasks, in_specs=[], out_shape=[], out_specs=[])()


def _rows(ref, block, n_blocks):
    r = ref.shape[0] // n_blocks
    return ref.at[pl.ds(block * r, r), :]


def _remote(src, dst, send_sems, recv_sems, k, to):
    return pltpu.make_async_remote_copy(src_ref=src, dst_ref=dst, send_sem=send_sems.at[k], recv_sem=recv_sems.at[k],
                                        device_id=to, device_id_type=MESH)


def _gather_stage1(shards, done, part=(0, 1), into=None):
    n = len(shards)
    k, n_parts = part

    def plan(ins, outs, send_sems, recv_sems, local_sems, pr, pq):
        x, y, c, chips = _place()
        mine = 4 * x + 2 * y + c
        remote, local = [], []
        for w in range(n):
            r = shards[w].shape[0]
            rp = r // n_parts
            src = ins[w].at[pl.ds(k * rp, rp), :]
            dst = outs[w].at[pl.ds(mine * r + k * rp, rp), :]
            local.append(pltpu.make_async_copy(src, dst, local_sems.at[pq + w]))
            for j, to in enumerate([(x, y, 1 - c)] + [(*chip, c) for chip in chips]):
                remote.append(_remote(src, dst, send_sems, recv_sems, pr + 4 * w + j, to))
        return remote, local

    shapes = [jax.ShapeDtypeStruct((N_DEV * s.shape[0], s.shape[1]), s.dtype) for s in shards]
    if into is None:
        return _Task(shards, shapes, 4 * n, n, plan, done)
    return _Task(list(shards) + list(into), shapes, 4 * n, n, plan, done, aliases={n + w: w for w in range(n)})


def _gather_stage2(partial, done):
    n = len(partial)

    def plan(ins, outs, send_sems, recv_sems, local_sems, pr, pq):
        x, y, c, chips = _place()
        remote = []
        for w in range(n):
            for j, chip in enumerate(chips):
                rows = _rows(outs[w], 4 * chip[0] + 2 * chip[1] + c, N_DEV)
                remote.append(_remote(rows, rows, send_sems, recv_sems, pr + 3 * w + j, (x, y, 1 - c)))
        return remote, []

    shapes = [jax.ShapeDtypeStruct(p.shape, p.dtype) for p in partial]
    return _Task(partial, shapes, 3 * n, 0, plan, done, aliases={w: w for w in range(n)})


def _sibling_stage(grads, done):
    n = len(grads)

    def plan(ins, outs, send_sems, recv_sems, local_sems, pr, pq):
        x, y, c, _ = _place()
        remote = []
        for w in range(n):
            for k in range(4):
                src = _rows(ins[w], 4 * (k // 2) + 2 * (k % 2) + (1 - c), N_DEV)
                remote.append(_remote(src, _rows(outs[w], k, 4), send_sems, recv_sems, pr + 4 * w + k, (x, y, 1 - c)))
        return remote, []

    shapes = [jax.ShapeDtypeStruct((g.shape[0] // 2, g.shape[1]), g.dtype) for g in grads]
    return _Task(grads, shapes, 4 * n, 0, plan, done)


def _chips_stage(sends, done, part=(0, 1), into=None):
    n = len(sends)
    k, n_parts = part

    def plan(ins, outs, send_sems, recv_sems, local_sems, pr, pq):
        x, y, c, chips = _place()
        remote = []
        for w in range(n):
            r = sends[w].shape[0] // 3
            rp = r // n_parts
            for j, chip in enumerate(chips):
                rows = pl.ds(j * r + k * rp, rp)
                remote.append(_remote(ins[w].at[rows, :], outs[w].at[rows, :], send_sems, recv_sems, pr + 3 * w + j, (*chip, c)))
        return remote, []

    shapes = [jax.ShapeDtypeStruct(s.shape, s.dtype) for s in sends]
    if into is None:
        return _Task(sends, shapes, 3 * n, 0, plan, done)
    return _Task(list(sends) + list(into), shapes, 3 * n, 0, plan, done, aliases={n + w: w for w in range(n)})


def _pair_sum(grad, got, place, *, name):
    R, C = grad.shape
    r = R // N_DEV
    tr = _tile(r, 256, 16)
    nt = r // tr

    def chip_of(s, pr):
        fx = jnp.where((s == 1) | (s == 3), 1, 0)
        fy = jnp.where(s >= 2, 1, 0)
        return pr[0] ^ fx, pr[1] ^ fy

    def grad_map(s, t, pr):
        kx, ky = chip_of(s, pr)
        return ((4 * kx + 2 * ky + pr[2]) * nt + t, 0)

    def got_map(s, t, pr):
        kx, ky = chip_of(s, pr)
        return ((2 * kx + ky) * nt + t, 0)

    def body(pr, g_ref, r_ref, own_ref, send_ref):
        s = pl.program_id(0)
        tot = g_ref[...] + r_ref[...]

        @pl.when(s == 0)
        def _():
            own_ref[...] = tot

        @pl.when(s > 0)
        def _():
            send_ref[...] = tot.astype(BF16)

    return pl.pallas_call(
        body,
        name=name,
        grid_spec=pltpu.PrefetchScalarGridSpec(
            num_scalar_prefetch=1,
            grid=(4, nt),
            in_specs=[pl.BlockSpec((tr, C), grad_map), pl.BlockSpec((tr, C), got_map)],
            out_specs=[
                pl.BlockSpec((tr, C), lambda s, t, pr: (jnp.where(s == 0, t, nt - 1), 0)),
                pl.BlockSpec((tr, C), lambda s, t, pr: (jnp.where(s == 0, 0, (s - 1) * nt + t), 0)),
            ],
        ),
        out_shape=[jax.ShapeDtypeStruct((r, C), F32), jax.ShapeDtypeStruct((3 * r, C), BF16)],
        compiler_params=_params(("arbitrary", "arbitrary"), [((tr, C), F32)] * 4),
    )(place, grad, got)


def _adam_vals(w, g, m, v):
    m = ADAM_B1 * m + (1.0 - ADAM_B1) * g
    v = ADAM_B2 * v + (1.0 - ADAM_B2) * (g * g)
    m_hat = m / (1.0 - ADAM_B1**ADAM_STEP)
    v_hat = v / (1.0 - ADAM_B2**ADAM_STEP)
    delta = -ADAM_LR * (m_hat / (jnp.sqrt(v_hat) + ADAM_EPS) + ADAM_WD * w)
    return delta, m, v


def _adam_big(w3, m3, v3, own, got, layer, transposed, prev, *, name, sched=None):
    _, A, B = w3.shape
    ta = _tile(A, 256, 16)
    nt = A // ta
    b_pad = (-B) % LANES

    def body(*refs):
        w_ref, m_ref, v_ref, own_ref, g1_ref, g2_ref, g3_ref = refs[:7]
        g_ref, d_ref, nm_ref, nv_ref = refs[-4:]
        g = ((own_ref[...] + g1_ref[...].astype(F32)) + g2_ref[...].astype(F32)) + g3_ref[...].astype(F32)
        if transposed:
            if b_pad:
                g = jnp.concatenate([g, jnp.zeros((b_pad, ta), F32)], axis=0)
            g = g.T[:, :B]
        g_ref[...] = g
        d_ref[...], nm_ref[...], nv_ref[...] = _adam_vals(w_ref[...], g, m_ref[...], v_ref[...])

    nat = pl.BlockSpec((None, ta, B), lambda t: (layer, t, 0))
    if transposed:
        parts = [pl.BlockSpec((B, ta), lambda t: (0, t))] + [pl.BlockSpec((B, ta), lambda t, j=j: (j, t)) for j in range(3)]
    else:
        parts = [pl.BlockSpec((ta, B), lambda t: (t, 0))] + [pl.BlockSpec((ta, B), lambda t, j=j: (j * nt + t, 0)) for j in range(3)]
    keep = [] if prev is None else list(prev)
    outs = _hosted(
        body,
        name=name,
        sched=sched,
        grid=(nt,),
        in_specs=[nat, nat, nat] + parts + [pl.BlockSpec(memory_space=pl.ANY)] * len(keep),
        out_specs=[nat] * 4,
        out_shape=[jax.ShapeDtypeStruct(w3.shape, F32)] * 4,
        input_output_aliases={7 + k: k for k in range(len(keep))},
        compiler_params=_params(("arbitrary",), [((ta, B), F32)] * 10),
    )(w3, m3, v3, own, got, got, got, *keep)
    return list(outs)


def _adam_small(w, m, v, parts, *, name, sched=None):
    R = w.shape[0]

    def body(w_ref, m_ref, v_ref, p_ref, g_ref, d_ref, nm_ref, nv_ref):
        g = p_ref[pl.ds(0, R), :]
        for k in range(1, N_DEV):
            g = g + p_ref[pl.ds(k * R, R), :]
        g_ref[...] = g
        d_ref[...], nm_ref[...], nv_ref[...] = _adam_vals(w_ref[...], g, m_ref[...], v_ref[...])

    return _hosted(
        body,
        name=name,
        sched=sched,
        out_shape=[jax.ShapeDtypeStruct((R, LANES), F32)] * 4,
        compiler_params=_params(None, [((R, LANES), F32)] * 16),
    )(w, m, v, parts)


SMALL_NAMES = ("b_gate", "ln_v_g", "ln_v_b", "w_s", "b_s", "sinks", "ln1_g", "ln1_b", "ln2_g", "ln2_b", "ln3_g", "ln3_b")
PACK_ALIGN = 8 * LANES


def _pack(arrays):
    flat = []
    for a in arrays:
        a = a.reshape(-1)
        flat.append(jnp.pad(a, (0, (-a.shape[0]) % PACK_ALIGN)))
    return jnp.concatenate(flat).reshape(-1, LANES)


def _unpack(packed, shapes):
    flat = packed.reshape(-1)
    out, pos = [], 0
    for s in shapes:
        n = 1
        for e in s:
            n *= e
        out.append(flat[pos : pos + n].reshape(s))
        pos += n + (-n) % PACK_ALIGN
    return out


BIG = (("in", "w_in", True), ("a", "w_br_a", True), ("b", "w_br_b", True), ("o", "w_o", False), ("xq", "w_xq", False),
       ("xkv", "w_xkv", False), ("xo", "w_xo", True), ("up", "w_up", True), ("down", "w_down", False))


SMALL_BIG = ("a", "b", "o", "xq", "xkv", "xo")
GATHER_PLAN = (
    (0, ("in",), None, None),
    (0, SMALL_BIG, ("proj_l0",), "gmlp_fwd_l0"),
    (0, ("up",), ("swa_fwd_l0", "merge_l0"), "oproj_ln_l0"),
    (0, ("down",), ("oproj_ln_l0", "xattn_fwd_l0"), "up_l0"),
    (1, ("in",), ("up_l0",), "down_ln_l0"),
    (1, SMALL_BIG, ("down_ln_l0",), "proj_l1"),
    (1, ("up",), ("down_ln_l0", "proj_l1"), "gmlp_fwd_l1"),
    (1, ("down",), ("swa_fwd_l1", "merge_l1"), "oproj_ln_l1"),
)
EARLY_SMALL_BIG = ("o", "xq", "xkv", "xo")
LATE_SMALL_BIG = ("a", "b")
GRAD_HOSTS = {
    1: {"down": ("xattn_bwd_l1", "swa_bwd_l1"), "up": ("xattn_bwd_l1", "proj_dw_l1"),
        **{g: ("gmlp_bwd_l1", "proj_dx_l1") for g in EARLY_SMALL_BIG}, **{g: ("proj_dw_l1", "proj_dx_l1") for g in LATE_SMALL_BIG},
        "in": ("proj_dx_l1", "down_dx_l0")},
    0: {"down": ("xattn_bwd_l0", ("oproj_dx_l0", "gmlp_bwd_l0")), "up": ("xattn_bwd_l0", "swa_bwd_l0"),
        **{g: ("gmlp_bwd_l0", "proj_dw_l0") for g in EARLY_SMALL_BIG}, **{g: ("proj_dw_l0", "proj_dx_l0") for g in LATE_SMALL_BIG},
        "in": (None, "proj_dx_l0")},
}
SMALL_GRAD_HOSTS = ("proj_dw_l0", "proj_dx_l0")


def kernel(x, mem, w_in, b_gate, ln_v_g, ln_v_b, w_s, b_s, sinks, w_br_a, w_br_b, w_o, ln1_g, ln1_b, w_xq, w_xkv, w_xo, ln2_g, ln2_b, w_up, w_down, ln3_g, ln3_b, loss_target, m_w_in, m_b_gate, m_ln_v_g, m_ln_v_b, m_w_s, m_b_s, m_sinks, m_w_br_a, m_w_br_b, m_w_o, m_ln1_g, m_ln1_b, m_w_xq, m_w_xkv, m_w_xo, m_ln2_g, m_ln2_b, m_w_up, m_w_down, m_ln3_g, m_ln3_b, v_w_in, v_b_gate, v_ln_v_g, v_ln_v_b, v_w_s, v_b_s, v_sinks, v_w_br_a, v_w_br_b, v_w_o, v_ln1_g, v_ln1_b, v_w_xq, v_w_xkv, v_w_xo, v_ln2_g, v_ln2_b, v_w_up, v_w_down, v_ln3_g, v_ln3_b):
    given = dict(locals())
    T, D = x.shape[1], x.shape[2]
    IN, GW, AW, XW, DFF = w_in.shape[2] * N_DEV, ln_v_g.shape[1], w_br_b.shape[1], w_xq.shape[2], w_up.shape[2] * N_DEV
    KVW = (IN - 2 * GW - AW - 2 * D) // 2
    d = Dims(T=T, D=D, IN=IN, GW=GW, G=GW // GROUP_DIM, AW=AW, KVW=KVW, HQ=AW // HEAD_DIM, HKV=KVW // HEAD_DIM, XW=XW,
             XH=XW // X_HEAD_DIM, MEM=mem.shape[1], DFF=DFF)
    place = jnp.stack([lax.axis_index("x"), lax.axis_index("y"), lax.axis_index("c")]).astype(jnp.int32)

    sched = _Sched()
    big_of = {g: (p, tr) for g, p, tr in BIG}

    def shard(l, g):
        p, tr = big_of[g]
        return (given[p][l].T if tr else given[p][l]).astype(BF16)

    Gs = [{}, {}]

    def plan_gather(l, names, hosts1, host2):
        shards = [shard(l, g) for g in names]
        if hosts1 is None:
            Gs[l].update(zip(names, _all_gather(shards, name=f"gather_{names[0]}_l{l}")))
            return

        def register(k, into):
            def done(outs):
                if k + 1 < len(hosts1):
                    register(k + 1, outs)
                else:
                    sched.at(host2, _gather_stage2(outs, lambda full: Gs[l].update(zip(names, full))))

            sched.at(hosts1[k], _gather_stage1(shards, done, part=(k, len(hosts1)), into=into))

        register(0, None)

    for entry in GATHER_PLAN:
        plan_gather(*entry)
    Ss = []
    for l in range(DEPTH):
        S = {n: given[n][l].reshape(1, -1) for n in SMALL_NAMES if n not in ("w_s", "b_s", "sinks")}
        S["w_s"], S["b_sT"], S["sinks"] = w_s[l], b_s[l].T, sinks[l]
        Ss.append(S)

    owns, recvs = {}, {}

    smalls_seen = {}
    gathered_small = []

    def pack_small(src):
        parts = []
        for l in range(DEPTH):
            for n in SMALL_NAMES:
                a = src[l]["b_sT"].T if n == "b_s" else src[l][n]
                parts.append(a[0, : d.HQ] if n == "sinks" else a)
        return _pack(parts)

    def on_small(l, small):
        smalls_seen[l] = small
        if len(smalls_seen) == DEPTH:
            host1, host2 = SMALL_GRAD_HOSTS
            sched.at(host1, _gather_stage1([pack_small(smalls_seen)], lambda part: sched.at(
                host2, _gather_stage2(part, lambda full: gathered_small.append(full[0])))))

    def on_grad(l, g, grad):
        if g == "small":
            return on_small(l, grad)
        sib_host, chips_host = GRAD_HOSTS[l][g]

        def after_sibling(got):
            owns[(l, g)], send = _pair_sum(grad, got[0], place, name=f"grad_pair_sum_{g}_l{l}")
            hosts = chips_host if isinstance(chips_host, tuple) else (chips_host,)

            def register(k, into):
                def done(r):
                    if k + 1 < len(hosts):
                        register(k + 1, r)
                    else:
                        recvs[(l, g)] = r[0]

                task = _chips_stage([send], done, part=(k, len(hosts)), into=into)
                if hosts[k] is None:
                    _comm_only([task], name=f"grad_chips_{g}_l{l}")
                else:
                    sched.at(hosts[k], task)

            register(0, None)

        task = _sibling_stage([grad], after_sibling)
        if sib_host is None:
            _comm_only([task], name=f"grad_sibling_{g}_l{l}")
        else:
            sched.at(sib_host, task)

    loss, dX, bigs, smalls = _local_step(d, x[0], mem[0], loss_target[0], Gs, Ss, sched, on_grad)
    loss = lax.psum(loss[0, 0], ("x", "y", "c"))

    assert not sched.pending, sorted(sched.pending)

    def viewed(p, tr):
        return tr and given[p].shape[2] % LANES != 0

    res = {p: None for _, p, _ in BIG}
    for l in reversed(range(DEPTH)):
        for g, p, tr in BIG:
            view = viewed(p, tr)
            w3, m3, v3 = ((jnp.swapaxes(a, 1, 2) if view else a) for a in (given[p], given["m_" + p], given["v_" + p]))
            res[p] = _adam_big(w3, m3, v3, owns[(l, g)], recvs[(l, g)], l, tr and not view, res[p], name=f"adamw_{g}_l{l}")
    for g, p, tr in BIG:
        if viewed(p, tr):
            res[p] = [jnp.swapaxes(a, 1, 2) for a in res[p]]

    shapes = [given[n].shape[1:] for n in SMALL_NAMES]
    pw, pm, pv = (_pack([given[pre + n][l] for l in range(DEPTH) for n in SMALL_NAMES]) for pre in ("", "m_", "v_"))
    small_out = [_unpack(o, shapes * DEPTH) for o in _adam_small(pw, pm, pv, gathered_small[0], name="adamw_small")]

    names = ["w_in", "b_gate", "ln_v_g", "ln_v_b", "w_s", "b_s", "sinks", "w_br_a", "w_br_b", "w_o", "ln1_g", "ln1_b", "w_xq",
             "w_xkv", "w_xo", "ln2_g", "ln2_b", "w_up", "w_down", "ln3_g", "ln3_b"]
    out = [loss, dX[None]]
    for kind in range(4):
        for n in names:
            if n in SMALL_NAMES:
                i = SMALL_NAMES.index(n)
                out.append(jnp.stack([small_out[kind][l * len(SMALL_NAMES) + i] for l in range(DEPTH)]))
            else:
                out.append(res[n][kind])
    return tuple(out)
```

```python
import collections
import functools

import jax
import jax.numpy as jnp
from jax import lax
from jax.experimental import pallas as pl
from jax.experimental.pallas import tpu as pltpu

F32 = jnp.float32
BF16 = jnp.bfloat16
MESH = pl.DeviceIdType.MESH

N_DEV = 8
DEPTH = 2
CHUNK = 128
HEAD_DIM = 64
ROPE_HALF = HEAD_DIM // 2
X_HEAD_DIM = 128
GROUP_DIM = 128
LANES = 128
ROPE_THETA = 10000.0
LN_EPS = 1e-5
ALPHA = (2 * DEPTH) ** 0.25
ADAM_LR = 0.001
ADAM_B1 = 0.9
ADAM_B2 = 0.999
ADAM_EPS = 1e-08
ADAM_WD = 0.01
ADAM_STEP = 10
VMEM_BYTES = 64 * 1024 * 1024
VMEM_TEMP_BYTES = 20 * 1024 * 1024

Dims = collections.namedtuple("Dims", "T D IN GW G AW KVW HQ HKV XW XH MEM DFF")


def _offsets(d):
    off_v = d.GW
    off_q = 2 * d.GW
    off_k = off_q + d.AW
    off_va = off_k + d.KVW
    off_ga = off_va + d.KVW
    off_gb = off_ga + d.D
    return off_v, off_q, off_k, off_va, off_ga, off_gb


def _tile(dim, pref, align=LANES):
    if dim <= pref:
        return dim
    t = pref - pref % align
    while t >= align:
        if dim % t == 0:
            return t
        t -= align
    return dim


def _nbytes(shape, dtype):
    n = 1
    for s in shape:
        n *= s
    return n * jnp.dtype(dtype).itemsize


def _params(sem, blocks, scratch=0):
    need = 2 * sum(_nbytes(s, t) for s, t in blocks) + scratch + VMEM_TEMP_BYTES
    limit = min(max(need, 32 * 1024 * 1024), VMEM_BYTES - 4 * 1024 * 1024)
    return pltpu.CompilerParams(dimension_semantics=sem, vmem_limit_bytes=limit)


def _dot(a, b, kind):
    dn = {"nn": (((1,), (0,)), ((), ())), "nt": (((1,), (1,)), ((), ())), "tn": (((0,), (0,)), ((), ()))}[kind]
    return lax.dot_general(a, b, dn, preferred_element_type=F32)


def _gelu(x):
    c = 0.7978845608028654
    t = jnp.tanh(c * (x + 0.044715 * (x * x * x)))
    return 0.5 * x * (1.0 + t)


def _gelu_grad(x):
    c = 0.7978845608028654
    x2 = x * x
    t = jnp.tanh(c * (x + 0.044715 * (x2 * x)))
    return 0.5 * (1.0 + t) + 0.5 * x * (1.0 - t * t) * (c * (1.0 + 3.0 * 0.044715 * x2))


def _ln_fwd(z, g, b):
    mu = jnp.mean(z, axis=-1, keepdims=True)
    zc = z - mu
    var = jnp.mean(zc * zc, axis=-1, keepdims=True)
    rstd = lax.rsqrt(var + LN_EPS)
    xhat = zc * rstd
    return xhat * g + b, xhat, rstd


def _ln_bwd_vals(dy, xhat, rstd, g):
    gd = dy * g
    m1 = jnp.mean(gd, axis=-1, keepdims=True)
    m2 = jnp.mean(gd * xhat, axis=-1, keepdims=True)
    return rstd * (gd - m1 - xhat * m2)


def _acc_store(ref, val, first):
    @pl.when(first)
    def _():
        ref[...] = val

    @pl.when(jnp.logical_not(first))
    def _():
        ref[...] += val


def _matmul(a, b, kind, *, name, tm=512, tn=1024, tk=2048, out_defs=(("mn", F32),), epilogue=None, extras=(), sched=None):
    if kind == "nn":
        (M, K), (K2, N) = a.shape, b.shape
    elif kind == "nt":
        (M, K), (N, K2) = a.shape, b.shape
    else:
        (K, M), (K2, N) = a.shape, b.shape
    assert K == K2, (a.shape, b.shape, kind)
    tm, tn, tk = _tile(M, tm), _tile(N, tn), _tile(K, tk)
    nk = K // tk
    blocks = []

    def spec(shape, imap, dtype):
        blocks.append((shape, dtype))
        return pl.BlockSpec(shape, imap)

    if kind == "tn":
        a_spec = spec((tk, tm), lambda j, i, k: (k, i), a.dtype)
    else:
        a_spec = spec((tm, tk), lambda j, i, k: (i, k), a.dtype)
    if kind == "nt":
        b_spec = spec((tn, tk), lambda j, i, k: (j, k), b.dtype)
    else:
        b_spec = spec((tk, tn), lambda j, i, k: (k, j), b.dtype)
    in_specs = [a_spec, b_spec]
    operands = [a, b]
    for arr, ekind, off in extras:
        if ekind == "mn":
            assert off % tn == 0
            in_specs.append(spec((tm, tn), lambda j, i, k, o=off // tn: (i, j + o), arr.dtype))
        elif ekind == "row":
            assert off % tn == 0
            in_specs.append(spec((1, tn), lambda j, i, k, o=off // tn: (0, j + o), arr.dtype))
        else:
            in_specs.append(spec((tm, 1), lambda j, i, k: (i, 0), arr.dtype))
        operands.append(arr)
    out_shape, out_specs = [], []
    for od in out_defs:
        okind, dt = od[0], od[1]
        if okind == "mn":
            out_shape.append(jax.ShapeDtypeStruct((M, N), dt))
            out_specs.append(spec((tm, tn), lambda j, i, k: (i, j), dt))
        elif okind == "cols":
            assert od[3] % tn == 0
            out_shape.append(jax.ShapeDtypeStruct((M, od[2]), dt))
            out_specs.append(spec((tm, tn), lambda j, i, k, o=od[3] // tn: (i, j + o), dt))
        elif okind == "m1":
            assert N == tn
            out_shape.append(jax.ShapeDtypeStruct((M, 1), dt))
            out_specs.append(spec((tm, 1), lambda j, i, k: (i, 0), dt))
        else:
            out_shape.append(jax.ShapeDtypeStruct((1, N), F32))
            out_specs.append(spec((1, tn), lambda j, i, k: (0, j), F32))
    n_ex, n_out = len(extras), len(out_defs)
    ep = epilogue if epilogue is not None else (lambda acc: (acc,))
    in_place = nk > 1 and epilogue is None and tuple(out_defs) == (("mn", F32),)

    def body(*refs):
        a_ref, b_ref = refs[0], refs[1]
        ex_refs = refs[2 : 2 + n_ex]
        out_refs = refs[2 + n_ex : 2 + n_ex + n_out]
        i = pl.program_id(1)
        k = pl.program_id(2)

        def product():
            return _dot(a_ref[...], b_ref[...], kind)

        def finish(acc):
            vals = ep(acc, *[r[...] for r in ex_refs])
            for od, r, v in zip(out_defs, out_refs, vals):
                if od[0] == "acc":
                    _acc_store(r, v, i == 0)
                else:
                    r[...] = v.astype(od[1])

        if nk == 1:
            finish(product())
            return
        acc_ref = out_refs[0] if in_place else refs[-1]

        @pl.when(k == 0)
        def _():
            acc_ref[...] = product()

        if in_place:
            @pl.when(k > 0)
            def _():
                acc_ref[...] += product()
        else:
            if nk > 2:
                @pl.when(jnp.logical_and(k > 0, k < nk - 1))
                def _():
                    acc_ref[...] += product()

            @pl.when(k == nk - 1)
            def _():
                finish(acc_ref[...] + product())

    scratch = [pltpu.VMEM((tm, tn), F32)] if nk > 1 and not in_place else []
    outs = _hosted(
        body,
        name=name,
        sched=sched,
        grid=(N // tn, M // tm, nk),
        in_specs=in_specs,
        out_specs=out_specs,
        out_shape=out_shape,
        scratch_shapes=scratch,
        compiler_params=_params(("arbitrary", "arbitrary", "arbitrary"), blocks, _nbytes((tm, tn), F32) if scratch else 0),
    )(*operands)
    return outs if len(outs) > 1 else outs[0]


def _ep_resid_ln(acc, resid, g, b):
    y, xhat, rstd = _ln_fwd(ALPHA * resid + acc, g, b)
    return y, y, xhat, rstd


def _ep_resid(acc, resid):
    return (ALPHA * resid + acc,)


def _ep_resid_ln_bwd(acc, resid, xhat, rstd, g):
    dy = ALPHA * resid + acc
    dz = _ln_bwd_vals(dy, xhat, rstd, g)
    return dz, dz, jnp.sum(dy * xhat, axis=0, keepdims=True), jnp.sum(dy, axis=0, keepdims=True)


def _ep_relu2(acc):
    r = jnp.maximum(acc, 0.0)
    return acc, r * r


def _ep_relu2_bwd(acc, h):
    return (acc * (2.0 * jnp.maximum(h, 0.0)),)


def _ep_gate_bwd(acc, ga, gb, ya, yb, bga, bgb):
    sa = jax.nn.sigmoid(ga + bga)
    sb = jax.nn.sigmoid(gb + bgb)
    dga = acc * ya * (sa * (1.0 - sa))
    dgb = acc * yb * (sb * (1.0 - sb))
    return (acc * sa, acc * sb, dga, dgb, jnp.sum(dga, axis=0, keepdims=True), jnp.sum(dgb, axis=0, keepdims=True))


def _ln_bwd(dy, xhat, rstd, g, *, name, sched=None, target=None):
    T, D = dy.shape
    tm = _tile(T, 256)
    head = target is not None

    def body(*refs):
        dy_ref, xh_ref, rs_ref, g_ref = refs[:4]
        dz_ref, dzb_ref, dg_ref, db_ref = refs[4 + head : 8 + head]
        first = pl.program_id(0) == 0
        dyv, xh = dy_ref[...], xh_ref[...]
        if head:
            err = dyv - refs[4][...]
            dyv = err * (1.0 / D)
            part = 0.5 * jnp.sum(jnp.sum(err * err, axis=1, keepdims=True) * (1.0 / D), axis=0, keepdims=True)
            _acc_store(refs[-1], part, first)
        dz = _ln_bwd_vals(dyv, xh, rs_ref[...], g_ref[...])
        dz_ref[...] = dz
        dzb_ref[...] = dz.astype(BF16)
        _acc_store(dg_ref, jnp.sum(dyv * xh, axis=0, keepdims=True), first)
        _acc_store(db_ref, jnp.sum(dyv, axis=0, keepdims=True), first)

    row = pl.BlockSpec((tm, D), lambda i: (i, 0))
    vec = pl.BlockSpec((1, D), lambda i: (0, 0))
    one = pl.BlockSpec((1, 1), lambda i: (0, 0))
    return _hosted(
        body,
        name=name,
        sched=sched,
        grid=(T // tm,),
        in_specs=[row, row, pl.BlockSpec((tm, 1), lambda i: (i, 0)), vec] + [row] * head,
        out_specs=[row, row, vec, vec] + [one] * head,
        out_shape=[
            jax.ShapeDtypeStruct((T, D), F32),
            jax.ShapeDtypeStruct((T, D), BF16),
            jax.ShapeDtypeStruct((1, D), F32),
            jax.ShapeDtypeStruct((1, D), F32),
        ] + [jax.ShapeDtypeStruct((1, 1), F32)] * head,
        compiler_params=_params(("arbitrary",), [((tm, D), F32)] * (4 + head)),
    )(*([dy, xhat, rstd, g] + [target] * head))


def _masked_mix_weights(wm_ref, G):
    r = lax.broadcasted_iota(jnp.int32, (CHUNK, CHUNK), 0)
    c = lax.broadcasted_iota(jnp.int32, (CHUNK, CHUNK), 1)
    return [jnp.where(c <= r, wm_ref[g], 0.0).astype(BF16) for g in range(G)]


def _gmlp_fwd(d, P, lnv_g, lnv_b, wm, bsT, *, name, sched=None):
    T, GW, G = d.T, d.GW, d.G
    tm = _tile(T, 256)
    nc = tm // CHUNK

    def body(u_ref, v_ref, g_ref, b_ref, wm_ref, bs_ref, o_ref):
        gu = _gelu(u_ref[...])
        vn, _, _ = _ln_fwd(_gelu(v_ref[...]), g_ref[...], b_ref[...])
        vn = vn.astype(BF16)
        wl = _masked_mix_weights(wm_ref, G)
        for c in range(nc):
            rows = slice(c * CHUNK, (c + 1) * CHUNK)
            for g in range(G):
                cols = slice(g * GROUP_DIM, (g + 1) * GROUP_DIM)
                mixed = _dot(wl[g], vn[rows, cols], "nn") + bs_ref[:, g : g + 1]
                o_ref[rows, cols] = (gu[rows, cols] * mixed).astype(BF16)

    return _hosted(
        body,
        name=name,
        sched=sched,
        grid=(T // tm,),
        in_specs=[
            pl.BlockSpec((tm, GW), lambda i: (i, 0)),
            pl.BlockSpec((tm, GW), lambda i: (i, 1)),
            pl.BlockSpec((1, GW), lambda i: (0, 0)),
            pl.BlockSpec((1, GW), lambda i: (0, 0)),
            pl.BlockSpec((G, CHUNK, CHUNK), lambda i: (0, 0, 0)),
            pl.BlockSpec((CHUNK, G), lambda i: (0, 0)),
        ],
        out_specs=pl.BlockSpec((tm, GW), lambda i: (i, 0)),
        out_shape=jax.ShapeDtypeStruct((T, GW), BF16),
        compiler_params=_params(("arbitrary",), [((tm, GW), F32)] * 3),
    )(P, P, lnv_g, lnv_b, wm, bsT)


def _gmlp_bwd(d, P, dya_b, Ga, lnv_g, lnv_b, wm, bsT, dP, *, name, sched=None):
    T, D, GW, G = d.T, d.D, d.GW, d.G
    tm = _tile(T, 256)
    nc = tm // CHUNK

    def body(u_ref, v_ref, dya_ref, ga_ref, g_ref, b_ref, wm_ref, bs_ref, dp_in, duv_ref, dwm_ref, dbs_ref, dlg_ref, dlb_ref, dgu_s, dvn_s):
        first = pl.program_id(0) == 0
        dsgu = _dot(dya_ref[...], ga_ref[...], "nn")
        u, v = u_ref[...], v_ref[...]
        gu = _gelu(u)
        lng = g_ref[...]
        vn, xh, rstd = _ln_fwd(_gelu(v), lng, b_ref[...])
        vn = vn.astype(BF16)
        wl = _masked_mix_weights(wm_ref, G)
        r = lax.broadcasted_iota(jnp.int32, (CHUNK, CHUNK), 0)
        cc = lax.broadcasted_iota(jnp.int32, (CHUNK, CHUNK), 1)
        lane_g = lax.broadcasted_iota(jnp.int32, (CHUNK, G), 1)
        dbs = jnp.zeros((CHUNK, G), F32)
        for g in range(G):
            cols = slice(g * GROUP_DIM, (g + 1) * GROUP_DIM)
            dw = jnp.zeros((CHUNK, CHUNK), F32)
            for c in range(nc):
                rows = slice(c * CHUNK, (c + 1) * CHUNK)
                mixed = _dot(wl[g], vn[rows, cols], "nn") + bs_ref[:, g : g + 1]
                ds = dsgu[rows, cols]
                dgu_s[rows, cols] = ds * mixed
                dmx = ds * gu[rows, cols]
                dbs = dbs + jnp.where(lane_g == g, jnp.sum(dmx, axis=1, keepdims=True), 0.0)
                dmx = dmx.astype(BF16)
                dw = dw + _dot(dmx, vn[rows, cols], "nt")
                dvn_s[rows, cols] = _dot(wl[g], dmx, "tn")
            _acc_store(dwm_ref.at[g], jnp.where(cc <= r, dw, 0.0), first)
        _acc_store(dbs_ref, dbs, first)
        dvn = dvn_s[...]
        _acc_store(dlg_ref, jnp.sum(dvn * xh, axis=0, keepdims=True), first)
        _acc_store(dlb_ref, jnp.sum(dvn, axis=0, keepdims=True), first)
        dgv = _ln_bwd_vals(dvn, xh, rstd, lng)
        duv_ref[:, :GW] = (dgu_s[...] * _gelu_grad(u)).astype(BF16)
        duv_ref[:, GW:] = (dgv * _gelu_grad(v)).astype(BF16)

    return _hosted(
        body,
        name=name,
        sched=sched,
        grid=(T // tm,),
        in_specs=[
            pl.BlockSpec((tm, GW), lambda i: (i, 0)),
            pl.BlockSpec((tm, GW), lambda i: (i, 1)),
            pl.BlockSpec((tm, D), lambda i: (i, 0)),
            pl.BlockSpec((D, GW), lambda i: (0, 0)),
            pl.BlockSpec((1, GW), lambda i: (0, 0)),
            pl.BlockSpec((1, GW), lambda i: (0, 0)),
            pl.BlockSpec((G, CHUNK, CHUNK), lambda i: (0, 0, 0)),
            pl.BlockSpec((CHUNK, G), lambda i: (0, 0)),
            pl.BlockSpec(memory_space=pl.ANY),
        ],
        out_specs=[
            pl.BlockSpec((tm, 2 * GW), lambda i: (i, 0)),
            pl.BlockSpec((G, CHUNK, CHUNK), lambda i: (0, 0, 0)),
            pl.BlockSpec((CHUNK, G), lambda i: (0, 0)),
            pl.BlockSpec((1, GW), lambda i: (0, 0)),
            pl.BlockSpec((1, GW), lambda i: (0, 0)),
        ],
        out_shape=[
            jax.ShapeDtypeStruct(dP.shape, BF16),
            jax.ShapeDtypeStruct((G, CHUNK, CHUNK), F32),
            jax.ShapeDtypeStruct((CHUNK, G), F32),
            jax.ShapeDtypeStruct((1, GW), F32),
            jax.ShapeDtypeStruct((1, GW), F32),
        ],
        scratch_shapes=[pltpu.VMEM((tm, GW), F32), pltpu.VMEM((tm, GW), F32)],
        input_output_aliases={8: 0},
        compiler_params=_params(
            ("arbitrary",), [((tm, GW), F32)] * 3 + [((tm, D), BF16), ((D, GW), BF16)], 2 * _nbytes((tm, GW), F32)
        ),
    )(P, P, dya_b, Ga, lnv_g, lnv_b, wm, bsT, dP)


def _swap_halves(x):
    w = x.shape[1]
    lane = lax.broadcasted_iota(jnp.int32, x.shape, 1)
    return jnp.where((lane % HEAD_DIM) < ROPE_HALF, pltpu.roll(x, w - ROPE_HALF, 1), pltpu.roll(x, ROPE_HALF, 1))


def _lane_tile(t, width):
    reps = width // LANES
    return t if reps == 1 else jnp.tile(t, (1, reps))


def _rope(x, cos2, sin2):
    w = x.shape[1]
    return x * _lane_tile(cos2, w) + _swap_halves(x) * _lane_tile(sin2, w)


def _rope_bwd(g, cos2, sin2):
    w = g.shape[1]
    return g * _lane_tile(cos2, w) - _swap_halves(g) * _lane_tile(sin2, w)


PAIR = LANES // HEAD_DIM


def _swa_valid(i):
    s = lax.broadcasted_iota(jnp.int32, (2 * CHUNK, CHUNK), 0)
    t = lax.broadcasted_iota(jnp.int32, (2 * CHUNK, CHUNK), 1)
    cur = jnp.logical_and(s >= CHUNK, s - CHUNK <= t)
    prev = jnp.logical_and(jnp.logical_and(s < CHUNK, s > t), i > 0)
    return jnp.logical_or(cur, prev)


def _half_variants(x, odd):
    lane = lax.broadcasted_iota(jnp.int32, x.shape, 1)
    if odd:
        hi = jnp.where(lane >= HEAD_DIM, x, jnp.zeros_like(x))
        return pltpu.roll(hi, HEAD_DIM, 1), hi
    lo = jnp.where(lane < HEAD_DIM, x, jnp.zeros_like(x))
    return lo, pltpu.roll(lo, HEAD_DIM, 1)


def _swa_probs_t(kx, qp, sink, valid):
    s = jnp.where(valid, _dot(kx, qp, "nt"), -jnp.inf)
    m = jnp.maximum(jnp.max(s, axis=0, keepdims=True), sink)
    e = jnp.exp(s - m)
    e_s = jnp.exp(sink - m)
    den = jnp.sum(e, axis=0, keepdims=True) + e_s
    return e / den, e_s / den


SWA_SCALE = HEAD_DIM**-0.5


def _swa_load(q_ref, kc_ref, kp_ref, vc_ref, vp_ref, cc, sc, cp, sp):
    qr = (_rope(q_ref[...], cc, sc) * SWA_SCALE).astype(BF16)
    kcat = jnp.concatenate([_rope(kp_ref[...], cp, sp), _rope(kc_ref[...], cc, sc)], axis=0)
    vcat = jnp.concatenate([vp_ref[...], vc_ref[...]], axis=0)
    return qr, kcat, vcat


def _swa_specs(d):
    _, off_q, off_k, off_va, _, _ = _offsets(d)
    assert off_q % d.AW == 0 and off_k % d.KVW == 0 and off_va % d.KVW == 0
    prev = lambda i: jnp.maximum(i - 1, 0)
    return [
        pl.BlockSpec(memory_space=pltpu.SMEM),
        pl.BlockSpec((CHUNK, d.AW), lambda i, o=off_q // d.AW: (i, o)),
        pl.BlockSpec((CHUNK, d.KVW), lambda i, o=off_k // d.KVW: (i, o)),
        pl.BlockSpec((CHUNK, d.KVW), lambda i, o=off_k // d.KVW: (prev(i), o)),
        pl.BlockSpec((CHUNK, d.KVW), lambda i, o=off_va // d.KVW: (i, o)),
        pl.BlockSpec((CHUNK, d.KVW), lambda i, o=off_va // d.KVW: (prev(i), o)),
        pl.BlockSpec((CHUNK, LANES), lambda i: (i, 0)),
        pl.BlockSpec((CHUNK, LANES), lambda i: (i, 0)),
        pl.BlockSpec((CHUNK, LANES), lambda i: (prev(i), 0)),
        pl.BlockSpec((CHUNK, LANES), lambda i: (prev(i), 0)),
    ]


def _swa_fwd(d, P, cos2, sin2, sinks, *, name, sched=None):
    T, AW, HQ, HKV = d.T, d.AW, d.HQ, d.HKV
    grp = HQ // HKV
    assert grp % PAIR == 0 and HKV % PAIR == 0

    def body(sink_ref, q_ref, kc_ref, kp_ref, vc_ref, vp_ref, cc_ref, sc_ref, cp_ref, sp_ref, o_ref):
        i = pl.program_id(0)
        qr, kcat, vcat = _swa_load(q_ref, kc_ref, kp_ref, vc_ref, vp_ref, cc_ref[...], sc_ref[...], cp_ref[...], sp_ref[...])
        valid = _swa_valid(i)
        for kv in range(HKV):
            col = slice((kv // PAIR) * LANES, (kv // PAIR + 1) * LANES)
            kx = [t.astype(BF16) for t in _half_variants(kcat[:, col], kv % PAIR)]
            vx = [t.astype(BF16) for t in _half_variants(vcat[:, col], kv % PAIR)]
            for pp in range(grp // PAIR):
                p = kv * (grp // PAIR) + pp
                qs = slice(p * LANES, (p + 1) * LANES)
                out = jnp.zeros((CHUNK, LANES), F32)
                for half in range(PAIR):
                    pt, _ = _swa_probs_t(kx[half], qr[:, qs], sink_ref[PAIR * p + half], valid)
                    out = out + _dot(pt.astype(BF16), vx[half], "tn")
                o_ref[:, qs] = out.astype(BF16)

    return _hosted(
        body,
        name=name,
        sched=sched,
        grid=(T // CHUNK,),
        in_specs=_swa_specs(d),
        out_specs=pl.BlockSpec((CHUNK, AW), lambda i: (i, 0)),
        out_shape=jax.ShapeDtypeStruct((T, AW), BF16),
        compiler_params=_params(("arbitrary",), [((CHUNK, AW), F32)] * 3),
    )(sinks, P, P, P, P, P, cos2, sin2, cos2, sin2)


def _swa_bwd(d, P, dyb_b, Gb, cos2, sin2, sinks, dP, *, name, sched=None):
    T, D, AW, KVW, HQ, HKV = d.T, d.D, d.AW, d.KVW, d.HQ, d.HKV
    grp = HQ // HKV
    nb = T // CHUNK
    off_q = _offsets(d)[1]
    assert grp % PAIR == 0 and HKV % PAIR == 0 and off_q % AW == 0

    def body(sink_ref, q_ref, kc_ref, kp_ref, vc_ref, vp_ref, cc_ref, sc_ref, cp_ref, sp_ref, dyb_ref, gb_ref, dp_in,
             dq_ref, dkv_ref, dsk_ref, acc_s, dq_s, dk_s, dv_s):
        i = pl.program_id(0)
        cc, sc, cp, sp = cc_ref[...], sc_ref[...], cp_ref[...], sp_ref[...]
        datt = _dot(dyb_ref[...], gb_ref[...], "nn").astype(BF16)
        qr, kcat, vcat = _swa_load(q_ref, kc_ref, kp_ref, vc_ref, vp_ref, cc, sc, cp, sp)
        valid = _swa_valid(i)
        lane1 = lax.broadcasted_iota(jnp.int32, (1, LANES), 1)
        lane2 = lax.broadcasted_iota(jnp.int32, (2 * CHUNK, LANES), 1)
        dsk = jnp.zeros((1, LANES), F32)
        for kvp in range(HKV // PAIR):
            col = slice(kvp * LANES, (kvp + 1) * LANES)
            dk_col = jnp.zeros((2 * CHUNK, LANES), F32)
            dv_col = jnp.zeros((2 * CHUNK, LANES), F32)
            for odd in range(PAIR):
                kv = kvp * PAIR + odd
                kx = [t.astype(BF16) for t in _half_variants(kcat[:, col], odd)]
                vx = [t.astype(BF16) for t in _half_variants(vcat[:, col], odd)]
                dk_half = [jnp.zeros((2 * CHUNK, LANES), F32) for _ in range(PAIR)]
                dv_half = [jnp.zeros((2 * CHUNK, LANES), F32) for _ in range(PAIR)]
                for pp in range(grp // PAIR):
                    p = kv * (grp // PAIR) + pp
                    qs = slice(p * LANES, (p + 1) * LANES)
                    qp, dop = qr[:, qs], datt[:, qs]
                    dq = jnp.zeros((CHUNK, LANES), F32)
                    for half in range(PAIR):
                        h = PAIR * p + half
                        pt, p_s = _swa_probs_t(kx[half], qp, sink_ref[h], valid)
                        dpt = _dot(vx[half], dop, "nt")
                        rs = jnp.sum(pt * dpt, axis=0, keepdims=True)
                        dst = (pt * (dpt - rs)).astype(BF16)
                        dsk = dsk + jnp.where(lane1 == h, -jnp.sum(p_s * rs, axis=1, keepdims=True), 0.0)
                        dq = dq + _dot(dst, kx[half], "tn")
                        dk_half[half] = dk_half[half] + _dot(dst, qp, "nn")
                        dv_half[half] = dv_half[half] + _dot(pt.astype(BF16), dop, "nn")
                    dq_s[:, qs] = dq * SWA_SCALE
                for acc_half, is_k in ((dk_half, True), (dv_half, False)):
                    lo = jnp.where(lane2 < HEAD_DIM, acc_half[0], 0.0)
                    hi = jnp.where(lane2 >= HEAD_DIM, acc_half[1], 0.0)
                    both = (pltpu.roll(lo, HEAD_DIM, 1) + hi) if odd else (lo + pltpu.roll(hi, HEAD_DIM, 1))
                    if is_k:
                        dk_col = dk_col + both
                    else:
                        dv_col = dv_col + both
            dk_s[:, col] = dk_col
            dv_s[:, col] = dv_col
        dq_ref[...] = _rope_bwd(dq_s[...], cc, sc).astype(BF16)
        cur = pl.ds(pl.multiple_of(i * CHUNK, CHUNK), CHUNK)
        acc_s[cur, :KVW] = _rope_bwd(dk_s[CHUNK:, :], cc, sc)
        acc_s[cur, KVW:] = dv_s[CHUNK:, :]

        @pl.when(i > 0)
        def _():
            prv = pl.ds(pl.multiple_of((i - 1) * CHUNK, CHUNK), CHUNK)
            acc_s[prv, :KVW] += _rope_bwd(dk_s[:CHUNK, :], cp, sp)
            acc_s[prv, KVW:] += dv_s[:CHUNK, :]

        _acc_store(dsk_ref, dsk, i == 0)

        @pl.when(i == nb - 1)
        def _():
            dkv_ref[...] = acc_s[...].astype(BF16)

    return _hosted(
        body,
        name=name,
        sched=sched,
        grid=(nb,),
        in_specs=_swa_specs(d) + [pl.BlockSpec((CHUNK, D), lambda i: (i, 0)), pl.BlockSpec((D, AW), lambda i: (0, 0)),
                                   pl.BlockSpec(memory_space=pl.ANY)],
        out_specs=[
            pl.BlockSpec((CHUNK, AW), lambda i, o=off_q // AW: (i, o)),
            pl.BlockSpec((T, 2 * KVW), lambda i: (0, 0)),
            pl.BlockSpec((1, LANES), lambda i: (0, 0)),
        ],
        out_shape=[
            jax.ShapeDtypeStruct(dP.shape, BF16),
            jax.ShapeDtypeStruct((T, 2 * KVW), BF16),
            jax.ShapeDtypeStruct((1, LANES), F32),
        ],
        input_output_aliases={12: 0},
        scratch_shapes=[
            pltpu.VMEM((T, 2 * KVW), F32),
            pltpu.VMEM((CHUNK, AW), F32),
            pltpu.VMEM((2 * CHUNK, KVW), F32),
            pltpu.VMEM((2 * CHUNK, KVW), F32),
        ],
        compiler_params=_params(
            ("arbitrary",), [((CHUNK, AW), F32)] * 3 + [((D, AW), BF16), ((T, 2 * KVW), BF16)], _nbytes((T, 2 * KVW), F32)
        ),
    )(sinks, P, P, P, P, P, cos2, sin2, cos2, sin2, dyb_b, Gb, dP)


def _place_cols(dst, src, off, *, name, sched=None):
    T, w = src.shape
    tm, tw = _tile(T, 512), _tile(w, 512)
    assert off % tw == 0

    def body(src_ref, dst_in, out_ref):
        out_ref[...] = src_ref[...]

    return _hosted(
        body,
        name=name,
        sched=sched,
        grid=(T // tm, w // tw),
        in_specs=[pl.BlockSpec((tm, tw), lambda i, j: (i, j)), pl.BlockSpec(memory_space=pl.ANY)],
        out_specs=pl.BlockSpec((tm, tw), lambda i, j, o=off // tw: (i, j + o)),
        out_shape=jax.ShapeDtypeStruct(dst.shape, dst.dtype),
        input_output_aliases={1: 0},
        compiler_params=_params(("arbitrary", "arbitrary"), [((tm, tw), dst.dtype)] * 2),
    )(src, dst)


def _branch_merge(d, sgu, att, Ga, Gb, P, b_gate, *, name, sched=None):
    T, D, GW, AW = d.T, d.D, d.GW, d.AW
    _, _, _, _, off_ga, off_gb = _offsets(d)
    tm, tn = _tile(T, 1024), _tile(D, 512)
    assert off_ga % tn == 0 and off_gb % tn == 0

    def body(s_ref, a_ref, ga_w, gb_w, ga_ref, gb_ref, bga_ref, bgb_ref, ya_ref, yb_ref, mg_ref):
        ya = _dot(s_ref[...], ga_w[...], "nt")
        yb = _dot(a_ref[...], gb_w[...], "nt")
        ya_ref[...] = ya
        yb_ref[...] = yb
        sa = jax.nn.sigmoid(ga_ref[...] + bga_ref[...])
        sb = jax.nn.sigmoid(gb_ref[...] + bgb_ref[...])
        mg_ref[...] = (sa * ya + sb * yb).astype(BF16)

    tile = pl.BlockSpec((tm, tn), lambda j, i: (i, j))
    return _hosted(
        body,
        name=name,
        sched=sched,
        grid=(D // tn, T // tm),
        in_specs=[
            pl.BlockSpec((tm, GW), lambda j, i: (i, 0)),
            pl.BlockSpec((tm, AW), lambda j, i: (i, 0)),
            pl.BlockSpec((tn, GW), lambda j, i: (j, 0)),
            pl.BlockSpec((tn, AW), lambda j, i: (j, 0)),
            pl.BlockSpec((tm, tn), lambda j, i, o=off_ga // tn: (i, j + o)),
            pl.BlockSpec((tm, tn), lambda j, i, o=off_gb // tn: (i, j + o)),
            pl.BlockSpec((1, tn), lambda j, i: (0, j)),
            pl.BlockSpec((1, tn), lambda j, i, o=D // tn: (0, j + o)),
        ],
        out_specs=[tile, tile, tile],
        out_shape=[jax.ShapeDtypeStruct((T, D), F32), jax.ShapeDtypeStruct((T, D), F32), jax.ShapeDtypeStruct((T, D), BF16)],
        compiler_params=_params(
            ("arbitrary", "arbitrary"),
            [((tm, GW), BF16), ((tm, AW), BF16), ((tn, GW), BF16), ((tn, AW), BF16)] + [((tm, tn), F32)] * 5,
        ),
    )(sgu, att, Ga, Gb, P, P, b_gate, b_gate)


def _xattn_probs(qh, kh):
    s = _dot(qh, kh, "nt") * (X_HEAD_DIM**-0.5)
    e = jnp.exp(s - jnp.max(s, axis=1, keepdims=True))
    return e / jnp.sum(e, axis=1, keepdims=True)


def _xattn_fwd(d, X, Xb, kv, Gxq, Gxo, ln_g, ln_b, *, name, sched=None):
    T, D, XW, XH, MEM = d.T, d.D, d.XW, d.XH, d.MEM
    tm = _tile(T, 256)

    def body(x_ref, xb_ref, kv_ref, wq_ref, wo_ref, g_ref, b_ref, y_ref, yb_ref, xh_ref, rs_ref, q_ref, o_ref, o_s):
        q = _dot(xb_ref[...], wq_ref[...], "nn").astype(BF16)
        q_ref[...] = q
        for h in range(XH):
            hs = slice(h * X_HEAD_DIM, (h + 1) * X_HEAD_DIM)
            p = _xattn_probs(q[:, hs], kv_ref[:, hs]).astype(BF16)
            o_s[:, hs] = _dot(p, kv_ref[:, XW + h * X_HEAD_DIM : XW + (h + 1) * X_HEAD_DIM], "nn").astype(BF16)
        o = o_s[...]
        o_ref[...] = o
        y, xhat, rstd = _ln_fwd(ALPHA * x_ref[...] + _dot(o, wo_ref[...], "nt"), g_ref[...], b_ref[...])
        y_ref[...] = y
        yb_ref[...] = y.astype(BF16)
        xh_ref[...] = xhat
        rs_ref[...] = rstd

    row = pl.BlockSpec((tm, D), lambda i: (i, 0))
    nar = pl.BlockSpec((tm, XW), lambda i: (i, 0))
    vec = pl.BlockSpec((1, D), lambda i: (0, 0))
    return _hosted(
        body,
        name=name,
        sched=sched,
        grid=(T // tm,),
        in_specs=[
            row,
            row,
            pl.BlockSpec((MEM, 2 * XW), lambda i: (0, 0)),
            pl.BlockSpec((D, XW), lambda i: (0, 0)),
            pl.BlockSpec((D, XW), lambda i: (0, 0)),
            vec,
            vec,
        ],
        out_specs=[row, row, row, pl.BlockSpec((tm, 1), lambda i: (i, 0)), nar, nar],
        out_shape=[
            jax.ShapeDtypeStruct((T, D), F32),
            jax.ShapeDtypeStruct((T, D), BF16),
            jax.ShapeDtypeStruct((T, D), F32),
            jax.ShapeDtypeStruct((T, 1), F32),
            jax.ShapeDtypeStruct((T, XW), BF16),
            jax.ShapeDtypeStruct((T, XW), BF16),
        ],
        scratch_shapes=[pltpu.VMEM((tm, XW), BF16)],
        compiler_params=_params(("arbitrary",), [((tm, D), F32)] * 4 + [((D, XW), BF16)] * 2),
    )(X, Xb, kv, Gxq, Gxo, ln_g, ln_b)


def _xattn_bwd(d, dz, dzb, q_b, kv, Gxq, Gxo, xhat, rstd, ln_g, *, name, sched=None):
    T, D, XW, XH, MEM = d.T, d.D, d.XW, d.XH, d.MEM
    tm = _tile(T, 256)
    scale = X_HEAD_DIM**-0.5

    def body(dz_ref, dzb_ref, q_ref, kv_ref, wq_ref, wo_ref, xh_ref, rs_ref, g_ref, dx_ref, dxb_ref, dg_ref, db_ref, dq_ref, dkv_ref, dq_s):
        first = pl.program_id(0) == 0
        do = _dot(dzb_ref[...], wo_ref[...], "nn").astype(BF16)
        for h in range(XH):
            hs = slice(h * X_HEAD_DIM, (h + 1) * X_HEAD_DIM)
            vs = slice(XW + h * X_HEAD_DIM, XW + (h + 1) * X_HEAD_DIM)
            kh, vh, qh, doh = kv_ref[:, hs], kv_ref[:, vs], q_ref[:, hs], do[:, hs]
            p = _xattn_probs(qh, kh)
            dp = _dot(doh, vh, "nt")
            ds = (p * (dp - jnp.sum(p * dp, axis=1, keepdims=True)) * scale).astype(BF16)
            dq_s[:, hs] = _dot(ds, kh, "nn").astype(BF16)
            _acc_store(dkv_ref.at[h], _dot(ds, qh, "tn"), first)
            _acc_store(dkv_ref.at[XH + h], _dot(p.astype(BF16), doh, "tn"), first)
        dq = dq_s[...]
        dq_ref[...] = dq
        dy = ALPHA * dz_ref[...] + _dot(dq, wq_ref[...], "nt")
        xh = xh_ref[...]
        dz_in = _ln_bwd_vals(dy, xh, rs_ref[...], g_ref[...])
        dx_ref[...] = dz_in
        dxb_ref[...] = dz_in.astype(BF16)
        _acc_store(dg_ref, jnp.sum(dy * xh, axis=0, keepdims=True), first)
        _acc_store(db_ref, jnp.sum(dy, axis=0, keepdims=True), first)

    row = pl.BlockSpec((tm, D), lambda i: (i, 0))
    nar = pl.BlockSpec((tm, XW), lambda i: (i, 0))
    vec = pl.BlockSpec((1, D), lambda i: (0, 0))
    return _hosted(
        body,
        name=name,
        sched=sched,
        grid=(T // tm,),
        in_specs=[
            row,
            row,
            nar,
            pl.BlockSpec((MEM, 2 * XW), lambda i: (0, 0)),
            pl.BlockSpec((D, XW), lambda i: (0, 0)),
            pl.BlockSpec((D, XW), lambda i: (0, 0)),
            row,
            pl.BlockSpec((tm, 1), lambda i: (i, 0)),
            vec,
        ],
        out_specs=[row, row, vec, vec, nar, pl.BlockSpec((2 * XH, MEM, X_HEAD_DIM), lambda i: (0, 0, 0))],
        out_shape=[
            jax.ShapeDtypeStruct((T, D), F32),
            jax.ShapeDtypeStruct((T, D), BF16),
            jax.ShapeDtypeStruct((1, D), F32),
            jax.ShapeDtypeStruct((1, D), F32),
            jax.ShapeDtypeStruct((T, XW), BF16),
            jax.ShapeDtypeStruct((2 * XH, MEM, X_HEAD_DIM), F32),
        ],
        scratch_shapes=[pltpu.VMEM((tm, XW), BF16)],
        compiler_params=_params(("arbitrary",), [((tm, D), F32)] * 5 + [((D, XW), BF16)] * 2),
    )(dz, dzb, q_b, kv, Gxq, Gxo, xhat, rstd, ln_g)


def _resid_ln_outs():
    return (("mn", F32), ("mn", BF16), ("mn", F32), ("m1", F32))


def _layer_fwd(d, X, Xb, memb, G, S, cos2, sin2, tag, sched=None):
    D = d.D
    mm = functools.partial(_matmul, sched=sched)
    P = mm(Xb, G["in"], "nt", name=f"proj_{tag}", tm=1024, tn=_tile(d.IN, 1280))
    sgu = _gmlp_fwd(d, P, S["ln_v_g"], S["ln_v_b"], S["w_s"], S["b_sT"], name=f"gmlp_fwd_{tag}", sched=sched)
    att = _swa_fwd(d, P, cos2, sin2, S["sinks"], name=f"swa_fwd_{tag}", sched=sched)
    ya, yb, merged = _branch_merge(d, sgu, att, G["a"], G["b"], P, S["b_gate"], name=f"merge_{tag}", sched=sched)
    X1, X1b, xh1, rs1 = mm(
        merged, G["o"], "nn", name=f"oproj_ln_{tag}", tn=D, tk=1024, out_defs=_resid_ln_outs(), epilogue=_ep_resid_ln,
        extras=((X, "mn", 0), (S["ln1_g"], "row", 0), (S["ln1_b"], "row", 0)),
    )
    kv = mm(memb, G["xkv"], "nn", name=f"xkv_{tag}", out_defs=(("mn", BF16),))
    X2, X2b, xh2, rs2, q_b, o_b = _xattn_fwd(d, X1, X1b, kv, G["xq"], G["xo"], S["ln2_g"], S["ln2_b"], name=f"xattn_fwd_{tag}", sched=sched)
    H, A = mm(X2b, G["up"], "nt", name=f"up_{tag}", tm=1024, out_defs=(("mn", F32), ("mn", BF16)), epilogue=_ep_relu2)
    X3, X3b, xh3, rs3 = mm(
        A, G["down"], "nn", name=f"down_ln_{tag}", tn=D, tk=1024, out_defs=_resid_ln_outs(), epilogue=_ep_resid_ln,
        extras=((X2, "mn", 0), (S["ln3_g"], "row", 0), (S["ln3_b"], "row", 0)),
    )
    saved = dict(Xb=Xb, P=P, sgu=sgu, att=att, ya=ya, yb=yb, merged=merged, X1b=X1b, xh1=xh1, rs1=rs1, kv=kv, q_b=q_b,
                 o_b=o_b, X2b=X2b, xh2=xh2, rs2=rs2, H=H, A=A, xh3=xh3, rs3=rs3)
    return X3, X3b, saved


def _layer_bwd(d, dXout, sv, memb, G, S, cos2, sin2, tag, sched=None, on_grad=None, target=None, below=None):
    D = d.D
    mm = functools.partial(_matmul, sched=sched)
    emit = on_grad if on_grad is not None else (lambda n, g: None)
    wide = dict(tn=D)
    _, _, off_k, _, off_ga, off_gb = _offsets(d)
    bf = (("mn", BF16),)
    if isinstance(dXout, tuple):
        (dz3, dz3b, dg3, db3), loss = dXout, []
    else:
        dz3, dz3b, dg3, db3, *loss = _ln_bwd(dXout, sv["xh3"], sv["rs3"], S["ln3_g"], name=f"ln3_bwd_{tag}", sched=sched, target=target)
    dHb = mm(dz3b, G["down"], "nt", name=f"down_dx_{tag}", tm=1024, out_defs=bf, epilogue=_ep_relu2_bwd, extras=((sv["H"], "mn", 0),))
    g_down = mm(sv["A"], dz3b, "tn", name=f"down_dw_{tag}", **wide)
    emit("down", g_down)
    ln_bwd_outs = (("mn", F32), ("mn", BF16), ("acc", F32), ("acc", F32))
    dz2, dz2b, dg2, db2 = mm(
        dHb, G["up"], "nn", name=f"up_dx_{tag}", tn=D, tk=1024, out_defs=ln_bwd_outs, epilogue=_ep_resid_ln_bwd,
        extras=((dz3, "mn", 0), (sv["xh2"], "mn", 0), (sv["rs2"], "m1", 0), (S["ln2_g"], "row", 0)),
    )
    g_up = mm(dHb, sv["X2b"], "tn", name=f"up_dw_{tag}", **wide)
    emit("up", g_up)
    dz1, dz1b, dg1, db1, dq_b, dkv = _xattn_bwd(d, dz2, dz2b, sv["q_b"], sv["kv"], G["xq"], G["xo"], sv["xh1"], sv["rs1"], S["ln1_g"],
                                               name=f"xattn_bwd_{tag}", sched=sched)
    g_xo = mm(dz2b, sv["o_b"], "tn", name=f"xo_dw_{tag}")
    emit("xo", g_xo)
    g_xq = mm(sv["X1b"], dq_b, "tn", name=f"xq_dw_{tag}")
    emit("xq", g_xq)
    dkv_b = dkv.transpose(1, 0, 2).reshape(d.MEM, 2 * d.XW).astype(BF16)
    g_xkv = mm(memb, dkv_b, "tn", name=f"xkv_dw_{tag}")
    emit("xkv", g_xkv)
    dya_b, dyb_b, dP, dgb_b, dbga, dbgb = mm(
        dz1b, G["o"], "nt", name=f"oproj_dx_{tag}", tm=1024, tn=512,
        out_defs=(("mn", BF16), ("mn", BF16), ("cols", BF16, d.IN, off_ga), ("mn", BF16), ("acc", F32), ("acc", F32)),
        epilogue=_ep_gate_bwd,
        extras=((sv["P"], "mn", off_ga), (sv["P"], "mn", off_gb), (sv["ya"], "mn", 0), (sv["yb"], "mn", 0),
                (S["b_gate"], "row", 0), (S["b_gate"], "row", D)),
    )
    g_o = mm(sv["merged"], dz1b, "tn", name=f"oproj_dw_{tag}", **wide)
    emit("o", g_o)
    dP = _place_cols(dP, dgb_b, off_gb, name=f"place_dgb_{tag}")
    dP, dwm, dbsT, dlvg, dlvb = _gmlp_bwd(d, sv["P"], dya_b, G["a"], S["ln_v_g"], S["ln_v_b"], S["w_s"], S["b_sT"], dP, name=f"gmlp_bwd_{tag}", sched=sched)
    g_a = mm(dya_b, sv["sgu"], "tn", name=f"bra_dw_{tag}")
    emit("a", g_a)
    dP, dkvr_b, dsk = _swa_bwd(d, sv["P"], dyb_b, G["b"], cos2, sin2, S["sinks"], dP, name=f"swa_bwd_{tag}", sched=sched)
    g_b = mm(dyb_b, sv["att"], "tn", name=f"brb_dw_{tag}")
    emit("b", g_b)
    dP = _place_cols(dP, dkvr_b, off_k, name=f"place_dkv_{tag}")
    small = dict(b_gate=jnp.concatenate([dbga, dbgb], axis=1), ln_v_g=dlvg, ln_v_b=dlvb, w_s=dwm, b_sT=dbsT, sinks=dsk,
                 ln1_g=dg1, ln1_b=db1, ln2_g=dg2, ln2_b=db2, ln3_g=dg3, ln3_b=db3)
    emit("small", small)
    g_in = mm(dP, sv["Xb"], "tn", name=f"proj_dw_{tag}", **wide)
    emit("in", g_in)
    if below is None:
        dXin = mm(dP, G["in"], "nn", name=f"proj_dx_{tag}", tn=D, tk=_tile(d.IN, 1920), epilogue=_ep_resid, extras=((dz1, "mn", 0),))
    else:
        dXin = tuple(mm(
            dP, G["in"], "nn", name=f"proj_dx_{tag}", tn=D, tk=_tile(d.IN, 960), out_defs=ln_bwd_outs, epilogue=_ep_resid_ln_bwd,
            extras=((dz1, "mn", 0), (below[0], "mn", 0), (below[1], "m1", 0), (below[2], "row", 0)),
        ))
    big ={"in": g_in, "a": g_a, "b": g_b, "o": g_o, "xq": g_xq, "xkv": g_xkv, "xo": g_xo, "up": g_up, "down": g_down}
    return dXin, big, small, (loss[0] if loss else None)


def _rope_tables(T):
    inv = 1.0 / (ROPE_THETA ** (jnp.arange(0, HEAD_DIM, 2, dtype=F32) / HEAD_DIM))
    ang = jnp.arange(T, dtype=F32)[:, None] * inv[None, :]
    cos, sin = jnp.cos(ang), jnp.sin(ang)
    reps = LANES // HEAD_DIM
    return jnp.concatenate([cos, cos] * reps, axis=1), jnp.concatenate([-sin, sin] * reps, axis=1)


def _local_step(d, x, mem, target, Gs, Ss, sched=None, on_grad=None):
    cos2, sin2 = _rope_tables(d.T)
    memb = mem.astype(BF16)
    X, Xb = x, x.astype(BF16)
    saved = []
    for l in range(DEPTH):
        X, Xb, sv = _layer_fwd(d, X, Xb, memb, Gs[l], Ss[l], cos2, sin2, f"l{l}", sched)
        saved.append(sv)
    bigs, smalls = [None] * DEPTH, [None] * DEPTH
    dX, loss = X, None
    for l in reversed(range(DEPTH)):
        emit = None if on_grad is None else functools.partial(on_grad, l)
        head = target if l == DEPTH - 1 else None
        below = (saved[l - 1]["xh3"], saved[l - 1]["rs3"], Ss[l - 1]["ln3_g"]) if l > 0 else None
        dX, bigs[l], smalls[l], got = _layer_bwd(d, dX, saved[l], memb, Gs[l], Ss[l], cos2, sin2, f"l{l}", sched, emit, head, below)
        loss = got if got is not None else loss
    return loss, dX, bigs, smalls


def _place():
    x, y, c = lax.axis_index("x"), lax.axis_index("y"), lax.axis_index("c")
    return x, y, c, [(1 - x, y), (x, 1 - y), (1 - x, 1 - y)]


def _all_gather(shards, *, name, sched=None):
    n = len(shards)
    any_spec = pl.BlockSpec(memory_space=pl.ANY)

    def body(*refs):
        ins, outs = refs[:n], refs[n : 2 * n]
        send_sems, recv_sems, local_sems = refs[2 * n :]
        x, y, c, chips = _place()
        me, sibling = (x, y, c), (x, y, 1 - c)

        def rows(w, p):
            r = ins[w].shape[0]
            return outs[w].at[pl.ds((4 * p[0] + 2 * p[1] + p[2]) * r, r), :]

        def copy(w, k, block, to, src=None):
            return pltpu.make_async_remote_copy(
                src_ref=rows(w, block) if src is None else src, dst_ref=rows(w, block),
                send_sem=send_sems.at[7 * w + k], recv_sem=recv_sems.at[7 * w + k], device_id=to, device_id_type=MESH)

        mine = [pltpu.make_async_copy(ins[w], rows(w, me), local_sems.at[w]) for w in range(n)]
        for cp in mine:
            cp.start()
        first = []
        for w in range(n):
            first.append(copy(w, 0, me, sibling, src=ins[w]))
            first += [copy(w, 1 + j, me, (*chip, c), src=ins[w]) for j, chip in enumerate(chips)]
        for cp in first:
            cp.start()
        passed = []
        for j, chip in enumerate(chips):
            for w in range(n):
                copy(w, 1 + j, (*chip, c), me).wait_recv()
                fwd = copy(w, 4 + j, (*chip, c), sibling)
                fwd.start()
                passed.append(fwd)
        for w in range(n):
            copy(w, 0, sibling, me).wait_recv()
            for j, chip in enumerate(chips):
                copy(w, 4 + j, (*chip, 1 - c), me).wait_recv()
        for cp in first + passed:
            cp.wait_send()
        for cp in mine:
            cp.wait()

    return _hosted(
        body,
        name=name,
        sched=sched,
        in_specs=[any_spec] * n,
        out_specs=[any_spec] * n,
        out_shape=[jax.ShapeDtypeStruct((N_DEV * s.shape[0], s.shape[1]), s.dtype) for s in shards],
        scratch_shapes=[pltpu.SemaphoreType.DMA((7 * n,)), pltpu.SemaphoreType.DMA((7 * n,)), pltpu.SemaphoreType.DMA((n,))],
    )(*shards)


class _Task:
    def __init__(self, ins, out_shapes, n_remote, n_local, plan, done, aliases=None):
        self.ins, self.out_shapes, self.n_remote, self.n_local = list(ins), list(out_shapes), n_remote, n_local
        self.plan, self.done, self.aliases = plan, done, dict(aliases or {})


class _Sched:
    def __init__(self):
        self.pending = {}

    def at(self, host, task):
        self.pending.setdefault(host, []).append(task)

    def take(self, host):
        return self.pending.pop(host, [])


def _in_hbm(out_shape):
    if isinstance(out_shape, (list, tuple)):
        return [_in_hbm(s) for s in out_shape]
    return pltpu.HBM(out_shape.shape, out_shape.dtype)


def _hosted(body, *, name, sched=None, tasks=(), grid=(), in_specs=None, out_specs=None, out_shape=(), scratch_shapes=(),
            compiler_params=None, input_output_aliases=None):
    tasks = list(tasks) + (sched.take(name) if sched is not None else [])
    scratch_shapes = list(scratch_shapes)
    kwargs = dict(name=name)
    core_aliases = dict(input_output_aliases or {})
    if grid:
        kwargs["grid"] = grid
    if compiler_params is not None:
        kwargs["compiler_params"] = compiler_params
    if not tasks:
        if in_specs is not None:
            kwargs.update(in_specs=in_specs, out_specs=out_specs)
        return pl.pallas_call(body, out_shape=_in_hbm(out_shape), scratch_shapes=scratch_shapes, input_output_aliases=core_aliases, **kwargs)
    assert in_specs is not None and out_specs is not None, name
    single = not isinstance(out_shape, (list, tuple))
    core_shape = [out_shape] if single else list(out_shape)
    core_specs = [out_specs] if single else list(out_specs)
    in_specs = list(in_specs)
    n_in, n_out, n_scr = len(in_specs), len(core_shape), len(scratch_shapes)
    t_ins = [a for t in tasks for a in t.ins]
    t_outs = [s for t in tasks for s in t.out_shapes]
    n_rem = sum(t.n_remote for t in tasks)
    n_loc = sum(t.n_local for t in tasks)
    aliases, pos_in, pos_out = dict(core_aliases), n_in, n_out
    for t in tasks:
        for a, b in t.aliases.items():
            aliases[pos_in + a] = pos_out + b
        pos_in += len(t.ins)
        pos_out += len(t.out_shapes)
    any_spec = pl.BlockSpec(memory_space=pl.ANY)

    def wrapped(*refs):
        core_in, t_in = refs[:n_in], refs[n_in : n_in + len(t_ins)]
        o0 = n_in + len(t_ins)
        core_out, t_out = refs[o0 : o0 + n_out], refs[o0 + n_out : o0 + n_out + len(t_outs)]
        s0 = o0 + n_out + len(t_outs)
        core_scr = refs[s0 : s0 + n_scr]
        send_sems, recv_sems, local_sems = refs[s0 + n_scr :]
        remote, local = [], []
        pi = po = pr = pq = 0
        for t in tasks:
            r, q = t.plan(t_in[pi : pi + len(t.ins)], t_out[po : po + len(t.out_shapes)], send_sems, recv_sems, local_sems, pr, pq)
            assert len(r) == t.n_remote and len(q) == t.n_local
            remote += r
            local += q
            pi, po, pr, pq = pi + len(t.ins), po + len(t.out_shapes), pr + t.n_remote, pq + t.n_local

        def start():
            for cp in local + remote:
                cp.start()

        def finish():
            for cp in remote + local:
                cp.wait()

        if grid:
            ids = [pl.program_id(a) for a in range(len(grid))]
            first = functools.reduce(jnp.logical_and, [i == 0 for i in ids])
            last = functools.reduce(jnp.logical_and, [i == g - 1 for i, g in zip(ids, grid)])
            pl.when(first)(start)
            body(*core_in, *core_out, *core_scr)
            pl.when(last)(finish)
        else:
            start()
            body(*core_in, *core_out, *core_scr)
            finish()

    call = pl.pallas_call(
        wrapped,
        in_specs=in_specs + [any_spec] * len(t_ins),
        out_specs=core_specs + [any_spec] * len(t_outs),
        out_shape=_in_hbm(core_shape + t_outs),
        scratch_shapes=scratch_shapes
        + [pltpu.SemaphoreType.DMA((n_rem,)), pltpu.SemaphoreType.DMA((n_rem,)), pltpu.SemaphoreType.DMA((max(n_loc, 1),))],
        input_output_aliases=aliases,
        **kwargs,
    )

    def run(*operands):
        outs = call(*operands, *t_ins)
        po = n_out
        for t in tasks:
            t.done(list(outs[po : po + len(t.out_shapes)]))
            po += len(t.out_shapes)
        return outs[0] if single else list(outs[:n_out])

    return run


def _comm_only(tasks, *, name):
    _hosted(lambda: None, name=name, tasks=tasks, in_specs=[], out_shape=[], out_specs=[])()


def _rows(ref, block, n_blocks):
    r = ref.shape[0] // n_blocks
    return ref.at[pl.ds(block * r, r), :]


def _remote(src, dst, send_sems, recv_sems, k, to):
    return pltpu.make_async_remote_copy(src_ref=src, dst_ref=dst, send_sem=send_sems.at[k], recv_sem=recv_sems.at[k],
                                        device_id=to, device_id_type=MESH)


def _gather_stage1(shards, done, part=(0, 1), into=None):
    n = len(shards)
    k, n_parts = part

    def plan(ins, outs, send_sems, recv_sems, local_sems, pr, pq):
        x, y, c, chips = _place()
        mine = 4 * x + 2 * y + c
        remote, local = [], []
        for w in range(n):
            r = shards[w].shape[0]
            rp = r // n_parts
            src = ins[w].at[pl.ds(k * rp, rp), :]
            dst = outs[w].at[pl.ds(mine * r + k * rp, rp), :]
            local.append(pltpu.make_async_copy(src, dst, local_sems.at[pq + w]))
            for j, to in enumerate([(x, y, 1 - c)] + [(*chip, c) for chip in chips]):
                remote.append(_remote(src, dst, send_sems, recv_sems, pr + 4 * w + j, to))
        return remote, local

    shapes = [jax.ShapeDtypeStruct((N_DEV * s.shape[0], s.shape[1]), s.dtype) for s in shards]
    if into is None:
        return _Task(shards, shapes, 4 * n, n, plan, done)
    return _Task(list(shards) + list(into), shapes, 4 * n, n, plan, done, aliases={n + w: w for w in range(n)})


def _gather_stage2(partial, done):
    n = len(partial)

    def plan(ins, outs, send_sems, recv_sems, local_sems, pr, pq):
        x, y, c, chips = _place()
        remote = []
        for w in range(n):
            for j, chip in enumerate(chips):
                rows = _rows(outs[w], 4 * chip[0] + 2 * chip[1] + c, N_DEV)
                remote.append(_remote(rows, rows, send_sems, recv_sems, pr + 3 * w + j, (x, y, 1 - c)))
        return remote, []

    shapes = [jax.ShapeDtypeStruct(p.shape, p.dtype) for p in partial]
    return _Task(partial, shapes, 3 * n, 0, plan, done, aliases={w: w for w in range(n)})


def _sibling_stage(grads, done):
    n = len(grads)

    def plan(ins, outs, send_sems, recv_sems, local_sems, pr, pq):
        x, y, c, _ = _place()
        remote = []
        for w in range(n):
            for k in range(4):
                src = _rows(ins[w], 4 * (k // 2) + 2 * (k % 2) + (1 - c), N_DEV)
                remote.append(_remote(src, _rows(outs[w], k, 4), send_sems, recv_sems, pr + 4 * w + k, (x, y, 1 - c)))
        return remote, []

    shapes = [jax.ShapeDtypeStruct((g.shape[0] // 2, g.shape[1]), g.dtype) for g in grads]
    return _Task(grads, shapes, 4 * n, 0, plan, done)


def _chips_stage(sends, done, part=(0, 1), into=None):
    n = len(sends)
    k, n_parts = part

    def plan(ins, outs, send_sems, recv_sems, local_sems, pr, pq):
        x, y, c, chips = _place()
        remote = []
        for w in range(n):
            r = sends[w].shape[0] // 3
            rp = r // n_parts
            for j, chip in enumerate(chips):
                rows = pl.ds(j * r + k * rp, rp)
                remote.append(_remote(ins[w].at[rows, :], outs[w].at[rows, :], send_sems, recv_sems, pr + 3 * w + j, (*chip, c)))
        return remote, []

    shapes = [jax.ShapeDtypeStruct(s.shape, s.dtype) for s in sends]
    if into is None:
        return _Task(sends, shapes, 3 * n, 0, plan, done)
    return _Task(list(sends) + list(into), shapes, 3 * n, 0, plan, done, aliases={n + w: w for w in range(n)})


def _pair_sum(grad, got, place, *, name):
    R, C = grad.shape
    r = R // N_DEV
    tr = _tile(r, 256, 16)
    nt = r // tr

    def chip_of(s, pr):
        fx = jnp.where((s == 1) | (s == 3), 1, 0)
        fy = jnp.where(s >= 2, 1, 0)
        return pr[0] ^ fx, pr[1] ^ fy

    def grad_map(s, t, pr):
        kx, ky = chip_of(s, pr)
        return ((4 * kx + 2 * ky + pr[2]) * nt + t, 0)

    def got_map(s, t, pr):
        kx, ky = chip_of(s, pr)
        return ((2 * kx + ky) * nt + t, 0)

    def body(pr, g_ref, r_ref, own_ref, send_ref):
        s = pl.program_id(0)
        tot = g_ref[...] + r_ref[...]

        @pl.when(s == 0)
        def _():
            own_ref[...] = tot

        @pl.when(s > 0)
        def _():
            send_ref[...] = tot.astype(BF16)

    return pl.pallas_call(
        body,
        name=name,
        grid_spec=pltpu.PrefetchScalarGridSpec(
            num_scalar_prefetch=1,
            grid=(4, nt),
            in_specs=[pl.BlockSpec((tr, C), grad_map), pl.BlockSpec((tr, C), got_map)],
            out_specs=[
                pl.BlockSpec((tr, C), lambda s, t, pr: (jnp.where(s == 0, t, nt - 1), 0)),
                pl.BlockSpec((tr, C), lambda s, t, pr: (jnp.where(s == 0, 0, (s - 1) * nt + t), 0)),
            ],
        ),
        out_shape=[jax.ShapeDtypeStruct((r, C), F32), jax.ShapeDtypeStruct((3 * r, C), BF16)],
        compiler_params=_params(("arbitrary", "arbitrary"), [((tr, C), F32)] * 4),
    )(place, grad, got)


def _adam_vals(w, g, m, v):
    m = ADAM_B1 * m + (1.0 - ADAM_B1) * g
    v = ADAM_B2 * v + (1.0 - ADAM_B2) * (g * g)
    m_hat = m / (1.0 - ADAM_B1**ADAM_STEP)
    v_hat = v / (1.0 - ADAM_B2**ADAM_STEP)
    delta = -ADAM_LR * (m_hat / (jnp.sqrt(v_hat) + ADAM_EPS) + ADAM_WD * w)
    return delta, m, v


def _adam_big(w3, m3, v3, own, got, layer, transposed, prev, *, name, sched=None):
    _, A, B = w3.shape
    ta = _tile(A, 256, 16)
    nt = A // ta
    b_pad = (-B) % LANES

    def body(*refs):
        w_ref, m_ref, v_ref, own_ref, g1_ref, g2_ref, g3_ref = refs[:7]
        g_ref, d_ref, nm_ref, nv_ref = refs[-4:]
        g = ((own_ref[...] + g1_ref[...].astype(F32)) + g2_ref[...].astype(F32)) + g3_ref[...].astype(F32)
        if transposed:
            if b_pad:
                g = jnp.concatenate([g, jnp.zeros((b_pad, ta), F32)], axis=0)
            g = g.T[:, :B]
        g_ref[...] = g
        d_ref[...], nm_ref[...], nv_ref[...] = _adam_vals(w_ref[...], g, m_ref[...], v_ref[...])

    nat = pl.BlockSpec((None, ta, B), lambda t: (layer, t, 0))
    if transposed:
        parts = [pl.BlockSpec((B, ta), lambda t: (0, t))] + [pl.BlockSpec((B, ta), lambda t, j=j: (j, t)) for j in range(3)]
    else:
        parts = [pl.BlockSpec((ta, B), lambda t: (t, 0))] + [pl.BlockSpec((ta, B), lambda t, j=j: (j * nt + t, 0)) for j in range(3)]
    keep = [] if prev is None else list(prev)
    outs = _hosted(
        body,
        name=name,
        sched=sched,
        grid=(nt,),
        in_specs=[nat, nat, nat] + parts + [pl.BlockSpec(memory_space=pl.ANY)] * len(keep),
        out_specs=[nat] * 4,
        out_shape=[jax.ShapeDtypeStruct(w3.shape, F32)] * 4,
        input_output_aliases={7 + k: k for k in range(len(keep))},
        compiler_params=_params(("arbitrary",), [((ta, B), F32)] * 10),
    )(w3, m3, v3, own, got, got, got, *keep)
    return list(outs)


def _adam_small(w, m, v, parts, *, name, sched=None):
    R = w.shape[0]

    def body(w_ref, m_ref, v_ref, p_ref, g_ref, d_ref, nm_ref, nv_ref):
        g = p_ref[pl.ds(0, R), :]
        for k in range(1, N_DEV):
            g = g + p_ref[pl.ds(k * R, R), :]
        g_ref[...] = g
        d_ref[...], nm_ref[...], nv_ref[...] = _adam_vals(w_ref[...], g, m_ref[...], v_ref[...])

    return _hosted(
        body,
        name=name,
        sched=sched,
        out_shape=[jax.ShapeDtypeStruct((R, LANES), F32)] * 4,
        compiler_params=_params(None, [((R, LANES), F32)] * 16),
    )(w, m, v, parts)


SMALL_NAMES = ("b_gate", "ln_v_g", "ln_v_b", "w_s", "b_s", "sinks", "ln1_g", "ln1_b", "ln2_g", "ln2_b", "ln3_g", "ln3_b")
PACK_ALIGN = 8 * LANES


def _pack(arrays):
    flat = []
    for a in arrays:
        a = a.reshape(-1)
        flat.append(jnp.pad(a, (0, (-a.shape[0]) % PACK_ALIGN)))
    return jnp.concatenate(flat).reshape(-1, LANES)


def _unpack(packed, shapes):
    flat = packed.reshape(-1)
    out, pos = [], 0
    for s in shapes:
        n = 1
        for e in s:
            n *= e
        out.append(flat[pos : pos + n].reshape(s))
        pos += n + (-n) % PACK_ALIGN
    return out


BIG = (("in", "w_in", True), ("a", "w_br_a", True), ("b", "w_br_b", True), ("o", "w_o", False), ("xq", "w_xq", False),
       ("xkv", "w_xkv", False), ("xo", "w_xo", True), ("up", "w_up", True), ("down", "w_down", False))


SMALL_BIG = ("a", "b", "o", "xq", "xkv", "xo")
GATHER_PLAN = (
    (0, ("in",), None, None),
    (0, SMALL_BIG, ("proj_l0",), "gmlp_fwd_l0"),
    (0, ("up",), ("swa_fwd_l0", "merge_l0"), "oproj_ln_l0"),
    (0, ("down",), ("oproj_ln_l0", "xattn_fwd_l0"), "up_l0"),
    (1, ("in",), ("up_l0",), "down_ln_l0"),
    (1, SMALL_BIG, ("down_ln_l0",), "proj_l1"),
    (1, ("up",), ("down_ln_l0", "proj_l1"), "gmlp_fwd_l1"),
    (1, ("down",), ("swa_fwd_l1", "merge_l1"), "oproj_ln_l1"),
)
EARLY_SMALL_BIG = ("o", "xq", "xkv", "xo")
LATE_SMALL_BIG = ("a", "b")
GRAD_HOSTS = {
    1: {"down": ("xattn_bwd_l1", "swa_bwd_l1"), "up": ("xattn_bwd_l1", "proj_dw_l1"),
        **{g: ("gmlp_bwd_l1", "proj_dx_l1") for g in EARLY_SMALL_BIG}, **{g: ("proj_dw_l1", "proj_dx_l1") for g in LATE_SMALL_BIG},
        "in": ("proj_dx_l1", "down_dx_l0")},
    0: {"down": ("xattn_bwd_l0", ("oproj_dx_l0", "gmlp_bwd_l0")), "up": ("xattn_bwd_l0", "swa_bwd_l0"),
        **{g: ("gmlp_bwd_l0", "proj_dw_l0") for g in EARLY_SMALL_BIG}, **{g: ("proj_dw_l0", "proj_dx_l0") for g in LATE_SMALL_BIG},
        "in": (None, "proj_dx_l0")},
}
SMALL_GRAD_HOSTS = ("proj_dw_l0", "proj_dx_l0")


def kernel(x, mem, w_in, b_gate, ln_v_g, ln_v_b, w_s, b_s, sinks, w_br_a, w_br_b, w_o, ln1_g, ln1_b, w_xq, w_xkv, w_xo, ln2_g, ln2_b, w_up, w_down, ln3_g, ln3_b, loss_target, m_w_in, m_b_gate, m_ln_v_g, m_ln_v_b, m_w_s, m_b_s, m_sinks, m_w_br_a, m_w_br_b, m_w_o, m_ln1_g, m_ln1_b, m_w_xq, m_w_xkv, m_w_xo, m_ln2_g, m_ln2_b, m_w_up, m_w_down, m_ln3_g, m_ln3_b, v_w_in, v_b_gate, v_ln_v_g, v_ln_v_b, v_w_s, v_b_s, v_sinks, v_w_br_a, v_w_br_b, v_w_o, v_ln1_g, v_ln1_b, v_w_xq, v_w_xkv, v_w_xo, v_ln2_g, v_ln2_b, v_w_up, v_w_down, v_ln3_g, v_ln3_b):
    given = dict(locals())
    T, D = x.shape[1], x.shape[2]
    IN, GW, AW, XW, DFF = w_in.shape[2] * N_DEV, ln_v_g.shape[1], w_br_b.shape[1], w_xq.shape[2], w_up.shape[2] * N_DEV
    KVW = (IN - 2 * GW - AW - 2 * D) // 2
    d = Dims(T=T, D=D, IN=IN, GW=GW, G=GW // GROUP_DIM, AW=AW, KVW=KVW, HQ=AW // HEAD_DIM, HKV=KVW // HEAD_DIM, XW=XW,
             XH=XW // X_HEAD_DIM, MEM=mem.shape[1], DFF=DFF)
    place = jnp.stack([lax.axis_index("x"), lax.axis_index("y"), lax.axis_index("c")]).astype(jnp.int32)

    sched = _Sched()
    big_of = {g: (p, tr) for g, p, tr in BIG}

    def shard(l, g):
        p, tr = big_of[g]
        return (given[p][l].T if tr else given[p][l]).astype(BF16)

    Gs = [{}, {}]

    def plan_gather(l, names, hosts1, host2):
        shards = [shard(l, g) for g in names]
        if hosts1 is None:
            Gs[l].update(zip(names, _all_gather(shards, name=f"gather_{names[0]}_l{l}")))
            return

        def register(k, into):
            def done(outs):
                if k + 1 < len(hosts1):
                    register(k + 1, outs)
                else:
                    sched.at(host2, _gather_stage2(outs, lambda full: Gs[l].update(zip(names, full))))

            sched.at(hosts1[k], _gather_stage1(shards, done, part=(k, len(hosts1)), into=into))

        register(0, None)

    for entry in GATHER_PLAN:
        plan_gather(*entry)
    Ss = []
    for l in range(DEPTH):
        S = {n: given[n][l].reshape(1, -1) for n in SMALL_NAMES if n not in ("w_s", "b_s", "sinks")}
        S["w_s"], S["b_sT"], S["sinks"] = w_s[l], b_s[l].T, sinks[l]
        Ss.append(S)

    owns, recvs = {}, {}

    smalls_seen = {}
    gathered_small = []

    def pack_small(src):
        parts = []
        for l in range(DEPTH):
            for n in SMALL_NAMES:
                a = src[l]["b_sT"].T if n == "b_s" else src[l][n]
                parts.append(a[0, : d.HQ] if n == "sinks" else a)
        return _pack(parts)

    def on_small(l, small):
        smalls_seen[l] = small
        if len(smalls_seen) == DEPTH:
            host1, host2 = SMALL_GRAD_HOSTS
            sched.at(host1, _gather_stage1([pack_small(smalls_seen)], lambda part: sched.at(
                host2, _gather_stage2(part, lambda full: gathered_small.append(full[0])))))

    def on_grad(l, g, grad):
        if g == "small":
            return on_small(l, grad)
        sib_host, chips_host = GRAD_HOSTS[l][g]

        def after_sibling(got):
            owns[(l, g)], send = _pair_sum(grad, got[0], place, name=f"grad_pair_sum_{g}_l{l}")
            hosts = chips_host if isinstance(chips_host, tuple) else (chips_host,)

            def register(k, into):
                def done(r):
                    if k + 1 < len(hosts):
                        register(k + 1, r)
                    else:
                        recvs[(l, g)] = r[0]

                task = _chips_stage([send], done, part=(k, len(hosts)), into=into)
                if hosts[k] is None:
                    _comm_only([task], name=f"grad_chips_{g}_l{l}")
                else:
                    sched.at(hosts[k], task)

            register(0, None)

        task = _sibling_stage([grad], after_sibling)
        if sib_host is None:
            _comm_only([task], name=f"grad_sibling_{g}_l{l}")
        else:
            sched.at(sib_host, task)

    loss, dX, bigs, smalls = _local_step(d, x[0], mem[0], loss_target[0], Gs, Ss, sched, on_grad)
    loss = lax.psum(loss[0, 0], ("x", "y", "c"))

    assert not sched.pending, sorted(sched.pending)

    def viewed(p, tr):
        return tr and given[p].shape[2] % LANES != 0

    res = {p: None for _, p, _ in BIG}
    for l in reversed(range(DEPTH)):
        for g, p, tr in BIG:
            view = viewed(p, tr)
            w3, m3, v3 = ((jnp.swapaxes(a, 1, 2) if view else a) for a in (given[p], given["m_" + p], given["v_" + p]))
            res[p] = _adam_big(w3, m3, v3, owns[(l, g)], recvs[(l, g)], l, tr and not view, res[p], name=f"adamw_{g}_l{l}")
    for g, p, tr in BIG:
        if viewed(p, tr):
            res[p] = [jnp.swapaxes(a, 1, 2) for a in res[p]]

    shapes = [given[n].shape[1:] for n in SMALL_NAMES]
    pw, pm, pv = (_pack([given[pre + n][l] for l in range(DEPTH) for n in SMALL_NAMES]) for pre in ("", "m_", "v_"))
    small_out = [_unpack(o, shapes * DEPTH) for o in _adam_small(pw, pm, pv, gathered_small[0], name="adamw_small")]

    names = ["w_in", "b_gate", "ln_v_g", "ln_v_b", "w_s", "b_s", "sinks", "w_br_a", "w_br_b", "w_o", "ln1_g", "ln1_b", "w_xq",
             "w_xkv", "w_xo", "ln2_g", "ln2_b", "w_up", "w_down", "ln3_g", "ln3_b"]
    out = [loss, dX[None]]
    for kind in range(4):
        for n in names:
            if n in SMALL_NAMES:
                i = SMALL_NAMES.index(n)
                out.append(jnp.stack([small_out[kind][l * len(SMALL_NAMES) + i] for l in range(DEPTH)]))
            else:
                out.append(res[n][kind])
    return tuple(out)
```

```python
import collections
import functools

import jax
import jax.numpy as jnp
from jax import lax
from jax.experimental import pallas as pl
from jax.experimental.pallas import tpu as pltpu

F32 = jnp.float32
BF16 = jnp.bfloat16
MESH = pl.DeviceIdType.MESH

N_DEV = 8
DEPTH = 2
CHUNK = 128
HEAD_DIM = 64
ROPE_HALF = HEAD_DIM // 2
X_HEAD_DIM = 128
GROUP_DIM = 128
LANES = 128
ROPE_THETA = 10000.0
LN_EPS = 1e-5
ALPHA = (2 * DEPTH) ** 0.25
ADAM_LR = 0.001
ADAM_B1 = 0.9
ADAM_B2 = 0.999
ADAM_EPS = 1e-08
ADAM_WD = 0.01
ADAM_STEP = 10
VMEM_BYTES = 64 * 1024 * 1024
VMEM_TEMP_BYTES = 20 * 1024 * 1024

Dims = collections.namedtuple("Dims", "T D IN GW G AW KVW HQ HKV XW XH MEM DFF")


def _offsets(d):
    off_v = d.GW
    off_q = 2 * d.GW
    off_k = off_q + d.AW
    off_va = off_k + d.KVW
    off_ga = off_va + d.KVW
    off_gb = off_ga + d.D
    return off_v, off_q, off_k, off_va, off_ga, off_gb


def _tile(dim, pref, align=LANES):
    if dim <= pref:
        return dim
    t = pref - pref % align
    while t >= align:
        if dim % t == 0:
            return t
        t -= align
    return dim


def _nbytes(shape, dtype):
    n = 1
    for s in shape:
        n *= s
    return n * jnp.dtype(dtype).itemsize


def _params(sem, blocks, scratch=0):
    need = 2 * sum(_nbytes(s, t) for s, t in blocks) + scratch + VMEM_TEMP_BYTES
    limit = min(max(need, 32 * 1024 * 1024), VMEM_BYTES - 4 * 1024 * 1024)
    return pltpu.CompilerParams(dimension_semantics=sem, vmem_limit_bytes=limit)


def _dot(a, b, kind):
    dn = {"nn": (((1,), (0,)), ((), ())), "nt": (((1,), (1,)), ((), ())), "tn": (((0,), (0,)), ((), ()))}[kind]
    return lax.dot_general(a, b, dn, preferred_element_type=F32)


def _gelu(x):
    c = 0.7978845608028654
    t = jnp.tanh(c * (x + 0.044715 * (x * x * x)))
    return 0.5 * x * (1.0 + t)


def _gelu_grad(x):
    c = 0.7978845608028654
    x2 = x * x
    t = jnp.tanh(c * (x + 0.044715 * (x2 * x)))
    return 0.5 * (1.0 + t) + 0.5 * x * (1.0 - t * t) * (c * (1.0 + 3.0 * 0.044715 * x2))


def _ln_fwd(z, g, b):
    mu = jnp.mean(z, axis=-1, keepdims=True)
    zc = z - mu
    var = jnp.mean(zc * zc, axis=-1, keepdims=True)
    rstd = lax.rsqrt(var + LN_EPS)
    xhat = zc * rstd
    return xhat * g + b, xhat, rstd


def _ln_bwd_vals(dy, xhat, rstd, g):
    gd = dy * g
    m1 = jnp.mean(gd, axis=-1, keepdims=True)
    m2 = jnp.mean(gd * xhat, axis=-1, keepdims=True)
    return rstd * (gd - m1 - xhat * m2)


def _acc_store(ref, val, first):
    @pl.when(first)
    def _():
        ref[...] = val

    @pl.when(jnp.logical_not(first))
    def _():
        ref[...] += val


def _matmul(a, b, kind, *, name, tm=512, tn=1024, tk=2048, out_defs=(("mn", F32),), epilogue=None, extras=(), sched=None):
    if kind == "nn":
        (M, K), (K2, N) = a.shape, b.shape
    elif kind == "nt":
        (M, K), (N, K2) = a.shape, b.shape
    else:
        (K, M), (K2, N) = a.shape, b.shape
    assert K == K2, (a.shape, b.shape, kind)
    tm, tn, tk = _tile(M, tm), _tile(N, tn), _tile(K, tk)
    nk = K // tk
    blocks = []

    def spec(shape, imap, dtype):
        blocks.append((shape, dtype))
        return pl.BlockSpec(shape, imap)

    if kind == "tn":
        a_spec = spec((tk, tm), lambda j, i, k: (k, i), a.dtype)
    else:
        a_spec = spec((tm, tk), lambda j, i, k: (i, k), a.dtype)
    if kind == "nt":
        b_spec = spec((tn, tk), lambda j, i, k: (j, k), b.dtype)
    else:
        b_spec = spec((tk, tn), lambda j, i, k: (k, j), b.dtype)
    in_specs = [a_spec, b_spec]
    operands = [a, b]
    for arr, ekind, off in extras:
        if ekind == "mn":
            assert off % tn == 0
            in_specs.append(spec((tm, tn), lambda j, i, k, o=off // tn: (i, j + o), arr.dtype))
        elif ekind == "row":
            assert off % tn == 0
            in_specs.append(spec((1, tn), lambda j, i, k, o=off // tn: (0, j + o), arr.dtype))
        else:
            in_specs.append(spec((tm, 1), lambda j, i, k: (i, 0), arr.dtype))
        operands.append(arr)
    out_shape, out_specs = [], []
    for od in out_defs:
        okind, dt = od[0], od[1]
        if okind == "mn":
            out_shape.append(jax.ShapeDtypeStruct((M, N), dt))
            out_specs.append(spec((tm, tn), lambda j, i, k: (i, j), dt))
        elif okind == "cols":
            assert od[3] % tn == 0
            out_shape.append(jax.ShapeDtypeStruct((M, od[2]), dt))
            out_specs.append(spec((tm, tn), lambda j, i, k, o=od[3] // tn: (i, j + o), dt))
        elif okind == "m1":
            assert N == tn
            out_shape.append(jax.ShapeDtypeStruct((M, 1), dt))
            out_specs.append(spec((tm, 1), lambda j, i, k: (i, 0), dt))
        else:
            out_shape.append(jax.ShapeDtypeStruct((1, N), F32))
            out_specs.append(spec((1, tn), lambda j, i, k: (0, j), F32))
    n_ex, n_out = len(extras), len(out_defs)
    ep = epilogue if epilogue is not None else (lambda acc: (acc,))
    in_place = nk > 1 and epilogue is None and tuple(out_defs) == (("mn", F32),)

    def body(*refs):
        a_ref, b_ref = refs[0], refs[1]
        ex_refs = refs[2 : 2 + n_ex]
        out_refs = refs[2 + n_ex : 2 + n_ex + n_out]
        i = pl.program_id(1)
        k = pl.program_id(2)

        def product():
            return _dot(a_ref[...], b_ref[...], kind)

        def finish(acc):
            vals = ep(acc, *[r[...] for r in ex_refs])
            for od, r, v in zip(out_defs, out_refs, vals):
                if od[0] == "acc":
                    _acc_store(r, v, i == 0)
                else:
                    r[...] = v.astype(od[1])

        if nk == 1:
            finish(product())
            return
        acc_ref = out_refs[0] if in_place else refs[-1]

        @pl.when(k == 0)
        def _():
            acc_ref[...] = product()

        if in_place:
            @pl.when(k > 0)
            def _():
                acc_ref[...] += product()
        else:
            if nk > 2:
                @pl.when(jnp.logical_and(k > 0, k < nk - 1))
                def _():
                    acc_ref[...] += product()

            @pl.when(k == nk - 1)
            def _():
                finish(acc_ref[...] + product())

    scratch = [pltpu.VMEM((tm, tn), F32)] if nk > 1 and not in_place else []
    outs = _hosted(
        body,
        name=name,
        sched=sched,
        grid=(N // tn, M // tm, nk),
        in_specs=in_specs,
        out_specs=out_specs,
        out_shape=out_shape,
        scratch_shapes=scratch,
        compiler_params=_params(("arbitrary", "arbitrary", "arbitrary"), blocks, _nbytes((tm, tn), F32) if scratch else 0),
    )(*operands)
    return outs if len(outs) > 1 else outs[0]


def _ep_resid_ln(acc, resid, g, b):
    y, xhat, rstd = _ln_fwd(ALPHA * resid + acc, g, b)
    return y, y, xhat, rstd


def _ep_resid(acc, resid):
    return (ALPHA * resid + acc,)


def _ep_resid_ln_bwd(acc, resid, xhat, rstd, g):
    dy = ALPHA * resid + acc
    dz = _ln_bwd_vals(dy, xhat, rstd, g)
    return dz, dz, jnp.sum(dy * xhat, axis=0, keepdims=True), jnp.sum(dy, axis=0, keepdims=True)


def _ep_relu2(acc):
    r = jnp.maximum(acc, 0.0)
    return acc, r * r


def _ep_relu2_bwd(acc, h):
    return (acc * (2.0 * jnp.maximum(h, 0.0)),)


def _ep_gate_bwd(acc, ga, gb, ya, yb, bga, bgb):
    sa = jax.nn.sigmoid(ga + bga)
    sb = jax.nn.sigmoid(gb + bgb)
    dga = acc * ya * (sa * (1.0 - sa))
    dgb = acc * yb * (sb * (1.0 - sb))
    return (acc * sa, acc * sb, dga, dgb, jnp.sum(dga, axis=0, keepdims=True), jnp.sum(dgb, axis=0, keepdims=True))


def _ln_bwd(dy, xhat, rstd, g, *, name, sched=None, target=None):
    T, D = dy.shape
    tm = _tile(T, 256)
    head = target is not None

    def body(*refs):
        dy_ref, xh_ref, rs_ref, g_ref = refs[:4]
        dz_ref, dzb_ref, dg_ref, db_ref = refs[4 + head : 8 + head]
        first = pl.program_id(0) == 0
        dyv, xh = dy_ref[...], xh_ref[...]
        if head:
            err = dyv - refs[4][...]
            dyv = err * (1.0 / D)
            part = 0.5 * jnp.sum(jnp.sum(err * err, axis=1, keepdims=True) * (1.0 / D), axis=0, keepdims=True)
            _acc_store(refs[-1], part, first)
        dz = _ln_bwd_vals(dyv, xh, rs_ref[...], g_ref[...])
        dz_ref[...] = dz
        dzb_ref[...] = dz.astype(BF16)
        _acc_store(dg_ref, jnp.sum(dyv * xh, axis=0, keepdims=True), first)
        _acc_store(db_ref, jnp.sum(dyv, axis=0, keepdims=True), first)

    row = pl.BlockSpec((tm, D), lambda i: (i, 0))
    vec = pl.BlockSpec((1, D), lambda i: (0, 0))
    one = pl.BlockSpec((1, 1), lambda i: (0, 0))
    return _hosted(
        body,
        name=name,
        sched=sched,
        grid=(T // tm,),
        in_specs=[row, row, pl.BlockSpec((tm, 1), lambda i: (i, 0)), vec] + [row] * head,
        out_specs=[row, row, vec, vec] + [one] * head,
        out_shape=[
            jax.ShapeDtypeStruct((T, D), F32),
            jax.ShapeDtypeStruct((T, D), BF16),
            jax.ShapeDtypeStruct((1, D), F32),
            jax.ShapeDtypeStruct((1, D), F32),
        ] + [jax.ShapeDtypeStruct((1, 1), F32)] * head,
        compiler_params=_params(("arbitrary",), [((tm, D), F32)] * (4 + head)),
    )(*([dy, xhat, rstd, g] + [target] * head))


def _masked_mix_weights(wm_ref, G):
    r = lax.broadcasted_iota(jnp.int32, (CHUNK, CHUNK), 0)
    c = lax.broadcasted_iota(jnp.int32, (CHUNK, CHUNK), 1)
    return [jnp.where(c <= r, wm_ref[g], 0.0).astype(BF16) for g in range(G)]


def _gmlp_fwd(d, P, lnv_g, lnv_b, wm, bsT, *, name, sched=None):
    T, GW, G = d.T, d.GW, d.G
    tm = _tile(T, 256)
    nc = tm // CHUNK

    def body(u_ref, v_ref, g_ref, b_ref, wm_ref, bs_ref, o_ref):
        gu = _gelu(u_ref[...])
        vn, _, _ = _ln_fwd(_gelu(v_ref[...]), g_ref[...], b_ref[...])
        vn = vn.astype(BF16)
        wl = _masked_mix_weights(wm_ref, G)
        for c in range(nc):
            rows = slice(c * CHUNK, (c + 1) * CHUNK)
            for g in range(G):
                cols = slice(g * GROUP_DIM, (g + 1) * GROUP_DIM)
                mixed = _dot(wl[g], vn[rows, cols], "nn") + bs_ref[:, g : g + 1]
                o_ref[rows, cols] = (gu[rows, cols] * mixed).astype(BF16)

    return _hosted(
        body,
        name=name,
        sched=sched,
        grid=(T // tm,),
        in_specs=[
            pl.BlockSpec((tm, GW), lambda i: (i, 0)),
            pl.BlockSpec((tm, GW), lambda i: (i, 1)),
            pl.BlockSpec((1, GW), lambda i: (0, 0)),
            pl.BlockSpec((1, GW), lambda i: (0, 0)),
            pl.BlockSpec((G, CHUNK, CHUNK), lambda i: (0, 0, 0)),
            pl.BlockSpec((CHUNK, G), lambda i: (0, 0)),
        ],
        out_specs=pl.BlockSpec((tm, GW), lambda i: (i, 0)),
        out_shape=jax.ShapeDtypeStruct((T, GW), BF16),
        compiler_params=_params(("arbitrary",), [((tm, GW), F32)] * 3),
    )(P, P, lnv_g, lnv_b, wm, bsT)


def _gmlp_bwd(d, P, dya_b, Ga, lnv_g, lnv_b, wm, bsT, dP, *, name, sched=None):
    T, D, GW, G = d.T, d.D, d.GW, d.G
    tm = _tile(T, 256)
    nc = tm // CHUNK

    def body(u_ref, v_ref, dya_ref, ga_ref, g_ref, b_ref, wm_ref, bs_ref, dp_in, duv_ref, dwm_ref, dbs_ref, dlg_ref, dlb_ref, dgu_s, dvn_s):
        first = pl.program_id(0) == 0
        dsgu = _dot(dya_ref[...], ga_ref[...], "nn")
        u, v = u_ref[...], v_ref[...]
        gu = _gelu(u)
        lng = g_ref[...]
        vn, xh, rstd = _ln_fwd(_gelu(v), lng, b_ref[...])
        vn = vn.astype(BF16)
        wl = _masked_mix_weights(wm_ref, G)
        r = lax.broadcasted_iota(jnp.int32, (CHUNK, CHUNK), 0)
        cc = lax.broadcasted_iota(jnp.int32, (CHUNK, CHUNK), 1)
        lane_g = lax.broadcasted_iota(jnp.int32, (CHUNK, G), 1)
        dbs = jnp.zeros((CHUNK, G), F32)
        for g in range(G):
            cols = slice(g * GROUP_DIM, (g + 1) * GROUP_DIM)
            dw = jnp.zeros((CHUNK, CHUNK), F32)
            for c in range(nc):
                rows = slice(c * CHUNK, (c + 1) * CHUNK)
                mixed = _dot(wl[g], vn[rows, cols], "nn") + bs_ref[:, g : g + 1]
                ds = dsgu[rows, cols]
                dgu_s[rows, cols] = ds * mixed
                dmx = ds * gu[rows, cols]
                dbs = dbs + jnp.where(lane_g == g, jnp.sum(dmx, axis=1, keepdims=True), 0.0)
                dmx = dmx.astype(BF16)
                dw = dw + _dot(dmx, vn[rows, cols], "nt")
                dvn_s[rows, cols] = _dot(wl[g], dmx, "tn")
            _acc_store(dwm_ref.at[g], jnp.where(cc <= r, dw, 0.0), first)
        _acc_store(dbs_ref, dbs, first)
        dvn = dvn_s[...]
        _acc_store(dlg_ref, jnp.sum(dvn * xh, axis=0, keepdims=True), first)
        _acc_store(dlb_ref, jnp.sum(dvn, axis=0, keepdims=True), first)
        dgv = _ln_bwd_vals(dvn, xh, rstd, lng)
        duv_ref[:, :GW] = (dgu_s[...] * _gelu_grad(u)).astype(BF16)
        duv_ref[:, GW:] = (dgv * _gelu_grad(v)).astype(BF16)

    return _hosted(
        body,
        name=name,
        sched=sched,
        grid=(T // tm,),
        in_specs=[
            pl.BlockSpec((tm, GW), lambda i: (i, 0)),
            pl.BlockSpec((tm, GW), lambda i: (i, 1)),
            pl.BlockSpec((tm, D), lambda i: (i, 0)),
            pl.BlockSpec((D, GW), lambda i: (0, 0)),
            pl.BlockSpec((1, GW), lambda i: (0, 0)),
            pl.BlockSpec((1, GW), lambda i: (0, 0)),
            pl.BlockSpec((G, CHUNK, CHUNK), lambda i: (0, 0, 0)),
            pl.BlockSpec((CHUNK, G), lambda i: (0, 0)),
            pl.BlockSpec(memory_space=pl.ANY),
        ],
        out_specs=[
            pl.BlockSpec((tm, 2 * GW), lambda i: (i, 0)),
            pl.BlockSpec((G, CHUNK, CHUNK), lambda i: (0, 0, 0)),
            pl.BlockSpec((CHUNK, G), lambda i: (0, 0)),
            pl.BlockSpec((1, GW), lambda i: (0, 0)),
            pl.BlockSpec((1, GW), lambda i: (0, 0)),
        ],
        out_shape=[
            jax.ShapeDtypeStruct(dP.shape, BF16),
            jax.ShapeDtypeStruct((G, CHUNK, CHUNK), F32),
            jax.ShapeDtypeStruct((CHUNK, G), F32),
            jax.ShapeDtypeStruct((1, GW), F32),
            jax.ShapeDtypeStruct((1, GW), F32),
        ],
        scratch_shapes=[pltpu.VMEM((tm, GW), F32), pltpu.VMEM((tm, GW), F32)],
        input_output_aliases={8: 0},
        compiler_params=_params(
            ("arbitrary",), [((tm, GW), F32)] * 3 + [((tm, D), BF16), ((D, GW), BF16)], 2 * _nbytes((tm, GW), F32)
        ),
    )(P, P, dya_b, Ga, lnv_g, lnv_b, wm, bsT, dP)


def _swap_halves(x):
    w = x.shape[1]
    lane = lax.broadcasted_iota(jnp.int32, x.shape, 1)
    return jnp.where((lane % HEAD_DIM) < ROPE_HALF, pltpu.roll(x, w - ROPE_HALF, 1), pltpu.roll(x, ROPE_HALF, 1))


def _lane_tile(t, width):
    reps = width // LANES
    return t if reps == 1 else jnp.tile(t, (1, reps))


def _rope(x, cos2, sin2):
    w = x.shape[1]
    return x * _lane_tile(cos2, w) + _swap_halves(x) * _lane_tile(sin2, w)


def _rope_bwd(g, cos2, sin2):
    w = g.shape[1]
    return g * _lane_tile(cos2, w) - _swap_halves(g) * _lane_tile(sin2, w)


PAIR = LANES // HEAD_DIM


def _swa_valid(i):
    s = lax.broadcasted_iota(jnp.int32, (2 * CHUNK, CHUNK), 0)
    t = lax.broadcasted_iota(jnp.int32, (2 * CHUNK, CHUNK), 1)
    cur = jnp.logical_and(s >= CHUNK, s - CHUNK <= t)
    prev = jnp.logical_and(jnp.logical_and(s < CHUNK, s > t), i > 0)
    return jnp.logical_or(cur, prev)


def _half_variants(x, odd):
    lane = lax.broadcasted_iota(jnp.int32, x.shape, 1)
    if odd:
        hi = jnp.where(lane >= HEAD_DIM, x, jnp.zeros_like(x))
        return pltpu.roll(hi, HEAD_DIM, 1), hi
    lo = jnp.where(lane < HEAD_DIM, x, jnp.zeros_like(x))
    return lo, pltpu.roll(lo, HEAD_DIM, 1)


def _swa_probs_t(kx, qp, sink, valid):
    s = jnp.where(valid, _dot(kx, qp, "nt") * (HEAD_DIM**-0.5), -jnp.inf)
    m = jnp.maximum(jnp.max(s, axis=0, keepdims=True), sink)
    e = jnp.exp(s - m)
    e_s = jnp.exp(sink - m)
    den = jnp.sum(e, axis=0, keepdims=True) + e_s
    return e / den, e_s / den


def _swa_load(q_ref, kc_ref, kp_ref, vc_ref, vp_ref, cc, sc, cp, sp):
    qr = _rope(q_ref[...], cc, sc).astype(BF16)
    kcat = jnp.concatenate([_rope(kp_ref[...], cp, sp), _rope(kc_ref[...], cc, sc)], axis=0)
    vcat = jnp.concatenate([vp_ref[...], vc_ref[...]], axis=0)
    return qr, kcat, vcat


def _swa_specs(d):
    _, off_q, off_k, off_va, _, _ = _offsets(d)
    assert off_q % d.AW == 0 and off_k % d.KVW == 0 and off_va % d.KVW == 0
    prev = lambda i: jnp.maximum(i - 1, 0)
    return [
        pl.BlockSpec(memory_space=pltpu.SMEM),
        pl.BlockSpec((CHUNK, d.AW), lambda i, o=off_q // d.AW: (i, o)),
        pl.BlockSpec((CHUNK, d.KVW), lambda i, o=off_k // d.KVW: (i, o)),
        pl.BlockSpec((CHUNK, d.KVW), lambda i, o=off_k // d.KVW: (prev(i), o)),
        pl.BlockSpec((CHUNK, d.KVW), lambda i, o=off_va // d.KVW: (i, o)),
        pl.BlockSpec((CHUNK, d.KVW), lambda i, o=off_va // d.KVW: (prev(i), o)),
        pl.BlockSpec((CHUNK, LANES), lambda i: (i, 0)),
        pl.BlockSpec((CHUNK, LANES), lambda i: (i, 0)),
        pl.BlockSpec((CHUNK, LANES), lambda i: (prev(i), 0)),
        pl.BlockSpec((CHUNK, LANES), lambda i: (prev(i), 0)),
    ]


def _swa_fwd(d, P, cos2, sin2, sinks, *, name, sched=None):
    T, AW, HQ, HKV = d.T, d.AW, d.HQ, d.HKV
    grp = HQ // HKV
    assert grp % PAIR == 0 and HKV % PAIR == 0

    def body(sink_ref, q_ref, kc_ref, kp_ref, vc_ref, vp_ref, cc_ref, sc_ref, cp_ref, sp_ref, o_ref):
        i = pl.program_id(0)
        qr, kcat, vcat = _swa_load(q_ref, kc_ref, kp_ref, vc_ref, vp_ref, cc_ref[...], sc_ref[...], cp_ref[...], sp_ref[...])
        valid = _swa_valid(i)
        for kv in range(HKV):
            col = slice((kv // PAIR) * LANES, (kv // PAIR + 1) * LANES)
            kx = [t.astype(BF16) for t in _half_variants(kcat[:, col], kv % PAIR)]
            vx = [t.astype(BF16) for t in _half_variants(vcat[:, col], kv % PAIR)]
            for pp in range(grp // PAIR):
                p = kv * (grp // PAIR) + pp
                qs = slice(p * LANES, (p + 1) * LANES)
                out = jnp.zeros((CHUNK, LANES), F32)
                for half in range(PAIR):
                    pt, _ = _swa_probs_t(kx[half], qr[:, qs], sink_ref[PAIR * p + half], valid)
                    out = out + _dot(pt.astype(BF16), vx[half], "tn")
                o_ref[:, qs] = out.astype(BF16)

    return _hosted(
        body,
        name=name,
        sched=sched,
        grid=(T // CHUNK,),
        in_specs=_swa_specs(d),
        out_specs=pl.BlockSpec((CHUNK, AW), lambda i: (i, 0)),
        out_shape=jax.ShapeDtypeStruct((T, AW), BF16),
        compiler_params=_params(("arbitrary",), [((CHUNK, AW), F32)] * 3),
    )(sinks, P, P, P, P, P, cos2, sin2, cos2, sin2)


def _swa_bwd(d, P, dyb_b, Gb, cos2, sin2, sinks, dP, *, name, sched=None):
    T, D, AW, KVW, HQ, HKV = d.T, d.D, d.AW, d.KVW, d.HQ, d.HKV
    grp = HQ // HKV
    nb = T // CHUNK
    scale = HEAD_DIM**-0.5
    off_q = _offsets(d)[1]
    assert grp % PAIR == 0 and HKV % PAIR == 0 and off_q % AW == 0

    def body(sink_ref, q_ref, kc_ref, kp_ref, vc_ref, vp_ref, cc_ref, sc_ref, cp_ref, sp_ref, dyb_ref, gb_ref, dp_in,
             dq_ref, dkv_ref, dsk_ref, acc_s, dq_s, dk_s, dv_s):
        i = pl.program_id(0)
        cc, sc, cp, sp = cc_ref[...], sc_ref[...], cp_ref[...], sp_ref[...]
        datt = _dot(dyb_ref[...], gb_ref[...], "nn").astype(BF16)
        qr, kcat, vcat = _swa_load(q_ref, kc_ref, kp_ref, vc_ref, vp_ref, cc, sc, cp, sp)
        valid = _swa_valid(i)
        lane1 = lax.broadcasted_iota(jnp.int32, (1, LANES), 1)
        lane2 = lax.broadcasted_iota(jnp.int32, (2 * CHUNK, LANES), 1)
        dsk = jnp.zeros((1, LANES), F32)
        for kvp in range(HKV // PAIR):
            col = slice(kvp * LANES, (kvp + 1) * LANES)
            dk_col = jnp.zeros((2 * CHUNK, LANES), F32)
            dv_col = jnp.zeros((2 * CHUNK, LANES), F32)
            for odd in range(PAIR):
                kv = kvp * PAIR + odd
                kx = [t.astype(BF16) for t in _half_variants(kcat[:, col], odd)]
                vx = [t.astype(BF16) for t in _half_variants(vcat[:, col], odd)]
                dk_half = [jnp.zeros((2 * CHUNK, LANES), F32) for _ in range(PAIR)]
                dv_half = [jnp.zeros((2 * CHUNK, LANES), F32) for _ in range(PAIR)]
                for pp in range(grp // PAIR):
                    p = kv * (grp // PAIR) + pp
                    qs = slice(p * LANES, (p + 1) * LANES)
                    qp, dop = qr[:, qs], datt[:, qs]
                    dq = jnp.zeros((CHUNK, LANES), F32)
                    for half in range(PAIR):
                        h = PAIR * p + half
                        pt, p_s = _swa_probs_t(kx[half], qp, sink_ref[h], valid)
                        dpt = _dot(vx[half], dop, "nt")
                        rs = jnp.sum(pt * dpt, axis=0, keepdims=True)
                        dst = (pt * (dpt - rs) * scale).astype(BF16)
                        dsk = dsk + jnp.where(lane1 == h, -jnp.sum(p_s * rs, axis=1, keepdims=True), 0.0)
                        dq = dq + _dot(dst, kx[half], "tn")
                        dk_half[half] = dk_half[half] + _dot(dst, qp, "nn")
                        dv_half[half] = dv_half[half] + _dot(pt.astype(BF16), dop, "nn")
                    dq_s[:, qs] = dq
                for acc_half, is_k in ((dk_half, True), (dv_half, False)):
                    lo = jnp.where(lane2 < HEAD_DIM, acc_half[0], 0.0)
                    hi = jnp.where(lane2 >= HEAD_DIM, acc_half[1], 0.0)
                    both = (pltpu.roll(lo, HEAD_DIM, 1) + hi) if odd else (lo + pltpu.roll(hi, HEAD_DIM, 1))
                    if is_k:
                        dk_col = dk_col + both
                    else:
                        dv_col = dv_col + both
            dk_s[:, col] = dk_col
            dv_s[:, col] = dv_col
        dq_ref[...] = _rope_bwd(dq_s[...], cc, sc).astype(BF16)
        cur = pl.ds(pl.multiple_of(i * CHUNK, CHUNK), CHUNK)
        acc_s[cur, :KVW] = _rope_bwd(dk_s[CHUNK:, :], cc, sc)
        acc_s[cur, KVW:] = dv_s[CHUNK:, :]

        @pl.when(i > 0)
        def _():
            prv = pl.ds(pl.multiple_of((i - 1) * CHUNK, CHUNK), CHUNK)
            acc_s[prv, :KVW] += _rope_bwd(dk_s[:CHUNK, :], cp, sp)
            acc_s[prv, KVW:] += dv_s[:CHUNK, :]

        _acc_store(dsk_ref, dsk, i == 0)

        @pl.when(i == nb - 1)
        def _():
            dkv_ref[...] = acc_s[...].astype(BF16)

    return _hosted(
        body,
        name=name,
        sched=sched,
        grid=(nb,),
        in_specs=_swa_specs(d) + [pl.BlockSpec((CHUNK, D), lambda i: (i, 0)), pl.BlockSpec((D, AW), lambda i: (0, 0)),
                                   pl.BlockSpec(memory_space=pl.ANY)],
        out_specs=[
            pl.BlockSpec((CHUNK, AW), lambda i, o=off_q // AW: (i, o)),
            pl.BlockSpec((T, 2 * KVW), lambda i: (0, 0)),
            pl.BlockSpec((1, LANES), lambda i: (0, 0)),
        ],
        out_shape=[
            jax.ShapeDtypeStruct(dP.shape, BF16),
            jax.ShapeDtypeStruct((T, 2 * KVW), BF16),
            jax.ShapeDtypeStruct((1, LANES), F32),
        ],
        input_output_aliases={12: 0},
        scratch_shapes=[
            pltpu.VMEM((T, 2 * KVW), F32),
            pltpu.VMEM((CHUNK, AW), F32),
            pltpu.VMEM((2 * CHUNK, KVW), F32),
            pltpu.VMEM((2 * CHUNK, KVW), F32),
        ],
        compiler_params=_params(
            ("arbitrary",), [((CHUNK, AW), F32)] * 3 + [((D, AW), BF16), ((T, 2 * KVW), BF16)], _nbytes((T, 2 * KVW), F32)
        ),
    )(sinks, P, P, P, P, P, cos2, sin2, cos2, sin2, dyb_b, Gb, dP)


def _place_cols(dst, src, off, *, name, sched=None):
    T, w = src.shape
    tm, tw = _tile(T, 512), _tile(w, 512)
    assert off % tw == 0

    def body(src_ref, dst_in, out_ref):
        out_ref[...] = src_ref[...]

    return _hosted(
        body,
        name=name,
        sched=sched,
        grid=(T // tm, w // tw),
        in_specs=[pl.BlockSpec((tm, tw), lambda i, j: (i, j)), pl.BlockSpec(memory_space=pl.ANY)],
        out_specs=pl.BlockSpec((tm, tw), lambda i, j, o=off // tw: (i, j + o)),
        out_shape=jax.ShapeDtypeStruct(dst.shape, dst.dtype),
        input_output_aliases={1: 0},
        compiler_params=_params(("arbitrary", "arbitrary"), [((tm, tw), dst.dtype)] * 2),
    )(src, dst)


def _branch_merge(d, sgu, att, Ga, Gb, P, b_gate, *, name, sched=None):
    T, D, GW, AW = d.T, d.D, d.GW, d.AW
    _, _, _, _, off_ga, off_gb = _offsets(d)
    tm, tn = _tile(T, 1024), _tile(D, 512)
    assert off_ga % tn == 0 and off_gb % tn == 0

    def body(s_ref, a_ref, ga_w, gb_w, ga_ref, gb_ref, bga_ref, bgb_ref, ya_ref, yb_ref, mg_ref):
        ya = _dot(s_ref[...], ga_w[...], "nt")
        yb = _dot(a_ref[...], gb_w[...], "nt")
        ya_ref[...] = ya
        yb_ref[...] = yb
        sa = jax.nn.sigmoid(ga_ref[...] + bga_ref[...])
        sb = jax.nn.sigmoid(gb_ref[...] + bgb_ref[...])
        mg_ref[...] = (sa * ya + sb * yb).astype(BF16)

    tile = pl.BlockSpec((tm, tn), lambda j, i: (i, j))
    return _hosted(
        body,
        name=name,
        sched=sched,
        grid=(D // tn, T // tm),
        in_specs=[
            pl.BlockSpec((tm, GW), lambda j, i: (i, 0)),
            pl.BlockSpec((tm, AW), lambda j, i: (i, 0)),
            pl.BlockSpec((tn, GW), lambda j, i: (j, 0)),
            pl.BlockSpec((tn, AW), lambda j, i: (j, 0)),
            pl.BlockSpec((tm, tn), lambda j, i, o=off_ga // tn: (i, j + o)),
            pl.BlockSpec((tm, tn), lambda j, i, o=off_gb // tn: (i, j + o)),
            pl.BlockSpec((1, tn), lambda j, i: (0, j)),
            pl.BlockSpec((1, tn), lambda j, i, o=D // tn: (0, j + o)),
        ],
        out_specs=[tile, tile, tile],
        out_shape=[jax.ShapeDtypeStruct((T, D), F32), jax.ShapeDtypeStruct((T, D), F32), jax.ShapeDtypeStruct((T, D), BF16)],
        compiler_params=_params(
            ("arbitrary", "arbitrary"),
            [((tm, GW), BF16), ((tm, AW), BF16), ((tn, GW), BF16), ((tn, AW), BF16)] + [((tm, tn), F32)] * 5,
        ),
    )(sgu, att, Ga, Gb, P, P, b_gate, b_gate)


def _xattn_probs(qh, kh):
    s = _dot(qh, kh, "nt") * (X_HEAD_DIM**-0.5)
    e = jnp.exp(s - jnp.max(s, axis=1, keepdims=True))
    return e / jnp.sum(e, axis=1, keepdims=True)


def _xattn_fwd(d, X, Xb, kv, Gxq, Gxo, ln_g, ln_b, *, name, sched=None):
    T, D, XW, XH, MEM = d.T, d.D, d.XW, d.XH, d.MEM
    tm = _tile(T, 256)

    def body(x_ref, xb_ref, kv_ref, wq_ref, wo_ref, g_ref, b_ref, y_ref, yb_ref, xh_ref, rs_ref, q_ref, o_ref, o_s):
        q = _dot(xb_ref[...], wq_ref[...], "nn").astype(BF16)
        q_ref[...] = q
        for h in range(XH):
            hs = slice(h * X_HEAD_DIM, (h + 1) * X_HEAD_DIM)
            p = _xattn_probs(q[:, hs], kv_ref[:, hs]).astype(BF16)
            o_s[:, hs] = _dot(p, kv_ref[:, XW + h * X_HEAD_DIM : XW + (h + 1) * X_HEAD_DIM], "nn").astype(BF16)
        o = o_s[...]
        o_ref[...] = o
        y, xhat, rstd = _ln_fwd(ALPHA * x_ref[...] + _dot(o, wo_ref[...], "nt"), g_ref[...], b_ref[...])
        y_ref[...] = y
        yb_ref[...] = y.astype(BF16)
        xh_ref[...] = xhat
        rs_ref[...] = rstd

    row = pl.BlockSpec((tm, D), lambda i: (i, 0))
    nar = pl.BlockSpec((tm, XW), lambda i: (i, 0))
    vec = pl.BlockSpec((1, D), lambda i: (0, 0))
    return _hosted(
        body,
        name=name,
        sched=sched,
        grid=(T // tm,),
        in_specs=[
            row,
            row,
            pl.BlockSpec((MEM, 2 * XW), lambda i: (0, 0)),
            pl.BlockSpec((D, XW), lambda i: (0, 0)),
            pl.BlockSpec((D, XW), lambda i: (0, 0)),
            vec,
            vec,
        ],
        out_specs=[row, row, row, pl.BlockSpec((tm, 1), lambda i: (i, 0)), nar, nar],
        out_shape=[
            jax.ShapeDtypeStruct((T, D), F32),
            jax.ShapeDtypeStruct((T, D), BF16),
            jax.ShapeDtypeStruct((T, D), F32),
            jax.ShapeDtypeStruct((T, 1), F32),
            jax.ShapeDtypeStruct((T, XW), BF16),
            jax.ShapeDtypeStruct((T, XW), BF16),
        ],
        scratch_shapes=[pltpu.VMEM((tm, XW), BF16)],
        compiler_params=_params(("arbitrary",), [((tm, D), F32)] * 4 + [((D, XW), BF16)] * 2),
    )(X, Xb, kv, Gxq, Gxo, ln_g, ln_b)


def _xattn_bwd(d, dz, dzb, q_b, kv, Gxq, Gxo, xhat, rstd, ln_g, *, name, sched=None):
    T, D, XW, XH, MEM = d.T, d.D, d.XW, d.XH, d.MEM
    tm = _tile(T, 256)
    scale = X_HEAD_DIM**-0.5

    def body(dz_ref, dzb_ref, q_ref, kv_ref, wq_ref, wo_ref, xh_ref, rs_ref, g_ref, dx_ref, dxb_ref, dg_ref, db_ref, dq_ref, dkv_ref, dq_s):
        first = pl.program_id(0) == 0
        do = _dot(dzb_ref[...], wo_ref[...], "nn").astype(BF16)
        for h in range(XH):
            hs = slice(h * X_HEAD_DIM, (h + 1) * X_HEAD_DIM)
            vs = slice(XW + h * X_HEAD_DIM, XW + (h + 1) * X_HEAD_DIM)
            kh, vh, qh, doh = kv_ref[:, hs], kv_ref[:, vs], q_ref[:, hs], do[:, hs]
            p = _xattn_probs(qh, kh)
            dp = _dot(doh, vh, "nt")
            ds = (p * (dp - jnp.sum(p * dp, axis=1, keepdims=True)) * scale).astype(BF16)
            dq_s[:, hs] = _dot(ds, kh, "nn").astype(BF16)
            _acc_store(dkv_ref.at[h], _dot(ds, qh, "tn"), first)
            _acc_store(dkv_ref.at[XH + h], _dot(p.astype(BF16), doh, "tn"), first)
        dq = dq_s[...]
        dq_ref[...] = dq
        dy = ALPHA * dz_ref[...] + _dot(dq, wq_ref[...], "nt")
        xh = xh_ref[...]
        dz_in = _ln_bwd_vals(dy, xh, rs_ref[...], g_ref[...])
        dx_ref[...] = dz_in
        dxb_ref[...] = dz_in.astype(BF16)
        _acc_store(dg_ref, jnp.sum(dy * xh, axis=0, keepdims=True), first)
        _acc_store(db_ref, jnp.sum(dy, axis=0, keepdims=True), first)

    row = pl.BlockSpec((tm, D), lambda i: (i, 0))
    nar = pl.BlockSpec((tm, XW), lambda i: (i, 0))
    vec = pl.BlockSpec((1, D), lambda i: (0, 0))
    return _hosted(
        body,
        name=name,
        sched=sched,
        grid=(T // tm,),
        in_specs=[
            row,
            row,
            nar,
            pl.BlockSpec((MEM, 2 * XW), lambda i: (0, 0)),
            pl.BlockSpec((D, XW), lambda i: (0, 0)),
            pl.BlockSpec((D, XW), lambda i: (0, 0)),
            row,
            pl.BlockSpec((tm, 1), lambda i: (i, 0)),
            vec,
        ],
        out_specs=[row, row, vec, vec, nar, pl.BlockSpec((2 * XH, MEM, X_HEAD_DIM), lambda i: (0, 0, 0))],
        out_shape=[
            jax.ShapeDtypeStruct((T, D), F32),
            jax.ShapeDtypeStruct((T, D), BF16),
            jax.ShapeDtypeStruct((1, D), F32),
            jax.ShapeDtypeStruct((1, D), F32),
            jax.ShapeDtypeStruct((T, XW), BF16),
            jax.ShapeDtypeStruct((2 * XH, MEM, X_HEAD_DIM), F32),
        ],
        scratch_shapes=[pltpu.VMEM((tm, XW), BF16)],
        compiler_params=_params(("arbitrary",), [((tm, D), F32)] * 5 + [((D, XW), BF16)] * 2),
    )(dz, dzb, q_b, kv, Gxq, Gxo, xhat, rstd, ln_g)


def _resid_ln_outs():
    return (("mn", F32), ("mn", BF16), ("mn", F32), ("m1", F32))


def _layer_fwd(d, X, Xb, memb, G, S, cos2, sin2, tag, sched=None):
    D = d.D
    mm = functools.partial(_matmul, sched=sched)
    P = mm(Xb, G["in"], "nt", name=f"proj_{tag}", tm=1024, tn=_tile(d.IN, 1280))
    sgu = _gmlp_fwd(d, P, S["ln_v_g"], S["ln_v_b"], S["w_s"], S["b_sT"], name=f"gmlp_fwd_{tag}", sched=sched)
    att = _swa_fwd(d, P, cos2, sin2, S["sinks"], name=f"swa_fwd_{tag}", sched=sched)
    ya, yb, merged = _branch_merge(d, sgu, att, G["a"], G["b"], P, S["b_gate"], name=f"merge_{tag}", sched=sched)
    X1, X1b, xh1, rs1 = mm(
        merged, G["o"], "nn", name=f"oproj_ln_{tag}", tn=D, tk=1024, out_defs=_resid_ln_outs(), epilogue=_ep_resid_ln,
        extras=((X, "mn", 0), (S["ln1_g"], "row", 0), (S["ln1_b"], "row", 0)),
    )
    kv = mm(memb, G["xkv"], "nn", name=f"xkv_{tag}", out_defs=(("mn", BF16),))
    X2, X2b, xh2, rs2, q_b, o_b = _xattn_fwd(d, X1, X1b, kv, G["xq"], G["xo"], S["ln2_g"], S["ln2_b"], name=f"xattn_fwd_{tag}", sched=sched)
    H, A = mm(X2b, G["up"], "nt", name=f"up_{tag}", tm=1024, out_defs=(("mn", F32), ("mn", BF16)), epilogue=_ep_relu2)
    X3, X3b, xh3, rs3 = mm(
        A, G["down"], "nn", name=f"down_ln_{tag}", tn=D, tk=1024, out_defs=_resid_ln_outs(), epilogue=_ep_resid_ln,
        extras=((X2, "mn", 0), (S["ln3_g"], "row", 0), (S["ln3_b"], "row", 0)),
    )
    saved = dict(Xb=Xb, P=P, sgu=sgu, att=att, ya=ya, yb=yb, merged=merged, X1b=X1b, xh1=xh1, rs1=rs1, kv=kv, q_b=q_b,
                 o_b=o_b, X2b=X2b, xh2=xh2, rs2=rs2, H=H, A=A, xh3=xh3, rs3=rs3)
    return X3, X3b, saved


def _layer_bwd(d, dXout, sv, memb, G, S, cos2, sin2, tag, sched=None, on_grad=None, target=None):
    D = d.D
    mm = functools.partial(_matmul, sched=sched)
    emit = on_grad if on_grad is not None else (lambda n, g: None)
    wide = dict(tn=D)
    _, _, off_k, _, off_ga, off_gb = _offsets(d)
    bf = (("mn", BF16),)
    dz3, dz3b, dg3, db3, *loss = _ln_bwd(dXout, sv["xh3"], sv["rs3"], S["ln3_g"], name=f"ln3_bwd_{tag}", sched=sched, target=target)
    dHb = mm(dz3b, G["down"], "nt", name=f"down_dx_{tag}", tm=1024, out_defs=bf, epilogue=_ep_relu2_bwd, extras=((sv["H"], "mn", 0),))
    g_down = mm(sv["A"], dz3b, "tn", name=f"down_dw_{tag}", **wide)
    emit("down", g_down)
    ln_bwd_outs = (("mn", F32), ("mn", BF16), ("acc", F32), ("acc", F32))
    dz2, dz2b, dg2, db2 = mm(
        dHb, G["up"], "nn", name=f"up_dx_{tag}", tn=D, tk=1024, out_defs=ln_bwd_outs, epilogue=_ep_resid_ln_bwd,
        extras=((dz3, "mn", 0), (sv["xh2"], "mn", 0), (sv["rs2"], "m1", 0), (S["ln2_g"], "row", 0)),
    )
    g_up = mm(dHb, sv["X2b"], "tn", name=f"up_dw_{tag}", **wide)
    emit("up", g_up)
    dz1, dz1b, dg1, db1, dq_b, dkv = _xattn_bwd(d, dz2, dz2b, sv["q_b"], sv["kv"], G["xq"], G["xo"], sv["xh1"], sv["rs1"], S["ln1_g"],
                                               name=f"xattn_bwd_{tag}", sched=sched)
    g_xo = mm(dz2b, sv["o_b"], "tn", name=f"xo_dw_{tag}")
    emit("xo", g_xo)
    g_xq = mm(sv["X1b"], dq_b, "tn", name=f"xq_dw_{tag}")
    emit("xq", g_xq)
    dkv_b = dkv.transpose(1, 0, 2).reshape(d.MEM, 2 * d.XW).astype(BF16)
    g_xkv = mm(memb, dkv_b, "tn", name=f"xkv_dw_{tag}")
    emit("xkv", g_xkv)
    dya_b, dyb_b, dP, dgb_b, dbga, dbgb = mm(
        dz1b, G["o"], "nt", name=f"oproj_dx_{tag}", tm=1024, tn=512,
        out_defs=(("mn", BF16), ("mn", BF16), ("cols", BF16, d.IN, off_ga), ("mn", BF16), ("acc", F32), ("acc", F32)),
        epilogue=_ep_gate_bwd,
        extras=((sv["P"], "mn", off_ga), (sv["P"], "mn", off_gb), (sv["ya"], "mn", 0), (sv["yb"], "mn", 0),
                (S["b_gate"], "row", 0), (S["b_gate"], "row", D)),
    )
    g_o = mm(sv["merged"], dz1b, "tn", name=f"oproj_dw_{tag}", **wide)
    emit("o", g_o)
    dP = _place_cols(dP, dgb_b, off_gb, name=f"place_dgb_{tag}")
    dP, dwm, dbsT, dlvg, dlvb = _gmlp_bwd(d, sv["P"], dya_b, G["a"], S["ln_v_g"], S["ln_v_b"], S["w_s"], S["b_sT"], dP, name=f"gmlp_bwd_{tag}", sched=sched)
    g_a = mm(dya_b, sv["sgu"], "tn", name=f"bra_dw_{tag}")
    emit("a", g_a)
    dP, dkvr_b, dsk = _swa_bwd(d, sv["P"], dyb_b, G["b"], cos2, sin2, S["sinks"], dP, name=f"swa_bwd_{tag}", sched=sched)
    g_b = mm(dyb_b, sv["att"], "tn", name=f"brb_dw_{tag}")
    emit("b", g_b)
    dP = _place_cols(dP, dkvr_b, off_k, name=f"place_dkv_{tag}")
    small = dict(b_gate=jnp.concatenate([dbga, dbgb], axis=1), ln_v_g=dlvg, ln_v_b=dlvb, w_s=dwm, b_sT=dbsT, sinks=dsk,
                 ln1_g=dg1, ln1_b=db1, ln2_g=dg2, ln2_b=db2, ln3_g=dg3, ln3_b=db3)
    emit("small", small)
    g_in = mm(dP, sv["Xb"], "tn", name=f"proj_dw_{tag}", **wide)
    emit("in", g_in)
    dXin = mm(dP, G["in"], "nn", name=f"proj_dx_{tag}", tn=D, tk=_tile(d.IN, 1920), epilogue=_ep_resid, extras=((dz1, "mn", 0),))
    big = {"in": g_in, "a": g_a, "b": g_b, "o": g_o, "xq": g_xq, "xkv": g_xkv, "xo": g_xo, "up": g_up, "down": g_down}
    return dXin, big, small, (loss[0] if loss else None)


def _rope_tables(T):
    inv = 1.0 / (ROPE_THETA ** (jnp.arange(0, HEAD_DIM, 2, dtype=F32) / HEAD_DIM))
    ang = jnp.arange(T, dtype=F32)[:, None] * inv[None, :]
    cos, sin = jnp.cos(ang), jnp.sin(ang)
    reps = LANES // HEAD_DIM
    return jnp.concatenate([cos, cos] * reps, axis=1), jnp.concatenate([-sin, sin] * reps, axis=1)


def _local_step(d, x, mem, target, Gs, Ss, sched=None, on_grad=None):
    cos2, sin2 = _rope_tables(d.T)
    memb = mem.astype(BF16)
    X, Xb = x, x.astype(BF16)
    saved = []
    for l in range(DEPTH):
        X, Xb, sv = _layer_fwd(d, X, Xb, memb, Gs[l], Ss[l], cos2, sin2, f"l{l}", sched)
        saved.append(sv)
    bigs, smalls = [None] * DEPTH, [None] * DEPTH
    dX, loss = X, None
    for l in reversed(range(DEPTH)):
        emit = None if on_grad is None else functools.partial(on_grad, l)
        head = target if l == DEPTH - 1 else None
        dX, bigs[l], smalls[l], got = _layer_bwd(d, dX, saved[l], memb, Gs[l], Ss[l], cos2, sin2, f"l{l}", sched, emit, head)
        loss = got if got is not None else loss
    return loss, dX, bigs, smalls


def _place():
    x, y, c = lax.axis_index("x"), lax.axis_index("y"), lax.axis_index("c")
    return x, y, c, [(1 - x, y), (x, 1 - y), (1 - x, 1 - y)]


class _Task:
    def __init__(self, ins, out_shapes, n_remote, n_local, plan, done, aliases=None):
        self.ins, self.out_shapes, self.n_remote, self.n_local = list(ins), list(out_shapes), n_remote, n_local
        self.plan, self.done, self.aliases = plan, done, dict(aliases or {})


class _Sched:
    def __init__(self):
        self.pending = {}

    def at(self, host, task):
        self.pending.setdefault(host, []).append(task)

    def take(self, host):
        return self.pending.pop(host, [])


def _in_hbm(out_shape):
    if isinstance(out_shape, (list, tuple)):
        return [_in_hbm(s) for s in out_shape]
    return pltpu.HBM(out_shape.shape, out_shape.dtype)


def _hosted(body, *, name, sched=None, tasks=(), grid=(), in_specs=None, out_specs=None, out_shape=(), scratch_shapes=(),
            compiler_params=None, input_output_aliases=None):
    tasks = list(tasks) + (sched.take(name) if sched is not None else [])
    scratch_shapes = list(scratch_shapes)
    kwargs = dict(name=name)
    core_aliases = dict(input_output_aliases or {})
    if grid:
        kwargs["grid"] = grid
    if compiler_params is not None:
        kwargs["compiler_params"] = compiler_params
    if not tasks:
        if in_specs is not None:
            kwargs.update(in_specs=in_specs, out_specs=out_specs)
        return pl.pallas_call(body, out_shape=_in_hbm(out_shape), scratch_shapes=scratch_shapes, input_output_aliases=core_aliases, **kwargs)
    assert in_specs is not None and out_specs is not None, name
    single = not isinstance(out_shape, (list, tuple))
    core_shape = [out_shape] if single else list(out_shape)
    core_specs = [out_specs] if single else list(out_specs)
    in_specs = list(in_specs)
    n_in, n_out, n_scr = len(in_specs), len(core_shape), len(scratch_shapes)
    t_ins = [a for t in tasks for a in t.ins]
    t_outs = [s for t in tasks for s in t.out_shapes]
    n_rem = sum(t.n_remote for t in tasks)
    n_loc = sum(t.n_local for t in tasks)
    aliases, pos_in, pos_out = dict(core_aliases), n_in, n_out
    for t in tasks:
        for a, b in t.aliases.items():
            aliases[pos_in + a] = pos_out + b
        pos_in += len(t.ins)
        pos_out += len(t.out_shapes)
    any_spec = pl.BlockSpec(memory_space=pl.ANY)

    def wrapped(*refs):
        core_in, t_in = refs[:n_in], refs[n_in : n_in + len(t_ins)]
        o0 = n_in + len(t_ins)
        core_out, t_out = refs[o0 : o0 + n_out], refs[o0 + n_out : o0 + n_out + len(t_outs)]
        s0 = o0 + n_out + len(t_outs)
        core_scr = refs[s0 : s0 + n_scr]
        send_sems, recv_sems, local_sems = refs[s0 + n_scr :]
        remote, local = [], []
        pi = po = pr = pq = 0
        for t in tasks:
            r, q = t.plan(t_in[pi : pi + len(t.ins)], t_out[po : po + len(t.out_shapes)], send_sems, recv_sems, local_sems, pr, pq)
            assert len(r) == t.n_remote and len(q) == t.n_local
            remote += r
            local += q
            pi, po, pr, pq = pi + len(t.ins), po + len(t.out_shapes), pr + t.n_remote, pq + t.n_local

        def start():
            for cp in local + remote:
                cp.start()

        def finish():
            for cp in remote + local:
                cp.wait()

        if grid:
            ids = [pl.program_id(a) for a in range(len(grid))]
            first = functools.reduce(jnp.logical_and, [i == 0 for i in ids])
            last = functools.reduce(jnp.logical_and, [i == g - 1 for i, g in zip(ids, grid)])
            pl.when(first)(start)
            body(*core_in, *core_out, *core_scr)
            pl.when(last)(finish)
        else:
            start()
            body(*core_in, *core_out, *core_scr)
            finish()

    call = pl.pallas_call(
        wrapped,
        in_specs=in_specs + [any_spec] * len(t_ins),
        out_specs=core_specs + [any_spec] * len(t_outs),
        out_shape=_in_hbm(core_shape + t_outs),
        scratch_shapes=scratch_shapes
        + [pltpu.SemaphoreType.DMA((n_rem,)), pltpu.SemaphoreType.DMA((n_rem,)), pltpu.SemaphoreType.DMA((max(n_loc, 1),))],
        input_output_aliases=aliases,
        **kwargs,
    )

    def run(*operands):
        outs = call(*operands, *t_ins)
        po = n_out
        for t in tasks:
            t.done(list(outs[po : po + len(t.out_shapes)]))
            po += len(t.out_shapes)
        return outs[0] if single else list(outs[:n_out])

    return run


def _comm_only(tasks, *, name):
    _hosted(lambda: None, name=name, tasks=tasks, in_specs=[], out_shape=[], out_specs=[])()


def _rows(ref, block, n_blocks):
    r = ref.shape[0] // n_blocks
    return ref.at[pl.ds(block * r, r), :]


def _remote(src, dst, send_sems, recv_sems, k, to):
    return pltpu.make_async_remote_copy(src_ref=src, dst_ref=dst, send_sem=send_sems.at[k], recv_sem=recv_sems.at[k],
                                        device_id=to, device_id_type=MESH)


def _block_rows(ref, px, py, pc):
    return _rows(ref, 4 * px + 2 * py + pc, N_DEV)


def _gather_stage1(shards, done):
    n = len(shards)

    def plan(ins, outs, send_sems, recv_sems, local_sems, pr, pq):
        x, y, c, _ = _place()
        remote, local = [], []
        for w in range(n):
            dst = _block_rows(outs[w], x, y, c)
            local.append(pltpu.make_async_copy(ins[w], dst, local_sems.at[pq + w]))
            for j, to in enumerate([(x, y, 1 - c), (1 - x, y, c), (x, 1 - y, c)]):
                remote.append(_remote(ins[w], dst, send_sems, recv_sems, pr + 3 * w + j, to))
        return remote, local

    shapes = [jax.ShapeDtypeStruct((N_DEV * s.shape[0], s.shape[1]), s.dtype) for s in shards]
    return _Task(shards, shapes, 3 * n, n, plan, done)


def _gather_stage2(partial, done):
    n = len(partial)

    def plan(ins, outs, send_sems, recv_sems, local_sems, pr, pq):
        x, y, c, _ = _place()
        along_y = c == 1
        from_chip = (jnp.where(along_y, 1 - x, x), jnp.where(along_y, y, 1 - y))
        to_chip = (jnp.where(along_y, x, 1 - x), jnp.where(along_y, 1 - y, y))
        remote = []
        for w in range(n):
            rows = _block_rows(outs[w], *from_chip, c)
            remote.append(_remote(rows, rows, send_sems, recv_sems, pr + 3 * w, (*to_chip, c)))
            for j, chip in enumerate([(1 - x, y), (x, 1 - y)]):
                rows = _block_rows(outs[w], *chip, c)
                remote.append(_remote(rows, rows, send_sems, recv_sems, pr + 3 * w + 1 + j, (x, y, 1 - c)))
        return remote, []

    shapes = [jax.ShapeDtypeStruct(p.shape, p.dtype) for p in partial]
    return _Task(partial, shapes, 3 * n, 0, plan, done, aliases={w: w for w in range(n)})


def _gather_stage3(partial, done):
    n = len(partial)

    def plan(ins, outs, send_sems, recv_sems, local_sems, pr, pq):
        x, y, c, _ = _place()
        remote = []
        for w in range(n):
            rows = _block_rows(outs[w], 1 - x, 1 - y, c)
            remote.append(_remote(rows, rows, send_sems, recv_sems, pr + w, (x, y, 1 - c)))
        return remote, []

    shapes = [jax.ShapeDtypeStruct(p.shape, p.dtype) for p in partial]
    return _Task(partial, shapes, n, 0, plan, done, aliases={w: w for w in range(n)})


def _gather_chain(arrays, hosts, sched, finished, label):
    stages = (_gather_stage1, _gather_stage2, _gather_stage3)

    def run(k, operands):
        def done(outs):
            if k + 1 < len(stages):
                run(k + 1, outs)
            else:
                finished(outs)

        task = stages[k](operands, done)
        if hosts[k] is None:
            _comm_only([task], name=f"{label}_stage{k + 1}")
        else:
            sched.at(hosts[k], task)

    run(0, list(arrays))


def _sibling_stage(grads, done):
    n = len(grads)

    def plan(ins, outs, send_sems, recv_sems, local_sems, pr, pq):
        x, y, c, _ = _place()
        remote = []
        for w in range(n):
            for k in range(4):
                src = _rows(ins[w], 4 * (k // 2) + 2 * (k % 2) + (1 - c), N_DEV)
                remote.append(_remote(src, _rows(outs[w], k, 4), send_sems, recv_sems, pr + 4 * w + k, (x, y, 1 - c)))
        return remote, []

    shapes = [jax.ShapeDtypeStruct((g.shape[0] // 2, g.shape[1]), g.dtype) for g in grads]
    return _Task(grads, shapes, 4 * n, 0, plan, done)


def _chips_stage(sends, done, part=(0, 1), into=None):
    n = len(sends)
    k, n_parts = part

    def plan(ins, outs, send_sems, recv_sems, local_sems, pr, pq):
        x, y, c, chips = _place()
        remote = []
        for w in range(n):
            r = sends[w].shape[0] // 3
            rp = r // n_parts
            for j, chip in enumerate(chips):
                rows = pl.ds(j * r + k * rp, rp)
                remote.append(_remote(ins[w].at[rows, :], outs[w].at[rows, :], send_sems, recv_sems, pr + 3 * w + j, (*chip, c)))
        return remote, []

    shapes = [jax.ShapeDtypeStruct(s.shape, s.dtype) for s in sends]
    if into is None:
        return _Task(sends, shapes, 3 * n, 0, plan, done)
    return _Task(list(sends) + list(into), shapes, 3 * n, 0, plan, done, aliases={n + w: w for w in range(n)})


def _pair_sum(grad, got, place, *, name):
    R, C = grad.shape
    r = R // N_DEV
    tr = _tile(r, 256, 16)
    nt = r // tr

    def chip_of(s, pr):
        fx = jnp.where((s == 1) | (s == 3), 1, 0)
        fy = jnp.where(s >= 2, 1, 0)
        return pr[0] ^ fx, pr[1] ^ fy

    def grad_map(s, t, pr):
        kx, ky = chip_of(s, pr)
        return ((4 * kx + 2 * ky + pr[2]) * nt + t, 0)

    def got_map(s, t, pr):
        kx, ky = chip_of(s, pr)
        return ((2 * kx + ky) * nt + t, 0)

    def body(pr, g_ref, r_ref, own_ref, send_ref):
        s = pl.program_id(0)
        tot = g_ref[...] + r_ref[...]

        @pl.when(s == 0)
        def _():
            own_ref[...] = tot

        @pl.when(s > 0)
        def _():
            send_ref[...] = tot.astype(BF16)

    return pl.pallas_call(
        body,
        name=name,
        grid_spec=pltpu.PrefetchScalarGridSpec(
            num_scalar_prefetch=1,
            grid=(4, nt),
            in_specs=[pl.BlockSpec((tr, C), grad_map), pl.BlockSpec((tr, C), got_map)],
            out_specs=[
                pl.BlockSpec((tr, C), lambda s, t, pr: (jnp.where(s == 0, t, nt - 1), 0)),
                pl.BlockSpec((tr, C), lambda s, t, pr: (jnp.where(s == 0, 0, (s - 1) * nt + t), 0)),
            ],
        ),
        out_shape=[jax.ShapeDtypeStruct((r, C), F32), jax.ShapeDtypeStruct((3 * r, C), BF16)],
        compiler_params=_params(("arbitrary", "arbitrary"), [((tr, C), F32)] * 4),
    )(place, grad, got)


def _adam_vals(w, g, m, v):
    m = ADAM_B1 * m + (1.0 - ADAM_B1) * g
    v = ADAM_B2 * v + (1.0 - ADAM_B2) * (g * g)
    m_hat = m / (1.0 - ADAM_B1**ADAM_STEP)
    v_hat = v / (1.0 - ADAM_B2**ADAM_STEP)
    delta = -ADAM_LR * (m_hat / (jnp.sqrt(v_hat) + ADAM_EPS) + ADAM_WD * w)
    return delta, m, v


def _adam_big(w3, m3, v3, own, got, layer, transposed, prev, *, name, sched=None):
    _, A, B = w3.shape
    ta = _tile(A, 256, 16)
    nt = A // ta
    b_pad = (-B) % LANES

    def body(*refs):
        w_ref, m_ref, v_ref, own_ref, g1_ref, g2_ref, g3_ref = refs[:7]
        g_ref, d_ref, nm_ref, nv_ref = refs[-4:]
        g = ((own_ref[...] + g1_ref[...].astype(F32)) + g2_ref[...].astype(F32)) + g3_ref[...].astype(F32)
        if transposed:
            if b_pad:
                g = jnp.concatenate([g, jnp.zeros((b_pad, ta), F32)], axis=0)
            g = g.T[:, :B]
        g_ref[...] = g
        d_ref[...], nm_ref[...], nv_ref[...] = _adam_vals(w_ref[...], g, m_ref[...], v_ref[...])

    nat = pl.BlockSpec((None, ta, B), lambda t: (layer, t, 0))
    if transposed:
        parts = [pl.BlockSpec((B, ta), lambda t: (0, t))] + [pl.BlockSpec((B, ta), lambda t, j=j: (j, t)) for j in range(3)]
    else:
        parts = [pl.BlockSpec((ta, B), lambda t: (t, 0))] + [pl.BlockSpec((ta, B), lambda t, j=j: (j * nt + t, 0)) for j in range(3)]
    keep = [] if prev is None else list(prev)
    outs = _hosted(
        body,
        name=name,
        sched=sched,
        grid=(nt,),
        in_specs=[nat, nat, nat] + parts + [pl.BlockSpec(memory_space=pl.ANY)] * len(keep),
        out_specs=[nat] * 4,
        out_shape=[jax.ShapeDtypeStruct(w3.shape, F32)] * 4,
        input_output_aliases={7 + k: k for k in range(len(keep))},
        compiler_params=_params(("arbitrary",), [((ta, B), F32)] * 10),
    )(w3, m3, v3, own, got, got, got, *keep)
    return list(outs)


def _adam_small(w, m, v, parts, *, name, sched=None):
    R = w.shape[0]

    def body(w_ref, m_ref, v_ref, p_ref, g_ref, d_ref, nm_ref, nv_ref):
        g = p_ref[pl.ds(0, R), :]
        for k in range(1, N_DEV):
            g = g + p_ref[pl.ds(k * R, R), :]
        g_ref[...] = g
        d_ref[...], nm_ref[...], nv_ref[...] = _adam_vals(w_ref[...], g, m_ref[...], v_ref[...])

    return _hosted(
        body,
        name=name,
        sched=sched,
        out_shape=[jax.ShapeDtypeStruct((R, LANES), F32)] * 4,
        compiler_params=_params(None, [((R, LANES), F32)] * 16),
    )(w, m, v, parts)


SMALL_NAMES = ("b_gate", "ln_v_g", "ln_v_b", "w_s", "b_s", "sinks", "ln1_g", "ln1_b", "ln2_g", "ln2_b", "ln3_g", "ln3_b")
PACK_ALIGN = 8 * LANES


def _pack(arrays):
    flat = []
    for a in arrays:
        a = a.reshape(-1)
        flat.append(jnp.pad(a, (0, (-a.shape[0]) % PACK_ALIGN)))
    return jnp.concatenate(flat).reshape(-1, LANES)


def _unpack(packed, shapes):
    flat = packed.reshape(-1)
    out, pos = [], 0
    for s in shapes:
        n = 1
        for e in s:
            n *= e
        out.append(flat[pos : pos + n].reshape(s))
        pos += n + (-n) % PACK_ALIGN
    return out


BIG = (("in", "w_in", True), ("a", "w_br_a", True), ("b", "w_br_b", True), ("o", "w_o", False), ("xq", "w_xq", False),
       ("xkv", "w_xkv", False), ("xo", "w_xo", True), ("up", "w_up", True), ("down", "w_down", False))


SMALL_BIG = ("a", "b", "o", "xq", "xkv", "xo")
GATHER_PLAN = (
    (0, ("in",), (None, None, None)),
    (0, SMALL_BIG, ("proj_l0", "gmlp_fwd_l0", "swa_fwd_l0")),
    (0, ("up",), ("swa_fwd_l0", "merge_l0", "oproj_ln_l0")),
    (0, ("down",), ("oproj_ln_l0", "xattn_fwd_l0", "up_l0")),
    (1, ("in",), ("up_l0", "down_ln_l0", None)),
    (1, SMALL_BIG, ("down_ln_l0", "proj_l1", "gmlp_fwd_l1")),
    (1, ("up",), ("proj_l1", "swa_fwd_l1", "merge_l1")),
    (1, ("down",), ("merge_l1", "oproj_ln_l1", "xattn_fwd_l1")),
)
EARLY_SMALL_BIG = ("o", "xq", "xkv", "xo")
LATE_SMALL_BIG = ("a", "b")
GRAD_HOSTS = {
    1: {"down": ("xattn_bwd_l1", "swa_bwd_l1"), "up": ("xattn_bwd_l1", "proj_dw_l1"),
        **{g: ("gmlp_bwd_l1", "proj_dx_l1") for g in EARLY_SMALL_BIG}, **{g: ("proj_dw_l1", "proj_dx_l1") for g in LATE_SMALL_BIG},
        "in": ("proj_dx_l1", "down_dx_l0")},
    0: {"down": ("xattn_bwd_l0", ("oproj_dx_l0", "gmlp_bwd_l0")), "up": ("xattn_bwd_l0", "swa_bwd_l0"),
        **{g: ("gmlp_bwd_l0", "proj_dw_l0") for g in EARLY_SMALL_BIG}, **{g: ("proj_dw_l0", "proj_dx_l0") for g in LATE_SMALL_BIG},
        "in": (None, "proj_dx_l0")},
}
SMALL_GRAD_HOSTS = ("proj_dw_l0", "proj_dx_l0", None)


def kernel(x, mem, w_in, b_gate, ln_v_g, ln_v_b, w_s, b_s, sinks, w_br_a, w_br_b, w_o, ln1_g, ln1_b, w_xq, w_xkv, w_xo, ln2_g, ln2_b, w_up, w_down, ln3_g, ln3_b, loss_target, m_w_in, m_b_gate, m_ln_v_g, m_ln_v_b, m_w_s, m_b_s, m_sinks, m_w_br_a, m_w_br_b, m_w_o, m_ln1_g, m_ln1_b, m_w_xq, m_w_xkv, m_w_xo, m_ln2_g, m_ln2_b, m_w_up, m_w_down, m_ln3_g, m_ln3_b, v_w_in, v_b_gate, v_ln_v_g, v_ln_v_b, v_w_s, v_b_s, v_sinks, v_w_br_a, v_w_br_b, v_w_o, v_ln1_g, v_ln1_b, v_w_xq, v_w_xkv, v_w_xo, v_ln2_g, v_ln2_b, v_w_up, v_w_down, v_ln3_g, v_ln3_b):
    given = dict(locals())
    T, D = x.shape[1], x.shape[2]
    IN, GW, AW, XW, DFF = w_in.shape[2] * N_DEV, ln_v_g.shape[1], w_br_b.shape[1], w_xq.shape[2], w_up.shape[2] * N_DEV
    KVW = (IN - 2 * GW - AW - 2 * D) // 2
    d = Dims(T=T, D=D, IN=IN, GW=GW, G=GW // GROUP_DIM, AW=AW, KVW=KVW, HQ=AW // HEAD_DIM, HKV=KVW // HEAD_DIM, XW=XW,
             XH=XW // X_HEAD_DIM, MEM=mem.shape[1], DFF=DFF)
    place = jnp.stack([lax.axis_index("x"), lax.axis_index("y"), lax.axis_index("c")]).astype(jnp.int32)

    sched = _Sched()
    big_of = {g: (p, tr) for g, p, tr in BIG}

    def shard(l, g):
        p, tr = big_of[g]
        return (given[p][l].T if tr else given[p][l]).astype(BF16)

    Gs = [{}, {}]

    def plan_gather(l, names, hosts):
        _gather_chain([shard(l, g) for g in names], hosts, sched, lambda full: Gs[l].update(zip(names, full)),
                      f"gather_{names[0]}_l{l}")

    for entry in GATHER_PLAN:
        plan_gather(*entry)
    Ss = []
    for l in range(DEPTH):
        S = {n: given[n][l].reshape(1, -1) for n in SMALL_NAMES if n not in ("w_s", "b_s", "sinks")}
        S["w_s"], S["b_sT"], S["sinks"] = w_s[l], b_s[l].T, sinks[l]
        Ss.append(S)

    owns, recvs = {}, {}

    smalls_seen = {}
    gathered_small = []

    def pack_small(src):
        parts = []
        for l in range(DEPTH):
            for n in SMALL_NAMES:
                a = src[l]["b_sT"].T if n == "b_s" else src[l][n]
                parts.append(a[0, : d.HQ] if n == "sinks" else a)
        return _pack(parts)

    def on_small(l, small):
        smalls_seen[l] = small
        if len(smalls_seen) == DEPTH:
            _gather_chain([pack_small(smalls_seen)], SMALL_GRAD_HOSTS, sched, lambda full: gathered_small.append(full[0]),
                          "gather_small_grads")

    def on_grad(l, g, grad):
        if g == "small":
            return on_small(l, grad)
        sib_host, chips_host = GRAD_HOSTS[l][g]

        def after_sibling(got):
            owns[(l, g)], send = _pair_sum(grad, got[0], place, name=f"grad_pair_sum_{g}_l{l}")
            hosts = chips_host if isinstance(chips_host, tuple) else (chips_host,)

            def register(k, into):
                def done(r):
                    if k + 1 < len(hosts):
                        register(k + 1, r)
                    else:
                        recvs[(l, g)] = r[0]

                task = _chips_stage([send], done, part=(k, len(hosts)), into=into)
                if hosts[k] is None:
                    _comm_only([task], name=f"grad_chips_{g}_l{l}")
                else:
                    sched.at(hosts[k], task)

            register(0, None)

        task = _sibling_stage([grad], after_sibling)
        if sib_host is None:
            _comm_only([task], name=f"grad_sibling_{g}_l{l}")
        else:
            sched.at(sib_host, task)

    loss, dX, bigs, smalls = _local_step(d, x[0], mem[0], loss_target[0], Gs, Ss, sched, on_grad)
    loss = lax.psum(loss[0, 0], ("x", "y", "c"))

    assert not sched.pending, sorted(sched.pending)

    def viewed(p, tr):
        return tr and given[p].shape[2] % LANES != 0

    res = {p: None for _, p, _ in BIG}
    for l in reversed(range(DEPTH)):
        for g, p, tr in BIG:
            view = viewed(p, tr)
            w3, m3, v3 = ((jnp.swapaxes(a, 1, 2) if view else a) for a in (given[p], given["m_" + p], given["v_" + p]))
            res[p] = _adam_big(w3, m3, v3, owns[(l, g)], recvs[(l, g)], l, tr and not view, res[p], name=f"adamw_{g}_l{l}")
    for g, p, tr in BIG:
        if viewed(p, tr):
            res[p] = [jnp.swapaxes(a, 1, 2) for a in res[p]]

    shapes = [given[n].shape[1:] for n in SMALL_NAMES]
    pw, pm, pv = (_pack([given[pre + n][l] for l in range(DEPTH) for n in SMALL_NAMES]) for pre in ("", "m_", "v_"))
    small_out = [_unpack(o, shapes * DEPTH) for o in _adam_small(pw, pm, pv, gathered_small[0], name="adamw_small")]

    names = ["w_in", "b_gate", "ln_v_g", "ln_v_b", "w_s", "b_s", "sinks", "w_br_a", "w_br_b", "w_o", "ln1_g", "ln1_b", "w_xq",
             "w_xkv", "w_xo", "ln2_g", "ln2_b", "w_up", "w_down", "ln3_g", "ln3_b"]
    out = [loss, dX[None]]
    for kind in range(4):
        for n in names:
            if n in SMALL_NAMES:
                i = SMALL_NAMES.index(n)
                out.append(jnp.stack([small_out[kind][l * len(SMALL_NAMES) + i] for l in range(DEPTH)]))
            else:
                out.append(res[n][kind])
    return tuple(out)
```

```python
import collections
import functools

import jax
import jax.numpy as jnp
from jax import lax
from jax.experimental import pallas as pl
from jax.experimental.pallas import tpu as pltpu

F32 = jnp.float32
BF16 = jnp.bfloat16
MESH = pl.DeviceIdType.MESH

N_DEV = 8
DEPTH = 2
CHUNK = 128
HEAD_DIM = 64
ROPE_HALF = HEAD_DIM // 2
X_HEAD_DIM = 128
GROUP_DIM = 128
LANES = 128
ROPE_THETA = 10000.0
LN_EPS = 1e-5
ALPHA = (2 * DEPTH) ** 0.25
ADAM_LR = 0.001
ADAM_B1 = 0.9
ADAM_B2 = 0.999
ADAM_EPS = 1e-08
ADAM_WD = 0.01
ADAM_STEP = 10
VMEM_BYTES = 64 * 1024 * 1024
VMEM_TEMP_BYTES = 20 * 1024 * 1024

Dims = collections.namedtuple("Dims", "T D IN GW G AW KVW HQ HKV XW XH MEM DFF")


def _offsets(d):
    off_v = d.GW
    off_q = 2 * d.GW
    off_k = off_q + d.AW
    off_va = off_k + d.KVW
    off_ga = off_va + d.KVW
    off_gb = off_ga + d.D
    return off_v, off_q, off_k, off_va, off_ga, off_gb


def _tile(dim, pref, align=LANES):
    if dim <= pref:
        return dim
    t = pref - pref % align
    while t >= align:
        if dim % t == 0:
            return t
        t -= align
    return dim


def _nbytes(shape, dtype):
    n = 1
    for s in shape:
        n *= s
    return n * jnp.dtype(dtype).itemsize


def _params(sem, blocks, scratch=0):
    need = 2 * sum(_nbytes(s, t) for s, t in blocks) + scratch + VMEM_TEMP_BYTES
    limit = min(max(need, 32 * 1024 * 1024), VMEM_BYTES - 4 * 1024 * 1024)
    return pltpu.CompilerParams(dimension_semantics=sem, vmem_limit_bytes=limit)


def _dot(a, b, kind):
    dn = {"nn": (((1,), (0,)), ((), ())), "nt": (((1,), (1,)), ((), ())), "tn": (((0,), (0,)), ((), ()))}[kind]
    return lax.dot_general(a, b, dn, preferred_element_type=F32)


def _gelu(x):
    c = 0.7978845608028654
    t = jnp.tanh(c * (x + 0.044715 * (x * x * x)))
    return 0.5 * x * (1.0 + t)


def _gelu_grad(x):
    c = 0.7978845608028654
    x2 = x * x
    t = jnp.tanh(c * (x + 0.044715 * (x2 * x)))
    return 0.5 * (1.0 + t) + 0.5 * x * (1.0 - t * t) * (c * (1.0 + 3.0 * 0.044715 * x2))


def _ln_fwd(z, g, b):
    mu = jnp.mean(z, axis=-1, keepdims=True)
    zc = z - mu
    var = jnp.mean(zc * zc, axis=-1, keepdims=True)
    rstd = lax.rsqrt(var + LN_EPS)
    xhat = zc * rstd
    return xhat * g + b, xhat, rstd


def _ln_bwd_vals(dy, xhat, rstd, g):
    gd = dy * g
    m1 = jnp.mean(gd, axis=-1, keepdims=True)
    m2 = jnp.mean(gd * xhat, axis=-1, keepdims=True)
    return rstd * (gd - m1 - xhat * m2)


def _acc_store(ref, val, first):
    @pl.when(first)
    def _():
        ref[...] = val

    @pl.when(jnp.logical_not(first))
    def _():
        ref[...] += val


def _matmul(a, b, kind, *, name, tm=512, tn=1024, tk=2048, out_defs=(("mn", F32),), epilogue=None, extras=(), sched=None):
    if kind == "nn":
        (M, K), (K2, N) = a.shape, b.shape
    elif kind == "nt":
        (M, K), (N, K2) = a.shape, b.shape
    else:
        (K, M), (K2, N) = a.shape, b.shape
    assert K == K2, (a.shape, b.shape, kind)
    tm, tn, tk = _tile(M, tm), _tile(N, tn), _tile(K, tk)
    nk = K // tk
    blocks = []

    def spec(shape, imap, dtype):
        blocks.append((shape, dtype))
        return pl.BlockSpec(shape, imap)

    if kind == "tn":
        a_spec = spec((tk, tm), lambda j, i, k: (k, i), a.dtype)
    else:
        a_spec = spec((tm, tk), lambda j, i, k: (i, k), a.dtype)
    if kind == "nt":
        b_spec = spec((tn, tk), lambda j, i, k: (j, k), b.dtype)
    else:
        b_spec = spec((tk, tn), lambda j, i, k: (k, j), b.dtype)
    in_specs = [a_spec, b_spec]
    operands = [a, b]
    for arr, ekind, off in extras:
        if ekind == "mn":
            assert off % tn == 0
            in_specs.append(spec((tm, tn), lambda j, i, k, o=off // tn: (i, j + o), arr.dtype))
        elif ekind == "row":
            assert off % tn == 0
            in_specs.append(spec((1, tn), lambda j, i, k, o=off // tn: (0, j + o), arr.dtype))
        else:
            in_specs.append(spec((tm, 1), lambda j, i, k: (i, 0), arr.dtype))
        operands.append(arr)
    out_shape, out_specs = [], []
    for od in out_defs:
        okind, dt = od[0], od[1]
        if okind == "mn":
            out_shape.append(jax.ShapeDtypeStruct((M, N), dt))
            out_specs.append(spec((tm, tn), lambda j, i, k: (i, j), dt))
        elif okind == "cols":
            assert od[3] % tn == 0
            out_shape.append(jax.ShapeDtypeStruct((M, od[2]), dt))
            out_specs.append(spec((tm, tn), lambda j, i, k, o=od[3] // tn: (i, j + o), dt))
        elif okind == "m1":
            assert N == tn
            out_shape.append(jax.ShapeDtypeStruct((M, 1), dt))
            out_specs.append(spec((tm, 1), lambda j, i, k: (i, 0), dt))
        else:
            out_shape.append(jax.ShapeDtypeStruct((1, N), F32))
            out_specs.append(spec((1, tn), lambda j, i, k: (0, j), F32))
    n_ex, n_out = len(extras), len(out_defs)
    ep = epilogue if epilogue is not None else (lambda acc: (acc,))
    in_place = nk > 1 and epilogue is None and tuple(out_defs) == (("mn", F32),)

    def body(*refs):
        a_ref, b_ref = refs[0], refs[1]
        ex_refs = refs[2 : 2 + n_ex]
        out_refs = refs[2 + n_ex : 2 + n_ex + n_out]
        i = pl.program_id(1)
        k = pl.program_id(2)

        def product():
            return _dot(a_ref[...], b_ref[...], kind)

        def finish(acc):
            vals = ep(acc, *[r[...] for r in ex_refs])
            for od, r, v in zip(out_defs, out_refs, vals):
                if od[0] == "acc":
                    _acc_store(r, v, i == 0)
                else:
                    r[...] = v.astype(od[1])

        if nk == 1:
            finish(product())
            return
        acc_ref = out_refs[0] if in_place else refs[-1]

        @pl.when(k == 0)
        def _():
            acc_ref[...] = product()

        if in_place:
            @pl.when(k > 0)
            def _():
                acc_ref[...] += product()
        else:
            if nk > 2:
                @pl.when(jnp.logical_and(k > 0, k < nk - 1))
                def _():
                    acc_ref[...] += product()

            @pl.when(k == nk - 1)
            def _():
                finish(acc_ref[...] + product())

    scratch = [pltpu.VMEM((tm, tn), F32)] if nk > 1 and not in_place else []
    outs = _hosted(
        body,
        name=name,
        sched=sched,
        grid=(N // tn, M // tm, nk),
        in_specs=in_specs,
        out_specs=out_specs,
        out_shape=out_shape,
        scratch_shapes=scratch,
        compiler_params=_params(("arbitrary", "arbitrary", "arbitrary"), blocks, _nbytes((tm, tn), F32) if scratch else 0),
    )(*operands)
    return outs if len(outs) > 1 else outs[0]


def _ep_resid_ln(acc, resid, g, b):
    y, xhat, rstd = _ln_fwd(ALPHA * resid + acc, g, b)
    return y, y, xhat, rstd


def _ep_resid(acc, resid):
    return (ALPHA * resid + acc,)


def _ep_resid_ln_bwd(acc, resid, xhat, rstd, g):
    dy = ALPHA * resid + acc
    dz = _ln_bwd_vals(dy, xhat, rstd, g)
    return dz, dz, jnp.sum(dy * xhat, axis=0, keepdims=True), jnp.sum(dy, axis=0, keepdims=True)


def _ep_relu2(acc):
    r = jnp.maximum(acc, 0.0)
    return acc, r * r


def _ep_relu2_bwd(acc, h):
    return (acc * (2.0 * jnp.maximum(h, 0.0)),)


def _ep_gate_bwd(acc, ga, gb, ya, yb, bga, bgb):
    sa = jax.nn.sigmoid(ga + bga)
    sb = jax.nn.sigmoid(gb + bgb)
    dga = acc * ya * (sa * (1.0 - sa))
    dgb = acc * yb * (sb * (1.0 - sb))
    return (acc * sa, acc * sb, dga, dgb, jnp.sum(dga, axis=0, keepdims=True), jnp.sum(dgb, axis=0, keepdims=True))


def _ln_bwd(dy, xhat, rstd, g, *, name, sched=None, target=None):
    T, D = dy.shape
    tm = _tile(T, 256)
    head = target is not None

    def body(*refs):
        dy_ref, xh_ref, rs_ref, g_ref = refs[:4]
        dz_ref, dzb_ref, dg_ref, db_ref = refs[4 + head : 8 + head]
        first = pl.program_id(0) == 0
        dyv, xh = dy_ref[...], xh_ref[...]
        if head:
            err = dyv - refs[4][...]
            dyv = err * (1.0 / D)
            part = 0.5 * jnp.sum(jnp.sum(err * err, axis=1, keepdims=True) * (1.0 / D), axis=0, keepdims=True)
            _acc_store(refs[-1], part, first)
        dz = _ln_bwd_vals(dyv, xh, rs_ref[...], g_ref[...])
        dz_ref[...] = dz
        dzb_ref[...] = dz.astype(BF16)
        _acc_store(dg_ref, jnp.sum(dyv * xh, axis=0, keepdims=True), first)
        _acc_store(db_ref, jnp.sum(dyv, axis=0, keepdims=True), first)

    row = pl.BlockSpec((tm, D), lambda i: (i, 0))
    vec = pl.BlockSpec((1, D), lambda i: (0, 0))
    one = pl.BlockSpec((1, 1), lambda i: (0, 0))
    return _hosted(
        body,
        name=name,
        sched=sched,
        grid=(T // tm,),
        in_specs=[row, row, pl.BlockSpec((tm, 1), lambda i: (i, 0)), vec] + [row] * head,
        out_specs=[row, row, vec, vec] + [one] * head,
        out_shape=[
            jax.ShapeDtypeStruct((T, D), F32),
            jax.ShapeDtypeStruct((T, D), BF16),
            jax.ShapeDtypeStruct((1, D), F32),
            jax.ShapeDtypeStruct((1, D), F32),
        ] + [jax.ShapeDtypeStruct((1, 1), F32)] * head,
        compiler_params=_params(("arbitrary",), [((tm, D), F32)] * (4 + head)),
    )(*([dy, xhat, rstd, g] + [target] * head))


def _masked_mix_weights(wm_ref, G):
    r = lax.broadcasted_iota(jnp.int32, (CHUNK, CHUNK), 0)
    c = lax.broadcasted_iota(jnp.int32, (CHUNK, CHUNK), 1)
    return [jnp.where(c <= r, wm_ref[g], 0.0).astype(BF16) for g in range(G)]


def _gmlp_fwd(d, P, lnv_g, lnv_b, wm, bsT, *, name, sched=None):
    T, GW, G = d.T, d.GW, d.G
    tm = _tile(T, 256)
    nc = tm // CHUNK

    def body(u_ref, v_ref, g_ref, b_ref, wm_ref, bs_ref, o_ref):
        gu = _gelu(u_ref[...])
        vn, _, _ = _ln_fwd(_gelu(v_ref[...]), g_ref[...], b_ref[...])
        vn = vn.astype(BF16)
        wl = _masked_mix_weights(wm_ref, G)
        for c in range(nc):
            rows = slice(c * CHUNK, (c + 1) * CHUNK)
            for g in range(G):
                cols = slice(g * GROUP_DIM, (g + 1) * GROUP_DIM)
                mixed = _dot(wl[g], vn[rows, cols], "nn") + bs_ref[:, g : g + 1]
                o_ref[rows, cols] = (gu[rows, cols] * mixed).astype(BF16)

    return _hosted(
        body,
        name=name,
        sched=sched,
        grid=(T // tm,),
        in_specs=[
            pl.BlockSpec((tm, GW), lambda i: (i, 0)),
            pl.BlockSpec((tm, GW), lambda i: (i, 1)),
            pl.BlockSpec((1, GW), lambda i: (0, 0)),
            pl.BlockSpec((1, GW), lambda i: (0, 0)),
            pl.BlockSpec((G, CHUNK, CHUNK), lambda i: (0, 0, 0)),
            pl.BlockSpec((CHUNK, G), lambda i: (0, 0)),
        ],
        out_specs=pl.BlockSpec((tm, GW), lambda i: (i, 0)),
        out_shape=jax.ShapeDtypeStruct((T, GW), BF16),
        compiler_params=_params(("arbitrary",), [((tm, GW), F32)] * 3),
    )(P, P, lnv_g, lnv_b, wm, bsT)


def _gmlp_bwd(d, P, dya_b, Ga, lnv_g, lnv_b, wm, bsT, dP, *, name, sched=None):
    T, D, GW, G = d.T, d.D, d.GW, d.G
    tm = _tile(T, 256)
    nc = tm // CHUNK

    def body(u_ref, v_ref, dya_ref, ga_ref, g_ref, b_ref, wm_ref, bs_ref, dp_in, duv_ref, dwm_ref, dbs_ref, dlg_ref, dlb_ref, dgu_s, dvn_s):
        first = pl.program_id(0) == 0
        dsgu = _dot(dya_ref[...], ga_ref[...], "nn")
        u, v = u_ref[...], v_ref[...]
        gu = _gelu(u)
        lng = g_ref[...]
        vn, xh, rstd = _ln_fwd(_gelu(v), lng, b_ref[...])
        vn = vn.astype(BF16)
        wl = _masked_mix_weights(wm_ref, G)
        r = lax.broadcasted_iota(jnp.int32, (CHUNK, CHUNK), 0)
        cc = lax.broadcasted_iota(jnp.int32, (CHUNK, CHUNK), 1)
        lane_g = lax.broadcasted_iota(jnp.int32, (CHUNK, G), 1)
        dbs = jnp.zeros((CHUNK, G), F32)
        for g in range(G):
            cols = slice(g * GROUP_DIM, (g + 1) * GROUP_DIM)
            dw = jnp.zeros((CHUNK, CHUNK), F32)
            for c in range(nc):
                rows = slice(c * CHUNK, (c + 1) * CHUNK)
                mixed = _dot(wl[g], vn[rows, cols], "nn") + bs_ref[:, g : g + 1]
                ds = dsgu[rows, cols]
                dgu_s[rows, cols] = ds * mixed
                dmx = ds * gu[rows, cols]
                dbs = dbs + jnp.where(lane_g == g, jnp.sum(dmx, axis=1, keepdims=True), 0.0)
                dmx = dmx.astype(BF16)
                dw = dw + _dot(dmx, vn[rows, cols], "nt")
                dvn_s[rows, cols] = _dot(wl[g], dmx, "tn")
            _acc_store(dwm_ref.at[g], jnp.where(cc <= r, dw, 0.0), first)
        _acc_store(dbs_ref, dbs, first)
        dvn = dvn_s[...]
        _acc_store(dlg_ref, jnp.sum(dvn * xh, axis=0, keepdims=True), first)
        _acc_store(dlb_ref, jnp.sum(dvn, axis=0, keepdims=True), first)
        dgv = _ln_bwd_vals(dvn, xh, rstd, lng)
        duv_ref[:, :GW] = (dgu_s[...] * _gelu_grad(u)).astype(BF16)
        duv_ref[:, GW:] = (dgv * _gelu_grad(v)).astype(BF16)

    return _hosted(
        body,
        name=name,
        sched=sched,
        grid=(T // tm,),
        in_specs=[
            pl.BlockSpec((tm, GW), lambda i: (i, 0)),
            pl.BlockSpec((tm, GW), lambda i: (i, 1)),
            pl.BlockSpec((tm, D), lambda i: (i, 0)),
            pl.BlockSpec((D, GW), lambda i: (0, 0)),
            pl.BlockSpec((1, GW), lambda i: (0, 0)),
            pl.BlockSpec((1, GW), lambda i: (0, 0)),
            pl.BlockSpec((G, CHUNK, CHUNK), lambda i: (0, 0, 0)),
            pl.BlockSpec((CHUNK, G), lambda i: (0, 0)),
            pl.BlockSpec(memory_space=pl.ANY),
        ],
        out_specs=[
            pl.BlockSpec((tm, 2 * GW), lambda i: (i, 0)),
            pl.BlockSpec((G, CHUNK, CHUNK), lambda i: (0, 0, 0)),
            pl.BlockSpec((CHUNK, G), lambda i: (0, 0)),
            pl.BlockSpec((1, GW), lambda i: (0, 0)),
            pl.BlockSpec((1, GW), lambda i: (0, 0)),
        ],
        out_shape=[
            jax.ShapeDtypeStruct(dP.shape, BF16),
            jax.ShapeDtypeStruct((G, CHUNK, CHUNK), F32),
            jax.ShapeDtypeStruct((CHUNK, G), F32),
            jax.ShapeDtypeStruct((1, GW), F32),
            jax.ShapeDtypeStruct((1, GW), F32),
        ],
        scratch_shapes=[pltpu.VMEM((tm, GW), F32), pltpu.VMEM((tm, GW), F32)],
        input_output_aliases={8: 0},
        compiler_params=_params(
            ("arbitrary",), [((tm, GW), F32)] * 3 + [((tm, D), BF16), ((D, GW), BF16)], 2 * _nbytes((tm, GW), F32)
        ),
    )(P, P, dya_b, Ga, lnv_g, lnv_b, wm, bsT, dP)


def _swap_halves(x):
    w = x.shape[1]
    lane = lax.broadcasted_iota(jnp.int32, x.shape, 1)
    return jnp.where((lane % HEAD_DIM) < ROPE_HALF, pltpu.roll(x, w - ROPE_HALF, 1), pltpu.roll(x, ROPE_HALF, 1))


def _lane_tile(t, width):
    reps = width // LANES
    return t if reps == 1 else jnp.tile(t, (1, reps))


def _rope(x, cos2, sin2):
    w = x.shape[1]
    return x * _lane_tile(cos2, w) + _swap_halves(x) * _lane_tile(sin2, w)


def _rope_bwd(g, cos2, sin2):
    w = g.shape[1]
    return g * _lane_tile(cos2, w) - _swap_halves(g) * _lane_tile(sin2, w)


PAIR = LANES // HEAD_DIM


def _swa_valid(i):
    s = lax.broadcasted_iota(jnp.int32, (2 * CHUNK, CHUNK), 0)
    t = lax.broadcasted_iota(jnp.int32, (2 * CHUNK, CHUNK), 1)
    cur = jnp.logical_and(s >= CHUNK, s - CHUNK <= t)
    prev = jnp.logical_and(jnp.logical_and(s < CHUNK, s > t), i > 0)
    return jnp.logical_or(cur, prev)


def _half_variants(x, odd):
    lane = lax.broadcasted_iota(jnp.int32, x.shape, 1)
    if odd:
        hi = jnp.where(lane >= HEAD_DIM, x, jnp.zeros_like(x))
        return pltpu.roll(hi, HEAD_DIM, 1), hi
    lo = jnp.where(lane < HEAD_DIM, x, jnp.zeros_like(x))
    return lo, pltpu.roll(lo, HEAD_DIM, 1)


def _swa_probs_t(kx, qp, sink, valid):
    s = jnp.where(valid, _dot(kx, qp, "nt") * (HEAD_DIM**-0.5), -jnp.inf)
    m = jnp.maximum(jnp.max(s, axis=0, keepdims=True), sink)
    e = jnp.exp(s - m)
    e_s = jnp.exp(sink - m)
    den = jnp.sum(e, axis=0, keepdims=True) + e_s
    return e / den, e_s / den


def _swa_load(q_ref, kc_ref, kp_ref, vc_ref, vp_ref, cc, sc, cp, sp):
    qr = _rope(q_ref[...], cc, sc).astype(BF16)
    kcat = jnp.concatenate([_rope(kp_ref[...], cp, sp), _rope(kc_ref[...], cc, sc)], axis=0)
    vcat = jnp.concatenate([vp_ref[...], vc_ref[...]], axis=0)
    return qr, kcat, vcat


def _swa_specs(d):
    _, off_q, off_k, off_va, _, _ = _offsets(d)
    assert off_q % d.AW == 0 and off_k % d.KVW == 0 and off_va % d.KVW == 0
    prev = lambda i: jnp.maximum(i - 1, 0)
    return [
        pl.BlockSpec(memory_space=pltpu.SMEM),
        pl.BlockSpec((CHUNK, d.AW), lambda i, o=off_q // d.AW: (i, o)),
        pl.BlockSpec((CHUNK, d.KVW), lambda i, o=off_k // d.KVW: (i, o)),
        pl.BlockSpec((CHUNK, d.KVW), lambda i, o=off_k // d.KVW: (prev(i), o)),
        pl.BlockSpec((CHUNK, d.KVW), lambda i, o=off_va // d.KVW: (i, o)),
        pl.BlockSpec((CHUNK, d.KVW), lambda i, o=off_va // d.KVW: (prev(i), o)),
        pl.BlockSpec((CHUNK, LANES), lambda i: (i, 0)),
        pl.BlockSpec((CHUNK, LANES), lambda i: (i, 0)),
        pl.BlockSpec((CHUNK, LANES), lambda i: (prev(i), 0)),
        pl.BlockSpec((CHUNK, LANES), lambda i: (prev(i), 0)),
    ]


def _swa_fwd(d, P, cos2, sin2, sinks, *, name, sched=None):
    T, AW, HQ, HKV = d.T, d.AW, d.HQ, d.HKV
    grp = HQ // HKV
    assert grp % PAIR == 0 and HKV % PAIR == 0

    def body(sink_ref, q_ref, kc_ref, kp_ref, vc_ref, vp_ref, cc_ref, sc_ref, cp_ref, sp_ref, o_ref):
        i = pl.program_id(0)
        qr, kcat, vcat = _swa_load(q_ref, kc_ref, kp_ref, vc_ref, vp_ref, cc_ref[...], sc_ref[...], cp_ref[...], sp_ref[...])
        valid = _swa_valid(i)
        for kv in range(HKV):
            col = slice((kv // PAIR) * LANES, (kv // PAIR + 1) * LANES)
            kx = [t.astype(BF16) for t in _half_variants(kcat[:, col], kv % PAIR)]
            vx = [t.astype(BF16) for t in _half_variants(vcat[:, col], kv % PAIR)]
            for pp in range(grp // PAIR):
                p = kv * (grp // PAIR) + pp
                qs = slice(p * LANES, (p + 1) * LANES)
                out = jnp.zeros((CHUNK, LANES), F32)
                for half in range(PAIR):
                    pt, _ = _swa_probs_t(kx[half], qr[:, qs], sink_ref[PAIR * p + half], valid)
                    out = out + _dot(pt.astype(BF16), vx[half], "tn")
                o_ref[:, qs] = out.astype(BF16)

    return _hosted(
        body,
        name=name,
        sched=sched,
        grid=(T // CHUNK,),
        in_specs=_swa_specs(d),
        out_specs=pl.BlockSpec((CHUNK, AW), lambda i: (i, 0)),
        out_shape=jax.ShapeDtypeStruct((T, AW), BF16),
        compiler_params=_params(("arbitrary",), [((CHUNK, AW), F32)] * 3),
    )(sinks, P, P, P, P, P, cos2, sin2, cos2, sin2)


def _swa_bwd(d, P, dyb_b, Gb, cos2, sin2, sinks, dP, *, name, sched=None):
    T, D, AW, KVW, HQ, HKV = d.T, d.D, d.AW, d.KVW, d.HQ, d.HKV
    grp = HQ // HKV
    nb = T // CHUNK
    scale = HEAD_DIM**-0.5
    off_q = _offsets(d)[1]
    assert grp % PAIR == 0 and HKV % PAIR == 0 and off_q % AW == 0

    def body(sink_ref, q_ref, kc_ref, kp_ref, vc_ref, vp_ref, cc_ref, sc_ref, cp_ref, sp_ref, dyb_ref, gb_ref, dp_in,
             dq_ref, dkv_ref, dsk_ref, acc_s, dq_s, dk_s, dv_s):
        i = pl.program_id(0)
        cc, sc, cp, sp = cc_ref[...], sc_ref[...], cp_ref[...], sp_ref[...]
        datt = _dot(dyb_ref[...], gb_ref[...], "nn").astype(BF16)
        qr, kcat, vcat = _swa_load(q_ref, kc_ref, kp_ref, vc_ref, vp_ref, cc, sc, cp, sp)
        valid = _swa_valid(i)
        lane1 = lax.broadcasted_iota(jnp.int32, (1, LANES), 1)
        lane2 = lax.broadcasted_iota(jnp.int32, (2 * CHUNK, LANES), 1)
        dsk = jnp.zeros((1, LANES), F32)
        for kvp in range(HKV // PAIR):
            col = slice(kvp * LANES, (kvp + 1) * LANES)
            dk_col = jnp.zeros((2 * CHUNK, LANES), F32)
            dv_col = jnp.zeros((2 * CHUNK, LANES), F32)
            for odd in range(PAIR):
                kv = kvp * PAIR + odd
                kx = [t.astype(BF16) for t in _half_variants(kcat[:, col], odd)]
                vx = [t.astype(BF16) for t in _half_variants(vcat[:, col], odd)]
                dk_half = [jnp.zeros((2 * CHUNK, LANES), F32) for _ in range(PAIR)]
                dv_half = [jnp.zeros((2 * CHUNK, LANES), F32) for _ in range(PAIR)]
                for pp in range(grp // PAIR):
                    p = kv * (grp // PAIR) + pp
                    qs = slice(p * LANES, (p + 1) * LANES)
                    qp, dop = qr[:, qs], datt[:, qs]
                    dq = jnp.zeros((CHUNK, LANES), F32)
                    for half in range(PAIR):
                        h = PAIR * p + half
                        pt, p_s = _swa_probs_t(kx[half], qp, sink_ref[h], valid)
                        dpt = _dot(vx[half], dop, "nt")
                        rs = jnp.sum(pt * dpt, axis=0, keepdims=True)
                        dst = (pt * (dpt - rs) * scale).astype(BF16)
                        dsk = dsk + jnp.where(lane1 == h, -jnp.sum(p_s * rs, axis=1, keepdims=True), 0.0)
                        dq = dq + _dot(dst, kx[half], "tn")
                        dk_half[half] = dk_half[half] + _dot(dst, qp, "nn")
                        dv_half[half] = dv_half[half] + _dot(pt.astype(BF16), dop, "nn")
                    dq_s[:, qs] = dq
                for acc_half, is_k in ((dk_half, True), (dv_half, False)):
                    lo = jnp.where(lane2 < HEAD_DIM, acc_half[0], 0.0)
                    hi = jnp.where(lane2 >= HEAD_DIM, acc_half[1], 0.0)
                    both = (pltpu.roll(lo, HEAD_DIM, 1) + hi) if odd else (lo + pltpu.roll(hi, HEAD_DIM, 1))
                    if is_k:
                        dk_col = dk_col + both
                    else:
                        dv_col = dv_col + both
            dk_s[:, col] = dk_col
            dv_s[:, col] = dv_col
        dq_ref[...] = _rope_bwd(dq_s[...], cc, sc).astype(BF16)
        cur = pl.ds(pl.multiple_of(i * CHUNK, CHUNK), CHUNK)
        acc_s[cur, :KVW] = _rope_bwd(dk_s[CHUNK:, :], cc, sc)
        acc_s[cur, KVW:] = dv_s[CHUNK:, :]

        @pl.when(i > 0)
        def _():
            prv = pl.ds(pl.multiple_of((i - 1) * CHUNK, CHUNK), CHUNK)
            acc_s[prv, :KVW] += _rope_bwd(dk_s[:CHUNK, :], cp, sp)
            acc_s[prv, KVW:] += dv_s[:CHUNK, :]

        _acc_store(dsk_ref, dsk, i == 0)

        @pl.when(i == nb - 1)
        def _():
            dkv_ref[...] = acc_s[...].astype(BF16)

    return _hosted(
        body,
        name=name,
        sched=sched,
        grid=(nb,),
        in_specs=_swa_specs(d) + [pl.BlockSpec((CHUNK, D), lambda i: (i, 0)), pl.BlockSpec((D, AW), lambda i: (0, 0)),
                                   pl.BlockSpec(memory_space=pl.ANY)],
        out_specs=[
            pl.BlockSpec((CHUNK, AW), lambda i, o=off_q // AW: (i, o)),
            pl.BlockSpec((T, 2 * KVW), lambda i: (0, 0)),
            pl.BlockSpec((1, LANES), lambda i: (0, 0)),
        ],
        out_shape=[
            jax.ShapeDtypeStruct(dP.shape, BF16),
            jax.ShapeDtypeStruct((T, 2 * KVW), BF16),
            jax.ShapeDtypeStruct((1, LANES), F32),
        ],
        input_output_aliases={12: 0},
        scratch_shapes=[
            pltpu.VMEM((T, 2 * KVW), F32),
            pltpu.VMEM((CHUNK, AW), F32),
            pltpu.VMEM((2 * CHUNK, KVW), F32),
            pltpu.VMEM((2 * CHUNK, KVW), F32),
        ],
        compiler_params=_params(
            ("arbitrary",), [((CHUNK, AW), F32)] * 3 + [((D, AW), BF16), ((T, 2 * KVW), BF16)], _nbytes((T, 2 * KVW), F32)
        ),
    )(sinks, P, P, P, P, P, cos2, sin2, cos2, sin2, dyb_b, Gb, dP)


def _place_cols(dst, src, off, *, name, sched=None):
    T, w = src.shape
    tm, tw = _tile(T, 512), _tile(w, 512)
    assert off % tw == 0

    def body(src_ref, dst_in, out_ref):
        out_ref[...] = src_ref[...]

    return _hosted(
        body,
        name=name,
        sched=sched,
        grid=(T // tm, w // tw),
        in_specs=[pl.BlockSpec((tm, tw), lambda i, j: (i, j)), pl.BlockSpec(memory_space=pl.ANY)],
        out_specs=pl.BlockSpec((tm, tw), lambda i, j, o=off // tw: (i, j + o)),
        out_shape=jax.ShapeDtypeStruct(dst.shape, dst.dtype),
        input_output_aliases={1: 0},
        compiler_params=_params(("arbitrary", "arbitrary"), [((tm, tw), dst.dtype)] * 2),
    )(src, dst)


def _branch_merge(d, sgu, att, Ga, Gb, P, b_gate, *, name, sched=None):
    T, D, GW, AW = d.T, d.D, d.GW, d.AW
    _, _, _, _, off_ga, off_gb = _offsets(d)
    tm, tn = _tile(T, 1024), _tile(D, 512)
    assert off_ga % tn == 0 and off_gb % tn == 0

    def body(s_ref, a_ref, ga_w, gb_w, ga_ref, gb_ref, bga_ref, bgb_ref, ya_ref, yb_ref, mg_ref):
        ya = _dot(s_ref[...], ga_w[...], "nt")
        yb = _dot(a_ref[...], gb_w[...], "nt")
        ya_ref[...] = ya
        yb_ref[...] = yb
        sa = jax.nn.sigmoid(ga_ref[...] + bga_ref[...])
        sb = jax.nn.sigmoid(gb_ref[...] + bgb_ref[...])
        mg_ref[...] = (sa * ya + sb * yb).astype(BF16)

    tile = pl.BlockSpec((tm, tn), lambda j, i: (i, j))
    return _hosted(
        body,
        name=name,
        sched=sched,
        grid=(D // tn, T // tm),
        in_specs=[
            pl.BlockSpec((tm, GW), lambda j, i: (i, 0)),
            pl.BlockSpec((tm, AW), lambda j, i: (i, 0)),
            pl.BlockSpec((tn, GW), lambda j, i: (j, 0)),
            pl.BlockSpec((tn, AW), lambda j, i: (j, 0)),
            pl.BlockSpec((tm, tn), lambda j, i, o=off_ga // tn: (i, j + o)),
            pl.BlockSpec((tm, tn), lambda j, i, o=off_gb // tn: (i, j + o)),
            pl.BlockSpec((1, tn), lambda j, i: (0, j)),
            pl.BlockSpec((1, tn), lambda j, i, o=D // tn: (0, j + o)),
        ],
        out_specs=[tile, tile, tile],
        out_shape=[jax.ShapeDtypeStruct((T, D), F32), jax.ShapeDtypeStruct((T, D), F32), jax.ShapeDtypeStruct((T, D), BF16)],
        compiler_params=_params(
            ("arbitrary", "arbitrary"),
            [((tm, GW), BF16), ((tm, AW), BF16), ((tn, GW), BF16), ((tn, AW), BF16)] + [((tm, tn), F32)] * 5,
        ),
    )(sgu, att, Ga, Gb, P, P, b_gate, b_gate)


def _xattn_probs(qh, kh):
    s = _dot(qh, kh, "nt") * (X_HEAD_DIM**-0.5)
    e = jnp.exp(s - jnp.max(s, axis=1, keepdims=True))
    return e / jnp.sum(e, axis=1, keepdims=True)


def _xattn_fwd(d, X, Xb, kv, Gxq, Gxo, ln_g, ln_b, *, name, sched=None):
    T, D, XW, XH, MEM = d.T, d.D, d.XW, d.XH, d.MEM
    tm = _tile(T, 256)

    def body(x_ref, xb_ref, kv_ref, wq_ref, wo_ref, g_ref, b_ref, y_ref, yb_ref, xh_ref, rs_ref, q_ref, o_ref, o_s):
        q = _dot(xb_ref[...], wq_ref[...], "nn").astype(BF16)
        q_ref[...] = q
        for h in range(XH):
            hs = slice(h * X_HEAD_DIM, (h + 1) * X_HEAD_DIM)
            p = _xattn_probs(q[:, hs], kv_ref[:, hs]).astype(BF16)
            o_s[:, hs] = _dot(p, kv_ref[:, XW + h * X_HEAD_DIM : XW + (h + 1) * X_HEAD_DIM], "nn").astype(BF16)
        o = o_s[...]
        o_ref[...] = o
        y, xhat, rstd = _ln_fwd(ALPHA * x_ref[...] + _dot(o, wo_ref[...], "nt"), g_ref[...], b_ref[...])
        y_ref[...] = y
        yb_ref[...] = y.astype(BF16)
        xh_ref[...] = xhat
        rs_ref[...] = rstd

    row = pl.BlockSpec((tm, D), lambda i: (i, 0))
    nar = pl.BlockSpec((tm, XW), lambda i: (i, 0))
    vec = pl.BlockSpec((1, D), lambda i: (0, 0))
    return _hosted(
        body,
        name=name,
        sched=sched,
        grid=(T // tm,),
        in_specs=[
            row,
            row,
            pl.BlockSpec((MEM, 2 * XW), lambda i: (0, 0)),
            pl.BlockSpec((D, XW), lambda i: (0, 0)),
            pl.BlockSpec((D, XW), lambda i: (0, 0)),
            vec,
            vec,
        ],
        out_specs=[row, row, row, pl.BlockSpec((tm, 1), lambda i: (i, 0)), nar, nar],
        out_shape=[
            jax.ShapeDtypeStruct((T, D), F32),
            jax.ShapeDtypeStruct((T, D), BF16),
            jax.ShapeDtypeStruct((T, D), F32),
            jax.ShapeDtypeStruct((T, 1), F32),
            jax.ShapeDtypeStruct((T, XW), BF16),
            jax.ShapeDtypeStruct((T, XW), BF16),
        ],
        scratch_shapes=[pltpu.VMEM((tm, XW), BF16)],
        compiler_params=_params(("arbitrary",), [((tm, D), F32)] * 4 + [((D, XW), BF16)] * 2),
    )(X, Xb, kv, Gxq, Gxo, ln_g, ln_b)


def _xattn_bwd(d, dz, dzb, q_b, kv, Gxq, Gxo, xhat, rstd, ln_g, *, name, sched=None):
    T, D, XW, XH, MEM = d.T, d.D, d.XW, d.XH, d.MEM
    tm = _tile(T, 256)
    scale = X_HEAD_DIM**-0.5

    def body(dz_ref, dzb_ref, q_ref, kv_ref, wq_ref, wo_ref, xh_ref, rs_ref, g_ref, dx_ref, dxb_ref, dg_ref, db_ref, dq_ref, dkv_ref, dq_s):
        first = pl.program_id(0) == 0
        do = _dot(dzb_ref[...], wo_ref[...], "nn").astype(BF16)
        for h in range(XH):
            hs = slice(h * X_HEAD_DIM, (h + 1) * X_HEAD_DIM)
            vs = slice(XW + h * X_HEAD_DIM, XW + (h + 1) * X_HEAD_DIM)
            kh, vh, qh, doh = kv_ref[:, hs], kv_ref[:, vs], q_ref[:, hs], do[:, hs]
            p = _xattn_probs(qh, kh)
            dp = _dot(doh, vh, "nt")
            ds = (p * (dp - jnp.sum(p * dp, axis=1, keepdims=True)) * scale).astype(BF16)
            dq_s[:, hs] = _dot(ds, kh, "nn").astype(BF16)
            _acc_store(dkv_ref.at[h], _dot(ds, qh, "tn"), first)
            _acc_store(dkv_ref.at[XH + h], _dot(p.astype(BF16), doh, "tn"), first)
        dq = dq_s[...]
        dq_ref[...] = dq
        dy = ALPHA * dz_ref[...] + _dot(dq, wq_ref[...], "nt")
        xh = xh_ref[...]
        dz_in = _ln_bwd_vals(dy, xh, rs_ref[...], g_ref[...])
        dx_ref[...] = dz_in
        dxb_ref[...] = dz_in.astype(BF16)
        _acc_store(dg_ref, jnp.sum(dy * xh, axis=0, keepdims=True), first)
        _acc_store(db_ref, jnp.sum(dy, axis=0, keepdims=True), first)

    row = pl.BlockSpec((tm, D), lambda i: (i, 0))
    nar = pl.BlockSpec((tm, XW), lambda i: (i, 0))
    vec = pl.BlockSpec((1, D), lambda i: (0, 0))
    return _hosted(
        body,
        name=name,
        sched=sched,
        grid=(T // tm,),
        in_specs=[
            row,
            row,
            nar,
            pl.BlockSpec((MEM, 2 * XW), lambda i: (0, 0)),
            pl.BlockSpec((D, XW), lambda i: (0, 0)),
            pl.BlockSpec((D, XW), lambda i: (0, 0)),
            row,
            pl.BlockSpec((tm, 1), lambda i: (i, 0)),
            vec,
        ],
        out_specs=[row, row, vec, vec, nar, pl.BlockSpec((2 * XH, MEM, X_HEAD_DIM), lambda i: (0, 0, 0))],
        out_shape=[
            jax.ShapeDtypeStruct((T, D), F32),
            jax.ShapeDtypeStruct((T, D), BF16),
            jax.ShapeDtypeStruct((1, D), F32),
            jax.ShapeDtypeStruct((1, D), F32),
            jax.ShapeDtypeStruct((T, XW), BF16),
            jax.ShapeDtypeStruct((2 * XH, MEM, X_HEAD_DIM), F32),
        ],
        scratch_shapes=[pltpu.VMEM((tm, XW), BF16)],
        compiler_params=_params(("arbitrary",), [((tm, D), F32)] * 5 + [((D, XW), BF16)] * 2),
    )(dz, dzb, q_b, kv, Gxq, Gxo, xhat, rstd, ln_g)


def _resid_ln_outs():
    return (("mn", F32), ("mn", BF16), ("mn", F32), ("m1", F32))


def _layer_fwd(d, X, Xb, memb, G, S, cos2, sin2, tag, sched=None):
    D = d.D
    mm = functools.partial(_matmul, sched=sched)
    P = mm(Xb, G["in"], "nt", name=f"proj_{tag}", tm=1024, tn=_tile(d.IN, 1280))
    sgu = _gmlp_fwd(d, P, S["ln_v_g"], S["ln_v_b"], S["w_s"], S["b_sT"], name=f"gmlp_fwd_{tag}", sched=sched)
    att = _swa_fwd(d, P, cos2, sin2, S["sinks"], name=f"swa_fwd_{tag}", sched=sched)
    ya, yb, merged = _branch_merge(d, sgu, att, G["a"], G["b"], P, S["b_gate"], name=f"merge_{tag}", sched=sched)
    X1, X1b, xh1, rs1 = mm(
        merged, G["o"], "nn", name=f"oproj_ln_{tag}", tn=D, tk=1024, out_defs=_resid_ln_outs(), epilogue=_ep_resid_ln,
        extras=((X, "mn", 0), (S["ln1_g"], "row", 0), (S["ln1_b"], "row", 0)),
    )
    kv = mm(memb, G["xkv"], "nn", name=f"xkv_{tag}", out_defs=(("mn", BF16),))
    X2, X2b, xh2, rs2, q_b, o_b = _xattn_fwd(d, X1, X1b, kv, G["xq"], G["xo"], S["ln2_g"], S["ln2_b"], name=f"xattn_fwd_{tag}", sched=sched)
    H, A = mm(X2b, G["up"], "nt", name=f"up_{tag}", tm=1024, out_defs=(("mn", F32), ("mn", BF16)), epilogue=_ep_relu2)
    X3, X3b, xh3, rs3 = mm(
        A, G["down"], "nn", name=f"down_ln_{tag}", tn=D, tk=1024, out_defs=_resid_ln_outs(), epilogue=_ep_resid_ln,
        extras=((X2, "mn", 0), (S["ln3_g"], "row", 0), (S["ln3_b"], "row", 0)),
    )
    saved = dict(Xb=Xb, P=P, sgu=sgu, att=att, ya=ya, yb=yb, merged=merged, X1b=X1b, xh1=xh1, rs1=rs1, kv=kv, q_b=q_b,
                 o_b=o_b, X2b=X2b, xh2=xh2, rs2=rs2, H=H, A=A, xh3=xh3, rs3=rs3)
    return X3, X3b, saved


def _layer_bwd(d, dXout, sv, memb, G, S, cos2, sin2, tag, sched=None, on_grad=None, target=None):
    D = d.D
    mm = functools.partial(_matmul, sched=sched)
    emit = on_grad if on_grad is not None else (lambda n, g: None)
    wide = dict(tn=D)
    _, _, off_k, _, off_ga, off_gb = _offsets(d)
    bf = (("mn", BF16),)
    dz3, dz3b, dg3, db3, *loss = _ln_bwd(dXout, sv["xh3"], sv["rs3"], S["ln3_g"], name=f"ln3_bwd_{tag}", sched=sched, target=target)
    dHb = mm(dz3b, G["down"], "nt", name=f"down_dx_{tag}", tm=1024, out_defs=bf, epilogue=_ep_relu2_bwd, extras=((sv["H"], "mn", 0),))
    g_down = mm(sv["A"], dz3b, "tn", name=f"down_dw_{tag}", **wide)
    emit("down", g_down)
    ln_bwd_outs = (("mn", F32), ("mn", BF16), ("acc", F32), ("acc", F32))
    dz2, dz2b, dg2, db2 = mm(
        dHb, G["up"], "nn", name=f"up_dx_{tag}", tn=D, tk=1024, out_defs=ln_bwd_outs, epilogue=_ep_resid_ln_bwd,
        extras=((dz3, "mn", 0), (sv["xh2"], "mn", 0), (sv["rs2"], "m1", 0), (S["ln2_g"], "row", 0)),
    )
    g_up = mm(dHb, sv["X2b"], "tn", name=f"up_dw_{tag}", **wide)
    emit("up", g_up)
    dz1, dz1b, dg1, db1, dq_b, dkv = _xattn_bwd(d, dz2, dz2b, sv["q_b"], sv["kv"], G["xq"], G["xo"], sv["xh1"], sv["rs1"], S["ln1_g"],
                                               name=f"xattn_bwd_{tag}", sched=sched)
    g_xo = mm(dz2b, sv["o_b"], "tn", name=f"xo_dw_{tag}")
    emit("xo", g_xo)
    g_xq = mm(sv["X1b"], dq_b, "tn", name=f"xq_dw_{tag}")
    emit("xq", g_xq)
    dkv_b = dkv.transpose(1, 0, 2).reshape(d.MEM, 2 * d.XW).astype(BF16)
    g_xkv = mm(memb, dkv_b, "tn", name=f"xkv_dw_{tag}")
    emit("xkv", g_xkv)
    dya_b, dyb_b, dP, dgb_b, dbga, dbgb = mm(
        dz1b, G["o"], "nt", name=f"oproj_dx_{tag}", tm=1024, tn=512,
        out_defs=(("mn", BF16), ("mn", BF16), ("cols", BF16, d.IN, off_ga), ("mn", BF16), ("acc", F32), ("acc", F32)),
        epilogue=_ep_gate_bwd,
        extras=((sv["P"], "mn", off_ga), (sv["P"], "mn", off_gb), (sv["ya"], "mn", 0), (sv["yb"], "mn", 0),
                (S["b_gate"], "row", 0), (S["b_gate"], "row", D)),
    )
    g_o = mm(sv["merged"], dz1b, "tn", name=f"oproj_dw_{tag}", **wide)
    emit("o", g_o)
    dP = _place_cols(dP, dgb_b, off_gb, name=f"place_dgb_{tag}")
    dP, dwm, dbsT, dlvg, dlvb = _gmlp_bwd(d, sv["P"], dya_b, G["a"], S["ln_v_g"], S["ln_v_b"], S["w_s"], S["b_sT"], dP, name=f"gmlp_bwd_{tag}", sched=sched)
    g_a = mm(dya_b, sv["sgu"], "tn", name=f"bra_dw_{tag}")
    emit("a", g_a)
    dP, dkvr_b, dsk = _swa_bwd(d, sv["P"], dyb_b, G["b"], cos2, sin2, S["sinks"], dP, name=f"swa_bwd_{tag}", sched=sched)
    g_b = mm(dyb_b, sv["att"], "tn", name=f"brb_dw_{tag}")
    emit("b", g_b)
    dP = _place_cols(dP, dkvr_b, off_k, name=f"place_dkv_{tag}")
    small = dict(b_gate=jnp.concatenate([dbga, dbgb], axis=1), ln_v_g=dlvg, ln_v_b=dlvb, w_s=dwm, b_sT=dbsT, sinks=dsk,
                 ln1_g=dg1, ln1_b=db1, ln2_g=dg2, ln2_b=db2, ln3_g=dg3, ln3_b=db3)
    emit("small", small)
    g_in = mm(dP, sv["Xb"], "tn", name=f"proj_dw_{tag}", **wide)
    emit("in", g_in)
    dXin = mm(dP, G["in"], "nn", name=f"proj_dx_{tag}", tn=D, tk=_tile(d.IN, 1920), epilogue=_ep_resid, extras=((dz1, "mn", 0),))
    big = {"in": g_in, "a": g_a, "b": g_b, "o": g_o, "xq": g_xq, "xkv": g_xkv, "xo": g_xo, "up": g_up, "down": g_down}
    return dXin, big, small, (loss[0] if loss else None)


def _rope_tables(T):
    inv = 1.0 / (ROPE_THETA ** (jnp.arange(0, HEAD_DIM, 2, dtype=F32) / HEAD_DIM))
    ang = jnp.arange(T, dtype=F32)[:, None] * inv[None, :]
    cos, sin = jnp.cos(ang), jnp.sin(ang)
    reps = LANES // HEAD_DIM
    return jnp.concatenate([cos, cos] * reps, axis=1), jnp.concatenate([-sin, sin] * reps, axis=1)


def _local_step(d, x, mem, target, Gs, Ss, sched=None, on_grad=None):
    cos2, sin2 = _rope_tables(d.T)
    memb = mem.astype(BF16)
    X, Xb = x, x.astype(BF16)
    saved = []
    for l in range(DEPTH):
        X, Xb, sv = _layer_fwd(d, X, Xb, memb, Gs[l], Ss[l], cos2, sin2, f"l{l}", sched)
        saved.append(sv)
    bigs, smalls = [None] * DEPTH, [None] * DEPTH
    dX, loss = X, None
    for l in reversed(range(DEPTH)):
        emit = None if on_grad is None else functools.partial(on_grad, l)
        head = target if l == DEPTH - 1 else None
        dX, bigs[l], smalls[l], got = _layer_bwd(d, dX, saved[l], memb, Gs[l], Ss[l], cos2, sin2, f"l{l}", sched, emit, head)
        loss = got if got is not None else loss
    return loss, dX, bigs, smalls


def _place():
    x, y, c = lax.axis_index("x"), lax.axis_index("y"), lax.axis_index("c")
    return x, y, c, [(1 - x, y), (x, 1 - y), (1 - x, 1 - y)]


class _Task:
    def __init__(self, ins, out_shapes, n_remote, n_local, plan, done, aliases=None):
        self.ins, self.out_shapes, self.n_remote, self.n_local = list(ins), list(out_shapes), n_remote, n_local
        self.plan, self.done, self.aliases = plan, done, dict(aliases or {})


class _Sched:
    def __init__(self):
        self.pending = {}

    def at(self, host, task):
        self.pending.setdefault(host, []).append(task)

    def take(self, host):
        return self.pending.pop(host, [])


def _in_hbm(out_shape):
    if isinstance(out_shape, (list, tuple)):
        return [_in_hbm(s) for s in out_shape]
    return pltpu.HBM(out_shape.shape, out_shape.dtype)


def _hosted(body, *, name, sched=None, tasks=(), grid=(), in_specs=None, out_specs=None, out_shape=(), scratch_shapes=(),
            compiler_params=None, input_output_aliases=None):
    tasks = list(tasks) + (sched.take(name) if sched is not None else [])
    scratch_shapes = list(scratch_shapes)
    kwargs = dict(name=name)
    core_aliases = dict(input_output_aliases or {})
    if grid:
        kwargs["grid"] = grid
    if compiler_params is not None:
        kwargs["compiler_params"] = compiler_params
    if not tasks:
        if in_specs is not None:
            kwargs.update(in_specs=in_specs, out_specs=out_specs)
        return pl.pallas_call(body, out_shape=_in_hbm(out_shape), scratch_shapes=scratch_shapes, input_output_aliases=core_aliases, **kwargs)
    assert in_specs is not None and out_specs is not None, name
    single = not isinstance(out_shape, (list, tuple))
    core_shape = [out_shape] if single else list(out_shape)
    core_specs = [out_specs] if single else list(out_specs)
    in_specs = list(in_specs)
    n_in, n_out, n_scr = len(in_specs), len(core_shape), len(scratch_shapes)
    t_ins = [a for t in tasks for a in t.ins]
    t_outs = [s for t in tasks for s in t.out_shapes]
    n_rem = sum(t.n_remote for t in tasks)
    n_loc = sum(t.n_local for t in tasks)
    aliases, pos_in, pos_out = dict(core_aliases), n_in, n_out
    for t in tasks:
        for a, b in t.aliases.items():
            aliases[pos_in + a] = pos_out + b
        pos_in += len(t.ins)
        pos_out += len(t.out_shapes)
    any_spec = pl.BlockSpec(memory_space=pl.ANY)

    def wrapped(*refs):
        core_in, t_in = refs[:n_in], refs[n_in : n_in + len(t_ins)]
        o0 = n_in + len(t_ins)
        core_out, t_out = refs[o0 : o0 + n_out], refs[o0 + n_out : o0 + n_out + len(t_outs)]
        s0 = o0 + n_out + len(t_outs)
        core_scr = refs[s0 : s0 + n_scr]
        send_sems, recv_sems, local_sems = refs[s0 + n_scr :]
        remote, local = [], []
        pi = po = pr = pq = 0
        for t in tasks:
            r, q = t.plan(t_in[pi : pi + len(t.ins)], t_out[po : po + len(t.out_shapes)], send_sems, recv_sems, local_sems, pr, pq)
            assert len(r) == t.n_remote and len(q) == t.n_local
            remote += r
            local += q
            pi, po, pr, pq = pi + len(t.ins), po + len(t.out_shapes), pr + t.n_remote, pq + t.n_local

        def start():
            for cp in local + remote:
                cp.start()

        def finish():
            for cp in remote + local:
                cp.wait()

        if grid:
            ids = [pl.program_id(a) for a in range(len(grid))]
            first = functools.reduce(jnp.logical_and, [i == 0 for i in ids])
            last = functools.reduce(jnp.logical_and, [i == g - 1 for i, g in zip(ids, grid)])
            pl.when(first)(start)
            body(*core_in, *core_out, *core_scr)
            pl.when(last)(finish)
        else:
            start()
            body(*core_in, *core_out, *core_scr)
            finish()

    call = pl.pallas_call(
        wrapped,
        in_specs=in_specs + [any_spec] * len(t_ins),
        out_specs=core_specs + [any_spec] * len(t_outs),
        out_shape=_in_hbm(core_shape + t_outs),
        scratch_shapes=scratch_shapes
        + [pltpu.SemaphoreType.DMA((n_rem,)), pltpu.SemaphoreType.DMA((n_rem,)), pltpu.SemaphoreType.DMA((max(n_loc, 1),))],
        input_output_aliases=aliases,
        **kwargs,
    )

    def run(*operands):
        outs = call(*operands, *t_ins)
        po = n_out
        for t in tasks:
            t.done(list(outs[po : po + len(t.out_shapes)]))
            po += len(t.out_shapes)
        return outs[0] if single else list(outs[:n_out])

    return run


def _comm_only(tasks, *, name):
    _hosted(lambda: None, name=name, tasks=tasks, in_specs=[], out_shape=[], out_specs=[])()


def _rows(ref, block, n_blocks):
    r = ref.shape[0] // n_blocks
    return ref.at[pl.ds(block * r, r), :]


def _remote(src, dst, send_sems, recv_sems, k, to):
    return pltpu.make_async_remote_copy(src_ref=src, dst_ref=dst, send_sem=send_sems.at[k], recv_sem=recv_sems.at[k],
                                        device_id=to, device_id_type=MESH)


def _block_rows(ref, px, py, pc):
    return _rows(ref, 4 * px + 2 * py + pc, N_DEV)


ICI_SPLIT = 2


def _row_parts(ref, n_parts):
    r = ref.shape[0] // n_parts
    return [ref.at[pl.ds(k * r, r), :] for k in range(n_parts)]


def _gather_stage1(shards, done):
    n = len(shards)
    per = 1 + 2 * ICI_SPLIT

    def plan(ins, outs, send_sems, recv_sems, local_sems, pr, pq):
        x, y, c, _ = _place()
        remote, local = [], []
        for w in range(n):
            dst = _block_rows(outs[w], x, y, c)
            local.append(pltpu.make_async_copy(ins[w], dst, local_sems.at[pq + w]))
            remote.append(_remote(ins[w], dst, send_sems, recv_sems, pr + per * w, (x, y, 1 - c)))
            for j, to in enumerate([(1 - x, y, c), (x, 1 - y, c)]):
                for k, (s, t) in enumerate(zip(_row_parts(ins[w], ICI_SPLIT), _row_parts(dst, ICI_SPLIT))):
                    remote.append(_remote(s, t, send_sems, recv_sems, pr + per * w + 1 + ICI_SPLIT * j + k, to))
        return remote, local

    shapes = [jax.ShapeDtypeStruct((N_DEV * s.shape[0], s.shape[1]), s.dtype) for s in shards]
    return _Task(shards, shapes, per * n, n, plan, done)


def _gather_stage2(partial, done):
    n = len(partial)
    per = ICI_SPLIT + 2

    def plan(ins, outs, send_sems, recv_sems, local_sems, pr, pq):
        x, y, c, _ = _place()
        along_y = c == 1
        from_chip = (jnp.where(along_y, 1 - x, x), jnp.where(along_y, y, 1 - y))
        to_chip = (jnp.where(along_y, x, 1 - x), jnp.where(along_y, 1 - y, y))
        remote = []
        for w in range(n):
            for k, rows in enumerate(_row_parts(_block_rows(outs[w], *from_chip, c), ICI_SPLIT)):
                remote.append(_remote(rows, rows, send_sems, recv_sems, pr + per * w + k, (*to_chip, c)))
            for j, chip in enumerate([(1 - x, y), (x, 1 - y)]):
                rows = _block_rows(outs[w], *chip, c)
                remote.append(_remote(rows, rows, send_sems, recv_sems, pr + per * w + ICI_SPLIT + j, (x, y, 1 - c)))
        return remote, []

    shapes = [jax.ShapeDtypeStruct(p.shape, p.dtype) for p in partial]
    return _Task(partial, shapes, per * n, 0, plan, done, aliases={w: w for w in range(n)})


def _gather_stage3(partial, done):
    n = len(partial)

    def plan(ins, outs, send_sems, recv_sems, local_sems, pr, pq):
        x, y, c, _ = _place()
        remote = []
        for w in range(n):
            rows = _block_rows(outs[w], 1 - x, 1 - y, c)
            remote.append(_remote(rows, rows, send_sems, recv_sems, pr + w, (x, y, 1 - c)))
        return remote, []

    shapes = [jax.ShapeDtypeStruct(p.shape, p.dtype) for p in partial]
    return _Task(partial, shapes, n, 0, plan, done, aliases={w: w for w in range(n)})


def _gather_chain(arrays, hosts, sched, finished, label):
    stages = (_gather_stage1, _gather_stage2, _gather_stage3)

    def run(k, operands):
        def done(outs):
            if k + 1 < len(stages):
                run(k + 1, outs)
            else:
                finished(outs)

        task = stages[k](operands, done)
        if hosts[k] is None:
            _comm_only([task], name=f"{label}_stage{k + 1}")
        else:
            sched.at(hosts[k], task)

    run(0, list(arrays))


def _sibling_stage(grads, done):
    n = len(grads)

    def plan(ins, outs, send_sems, recv_sems, local_sems, pr, pq):
        x, y, c, _ = _place()
        remote = []
        for w in range(n):
            for k in range(4):
                src = _rows(ins[w], 4 * (k // 2) + 2 * (k % 2) + (1 - c), N_DEV)
                remote.append(_remote(src, _rows(outs[w], k, 4), send_sems, recv_sems, pr + 4 * w + k, (x, y, 1 - c)))
        return remote, []

    shapes = [jax.ShapeDtypeStruct((g.shape[0] // 2, g.shape[1]), g.dtype) for g in grads]
    return _Task(grads, shapes, 4 * n, 0, plan, done)


def _chips_stage(sends, done, part=(0, 1), into=None):
    n = len(sends)
    k, n_parts = part

    def plan(ins, outs, send_sems, recv_sems, local_sems, pr, pq):
        x, y, c, chips = _place()
        remote = []
        for w in range(n):
            r = sends[w].shape[0] // 3
            rp = r // n_parts
            for j, chip in enumerate(chips):
                rows = pl.ds(j * r + k * rp, rp)
                remote.append(_remote(ins[w].at[rows, :], outs[w].at[rows, :], send_sems, recv_sems, pr + 3 * w + j, (*chip, c)))
        return remote, []

    shapes = [jax.ShapeDtypeStruct(s.shape, s.dtype) for s in sends]
    if into is None:
        return _Task(sends, shapes, 3 * n, 0, plan, done)
    return _Task(list(sends) + list(into), shapes, 3 * n, 0, plan, done, aliases={n + w: w for w in range(n)})


def _pair_sum(grad, got, place, *, name):
    R, C = grad.shape
    r = R // N_DEV
    tr = _tile(r, 256, 16)
    nt = r // tr

    def chip_of(s, pr):
        fx = jnp.where((s == 1) | (s == 3), 1, 0)
        fy = jnp.where(s >= 2, 1, 0)
        return pr[0] ^ fx, pr[1] ^ fy

    def grad_map(s, t, pr):
        kx, ky = chip_of(s, pr)
        return ((4 * kx + 2 * ky + pr[2]) * nt + t, 0)

    def got_map(s, t, pr):
        kx, ky = chip_of(s, pr)
        return ((2 * kx + ky) * nt + t, 0)

    def body(pr, g_ref, r_ref, own_ref, send_ref):
        s = pl.program_id(0)
        tot = g_ref[...] + r_ref[...]

        @pl.when(s == 0)
        def _():
            own_ref[...] = tot

        @pl.when(s > 0)
        def _():
            send_ref[...] = tot.astype(BF16)

    return pl.pallas_call(
        body,
        name=name,
        grid_spec=pltpu.PrefetchScalarGridSpec(
            num_scalar_prefetch=1,
            grid=(4, nt),
            in_specs=[pl.BlockSpec((tr, C), grad_map), pl.BlockSpec((tr, C), got_map)],
            out_specs=[
                pl.BlockSpec((tr, C), lambda s, t, pr: (jnp.where(s == 0, t, nt - 1), 0)),
                pl.BlockSpec((tr, C), lambda s, t, pr: (jnp.where(s == 0, 0, (s - 1) * nt + t), 0)),
            ],
        ),
        out_shape=[jax.ShapeDtypeStruct((r, C), F32), jax.ShapeDtypeStruct((3 * r, C), BF16)],
        compiler_params=_params(("arbitrary", "arbitrary"), [((tr, C), F32)] * 4),
    )(place, grad, got)


def _adam_vals(w, g, m, v):
    m = ADAM_B1 * m + (1.0 - ADAM_B1) * g
    v = ADAM_B2 * v + (1.0 - ADAM_B2) * (g * g)
    m_hat = m / (1.0 - ADAM_B1**ADAM_STEP)
    v_hat = v / (1.0 - ADAM_B2**ADAM_STEP)
    delta = -ADAM_LR * (m_hat / (jnp.sqrt(v_hat) + ADAM_EPS) + ADAM_WD * w)
    return delta, m, v


def _adam_big(w3, m3, v3, own, got, layer, transposed, prev, *, name, sched=None):
    _, A, B = w3.shape
    ta = _tile(A, 256, 16)
    nt = A // ta
    b_pad = (-B) % LANES

    def body(*refs):
        w_ref, m_ref, v_ref, own_ref, g1_ref, g2_ref, g3_ref = refs[:7]
        g_ref, d_ref, nm_ref, nv_ref = refs[-4:]
        g = ((own_ref[...] + g1_ref[...].astype(F32)) + g2_ref[...].astype(F32)) + g3_ref[...].astype(F32)
        if transposed:
            if b_pad:
                g = jnp.concatenate([g, jnp.zeros((b_pad, ta), F32)], axis=0)
            g = g.T[:, :B]
        g_ref[...] = g
        d_ref[...], nm_ref[...], nv_ref[...] = _adam_vals(w_ref[...], g, m_ref[...], v_ref[...])

    nat = pl.BlockSpec((None, ta, B), lambda t: (layer, t, 0))
    if transposed:
        parts = [pl.BlockSpec((B, ta), lambda t: (0, t))] + [pl.BlockSpec((B, ta), lambda t, j=j: (j, t)) for j in range(3)]
    else:
        parts = [pl.BlockSpec((ta, B), lambda t: (t, 0))] + [pl.BlockSpec((ta, B), lambda t, j=j: (j * nt + t, 0)) for j in range(3)]
    keep = [] if prev is None else list(prev)
    outs = _hosted(
        body,
        name=name,
        sched=sched,
        grid=(nt,),
        in_specs=[nat, nat, nat] + parts + [pl.BlockSpec(memory_space=pl.ANY)] * len(keep),
        out_specs=[nat] * 4,
        out_shape=[jax.ShapeDtypeStruct(w3.shape, F32)] * 4,
        input_output_aliases={7 + k: k for k in range(len(keep))},
        compiler_params=_params(("arbitrary",), [((ta, B), F32)] * 10),
    )(w3, m3, v3, own, got, got, got, *keep)
    return list(outs)


def _adam_small(w, m, v, parts, *, name, sched=None):
    R = w.shape[0]

    def body(w_ref, m_ref, v_ref, p_ref, g_ref, d_ref, nm_ref, nv_ref):
        g = p_ref[pl.ds(0, R), :]
        for k in range(1, N_DEV):
            g = g + p_ref[pl.ds(k * R, R), :]
        g_ref[...] = g
        d_ref[...], nm_ref[...], nv_ref[...] = _adam_vals(w_ref[...], g, m_ref[...], v_ref[...])

    return _hosted(
        body,
        name=name,
        sched=sched,
        out_shape=[jax.ShapeDtypeStruct((R, LANES), F32)] * 4,
        compiler_params=_params(None, [((R, LANES), F32)] * 16),
    )(w, m, v, parts)


SMALL_NAMES = ("b_gate", "ln_v_g", "ln_v_b", "w_s", "b_s", "sinks", "ln1_g", "ln1_b", "ln2_g", "ln2_b", "ln3_g", "ln3_b")
PACK_ALIGN = 8 * LANES


def _pack(arrays):
    flat = []
    for a in arrays:
        a = a.reshape(-1)
        flat.append(jnp.pad(a, (0, (-a.shape[0]) % PACK_ALIGN)))
    return jnp.concatenate(flat).reshape(-1, LANES)


def _unpack(packed, shapes):
    flat = packed.reshape(-1)
    out, pos = [], 0
    for s in shapes:
        n = 1
        for e in s:
            n *= e
        out.append(flat[pos : pos + n].reshape(s))
        pos += n + (-n) % PACK_ALIGN
    return out


BIG = (("in", "w_in", True), ("a", "w_br_a", True), ("b", "w_br_b", True), ("o", "w_o", False), ("xq", "w_xq", False),
       ("xkv", "w_xkv", False), ("xo", "w_xo", True), ("up", "w_up", True), ("down", "w_down", False))


SMALL_BIG = ("a", "b", "o", "xq", "xkv", "xo")
GATHER_PLAN = (
    (0, ("in",), (None, None, None)),
    (0, SMALL_BIG, ("proj_l0", "gmlp_fwd_l0", "swa_fwd_l0")),
    (0, ("up",), ("swa_fwd_l0", "merge_l0", "oproj_ln_l0")),
    (0, ("down",), ("oproj_ln_l0", "xattn_fwd_l0", "up_l0")),
    (1, ("in",), ("up_l0", "down_ln_l0", None)),
    (1, SMALL_BIG, ("down_ln_l0", "proj_l1", "gmlp_fwd_l1")),
    (1, ("up",), ("proj_l1", "swa_fwd_l1", "merge_l1")),
    (1, ("down",), ("merge_l1", "oproj_ln_l1", "xattn_fwd_l1")),
)
EARLY_SMALL_BIG = ("o", "xq", "xkv", "xo")
LATE_SMALL_BIG = ("a", "b")
GRAD_HOSTS = {
    1: {"down": ("xattn_bwd_l1", "swa_bwd_l1"), "up": ("xattn_bwd_l1", "proj_dw_l1"),
        **{g: ("gmlp_bwd_l1", "proj_dx_l1") for g in EARLY_SMALL_BIG}, **{g: ("proj_dw_l1", "proj_dx_l1") for g in LATE_SMALL_BIG},
        "in": ("proj_dx_l1", "down_dx_l0")},
    0: {"down": ("xattn_bwd_l0", ("oproj_dx_l0", "gmlp_bwd_l0")), "up": ("xattn_bwd_l0", "swa_bwd_l0"),
        **{g: ("gmlp_bwd_l0", "proj_dw_l0") for g in EARLY_SMALL_BIG}, **{g: ("proj_dw_l0", "proj_dx_l0") for g in LATE_SMALL_BIG},
        "in": (None, "proj_dx_l0")},
}
SMALL_GRAD_HOSTS = ("proj_dw_l0", "proj_dx_l0", None)


def kernel(x, mem, w_in, b_gate, ln_v_g, ln_v_b, w_s, b_s, sinks, w_br_a, w_br_b, w_o, ln1_g, ln1_b, w_xq, w_xkv, w_xo, ln2_g, ln2_b, w_up, w_down, ln3_g, ln3_b, loss_target, m_w_in, m_b_gate, m_ln_v_g, m_ln_v_b, m_w_s, m_b_s, m_sinks, m_w_br_a, m_w_br_b, m_w_o, m_ln1_g, m_ln1_b, m_w_xq, m_w_xkv, m_w_xo, m_ln2_g, m_ln2_b, m_w_up, m_w_down, m_ln3_g, m_ln3_b, v_w_in, v_b_gate, v_ln_v_g, v_ln_v_b, v_w_s, v_b_s, v_sinks, v_w_br_a, v_w_br_b, v_w_o, v_ln1_g, v_ln1_b, v_w_xq, v_w_xkv, v_w_xo, v_ln2_g, v_ln2_b, v_w_up, v_w_down, v_ln3_g, v_ln3_b):
    given = dict(locals())
    T, D = x.shape[1], x.shape[2]
    IN, GW, AW, XW, DFF = w_in.shape[2] * N_DEV, ln_v_g.shape[1], w_br_b.shape[1], w_xq.shape[2], w_up.shape[2] * N_DEV
    KVW = (IN - 2 * GW - AW - 2 * D) // 2
    d = Dims(T=T, D=D, IN=IN, GW=GW, G=GW // GROUP_DIM, AW=AW, KVW=KVW, HQ=AW // HEAD_DIM, HKV=KVW // HEAD_DIM, XW=XW,
             XH=XW // X_HEAD_DIM, MEM=mem.shape[1], DFF=DFF)
    place = jnp.stack([lax.axis_index("x"), lax.axis_index("y"), lax.axis_index("c")]).astype(jnp.int32)

    sched = _Sched()
    big_of = {g: (p, tr) for g, p, tr in BIG}

    def shard(l, g):
        p, tr = big_of[g]
        return (given[p][l].T if tr else given[p][l]).astype(BF16)

    Gs = [{}, {}]

    def plan_gather(l, names, hosts):
        _gather_chain([shard(l, g) for g in names], hosts, sched, lambda full: Gs[l].update(zip(names, full)),
                      f"gather_{names[0]}_l{l}")

    for entry in GATHER_PLAN:
        plan_gather(*entry)
    Ss = []
    for l in range(DEPTH):
        S = {n: given[n][l].reshape(1, -1) for n in SMALL_NAMES if n not in ("w_s", "b_s", "sinks")}
        S["w_s"], S["b_sT"], S["sinks"] = w_s[l], b_s[l].T, sinks[l]
        Ss.append(S)

    owns, recvs = {}, {}

    smalls_seen = {}
    gathered_small = []

    def pack_small(src):
        parts = []
        for l in range(DEPTH):
            for n in SMALL_NAMES:
                a = src[l]["b_sT"].T if n == "b_s" else src[l][n]
                parts.append(a[0, : d.HQ] if n == "sinks" else a)
        return _pack(parts)

    def on_small(l, small):
        smalls_seen[l] = small
        if len(smalls_seen) == DEPTH:
            _gather_chain([pack_small(smalls_seen)], SMALL_GRAD_HOSTS, sched, lambda full: gathered_small.append(full[0]),
                          "gather_small_grads")

    def on_grad(l, g, grad):
        if g == "small":
            return on_small(l, grad)
        sib_host, chips_host = GRAD_HOSTS[l][g]

        def after_sibling(got):
            owns[(l, g)], send = _pair_sum(grad, got[0], place, name=f"grad_pair_sum_{g}_l{l}")
            hosts = chips_host if isinstance(chips_host, tuple) else (chips_host,)

            def register(k, into):
                def done(r):
                    if k + 1 < len(hosts):
                        register(k + 1, r)
                    else:
                        recvs[(l, g)] = r[0]

                task = _chips_stage([send], done, part=(k, len(hosts)), into=into)
                if hosts[k] is None:
                    _comm_only([task], name=f"grad_chips_{g}_l{l}")
                else:
                    sched.at(hosts[k], task)

            register(0, None)

        task = _sibling_stage([grad], after_sibling)
        if sib_host is None:
            _comm_only([task], name=f"grad_sibling_{g}_l{l}")
        else:
            sched.at(sib_host, task)

    loss, dX, bigs, smalls = _local_step(d, x[0], mem[0], loss_target[0], Gs, Ss, sched, on_grad)
    loss = lax.psum(loss[0, 0], ("x", "y", "c"))

    assert not sched.pending, sorted(sched.pending)

    def viewed(p, tr):
        return tr and given[p].shape[2] % LANES != 0

    res = {p: None for _, p, _ in BIG}
    for l in reversed(range(DEPTH)):
        for g, p, tr in BIG:
            view = viewed(p, tr)
            w3, m3, v3 = ((jnp.swapaxes(a, 1, 2) if view else a) for a in (given[p], given["m_" + p], given["v_" + p]))
            res[p] = _adam_big(w3, m3, v3, owns[(l, g)], recvs[(l, g)], l, tr and not view, res[p], name=f"adamw_{g}_l{l}")
    for g, p, tr in BIG:
        if viewed(p, tr):
            res[p] = [jnp.swapaxes(a, 1, 2) for a in res[p]]

    shapes = [given[n].shape[1:] for n in SMALL_NAMES]
    pw, pm, pv = (_pack([given[pre + n][l] for l in range(DEPTH) for n in SMALL_NAMES]) for pre in ("", "m_", "v_"))
    small_out = [_unpack(o, shapes * DEPTH) for o in _adam_small(pw, pm, pv, gathered_small[0], name="adamw_small")]

    names = ["w_in", "b_gate", "ln_v_g", "ln_v_b", "w_s", "b_s", "sinks", "w_br_a", "w_br_b", "w_o", "ln1_g", "ln1_b", "w_xq",
             "w_xkv", "w_xo", "ln2_g", "ln2_b", "w_up", "w_down", "ln3_g", "ln3_b"]
    out = [loss, dX[None]]
    for kind in range(4):
        for n in names:
            if n in SMALL_NAMES:
                i = SMALL_NAMES.index(n)
                out.append(jnp.stack([small_out[kind][l * len(SMALL_NAMES) + i] for l in range(DEPTH)]))
            else:
                out.append(res[n][kind])
    return tuple(out)
```

```python
import collections
import functools

import jax
import jax.numpy as jnp
from jax import lax
from jax.experimental import pallas as pl
from jax.experimental.pallas import tpu as pltpu

F32 = jnp.float32
BF16 = jnp.bfloat16
MESH = pl.DeviceIdType.MESH

N_DEV = 8
DEPTH = 2
CHUNK = 128
HEAD_DIM = 64
ROPE_HALF = HEAD_DIM // 2
X_HEAD_DIM = 128
GROUP_DIM = 128
LANES = 128
ROPE_THETA = 10000.0
LN_EPS = 1e-5
ALPHA = (2 * DEPTH) ** 0.25
ADAM_LR = 0.001
ADAM_B1 = 0.9
ADAM_B2 = 0.999
ADAM_EPS = 1e-08
ADAM_WD = 0.01
ADAM_STEP = 10
VMEM_BYTES = 64 * 1024 * 1024
VMEM_TEMP_BYTES = 20 * 1024 * 1024

Dims = collections.namedtuple("Dims", "T D IN GW G AW KVW HQ HKV XW XH MEM DFF")


def _offsets(d):
    off_v = d.GW
    off_q = 2 * d.GW
    off_k = off_q + d.AW
    off_va = off_k + d.KVW
    off_ga = off_va + d.KVW
    off_gb = off_ga + d.D
    return off_v, off_q, off_k, off_va, off_ga, off_gb


def _tile(dim, pref, align=LANES):
    if dim <= pref:
        return dim
    t = pref - pref % align
    while t >= align:
        if dim % t == 0:
            return t
        t -= align
    return dim


def _nbytes(shape, dtype):
    n = 1
    for s in shape:
        n *= s
    return n * jnp.dtype(dtype).itemsize


def _params(sem, blocks, scratch=0):
    need = 2 * sum(_nbytes(s, t) for s, t in blocks) + scratch + VMEM_TEMP_BYTES
    limit = min(max(need, 32 * 1024 * 1024), VMEM_BYTES - 4 * 1024 * 1024)
    return pltpu.CompilerParams(dimension_semantics=sem, vmem_limit_bytes=limit)


def _dot(a, b, kind):
    dn = {"nn": (((1,), (0,)), ((), ())), "nt": (((1,), (1,)), ((), ())), "tn": (((0,), (0,)), ((), ()))}[kind]
    return lax.dot_general(a, b, dn, preferred_element_type=F32)


def _gelu(x):
    c = 0.7978845608028654
    t = jnp.tanh(c * (x + 0.044715 * (x * x * x)))
    return 0.5 * x * (1.0 + t)


def _gelu_grad(x):
    c = 0.7978845608028654
    x2 = x * x
    t = jnp.tanh(c * (x + 0.044715 * (x2 * x)))
    return 0.5 * (1.0 + t) + 0.5 * x * (1.0 - t * t) * (c * (1.0 + 3.0 * 0.044715 * x2))


def _ln_fwd(z, g, b):
    mu = jnp.mean(z, axis=-1, keepdims=True)
    zc = z - mu
    var = jnp.mean(zc * zc, axis=-1, keepdims=True)
    rstd = lax.rsqrt(var + LN_EPS)
    xhat = zc * rstd
    return xhat * g + b, xhat, rstd


def _ln_bwd_vals(dy, xhat, rstd, g):
    gd = dy * g
    m1 = jnp.mean(gd, axis=-1, keepdims=True)
    m2 = jnp.mean(gd * xhat, axis=-1, keepdims=True)
    return rstd * (gd - m1 - xhat * m2)


def _acc_store(ref, val, first):
    @pl.when(first)
    def _():
        ref[...] = val

    @pl.when(jnp.logical_not(first))
    def _():
        ref[...] += val


def _matmul(a, b, kind, *, name, tm=512, tn=1024, tk=2048, out_defs=(("mn", F32),), epilogue=None, extras=(), sched=None):
    if kind == "nn":
        (M, K), (K2, N) = a.shape, b.shape
    elif kind == "nt":
        (M, K), (N, K2) = a.shape, b.shape
    else:
        (K, M), (K2, N) = a.shape, b.shape
    assert K == K2, (a.shape, b.shape, kind)
    tm, tn, tk = _tile(M, tm), _tile(N, tn), _tile(K, tk)
    nk = K // tk
    blocks = []

    def spec(shape, imap, dtype):
        blocks.append((shape, dtype))
        return pl.BlockSpec(shape, imap)

    if kind == "tn":
        a_spec = spec((tk, tm), lambda j, i, k: (k, i), a.dtype)
    else:
        a_spec = spec((tm, tk), lambda j, i, k: (i, k), a.dtype)
    if kind == "nt":
        b_spec = spec((tn, tk), lambda j, i, k: (j, k), b.dtype)
    else:
        b_spec = spec((tk, tn), lambda j, i, k: (k, j), b.dtype)
    in_specs = [a_spec, b_spec]
    operands = [a, b]
    for arr, ekind, off in extras:
        if ekind == "mn":
            assert off % tn == 0
            in_specs.append(spec((tm, tn), lambda j, i, k, o=off // tn: (i, j + o), arr.dtype))
        elif ekind == "row":
            assert off % tn == 0
            in_specs.append(spec((1, tn), lambda j, i, k, o=off // tn: (0, j + o), arr.dtype))
        else:
            in_specs.append(spec((tm, 1), lambda j, i, k: (i, 0), arr.dtype))
        operands.append(arr)
    out_shape, out_specs = [], []
    for od in out_defs:
        okind, dt = od[0], od[1]
        if okind == "mn":
            out_shape.append(jax.ShapeDtypeStruct((M, N), dt))
            out_specs.append(spec((tm, tn), lambda j, i, k: (i, j), dt))
        elif okind == "cols":
            assert od[3] % tn == 0
            out_shape.append(jax.ShapeDtypeStruct((M, od[2]), dt))
            out_specs.append(spec((tm, tn), lambda j, i, k, o=od[3] // tn: (i, j + o), dt))
        elif okind == "m1":
            assert N == tn
            out_shape.append(jax.ShapeDtypeStruct((M, 1), dt))
            out_specs.append(spec((tm, 1), lambda j, i, k: (i, 0), dt))
        else:
            out_shape.append(jax.ShapeDtypeStruct((1, N), F32))
            out_specs.append(spec((1, tn), lambda j, i, k: (0, j), F32))
    n_ex, n_out = len(extras), len(out_defs)
    ep = epilogue if epilogue is not None else (lambda acc: (acc,))
    in_place = nk > 1 and epilogue is None and tuple(out_defs) == (("mn", F32),)

    def body(*refs):
        a_ref, b_ref = refs[0], refs[1]
        ex_refs = refs[2 : 2 + n_ex]
        out_refs = refs[2 + n_ex : 2 + n_ex + n_out]
        i = pl.program_id(1)
        k = pl.program_id(2)

        def product():
            return _dot(a_ref[...], b_ref[...], kind)

        def finish(acc):
            vals = ep(acc, *[r[...] for r in ex_refs])
            for od, r, v in zip(out_defs, out_refs, vals):
                if od[0] == "acc":
                    _acc_store(r, v, i == 0)
                else:
                    r[...] = v.astype(od[1])

        if nk == 1:
            finish(product())
            return
        acc_ref = out_refs[0] if in_place else refs[-1]

        @pl.when(k == 0)
        def _():
            acc_ref[...] = product()

        if in_place:
            @pl.when(k > 0)
            def _():
                acc_ref[...] += product()
        else:
            if nk > 2:
                @pl.when(jnp.logical_and(k > 0, k < nk - 1))
                def _():
                    acc_ref[...] += product()

            @pl.when(k == nk - 1)
            def _():
                finish(acc_ref[...] + product())

    scratch = [pltpu.VMEM((tm, tn), F32)] if nk > 1 and not in_place else []
    outs = _hosted(
        body,
        name=name,
        sched=sched,
        grid=(N // tn, M // tm, nk),
        in_specs=in_specs,
        out_specs=out_specs,
        out_shape=out_shape,
        scratch_shapes=scratch,
        compiler_params=_params(("arbitrary", "arbitrary", "arbitrary"), blocks, _nbytes((tm, tn), F32) if scratch else 0),
    )(*operands)
    return outs if len(outs) > 1 else outs[0]


def _ep_resid_ln(acc, resid, g, b):
    y, xhat, rstd = _ln_fwd(ALPHA * resid + acc, g, b)
    return y, y, xhat, rstd


def _ep_resid(acc, resid):
    return (ALPHA * resid + acc,)


def _ep_resid_ln_bwd(acc, resid, xhat, rstd, g):
    dy = ALPHA * resid + acc
    dz = _ln_bwd_vals(dy, xhat, rstd, g)
    return dz, dz, jnp.sum(dy * xhat, axis=0, keepdims=True), jnp.sum(dy, axis=0, keepdims=True)


def _ep_relu2(acc):
    r = jnp.maximum(acc, 0.0)
    return acc, r * r


def _ep_relu2_bwd(acc, h):
    return (acc * (2.0 * jnp.maximum(h, 0.0)),)


def _ep_gate_bwd(acc, ga, gb, ya, yb, bga, bgb):
    sa = jax.nn.sigmoid(ga + bga)
    sb = jax.nn.sigmoid(gb + bgb)
    dga = acc * ya * (sa * (1.0 - sa))
    dgb = acc * yb * (sb * (1.0 - sb))
    return (acc * sa, acc * sb, dga, dgb, jnp.sum(dga, axis=0, keepdims=True), jnp.sum(dgb, axis=0, keepdims=True))


def _ln_bwd(dy, xhat, rstd, g, *, name, sched=None, target=None):
    T, D = dy.shape
    tm = _tile(T, 256)
    head = target is not None

    def body(*refs):
        dy_ref, xh_ref, rs_ref, g_ref = refs[:4]
        dz_ref, dzb_ref, dg_ref, db_ref = refs[4 + head : 8 + head]
        first = pl.program_id(0) == 0
        dyv, xh = dy_ref[...], xh_ref[...]
        if head:
            err = dyv - refs[4][...]
            dyv = err * (1.0 / D)
            part = 0.5 * jnp.sum(jnp.sum(err * err, axis=1, keepdims=True) * (1.0 / D), axis=0, keepdims=True)
            _acc_store(refs[-1], part, first)
        dz = _ln_bwd_vals(dyv, xh, rs_ref[...], g_ref[...])
        dz_ref[...] = dz
        dzb_ref[...] = dz.astype(BF16)
        _acc_store(dg_ref, jnp.sum(dyv * xh, axis=0, keepdims=True), first)
        _acc_store(db_ref, jnp.sum(dyv, axis=0, keepdims=True), first)

    row = pl.BlockSpec((tm, D), lambda i: (i, 0))
    vec = pl.BlockSpec((1, D), lambda i: (0, 0))
    one = pl.BlockSpec((1, 1), lambda i: (0, 0))
    return _hosted(
        body,
        name=name,
        sched=sched,
        grid=(T // tm,),
        in_specs=[row, row, pl.BlockSpec((tm, 1), lambda i: (i, 0)), vec] + [row] * head,
        out_specs=[row, row, vec, vec] + [one] * head,
        out_shape=[
            jax.ShapeDtypeStruct((T, D), F32),
            jax.ShapeDtypeStruct((T, D), BF16),
            jax.ShapeDtypeStruct((1, D), F32),
            jax.ShapeDtypeStruct((1, D), F32),
        ] + [jax.ShapeDtypeStruct((1, 1), F32)] * head,
        compiler_params=_params(("arbitrary",), [((tm, D), F32)] * (4 + head)),
    )(*([dy, xhat, rstd, g] + [target] * head))


def _masked_mix_weights(wm_ref, G):
    r = lax.broadcasted_iota(jnp.int32, (CHUNK, CHUNK), 0)
    c = lax.broadcasted_iota(jnp.int32, (CHUNK, CHUNK), 1)
    return [jnp.where(c <= r, wm_ref[g], 0.0).astype(BF16) for g in range(G)]


def _gmlp_fwd(d, P, lnv_g, lnv_b, wm, bsT, *, name, sched=None):
    T, GW, G = d.T, d.GW, d.G
    tm = _tile(T, 256)
    nc = tm // CHUNK

    def body(u_ref, v_ref, g_ref, b_ref, wm_ref, bs_ref, o_ref):
        gu = _gelu(u_ref[...])
        vn, _, _ = _ln_fwd(_gelu(v_ref[...]), g_ref[...], b_ref[...])
        vn = vn.astype(BF16)
        wl = _masked_mix_weights(wm_ref, G)
        for c in range(nc):
            rows = slice(c * CHUNK, (c + 1) * CHUNK)
            for g in range(G):
                cols = slice(g * GROUP_DIM, (g + 1) * GROUP_DIM)
                mixed = _dot(wl[g], vn[rows, cols], "nn") + bs_ref[:, g : g + 1]
                o_ref[rows, cols] = (gu[rows, cols] * mixed).astype(BF16)

    return _hosted(
        body,
        name=name,
        sched=sched,
        grid=(T // tm,),
        in_specs=[
            pl.BlockSpec((tm, GW), lambda i: (i, 0)),
            pl.BlockSpec((tm, GW), lambda i: (i, 1)),
            pl.BlockSpec((1, GW), lambda i: (0, 0)),
            pl.BlockSpec((1, GW), lambda i: (0, 0)),
            pl.BlockSpec((G, CHUNK, CHUNK), lambda i: (0, 0, 0)),
            pl.BlockSpec((CHUNK, G), lambda i: (0, 0)),
        ],
        out_specs=pl.BlockSpec((tm, GW), lambda i: (i, 0)),
        out_shape=jax.ShapeDtypeStruct((T, GW), BF16),
        compiler_params=_params(("arbitrary",), [((tm, GW), F32)] * 3),
    )(P, P, lnv_g, lnv_b, wm, bsT)


def _gmlp_bwd(d, P, dya_b, Ga, lnv_g, lnv_b, wm, bsT, dP, *, name, sched=None):
    T, D, GW, G = d.T, d.D, d.GW, d.G
    tm = _tile(T, 256)
    nc = tm // CHUNK

    def body(u_ref, v_ref, dya_ref, ga_ref, g_ref, b_ref, wm_ref, bs_ref, dp_in, duv_ref, dwm_ref, dbs_ref, dlg_ref, dlb_ref, dgu_s, dvn_s):
        first = pl.program_id(0) == 0
        dsgu = _dot(dya_ref[...], ga_ref[...], "nn")
        u, v = u_ref[...], v_ref[...]
        gu = _gelu(u)
        lng = g_ref[...]
        vn, xh, rstd = _ln_fwd(_gelu(v), lng, b_ref[...])
        vn = vn.astype(BF16)
        wl = _masked_mix_weights(wm_ref, G)
        r = lax.broadcasted_iota(jnp.int32, (CHUNK, CHUNK), 0)
        cc = lax.broadcasted_iota(jnp.int32, (CHUNK, CHUNK), 1)
        lane_g = lax.broadcasted_iota(jnp.int32, (CHUNK, G), 1)
        dbs = jnp.zeros((CHUNK, G), F32)
        for g in range(G):
            cols = slice(g * GROUP_DIM, (g + 1) * GROUP_DIM)
            dw = jnp.zeros((CHUNK, CHUNK), F32)
            for c in range(nc):
                rows = slice(c * CHUNK, (c + 1) * CHUNK)
                mixed = _dot(wl[g], vn[rows, cols], "nn") + bs_ref[:, g : g + 1]
                ds = dsgu[rows, cols]
                dgu_s[rows, cols] = ds * mixed
                dmx = ds * gu[rows, cols]
                dbs = dbs + jnp.where(lane_g == g, jnp.sum(dmx, axis=1, keepdims=True), 0.0)
                dmx = dmx.astype(BF16)
                dw = dw + _dot(dmx, vn[rows, cols], "nt")
                dvn_s[rows, cols] = _dot(wl[g], dmx, "tn")
            _acc_store(dwm_ref.at[g], jnp.where(cc <= r, dw, 0.0), first)
        _acc_store(dbs_ref, dbs, first)
        dvn = dvn_s[...]
        _acc_store(dlg_ref, jnp.sum(dvn * xh, axis=0, keepdims=True), first)
        _acc_store(dlb_ref, jnp.sum(dvn, axis=0, keepdims=True), first)
        dgv = _ln_bwd_vals(dvn, xh, rstd, lng)
        duv_ref[:, :GW] = (dgu_s[...] * _gelu_grad(u)).astype(BF16)
        duv_ref[:, GW:] = (dgv * _gelu_grad(v)).astype(BF16)

    return _hosted(
        body,
        name=name,
        sched=sched,
        grid=(T // tm,),
        in_specs=[
            pl.BlockSpec((tm, GW), lambda i: (i, 0)),
            pl.BlockSpec((tm, GW), lambda i: (i, 1)),
            pl.BlockSpec((tm, D), lambda i: (i, 0)),
            pl.BlockSpec((D, GW), lambda i: (0, 0)),
            pl.BlockSpec((1, GW), lambda i: (0, 0)),
            pl.BlockSpec((1, GW), lambda i: (0, 0)),
            pl.BlockSpec((G, CHUNK, CHUNK), lambda i: (0, 0, 0)),
            pl.BlockSpec((CHUNK, G), lambda i: (0, 0)),
            pl.BlockSpec(memory_space=pl.ANY),
        ],
        out_specs=[
            pl.BlockSpec((tm, 2 * GW), lambda i: (i, 0)),
            pl.BlockSpec((G, CHUNK, CHUNK), lambda i: (0, 0, 0)),
            pl.BlockSpec((CHUNK, G), lambda i: (0, 0)),
            pl.BlockSpec((1, GW), lambda i: (0, 0)),
            pl.BlockSpec((1, GW), lambda i: (0, 0)),
        ],
        out_shape=[
            jax.ShapeDtypeStruct(dP.shape, BF16),
            jax.ShapeDtypeStruct((G, CHUNK, CHUNK), F32),
            jax.ShapeDtypeStruct((CHUNK, G), F32),
            jax.ShapeDtypeStruct((1, GW), F32),
            jax.ShapeDtypeStruct((1, GW), F32),
        ],
        scratch_shapes=[pltpu.VMEM((tm, GW), F32), pltpu.VMEM((tm, GW), F32)],
        input_output_aliases={8: 0},
        compiler_params=_params(
            ("arbitrary",), [((tm, GW), F32)] * 3 + [((tm, D), BF16), ((D, GW), BF16)], 2 * _nbytes((tm, GW), F32)
        ),
    )(P, P, dya_b, Ga, lnv_g, lnv_b, wm, bsT, dP)


def _swap_halves(x):
    w = x.shape[1]
    lane = lax.broadcasted_iota(jnp.int32, x.shape, 1)
    return jnp.where((lane % HEAD_DIM) < ROPE_HALF, pltpu.roll(x, w - ROPE_HALF, 1), pltpu.roll(x, ROPE_HALF, 1))


def _lane_tile(t, width):
    reps = width // LANES
    return t if reps == 1 else jnp.tile(t, (1, reps))


def _rope(x, cos2, sin2):
    w = x.shape[1]
    return x * _lane_tile(cos2, w) + _swap_halves(x) * _lane_tile(sin2, w)


def _rope_bwd(g, cos2, sin2):
    w = g.shape[1]
    return g * _lane_tile(cos2, w) - _swap_halves(g) * _lane_tile(sin2, w)


PAIR = LANES // HEAD_DIM


def _swa_valid(i):
    s = lax.broadcasted_iota(jnp.int32, (2 * CHUNK, CHUNK), 0)
    t = lax.broadcasted_iota(jnp.int32, (2 * CHUNK, CHUNK), 1)
    cur = jnp.logical_and(s >= CHUNK, s - CHUNK <= t)
    prev = jnp.logical_and(jnp.logical_and(s < CHUNK, s > t), i > 0)
    return jnp.logical_or(cur, prev)


def _half_variants(x, odd):
    lane = lax.broadcasted_iota(jnp.int32, x.shape, 1)
    if odd:
        hi = jnp.where(lane >= HEAD_DIM, x, jnp.zeros_like(x))
        return pltpu.roll(hi, HEAD_DIM, 1), hi
    lo = jnp.where(lane < HEAD_DIM, x, jnp.zeros_like(x))
    return lo, pltpu.roll(lo, HEAD_DIM, 1)


def _swa_probs_t(kx, qp, sink, valid):
    s = jnp.where(valid, _dot(kx, qp, "nt") * (HEAD_DIM**-0.5), -jnp.inf)
    m = jnp.maximum(jnp.max(s, axis=0, keepdims=True), sink)
    e = jnp.exp(s - m)
    e_s = jnp.exp(sink - m)
    den = jnp.sum(e, axis=0, keepdims=True) + e_s
    return e / den, e_s / den


def _swa_load(q_ref, kc_ref, kp_ref, vc_ref, vp_ref, cc, sc, cp, sp):
    qr = _rope(q_ref[...], cc, sc).astype(BF16)
    kcat = jnp.concatenate([_rope(kp_ref[...], cp, sp), _rope(kc_ref[...], cc, sc)], axis=0)
    vcat = jnp.concatenate([vp_ref[...], vc_ref[...]], axis=0)
    return qr, kcat, vcat


def _swa_specs(d):
    _, off_q, off_k, off_va, _, _ = _offsets(d)
    assert off_q % d.AW == 0 and off_k % d.KVW == 0 and off_va % d.KVW == 0
    prev = lambda i: jnp.maximum(i - 1, 0)
    return [
        pl.BlockSpec(memory_space=pltpu.SMEM),
        pl.BlockSpec((CHUNK, d.AW), lambda i, o=off_q // d.AW: (i, o)),
        pl.BlockSpec((CHUNK, d.KVW), lambda i, o=off_k // d.KVW: (i, o)),
        pl.BlockSpec((CHUNK, d.KVW), lambda i, o=off_k // d.KVW: (prev(i), o)),
        pl.BlockSpec((CHUNK, d.KVW), lambda i, o=off_va // d.KVW: (i, o)),
        pl.BlockSpec((CHUNK, d.KVW), lambda i, o=off_va // d.KVW: (prev(i), o)),
        pl.BlockSpec((CHUNK, LANES), lambda i: (i, 0)),
        pl.BlockSpec((CHUNK, LANES), lambda i: (i, 0)),
        pl.BlockSpec((CHUNK, LANES), lambda i: (prev(i), 0)),
        pl.BlockSpec((CHUNK, LANES), lambda i: (prev(i), 0)),
    ]


def _swa_fwd(d, P, cos2, sin2, sinks, *, name, sched=None):
    T, AW, HQ, HKV = d.T, d.AW, d.HQ, d.HKV
    grp = HQ // HKV
    assert grp % PAIR == 0 and HKV % PAIR == 0

    def body(sink_ref, q_ref, kc_ref, kp_ref, vc_ref, vp_ref, cc_ref, sc_ref, cp_ref, sp_ref, o_ref):
        i = pl.program_id(0)
        qr, kcat, vcat = _swa_load(q_ref, kc_ref, kp_ref, vc_ref, vp_ref, cc_ref[...], sc_ref[...], cp_ref[...], sp_ref[...])
        valid = _swa_valid(i)
        for kv in range(HKV):
            col = slice((kv // PAIR) * LANES, (kv // PAIR + 1) * LANES)
            kx = [t.astype(BF16) for t in _half_variants(kcat[:, col], kv % PAIR)]
            vx = [t.astype(BF16) for t in _half_variants(vcat[:, col], kv % PAIR)]
            for pp in range(grp // PAIR):
                p = kv * (grp // PAIR) + pp
                qs = slice(p * LANES, (p + 1) * LANES)
                out = jnp.zeros((CHUNK, LANES), F32)
                for half in range(PAIR):
                    pt, _ = _swa_probs_t(kx[half], qr[:, qs], sink_ref[PAIR * p + half], valid)
                    out = out + _dot(pt.astype(BF16), vx[half], "tn")
                o_ref[:, qs] = out.astype(BF16)

    return _hosted(
        body,
        name=name,
        sched=sched,
        grid=(T // CHUNK,),
        in_specs=_swa_specs(d),
        out_specs=pl.BlockSpec((CHUNK, AW), lambda i: (i, 0)),
        out_shape=jax.ShapeDtypeStruct((T, AW), BF16),
        compiler_params=_params(("arbitrary",), [((CHUNK, AW), F32)] * 3),
    )(sinks, P, P, P, P, P, cos2, sin2, cos2, sin2)


def _swa_bwd(d, P, dyb_b, Gb, cos2, sin2, sinks, dP, *, name, sched=None):
    T, D, AW, KVW, HQ, HKV = d.T, d.D, d.AW, d.KVW, d.HQ, d.HKV
    grp = HQ // HKV
    nb = T // CHUNK
    scale = HEAD_DIM**-0.5
    off_q = _offsets(d)[1]
    assert grp % PAIR == 0 and HKV % PAIR == 0 and off_q % AW == 0

    def body(sink_ref, q_ref, kc_ref, kp_ref, vc_ref, vp_ref, cc_ref, sc_ref, cp_ref, sp_ref, dyb_ref, gb_ref, dp_in,
             dq_ref, dkv_ref, dsk_ref, acc_s, dq_s, dk_s, dv_s):
        i = pl.program_id(0)
        cc, sc, cp, sp = cc_ref[...], sc_ref[...], cp_ref[...], sp_ref[...]
        datt = _dot(dyb_ref[...], gb_ref[...], "nn").astype(BF16)
        qr, kcat, vcat = _swa_load(q_ref, kc_ref, kp_ref, vc_ref, vp_ref, cc, sc, cp, sp)
        valid = _swa_valid(i)
        lane1 = lax.broadcasted_iota(jnp.int32, (1, LANES), 1)
        lane2 = lax.broadcasted_iota(jnp.int32, (2 * CHUNK, LANES), 1)
        dsk = jnp.zeros((1, LANES), F32)
        for kvp in range(HKV // PAIR):
            col = slice(kvp * LANES, (kvp + 1) * LANES)
            dk_col = jnp.zeros((2 * CHUNK, LANES), F32)
            dv_col = jnp.zeros((2 * CHUNK, LANES), F32)
            for odd in range(PAIR):
                kv = kvp * PAIR + odd
                kx = [t.astype(BF16) for t in _half_variants(kcat[:, col], odd)]
                vx = [t.astype(BF16) for t in _half_variants(vcat[:, col], odd)]
                dk_half = [jnp.zeros((2 * CHUNK, LANES), F32) for _ in range(PAIR)]
                dv_half = [jnp.zeros((2 * CHUNK, LANES), F32) for _ in range(PAIR)]
                for pp in range(grp // PAIR):
                    p = kv * (grp // PAIR) + pp
                    qs = slice(p * LANES, (p + 1) * LANES)
                    qp, dop = qr[:, qs], datt[:, qs]
                    dq = jnp.zeros((CHUNK, LANES), F32)
                    for half in range(PAIR):
                        h = PAIR * p + half
                        pt, p_s = _swa_probs_t(kx[half], qp, sink_ref[h], valid)
                        dpt = _dot(vx[half], dop, "nt")
                        rs = jnp.sum(pt * dpt, axis=0, keepdims=True)
                        dst = (pt * (dpt - rs) * scale).astype(BF16)
                        dsk = dsk + jnp.where(lane1 == h, -jnp.sum(p_s * rs, axis=1, keepdims=True), 0.0)
                        dq = dq + _dot(dst, kx[half], "tn")
                        dk_half[half] = dk_half[half] + _dot(dst, qp, "nn")
                        dv_half[half] = dv_half[half] + _dot(pt.astype(BF16), dop, "nn")
                    dq_s[:, qs] = dq
                for acc_half, is_k in ((dk_half, True), (dv_half, False)):
                    lo = jnp.where(lane2 < HEAD_DIM, acc_half[0], 0.0)
                    hi = jnp.where(lane2 >= HEAD_DIM, acc_half[1], 0.0)
                    both = (pltpu.roll(lo, HEAD_DIM, 1) + hi) if odd else (lo + pltpu.roll(hi, HEAD_DIM, 1))
                    if is_k:
                        dk_col = dk_col + both
                    else:
                        dv_col = dv_col + both
            dk_s[:, col] = dk_col
            dv_s[:, col] = dv_col
        dq_ref[...] = _rope_bwd(dq_s[...], cc, sc).astype(BF16)
        cur = pl.ds(pl.multiple_of(i * CHUNK, CHUNK), CHUNK)
        acc_s[cur, :KVW] = _rope_bwd(dk_s[CHUNK:, :], cc, sc)
        acc_s[cur, KVW:] = dv_s[CHUNK:, :]

        @pl.when(i > 0)
        def _():
            prv = pl.ds(pl.multiple_of((i - 1) * CHUNK, CHUNK), CHUNK)
            acc_s[prv, :KVW] += _rope_bwd(dk_s[:CHUNK, :], cp, sp)
            acc_s[prv, KVW:] += dv_s[:CHUNK, :]

        _acc_store(dsk_ref, dsk, i == 0)

        @pl.when(i == nb - 1)
        def _():
            dkv_ref[...] = acc_s[...].astype(BF16)

    return _hosted(
        body,
        name=name,
        sched=sched,
        grid=(nb,),
        in_specs=_swa_specs(d) + [pl.BlockSpec((CHUNK, D), lambda i: (i, 0)), pl.BlockSpec((D, AW), lambda i: (0, 0)),
                                   pl.BlockSpec(memory_space=pl.ANY)],
        out_specs=[
            pl.BlockSpec((CHUNK, AW), lambda i, o=off_q // AW: (i, o)),
            pl.BlockSpec((T, 2 * KVW), lambda i: (0, 0)),
            pl.BlockSpec((1, LANES), lambda i: (0, 0)),
        ],
        out_shape=[
            jax.ShapeDtypeStruct(dP.shape, BF16),
            jax.ShapeDtypeStruct((T, 2 * KVW), BF16),
            jax.ShapeDtypeStruct((1, LANES), F32),
        ],
        input_output_aliases={12: 0},
        scratch_shapes=[
            pltpu.VMEM((T, 2 * KVW), F32),
            pltpu.VMEM((CHUNK, AW), F32),
            pltpu.VMEM((2 * CHUNK, KVW), F32),
            pltpu.VMEM((2 * CHUNK, KVW), F32),
        ],
        compiler_params=_params(
            ("arbitrary",), [((CHUNK, AW), F32)] * 3 + [((D, AW), BF16), ((T, 2 * KVW), BF16)], _nbytes((T, 2 * KVW), F32)
        ),
    )(sinks, P, P, P, P, P, cos2, sin2, cos2, sin2, dyb_b, Gb, dP)


def _place_cols(dst, src, off, *, name, sched=None):
    T, w = src.shape
    tm, tw = _tile(T, 512), _tile(w, 512)
    assert off % tw == 0

    def body(src_ref, dst_in, out_ref):
        out_ref[...] = src_ref[...]

    return _hosted(
        body,
        name=name,
        sched=sched,
        grid=(T // tm, w // tw),
        in_specs=[pl.BlockSpec((tm, tw), lambda i, j: (i, j)), pl.BlockSpec(memory_space=pl.ANY)],
        out_specs=pl.BlockSpec((tm, tw), lambda i, j, o=off // tw: (i, j + o)),
        out_shape=jax.ShapeDtypeStruct(dst.shape, dst.dtype),
        input_output_aliases={1: 0},
        compiler_params=_params(("arbitrary", "arbitrary"), [((tm, tw), dst.dtype)] * 2),
    )(src, dst)


def _branch_merge(d, sgu, att, Ga, Gb, P, b_gate, *, name, sched=None):
    T, D, GW, AW = d.T, d.D, d.GW, d.AW
    _, _, _, _, off_ga, off_gb = _offsets(d)
    tm, tn = _tile(T, 1024), _tile(D, 512)
    assert off_ga % tn == 0 and off_gb % tn == 0

    def body(s_ref, a_ref, ga_w, gb_w, ga_ref, gb_ref, bga_ref, bgb_ref, ya_ref, yb_ref, mg_ref):
        ya = _dot(s_ref[...], ga_w[...], "nt")
        yb = _dot(a_ref[...], gb_w[...], "nt")
        ya_ref[...] = ya
        yb_ref[...] = yb
        sa = jax.nn.sigmoid(ga_ref[...] + bga_ref[...])
        sb = jax.nn.sigmoid(gb_ref[...] + bgb_ref[...])
        mg_ref[...] = (sa * ya + sb * yb).astype(BF16)

    tile = pl.BlockSpec((tm, tn), lambda j, i: (i, j))
    return _hosted(
        body,
        name=name,
        sched=sched,
        grid=(D // tn, T // tm),
        in_specs=[
            pl.BlockSpec((tm, GW), lambda j, i: (i, 0)),
            pl.BlockSpec((tm, AW), lambda j, i: (i, 0)),
            pl.BlockSpec((tn, GW), lambda j, i: (j, 0)),
            pl.BlockSpec((tn, AW), lambda j, i: (j, 0)),
            pl.BlockSpec((tm, tn), lambda j, i, o=off_ga // tn: (i, j + o)),
            pl.BlockSpec((tm, tn), lambda j, i, o=off_gb // tn: (i, j + o)),
            pl.BlockSpec((1, tn), lambda j, i: (0, j)),
            pl.BlockSpec((1, tn), lambda j, i, o=D // tn: (0, j + o)),
        ],
        out_specs=[tile, tile, tile],
        out_shape=[jax.ShapeDtypeStruct((T, D), F32), jax.ShapeDtypeStruct((T, D), F32), jax.ShapeDtypeStruct((T, D), BF16)],
        compiler_params=_params(
            ("arbitrary", "arbitrary"),
            [((tm, GW), BF16), ((tm, AW), BF16), ((tn, GW), BF16), ((tn, AW), BF16)] + [((tm, tn), F32)] * 5,
        ),
    )(sgu, att, Ga, Gb, P, P, b_gate, b_gate)


def _xattn_probs(qh, kh):
    s = _dot(qh, kh, "nt") * (X_HEAD_DIM**-0.5)
    e = jnp.exp(s - jnp.max(s, axis=1, keepdims=True))
    return e / jnp.sum(e, axis=1, keepdims=True)


def _xattn_fwd(d, X, Xb, kv, Gxq, Gxo, ln_g, ln_b, *, name, sched=None):
    T, D, XW, XH, MEM = d.T, d.D, d.XW, d.XH, d.MEM
    tm = _tile(T, 256)

    def body(x_ref, xb_ref, kv_ref, wq_ref, wo_ref, g_ref, b_ref, y_ref, yb_ref, xh_ref, rs_ref, q_ref, o_ref, o_s):
        q = _dot(xb_ref[...], wq_ref[...], "nn").astype(BF16)
        q_ref[...] = q
        for h in range(XH):
            hs = slice(h * X_HEAD_DIM, (h + 1) * X_HEAD_DIM)
            p = _xattn_probs(q[:, hs], kv_ref[:, hs]).astype(BF16)
            o_s[:, hs] = _dot(p, kv_ref[:, XW + h * X_HEAD_DIM : XW + (h + 1) * X_HEAD_DIM], "nn").astype(BF16)
        o = o_s[...]
        o_ref[...] = o
        y, xhat, rstd = _ln_fwd(ALPHA * x_ref[...] + _dot(o, wo_ref[...], "nt"), g_ref[...], b_ref[...])
        y_ref[...] = y
        yb_ref[...] = y.astype(BF16)
        xh_ref[...] = xhat
        rs_ref[...] = rstd

    row = pl.BlockSpec((tm, D), lambda i: (i, 0))
    nar = pl.BlockSpec((tm, XW), lambda i: (i, 0))
    vec = pl.BlockSpec((1, D), lambda i: (0, 0))
    return _hosted(
        body,
        name=name,
        sched=sched,
        grid=(T // tm,),
        in_specs=[
            row,
            row,
            pl.BlockSpec((MEM, 2 * XW), lambda i: (0, 0)),
            pl.BlockSpec((D, XW), lambda i: (0, 0)),
            pl.BlockSpec((D, XW), lambda i: (0, 0)),
            vec,
            vec,
        ],
        out_specs=[row, row, row, pl.BlockSpec((tm, 1), lambda i: (i, 0)), nar, nar],
        out_shape=[
            jax.ShapeDtypeStruct((T, D), F32),
            jax.ShapeDtypeStruct((T, D), BF16),
            jax.ShapeDtypeStruct((T, D), F32),
            jax.ShapeDtypeStruct((T, 1), F32),
            jax.ShapeDtypeStruct((T, XW), BF16),
            jax.ShapeDtypeStruct((T, XW), BF16),
        ],
        scratch_shapes=[pltpu.VMEM((tm, XW), BF16)],
        compiler_params=_params(("arbitrary",), [((tm, D), F32)] * 4 + [((D, XW), BF16)] * 2),
    )(X, Xb, kv, Gxq, Gxo, ln_g, ln_b)


def _xattn_bwd(d, dz, dzb, q_b, kv, Gxq, Gxo, xhat, rstd, ln_g, *, name, sched=None):
    T, D, XW, XH, MEM = d.T, d.D, d.XW, d.XH, d.MEM
    tm = _tile(T, 256)
    scale = X_HEAD_DIM**-0.5

    def body(dz_ref, dzb_ref, q_ref, kv_ref, wq_ref, wo_ref, xh_ref, rs_ref, g_ref, dx_ref, dxb_ref, dg_ref, db_ref, dq_ref, dkv_ref, dq_s):
        first = pl.program_id(0) == 0
        do = _dot(dzb_ref[...], wo_ref[...], "nn").astype(BF16)
        for h in range(XH):
            hs = slice(h * X_HEAD_DIM, (h + 1) * X_HEAD_DIM)
            vs = slice(XW + h * X_HEAD_DIM, XW + (h + 1) * X_HEAD_DIM)
            kh, vh, qh, doh = kv_ref[:, hs], kv_ref[:, vs], q_ref[:, hs], do[:, hs]
            p = _xattn_probs(qh, kh)
            dp = _dot(doh, vh, "nt")
            ds = (p * (dp - jnp.sum(p * dp, axis=1, keepdims=True)) * scale).astype(BF16)
            dq_s[:, hs] = _dot(ds, kh, "nn").astype(BF16)
            _acc_store(dkv_ref.at[h], _dot(ds, qh, "tn"), first)
            _acc_store(dkv_ref.at[XH + h], _dot(p.astype(BF16), doh, "tn"), first)
        dq = dq_s[...]
        dq_ref[...] = dq
        dy = ALPHA * dz_ref[...] + _dot(dq, wq_ref[...], "nt")
        xh = xh_ref[...]
        dz_in = _ln_bwd_vals(dy, xh, rs_ref[...], g_ref[...])
        dx_ref[...] = dz_in
        dxb_ref[...] = dz_in.astype(BF16)
        _acc_store(dg_ref, jnp.sum(dy * xh, axis=0, keepdims=True), first)
        _acc_store(db_ref, jnp.sum(dy, axis=0, keepdims=True), first)

    row = pl.BlockSpec((tm, D), lambda i: (i, 0))
    nar = pl.BlockSpec((tm, XW), lambda i: (i, 0))
    vec = pl.BlockSpec((1, D), lambda i: (0, 0))
    return _hosted(
        body,
        name=name,
        sched=sched,
        grid=(T // tm,),
        in_specs=[
            row,
            row,
            nar,
            pl.BlockSpec((MEM, 2 * XW), lambda i: (0, 0)),
            pl.BlockSpec((D, XW), lambda i: (0, 0)),
            pl.BlockSpec((D, XW), lambda i: (0, 0)),
            row,
            pl.BlockSpec((tm, 1), lambda i: (i, 0)),
            vec,
        ],
        out_specs=[row, row, vec, vec, nar, pl.BlockSpec((2 * XH, MEM, X_HEAD_DIM), lambda i: (0, 0, 0))],
        out_shape=[
            jax.ShapeDtypeStruct((T, D), F32),
            jax.ShapeDtypeStruct((T, D), BF16),
            jax.ShapeDtypeStruct((1, D), F32),
            jax.ShapeDtypeStruct((1, D), F32),
            jax.ShapeDtypeStruct((T, XW), BF16),
            jax.ShapeDtypeStruct((2 * XH, MEM, X_HEAD_DIM), F32),
        ],
        scratch_shapes=[pltpu.VMEM((tm, XW), BF16)],
        compiler_params=_params(("arbitrary",), [((tm, D), F32)] * 5 + [((D, XW), BF16)] * 2),
    )(dz, dzb, q_b, kv, Gxq, Gxo, xhat, rstd, ln_g)


def _resid_ln_outs():
    return (("mn", F32), ("mn", BF16), ("mn", F32), ("m1", F32))


def _layer_fwd(d, X, Xb, memb, G, S, cos2, sin2, tag, sched=None):
    D = d.D
    mm = functools.partial(_matmul, sched=sched)
    P = mm(Xb, G["in"], "nt", name=f"proj_{tag}", tm=1024, tn=_tile(d.IN, 1280))
    sgu = _gmlp_fwd(d, P, S["ln_v_g"], S["ln_v_b"], S["w_s"], S["b_sT"], name=f"gmlp_fwd_{tag}", sched=sched)
    att = _swa_fwd(d, P, cos2, sin2, S["sinks"], name=f"swa_fwd_{tag}", sched=sched)
    ya, yb, merged = _branch_merge(d, sgu, att, G["a"], G["b"], P, S["b_gate"], name=f"merge_{tag}", sched=sched)
    X1, X1b, xh1, rs1 = mm(
        merged, G["o"], "nn", name=f"oproj_ln_{tag}", tn=D, tk=1024, out_defs=_resid_ln_outs(), epilogue=_ep_resid_ln,
        extras=((X, "mn", 0), (S["ln1_g"], "row", 0), (S["ln1_b"], "row", 0)),
    )
    kv = mm(memb, G["xkv"], "nn", name=f"xkv_{tag}", out_defs=(("mn", BF16),))
    X2, X2b, xh2, rs2, q_b, o_b = _xattn_fwd(d, X1, X1b, kv, G["xq"], G["xo"], S["ln2_g"], S["ln2_b"], name=f"xattn_fwd_{tag}", sched=sched)
    H, A = mm(X2b, G["up"], "nt", name=f"up_{tag}", tm=1024, out_defs=(("mn", F32), ("mn", BF16)), epilogue=_ep_relu2)
    X3, X3b, xh3, rs3 = mm(
        A, G["down"], "nn", name=f"down_ln_{tag}", tn=D, tk=1024, out_defs=_resid_ln_outs(), epilogue=_ep_resid_ln,
        extras=((X2, "mn", 0), (S["ln3_g"], "row", 0), (S["ln3_b"], "row", 0)),
    )
    saved = dict(Xb=Xb, P=P, sgu=sgu, att=att, ya=ya, yb=yb, merged=merged, X1b=X1b, xh1=xh1, rs1=rs1, kv=kv, q_b=q_b,
                 o_b=o_b, X2b=X2b, xh2=xh2, rs2=rs2, H=H, A=A, xh3=xh3, rs3=rs3)
    return X3, X3b, saved


def _layer_bwd(d, dXout, sv, memb, G, S, cos2, sin2, tag, sched=None, on_grad=None, target=None):
    D = d.D
    mm = functools.partial(_matmul, sched=sched)
    emit = on_grad if on_grad is not None else (lambda n, g: None)
    wide = dict(tn=D)
    _, _, off_k, _, off_ga, off_gb = _offsets(d)
    bf = (("mn", BF16),)
    dz3, dz3b, dg3, db3, *loss = _ln_bwd(dXout, sv["xh3"], sv["rs3"], S["ln3_g"], name=f"ln3_bwd_{tag}", sched=sched, target=target)
    dHb = mm(dz3b, G["down"], "nt", name=f"down_dx_{tag}", tm=1024, out_defs=bf, epilogue=_ep_relu2_bwd, extras=((sv["H"], "mn", 0),))
    g_down = mm(sv["A"], dz3b, "tn", name=f"down_dw_{tag}", **wide)
    emit("down", g_down)
    ln_bwd_outs = (("mn", F32), ("mn", BF16), ("acc", F32), ("acc", F32))
    dz2, dz2b, dg2, db2 = mm(
        dHb, G["up"], "nn", name=f"up_dx_{tag}", tn=D, tk=1024, out_defs=ln_bwd_outs, epilogue=_ep_resid_ln_bwd,
        extras=((dz3, "mn", 0), (sv["xh2"], "mn", 0), (sv["rs2"], "m1", 0), (S["ln2_g"], "row", 0)),
    )
    g_up = mm(dHb, sv["X2b"], "tn", name=f"up_dw_{tag}", **wide)
    emit("up", g_up)
    dz1, dz1b, dg1, db1, dq_b, dkv = _xattn_bwd(d, dz2, dz2b, sv["q_b"], sv["kv"], G["xq"], G["xo"], sv["xh1"], sv["rs1"], S["ln1_g"],
                                               name=f"xattn_bwd_{tag}", sched=sched)
    g_xo = mm(dz2b, sv["o_b"], "tn", name=f"xo_dw_{tag}")
    emit("xo", g_xo)
    g_xq = mm(sv["X1b"], dq_b, "tn", name=f"xq_dw_{tag}")
    emit("xq", g_xq)
    dkv_b = dkv.transpose(1, 0, 2).reshape(d.MEM, 2 * d.XW).astype(BF16)
    g_xkv = mm(memb, dkv_b, "tn", name=f"xkv_dw_{tag}")
    emit("xkv", g_xkv)
    dya_b, dyb_b, dP, dgb_b, dbga, dbgb = mm(
        dz1b, G["o"], "nt", name=f"oproj_dx_{tag}", tm=1024, tn=512,
        out_defs=(("mn", BF16), ("mn", BF16), ("cols", BF16, d.IN, off_ga), ("mn", BF16), ("acc", F32), ("acc", F32)),
        epilogue=_ep_gate_bwd,
        extras=((sv["P"], "mn", off_ga), (sv["P"], "mn", off_gb), (sv["ya"], "mn", 0), (sv["yb"], "mn", 0),
                (S["b_gate"], "row", 0), (S["b_gate"], "row", D)),
    )
    g_o = mm(sv["merged"], dz1b, "tn", name=f"oproj_dw_{tag}", **wide)
    emit("o", g_o)
    dP = _place_cols(dP, dgb_b, off_gb, name=f"place_dgb_{tag}")
    dP, dwm, dbsT, dlvg, dlvb = _gmlp_bwd(d, sv["P"], dya_b, G["a"], S["ln_v_g"], S["ln_v_b"], S["w_s"], S["b_sT"], dP, name=f"gmlp_bwd_{tag}", sched=sched)
    g_a = mm(dya_b, sv["sgu"], "tn", name=f"bra_dw_{tag}")
    emit("a", g_a)
    dP, dkvr_b, dsk = _swa_bwd(d, sv["P"], dyb_b, G["b"], cos2, sin2, S["sinks"], dP, name=f"swa_bwd_{tag}", sched=sched)
    g_b = mm(dyb_b, sv["att"], "tn", name=f"brb_dw_{tag}")
    emit("b", g_b)
    dP = _place_cols(dP, dkvr_b, off_k, name=f"place_dkv_{tag}")
    small = dict(b_gate=jnp.concatenate([dbga, dbgb], axis=1), ln_v_g=dlvg, ln_v_b=dlvb, w_s=dwm, b_sT=dbsT, sinks=dsk,
                 ln1_g=dg1, ln1_b=db1, ln2_g=dg2, ln2_b=db2, ln3_g=dg3, ln3_b=db3)
    emit("small", small)
    g_in = mm(dP, sv["Xb"], "tn", name=f"proj_dw_{tag}", **wide)
    emit("in", g_in)
    dXin = mm(dP, G["in"], "nn", name=f"proj_dx_{tag}", tn=D, tk=_tile(d.IN, 1920), epilogue=_ep_resid, extras=((dz1, "mn", 0),))
    big = {"in": g_in, "a": g_a, "b": g_b, "o": g_o, "xq": g_xq, "xkv": g_xkv, "xo": g_xo, "up": g_up, "down": g_down}
    return dXin, big, small, (loss[0] if loss else None)


def _rope_tables(T):
    inv = 1.0 / (ROPE_THETA ** (jnp.arange(0, HEAD_DIM, 2, dtype=F32) / HEAD_DIM))
    ang = jnp.arange(T, dtype=F32)[:, None] * inv[None, :]
    cos, sin = jnp.cos(ang), jnp.sin(ang)
    reps = LANES // HEAD_DIM
    return jnp.concatenate([cos, cos] * reps, axis=1), jnp.concatenate([-sin, sin] * reps, axis=1)


def _local_step(d, x, mem, target, Gs, Ss, sched=None, on_grad=None):
    cos2, sin2 = _rope_tables(d.T)
    memb = mem.astype(BF16)
    X, Xb = x, x.astype(BF16)
    saved = []
    for l in range(DEPTH):
        X, Xb, sv = _layer_fwd(d, X, Xb, memb, Gs[l], Ss[l], cos2, sin2, f"l{l}", sched)
        saved.append(sv)
    bigs, smalls = [None] * DEPTH, [None] * DEPTH
    dX, loss = X, None
    for l in reversed(range(DEPTH)):
        emit = None if on_grad is None else functools.partial(on_grad, l)
        head = target if l == DEPTH - 1 else None
        dX, bigs[l], smalls[l], got = _layer_bwd(d, dX, saved[l], memb, Gs[l], Ss[l], cos2, sin2, f"l{l}", sched, emit, head)
        loss = got if got is not None else loss
    return loss, dX, bigs, smalls


def _place():
    x, y, c = lax.axis_index("x"), lax.axis_index("y"), lax.axis_index("c")
    return x, y, c, [(1 - x, y), (x, 1 - y), (1 - x, 1 - y)]


def _all_gather(shards, *, name, sched=None):
    n = len(shards)
    any_spec = pl.BlockSpec(memory_space=pl.ANY)

    def body(*refs):
        ins, outs = refs[:n], refs[n : 2 * n]
        send_sems, recv_sems, local_sems = refs[2 * n :]
        x, y, c, chips = _place()
        me, sibling = (x, y, c), (x, y, 1 - c)

        def rows(w, p):
            r = ins[w].shape[0]
            return outs[w].at[pl.ds((4 * p[0] + 2 * p[1] + p[2]) * r, r), :]

        def copy(w, k, block, to, src=None):
            return pltpu.make_async_remote_copy(
                src_ref=rows(w, block) if src is None else src, dst_ref=rows(w, block),
                send_sem=send_sems.at[7 * w + k], recv_sem=recv_sems.at[7 * w + k], device_id=to, device_id_type=MESH)

        mine = [pltpu.make_async_copy(ins[w], rows(w, me), local_sems.at[w]) for w in range(n)]
        for cp in mine:
            cp.start()
        first = []
        for w in range(n):
            first.append(copy(w, 0, me, sibling, src=ins[w]))
            first += [copy(w, 1 + j, me, (*chip, c), src=ins[w]) for j, chip in enumerate(chips)]
        for cp in first:
            cp.start()
        passed = []
        for j, chip in enumerate(chips):
            for w in range(n):
                copy(w, 1 + j, (*chip, c), me).wait_recv()
                fwd = copy(w, 4 + j, (*chip, c), sibling)
                fwd.start()
                passed.append(fwd)
        for w in range(n):
            copy(w, 0, sibling, me).wait_recv()
            for j, chip in enumerate(chips):
                copy(w, 4 + j, (*chip, 1 - c), me).wait_recv()
        for cp in first + passed:
            cp.wait_send()
        for cp in mine:
            cp.wait()

    return _hosted(
        body,
        name=name,
        sched=sched,
        in_specs=[any_spec] * n,
        out_specs=[any_spec] * n,
        out_shape=[jax.ShapeDtypeStruct((N_DEV * s.shape[0], s.shape[1]), s.dtype) for s in shards],
        scratch_shapes=[pltpu.SemaphoreType.DMA((7 * n,)), pltpu.SemaphoreType.DMA((7 * n,)), pltpu.SemaphoreType.DMA((n,))],
    )(*shards)


class _Task:
    def __init__(self, ins, out_shapes, n_remote, n_local, plan, done, aliases=None):
        self.ins, self.out_shapes, self.n_remote, self.n_local = list(ins), list(out_shapes), n_remote, n_local
        self.plan, self.done, self.aliases = plan, done, dict(aliases or {})


class _Sched:
    def __init__(self):
        self.pending = {}

    def at(self, host, task):
        self.pending.setdefault(host, []).append(task)

    def take(self, host):
        return self.pending.pop(host, [])


def _in_hbm(out_shape):
    if isinstance(out_shape, (list, tuple)):
        return [_in_hbm(s) for s in out_shape]
    return pltpu.HBM(out_shape.shape, out_shape.dtype)


def _hosted(body, *, name, sched=None, tasks=(), grid=(), in_specs=None, out_specs=None, out_shape=(), scratch_shapes=(),
            compiler_params=None, input_output_aliases=None):
    tasks = list(tasks) + (sched.take(name) if sched is not None else [])
    scratch_shapes = list(scratch_shapes)
    kwargs = dict(name=name)
    core_aliases = dict(input_output_aliases or {})
    if grid:
        kwargs["grid"] = grid
    if compiler_params is not None:
        kwargs["compiler_params"] = compiler_params
    if not tasks:
        if in_specs is not None:
            kwargs.update(in_specs=in_specs, out_specs=out_specs)
        return pl.pallas_call(body, out_shape=_in_hbm(out_shape), scratch_shapes=scratch_shapes, input_output_aliases=core_aliases, **kwargs)
    assert in_specs is not None and out_specs is not None, name
    single = not isinstance(out_shape, (list, tuple))
    core_shape = [out_shape] if single else list(out_shape)
    core_specs = [out_specs] if single else list(out_specs)
    in_specs = list(in_specs)
    n_in, n_out, n_scr = len(in_specs), len(core_shape), len(scratch_shapes)
    t_ins = [a for t in tasks for a in t.ins]
    t_outs = [s for t in tasks for s in t.out_shapes]
    n_rem = sum(t.n_remote for t in tasks)
    n_loc = sum(t.n_local for t in tasks)
    aliases, pos_in, pos_out = dict(core_aliases), n_in, n_out
    for t in tasks:
        for a, b in t.aliases.items():
            aliases[pos_in + a] = pos_out + b
        pos_in += len(t.ins)
        pos_out += len(t.out_shapes)
    any_spec = pl.BlockSpec(memory_space=pl.ANY)

    def wrapped(*refs):
        core_in, t_in = refs[:n_in], refs[n_in : n_in + len(t_ins)]
        o0 = n_in + len(t_ins)
        core_out, t_out = refs[o0 : o0 + n_out], refs[o0 + n_out : o0 + n_out + len(t_outs)]
        s0 = o0 + n_out + len(t_outs)
        core_scr = refs[s0 : s0 + n_scr]
        send_sems, recv_sems, local_sems = refs[s0 + n_scr :]
        remote, local = [], []
        pi = po = pr = pq = 0
        for t in tasks:
            r, q = t.plan(t_in[pi : pi + len(t.ins)], t_out[po : po + len(t.out_shapes)], send_sems, recv_sems, local_sems, pr, pq)
            assert len(r) == t.n_remote and len(q) == t.n_local
            remote += r
            local += q
            pi, po, pr, pq = pi + len(t.ins), po + len(t.out_shapes), pr + t.n_remote, pq + t.n_local

        def start():
            for cp in local + remote:
                cp.start()

        def finish():
            for cp in remote + local:
                cp.wait()

        if grid:
            ids = [pl.program_id(a) for a in range(len(grid))]
            first = functools.reduce(jnp.logical_and, [i == 0 for i in ids])
            last = functools.reduce(jnp.logical_and, [i == g - 1 for i, g in zip(ids, grid)])
            pl.when(first)(start)
            body(*core_in, *core_out, *core_scr)
            pl.when(last)(finish)
        else:
            start()
            body(*core_in, *core_out, *core_scr)
            finish()

    call = pl.pallas_call(
        wrapped,
        in_specs=in_specs + [any_spec] * len(t_ins),
        out_specs=core_specs + [any_spec] * len(t_outs),
        out_shape=_in_hbm(core_shape + t_outs),
        scratch_shapes=scratch_shapes
        + [pltpu.SemaphoreType.DMA((n_rem,)), pltpu.SemaphoreType.DMA((n_rem,)), pltpu.SemaphoreType.DMA((max(n_loc, 1),))],
        input_output_aliases=aliases,
        **kwargs,
    )

    def run(*operands):
        outs = call(*operands, *t_ins)
        po = n_out
        for t in tasks:
            t.done(list(outs[po : po + len(t.out_shapes)]))
            po += len(t.out_shapes)
        return outs[0] if single else list(outs[:n_out])

    return run


def _comm_only(tasks, *, name):
    _hosted(lambda: None, name=name, tasks=tasks, in_specs=[], out_shape=[], out_specs=[])()


def _rows(ref, block, n_blocks):
    r = ref.shape[0] // n_blocks
    return ref.at[pl.ds(block * r, r), :]


def _remote(src, dst, send_sems, recv_sems, k, to):
    return pltpu.make_async_remote_copy(src_ref=src, dst_ref=dst, send_sem=send_sems.at[k], recv_sem=recv_sems.at[k],
                                        device_id=to, device_id_type=MESH)


def _gather_stage1(shards, done, part=(0, 1, 1), into=None):
    n = len(shards)
    k0, k1, n_parts = part

    def plan(ins, outs, send_sems, recv_sems, local_sems, pr, pq):
        x, y, c, chips = _place()
        mine = 4 * x + 2 * y + c
        remote, local = [], []
        for w in range(n):
            r = shards[w].shape[0]
            rp = r // n_parts
            src = ins[w].at[pl.ds(k0 * rp, (k1 - k0) * rp), :]
            dst = outs[w].at[pl.ds(mine * r + k0 * rp, (k1 - k0) * rp), :]
            local.append(pltpu.make_async_copy(src, dst, local_sems.at[pq + w]))
            for j, to in enumerate([(x, y, 1 - c)] + [(*chip, c) for chip in chips]):
                remote.append(_remote(src, dst, send_sems, recv_sems, pr + 4 * w + j, to))
        return remote, local

    shapes = [jax.ShapeDtypeStruct((N_DEV * s.shape[0], s.shape[1]), s.dtype) for s in shards]
    if into is None:
        return _Task(shards, shapes, 4 * n, n, plan, done)
    return _Task(list(shards) + list(into), shapes, 4 * n, n, plan, done, aliases={n + w: w for w in range(n)})


def _gather_stage2(partial, done):
    n = len(partial)

    def plan(ins, outs, send_sems, recv_sems, local_sems, pr, pq):
        x, y, c, chips = _place()
        remote = []
        for w in range(n):
            for j, chip in enumerate(chips):
                rows = _rows(outs[w], 4 * chip[0] + 2 * chip[1] + c, N_DEV)
                remote.append(_remote(rows, rows, send_sems, recv_sems, pr + 3 * w + j, (x, y, 1 - c)))
        return remote, []

    shapes = [jax.ShapeDtypeStruct(p.shape, p.dtype) for p in partial]
    return _Task(partial, shapes, 3 * n, 0, plan, done, aliases={w: w for w in range(n)})


def _sibling_stage(grads, done):
    n = len(grads)

    def plan(ins, outs, send_sems, recv_sems, local_sems, pr, pq):
        x, y, c, _ = _place()
        remote = []
        for w in range(n):
            for k in range(4):
                src = _rows(ins[w], 4 * (k // 2) + 2 * (k % 2) + (1 - c), N_DEV)
                remote.append(_remote(src, _rows(outs[w], k, 4), send_sems, recv_sems, pr + 4 * w + k, (x, y, 1 - c)))
        return remote, []

    shapes = [jax.ShapeDtypeStruct((g.shape[0] // 2, g.shape[1]), g.dtype) for g in grads]
    return _Task(grads, shapes, 4 * n, 0, plan, done)


def _chips_stage(sends, done, part=(0, 1), into=None):
    n = len(sends)
    k, n_parts = part

    def plan(ins, outs, send_sems, recv_sems, local_sems, pr, pq):
        x, y, c, chips = _place()
        remote = []
        for w in range(n):
            r = sends[w].shape[0] // 3
            rp = r // n_parts
            for j, chip in enumerate(chips):
                rows = pl.ds(j * r + k * rp, rp)
                remote.append(_remote(ins[w].at[rows, :], outs[w].at[rows, :], send_sems, recv_sems, pr + 3 * w + j, (*chip, c)))
        return remote, []

    shapes = [jax.ShapeDtypeStruct(s.shape, s.dtype) for s in sends]
    if into is None:
        return _Task(sends, shapes, 3 * n, 0, plan, done)
    return _Task(list(sends) + list(into), shapes, 3 * n, 0, plan, done, aliases={n + w: w for w in range(n)})


def _pair_sum(grad, got, place, *, name):
    R, C = grad.shape
    r = R // N_DEV
    tr = _tile(r, 256, 16)
    nt = r // tr

    def chip_of(s, pr):
        fx = jnp.where((s == 1) | (s == 3), 1, 0)
        fy = jnp.where(s >= 2, 1, 0)
        return pr[0] ^ fx, pr[1] ^ fy

    def grad_map(s, t, pr):
        kx, ky = chip_of(s, pr)
        return ((4 * kx + 2 * ky + pr[2]) * nt + t, 0)

    def got_map(s, t, pr):
        kx, ky = chip_of(s, pr)
        return ((2 * kx + ky) * nt + t, 0)

    def body(pr, g_ref, r_ref, own_ref, send_ref):
        s = pl.program_id(0)
        tot = g_ref[...] + r_ref[...]

        @pl.when(s == 0)
        def _():
            own_ref[...] = tot

        @pl.when(s > 0)
        def _():
            send_ref[...] = tot.astype(BF16)

    return pl.pallas_call(
        body,
        name=name,
        grid_spec=pltpu.PrefetchScalarGridSpec(
            num_scalar_prefetch=1,
            grid=(4, nt),
            in_specs=[pl.BlockSpec((tr, C), grad_map), pl.BlockSpec((tr, C), got_map)],
            out_specs=[
                pl.BlockSpec((tr, C), lambda s, t, pr: (jnp.where(s == 0, t, nt - 1), 0)),
                pl.BlockSpec((tr, C), lambda s, t, pr: (jnp.where(s == 0, 0, (s - 1) * nt + t), 0)),
            ],
        ),
        out_shape=[jax.ShapeDtypeStruct((r, C), F32), jax.ShapeDtypeStruct((3 * r, C), BF16)],
        compiler_params=_params(("arbitrary", "arbitrary"), [((tr, C), F32)] * 4),
    )(place, grad, got)


def _adam_vals(w, g, m, v):
    m = ADAM_B1 * m + (1.0 - ADAM_B1) * g
    v = ADAM_B2 * v + (1.0 - ADAM_B2) * (g * g)
    m_hat = m / (1.0 - ADAM_B1**ADAM_STEP)
    v_hat = v / (1.0 - ADAM_B2**ADAM_STEP)
    delta = -ADAM_LR * (m_hat / (jnp.sqrt(v_hat) + ADAM_EPS) + ADAM_WD * w)
    return delta, m, v


def _adam_big(w3, m3, v3, own, got, layer, transposed, prev, *, name, sched=None):
    _, A, B = w3.shape
    ta = _tile(A, 256, 16)
    nt = A // ta
    b_pad = (-B) % LANES

    def body(*refs):
        w_ref, m_ref, v_ref, own_ref, g1_ref, g2_ref, g3_ref = refs[:7]
        g_ref, d_ref, nm_ref, nv_ref = refs[-4:]
        g = ((own_ref[...] + g1_ref[...].astype(F32)) + g2_ref[...].astype(F32)) + g3_ref[...].astype(F32)
        if transposed:
            if b_pad:
                g = jnp.concatenate([g, jnp.zeros((b_pad, ta), F32)], axis=0)
            g = g.T[:, :B]
        g_ref[...] = g
        d_ref[...], nm_ref[...], nv_ref[...] = _adam_vals(w_ref[...], g, m_ref[...], v_ref[...])

    nat = pl.BlockSpec((None, ta, B), lambda t: (layer, t, 0))
    if transposed:
        parts = [pl.BlockSpec((B, ta), lambda t: (0, t))] + [pl.BlockSpec((B, ta), lambda t, j=j: (j, t)) for j in range(3)]
    else:
        parts = [pl.BlockSpec((ta, B), lambda t: (t, 0))] + [pl.BlockSpec((ta, B), lambda t, j=j: (j * nt + t, 0)) for j in range(3)]
    keep = [] if prev is None else list(prev)
    outs = _hosted(
        body,
        name=name,
        sched=sched,
        grid=(nt,),
        in_specs=[nat, nat, nat] + parts + [pl.BlockSpec(memory_space=pl.ANY)] * len(keep),
        out_specs=[nat] * 4,
        out_shape=[jax.ShapeDtypeStruct(w3.shape, F32)] * 4,
        input_output_aliases={7 + k: k for k in range(len(keep))},
        compiler_params=_params(("arbitrary",), [((ta, B), F32)] * 10),
    )(w3, m3, v3, own, got, got, got, *keep)
    return list(outs)


def _adam_small(w, m, v, parts, *, name, sched=None):
    R = w.shape[0]

    def body(w_ref, m_ref, v_ref, p_ref, g_ref, d_ref, nm_ref, nv_ref):
        g = p_ref[pl.ds(0, R), :]
        for k in range(1, N_DEV):
            g = g + p_ref[pl.ds(k * R, R), :]
        g_ref[...] = g
        d_ref[...], nm_ref[...], nv_ref[...] = _adam_vals(w_ref[...], g, m_ref[...], v_ref[...])

    return _hosted(
        body,
        name=name,
        sched=sched,
        out_shape=[jax.ShapeDtypeStruct((R, LANES), F32)] * 4,
        compiler_params=_params(None, [((R, LANES), F32)] * 16),
    )(w, m, v, parts)


SMALL_NAMES = ("b_gate", "ln_v_g", "ln_v_b", "w_s", "b_s", "sinks", "ln1_g", "ln1_b", "ln2_g", "ln2_b", "ln3_g", "ln3_b")
PACK_ALIGN = 8 * LANES


def _pack(arrays):
    flat = []
    for a in arrays:
        a = a.reshape(-1)
        flat.append(jnp.pad(a, (0, (-a.shape[0]) % PACK_ALIGN)))
    return jnp.concatenate(flat).reshape(-1, LANES)


def _unpack(packed, shapes):
    flat = packed.reshape(-1)
    out, pos = [], 0
    for s in shapes:
        n = 1
        for e in s:
            n *= e
        out.append(flat[pos : pos + n].reshape(s))
        pos += n + (-n) % PACK_ALIGN
    return out


BIG = (("in", "w_in", True), ("a", "w_br_a", True), ("b", "w_br_b", True), ("o", "w_o", False), ("xq", "w_xq", False),
       ("xkv", "w_xkv", False), ("xo", "w_xo", True), ("up", "w_up", True), ("down", "w_down", False))


SMALL_BIG = ("a", "b", "o", "xq", "xkv", "xo")
GATHER_PLAN = (
    (0, ("in",), None, None),
    (0, SMALL_BIG, ("proj_l0",), "gmlp_fwd_l0"),
    (0, ("up",), ("swa_fwd_l0", "merge_l0"), "oproj_ln_l0"),
    (0, ("down",), ("oproj_ln_l0", "xattn_fwd_l0"), "up_l0"),
    (1, ("in",), ("up_l0",), "down_ln_l0"),
    (1, SMALL_BIG, ("down_ln_l0",), "proj_l1"),
    (1, ("up",), ("down_ln_l0", ("proj_l1", 3)), "gmlp_fwd_l1"),
    (1, ("down",), ("swa_fwd_l1", "merge_l1"), "oproj_ln_l1"),
)
EARLY_SMALL_BIG = ("o", "xq", "xkv", "xo")
LATE_SMALL_BIG = ("a", "b")
GRAD_HOSTS = {
    1: {"down": ("xattn_bwd_l1", "swa_bwd_l1"), "up": ("xattn_bwd_l1", ("gmlp_bwd_l1", "proj_dw_l1")),
        **{g: ("gmlp_bwd_l1", "proj_dx_l1") for g in EARLY_SMALL_BIG}, **{g: ("proj_dw_l1", "proj_dx_l1") for g in LATE_SMALL_BIG},
        "in": ("proj_dx_l1", ("down_dx_l0", "down_dw_l0"))},
    0: {"down": ("xattn_bwd_l0", ("oproj_dx_l0", "gmlp_bwd_l0")), "up": ("xattn_bwd_l0", "swa_bwd_l0"),
        **{g: ("gmlp_bwd_l0", "proj_dw_l0") for g in EARLY_SMALL_BIG}, **{g: ("proj_dw_l0", "proj_dx_l0") for g in LATE_SMALL_BIG},
        "in": (None, "proj_dx_l0")},
}
SMALL_GRAD_HOSTS = ("proj_dw_l0", "proj_dx_l0")


def kernel(x, mem, w_in, b_gate, ln_v_g, ln_v_b, w_s, b_s, sinks, w_br_a, w_br_b, w_o, ln1_g, ln1_b, w_xq, w_xkv, w_xo, ln2_g, ln2_b, w_up, w_down, ln3_g, ln3_b, loss_target, m_w_in, m_b_gate, m_ln_v_g, m_ln_v_b, m_w_s, m_b_s, m_sinks, m_w_br_a, m_w_br_b, m_w_o, m_ln1_g, m_ln1_b, m_w_xq, m_w_xkv, m_w_xo, m_ln2_g, m_ln2_b, m_w_up, m_w_down, m_ln3_g, m_ln3_b, v_w_in, v_b_gate, v_ln_v_g, v_ln_v_b, v_w_s, v_b_s, v_sinks, v_w_br_a, v_w_br_b, v_w_o, v_ln1_g, v_ln1_b, v_w_xq, v_w_xkv, v_w_xo, v_ln2_g, v_ln2_b, v_w_up, v_w_down, v_ln3_g, v_ln3_b):
    given = dict(locals())
    T, D = x.shape[1], x.shape[2]
    IN, GW, AW, XW, DFF = w_in.shape[2] * N_DEV, ln_v_g.shape[1], w_br_b.shape[1], w_xq.shape[2], w_up.shape[2] * N_DEV
    KVW = (IN - 2 * GW - AW - 2 * D) // 2
    d = Dims(T=T, D=D, IN=IN, GW=GW, G=GW // GROUP_DIM, AW=AW, KVW=KVW, HQ=AW // HEAD_DIM, HKV=KVW // HEAD_DIM, XW=XW,
             XH=XW // X_HEAD_DIM, MEM=mem.shape[1], DFF=DFF)
    place = jnp.stack([lax.axis_index("x"), lax.axis_index("y"), lax.axis_index("c")]).astype(jnp.int32)

    sched = _Sched()
    big_of = {g: (p, tr) for g, p, tr in BIG}

    def shard(l, g):
        p, tr = big_of[g]
        return (given[p][l].T if tr else given[p][l]).astype(BF16)

    Gs = [{}, {}]

    def plan_gather(l, names, hosts1, host2):
        shards = [shard(l, g) for g in names]
        if hosts1 is None:
            Gs[l].update(zip(names, _all_gather(shards, name=f"gather_{names[0]}_l{l}")))
            return

        spans = [(h, 1) if isinstance(h, str) else h for h in hosts1]
        total = sum(cnt for _, cnt in spans)

        def register(i, k0, into):
            host, cnt = spans[i]

            def done(outs):
                if i + 1 < len(spans):
                    register(i + 1, k0 + cnt, outs)
                else:
                    sched.at(host2, _gather_stage2(outs, lambda full: Gs[l].update(zip(names, full))))

            sched.at(host, _gather_stage1(shards, done, part=(k0, k0 + cnt, total), into=into))

        register(0, 0, None)

    for entry in GATHER_PLAN:
        plan_gather(*entry)
    Ss = []
    for l in range(DEPTH):
        S = {n: given[n][l].reshape(1, -1) for n in SMALL_NAMES if n not in ("w_s", "b_s", "sinks")}
        S["w_s"], S["b_sT"], S["sinks"] = w_s[l], b_s[l].T, sinks[l]
        Ss.append(S)

    owns, recvs = {}, {}

    smalls_seen = {}
    gathered_small = []

    def pack_small(src):
        parts = []
        for l in range(DEPTH):
            for n in SMALL_NAMES:
                a = src[l]["b_sT"].T if n == "b_s" else src[l][n]
                parts.append(a[0, : d.HQ] if n == "sinks" else a)
        return _pack(parts)

    def on_small(l, small):
        smalls_seen[l] = small
        if len(smalls_seen) == DEPTH:
            host1, host2 = SMALL_GRAD_HOSTS
            sched.at(host1, _gather_stage1([pack_small(smalls_seen)], lambda part: sched.at(
                host2, _gather_stage2(part, lambda full: gathered_small.append(full[0])))))

    def on_grad(l, g, grad):
        if g == "small":
            return on_small(l, grad)
        sib_host, chips_host = GRAD_HOSTS[l][g]

        def after_sibling(got):
            owns[(l, g)], send = _pair_sum(grad, got[0], place, name=f"grad_pair_sum_{g}_l{l}")
            hosts = chips_host if isinstance(chips_host, tuple) else (chips_host,)

            def register(k, into):
                def done(r):
                    if k + 1 < len(hosts):
                        register(k + 1, r)
                    else:
                        recvs[(l, g)] = r[0]

                task = _chips_stage([send], done, part=(k, len(hosts)), into=into)
                if hosts[k] is None:
                    _comm_only([task], name=f"grad_chips_{g}_l{l}")
                else:
                    sched.at(hosts[k], task)

            register(0, None)

        task = _sibling_stage([grad], after_sibling)
        if sib_host is None:
            _comm_only([task], name=f"grad_sibling_{g}_l{l}")
        else:
            sched.at(sib_host, task)

    loss, dX, bigs, smalls = _local_step(d, x[0], mem[0], loss_target[0], Gs, Ss, sched, on_grad)
    loss = lax.psum(loss[0, 0], ("x", "y", "c"))

    assert not sched.pending, sorted(sched.pending)

    def viewed(p, tr):
        return tr and given[p].shape[2] % LANES != 0

    res = {p: None for _, p, _ in BIG}
    for l in reversed(range(DEPTH)):
        for g, p, tr in BIG:
            view = viewed(p, tr)
            w3, m3, v3 = ((jnp.swapaxes(a, 1, 2) if view else a) for a in (given[p], given["m_" + p], given["v_" + p]))
            res[p] = _adam_big(w3, m3, v3, owns[(l, g)], recvs[(l, g)], l, tr and not view, res[p], name=f"adamw_{g}_l{l}")
    for g, p, tr in BIG:
        if viewed(p, tr):
            res[p] = [jnp.swapaxes(a, 1, 2) for a in res[p]]

    shapes = [given[n].shape[1:] for n in SMALL_NAMES]
    pw, pm, pv = (_pack([given[pre + n][l] for l in range(DEPTH) for n in SMALL_NAMES]) for pre in ("", "m_", "v_"))
    small_out = [_unpack(o, shapes * DEPTH) for o in _adam_small(pw, pm, pv, gathered_small[0], name="adamw_small")]

    names = ["w_in", "b_gate", "ln_v_g", "ln_v_b", "w_s", "b_s", "sinks", "w_br_a", "w_br_b", "w_o", "ln1_g", "ln1_b", "w_xq",
             "w_xkv", "w_xo", "ln2_g", "ln2_b", "w_up", "w_down", "ln3_g", "ln3_b"]
    out = [loss, dX[None]]
    for kind in range(4):
        for n in names:
            if n in SMALL_NAMES:
                i = SMALL_NAMES.index(n)
                out.append(jnp.stack([small_out[kind][l * len(SMALL_NAMES) + i] for l in range(DEPTH)]))
            else:
                out.append(res[n][kind])
    return tuple(out)
```

```python
import collections
import functools

import jax
import jax.numpy as jnp
from jax import lax
from jax.experimental import pallas as pl
from jax.experimental.pallas import tpu as pltpu

F32 = jnp.float32
BF16 = jnp.bfloat16
MESH = pl.DeviceIdType.MESH

N_DEV = 8
DEPTH = 2
CHUNK = 128
HEAD_DIM = 64
ROPE_HALF = HEAD_DIM // 2
X_HEAD_DIM = 128
GROUP_DIM = 128
LANES = 128
ROPE_THETA = 10000.0
LN_EPS = 1e-5
ALPHA = (2 * DEPTH) ** 0.25
ADAM_LR = 0.001
ADAM_B1 = 0.9
ADAM_B2 = 0.999
ADAM_EPS = 1e-08
ADAM_WD = 0.01
ADAM_STEP = 10
VMEM_BYTES = 64 * 1024 * 1024
VMEM_TEMP_BYTES = 20 * 1024 * 1024

Dims = collections.namedtuple("Dims", "T D IN GW G AW KVW HQ HKV XW XH MEM DFF")


def _offsets(d):
    off_v = d.GW
    off_q = 2 * d.GW
    off_k = off_q + d.AW
    off_va = off_k + d.KVW
    off_ga = off_va + d.KVW
    off_gb = off_ga + d.D
    return off_v, off_q, off_k, off_va, off_ga, off_gb


def _tile(dim, pref, align=LANES):
    if dim <= pref:
        return dim
    t = pref - pref % align
    while t >= align:
        if dim % t == 0:
            return t
        t -= align
    return dim


def _nbytes(shape, dtype):
    n = 1
    for s in shape:
        n *= s
    return n * jnp.dtype(dtype).itemsize


def _params(sem, blocks, scratch=0):
    need = 2 * sum(_nbytes(s, t) for s, t in blocks) + scratch + VMEM_TEMP_BYTES
    limit = min(max(need, 32 * 1024 * 1024), VMEM_BYTES - 4 * 1024 * 1024)
    return pltpu.CompilerParams(dimension_semantics=sem, vmem_limit_bytes=limit)


def _dot(a, b, kind):
    dn = {"nn": (((1,), (0,)), ((), ())), "nt": (((1,), (1,)), ((), ())), "tn": (((0,), (0,)), ((), ()))}[kind]
    return lax.dot_general(a, b, dn, preferred_element_type=F32)


def _gelu(x):
    c = 0.7978845608028654
    t = jnp.tanh(c * (x + 0.044715 * (x * x * x)))
    return 0.5 * x * (1.0 + t)


def _gelu_grad(x):
    c = 0.7978845608028654
    x2 = x * x
    t = jnp.tanh(c * (x + 0.044715 * (x2 * x)))
    return 0.5 * (1.0 + t) + 0.5 * x * (1.0 - t * t) * (c * (1.0 + 3.0 * 0.044715 * x2))


def _ln_fwd(z, g, b):
    mu = jnp.mean(z, axis=-1, keepdims=True)
    zc = z - mu
    var = jnp.mean(zc * zc, axis=-1, keepdims=True)
    rstd = lax.rsqrt(var + LN_EPS)
    xhat = zc * rstd
    return xhat * g + b, xhat, rstd


def _ln_bwd_vals(dy, xhat, rstd, g):
    gd = dy * g
    m1 = jnp.mean(gd, axis=-1, keepdims=True)
    m2 = jnp.mean(gd * xhat, axis=-1, keepdims=True)
    return rstd * (gd - m1 - xhat * m2)


def _acc_store(ref, val, first):
    @pl.when(first)
    def _():
        ref[...] = val

    @pl.when(jnp.logical_not(first))
    def _():
        ref[...] += val


def _matmul(a, b, kind, *, name, tm=512, tn=1024, tk=2048, out_defs=(("mn", F32),), epilogue=None, extras=(), sched=None):
    if kind == "nn":
        (M, K), (K2, N) = a.shape, b.shape
    elif kind == "nt":
        (M, K), (N, K2) = a.shape, b.shape
    else:
        (K, M), (K2, N) = a.shape, b.shape
    assert K == K2, (a.shape, b.shape, kind)
    tm, tn, tk = _tile(M, tm), _tile(N, tn), _tile(K, tk)
    nk = K // tk
    blocks = []

    def spec(shape, imap, dtype):
        blocks.append((shape, dtype))
        return pl.BlockSpec(shape, imap)

    if kind == "tn":
        a_spec = spec((tk, tm), lambda j, i, k: (k, i), a.dtype)
    else:
        a_spec = spec((tm, tk), lambda j, i, k: (i, k), a.dtype)
    if kind == "nt":
        b_spec = spec((tn, tk), lambda j, i, k: (j, k), b.dtype)
    else:
        b_spec = spec((tk, tn), lambda j, i, k: (k, j), b.dtype)
    in_specs = [a_spec, b_spec]
    operands = [a, b]
    for arr, ekind, off in extras:
        if ekind == "mn":
            assert off % tn == 0
            in_specs.append(spec((tm, tn), lambda j, i, k, o=off // tn: (i, j + o), arr.dtype))
        elif ekind == "row":
            assert off % tn == 0
            in_specs.append(spec((1, tn), lambda j, i, k, o=off // tn: (0, j + o), arr.dtype))
        else:
            in_specs.append(spec((tm, 1), lambda j, i, k: (i, 0), arr.dtype))
        operands.append(arr)
    out_shape, out_specs = [], []
    for od in out_defs:
        okind, dt = od[0], od[1]
        if okind == "mn":
            out_shape.append(jax.ShapeDtypeStruct((M, N), dt))
            out_specs.append(spec((tm, tn), lambda j, i, k: (i, j), dt))
        elif okind == "cols":
            assert od[3] % tn == 0
            out_shape.append(jax.ShapeDtypeStruct((M, od[2]), dt))
            out_specs.append(spec((tm, tn), lambda j, i, k, o=od[3] // tn: (i, j + o), dt))
        elif okind == "m1":
            assert N == tn
            out_shape.append(jax.ShapeDtypeStruct((M, 1), dt))
            out_specs.append(spec((tm, 1), lambda j, i, k: (i, 0), dt))
        else:
            out_shape.append(jax.ShapeDtypeStruct((1, N), F32))
            out_specs.append(spec((1, tn), lambda j, i, k: (0, j), F32))
    n_ex, n_out = len(extras), len(out_defs)
    ep = epilogue if epilogue is not None else (lambda acc: (acc,))
    in_place = nk > 1 and epilogue is None and tuple(out_defs) == (("mn", F32),)

    def body(*refs):
        a_ref, b_ref = refs[0], refs[1]
        ex_refs = refs[2 : 2 + n_ex]
        out_refs = refs[2 + n_ex : 2 + n_ex + n_out]
        i = pl.program_id(1)
        k = pl.program_id(2)

        def product():
            return _dot(a_ref[...], b_ref[...], kind)

        def finish(acc):
            vals = ep(acc, *[r[...] for r in ex_refs])
            for od, r, v in zip(out_defs, out_refs, vals):
                if od[0] == "acc":
                    _acc_store(r, v, i == 0)
                else:
                    r[...] = v.astype(od[1])

        if nk == 1:
            finish(product())
            return
        acc_ref = out_refs[0] if in_place else refs[-1]

        @pl.when(k == 0)
        def _():
            acc_ref[...] = product()

        if in_place:
            @pl.when(k > 0)
            def _():
                acc_ref[...] += product()
        else:
            if nk > 2:
                @pl.when(jnp.logical_and(k > 0, k < nk - 1))
                def _():
                    acc_ref[...] += product()

            @pl.when(k == nk - 1)
            def _():
                finish(acc_ref[...] + product())

    scratch = [pltpu.VMEM((tm, tn), F32)] if nk > 1 and not in_place else []
    outs = _hosted(
        body,
        name=name,
        sched=sched,
        grid=(N // tn, M // tm, nk),
        in_specs=in_specs,
        out_specs=out_specs,
        out_shape=out_shape,
        scratch_shapes=scratch,
        compiler_params=_params(("arbitrary", "arbitrary", "arbitrary"), blocks, _nbytes((tm, tn), F32) if scratch else 0),
    )(*operands)
    return outs if len(outs) > 1 else outs[0]


def _ep_resid_ln(acc, resid, g, b):
    y, xhat, rstd = _ln_fwd(ALPHA * resid + acc, g, b)
    return y, y, xhat, rstd


def _ep_resid(acc, resid):
    return (ALPHA * resid + acc,)


def _ep_resid_ln_bwd(acc, resid, xhat, rstd, g):
    dy = ALPHA * resid + acc
    dz = _ln_bwd_vals(dy, xhat, rstd, g)
    return dz, dz, jnp.sum(dy * xhat, axis=0, keepdims=True), jnp.sum(dy, axis=0, keepdims=True)


def _ep_relu2(acc):
    r = jnp.maximum(acc, 0.0)
    return acc, r * r


def _ep_relu2_bwd(acc, h):
    return (acc * (2.0 * jnp.maximum(h, 0.0)),)


def _ep_gate_bwd(acc, ga, gb, ya, yb, bga, bgb):
    sa = jax.nn.sigmoid(ga + bga)
    sb = jax.nn.sigmoid(gb + bgb)
    dga = acc * ya * (sa * (1.0 - sa))
    dgb = acc * yb * (sb * (1.0 - sb))
    return (acc * sa, acc * sb, dga, dgb, jnp.sum(dga, axis=0, keepdims=True), jnp.sum(dgb, axis=0, keepdims=True))


def _ln_bwd(dy, xhat, rstd, g, *, name, sched=None, target=None):
    T, D = dy.shape
    tm = _tile(T, 256)
    head = target is not None

    def body(*refs):
        dy_ref, xh_ref, rs_ref, g_ref = refs[:4]
        dz_ref, dzb_ref, dg_ref, db_ref = refs[4 + head : 8 + head]
        first = pl.program_id(0) == 0
        dyv, xh = dy_ref[...], xh_ref[...]
        if head:
            err = dyv - refs[4][...]
            dyv = err * (1.0 / D)
            part = 0.5 * jnp.sum(jnp.sum(err * err, axis=1, keepdims=True) * (1.0 / D), axis=0, keepdims=True)
            _acc_store(refs[-1], part, first)
        dz = _ln_bwd_vals(dyv, xh, rs_ref[...], g_ref[...])
        dz_ref[...] = dz
        dzb_ref[...] = dz.astype(BF16)
        _acc_store(dg_ref, jnp.sum(dyv * xh, axis=0, keepdims=True), first)
        _acc_store(db_ref, jnp.sum(dyv, axis=0, keepdims=True), first)

    row = pl.BlockSpec((tm, D), lambda i: (i, 0))
    vec = pl.BlockSpec((1, D), lambda i: (0, 0))
    one = pl.BlockSpec((1, 1), lambda i: (0, 0))
    return _hosted(
        body,
        name=name,
        sched=sched,
        grid=(T // tm,),
        in_specs=[row, row, pl.BlockSpec((tm, 1), lambda i: (i, 0)), vec] + [row] * head,
        out_specs=[row, row, vec, vec] + [one] * head,
        out_shape=[
            jax.ShapeDtypeStruct((T, D), F32),
            jax.ShapeDtypeStruct((T, D), BF16),
            jax.ShapeDtypeStruct((1, D), F32),
            jax.ShapeDtypeStruct((1, D), F32),
        ] + [jax.ShapeDtypeStruct((1, 1), F32)] * head,
        compiler_params=_params(("arbitrary",), [((tm, D), F32)] * (4 + head)),
    )(*([dy, xhat, rstd, g] + [target] * head))


def _masked_mix_weights(wm_ref, G):
    r = lax.broadcasted_iota(jnp.int32, (CHUNK, CHUNK), 0)
    c = lax.broadcasted_iota(jnp.int32, (CHUNK, CHUNK), 1)
    return [jnp.where(c <= r, wm_ref[g], 0.0).astype(BF16) for g in range(G)]


def _gmlp_fwd(d, P, lnv_g, lnv_b, wm, bsT, *, name, sched=None):
    T, GW, G = d.T, d.GW, d.G
    tm = _tile(T, 256)
    nc = tm // CHUNK

    def body(u_ref, v_ref, g_ref, b_ref, wm_ref, bs_ref, o_ref):
        gu = _gelu(u_ref[...])
        vn, _, _ = _ln_fwd(_gelu(v_ref[...]), g_ref[...], b_ref[...])
        vn = vn.astype(BF16)
        wl = _masked_mix_weights(wm_ref, G)
        for c in range(nc):
            rows = slice(c * CHUNK, (c + 1) * CHUNK)
            for g in range(G):
                cols = slice(g * GROUP_DIM, (g + 1) * GROUP_DIM)
                mixed = _dot(wl[g], vn[rows, cols], "nn") + bs_ref[:, g : g + 1]
                o_ref[rows, cols] = (gu[rows, cols] * mixed).astype(BF16)

    return _hosted(
        body,
        name=name,
        sched=sched,
        grid=(T // tm,),
        in_specs=[
            pl.BlockSpec((tm, GW), lambda i: (i, 0)),
            pl.BlockSpec((tm, GW), lambda i: (i, 1)),
            pl.BlockSpec((1, GW), lambda i: (0, 0)),
            pl.BlockSpec((1, GW), lambda i: (0, 0)),
            pl.BlockSpec((G, CHUNK, CHUNK), lambda i: (0, 0, 0)),
            pl.BlockSpec((CHUNK, G), lambda i: (0, 0)),
        ],
        out_specs=pl.BlockSpec((tm, GW), lambda i: (i, 0)),
        out_shape=jax.ShapeDtypeStruct((T, GW), BF16),
        compiler_params=_params(("arbitrary",), [((tm, GW), F32)] * 3),
    )(P, P, lnv_g, lnv_b, wm, bsT)


def _gmlp_bwd(d, P, dya_b, Ga, lnv_g, lnv_b, wm, bsT, dP, *, name, sched=None):
    T, D, GW, G = d.T, d.D, d.GW, d.G
    tm = _tile(T, 256)
    nc = tm // CHUNK

    def body(u_ref, v_ref, dya_ref, ga_ref, g_ref, b_ref, wm_ref, bs_ref, dp_in, duv_ref, dwm_ref, dbs_ref, dlg_ref, dlb_ref, dgu_s, dvn_s):
        first = pl.program_id(0) == 0
        dsgu = _dot(dya_ref[...], ga_ref[...], "nn")
        u, v = u_ref[...], v_ref[...]
        gu = _gelu(u)
        lng = g_ref[...]
        vn, xh, rstd = _ln_fwd(_gelu(v), lng, b_ref[...])
        vn = vn.astype(BF16)
        wl = _masked_mix_weights(wm_ref, G)
        r = lax.broadcasted_iota(jnp.int32, (CHUNK, CHUNK), 0)
        cc = lax.broadcasted_iota(jnp.int32, (CHUNK, CHUNK), 1)
        lane_g = lax.broadcasted_iota(jnp.int32, (CHUNK, G), 1)
        dbs = jnp.zeros((CHUNK, G), F32)
        for g in range(G):
            cols = slice(g * GROUP_DIM, (g + 1) * GROUP_DIM)
            dw = jnp.zeros((CHUNK, CHUNK), F32)
            for c in range(nc):
                rows = slice(c * CHUNK, (c + 1) * CHUNK)
                mixed = _dot(wl[g], vn[rows, cols], "nn") + bs_ref[:, g : g + 1]
                ds = dsgu[rows, cols]
                dgu_s[rows, cols] = ds * mixed
                dmx = ds * gu[rows, cols]
                dbs = dbs + jnp.where(lane_g == g, jnp.sum(dmx, axis=1, keepdims=True), 0.0)
                dmx = dmx.astype(BF16)
                dw = dw + _dot(dmx, vn[rows, cols], "nt")
                dvn_s[rows, cols] = _dot(wl[g], dmx, "tn")
            _acc_store(dwm_ref.at[g], jnp.where(cc <= r, dw, 0.0), first)
        _acc_store(dbs_ref, dbs, first)
        dvn = dvn_s[...]
        _acc_store(dlg_ref, jnp.sum(dvn * xh, axis=0, keepdims=True), first)
        _acc_store(dlb_ref, jnp.sum(dvn, axis=0, keepdims=True), first)
        dgv = _ln_bwd_vals(dvn, xh, rstd, lng)
        duv_ref[:, :GW] = (dgu_s[...] * _gelu_grad(u)).astype(BF16)
        duv_ref[:, GW:] = (dgv * _gelu_grad(v)).astype(BF16)

    return _hosted(
        body,
        name=name,
        sched=sched,
        grid=(T // tm,),
        in_specs=[
            pl.BlockSpec((tm, GW), lambda i: (i, 0)),
            pl.BlockSpec((tm, GW), lambda i: (i, 1)),
            pl.BlockSpec((tm, D), lambda i: (i, 0)),
            pl.BlockSpec((D, GW), lambda i: (0, 0)),
            pl.BlockSpec((1, GW), lambda i: (0, 0)),
            pl.BlockSpec((1, GW), lambda i: (0, 0)),
            pl.BlockSpec((G, CHUNK, CHUNK), lambda i: (0, 0, 0)),
            pl.BlockSpec((CHUNK, G), lambda i: (0, 0)),
            pl.BlockSpec(memory_space=pl.ANY),
        ],
        out_specs=[
            pl.BlockSpec((tm, 2 * GW), lambda i: (i, 0)),
            pl.BlockSpec((G, CHUNK, CHUNK), lambda i: (0, 0, 0)),
            pl.BlockSpec((CHUNK, G), lambda i: (0, 0)),
            pl.BlockSpec((1, GW), lambda i: (0, 0)),
            pl.BlockSpec((1, GW), lambda i: (0, 0)),
        ],
        out_shape=[
            jax.ShapeDtypeStruct(dP.shape, BF16),
            jax.ShapeDtypeStruct((G, CHUNK, CHUNK), F32),
            jax.ShapeDtypeStruct((CHUNK, G), F32),
            jax.ShapeDtypeStruct((1, GW), F32),
            jax.ShapeDtypeStruct((1, GW), F32),
        ],
        scratch_shapes=[pltpu.VMEM((tm, GW), F32), pltpu.VMEM((tm, GW), F32)],
        input_output_aliases={8: 0},
        compiler_params=_params(
            ("arbitrary",), [((tm, GW), F32)] * 3 + [((tm, D), BF16), ((D, GW), BF16)], 2 * _nbytes((tm, GW), F32)
        ),
    )(P, P, dya_b, Ga, lnv_g, lnv_b, wm, bsT, dP)


def _swap_halves(x):
    w = x.shape[1]
    lane = lax.broadcasted_iota(jnp.int32, x.shape, 1)
    return jnp.where((lane % HEAD_DIM) < ROPE_HALF, pltpu.roll(x, w - ROPE_HALF, 1), pltpu.roll(x, ROPE_HALF, 1))


def _lane_tile(t, width):
    reps = width // LANES
    return t if reps == 1 else jnp.tile(t, (1, reps))


def _rope(x, cos2, sin2):
    w = x.shape[1]
    return x * _lane_tile(cos2, w) + _swap_halves(x) * _lane_tile(sin2, w)


def _rope_bwd(g, cos2, sin2):
    w = g.shape[1]
    return g * _lane_tile(cos2, w) - _swap_halves(g) * _lane_tile(sin2, w)


PAIR = LANES // HEAD_DIM


def _swa_valid(i):
    s = lax.broadcasted_iota(jnp.int32, (2 * CHUNK, CHUNK), 0)
    t = lax.broadcasted_iota(jnp.int32, (2 * CHUNK, CHUNK), 1)
    cur = jnp.logical_and(s >= CHUNK, s - CHUNK <= t)
    prev = jnp.logical_and(jnp.logical_and(s < CHUNK, s > t), i > 0)
    return jnp.logical_or(cur, prev)


def _half_variants(x, odd):
    lane = lax.broadcasted_iota(jnp.int32, x.shape, 1)
    if odd:
        hi = jnp.where(lane >= HEAD_DIM, x, jnp.zeros_like(x))
        return pltpu.roll(hi, HEAD_DIM, 1), hi
    lo = jnp.where(lane < HEAD_DIM, x, jnp.zeros_like(x))
    return lo, pltpu.roll(lo, HEAD_DIM, 1)


def _swa_probs_t(kx, qp, sink, valid):
    s = jnp.where(valid, _dot(kx, qp, "nt") * (HEAD_DIM**-0.5), -jnp.inf)
    m = jnp.maximum(jnp.max(s, axis=0, keepdims=True), sink)
    e = jnp.exp(s - m)
    e_s = jnp.exp(sink - m)
    den = jnp.sum(e, axis=0, keepdims=True) + e_s
    return e / den, e_s / den


def _swa_load(q_ref, kc_ref, kp_ref, vc_ref, vp_ref, cc, sc, cp, sp):
    qr = _rope(q_ref[...], cc, sc).astype(BF16)
    kcat = jnp.concatenate([_rope(kp_ref[...], cp, sp), _rope(kc_ref[...], cc, sc)], axis=0)
    vcat = jnp.concatenate([vp_ref[...], vc_ref[...]], axis=0)
    return qr, kcat, vcat


def _swa_specs(d):
    _, off_q, off_k, off_va, _, _ = _offsets(d)
    assert off_q % d.AW == 0 and off_k % d.KVW == 0 and off_va % d.KVW == 0
    prev = lambda i: jnp.maximum(i - 1, 0)
    return [
        pl.BlockSpec(memory_space=pltpu.SMEM),
        pl.BlockSpec((CHUNK, d.AW), lambda i, o=off_q // d.AW: (i, o)),
        pl.BlockSpec((CHUNK, d.KVW), lambda i, o=off_k // d.KVW: (i, o)),
        pl.BlockSpec((CHUNK, d.KVW), lambda i, o=off_k // d.KVW: (prev(i), o)),
        pl.BlockSpec((CHUNK, d.KVW), lambda i, o=off_va // d.KVW: (i, o)),
        pl.BlockSpec((CHUNK, d.KVW), lambda i, o=off_va // d.KVW: (prev(i), o)),
        pl.BlockSpec((CHUNK, LANES), lambda i: (i, 0)),
        pl.BlockSpec((CHUNK, LANES), lambda i: (i, 0)),
        pl.BlockSpec((CHUNK, LANES), lambda i: (prev(i), 0)),
        pl.BlockSpec((CHUNK, LANES), lambda i: (prev(i), 0)),
    ]


def _swa_fwd(d, P, cos2, sin2, sinks, *, name, sched=None):
    T, AW, HQ, HKV = d.T, d.AW, d.HQ, d.HKV
    grp = HQ // HKV
    assert grp % PAIR == 0 and HKV % PAIR == 0

    def body(sink_ref, q_ref, kc_ref, kp_ref, vc_ref, vp_ref, cc_ref, sc_ref, cp_ref, sp_ref, o_ref):
        i = pl.program_id(0)
        qr, kcat, vcat = _swa_load(q_ref, kc_ref, kp_ref, vc_ref, vp_ref, cc_ref[...], sc_ref[...], cp_ref[...], sp_ref[...])
        valid = _swa_valid(i)
        for kv in range(HKV):
            col = slice((kv // PAIR) * LANES, (kv // PAIR + 1) * LANES)
            kx = [t.astype(BF16) for t in _half_variants(kcat[:, col], kv % PAIR)]
            vx = [t.astype(BF16) for t in _half_variants(vcat[:, col], kv % PAIR)]
            for pp in range(grp // PAIR):
                p = kv * (grp // PAIR) + pp
                qs = slice(p * LANES, (p + 1) * LANES)
                out = jnp.zeros((CHUNK, LANES), F32)
                for half in range(PAIR):
                    pt, _ = _swa_probs_t(kx[half], qr[:, qs], sink_ref[PAIR * p + half], valid)
                    out = out + _dot(pt.astype(BF16), vx[half], "tn")
                o_ref[:, qs] = out.astype(BF16)

    return _hosted(
        body,
        name=name,
        sched=sched,
        grid=(T // CHUNK,),
        in_specs=_swa_specs(d),
        out_specs=pl.BlockSpec((CHUNK, AW), lambda i: (i, 0)),
        out_shape=jax.ShapeDtypeStruct((T, AW), BF16),
        compiler_params=_params(("arbitrary",), [((CHUNK, AW), F32)] * 3),
    )(sinks, P, P, P, P, P, cos2, sin2, cos2, sin2)


def _swa_bwd(d, P, dyb_b, Gb, cos2, sin2, sinks, dP, *, name, sched=None):
    T, D, AW, KVW, HQ, HKV = d.T, d.D, d.AW, d.KVW, d.HQ, d.HKV
    grp = HQ // HKV
    nb = T // CHUNK
    scale = HEAD_DIM**-0.5
    off_q = _offsets(d)[1]
    assert grp % PAIR == 0 and HKV % PAIR == 0 and off_q % AW == 0

    def body(sink_ref, q_ref, kc_ref, kp_ref, vc_ref, vp_ref, cc_ref, sc_ref, cp_ref, sp_ref, dyb_ref, gb_ref, dp_in,
             dq_ref, dkv_ref, dsk_ref, acc_s, dq_s, dk_s, dv_s):
        i = pl.program_id(0)
        cc, sc, cp, sp = cc_ref[...], sc_ref[...], cp_ref[...], sp_ref[...]
        datt = _dot(dyb_ref[...], gb_ref[...], "nn").astype(BF16)
        qr, kcat, vcat = _swa_load(q_ref, kc_ref, kp_ref, vc_ref, vp_ref, cc, sc, cp, sp)
        valid = _swa_valid(i)
        lane1 = lax.broadcasted_iota(jnp.int32, (1, LANES), 1)
        lane2 = lax.broadcasted_iota(jnp.int32, (2 * CHUNK, LANES), 1)
        dsk = jnp.zeros((1, LANES), F32)
        for kvp in range(HKV // PAIR):
            col = slice(kvp * LANES, (kvp + 1) * LANES)
            dk_col = jnp.zeros((2 * CHUNK, LANES), F32)
            dv_col = jnp.zeros((2 * CHUNK, LANES), F32)
            for odd in range(PAIR):
                kv = kvp * PAIR + odd
                kx = [t.astype(BF16) for t in _half_variants(kcat[:, col], odd)]
                vx = [t.astype(BF16) for t in _half_variants(vcat[:, col], odd)]
                dk_half = [jnp.zeros((2 * CHUNK, LANES), F32) for _ in range(PAIR)]
                dv_half = [jnp.zeros((2 * CHUNK, LANES), F32) for _ in range(PAIR)]
                for pp in range(grp // PAIR):
                    p = kv * (grp // PAIR) + pp
                    qs = slice(p * LANES, (p + 1) * LANES)
                    qp, dop = qr[:, qs], datt[:, qs]
                    dq = jnp.zeros((CHUNK, LANES), F32)
                    for half in range(PAIR):
                        h = PAIR * p + half
                        pt, p_s = _swa_probs_t(kx[half], qp, sink_ref[h], valid)
                        dpt = _dot(vx[half], dop, "nt")
                        rs = jnp.sum(pt * dpt, axis=0, keepdims=True)
                        dst = (pt * (dpt - rs) * scale).astype(BF16)
                        dsk = dsk + jnp.where(lane1 == h, -jnp.sum(p_s * rs, axis=1, keepdims=True), 0.0)
                        dq = dq + _dot(dst, kx[half], "tn")
                        dk_half[half] = dk_half[half] + _dot(dst, qp, "nn")
                        dv_half[half] = dv_half[half] + _dot(pt.astype(BF16), dop, "nn")
                    dq_s[:, qs] = dq
                for acc_half, is_k in ((dk_half, True), (dv_half, False)):
                    lo = jnp.where(lane2 < HEAD_DIM, acc_half[0], 0.0)
                    hi = jnp.where(lane2 >= HEAD_DIM, acc_half[1], 0.0)
                    both = (pltpu.roll(lo, HEAD_DIM, 1) + hi) if odd else (lo + pltpu.roll(hi, HEAD_DIM, 1))
                    if is_k:
                        dk_col = dk_col + both
                    else:
                        dv_col = dv_col + both
            dk_s[:, col] = dk_col
            dv_s[:, col] = dv_col
        dq_ref[...] = _rope_bwd(dq_s[...], cc, sc).astype(BF16)
        cur = pl.ds(pl.multiple_of(i * CHUNK, CHUNK), CHUNK)
        acc_s[cur, :KVW] = _rope_bwd(dk_s[CHUNK:, :], cc, sc)
        acc_s[cur, KVW:] = dv_s[CHUNK:, :]

        @pl.when(i > 0)
        def _():
            prv = pl.ds(pl.multiple_of((i - 1) * CHUNK, CHUNK), CHUNK)
            acc_s[prv, :KVW] += _rope_bwd(dk_s[:CHUNK, :], cp, sp)
            acc_s[prv, KVW:] += dv_s[:CHUNK, :]

        _acc_store(dsk_ref, dsk, i == 0)

        @pl.when(i == nb - 1)
        def _():
            dkv_ref[...] = acc_s[...].astype(BF16)

    return _hosted(
        body,
        name=name,
        sched=sched,
        grid=(nb,),
        in_specs=_swa_specs(d) + [pl.BlockSpec((CHUNK, D), lambda i: (i, 0)), pl.BlockSpec((D, AW), lambda i: (0, 0)),
                                   pl.BlockSpec(memory_space=pl.ANY)],
        out_specs=[
            pl.BlockSpec((CHUNK, AW), lambda i, o=off_q // AW: (i, o)),
            pl.BlockSpec((T, 2 * KVW), lambda i: (0, 0)),
            pl.BlockSpec((1, LANES), lambda i: (0, 0)),
        ],
        out_shape=[
            jax.ShapeDtypeStruct(dP.shape, BF16),
            jax.ShapeDtypeStruct((T, 2 * KVW), BF16),
            jax.ShapeDtypeStruct((1, LANES), F32),
        ],
        input_output_aliases={12: 0},
        scratch_shapes=[
            pltpu.VMEM((T, 2 * KVW), F32),
            pltpu.VMEM((CHUNK, AW), F32),
            pltpu.VMEM((2 * CHUNK, KVW), F32),
            pltpu.VMEM((2 * CHUNK, KVW), F32),
        ],
        compiler_params=_params(
            ("arbitrary",), [((CHUNK, AW), F32)] * 3 + [((D, AW), BF16), ((T, 2 * KVW), BF16)], _nbytes((T, 2 * KVW), F32)
        ),
    )(sinks, P, P, P, P, P, cos2, sin2, cos2, sin2, dyb_b, Gb, dP)


def _place_cols(dst, src, off, *, name, sched=None):
    T, w = src.shape
    tm, tw = _tile(T, 512), _tile(w, 512)
    assert off % tw == 0

    def body(src_ref, dst_in, out_ref):
        out_ref[...] = src_ref[...]

    return _hosted(
        body,
        name=name,
        sched=sched,
        grid=(T // tm, w // tw),
        in_specs=[pl.BlockSpec((tm, tw), lambda i, j: (i, j)), pl.BlockSpec(memory_space=pl.ANY)],
        out_specs=pl.BlockSpec((tm, tw), lambda i, j, o=off // tw: (i, j + o)),
        out_shape=jax.ShapeDtypeStruct(dst.shape, dst.dtype),
        input_output_aliases={1: 0},
        compiler_params=_params(("arbitrary", "arbitrary"), [((tm, tw), dst.dtype)] * 2),
    )(src, dst)


def _branch_merge(d, sgu, att, Ga, Gb, P, b_gate, *, name, sched=None):
    T, D, GW, AW = d.T, d.D, d.GW, d.AW
    _, _, _, _, off_ga, off_gb = _offsets(d)
    tm, tn = _tile(T, 1024), _tile(D, 512)
    assert off_ga % tn == 0 and off_gb % tn == 0

    def body(s_ref, a_ref, ga_w, gb_w, ga_ref, gb_ref, bga_ref, bgb_ref, ya_ref, yb_ref, mg_ref):
        ya = _dot(s_ref[...], ga_w[...], "nt")
        yb = _dot(a_ref[...], gb_w[...], "nt")
        ya_ref[...] = ya
        yb_ref[...] = yb
        sa = jax.nn.sigmoid(ga_ref[...] + bga_ref[...])
        sb = jax.nn.sigmoid(gb_ref[...] + bgb_ref[...])
        mg_ref[...] = (sa * ya + sb * yb).astype(BF16)

    tile = pl.BlockSpec((tm, tn), lambda j, i: (i, j))
    return _hosted(
        body,
        name=name,
        sched=sched,
        grid=(D // tn, T // tm),
        in_specs=[
            pl.BlockSpec((tm, GW), lambda j, i: (i, 0)),
            pl.BlockSpec((tm, AW), lambda j, i: (i, 0)),
            pl.BlockSpec((tn, GW), lambda j, i: (j, 0)),
            pl.BlockSpec((tn, AW), lambda j, i: (j, 0)),
            pl.BlockSpec((tm, tn), lambda j, i, o=off_ga // tn: (i, j + o)),
            pl.BlockSpec((tm, tn), lambda j, i, o=off_gb // tn: (i, j + o)),
            pl.BlockSpec((1, tn), lambda j, i: (0, j)),
            pl.BlockSpec((1, tn), lambda j, i, o=D // tn: (0, j + o)),
        ],
        out_specs=[tile, tile, tile],
        out_shape=[jax.ShapeDtypeStruct((T, D), F32), jax.ShapeDtypeStruct((T, D), F32), jax.ShapeDtypeStruct((T, D), BF16)],
        compiler_params=_params(
            ("arbitrary", "arbitrary"),
            [((tm, GW), BF16), ((tm, AW), BF16), ((tn, GW), BF16), ((tn, AW), BF16)] + [((tm, tn), F32)] * 5,
        ),
    )(sgu, att, Ga, Gb, P, P, b_gate, b_gate)


def _xattn_probs(qh, kh):
    s = _dot(qh, kh, "nt") * (X_HEAD_DIM**-0.5)
    e = jnp.exp(s - jnp.max(s, axis=1, keepdims=True))
    return e / jnp.sum(e, axis=1, keepdims=True)


def _xattn_fwd(d, X, Xb, kv, Gxq, Gxo, ln_g, ln_b, *, name, sched=None):
    T, D, XW, XH, MEM = d.T, d.D, d.XW, d.XH, d.MEM
    tm = _tile(T, 256)

    def body(x_ref, xb_ref, kv_ref, wq_ref, wo_ref, g_ref, b_ref, y_ref, yb_ref, xh_ref, rs_ref, q_ref, o_ref, o_s):
        q = _dot(xb_ref[...], wq_ref[...], "nn").astype(BF16)
        q_ref[...] = q
        for h in range(XH):
            hs = slice(h * X_HEAD_DIM, (h + 1) * X_HEAD_DIM)
            p = _xattn_probs(q[:, hs], kv_ref[:, hs]).astype(BF16)
            o_s[:, hs] = _dot(p, kv_ref[:, XW + h * X_HEAD_DIM : XW + (h + 1) * X_HEAD_DIM], "nn").astype(BF16)
        o = o_s[...]
        o_ref[...] = o
        y, xhat, rstd = _ln_fwd(ALPHA * x_ref[...] + _dot(o, wo_ref[...], "nt"), g_ref[...], b_ref[...])
        y_ref[...] = y
        yb_ref[...] = y.astype(BF16)
        xh_ref[...] = xhat
        rs_ref[...] = rstd

    row = pl.BlockSpec((tm, D), lambda i: (i, 0))
    nar = pl.BlockSpec((tm, XW), lambda i: (i, 0))
    vec = pl.BlockSpec((1, D), lambda i: (0, 0))
    return _hosted(
        body,
        name=name,
        sched=sched,
        grid=(T // tm,),
        in_specs=[
            row,
            row,
            pl.BlockSpec((MEM, 2 * XW), lambda i: (0, 0)),
            pl.BlockSpec((D, XW), lambda i: (0, 0)),
            pl.BlockSpec((D, XW), lambda i: (0, 0)),
            vec,
            vec,
        ],
        out_specs=[row, row, row, pl.BlockSpec((tm, 1), lambda i: (i, 0)), nar, nar],
        out_shape=[
            jax.ShapeDtypeStruct((T, D), F32),
            jax.ShapeDtypeStruct((T, D), BF16),
            jax.ShapeDtypeStruct((T, D), F32),
            jax.ShapeDtypeStruct((T, 1), F32),
            jax.ShapeDtypeStruct((T, XW), BF16),
            jax.ShapeDtypeStruct((T, XW), BF16),
        ],
        scratch_shapes=[pltpu.VMEM((tm, XW), BF16)],
        compiler_params=_params(("arbitrary",), [((tm, D), F32)] * 4 + [((D, XW), BF16)] * 2),
    )(X, Xb, kv, Gxq, Gxo, ln_g, ln_b)


def _xattn_bwd(d, dz, dzb, q_b, kv, Gxq, Gxo, xhat, rstd, ln_g, *, name, sched=None):
    T, D, XW, XH, MEM = d.T, d.D, d.XW, d.XH, d.MEM
    tm = _tile(T, 256)
    scale = X_HEAD_DIM**-0.5

    def body(dz_ref, dzb_ref, q_ref, kv_ref, wq_ref, wo_ref, xh_ref, rs_ref, g_ref, dx_ref, dxb_ref, dg_ref, db_ref, dq_ref, dkv_ref, dq_s):
        first = pl.program_id(0) == 0
        do = _dot(dzb_ref[...], wo_ref[...], "nn").astype(BF16)
        for h in range(XH):
            hs = slice(h * X_HEAD_DIM, (h + 1) * X_HEAD_DIM)
            vs = slice(XW + h * X_HEAD_DIM, XW + (h + 1) * X_HEAD_DIM)
            kh, vh, qh, doh = kv_ref[:, hs], kv_ref[:, vs], q_ref[:, hs], do[:, hs]
            p = _xattn_probs(qh, kh)
            dp = _dot(doh, vh, "nt")
            ds = (p * (dp - jnp.sum(p * dp, axis=1, keepdims=True)) * scale).astype(BF16)
            dq_s[:, hs] = _dot(ds, kh, "nn").astype(BF16)
            _acc_store(dkv_ref.at[h], _dot(ds, qh, "tn"), first)
            _acc_store(dkv_ref.at[XH + h], _dot(p.astype(BF16), doh, "tn"), first)
        dq = dq_s[...]
        dq_ref[...] = dq
        dy = ALPHA * dz_ref[...] + _dot(dq, wq_ref[...], "nt")
        xh = xh_ref[...]
        dz_in = _ln_bwd_vals(dy, xh, rs_ref[...], g_ref[...])
        dx_ref[...] = dz_in
        dxb_ref[...] = dz_in.astype(BF16)
        _acc_store(dg_ref, jnp.sum(dy * xh, axis=0, keepdims=True), first)
        _acc_store(db_ref, jnp.sum(dy, axis=0, keepdims=True), first)

    row = pl.BlockSpec((tm, D), lambda i: (i, 0))
    nar = pl.BlockSpec((tm, XW), lambda i: (i, 0))
    vec = pl.BlockSpec((1, D), lambda i: (0, 0))
    return _hosted(
        body,
        name=name,
        sched=sched,
        grid=(T // tm,),
        in_specs=[
            row,
            row,
            nar,
            pl.BlockSpec((MEM, 2 * XW), lambda i: (0, 0)),
            pl.BlockSpec((D, XW), lambda i: (0, 0)),
            pl.BlockSpec((D, XW), lambda i: (0, 0)),
            row,
            pl.BlockSpec((tm, 1), lambda i: (i, 0)),
            vec,
        ],
        out_specs=[row, row, vec, vec, nar, pl.BlockSpec((2 * XH, MEM, X_HEAD_DIM), lambda i: (0, 0, 0))],
        out_shape=[
            jax.ShapeDtypeStruct((T, D), F32),
            jax.ShapeDtypeStruct((T, D), BF16),
            jax.ShapeDtypeStruct((1, D), F32),
            jax.ShapeDtypeStruct((1, D), F32),
            jax.ShapeDtypeStruct((T, XW), BF16),
            jax.ShapeDtypeStruct((2 * XH, MEM, X_HEAD_DIM), F32),
        ],
        scratch_shapes=[pltpu.VMEM((tm, XW), BF16)],
        compiler_params=_params(("arbitrary",), [((tm, D), F32)] * 5 + [((D, XW), BF16)] * 2),
    )(dz, dzb, q_b, kv, Gxq, Gxo, xhat, rstd, ln_g)


def _resid_ln_outs():
    return (("mn", F32), ("mn", BF16), ("mn", F32), ("m1", F32))


def _layer_fwd(d, X, Xb, memb, G, S, cos2, sin2, tag, sched=None):
    D = d.D
    mm = functools.partial(_matmul, sched=sched)
    P = mm(Xb, G["in"], "nt", name=f"proj_{tag}", tm=1024, tn=_tile(d.IN, 1280))
    sgu = _gmlp_fwd(d, P, S["ln_v_g"], S["ln_v_b"], S["w_s"], S["b_sT"], name=f"gmlp_fwd_{tag}", sched=sched)
    att = _swa_fwd(d, P, cos2, sin2, S["sinks"], name=f"swa_fwd_{tag}", sched=sched)
    ya, yb, merged = _branch_merge(d, sgu, att, G["a"], G["b"], P, S["b_gate"], name=f"merge_{tag}", sched=sched)
    X1, X1b, xh1, rs1 = mm(
        merged, G["o"], "nn", name=f"oproj_ln_{tag}", tn=D, tk=1024, out_defs=_resid_ln_outs(), epilogue=_ep_resid_ln,
        extras=((X, "mn", 0), (S["ln1_g"], "row", 0), (S["ln1_b"], "row", 0)),
    )
    kv = mm(memb, G["xkv"], "nn", name=f"xkv_{tag}", out_defs=(("mn", BF16),))
    X2, X2b, xh2, rs2, q_b, o_b = _xattn_fwd(d, X1, X1b, kv, G["xq"], G["xo"], S["ln2_g"], S["ln2_b"], name=f"xattn_fwd_{tag}", sched=sched)
    H, A = mm(X2b, G["up"], "nt", name=f"up_{tag}", tm=1024, out_defs=(("mn", F32), ("mn", BF16)), epilogue=_ep_relu2)
    X3, X3b, xh3, rs3 = mm(
        A, G["down"], "nn", name=f"down_ln_{tag}", tn=D, tk=1024, out_defs=_resid_ln_outs(), epilogue=_ep_resid_ln,
        extras=((X2, "mn", 0), (S["ln3_g"], "row", 0), (S["ln3_b"], "row", 0)),
    )
    saved = dict(Xb=Xb, P=P, sgu=sgu, att=att, ya=ya, yb=yb, merged=merged, X1b=X1b, xh1=xh1, rs1=rs1, kv=kv, q_b=q_b,
                 o_b=o_b, X2b=X2b, xh2=xh2, rs2=rs2, H=H, A=A, xh3=xh3, rs3=rs3)
    return X3, X3b, saved


def _layer_bwd(d, dXout, sv, memb, G, S, cos2, sin2, tag, sched=None, on_grad=None, target=None):
    D = d.D
    mm = functools.partial(_matmul, sched=sched)
    emit = on_grad if on_grad is not None else (lambda n, g: None)
    wide = dict(tn=D)
    _, _, off_k, _, off_ga, off_gb = _offsets(d)
    bf = (("mn", BF16),)
    dz3, dz3b, dg3, db3, *loss = _ln_bwd(dXout, sv["xh3"], sv["rs3"], S["ln3_g"], name=f"ln3_bwd_{tag}", sched=sched, target=target)
    dHb = mm(dz3b, G["down"], "nt", name=f"down_dx_{tag}", tm=1024, out_defs=bf, epilogue=_ep_relu2_bwd, extras=((sv["H"], "mn", 0),))
    g_down = mm(sv["A"], dz3b, "tn", name=f"down_dw_{tag}", **wide)
    emit("down", g_down)
    ln_bwd_outs = (("mn", F32), ("mn", BF16), ("acc", F32), ("acc", F32))
    dz2, dz2b, dg2, db2 = mm(
        dHb, G["up"], "nn", name=f"up_dx_{tag}", tn=D, tk=1024, out_defs=ln_bwd_outs, epilogue=_ep_resid_ln_bwd,
        extras=((dz3, "mn", 0), (sv["xh2"], "mn", 0), (sv["rs2"], "m1", 0), (S["ln2_g"], "row", 0)),
    )
    g_up = mm(dHb, sv["X2b"], "tn", name=f"up_dw_{tag}", **wide)
    emit("up", g_up)
    dz1, dz1b, dg1, db1, dq_b, dkv = _xattn_bwd(d, dz2, dz2b, sv["q_b"], sv["kv"], G["xq"], G["xo"], sv["xh1"], sv["rs1"], S["ln1_g"],
                                               name=f"xattn_bwd_{tag}", sched=sched)
    g_xo = mm(dz2b, sv["o_b"], "tn", name=f"xo_dw_{tag}")
    emit("xo", g_xo)
    g_xq = mm(sv["X1b"], dq_b, "tn", name=f"xq_dw_{tag}")
    emit("xq", g_xq)
    dkv_b = dkv.transpose(1, 0, 2).reshape(d.MEM, 2 * d.XW).astype(BF16)
    g_xkv = mm(memb, dkv_b, "tn", name=f"xkv_dw_{tag}")
    emit("xkv", g_xkv)
    dya_b, dyb_b, dP, dgb_b, dbga, dbgb = mm(
        dz1b, G["o"], "nt", name=f"oproj_dx_{tag}", tm=1024, tn=512,
        out_defs=(("mn", BF16), ("mn", BF16), ("cols", BF16, d.IN, off_ga), ("mn", BF16), ("acc", F32), ("acc", F32)),
        epilogue=_ep_gate_bwd,
        extras=((sv["P"], "mn", off_ga), (sv["P"], "mn", off_gb), (sv["ya"], "mn", 0), (sv["yb"], "mn", 0),
                (S["b_gate"], "row", 0), (S["b_gate"], "row", D)),
    )
    g_o = mm(sv["merged"], dz1b, "tn", name=f"oproj_dw_{tag}", **wide)
    emit("o", g_o)
    dP = _place_cols(dP, dgb_b, off_gb, name=f"place_dgb_{tag}")
    dP, dwm, dbsT, dlvg, dlvb = _gmlp_bwd(d, sv["P"], dya_b, G["a"], S["ln_v_g"], S["ln_v_b"], S["w_s"], S["b_sT"], dP, name=f"gmlp_bwd_{tag}", sched=sched)
    g_a = mm(dya_b, sv["sgu"], "tn", name=f"bra_dw_{tag}")
    emit("a", g_a)
    dP, dkvr_b, dsk = _swa_bwd(d, sv["P"], dyb_b, G["b"], cos2, sin2, S["sinks"], dP, name=f"swa_bwd_{tag}", sched=sched)
    g_b = mm(dyb_b, sv["att"], "tn", name=f"brb_dw_{tag}")
    emit("b", g_b)
    dP = _place_cols(dP, dkvr_b, off_k, name=f"place_dkv_{tag}")
    small = dict(b_gate=jnp.concatenate([dbga, dbgb], axis=1), ln_v_g=dlvg, ln_v_b=dlvb, w_s=dwm, b_sT=dbsT, sinks=dsk,
                 ln1_g=dg1, ln1_b=db1, ln2_g=dg2, ln2_b=db2, ln3_g=dg3, ln3_b=db3)
    emit("small", small)
    g_in = mm(dP, sv["Xb"], "tn", name=f"proj_dw_{tag}", **wide)
    emit("in", g_in)
    dXin = mm(dP, G["in"], "nn", name=f"proj_dx_{tag}", tn=D, tk=_tile(d.IN, 1920), epilogue=_ep_resid, extras=((dz1, "mn", 0),))
    big = {"in": g_in, "a": g_a, "b": g_b, "o": g_o, "xq": g_xq, "xkv": g_xkv, "xo": g_xo, "up": g_up, "down": g_down}
    return dXin, big, small, (loss[0] if loss else None)


def _rope_tables(T):
    inv = 1.0 / (ROPE_THETA ** (jnp.arange(0, HEAD_DIM, 2, dtype=F32) / HEAD_DIM))
    ang = jnp.arange(T, dtype=F32)[:, None] * inv[None, :]
    cos, sin = jnp.cos(ang), jnp.sin(ang)
    reps = LANES // HEAD_DIM
    return jnp.concatenate([cos, cos] * reps, axis=1), jnp.concatenate([-sin, sin] * reps, axis=1)


def _local_step(d, x, mem, target, Gs, Ss, sched=None, on_grad=None):
    cos2, sin2 = _rope_tables(d.T)
    memb = mem.astype(BF16)
    X, Xb = x, x.astype(BF16)
    saved = []
    for l in range(DEPTH):
        X, Xb, sv = _layer_fwd(d, X, Xb, memb, Gs[l], Ss[l], cos2, sin2, f"l{l}", sched)
        saved.append(sv)
    bigs, smalls = [None] * DEPTH, [None] * DEPTH
    dX, loss = X, None
    for l in reversed(range(DEPTH)):
        emit = None if on_grad is None else functools.partial(on_grad, l)
        head = target if l == DEPTH - 1 else None
        dX, bigs[l], smalls[l], got = _layer_bwd(d, dX, saved[l], memb, Gs[l], Ss[l], cos2, sin2, f"l{l}", sched, emit, head)
        loss = got if got is not None else loss
    return loss, dX, bigs, smalls


def _place():
    x, y, c = lax.axis_index("x"), lax.axis_index("y"), lax.axis_index("c")
    return x, y, c, [(1 - x, y), (x, 1 - y), (1 - x, 1 - y)]


def _all_gather(shards, *, name, sched=None):
    n = len(shards)
    any_spec = pl.BlockSpec(memory_space=pl.ANY)

    def body(*refs):
        ins, outs = refs[:n], refs[n : 2 * n]
        send_sems, recv_sems, local_sems = refs[2 * n :]
        x, y, c, chips = _place()
        me, sibling = (x, y, c), (x, y, 1 - c)

        def rows(w, p):
            r = ins[w].shape[0]
            return outs[w].at[pl.ds((4 * p[0] + 2 * p[1] + p[2]) * r, r), :]

        def copy(w, k, block, to, src=None):
            return pltpu.make_async_remote_copy(
                src_ref=rows(w, block) if src is None else src, dst_ref=rows(w, block),
                send_sem=send_sems.at[7 * w + k], recv_sem=recv_sems.at[7 * w + k], device_id=to, device_id_type=MESH)

        mine = [pltpu.make_async_copy(ins[w], rows(w, me), local_sems.at[w]) for w in range(n)]
        for cp in mine:
            cp.start()
        first = []
        for w in range(n):
            first.append(copy(w, 0, me, sibling, src=ins[w]))
            first += [copy(w, 1 + j, me, (*chip, c), src=ins[w]) for j, chip in enumerate(chips)]
        for cp in first:
            cp.start()
        passed = []
        for j, chip in enumerate(chips):
            for w in range(n):
                copy(w, 1 + j, (*chip, c), me).wait_recv()
                fwd = copy(w, 4 + j, (*chip, c), sibling)
                fwd.start()
                passed.append(fwd)
        for w in range(n):
            copy(w, 0, sibling, me).wait_recv()
            for j, chip in enumerate(chips):
                copy(w, 4 + j, (*chip, 1 - c), me).wait_recv()
        for cp in first + passed:
            cp.wait_send()
        for cp in mine:
            cp.wait()

    return _hosted(
        body,
        name=name,
        sched=sched,
        in_specs=[any_spec] * n,
        out_specs=[any_spec] * n,
        out_shape=[jax.ShapeDtypeStruct((N_DEV * s.shape[0], s.shape[1]), s.dtype) for s in shards],
        scratch_shapes=[pltpu.SemaphoreType.DMA((7 * n,)), pltpu.SemaphoreType.DMA((7 * n,)), pltpu.SemaphoreType.DMA((n,))],
    )(*shards)


class _Task:
    def __init__(self, ins, out_shapes, n_remote, n_local, plan, done, aliases=None):
        self.ins, self.out_shapes, self.n_remote, self.n_local = list(ins), list(out_shapes), n_remote, n_local
        self.plan, self.done, self.aliases = plan, done, dict(aliases or {})


class _Sched:
    def __init__(self):
        self.pending = {}

    def at(self, host, task):
        self.pending.setdefault(host, []).append(task)

    def take(self, host):
        return self.pending.pop(host, [])


def _in_hbm(out_shape):
    if isinstance(out_shape, (list, tuple)):
        return [_in_hbm(s) for s in out_shape]
    return pltpu.HBM(out_shape.shape, out_shape.dtype)


def _hosted(body, *, name, sched=None, tasks=(), grid=(), in_specs=None, out_specs=None, out_shape=(), scratch_shapes=(),
            compiler_params=None, input_output_aliases=None):
    tasks = list(tasks) + (sched.take(name) if sched is not None else [])
    scratch_shapes = list(scratch_shapes)
    kwargs = dict(name=name)
    core_aliases = dict(input_output_aliases or {})
    if grid:
        kwargs["grid"] = grid
    if compiler_params is not None:
        kwargs["compiler_params"] = compiler_params
    if not tasks:
        if in_specs is not None:
            kwargs.update(in_specs=in_specs, out_specs=out_specs)
        return pl.pallas_call(body, out_shape=_in_hbm(out_shape), scratch_shapes=scratch_shapes, input_output_aliases=core_aliases, **kwargs)
    assert in_specs is not None and out_specs is not None, name
    single = not isinstance(out_shape, (list, tuple))
    core_shape = [out_shape] if single else list(out_shape)
    core_specs = [out_specs] if single else list(out_specs)
    in_specs = list(in_specs)
    n_in, n_out, n_scr = len(in_specs), len(core_shape), len(scratch_shapes)
    t_ins = [a for t in tasks for a in t.ins]
    t_outs = [s for t in tasks for s in t.out_shapes]
    n_rem = sum(t.n_remote for t in tasks)
    n_loc = sum(t.n_local for t in tasks)
    aliases, pos_in, pos_out = dict(core_aliases), n_in, n_out
    for t in tasks:
        for a, b in t.aliases.items():
            aliases[pos_in + a] = pos_out + b
        pos_in += len(t.ins)
        pos_out += len(t.out_shapes)
    any_spec = pl.BlockSpec(memory_space=pl.ANY)

    def wrapped(*refs):
        core_in, t_in = refs[:n_in], refs[n_in : n_in + len(t_ins)]
        o0 = n_in + len(t_ins)
        core_out, t_out = refs[o0 : o0 + n_out], refs[o0 + n_out : o0 + n_out + len(t_outs)]
        s0 = o0 + n_out + len(t_outs)
        core_scr = refs[s0 : s0 + n_scr]
        send_sems, recv_sems, local_sems = refs[s0 + n_scr :]
        remote, local = [], []
        pi = po = pr = pq = 0
        for t in tasks:
            r, q = t.plan(t_in[pi : pi + len(t.ins)], t_out[po : po + len(t.out_shapes)], send_sems, recv_sems, local_sems, pr, pq)
            assert len(r) == t.n_remote and len(q) == t.n_local
            remote += r
            local += q
            pi, po, pr, pq = pi + len(t.ins), po + len(t.out_shapes), pr + t.n_remote, pq + t.n_local

        def start():
            for cp in local + remote:
                cp.start()

        def finish():
            for cp in remote + local:
                cp.wait()

        if grid:
            ids = [pl.program_id(a) for a in range(len(grid))]
            first = functools.reduce(jnp.logical_and, [i == 0 for i in ids])
            last = functools.reduce(jnp.logical_and, [i == g - 1 for i, g in zip(ids, grid)])
            pl.when(first)(start)
            body(*core_in, *core_out, *core_scr)
            pl.when(last)(finish)
        else:
            start()
            body(*core_in, *core_out, *core_scr)
            finish()

    call = pl.pallas_call(
        wrapped,
        in_specs=in_specs + [any_spec] * len(t_ins),
        out_specs=core_specs + [any_spec] * len(t_outs),
        out_shape=_in_hbm(core_shape + t_outs),
        scratch_shapes=scratch_shapes
        + [pltpu.SemaphoreType.DMA((n_rem,)), pltpu.SemaphoreType.DMA((n_rem,)), pltpu.SemaphoreType.DMA((max(n_loc, 1),))],
        input_output_aliases=aliases,
        **kwargs,
    )

    def run(*operands):
        outs = call(*operands, *t_ins)
        po = n_out
        for t in tasks:
            t.done(list(outs[po : po + len(t.out_shapes)]))
            po += len(t.out_shapes)
        return outs[0] if single else list(outs[:n_out])

    return run


def _comm_only(tasks, *, name):
    _hosted(lambda: None, name=name, tasks=tasks, in_specs=[], out_shape=[], out_specs=[])()


def _rows(ref, block, n_blocks):
    r = ref.shape[0] // n_blocks
    return ref.at[pl.ds(block * r, r), :]


def _remote(src, dst, send_sems, recv_sems, k, to):
    return pltpu.make_async_remote_copy(src_ref=src, dst_ref=dst, send_sem=send_sems.at[k], recv_sem=recv_sems.at[k],
                                        device_id=to, device_id_type=MESH)


def _gather_stage1(shards, done, part=(0, 1, 1), into=None):
    n = len(shards)
    k0, k1, n_parts = part

    def plan(ins, outs, send_sems, recv_sems, local_sems, pr, pq):
        x, y, c, chips = _place()
        mine = 4 * x + 2 * y + c
        remote, local = [], []
        for w in range(n):
            r = shards[w].shape[0]
            rp = r // n_parts
            src = ins[w].at[pl.ds(k0 * rp, (k1 - k0) * rp), :]
            dst = outs[w].at[pl.ds(mine * r + k0 * rp, (k1 - k0) * rp), :]
            local.append(pltpu.make_async_copy(src, dst, local_sems.at[pq + w]))
            for j, to in enumerate([(x, y, 1 - c)] + [(*chip, c) for chip in chips]):
                remote.append(_remote(src, dst, send_sems, recv_sems, pr + 4 * w + j, to))
        return remote, local

    shapes = [jax.ShapeDtypeStruct((N_DEV * s.shape[0], s.shape[1]), s.dtype) for s in shards]
    if into is None:
        return _Task(shards, shapes, 4 * n, n, plan, done)
    return _Task(list(shards) + list(into), shapes, 4 * n, n, plan, done, aliases={n + w: w for w in range(n)})


def _gather_stage2(partial, done):
    n = len(partial)

    def plan(ins, outs, send_sems, recv_sems, local_sems, pr, pq):
        x, y, c, chips = _place()
        remote = []
        for w in range(n):
            for j, chip in enumerate(chips):
                rows = _rows(outs[w], 4 * chip[0] + 2 * chip[1] + c, N_DEV)
                remote.append(_remote(rows, rows, send_sems, recv_sems, pr + 3 * w + j, (x, y, 1 - c)))
        return remote, []

    shapes = [jax.ShapeDtypeStruct(p.shape, p.dtype) for p in partial]
    return _Task(partial, shapes, 3 * n, 0, plan, done, aliases={w: w for w in range(n)})


def _sibling_stage(grads, done):
    n = len(grads)

    def plan(ins, outs, send_sems, recv_sems, local_sems, pr, pq):
        x, y, c, _ = _place()
        remote = []
        for w in range(n):
            for k in range(4):
                src = _rows(ins[w], 4 * (k // 2) + 2 * (k % 2) + (1 - c), N_DEV)
                remote.append(_remote(src, _rows(outs[w], k, 4), send_sems, recv_sems, pr + 4 * w + k, (x, y, 1 - c)))
        return remote, []

    shapes = [jax.ShapeDtypeStruct((g.shape[0] // 2, g.shape[1]), g.dtype) for g in grads]
    return _Task(grads, shapes, 4 * n, 0, plan, done)


def _chips_stage(sends, done, part=(0, 1), into=None):
    n = len(sends)
    k, n_parts = part

    def plan(ins, outs, send_sems, recv_sems, local_sems, pr, pq):
        x, y, c, chips = _place()
        remote = []
        for w in range(n):
            r = sends[w].shape[0] // 3
            rp = r // n_parts
            for j, chip in enumerate(chips):
                rows = pl.ds(j * r + k * rp, rp)
                remote.append(_remote(ins[w].at[rows, :], outs[w].at[rows, :], send_sems, recv_sems, pr + 3 * w + j, (*chip, c)))
        return remote, []

    shapes = [jax.ShapeDtypeStruct(s.shape, s.dtype) for s in sends]
    if into is None:
        return _Task(sends, shapes, 3 * n, 0, plan, done)
    return _Task(list(sends) + list(into), shapes, 3 * n, 0, plan, done, aliases={n + w: w for w in range(n)})


def _pair_sum(grad, got, place, *, name):
    R, C = grad.shape
    r = R // N_DEV
    tr = _tile(r, 256, 16)
    nt = r // tr

    def chip_of(s, pr):
        fx = jnp.where((s == 1) | (s == 3), 1, 0)
        fy = jnp.where(s >= 2, 1, 0)
        return pr[0] ^ fx, pr[1] ^ fy

    def grad_map(s, t, pr):
        kx, ky = chip_of(s, pr)
        return ((4 * kx + 2 * ky + pr[2]) * nt + t, 0)

    def got_map(s, t, pr):
        kx, ky = chip_of(s, pr)
        return ((2 * kx + ky) * nt + t, 0)

    def body(pr, g_ref, r_ref, own_ref, send_ref):
        s = pl.program_id(0)
        tot = g_ref[...] + r_ref[...]

        @pl.when(s == 0)
        def _():
            own_ref[...] = tot

        @pl.when(s > 0)
        def _():
            send_ref[...] = tot.astype(BF16)

    return pl.pallas_call(
        body,
        name=name,
        grid_spec=pltpu.PrefetchScalarGridSpec(
            num_scalar_prefetch=1,
            grid=(4, nt),
            in_specs=[pl.BlockSpec((tr, C), grad_map), pl.BlockSpec((tr, C), got_map)],
            out_specs=[
                pl.BlockSpec((tr, C), lambda s, t, pr: (jnp.where(s == 0, t, nt - 1), 0)),
                pl.BlockSpec((tr, C), lambda s, t, pr: (jnp.where(s == 0, 0, (s - 1) * nt + t), 0)),
            ],
        ),
        out_shape=[jax.ShapeDtypeStruct((r, C), F32), jax.ShapeDtypeStruct((3 * r, C), BF16)],
        compiler_params=_params(("arbitrary", "arbitrary"), [((tr, C), F32)] * 4),
    )(place, grad, got)


def _adam_vals(w, g, m, v):
    m = ADAM_B1 * m + (1.0 - ADAM_B1) * g
    v = ADAM_B2 * v + (1.0 - ADAM_B2) * (g * g)
    m_hat = m / (1.0 - ADAM_B1**ADAM_STEP)
    v_hat = v / (1.0 - ADAM_B2**ADAM_STEP)
    delta = -ADAM_LR * (m_hat / (jnp.sqrt(v_hat) + ADAM_EPS) + ADAM_WD * w)
    return delta, m, v


def _adam_big(w3, m3, v3, own, got, layer, transposed, prev, *, name, sched=None):
    _, A, B = w3.shape
    ta = _tile(A, 256, 16)
    nt = A // ta
    b_pad = (-B) % LANES

    def body(*refs):
        w_ref, m_ref, v_ref, own_ref, g1_ref, g2_ref, g3_ref = refs[:7]
        g_ref, d_ref, nm_ref, nv_ref = refs[-4:]
        g = ((own_ref[...] + g1_ref[...].astype(F32)) + g2_ref[...].astype(F32)) + g3_ref[...].astype(F32)
        if transposed:
            if b_pad:
                g = jnp.concatenate([g, jnp.zeros((b_pad, ta), F32)], axis=0)
            g = g.T[:, :B]
        g_ref[...] = g
        d_ref[...], nm_ref[...], nv_ref[...] = _adam_vals(w_ref[...], g, m_ref[...], v_ref[...])

    nat = pl.BlockSpec((None, ta, B), lambda t: (layer, t, 0))
    if transposed:
        parts = [pl.BlockSpec((B, ta), lambda t: (0, t))] + [pl.BlockSpec((B, ta), lambda t, j=j: (j, t)) for j in range(3)]
    else:
        parts = [pl.BlockSpec((ta, B), lambda t: (t, 0))] + [pl.BlockSpec((ta, B), lambda t, j=j: (j * nt + t, 0)) for j in range(3)]
    keep = [] if prev is None else list(prev)
    outs = _hosted(
        body,
        name=name,
        sched=sched,
        grid=(nt,),
        in_specs=[nat, nat, nat] + parts + [pl.BlockSpec(memory_space=pl.ANY)] * len(keep),
        out_specs=[nat] * 4,
        out_shape=[jax.ShapeDtypeStruct(w3.shape, F32)] * 4,
        input_output_aliases={7 + k: k for k in range(len(keep))},
        compiler_params=_params(("arbitrary",), [((ta, B), F32)] * 10),
    )(w3, m3, v3, own, got, got, got, *keep)
    return list(outs)


def _adam_small(w, m, v, parts, *, name, sched=None):
    R = w.shape[0]

    def body(w_ref, m_ref, v_ref, p_ref, g_ref, d_ref, nm_ref, nv_ref):
        g = p_ref[pl.ds(0, R), :]
        for k in range(1, N_DEV):
            g = g + p_ref[pl.ds(k * R, R), :]
        g_ref[...] = g
        d_ref[...], nm_ref[...], nv_ref[...] = _adam_vals(w_ref[...], g, m_ref[...], v_ref[...])

    return _hosted(
        body,
        name=name,
        sched=sched,
        out_shape=[jax.ShapeDtypeStruct((R, LANES), F32)] * 4,
        compiler_params=_params(None, [((R, LANES), F32)] * 16),
    )(w, m, v, parts)


SMALL_NAMES = ("b_gate", "ln_v_g", "ln_v_b", "w_s", "b_s", "sinks", "ln1_g", "ln1_b", "ln2_g", "ln2_b", "ln3_g", "ln3_b")
PACK_ALIGN = 8 * LANES


def _pack(arrays):
    flat = []
    for a in arrays:
        a = a.reshape(-1)
        flat.append(jnp.pad(a, (0, (-a.shape[0]) % PACK_ALIGN)))
    return jnp.concatenate(flat).reshape(-1, LANES)


def _unpack(packed, shapes):
    flat = packed.reshape(-1)
    out, pos = [], 0
    for s in shapes:
        n = 1
        for e in s:
            n *= e
        out.append(flat[pos : pos + n].reshape(s))
        pos += n + (-n) % PACK_ALIGN
    return out


BIG = (("in", "w_in", True), ("a", "w_br_a", True), ("b", "w_br_b", True), ("o", "w_o", False), ("xq", "w_xq", False),
       ("xkv", "w_xkv", False), ("xo", "w_xo", True), ("up", "w_up", True), ("down", "w_down", False))


SMALL_BIG = ("a", "b", "o", "xq", "xkv", "xo")
GATHER_PLAN = (
    (0, ("in",), None, None),
    (0, SMALL_BIG, ("proj_l0",), "gmlp_fwd_l0"),
    (0, ("up",), ("swa_fwd_l0", "merge_l0"), "oproj_ln_l0"),
    (0, ("down",), ("oproj_ln_l0", "xattn_fwd_l0"), "up_l0"),
    (1, ("in",), ("up_l0",), "down_ln_l0"),
    (1, SMALL_BIG, ("down_ln_l0",), "proj_l1"),
    (1, ("up",), ("down_ln_l0", ("proj_l1", 3)), "gmlp_fwd_l1"),
    (1, ("down",), (("swa_fwd_l1", 2), "merge_l1", "oproj_ln_l1"), "xattn_fwd_l1"),
)
EARLY_SMALL_BIG = ("o", "xq", "xkv", "xo")
LATE_SMALL_BIG = ("a", "b")
GRAD_HOSTS = {
    1: {"down": ("up_dx_l1", "swa_bwd_l1"), "up": ("xattn_bwd_l1", ("gmlp_bwd_l1", "proj_dw_l1")),
        **{g: ("gmlp_bwd_l1", "proj_dx_l1") for g in EARLY_SMALL_BIG}, **{g: ("proj_dw_l1", "proj_dx_l1") for g in LATE_SMALL_BIG},
        "in": ("proj_dx_l1", ("down_dx_l0", "down_dw_l0"))},
    0: {"down": ("up_dx_l0", ("oproj_dx_l0", "gmlp_bwd_l0")), "up": ("xattn_bwd_l0", "swa_bwd_l0"),
        **{g: ("gmlp_bwd_l0", "proj_dw_l0") for g in EARLY_SMALL_BIG}, **{g: ("proj_dw_l0", "proj_dx_l0") for g in LATE_SMALL_BIG},
        "in": (None, "proj_dx_l0")},
}
SMALL_GRAD_HOSTS = ("proj_dw_l0", "proj_dx_l0")


def kernel(x, mem, w_in, b_gate, ln_v_g, ln_v_b, w_s, b_s, sinks, w_br_a, w_br_b, w_o, ln1_g, ln1_b, w_xq, w_xkv, w_xo, ln2_g, ln2_b, w_up, w_down, ln3_g, ln3_b, loss_target, m_w_in, m_b_gate, m_ln_v_g, m_ln_v_b, m_w_s, m_b_s, m_sinks, m_w_br_a, m_w_br_b, m_w_o, m_ln1_g, m_ln1_b, m_w_xq, m_w_xkv, m_w_xo, m_ln2_g, m_ln2_b, m_w_up, m_w_down, m_ln3_g, m_ln3_b, v_w_in, v_b_gate, v_ln_v_g, v_ln_v_b, v_w_s, v_b_s, v_sinks, v_w_br_a, v_w_br_b, v_w_o, v_ln1_g, v_ln1_b, v_w_xq, v_w_xkv, v_w_xo, v_ln2_g, v_ln2_b, v_w_up, v_w_down, v_ln3_g, v_ln3_b):
    given = dict(locals())
    T, D = x.shape[1], x.shape[2]
    IN, GW, AW, XW, DFF = w_in.shape[2] * N_DEV, ln_v_g.shape[1], w_br_b.shape[1], w_xq.shape[2], w_up.shape[2] * N_DEV
    KVW = (IN - 2 * GW - AW - 2 * D) // 2
    d = Dims(T=T, D=D, IN=IN, GW=GW, G=GW // GROUP_DIM, AW=AW, KVW=KVW, HQ=AW // HEAD_DIM, HKV=KVW // HEAD_DIM, XW=XW,
             XH=XW // X_HEAD_DIM, MEM=mem.shape[1], DFF=DFF)
    place = jnp.stack([lax.axis_index("x"), lax.axis_index("y"), lax.axis_index("c")]).astype(jnp.int32)

    sched = _Sched()
    big_of = {g: (p, tr) for g, p, tr in BIG}

    def shard(l, g):
        p, tr = big_of[g]
        return (given[p][l].T if tr else given[p][l]).astype(BF16)

    Gs = [{}, {}]

    def plan_gather(l, names, hosts1, host2):
        shards = [shard(l, g) for g in names]
        if hosts1 is None:
            Gs[l].update(zip(names, _all_gather(shards, name=f"gather_{names[0]}_l{l}")))
            return

        spans = [(h, 1) if isinstance(h, str) else h for h in hosts1]
        total = sum(cnt for _, cnt in spans)

        def register(i, k0, into):
            host, cnt = spans[i]

            def done(outs):
                if i + 1 < len(spans):
                    register(i + 1, k0 + cnt, outs)
                else:
                    sched.at(host2, _gather_stage2(outs, lambda full: Gs[l].update(zip(names, full))))

            sched.at(host, _gather_stage1(shards, done, part=(k0, k0 + cnt, total), into=into))

        register(0, 0, None)

    for entry in GATHER_PLAN:
        plan_gather(*entry)
    Ss = []
    for l in range(DEPTH):
        S = {n: given[n][l].reshape(1, -1) for n in SMALL_NAMES if n not in ("w_s", "b_s", "sinks")}
        S["w_s"], S["b_sT"], S["sinks"] = w_s[l], b_s[l].T, sinks[l]
        Ss.append(S)

    owns, recvs = {}, {}

    smalls_seen = {}
    gathered_small = []

    def pack_small(src):
        parts = []
        for l in range(DEPTH):
            for n in SMALL_NAMES:
                a = src[l]["b_sT"].T if n == "b_s" else src[l][n]
                parts.append(a[0, : d.HQ] if n == "sinks" else a)
        return _pack(parts)

    def on_small(l, small):
        smalls_seen[l] = small
        if len(smalls_seen) == DEPTH:
            host1, host2 = SMALL_GRAD_HOSTS
            sched.at(host1, _gather_stage1([pack_small(smalls_seen)], lambda part: sched.at(
                host2, _gather_stage2(part, lambda full: gathered_small.append(full[0])))))

    def on_grad(l, g, grad):
        if g == "small":
            return on_small(l, grad)
        sib_host, chips_host = GRAD_HOSTS[l][g]

        def after_sibling(got):
            owns[(l, g)], send = _pair_sum(grad, got[0], place, name=f"grad_pair_sum_{g}_l{l}")
            hosts = chips_host if isinstance(chips_host, tuple) else (chips_host,)

            def register(k, into):
                def done(r):
                    if k + 1 < len(hosts):
                        register(k + 1, r)
                    else:
                        recvs[(l, g)] = r[0]

                task = _chips_stage([send], done, part=(k, len(hosts)), into=into)
                if hosts[k] is None:
                    _comm_only([task], name=f"grad_chips_{g}_l{l}")
                else:
                    sched.at(hosts[k], task)

            register(0, None)

        task = _sibling_stage([grad], after_sibling)
        if sib_host is None:
            _comm_only([task], name=f"grad_sibling_{g}_l{l}")
        else:
            sched.at(sib_host, task)

    loss, dX, bigs, smalls = _local_step(d, x[0], mem[0], loss_target[0], Gs, Ss, sched, on_grad)
    loss = lax.psum(loss[0, 0], ("x", "y", "c"))

    assert not sched.pending, sorted(sched.pending)

    def viewed(p, tr):
        return tr and given[p].shape[2] % LANES != 0

    res = {p: None for _, p, _ in BIG}
    for l in reversed(range(DEPTH)):
        for g, p, tr in BIG:
            view = viewed(p, tr)
            w3, m3, v3 = ((jnp.swapaxes(a, 1, 2) if view else a) for a in (given[p], given["m_" + p], given["v_" + p]))
            res[p] = _adam_big(w3, m3, v3, owns[(l, g)], recvs[(l, g)], l, tr and not view, res[p], name=f"adamw_{g}_l{l}")
    for g, p, tr in BIG:
        if viewed(p, tr):
            res[p] = [jnp.swapaxes(a, 1, 2) for a in res[p]]

    shapes = [given[n].shape[1:] for n in SMALL_NAMES]
    pw, pm, pv = (_pack([given[pre + n][l] for l in range(DEPTH) for n in SMALL_NAMES]) for pre in ("", "m_", "v_"))
    small_out = [_unpack(o, shapes * DEPTH) for o in _adam_small(pw, pm, pv, gathered_small[0], name="adamw_small")]

    names = ["w_in", "b_gate", "ln_v_g", "ln_v_b", "w_s", "b_s", "sinks", "w_br_a", "w_br_b", "w_o", "ln1_g", "ln1_b", "w_xq",
             "w_xkv", "w_xo", "ln2_g", "ln2_b", "w_up", "w_down", "ln3_g", "ln3_b"]
    out = [loss, dX[None]]
    for kind in range(4):
        for n in names:
            if n in SMALL_NAMES:
                i = SMALL_NAMES.index(n)
                out.append(jnp.stack([small_out[kind][l * len(SMALL_NAMES) + i] for l in range(DEPTH)]))
            else:
                out.append(res[n][kind])
    return tuple(out)
```

```python
import collections
import dataclasses
import functools

import jax
import jax.numpy as jnp
from jax import lax
from jax.experimental import pallas as pl
from jax.experimental.pallas import tpu as pltpu

F32 = jnp.float32
BF16 = jnp.bfloat16
MESH = pl.DeviceIdType.MESH

N_DEV = 8
DEPTH = 2
CHUNK = 128
HEAD_DIM = 64
ROPE_HALF = HEAD_DIM // 2
X_HEAD_DIM = 128
GROUP_DIM = 128
LANES = 128
ROPE_THETA = 10000.0
LN_EPS = 1e-5
ALPHA = (2 * DEPTH) ** 0.25
ADAM_LR = 0.001
ADAM_B1 = 0.9
ADAM_B2 = 0.999
ADAM_EPS = 1e-08
ADAM_WD = 0.01
ADAM_STEP = 10
VMEM_BYTES = 64 * 1024 * 1024
VMEM_TEMP_BYTES = 20 * 1024 * 1024

Dims = collections.namedtuple("Dims", "T D IN GW G AW KVW HQ HKV XW XH MEM DFF")


def _offsets(d):
    off_v = d.GW
    off_q = 2 * d.GW
    off_k = off_q + d.AW
    off_va = off_k + d.KVW
    off_ga = off_va + d.KVW
    off_gb = off_ga + d.D
    return off_v, off_q, off_k, off_va, off_ga, off_gb


def _tile(dim, pref, align=LANES):
    if dim <= pref:
        return dim
    t = pref - pref % align
    while t >= align:
        if dim % t == 0:
            return t
        t -= align
    return dim


def _nbytes(shape, dtype):
    n = 1
    for s in shape:
        n *= s
    return n * jnp.dtype(dtype).itemsize


def _params(sem, blocks, scratch=0):
    need = 2 * sum(_nbytes(s, t) for s, t in blocks) + scratch + VMEM_TEMP_BYTES
    limit = min(max(need, 32 * 1024 * 1024), VMEM_BYTES - 4 * 1024 * 1024)
    return pltpu.CompilerParams(dimension_semantics=sem, vmem_limit_bytes=limit)


def _dot(a, b, kind):
    dn = {"nn": (((1,), (0,)), ((), ())), "nt": (((1,), (1,)), ((), ())), "tn": (((0,), (0,)), ((), ()))}[kind]
    return lax.dot_general(a, b, dn, preferred_element_type=F32)


def _gelu(x):
    c = 0.7978845608028654
    t = jnp.tanh(c * (x + 0.044715 * (x * x * x)))
    return 0.5 * x * (1.0 + t)


def _gelu_grad(x):
    c = 0.7978845608028654
    x2 = x * x
    t = jnp.tanh(c * (x + 0.044715 * (x2 * x)))
    return 0.5 * (1.0 + t) + 0.5 * x * (1.0 - t * t) * (c * (1.0 + 3.0 * 0.044715 * x2))


def _ln_fwd(z, g, b):
    mu = jnp.mean(z, axis=-1, keepdims=True)
    zc = z - mu
    var = jnp.mean(zc * zc, axis=-1, keepdims=True)
    rstd = lax.rsqrt(var + LN_EPS)
    xhat = zc * rstd
    return xhat * g + b, xhat, rstd


def _ln_bwd_vals(dy, xhat, rstd, g):
    gd = dy * g
    m1 = jnp.mean(gd, axis=-1, keepdims=True)
    m2 = jnp.mean(gd * xhat, axis=-1, keepdims=True)
    return rstd * (gd - m1 - xhat * m2)


def _acc_store(ref, val, first):
    @pl.when(first)
    def _():
        ref[...] = val

    @pl.when(jnp.logical_not(first))
    def _():
        ref[...] += val


def _matmul(a, b, kind, *, name, tm=512, tn=1024, tk=2048, out_defs=(("mn", F32),), epilogue=None, extras=(), sched=None):
    if kind == "nn":
        (M, K), (K2, N) = a.shape, b.shape
    elif kind == "nt":
        (M, K), (N, K2) = a.shape, b.shape
    else:
        (K, M), (K2, N) = a.shape, b.shape
    assert K == K2, (a.shape, b.shape, kind)
    tm, tn, tk = _tile(M, tm), _tile(N, tn), _tile(K, tk)
    nk = K // tk
    blocks = []

    def spec(shape, imap, dtype):
        blocks.append((shape, dtype))
        return pl.BlockSpec(shape, imap)

    if kind == "tn":
        a_spec = spec((tk, tm), lambda j, i, k: (k, i), a.dtype)
    else:
        a_spec = spec((tm, tk), lambda j, i, k: (i, k), a.dtype)
    if kind == "nt":
        b_spec = spec((tn, tk), lambda j, i, k: (j, k), b.dtype)
    else:
        b_spec = spec((tk, tn), lambda j, i, k: (k, j), b.dtype)
    in_specs = [a_spec, b_spec]
    operands = [a, b]
    for arr, ekind, off in extras:
        if ekind == "mn":
            assert off % tn == 0
            in_specs.append(spec((tm, tn), lambda j, i, k, o=off // tn: (i, j + o), arr.dtype))
        elif ekind == "row":
            assert off % tn == 0
            in_specs.append(spec((1, tn), lambda j, i, k, o=off // tn: (0, j + o), arr.dtype))
        else:
            in_specs.append(spec((tm, 1), lambda j, i, k: (i, 0), arr.dtype))
        operands.append(arr)
    out_shape, out_specs = [], []
    for od in out_defs:
        okind, dt = od[0], od[1]
        if okind == "mn":
            out_shape.append(jax.ShapeDtypeStruct((M, N), dt))
            out_specs.append(spec((tm, tn), lambda j, i, k: (i, j), dt))
        elif okind == "cols":
            assert od[3] % tn == 0
            out_shape.append(jax.ShapeDtypeStruct((M, od[2]), dt))
            out_specs.append(spec((tm, tn), lambda j, i, k, o=od[3] // tn: (i, j + o), dt))
        elif okind == "m1":
            assert N == tn
            out_shape.append(jax.ShapeDtypeStruct((M, 1), dt))
            out_specs.append(spec((tm, 1), lambda j, i, k: (i, 0), dt))
        else:
            out_shape.append(jax.ShapeDtypeStruct((1, N), F32))
            out_specs.append(spec((1, tn), lambda j, i, k: (0, j), F32))
    n_ex, n_out = len(extras), len(out_defs)
    ep = epilogue if epilogue is not None else (lambda acc: (acc,))
    in_place = nk > 1 and epilogue is None and tuple(out_defs) == (("mn", F32),)

    def body(*refs):
        a_ref, b_ref = refs[0], refs[1]
        ex_refs = refs[2 : 2 + n_ex]
        out_refs = refs[2 + n_ex : 2 + n_ex + n_out]
        i = pl.program_id(1)
        k = pl.program_id(2)

        def product():
            return _dot(a_ref[...], b_ref[...], kind)

        def finish(acc):
            vals = ep(acc, *[r[...] for r in ex_refs])
            for od, r, v in zip(out_defs, out_refs, vals):
                if od[0] == "acc":
                    _acc_store(r, v, i == 0)
                else:
                    r[...] = v.astype(od[1])

        if nk == 1:
            finish(product())
            return
        acc_ref = out_refs[0] if in_place else refs[-1]

        @pl.when(k == 0)
        def _():
            acc_ref[...] = product()

        if in_place:
            @pl.when(k > 0)
            def _():
                acc_ref[...] += product()
        else:
            if nk > 2:
                @pl.when(jnp.logical_and(k > 0, k < nk - 1))
                def _():
                    acc_ref[...] += product()

            @pl.when(k == nk - 1)
            def _():
                finish(acc_ref[...] + product())

    scratch = [pltpu.VMEM((tm, tn), F32)] if nk > 1 and not in_place else []
    outs = _hosted(
        body,
        name=name,
        sched=sched,
        grid=(N // tn, M // tm, nk),
        in_specs=in_specs,
        out_specs=out_specs,
        out_shape=out_shape,
        scratch_shapes=scratch,
        compiler_params=_params(("arbitrary", "arbitrary", "arbitrary"), blocks, _nbytes((tm, tn), F32) if scratch else 0),
    )(*operands)
    return outs if len(outs) > 1 else outs[0]


def _ep_resid_ln(acc, resid, g, b):
    y, xhat, rstd = _ln_fwd(ALPHA * resid + acc, g, b)
    return y, y, xhat, rstd


def _ep_resid(acc, resid):
    return (ALPHA * resid + acc,)


def _ep_resid_ln_bwd(acc, resid, xhat, rstd, g):
    dy = ALPHA * resid + acc
    dz = _ln_bwd_vals(dy, xhat, rstd, g)
    return dz, dz, jnp.sum(dy * xhat, axis=0, keepdims=True), jnp.sum(dy, axis=0, keepdims=True)


def _ep_relu2(acc):
    r = jnp.maximum(acc, 0.0)
    return acc, r * r


def _ep_relu2_bwd(acc, h):
    return (acc * (2.0 * jnp.maximum(h, 0.0)),)


def _ep_gate_bwd(acc, ga, gb, ya, yb, bga, bgb):
    sa = jax.nn.sigmoid(ga + bga)
    sb = jax.nn.sigmoid(gb + bgb)
    dga = acc * ya * (sa * (1.0 - sa))
    dgb = acc * yb * (sb * (1.0 - sb))
    return (acc * sa, acc * sb, dga, dgb, jnp.sum(dga, axis=0, keepdims=True), jnp.sum(dgb, axis=0, keepdims=True))


def _ln_bwd(dy, xhat, rstd, g, *, name, sched=None, target=None):
    T, D = dy.shape
    tm = _tile(T, 256)
    head = target is not None

    def body(*refs):
        dy_ref, xh_ref, rs_ref, g_ref = refs[:4]
        dz_ref, dzb_ref, dg_ref, db_ref = refs[4 + head : 8 + head]
        first = pl.program_id(0) == 0
        dyv, xh = dy_ref[...], xh_ref[...]
        if head:
            err = dyv - refs[4][...]
            dyv = err * (1.0 / D)
            part = 0.5 * jnp.sum(jnp.sum(err * err, axis=1, keepdims=True) * (1.0 / D), axis=0, keepdims=True)
            _acc_store(refs[-1], part, first)
        dz = _ln_bwd_vals(dyv, xh, rs_ref[...], g_ref[...])
        dz_ref[...] = dz
        dzb_ref[...] = dz.astype(BF16)
        _acc_store(dg_ref, jnp.sum(dyv * xh, axis=0, keepdims=True), first)
        _acc_store(db_ref, jnp.sum(dyv, axis=0, keepdims=True), first)

    row = pl.BlockSpec((tm, D), lambda i: (i, 0))
    vec = pl.BlockSpec((1, D), lambda i: (0, 0))
    one = pl.BlockSpec((1, 1), lambda i: (0, 0))
    return _hosted(
        body,
        name=name,
        sched=sched,
        grid=(T // tm,),
        in_specs=[row, row, pl.BlockSpec((tm, 1), lambda i: (i, 0)), vec] + [row] * head,
        out_specs=[row, row, vec, vec] + [one] * head,
        out_shape=[
            jax.ShapeDtypeStruct((T, D), F32),
            jax.ShapeDtypeStruct((T, D), BF16),
            jax.ShapeDtypeStruct((1, D), F32),
            jax.ShapeDtypeStruct((1, D), F32),
        ] + [jax.ShapeDtypeStruct((1, 1), F32)] * head,
        compiler_params=_params(("arbitrary",), [((tm, D), F32)] * (4 + head)),
    )(*([dy, xhat, rstd, g] + [target] * head))


def _masked_mix_weights(wm_ref, G):
    r = lax.broadcasted_iota(jnp.int32, (CHUNK, CHUNK), 0)
    c = lax.broadcasted_iota(jnp.int32, (CHUNK, CHUNK), 1)
    return [jnp.where(c <= r, wm_ref[g], 0.0).astype(BF16) for g in range(G)]


def _gmlp_fwd(d, P, lnv_g, lnv_b, wm, bsT, *, name, sched=None):
    T, GW, G = d.T, d.GW, d.G
    tm = _tile(T, 256)
    nc = tm // CHUNK

    def body(u_ref, v_ref, g_ref, b_ref, wm_ref, bs_ref, o_ref):
        gu = _gelu(u_ref[...])
        vn, _, _ = _ln_fwd(_gelu(v_ref[...]), g_ref[...], b_ref[...])
        vn = vn.astype(BF16)
        wl = _masked_mix_weights(wm_ref, G)
        for c in range(nc):
            rows = slice(c * CHUNK, (c + 1) * CHUNK)
            for g in range(G):
                cols = slice(g * GROUP_DIM, (g + 1) * GROUP_DIM)
                mixed = _dot(wl[g], vn[rows, cols], "nn") + bs_ref[:, g : g + 1]
                o_ref[rows, cols] = (gu[rows, cols] * mixed).astype(BF16)

    return _hosted(
        body,
        name=name,
        sched=sched,
        grid=(T // tm,),
        in_specs=[
            pl.BlockSpec((tm, GW), lambda i: (i, 0)),
            pl.BlockSpec((tm, GW), lambda i: (i, 1)),
            pl.BlockSpec((1, GW), lambda i: (0, 0)),
            pl.BlockSpec((1, GW), lambda i: (0, 0)),
            pl.BlockSpec((G, CHUNK, CHUNK), lambda i: (0, 0, 0)),
            pl.BlockSpec((CHUNK, G), lambda i: (0, 0)),
        ],
        out_specs=pl.BlockSpec((tm, GW), lambda i: (i, 0)),
        out_shape=jax.ShapeDtypeStruct((T, GW), BF16),
        compiler_params=_params(("arbitrary",), [((tm, GW), F32)] * 3),
    )(P, P, lnv_g, lnv_b, wm, bsT)


def _gmlp_bwd(d, P, dya_b, Ga, lnv_g, lnv_b, wm, bsT, dP, *, name, sched=None):
    T, D, GW, G = d.T, d.D, d.GW, d.G
    tm = _tile(T, 256)
    nc = tm // CHUNK

    def body(u_ref, v_ref, dya_ref, ga_ref, g_ref, b_ref, wm_ref, bs_ref, dp_in, duv_ref, dwm_ref, dbs_ref, dlg_ref, dlb_ref, dgu_s, dvn_s):
        first = pl.program_id(0) == 0
        dsgu = _dot(dya_ref[...], ga_ref[...], "nn")
        u, v = u_ref[...], v_ref[...]
        gu = _gelu(u)
        lng = g_ref[...]
        vn, xh, rstd = _ln_fwd(_gelu(v), lng, b_ref[...])
        vn = vn.astype(BF16)
        wl = _masked_mix_weights(wm_ref, G)
        r = lax.broadcasted_iota(jnp.int32, (CHUNK, CHUNK), 0)
        cc = lax.broadcasted_iota(jnp.int32, (CHUNK, CHUNK), 1)
        lane_g = lax.broadcasted_iota(jnp.int32, (CHUNK, G), 1)
        dbs = jnp.zeros((CHUNK, G), F32)
        for g in range(G):
            cols = slice(g * GROUP_DIM, (g + 1) * GROUP_DIM)
            dw = jnp.zeros((CHUNK, CHUNK), F32)
            for c in range(nc):
                rows = slice(c * CHUNK, (c + 1) * CHUNK)
                mixed = _dot(wl[g], vn[rows, cols], "nn") + bs_ref[:, g : g + 1]
                ds = dsgu[rows, cols]
                dgu_s[rows, cols] = ds * mixed
                dmx = ds * gu[rows, cols]
                dbs = dbs + jnp.where(lane_g == g, jnp.sum(dmx, axis=1, keepdims=True), 0.0)
                dmx = dmx.astype(BF16)
                dw = dw + _dot(dmx, vn[rows, cols], "nt")
                dvn_s[rows, cols] = _dot(wl[g], dmx, "tn")
            _acc_store(dwm_ref.at[g], jnp.where(cc <= r, dw, 0.0), first)
        _acc_store(dbs_ref, dbs, first)
        dvn = dvn_s[...]
        _acc_store(dlg_ref, jnp.sum(dvn * xh, axis=0, keepdims=True), first)
        _acc_store(dlb_ref, jnp.sum(dvn, axis=0, keepdims=True), first)
        dgv = _ln_bwd_vals(dvn, xh, rstd, lng)
        duv_ref[:, :GW] = (dgu_s[...] * _gelu_grad(u)).astype(BF16)
        duv_ref[:, GW:] = (dgv * _gelu_grad(v)).astype(BF16)

    return _hosted(
        body,
        name=name,
        sched=sched,
        grid=(T // tm,),
        in_specs=[
            pl.BlockSpec((tm, GW), lambda i: (i, 0)),
            pl.BlockSpec((tm, GW), lambda i: (i, 1)),
            pl.BlockSpec((tm, D), lambda i: (i, 0)),
            pl.BlockSpec((D, GW), lambda i: (0, 0)),
            pl.BlockSpec((1, GW), lambda i: (0, 0)),
            pl.BlockSpec((1, GW), lambda i: (0, 0)),
            pl.BlockSpec((G, CHUNK, CHUNK), lambda i: (0, 0, 0)),
            pl.BlockSpec((CHUNK, G), lambda i: (0, 0)),
            pl.BlockSpec(memory_space=pl.ANY),
        ],
        out_specs=[
            pl.BlockSpec((tm, 2 * GW), lambda i: (i, 0)),
            pl.BlockSpec((G, CHUNK, CHUNK), lambda i: (0, 0, 0)),
            pl.BlockSpec((CHUNK, G), lambda i: (0, 0)),
            pl.BlockSpec((1, GW), lambda i: (0, 0)),
            pl.BlockSpec((1, GW), lambda i: (0, 0)),
        ],
        out_shape=[
            jax.ShapeDtypeStruct(dP.shape, BF16),
            jax.ShapeDtypeStruct((G, CHUNK, CHUNK), F32),
            jax.ShapeDtypeStruct((CHUNK, G), F32),
            jax.ShapeDtypeStruct((1, GW), F32),
            jax.ShapeDtypeStruct((1, GW), F32),
        ],
        scratch_shapes=[pltpu.VMEM((tm, GW), F32), pltpu.VMEM((tm, GW), F32)],
        input_output_aliases={8: 0},
        compiler_params=_params(
            ("arbitrary",), [((tm, GW), F32)] * 3 + [((tm, D), BF16), ((D, GW), BF16)], 2 * _nbytes((tm, GW), F32)
        ),
    )(P, P, dya_b, Ga, lnv_g, lnv_b, wm, bsT, dP)


def _swap_halves(x):
    w = x.shape[1]
    lane = lax.broadcasted_iota(jnp.int32, x.shape, 1)
    return jnp.where((lane % HEAD_DIM) < ROPE_HALF, pltpu.roll(x, w - ROPE_HALF, 1), pltpu.roll(x, ROPE_HALF, 1))


def _lane_tile(t, width):
    reps = width // LANES
    return t if reps == 1 else jnp.tile(t, (1, reps))


def _rope(x, cos2, sin2):
    w = x.shape[1]
    return x * _lane_tile(cos2, w) + _swap_halves(x) * _lane_tile(sin2, w)


def _rope_bwd(g, cos2, sin2):
    w = g.shape[1]
    return g * _lane_tile(cos2, w) - _swap_halves(g) * _lane_tile(sin2, w)


PAIR = LANES // HEAD_DIM


def _swa_valid(i):
    s = lax.broadcasted_iota(jnp.int32, (2 * CHUNK, CHUNK), 0)
    t = lax.broadcasted_iota(jnp.int32, (2 * CHUNK, CHUNK), 1)
    cur = jnp.logical_and(s >= CHUNK, s - CHUNK <= t)
    prev = jnp.logical_and(jnp.logical_and(s < CHUNK, s > t), i > 0)
    return jnp.logical_or(cur, prev)


def _half_variants(x, odd):
    lane = lax.broadcasted_iota(jnp.int32, x.shape, 1)
    if odd:
        hi = jnp.where(lane >= HEAD_DIM, x, jnp.zeros_like(x))
        return pltpu.roll(hi, HEAD_DIM, 1), hi
    lo = jnp.where(lane < HEAD_DIM, x, jnp.zeros_like(x))
    return lo, pltpu.roll(lo, HEAD_DIM, 1)


def _swa_probs_t(kx, qp, sink, valid):
    s = jnp.where(valid, _dot(kx, qp, "nt") * (HEAD_DIM**-0.5), -jnp.inf)
    m = jnp.maximum(jnp.max(s, axis=0, keepdims=True), sink)
    e = jnp.exp(s - m)
    e_s = jnp.exp(sink - m)
    den = jnp.sum(e, axis=0, keepdims=True) + e_s
    return e / den, e_s / den


def _swa_load(q_ref, kc_ref, kp_ref, vc_ref, vp_ref, cc, sc, cp, sp):
    qr = _rope(q_ref[...], cc, sc).astype(BF16)
    kcat = jnp.concatenate([_rope(kp_ref[...], cp, sp), _rope(kc_ref[...], cc, sc)], axis=0)
    vcat = jnp.concatenate([vp_ref[...], vc_ref[...]], axis=0)
    return qr, kcat, vcat


def _swa_specs(d):
    _, off_q, off_k, off_va, _, _ = _offsets(d)
    assert off_q % d.AW == 0 and off_k % d.KVW == 0 and off_va % d.KVW == 0
    prev = lambda i: jnp.maximum(i - 1, 0)
    return [
        pl.BlockSpec(memory_space=pltpu.SMEM),
        pl.BlockSpec((CHUNK, d.AW), lambda i, o=off_q // d.AW: (i, o)),
        pl.BlockSpec((CHUNK, d.KVW), lambda i, o=off_k // d.KVW: (i, o)),
        pl.BlockSpec((CHUNK, d.KVW), lambda i, o=off_k // d.KVW: (prev(i), o)),
        pl.BlockSpec((CHUNK, d.KVW), lambda i, o=off_va // d.KVW: (i, o)),
        pl.BlockSpec((CHUNK, d.KVW), lambda i, o=off_va // d.KVW: (prev(i), o)),
        pl.BlockSpec((CHUNK, LANES), lambda i: (i, 0)),
        pl.BlockSpec((CHUNK, LANES), lambda i: (i, 0)),
        pl.BlockSpec((CHUNK, LANES), lambda i: (prev(i), 0)),
        pl.BlockSpec((CHUNK, LANES), lambda i: (prev(i), 0)),
    ]


def _swa_fwd(d, P, cos2, sin2, sinks, *, name, sched=None):
    T, AW, HQ, HKV = d.T, d.AW, d.HQ, d.HKV
    grp = HQ // HKV
    assert grp % PAIR == 0 and HKV % PAIR == 0

    def body(sink_ref, q_ref, kc_ref, kp_ref, vc_ref, vp_ref, cc_ref, sc_ref, cp_ref, sp_ref, o_ref):
        i = pl.program_id(0)
        qr, kcat, vcat = _swa_load(q_ref, kc_ref, kp_ref, vc_ref, vp_ref, cc_ref[...], sc_ref[...], cp_ref[...], sp_ref[...])
        valid = _swa_valid(i)
        for kv in range(HKV):
            col = slice((kv // PAIR) * LANES, (kv // PAIR + 1) * LANES)
            kx = [t.astype(BF16) for t in _half_variants(kcat[:, col], kv % PAIR)]
            vx = [t.astype(BF16) for t in _half_variants(vcat[:, col], kv % PAIR)]
            for pp in range(grp // PAIR):
                p = kv * (grp // PAIR) + pp
                qs = slice(p * LANES, (p + 1) * LANES)
                out = jnp.zeros((CHUNK, LANES), F32)
                for half in range(PAIR):
                    pt, _ = _swa_probs_t(kx[half], qr[:, qs], sink_ref[PAIR * p + half], valid)
                    out = out + _dot(pt.astype(BF16), vx[half], "tn")
                o_ref[:, qs] = out.astype(BF16)

    return _hosted(
        body,
        name=name,
        sched=sched,
        grid=(T // CHUNK,),
        in_specs=_swa_specs(d),
        out_specs=pl.BlockSpec((CHUNK, AW), lambda i: (i, 0)),
        out_shape=jax.ShapeDtypeStruct((T, AW), BF16),
        compiler_params=_params(("arbitrary",), [((CHUNK, AW), F32)] * 3),
    )(sinks, P, P, P, P, P, cos2, sin2, cos2, sin2)


def _swa_bwd(d, P, dyb_b, Gb, cos2, sin2, sinks, dP, *, name, sched=None):
    T, D, AW, KVW, HQ, HKV = d.T, d.D, d.AW, d.KVW, d.HQ, d.HKV
    grp = HQ // HKV
    nb = T // CHUNK
    scale = HEAD_DIM**-0.5
    off_q = _offsets(d)[1]
    assert grp % PAIR == 0 and HKV % PAIR == 0 and off_q % AW == 0

    def body(sink_ref, q_ref, kc_ref, kp_ref, vc_ref, vp_ref, cc_ref, sc_ref, cp_ref, sp_ref, dyb_ref, gb_ref, dp_in,
             dq_ref, dkv_ref, dsk_ref, acc_s, dq_s, dk_s, dv_s):
        i = pl.program_id(0)
        cc, sc, cp, sp = cc_ref[...], sc_ref[...], cp_ref[...], sp_ref[...]
        datt = _dot(dyb_ref[...], gb_ref[...], "nn").astype(BF16)
        qr, kcat, vcat = _swa_load(q_ref, kc_ref, kp_ref, vc_ref, vp_ref, cc, sc, cp, sp)
        valid = _swa_valid(i)
        lane1 = lax.broadcasted_iota(jnp.int32, (1, LANES), 1)
        lane2 = lax.broadcasted_iota(jnp.int32, (2 * CHUNK, LANES), 1)
        dsk = jnp.zeros((1, LANES), F32)
        for kvp in range(HKV // PAIR):
            col = slice(kvp * LANES, (kvp + 1) * LANES)
            dk_col = jnp.zeros((2 * CHUNK, LANES), F32)
            dv_col = jnp.zeros((2 * CHUNK, LANES), F32)
            for odd in range(PAIR):
                kv = kvp * PAIR + odd
                kx = [t.astype(BF16) for t in _half_variants(kcat[:, col], odd)]
                vx = [t.astype(BF16) for t in _half_variants(vcat[:, col], odd)]
                dk_half = [jnp.zeros((2 * CHUNK, LANES), F32) for _ in range(PAIR)]
                dv_half = [jnp.zeros((2 * CHUNK, LANES), F32) for _ in range(PAIR)]
                for pp in range(grp // PAIR):
                    p = kv * (grp // PAIR) + pp
                    qs = slice(p * LANES, (p + 1) * LANES)
                    qp, dop = qr[:, qs], datt[:, qs]
                    dq = jnp.zeros((CHUNK, LANES), F32)
                    for half in range(PAIR):
                        h = PAIR * p + half
                        pt, p_s = _swa_probs_t(kx[half], qp, sink_ref[h], valid)
                        dpt = _dot(vx[half], dop, "nt")
                        rs = jnp.sum(pt * dpt, axis=0, keepdims=True)
                        dst = (pt * (dpt - rs) * scale).astype(BF16)
                        dsk = dsk + jnp.where(lane1 == h, -jnp.sum(p_s * rs, axis=1, keepdims=True), 0.0)
                        dq = dq + _dot(dst, kx[half], "tn")
                        dk_half[half] = dk_half[half] + _dot(dst, qp, "nn")
                        dv_half[half] = dv_half[half] + _dot(pt.astype(BF16), dop, "nn")
                    dq_s[:, qs] = dq
                for acc_half, is_k in ((dk_half, True), (dv_half, False)):
                    lo = jnp.where(lane2 < HEAD_DIM, acc_half[0], 0.0)
                    hi = jnp.where(lane2 >= HEAD_DIM, acc_half[1], 0.0)
                    both = (pltpu.roll(lo, HEAD_DIM, 1) + hi) if odd else (lo + pltpu.roll(hi, HEAD_DIM, 1))
                    if is_k:
                        dk_col = dk_col + both
                    else:
                        dv_col = dv_col + both
            dk_s[:, col] = dk_col
            dv_s[:, col] = dv_col
        dq_ref[...] = _rope_bwd(dq_s[...], cc, sc).astype(BF16)
        cur = pl.ds(pl.multiple_of(i * CHUNK, CHUNK), CHUNK)
        acc_s[cur, :KVW] = _rope_bwd(dk_s[CHUNK:, :], cc, sc)
        acc_s[cur, KVW:] = dv_s[CHUNK:, :]

        @pl.when(i > 0)
        def _():
            prv = pl.ds(pl.multiple_of((i - 1) * CHUNK, CHUNK), CHUNK)
            acc_s[prv, :KVW] += _rope_bwd(dk_s[:CHUNK, :], cp, sp)
            acc_s[prv, KVW:] += dv_s[:CHUNK, :]

        _acc_store(dsk_ref, dsk, i == 0)

        @pl.when(i == nb - 1)
        def _():
            dkv_ref[...] = acc_s[...].astype(BF16)

    return _hosted(
        body,
        name=name,
        sched=sched,
        grid=(nb,),
        in_specs=_swa_specs(d) + [pl.BlockSpec((CHUNK, D), lambda i: (i, 0)), pl.BlockSpec((D, AW), lambda i: (0, 0)),
                                   pl.BlockSpec(memory_space=pl.ANY)],
        out_specs=[
            pl.BlockSpec((CHUNK, AW), lambda i, o=off_q // AW: (i, o)),
            pl.BlockSpec((T, 2 * KVW), lambda i: (0, 0)),
            pl.BlockSpec((1, LANES), lambda i: (0, 0)),
        ],
        out_shape=[
            jax.ShapeDtypeStruct(dP.shape, BF16),
            jax.ShapeDtypeStruct((T, 2 * KVW), BF16),
            jax.ShapeDtypeStruct((1, LANES), F32),
        ],
        input_output_aliases={12: 0},
        scratch_shapes=[
            pltpu.VMEM((T, 2 * KVW), F32),
            pltpu.VMEM((CHUNK, AW), F32),
            pltpu.VMEM((2 * CHUNK, KVW), F32),
            pltpu.VMEM((2 * CHUNK, KVW), F32),
        ],
        compiler_params=_params(
            ("arbitrary",), [((CHUNK, AW), F32)] * 3 + [((D, AW), BF16), ((T, 2 * KVW), BF16)], _nbytes((T, 2 * KVW), F32)
        ),
    )(sinks, P, P, P, P, P, cos2, sin2, cos2, sin2, dyb_b, Gb, dP)


def _place_cols(dst, src, off, *, name, sched=None):
    T, w = src.shape
    tm, tw = _tile(T, 512), _tile(w, 512)
    assert off % tw == 0

    def body(src_ref, dst_in, out_ref):
        out_ref[...] = src_ref[...]

    return _hosted(
        body,
        name=name,
        sched=sched,
        grid=(T // tm, w // tw),
        in_specs=[pl.BlockSpec((tm, tw), lambda i, j: (i, j)), pl.BlockSpec(memory_space=pl.ANY)],
        out_specs=pl.BlockSpec((tm, tw), lambda i, j, o=off // tw: (i, j + o)),
        out_shape=jax.ShapeDtypeStruct(dst.shape, dst.dtype),
        input_output_aliases={1: 0},
        compiler_params=_params(("arbitrary", "arbitrary"), [((tm, tw), dst.dtype)] * 2),
    )(src, dst)


def _branch_merge(d, sgu, att, Ga, Gb, P, b_gate, *, name, sched=None):
    T, D, GW, AW = d.T, d.D, d.GW, d.AW
    _, _, _, _, off_ga, off_gb = _offsets(d)
    tm, tn = _tile(T, 1024), _tile(D, 512)
    assert off_ga % tn == 0 and off_gb % tn == 0

    def body(s_ref, a_ref, ga_w, gb_w, ga_ref, gb_ref, bga_ref, bgb_ref, ya_ref, yb_ref, mg_ref):
        ya = _dot(s_ref[...], ga_w[...], "nt")
        yb = _dot(a_ref[...], gb_w[...], "nt")
        ya_ref[...] = ya
        yb_ref[...] = yb
        sa = jax.nn.sigmoid(ga_ref[...] + bga_ref[...])
        sb = jax.nn.sigmoid(gb_ref[...] + bgb_ref[...])
        mg_ref[...] = (sa * ya + sb * yb).astype(BF16)

    tile = pl.BlockSpec((tm, tn), lambda j, i: (i, j))
    return _hosted(
        body,
        name=name,
        sched=sched,
        grid=(D // tn, T // tm),
        in_specs=[
            pl.BlockSpec((tm, GW), lambda j, i: (i, 0)),
            pl.BlockSpec((tm, AW), lambda j, i: (i, 0)),
            pl.BlockSpec((tn, GW), lambda j, i: (j, 0)),
            pl.BlockSpec((tn, AW), lambda j, i: (j, 0)),
            pl.BlockSpec((tm, tn), lambda j, i, o=off_ga // tn: (i, j + o)),
            pl.BlockSpec((tm, tn), lambda j, i, o=off_gb // tn: (i, j + o)),
            pl.BlockSpec((1, tn), lambda j, i: (0, j)),
            pl.BlockSpec((1, tn), lambda j, i, o=D // tn: (0, j + o)),
        ],
        out_specs=[tile, tile, tile],
        out_shape=[jax.ShapeDtypeStruct((T, D), F32), jax.ShapeDtypeStruct((T, D), F32), jax.ShapeDtypeStruct((T, D), BF16)],
        compiler_params=_params(
            ("arbitrary", "arbitrary"),
            [((tm, GW), BF16), ((tm, AW), BF16), ((tn, GW), BF16), ((tn, AW), BF16)] + [((tm, tn), F32)] * 5,
        ),
    )(sgu, att, Ga, Gb, P, P, b_gate, b_gate)


def _xattn_probs(qh, kh):
    s = _dot(qh, kh, "nt") * (X_HEAD_DIM**-0.5)
    e = jnp.exp(s - jnp.max(s, axis=1, keepdims=True))
    return e / jnp.sum(e, axis=1, keepdims=True)


def _xattn_fwd(d, X, Xb, kv, Gxq, Gxo, ln_g, ln_b, *, name, sched=None):
    T, D, XW, XH, MEM = d.T, d.D, d.XW, d.XH, d.MEM
    tm = _tile(T, 256)

    def body(x_ref, xb_ref, kv_ref, wq_ref, wo_ref, g_ref, b_ref, y_ref, yb_ref, xh_ref, rs_ref, q_ref, o_ref, o_s):
        q = _dot(xb_ref[...], wq_ref[...], "nn").astype(BF16)
        q_ref[...] = q
        for h in range(XH):
            hs = slice(h * X_HEAD_DIM, (h + 1) * X_HEAD_DIM)
            p = _xattn_probs(q[:, hs], kv_ref[:, hs]).astype(BF16)
            o_s[:, hs] = _dot(p, kv_ref[:, XW + h * X_HEAD_DIM : XW + (h + 1) * X_HEAD_DIM], "nn").astype(BF16)
        o = o_s[...]
        o_ref[...] = o
        y, xhat, rstd = _ln_fwd(ALPHA * x_ref[...] + _dot(o, wo_ref[...], "nt"), g_ref[...], b_ref[...])
        y_ref[...] = y
        yb_ref[...] = y.astype(BF16)
        xh_ref[...] = xhat
        rs_ref[...] = rstd

    row = pl.BlockSpec((tm, D), lambda i: (i, 0))
    nar = pl.BlockSpec((tm, XW), lambda i: (i, 0))
    vec = pl.BlockSpec((1, D), lambda i: (0, 0))
    return _hosted(
        body,
        name=name,
        sched=sched,
        grid=(T // tm,),
        in_specs=[
            row,
            row,
            pl.BlockSpec((MEM, 2 * XW), lambda i: (0, 0)),
            pl.BlockSpec((D, XW), lambda i: (0, 0)),
            pl.BlockSpec((D, XW), lambda i: (0, 0)),
            vec,
            vec,
        ],
        out_specs=[row, row, row, pl.BlockSpec((tm, 1), lambda i: (i, 0)), nar, nar],
        out_shape=[
            jax.ShapeDtypeStruct((T, D), F32),
            jax.ShapeDtypeStruct((T, D), BF16),
            jax.ShapeDtypeStruct((T, D), F32),
            jax.ShapeDtypeStruct((T, 1), F32),
            jax.ShapeDtypeStruct((T, XW), BF16),
            jax.ShapeDtypeStruct((T, XW), BF16),
        ],
        scratch_shapes=[pltpu.VMEM((tm, XW), BF16)],
        compiler_params=_params(("arbitrary",), [((tm, D), F32)] * 4 + [((D, XW), BF16)] * 2),
    )(X, Xb, kv, Gxq, Gxo, ln_g, ln_b)


def _xattn_bwd(d, dz, dzb, q_b, kv, Gxq, Gxo, xhat, rstd, ln_g, *, name, sched=None):
    T, D, XW, XH, MEM = d.T, d.D, d.XW, d.XH, d.MEM
    tm = _tile(T, 256)
    scale = X_HEAD_DIM**-0.5

    def body(dz_ref, dzb_ref, q_ref, kv_ref, wq_ref, wo_ref, xh_ref, rs_ref, g_ref, dx_ref, dxb_ref, dg_ref, db_ref, dq_ref, dkv_ref, dq_s):
        first = pl.program_id(0) == 0
        do = _dot(dzb_ref[...], wo_ref[...], "nn").astype(BF16)
        for h in range(XH):
            hs = slice(h * X_HEAD_DIM, (h + 1) * X_HEAD_DIM)
            vs = slice(XW + h * X_HEAD_DIM, XW + (h + 1) * X_HEAD_DIM)
            kh, vh, qh, doh = kv_ref[:, hs], kv_ref[:, vs], q_ref[:, hs], do[:, hs]
            p = _xattn_probs(qh, kh)
            dp = _dot(doh, vh, "nt")
            ds = (p * (dp - jnp.sum(p * dp, axis=1, keepdims=True)) * scale).astype(BF16)
            dq_s[:, hs] = _dot(ds, kh, "nn").astype(BF16)
            _acc_store(dkv_ref.at[h], _dot(ds, qh, "tn"), first)
            _acc_store(dkv_ref.at[XH + h], _dot(p.astype(BF16), doh, "tn"), first)
        dq = dq_s[...]
        dq_ref[...] = dq
        dy = ALPHA * dz_ref[...] + _dot(dq, wq_ref[...], "nt")
        xh = xh_ref[...]
        dz_in = _ln_bwd_vals(dy, xh, rs_ref[...], g_ref[...])
        dx_ref[...] = dz_in
        dxb_ref[...] = dz_in.astype(BF16)
        _acc_store(dg_ref, jnp.sum(dy * xh, axis=0, keepdims=True), first)
        _acc_store(db_ref, jnp.sum(dy, axis=0, keepdims=True), first)

    row = pl.BlockSpec((tm, D), lambda i: (i, 0))
    nar = pl.BlockSpec((tm, XW), lambda i: (i, 0))
    vec = pl.BlockSpec((1, D), lambda i: (0, 0))
    return _hosted(
        body,
        name=name,
        sched=sched,
        grid=(T // tm,),
        in_specs=[
            row,
            row,
            nar,
            pl.BlockSpec((MEM, 2 * XW), lambda i: (0, 0)),
            pl.BlockSpec((D, XW), lambda i: (0, 0)),
            pl.BlockSpec((D, XW), lambda i: (0, 0)),
            row,
            pl.BlockSpec((tm, 1), lambda i: (i, 0)),
            vec,
        ],
        out_specs=[row, row, vec, vec, nar, pl.BlockSpec((2 * XH, MEM, X_HEAD_DIM), lambda i: (0, 0, 0))],
        out_shape=[
            jax.ShapeDtypeStruct((T, D), F32),
            jax.ShapeDtypeStruct((T, D), BF16),
            jax.ShapeDtypeStruct((1, D), F32),
            jax.ShapeDtypeStruct((1, D), F32),
            jax.ShapeDtypeStruct((T, XW), BF16),
            jax.ShapeDtypeStruct((2 * XH, MEM, X_HEAD_DIM), F32),
        ],
        scratch_shapes=[pltpu.VMEM((tm, XW), BF16)],
        compiler_params=_params(("arbitrary",), [((tm, D), F32)] * 5 + [((D, XW), BF16)] * 2),
    )(dz, dzb, q_b, kv, Gxq, Gxo, xhat, rstd, ln_g)


def _resid_ln_outs():
    return (("mn", F32), ("mn", BF16), ("mn", F32), ("m1", F32))


def _layer_fwd(d, X, Xb, memb, G, S, cos2, sin2, tag, sched=None):
    D = d.D
    mm = functools.partial(_matmul, sched=sched)
    P = mm(Xb, G["in"], "nt", name=f"proj_{tag}", tm=1024, tn=_tile(d.IN, 1280))
    sgu = _gmlp_fwd(d, P, S["ln_v_g"], S["ln_v_b"], S["w_s"], S["b_sT"], name=f"gmlp_fwd_{tag}", sched=sched)
    att = _swa_fwd(d, P, cos2, sin2, S["sinks"], name=f"swa_fwd_{tag}", sched=sched)
    ya, yb, merged = _branch_merge(d, sgu, att, G["a"], G["b"], P, S["b_gate"], name=f"merge_{tag}", sched=sched)
    X1, X1b, xh1, rs1 = mm(
        merged, G["o"], "nn", name=f"oproj_ln_{tag}", tn=D, tk=1024, out_defs=_resid_ln_outs(), epilogue=_ep_resid_ln,
        extras=((X, "mn", 0), (S["ln1_g"], "row", 0), (S["ln1_b"], "row", 0)),
    )
    kv = mm(memb, G["xkv"], "nn", name=f"xkv_{tag}", out_defs=(("mn", BF16),))
    X2, X2b, xh2, rs2, q_b, o_b = _xattn_fwd(d, X1, X1b, kv, G["xq"], G["xo"], S["ln2_g"], S["ln2_b"], name=f"xattn_fwd_{tag}", sched=sched)
    H, A = mm(X2b, G["up"], "nt", name=f"up_{tag}", tm=1024, out_defs=(("mn", F32), ("mn", BF16)), epilogue=_ep_relu2)
    X3, X3b, xh3, rs3 = mm(
        A, G["down"], "nn", name=f"down_ln_{tag}", tn=D, tk=1024, out_defs=_resid_ln_outs(), epilogue=_ep_resid_ln,
        extras=((X2, "mn", 0), (S["ln3_g"], "row", 0), (S["ln3_b"], "row", 0)),
    )
    saved = dict(Xb=Xb, P=P, sgu=sgu, att=att, ya=ya, yb=yb, merged=merged, X1b=X1b, xh1=xh1, rs1=rs1, kv=kv, q_b=q_b,
                 o_b=o_b, X2b=X2b, xh2=xh2, rs2=rs2, H=H, A=A, xh3=xh3, rs3=rs3)
    return X3, X3b, saved


def _layer_bwd(d, dXout, sv, memb, G, S, cos2, sin2, tag, sched=None, on_grad=None, target=None):
    D = d.D
    mm = functools.partial(_matmul, sched=sched)
    emit = on_grad if on_grad is not None else (lambda n, g: None)
    wide = dict(tn=D)
    _, _, off_k, _, off_ga, off_gb = _offsets(d)
    bf = (("mn", BF16),)
    dz3, dz3b, dg3, db3, *loss = _ln_bwd(dXout, sv["xh3"], sv["rs3"], S["ln3_g"], name=f"ln3_bwd_{tag}", sched=sched, target=target)
    dHb = mm(dz3b, G["down"], "nt", name=f"down_dx_{tag}", tm=1024, out_defs=bf, epilogue=_ep_relu2_bwd, extras=((sv["H"], "mn", 0),))
    g_down = mm(sv["A"], dz3b, "tn", name=f"down_dw_{tag}", **wide)
    emit("down", g_down)
    ln_bwd_outs = (("mn", F32), ("mn", BF16), ("acc", F32), ("acc", F32))
    dz2, dz2b, dg2, db2 = mm(
        dHb, G["up"], "nn", name=f"up_dx_{tag}", tn=D, tk=1024, out_defs=ln_bwd_outs, epilogue=_ep_resid_ln_bwd,
        extras=((dz3, "mn", 0), (sv["xh2"], "mn", 0), (sv["rs2"], "m1", 0), (S["ln2_g"], "row", 0)),
    )
    g_up = mm(dHb, sv["X2b"], "tn", name=f"up_dw_{tag}", **wide)
    emit("up", g_up)
    dz1, dz1b, dg1, db1, dq_b, dkv = _xattn_bwd(d, dz2, dz2b, sv["q_b"], sv["kv"], G["xq"], G["xo"], sv["xh1"], sv["rs1"], S["ln1_g"],
                                               name=f"xattn_bwd_{tag}", sched=sched)
    g_xo = mm(dz2b, sv["o_b"], "tn", name=f"xo_dw_{tag}")
    emit("xo", g_xo)
    g_xq = mm(sv["X1b"], dq_b, "tn", name=f"xq_dw_{tag}")
    emit("xq", g_xq)
    dkv_b = dkv.transpose(1, 0, 2).reshape(d.MEM, 2 * d.XW).astype(BF16)
    g_xkv = mm(memb, dkv_b, "tn", name=f"xkv_dw_{tag}")
    emit("xkv", g_xkv)
    dya_b, dyb_b, dP, dgb_b, dbga, dbgb = mm(
        dz1b, G["o"], "nt", name=f"oproj_dx_{tag}", tm=1024, tn=512,
        out_defs=(("mn", BF16), ("mn", BF16), ("cols", BF16, d.IN, off_ga), ("mn", BF16), ("acc", F32), ("acc", F32)),
        epilogue=_ep_gate_bwd,
        extras=((sv["P"], "mn", off_ga), (sv["P"], "mn", off_gb), (sv["ya"], "mn", 0), (sv["yb"], "mn", 0),
                (S["b_gate"], "row", 0), (S["b_gate"], "row", D)),
    )
    g_o = mm(sv["merged"], dz1b, "tn", name=f"oproj_dw_{tag}", **wide)
    emit("o", g_o)
    dP = _place_cols(dP, dgb_b, off_gb, name=f"place_dgb_{tag}")
    dP, dwm, dbsT, dlvg, dlvb = _gmlp_bwd(d, sv["P"], dya_b, G["a"], S["ln_v_g"], S["ln_v_b"], S["w_s"], S["b_sT"], dP, name=f"gmlp_bwd_{tag}", sched=sched)
    g_a = mm(dya_b, sv["sgu"], "tn", name=f"bra_dw_{tag}")
    emit("a", g_a)
    dP, dkvr_b, dsk = _swa_bwd(d, sv["P"], dyb_b, G["b"], cos2, sin2, S["sinks"], dP, name=f"swa_bwd_{tag}", sched=sched)
    g_b = mm(dyb_b, sv["att"], "tn", name=f"brb_dw_{tag}")
    emit("b", g_b)
    dP = _place_cols(dP, dkvr_b, off_k, name=f"place_dkv_{tag}")
    small = dict(b_gate=jnp.concatenate([dbga, dbgb], axis=1), ln_v_g=dlvg, ln_v_b=dlvb, w_s=dwm, b_sT=dbsT, sinks=dsk,
                 ln1_g=dg1, ln1_b=db1, ln2_g=dg2, ln2_b=db2, ln3_g=dg3, ln3_b=db3)
    emit("small", small)
    g_in = mm(dP, sv["Xb"], "tn", name=f"proj_dw_{tag}", **wide)
    emit("in", g_in)
    dXin = mm(dP, G["in"], "nn", name=f"proj_dx_{tag}", tn=D, tk=_tile(d.IN, 1920), epilogue=_ep_resid, extras=((dz1, "mn", 0),))
    big = {"in": g_in, "a": g_a, "b": g_b, "o": g_o, "xq": g_xq, "xkv": g_xkv, "xo": g_xo, "up": g_up, "down": g_down}
    return dXin, big, small, (loss[0] if loss else None)


def _rope_tables(T):
    inv = 1.0 / (ROPE_THETA ** (jnp.arange(0, HEAD_DIM, 2, dtype=F32) / HEAD_DIM))
    ang = jnp.arange(T, dtype=F32)[:, None] * inv[None, :]
    cos, sin = jnp.cos(ang), jnp.sin(ang)
    reps = LANES // HEAD_DIM
    return jnp.concatenate([cos, cos] * reps, axis=1), jnp.concatenate([-sin, sin] * reps, axis=1)


def _local_step(d, x, mem, target, Gs, Ss, sched=None, on_grad=None):
    cos2, sin2 = _rope_tables(d.T)
    memb = mem.astype(BF16)
    X, Xb = x, x.astype(BF16)
    saved = []
    for l in range(DEPTH):
        X, Xb, sv = _layer_fwd(d, X, Xb, memb, Gs[l], Ss[l], cos2, sin2, f"l{l}", sched)
        saved.append(sv)
    bigs, smalls = [None] * DEPTH, [None] * DEPTH
    dX, loss = X, None
    for l in reversed(range(DEPTH)):
        emit = None if on_grad is None else functools.partial(on_grad, l)
        head = target if l == DEPTH - 1 else None
        dX, bigs[l], smalls[l], got = _layer_bwd(d, dX, saved[l], memb, Gs[l], Ss[l], cos2, sin2, f"l{l}", sched, emit, head)
        loss = got if got is not None else loss
    return loss, dX, bigs, smalls


def _place():
    x, y, c = lax.axis_index("x"), lax.axis_index("y"), lax.axis_index("c")
    return x, y, c, [(1 - x, y), (x, 1 - y), (1 - x, 1 - y)]


def _all_gather(shards, *, name, sched=None):
    n = len(shards)
    any_spec = pl.BlockSpec(memory_space=pl.ANY)

    def body(*refs):
        ins, outs = refs[:n], refs[n : 2 * n]
        send_sems, recv_sems, local_sems = refs[2 * n :]
        x, y, c, chips = _place()
        me, sibling = (x, y, c), (x, y, 1 - c)

        def rows(w, p):
            r = ins[w].shape[0]
            return outs[w].at[pl.ds((4 * p[0] + 2 * p[1] + p[2]) * r, r), :]

        def copy(w, k, block, to, src=None):
            return pltpu.make_async_remote_copy(
                src_ref=rows(w, block) if src is None else src, dst_ref=rows(w, block),
                send_sem=send_sems.at[7 * w + k], recv_sem=recv_sems.at[7 * w + k], device_id=to, device_id_type=MESH)

        mine = [pltpu.make_async_copy(ins[w], rows(w, me), local_sems.at[w]) for w in range(n)]
        for cp in mine:
            cp.start()
        first = []
        for w in range(n):
            first.append(copy(w, 0, me, sibling, src=ins[w]))
            first += [copy(w, 1 + j, me, (*chip, c), src=ins[w]) for j, chip in enumerate(chips)]
        for cp in first:
            cp.start()
        passed = []
        for j, chip in enumerate(chips):
            for w in range(n):
                copy(w, 1 + j, (*chip, c), me).wait_recv()
                fwd = copy(w, 4 + j, (*chip, c), sibling)
                fwd.start()
                passed.append(fwd)
        for w in range(n):
            copy(w, 0, sibling, me).wait_recv()
            for j, chip in enumerate(chips):
                copy(w, 4 + j, (*chip, 1 - c), me).wait_recv()
        for cp in first + passed:
            cp.wait_send()
        for cp in mine:
            cp.wait()

    return _hosted(
        body,
        name=name,
        sched=sched,
        in_specs=[any_spec] * n,
        out_specs=[any_spec] * n,
        out_shape=[jax.ShapeDtypeStruct((N_DEV * s.shape[0], s.shape[1]), s.dtype) for s in shards],
        scratch_shapes=[pltpu.SemaphoreType.DMA((7 * n,)), pltpu.SemaphoreType.DMA((7 * n,)), pltpu.SemaphoreType.DMA((n,))],
    )(*shards)


class _Task:
    def __init__(self, ins, out_shapes, n_remote, n_local, plan, done, aliases=None, sibling=False, chips=False):
        self.ins, self.out_shapes, self.n_remote, self.n_local = list(ins), list(out_shapes), n_remote, n_local
        self.plan, self.done, self.aliases = plan, done, dict(aliases or {})
        self.sibling, self.chips = sibling, chips


class _Sched:
    def __init__(self):
        self.pending = {}

    def at(self, host, task):
        self.pending.setdefault(host, []).append(task)

    def take(self, host):
        return self.pending.pop(host, [])


def _in_hbm(out_shape):
    if isinstance(out_shape, (list, tuple)):
        return [_in_hbm(s) for s in out_shape]
    return pltpu.HBM(out_shape.shape, out_shape.dtype)


def _hosted(body, *, name, sched=None, tasks=(), grid=(), in_specs=None, out_specs=None, out_shape=(), scratch_shapes=(),
            compiler_params=None, input_output_aliases=None):
    tasks = list(tasks) + (sched.take(name) if sched is not None else [])
    scratch_shapes = list(scratch_shapes)
    kwargs = dict(name=name)
    core_aliases = dict(input_output_aliases or {})
    if grid:
        kwargs["grid"] = grid
    if compiler_params is not None:
        kwargs["compiler_params"] = compiler_params
    if not tasks:
        if in_specs is not None:
            kwargs.update(in_specs=in_specs, out_specs=out_specs)
        return pl.pallas_call(body, out_shape=_in_hbm(out_shape), scratch_shapes=scratch_shapes, input_output_aliases=core_aliases, **kwargs)
    assert in_specs is not None and out_specs is not None, name
    with_sibling, with_chips = any(t.sibling for t in tasks), any(t.chips for t in tasks)
    collective_id = {(True, False): 1, (False, True): 2, (True, True): 3}[(with_sibling, with_chips)]
    kwargs["compiler_params"] = dataclasses.replace(
        compiler_params if compiler_params is not None else pltpu.CompilerParams(), collective_id=collective_id)
    single = not isinstance(out_shape, (list, tuple))
    core_shape = [out_shape] if single else list(out_shape)
    core_specs = [out_specs] if single else list(out_specs)
    in_specs = list(in_specs)
    n_in, n_out, n_scr = len(in_specs), len(core_shape), len(scratch_shapes)
    t_ins = [a for t in tasks for a in t.ins]
    t_outs = [s for t in tasks for s in t.out_shapes]
    n_rem = sum(t.n_remote for t in tasks)
    n_loc = sum(t.n_local for t in tasks)
    aliases, pos_in, pos_out = dict(core_aliases), n_in, n_out
    for t in tasks:
        for a, b in t.aliases.items():
            aliases[pos_in + a] = pos_out + b
        pos_in += len(t.ins)
        pos_out += len(t.out_shapes)
    any_spec = pl.BlockSpec(memory_space=pl.ANY)

    def wrapped(*refs):
        core_in, t_in = refs[:n_in], refs[n_in : n_in + len(t_ins)]
        o0 = n_in + len(t_ins)
        core_out, t_out = refs[o0 : o0 + n_out], refs[o0 + n_out : o0 + n_out + len(t_outs)]
        s0 = o0 + n_out + len(t_outs)
        core_scr = refs[s0 : s0 + n_scr]
        send_sems, recv_sems, local_sems = refs[s0 + n_scr :]
        remote, local = [], []
        pi = po = pr = pq = 0
        for t in tasks:
            r, q = t.plan(t_in[pi : pi + len(t.ins)], t_out[po : po + len(t.out_shapes)], send_sems, recv_sems, local_sems, pr, pq)
            assert len(r) == t.n_remote and len(q) == t.n_local
            remote += r
            local += q
            pi, po, pr, pq = pi + len(t.ins), po + len(t.out_shapes), pr + t.n_remote, pq + t.n_local

        def start():
            x, y, c, chips = _place()
            peers = ([(x, y, 1 - c)] if with_sibling else []) + ([(*chip, c) for chip in chips] if with_chips else [])
            barrier = pltpu.get_barrier_semaphore()
            for peer in peers:
                pl.semaphore_signal(barrier, inc=1, device_id=peer, device_id_type=MESH)
            pl.semaphore_wait(barrier, len(peers))
            for cp in local + remote:
                cp.start()

        def finish():
            for cp in remote + local:
                cp.wait()

        if grid:
            ids = [pl.program_id(a) for a in range(len(grid))]
            first = functools.reduce(jnp.logical_and, [i == 0 for i in ids])
            last = functools.reduce(jnp.logical_and, [i == g - 1 for i, g in zip(ids, grid)])
            pl.when(first)(start)
            body(*core_in, *core_out, *core_scr)
            pl.when(last)(finish)
        else:
            start()
            body(*core_in, *core_out, *core_scr)
            finish()

    call = pl.pallas_call(
        wrapped,
        in_specs=in_specs + [any_spec] * len(t_ins),
        out_specs=core_specs + [any_spec] * len(t_outs),
        out_shape=_in_hbm(core_shape + t_outs),
        scratch_shapes=scratch_shapes
        + [pltpu.SemaphoreType.DMA((n_rem,)), pltpu.SemaphoreType.DMA((n_rem,)), pltpu.SemaphoreType.DMA((max(n_loc, 1),))],
        input_output_aliases=aliases,
        **kwargs,
    )

    def run(*operands):
        outs = call(*operands, *t_ins)
        po = n_out
        for t in tasks:
            t.done(list(outs[po : po + len(t.out_shapes)]))
            po += len(t.out_shapes)
        return outs[0] if single else list(outs[:n_out])

    return run


def _comm_only(tasks, *, name):
    _hosted(lambda: None, name=name, tasks=tasks, in_specs=[], out_shape=[], out_specs=[])()


def _rows(ref, block, n_blocks):
    r = ref.shape[0] // n_blocks
    return ref.at[pl.ds(block * r, r), :]


def _remote(src, dst, send_sems, recv_sems, k, to):
    return pltpu.make_async_remote_copy(src_ref=src, dst_ref=dst, send_sem=send_sems.at[k], recv_sem=recv_sems.at[k],
                                        device_id=to, device_id_type=MESH)


def _gather_stage1(shards, done, part=(0, 1, 1), into=None):
    n = len(shards)
    k0, k1, n_parts = part

    def plan(ins, outs, send_sems, recv_sems, local_sems, pr, pq):
        x, y, c, chips = _place()
        mine = 4 * x + 2 * y + c
        remote, local = [], []
        for w in range(n):
            r = shards[w].shape[0]
            rp = r // n_parts
            src = ins[w].at[pl.ds(k0 * rp, (k1 - k0) * rp), :]
            dst = outs[w].at[pl.ds(mine * r + k0 * rp, (k1 - k0) * rp), :]
            local.append(pltpu.make_async_copy(src, dst, local_sems.at[pq + w]))
            for j, to in enumerate([(x, y, 1 - c)] + [(*chip, c) for chip in chips]):
                remote.append(_remote(src, dst, send_sems, recv_sems, pr + 4 * w + j, to))
        return remote, local

    shapes = [jax.ShapeDtypeStruct((N_DEV * s.shape[0], s.shape[1]), s.dtype) for s in shards]
    if into is None:
        return _Task(shards, shapes, 4 * n, n, plan, done, sibling=True, chips=True)
    return _Task(list(shards) + list(into), shapes, 4 * n, n, plan, done, aliases={n + w: w for w in range(n)}, sibling=True, chips=True)


def _gather_stage2(partial, done):
    n = len(partial)

    def plan(ins, outs, send_sems, recv_sems, local_sems, pr, pq):
        x, y, c, chips = _place()
        remote = []
        for w in range(n):
            for j, chip in enumerate(chips):
                rows = _rows(outs[w], 4 * chip[0] + 2 * chip[1] + c, N_DEV)
                remote.append(_remote(rows, rows, send_sems, recv_sems, pr + 3 * w + j, (x, y, 1 - c)))
        return remote, []

    shapes = [jax.ShapeDtypeStruct(p.shape, p.dtype) for p in partial]
    return _Task(partial, shapes, 3 * n, 0, plan, done, aliases={w: w for w in range(n)}, sibling=True)


def _sibling_stage(grads, done):
    n = len(grads)

    def plan(ins, outs, send_sems, recv_sems, local_sems, pr, pq):
        x, y, c, _ = _place()
        remote = []
        for w in range(n):
            for k in range(4):
                src = _rows(ins[w], 4 * (k // 2) + 2 * (k % 2) + (1 - c), N_DEV)
                remote.append(_remote(src, _rows(outs[w], k, 4), send_sems, recv_sems, pr + 4 * w + k, (x, y, 1 - c)))
        return remote, []

    shapes = [jax.ShapeDtypeStruct((g.shape[0] // 2, g.shape[1]), g.dtype) for g in grads]
    return _Task(grads, shapes, 4 * n, 0, plan, done, sibling=True)


def _chips_stage(sends, done, part=(0, 1), into=None):
    n = len(sends)
    k, n_parts = part

    def plan(ins, outs, send_sems, recv_sems, local_sems, pr, pq):
        x, y, c, chips = _place()
        remote = []
        for w in range(n):
            r = sends[w].shape[0] // 3
            rp = r // n_parts
            for j, chip in enumerate(chips):
                rows = pl.ds(j * r + k * rp, rp)
                remote.append(_remote(ins[w].at[rows, :], outs[w].at[rows, :], send_sems, recv_sems, pr + 3 * w + j, (*chip, c)))
        return remote, []

    shapes = [jax.ShapeDtypeStruct(s.shape, s.dtype) for s in sends]
    if into is None:
        return _Task(sends, shapes, 3 * n, 0, plan, done, chips=True)
    return _Task(list(sends) + list(into), shapes, 3 * n, 0, plan, done, aliases={n + w: w for w in range(n)}, chips=True)


def _pair_sum(grad, got, place, *, name):
    R, C = grad.shape
    r = R // N_DEV
    tr = _tile(r, 256, 16)
    nt = r // tr

    def chip_of(s, pr):
        fx = jnp.where((s == 1) | (s == 3), 1, 0)
        fy = jnp.where(s >= 2, 1, 0)
        return pr[0] ^ fx, pr[1] ^ fy

    def grad_map(s, t, pr):
        kx, ky = chip_of(s, pr)
        return ((4 * kx + 2 * ky + pr[2]) * nt + t, 0)

    def got_map(s, t, pr):
        kx, ky = chip_of(s, pr)
        return ((2 * kx + ky) * nt + t, 0)

    def body(pr, g_ref, r_ref, own_ref, send_ref):
        s = pl.program_id(0)
        tot = g_ref[...] + r_ref[...]

        @pl.when(s == 0)
        def _():
            own_ref[...] = tot

        @pl.when(s > 0)
        def _():
            send_ref[...] = tot.astype(BF16)

    return pl.pallas_call(
        body,
        name=name,
        grid_spec=pltpu.PrefetchScalarGridSpec(
            num_scalar_prefetch=1,
            grid=(4, nt),
            in_specs=[pl.BlockSpec((tr, C), grad_map), pl.BlockSpec((tr, C), got_map)],
            out_specs=[
                pl.BlockSpec((tr, C), lambda s, t, pr: (jnp.where(s == 0, t, nt - 1), 0)),
                pl.BlockSpec((tr, C), lambda s, t, pr: (jnp.where(s == 0, 0, (s - 1) * nt + t), 0)),
            ],
        ),
        out_shape=[jax.ShapeDtypeStruct((r, C), F32), jax.ShapeDtypeStruct((3 * r, C), BF16)],
        compiler_params=_params(("arbitrary", "arbitrary"), [((tr, C), F32)] * 4),
    )(place, grad, got)


def _adam_vals(w, g, m, v):
    m = ADAM_B1 * m + (1.0 - ADAM_B1) * g
    v = ADAM_B2 * v + (1.0 - ADAM_B2) * (g * g)
    m_hat = m / (1.0 - ADAM_B1**ADAM_STEP)
    v_hat = v / (1.0 - ADAM_B2**ADAM_STEP)
    delta = -ADAM_LR * (m_hat / (jnp.sqrt(v_hat) + ADAM_EPS) + ADAM_WD * w)
    return delta, m, v


def _adam_big(w3, m3, v3, own, got, layer, transposed, prev, *, name, sched=None):
    _, A, B = w3.shape
    ta = _tile(A, 256, 16)
    nt = A // ta
    b_pad = (-B) % LANES

    def body(*refs):
        w_ref, m_ref, v_ref, own_ref, g1_ref, g2_ref, g3_ref = refs[:7]
        g_ref, d_ref, nm_ref, nv_ref = refs[-4:]
        g = ((own_ref[...] + g1_ref[...].astype(F32)) + g2_ref[...].astype(F32)) + g3_ref[...].astype(F32)
        if transposed:
            if b_pad:
                g = jnp.concatenate([g, jnp.zeros((b_pad, ta), F32)], axis=0)
            g = g.T[:, :B]
        g_ref[...] = g
        d_ref[...], nm_ref[...], nv_ref[...] = _adam_vals(w_ref[...], g, m_ref[...], v_ref[...])

    nat = pl.BlockSpec((None, ta, B), lambda t: (layer, t, 0))
    if transposed:
        parts = [pl.BlockSpec((B, ta), lambda t: (0, t))] + [pl.BlockSpec((B, ta), lambda t, j=j: (j, t)) for j in range(3)]
    else:
        parts = [pl.BlockSpec((ta, B), lambda t: (t, 0))] + [pl.BlockSpec((ta, B), lambda t, j=j: (j * nt + t, 0)) for j in range(3)]
    keep = [] if prev is None else list(prev)
    outs = _hosted(
        body,
        name=name,
        sched=sched,
        grid=(nt,),
        in_specs=[nat, nat, nat] + parts + [pl.BlockSpec(memory_space=pl.ANY)] * len(keep),
        out_specs=[nat] * 4,
        out_shape=[jax.ShapeDtypeStruct(w3.shape, F32)] * 4,
        input_output_aliases={7 + k: k for k in range(len(keep))},
        compiler_params=_params(("arbitrary",), [((ta, B), F32)] * 10),
    )(w3, m3, v3, own, got, got, got, *keep)
    return list(outs)


def _adam_small(w, m, v, parts, *, name, sched=None):
    R = w.shape[0]

    def body(w_ref, m_ref, v_ref, p_ref, g_ref, d_ref, nm_ref, nv_ref):
        g = p_ref[pl.ds(0, R), :]
        for k in range(1, N_DEV):
            g = g + p_ref[pl.ds(k * R, R), :]
        g_ref[...] = g
        d_ref[...], nm_ref[...], nv_ref[...] = _adam_vals(w_ref[...], g, m_ref[...], v_ref[...])

    return _hosted(
        body,
        name=name,
        sched=sched,
        out_shape=[jax.ShapeDtypeStruct((R, LANES), F32)] * 4,
        compiler_params=_params(None, [((R, LANES), F32)] * 16),
    )(w, m, v, parts)


SMALL_NAMES = ("b_gate", "ln_v_g", "ln_v_b", "w_s", "b_s", "sinks", "ln1_g", "ln1_b", "ln2_g", "ln2_b", "ln3_g", "ln3_b")
PACK_ALIGN = 8 * LANES


def _pack(arrays):
    flat = []
    for a in arrays:
        a = a.reshape(-1)
        flat.append(jnp.pad(a, (0, (-a.shape[0]) % PACK_ALIGN)))
    return jnp.concatenate(flat).reshape(-1, LANES)


def _unpack(packed, shapes):
    flat = packed.reshape(-1)
    out, pos = [], 0
    for s in shapes:
        n = 1
        for e in s:
            n *= e
        out.append(flat[pos : pos + n].reshape(s))
        pos += n + (-n) % PACK_ALIGN
    return out


BIG = (("in", "w_in", True), ("a", "w_br_a", True), ("b", "w_br_b", True), ("o", "w_o", False), ("xq", "w_xq", False),
       ("xkv", "w_xkv", False), ("xo", "w_xo", True), ("up", "w_up", True), ("down", "w_down", False))


SMALL_BIG = ("a", "b", "o", "xq", "xkv", "xo")
GATHER_PLAN = (
    (0, ("in",), None, None),
    (0, SMALL_BIG, ("proj_l0",), "gmlp_fwd_l0"),
    (0, ("up",), ("swa_fwd_l0", "merge_l0"), "oproj_ln_l0"),
    (0, ("down",), ("oproj_ln_l0", "xattn_fwd_l0"), "up_l0"),
    (1, ("in",), ("up_l0",), "down_ln_l0"),
    (1, SMALL_BIG, ("down_ln_l0",), "proj_l1"),
    (1, ("up",), ("down_ln_l0", ("proj_l1", 3)), "gmlp_fwd_l1"),
    (1, ("down",), (("swa_fwd_l1", 2), "merge_l1", "oproj_ln_l1"), "xattn_fwd_l1"),
)
EARLY_SMALL_BIG = ("o", "xq", "xkv", "xo")
LATE_SMALL_BIG = ("a", "b")
GRAD_HOSTS = {
    1: {"down": ("up_dx_l1", "swa_bwd_l1"), "up": ("xattn_bwd_l1", ("gmlp_bwd_l1", "proj_dw_l1")),
        **{g: ("gmlp_bwd_l1", "proj_dx_l1") for g in EARLY_SMALL_BIG}, **{g: ("proj_dw_l1", "proj_dx_l1") for g in LATE_SMALL_BIG},
        "in": ("proj_dx_l1", ("down_dx_l0", "down_dw_l0"))},
    0: {"down": ("up_dx_l0", ("oproj_dx_l0", "gmlp_bwd_l0")), "up": ("xattn_bwd_l0", "swa_bwd_l0"),
        **{g: ("gmlp_bwd_l0", "proj_dw_l0") for g in EARLY_SMALL_BIG}, **{g: ("proj_dw_l0", "proj_dx_l0") for g in LATE_SMALL_BIG},
        "in": (None, "proj_dx_l0")},
}
SMALL_GRAD_HOSTS = ("proj_dw_l0", "proj_dx_l0")


def kernel(x, mem, w_in, b_gate, ln_v_g, ln_v_b, w_s, b_s, sinks, w_br_a, w_br_b, w_o, ln1_g, ln1_b, w_xq, w_xkv, w_xo, ln2_g, ln2_b, w_up, w_down, ln3_g, ln3_b, loss_target, m_w_in, m_b_gate, m_ln_v_g, m_ln_v_b, m_w_s, m_b_s, m_sinks, m_w_br_a, m_w_br_b, m_w_o, m_ln1_g, m_ln1_b, m_w_xq, m_w_xkv, m_w_xo, m_ln2_g, m_ln2_b, m_w_up, m_w_down, m_ln3_g, m_ln3_b, v_w_in, v_b_gate, v_ln_v_g, v_ln_v_b, v_w_s, v_b_s, v_sinks, v_w_br_a, v_w_br_b, v_w_o, v_ln1_g, v_ln1_b, v_w_xq, v_w_xkv, v_w_xo, v_ln2_g, v_ln2_b, v_w_up, v_w_down, v_ln3_g, v_ln3_b):
    given = dict(locals())
    T, D = x.shape[1], x.shape[2]
    IN, GW, AW, XW, DFF = w_in.shape[2] * N_DEV, ln_v_g.shape[1], w_br_b.shape[1], w_xq.shape[2], w_up.shape[2] * N_DEV
    KVW = (IN - 2 * GW - AW - 2 * D) // 2
    d = Dims(T=T, D=D, IN=IN, GW=GW, G=GW // GROUP_DIM, AW=AW, KVW=KVW, HQ=AW // HEAD_DIM, HKV=KVW // HEAD_DIM, XW=XW,
             XH=XW // X_HEAD_DIM, MEM=mem.shape[1], DFF=DFF)
    place = jnp.stack([lax.axis_index("x"), lax.axis_index("y"), lax.axis_index("c")]).astype(jnp.int32)

    sched = _Sched()
    big_of = {g: (p, tr) for g, p, tr in BIG}

    def shard(l, g):
        p, tr = big_of[g]
        return (given[p][l].T if tr else given[p][l]).astype(BF16)

    Gs = [{}, {}]

    def plan_gather(l, names, hosts1, host2):
        shards = [shard(l, g) for g in names]
        if hosts1 is None:
            Gs[l].update(zip(names, _all_gather(shards, name=f"gather_{names[0]}_l{l}")))
            return

        spans = [(h, 1) if isinstance(h, str) else h for h in hosts1]
        total = sum(cnt for _, cnt in spans)

        def register(i, k0, into):
            host, cnt = spans[i]

            def done(outs):
                if i + 1 < len(spans):
                    register(i + 1, k0 + cnt, outs)
                else:
                    sched.at(host2, _gather_stage2(outs, lambda full: Gs[l].update(zip(names, full))))

            sched.at(host, _gather_stage1(shards, done, part=(k0, k0 + cnt, total), into=into))

        register(0, 0, None)

    for entry in GATHER_PLAN:
        plan_gather(*entry)
    Ss = []
    for l in range(DEPTH):
        S = {n: given[n][l].reshape(1, -1) for n in SMALL_NAMES if n not in ("w_s", "b_s", "sinks")}
        S["w_s"], S["b_sT"], S["sinks"] = w_s[l], b_s[l].T, sinks[l]
        Ss.append(S)

    owns, recvs = {}, {}

    smalls_seen = {}
    gathered_small = []

    def pack_small(src):
        parts = []
        for l in range(DEPTH):
            for n in SMALL_NAMES:
                a = src[l]["b_sT"].T if n == "b_s" else src[l][n]
                parts.append(a[0, : d.HQ] if n == "sinks" else a)
        return _pack(parts)

    def on_small(l, small):
        smalls_seen[l] = small
        if len(smalls_seen) == DEPTH:
            host1, host2 = SMALL_GRAD_HOSTS
            sched.at(host1, _gather_stage1([pack_small(smalls_seen)], lambda part: sched.at(
                host2, _gather_stage2(part, lambda full: gathered_small.append(full[0])))))

    def on_grad(l, g, grad):
        if g == "small":
            return on_small(l, grad)
        sib_host, chips_host = GRAD_HOSTS[l][g]

        def after_sibling(got):
            owns[(l, g)], send = _pair_sum(grad, got[0], place, name=f"grad_pair_sum_{g}_l{l}")
            hosts = chips_host if isinstance(chips_host, tuple) else (chips_host,)

            def register(k, into):
                def done(r):
                    if k + 1 < len(hosts):
                        register(k + 1, r)
                    else:
                        recvs[(l, g)] = r[0]

                task = _chips_stage([send], done, part=(k, len(hosts)), into=into)
                if hosts[k] is None:
                    _comm_only([task], name=f"grad_chips_{g}_l{l}")
                else:
                    sched.at(hosts[k], task)

            register(0, None)

        task = _sibling_stage([grad], after_sibling)
        if sib_host is None:
            _comm_only([task], name=f"grad_sibling_{g}_l{l}")
        else:
            sched.at(sib_host, task)

    loss, dX, bigs, smalls = _local_step(d, x[0], mem[0], loss_target[0], Gs, Ss, sched, on_grad)
    loss = lax.psum(loss[0, 0], ("x", "y", "c"))

    assert not sched.pending, sorted(sched.pending)

    def viewed(p, tr):
        return tr and given[p].shape[2] % LANES != 0

    res = {p: None for _, p, _ in BIG}
    for l in reversed(range(DEPTH)):
        for g, p, tr in BIG:
            view = viewed(p, tr)
            w3, m3, v3 = ((jnp.swapaxes(a, 1, 2) if view else a) for a in (given[p], given["m_" + p], given["v_" + p]))
            res[p] = _adam_big(w3, m3, v3, owns[(l, g)], recvs[(l, g)], l, tr and not view, res[p], name=f"adamw_{g}_l{l}")
    for g, p, tr in BIG:
        if viewed(p, tr):
            res[p] = [jnp.swapaxes(a, 1, 2) for a in res[p]]

    shapes = [given[n].shape[1:] for n in SMALL_NAMES]
    pw, pm, pv = (_pack([given[pre + n][l] for l in range(DEPTH) for n in SMALL_NAMES]) for pre in ("", "m_", "v_"))
    small_out = [_unpack(o, shapes * DEPTH) for o in _adam_small(pw, pm, pv, gathered_small[0], name="adamw_small")]

    names = ["w_in", "b_gate", "ln_v_g", "ln_v_b", "w_s", "b_s", "sinks", "w_br_a", "w_br_b", "w_o", "ln1_g", "ln1_b", "w_xq",
             "w_xkv", "w_xo", "ln2_g", "ln2_b", "w_up", "w_down", "ln3_g", "ln3_b"]
    out = [loss, dX[None]]
    for kind in range(4):
        for n in names:
            if n in SMALL_NAMES:
                i = SMALL_NAMES.index(n)
                out.append(jnp.stack([small_out[kind][l * len(SMALL_NAMES) + i] for l in range(DEPTH)]))
            else:
                out.append(res[n][kind])
    return tuple(out)
```

```python
import collections
import dataclasses
import functools

import jax
import jax.numpy as jnp
from jax import lax
from jax.experimental import pallas as pl
from jax.experimental.pallas import tpu as pltpu

F32 = jnp.float32
BF16 = jnp.bfloat16
MESH = pl.DeviceIdType.MESH

N_DEV = 8
DEPTH = 2
CHUNK = 128
HEAD_DIM = 64
ROPE_HALF = HEAD_DIM // 2
X_HEAD_DIM = 128
GROUP_DIM = 128
LANES = 128
ROPE_THETA = 10000.0
LN_EPS = 1e-5
ALPHA = (2 * DEPTH) ** 0.25
ADAM_LR = 0.001
ADAM_B1 = 0.9
ADAM_B2 = 0.999
ADAM_EPS = 1e-08
ADAM_WD = 0.01
ADAM_STEP = 10
VMEM_BYTES = 64 * 1024 * 1024
VMEM_TEMP_BYTES = 20 * 1024 * 1024

Dims = collections.namedtuple("Dims", "T D IN GW G AW KVW HQ HKV XW XH MEM DFF")


def _offsets(d):
    off_v = d.GW
    off_q = 2 * d.GW
    off_k = off_q + d.AW
    off_va = off_k + d.KVW
    off_ga = off_va + d.KVW
    off_gb = off_ga + d.D
    return off_v, off_q, off_k, off_va, off_ga, off_gb


def _tile(dim, pref, align=LANES):
    if dim <= pref:
        return dim
    t = pref - pref % align
    while t >= align:
        if dim % t == 0:
            return t
        t -= align
    return dim


def _nbytes(shape, dtype):
    n = 1
    for s in shape:
        n *= s
    return n * jnp.dtype(dtype).itemsize


def _params(sem, blocks, scratch=0):
    need = 2 * sum(_nbytes(s, t) for s, t in blocks) + scratch + VMEM_TEMP_BYTES
    limit = min(max(need, 32 * 1024 * 1024), VMEM_BYTES - 4 * 1024 * 1024)
    return pltpu.CompilerParams(dimension_semantics=sem, vmem_limit_bytes=limit)


def _dot(a, b, kind):
    dn = {"nn": (((1,), (0,)), ((), ())), "nt": (((1,), (1,)), ((), ())), "tn": (((0,), (0,)), ((), ()))}[kind]
    return lax.dot_general(a, b, dn, preferred_element_type=F32)


def _gelu(x):
    c = 0.7978845608028654
    t = jnp.tanh(c * (x + 0.044715 * (x * x * x)))
    return 0.5 * x * (1.0 + t)


def _gelu_grad(x):
    c = 0.7978845608028654
    x2 = x * x
    t = jnp.tanh(c * (x + 0.044715 * (x2 * x)))
    return 0.5 * (1.0 + t) + 0.5 * x * (1.0 - t * t) * (c * (1.0 + 3.0 * 0.044715 * x2))


def _ln_fwd(z, g, b):
    mu = jnp.mean(z, axis=-1, keepdims=True)
    zc = z - mu
    var = jnp.mean(zc * zc, axis=-1, keepdims=True)
    rstd = lax.rsqrt(var + LN_EPS)
    xhat = zc * rstd
    return xhat * g + b, xhat, rstd


def _ln_bwd_vals(dy, xhat, rstd, g):
    gd = dy * g
    m1 = jnp.mean(gd, axis=-1, keepdims=True)
    m2 = jnp.mean(gd * xhat, axis=-1, keepdims=True)
    return rstd * (gd - m1 - xhat * m2)


def _acc_store(ref, val, first):
    @pl.when(first)
    def _():
        ref[...] = val

    @pl.when(jnp.logical_not(first))
    def _():
        ref[...] += val


def _matmul(a, b, kind, *, name, tm=512, tn=1024, tk=2048, out_defs=(("mn", F32),), epilogue=None, extras=(), sched=None):
    if kind == "nn":
        (M, K), (K2, N) = a.shape, b.shape
    elif kind == "nt":
        (M, K), (N, K2) = a.shape, b.shape
    else:
        (K, M), (K2, N) = a.shape, b.shape
    assert K == K2, (a.shape, b.shape, kind)
    tm, tn, tk = _tile(M, tm), _tile(N, tn), _tile(K, tk)
    nk = K // tk
    blocks = []

    def spec(shape, imap, dtype):
        blocks.append((shape, dtype))
        return pl.BlockSpec(shape, imap)

    if kind == "tn":
        a_spec = spec((tk, tm), lambda j, i, k: (k, i), a.dtype)
    else:
        a_spec = spec((tm, tk), lambda j, i, k: (i, k), a.dtype)
    if kind == "nt":
        b_spec = spec((tn, tk), lambda j, i, k: (j, k), b.dtype)
    else:
        b_spec = spec((tk, tn), lambda j, i, k: (k, j), b.dtype)
    in_specs = [a_spec, b_spec]
    operands = [a, b]
    for arr, ekind, off in extras:
        if ekind == "mn":
            assert off % tn == 0
            in_specs.append(spec((tm, tn), lambda j, i, k, o=off // tn: (i, j + o), arr.dtype))
        elif ekind == "row":
            assert off % tn == 0
            in_specs.append(spec((1, tn), lambda j, i, k, o=off // tn: (0, j + o), arr.dtype))
        else:
            in_specs.append(spec((tm, 1), lambda j, i, k: (i, 0), arr.dtype))
        operands.append(arr)
    out_shape, out_specs = [], []
    for od in out_defs:
        okind, dt = od[0], od[1]
        if okind == "mn":
            out_shape.append(jax.ShapeDtypeStruct((M, N), dt))
            out_specs.append(spec((tm, tn), lambda j, i, k: (i, j), dt))
        elif okind == "cols":
            assert od[3] % tn == 0
            out_shape.append(jax.ShapeDtypeStruct((M, od[2]), dt))
            out_specs.append(spec((tm, tn), lambda j, i, k, o=od[3] // tn: (i, j + o), dt))
        elif okind == "m1":
            assert N == tn
            out_shape.append(jax.ShapeDtypeStruct((M, 1), dt))
            out_specs.append(spec((tm, 1), lambda j, i, k: (i, 0), dt))
        else:
            out_shape.append(jax.ShapeDtypeStruct((1, N), F32))
            out_specs.append(spec((1, tn), lambda j, i, k: (0, j), F32))
    n_ex, n_out = len(extras), len(out_defs)
    ep = epilogue if epilogue is not None else (lambda acc: (acc,))
    in_place = nk > 1 and epilogue is None and tuple(out_defs) == (("mn", F32),)

    def body(*refs):
        a_ref, b_ref = refs[0], refs[1]
        ex_refs = refs[2 : 2 + n_ex]
        out_refs = refs[2 + n_ex : 2 + n_ex + n_out]
        i = pl.program_id(1)
        k = pl.program_id(2)

        def product():
            return _dot(a_ref[...], b_ref[...], kind)

        def finish(acc):
            vals = ep(acc, *[r[...] for r in ex_refs])
            for od, r, v in zip(out_defs, out_refs, vals):
                if od[0] == "acc":
                    _acc_store(r, v, i == 0)
                else:
                    r[...] = v.astype(od[1])

        if nk == 1:
            finish(product())
            return
        acc_ref = out_refs[0] if in_place else refs[-1]

        @pl.when(k == 0)
        def _():
            acc_ref[...] = product()

        if in_place:
            @pl.when(k > 0)
            def _():
                acc_ref[...] += product()
        else:
            if nk > 2:
                @pl.when(jnp.logical_and(k > 0, k < nk - 1))
                def _():
                    acc_ref[...] += product()

            @pl.when(k == nk - 1)
            def _():
                finish(acc_ref[...] + product())

    scratch = [pltpu.VMEM((tm, tn), F32)] if nk > 1 and not in_place else []
    outs = _hosted(
        body,
        name=name,
        sched=sched,
        grid=(N // tn, M // tm, nk),
        in_specs=in_specs,
        out_specs=out_specs,
        out_shape=out_shape,
        scratch_shapes=scratch,
        compiler_params=_params(("arbitrary", "arbitrary", "arbitrary"), blocks, _nbytes((tm, tn), F32) if scratch else 0),
    )(*operands)
    return outs if len(outs) > 1 else outs[0]


def _ep_resid_ln(acc, resid, g, b):
    y, xhat, rstd = _ln_fwd(ALPHA * resid + acc, g, b)
    return y, y, xhat, rstd


def _ep_resid(acc, resid):
    return (ALPHA * resid + acc,)


def _ep_resid_ln_bwd(acc, resid, xhat, rstd, g):
    dy = ALPHA * resid + acc
    dz = _ln_bwd_vals(dy, xhat, rstd, g)
    return dz, dz, jnp.sum(dy * xhat, axis=0, keepdims=True), jnp.sum(dy, axis=0, keepdims=True)


def _ep_relu2(acc):
    r = jnp.maximum(acc, 0.0)
    return acc, r * r


def _ep_relu2_bwd(acc, h):
    return (acc * (2.0 * jnp.maximum(h, 0.0)),)


def _ep_gate_bwd(acc, ga, gb, ya, yb, bga, bgb):
    sa = jax.nn.sigmoid(ga + bga)
    sb = jax.nn.sigmoid(gb + bgb)
    dga = acc * ya * (sa * (1.0 - sa))
    dgb = acc * yb * (sb * (1.0 - sb))
    return (acc * sa, acc * sb, dga, dgb, jnp.sum(dga, axis=0, keepdims=True), jnp.sum(dgb, axis=0, keepdims=True))


def _ln_bwd(dy, xhat, rstd, g, *, name, sched=None, target=None):
    T, D = dy.shape
    tm = _tile(T, 256)
    head = target is not None

    def body(*refs):
        dy_ref, xh_ref, rs_ref, g_ref = refs[:4]
        dz_ref, dzb_ref, dg_ref, db_ref = refs[4 + head : 8 + head]
        first = pl.program_id(0) == 0
        dyv, xh = dy_ref[...], xh_ref[...]
        if head:
            err = dyv - refs[4][...]
            dyv = err * (1.0 / D)
            part = 0.5 * jnp.sum(jnp.sum(err * err, axis=1, keepdims=True) * (1.0 / D), axis=0, keepdims=True)
            _acc_store(refs[-1], part, first)
        dz = _ln_bwd_vals(dyv, xh, rs_ref[...], g_ref[...])
        dz_ref[...] = dz
        dzb_ref[...] = dz.astype(BF16)
        _acc_store(dg_ref, jnp.sum(dyv * xh, axis=0, keepdims=True), first)
        _acc_store(db_ref, jnp.sum(dyv, axis=0, keepdims=True), first)

    row = pl.BlockSpec((tm, D), lambda i: (i, 0))
    vec = pl.BlockSpec((1, D), lambda i: (0, 0))
    one = pl.BlockSpec((1, 1), lambda i: (0, 0))
    return _hosted(
        body,
        name=name,
        sched=sched,
        grid=(T // tm,),
        in_specs=[row, row, pl.BlockSpec((tm, 1), lambda i: (i, 0)), vec] + [row] * head,
        out_specs=[row, row, vec, vec] + [one] * head,
        out_shape=[
            jax.ShapeDtypeStruct((T, D), F32),
            jax.ShapeDtypeStruct((T, D), BF16),
            jax.ShapeDtypeStruct((1, D), F32),
            jax.ShapeDtypeStruct((1, D), F32),
        ] + [jax.ShapeDtypeStruct((1, 1), F32)] * head,
        compiler_params=_params(("arbitrary",), [((tm, D), F32)] * (4 + head)),
    )(*([dy, xhat, rstd, g] + [target] * head))


def _masked_mix_weights(wm_ref, G):
    r = lax.broadcasted_iota(jnp.int32, (CHUNK, CHUNK), 0)
    c = lax.broadcasted_iota(jnp.int32, (CHUNK, CHUNK), 1)
    return [jnp.where(c <= r, wm_ref[g], 0.0).astype(BF16) for g in range(G)]


def _gmlp_fwd(d, P, lnv_g, lnv_b, wm, bsT, *, name, sched=None):
    T, GW, G = d.T, d.GW, d.G
    tm = _tile(T, 256)
    nc = tm // CHUNK

    def body(u_ref, v_ref, g_ref, b_ref, wm_ref, bs_ref, o_ref):
        gu = _gelu(u_ref[...])
        vn, _, _ = _ln_fwd(_gelu(v_ref[...]), g_ref[...], b_ref[...])
        vn = vn.astype(BF16)
        wl = _masked_mix_weights(wm_ref, G)
        for c in range(nc):
            rows = slice(c * CHUNK, (c + 1) * CHUNK)
            for g in range(G):
                cols = slice(g * GROUP_DIM, (g + 1) * GROUP_DIM)
                mixed = _dot(wl[g], vn[rows, cols], "nn") + bs_ref[:, g : g + 1]
                o_ref[rows, cols] = (gu[rows, cols] * mixed).astype(BF16)

    return _hosted(
        body,
        name=name,
        sched=sched,
        grid=(T // tm,),
        in_specs=[
            pl.BlockSpec((tm, GW), lambda i: (i, 0)),
            pl.BlockSpec((tm, GW), lambda i: (i, 1)),
            pl.BlockSpec((1, GW), lambda i: (0, 0)),
            pl.BlockSpec((1, GW), lambda i: (0, 0)),
            pl.BlockSpec((G, CHUNK, CHUNK), lambda i: (0, 0, 0)),
            pl.BlockSpec((CHUNK, G), lambda i: (0, 0)),
        ],
        out_specs=pl.BlockSpec((tm, GW), lambda i: (i, 0)),
        out_shape=jax.ShapeDtypeStruct((T, GW), BF16),
        compiler_params=_params(("arbitrary",), [((tm, GW), F32)] * 3),
    )(P, P, lnv_g, lnv_b, wm, bsT)


def _gmlp_bwd(d, P, dya_b, Ga, lnv_g, lnv_b, wm, bsT, dP, *, name, sched=None):
    T, D, GW, G = d.T, d.D, d.GW, d.G
    tm = _tile(T, 256)
    nc = tm // CHUNK

    def body(u_ref, v_ref, dya_ref, ga_ref, g_ref, b_ref, wm_ref, bs_ref, dp_in, duv_ref, dwm_ref, dbs_ref, dlg_ref, dlb_ref, dgu_s, dvn_s):
        first = pl.program_id(0) == 0
        dsgu = _dot(dya_ref[...], ga_ref[...], "nn")
        u, v = u_ref[...], v_ref[...]
        gu = _gelu(u)
        lng = g_ref[...]
        vn, xh, rstd = _ln_fwd(_gelu(v), lng, b_ref[...])
        vn = vn.astype(BF16)
        wl = _masked_mix_weights(wm_ref, G)
        r = lax.broadcasted_iota(jnp.int32, (CHUNK, CHUNK), 0)
        cc = lax.broadcasted_iota(jnp.int32, (CHUNK, CHUNK), 1)
        lane_g = lax.broadcasted_iota(jnp.int32, (CHUNK, G), 1)
        dbs = jnp.zeros((CHUNK, G), F32)
        for g in range(G):
            cols = slice(g * GROUP_DIM, (g + 1) * GROUP_DIM)
            dw = jnp.zeros((CHUNK, CHUNK), F32)
            for c in range(nc):
                rows = slice(c * CHUNK, (c + 1) * CHUNK)
                mixed = _dot(wl[g], vn[rows, cols], "nn") + bs_ref[:, g : g + 1]
                ds = dsgu[rows, cols]
                dgu_s[rows, cols] = ds * mixed
                dmx = ds * gu[rows, cols]
                dbs = dbs + jnp.where(lane_g == g, jnp.sum(dmx, axis=1, keepdims=True), 0.0)
                dmx = dmx.astype(BF16)
                dw = dw + _dot(dmx, vn[rows, cols], "nt")
                dvn_s[rows, cols] = _dot(wl[g], dmx, "tn")
            _acc_store(dwm_ref.at[g], jnp.where(cc <= r, dw, 0.0), first)
        _acc_store(dbs_ref, dbs, first)
        dvn = dvn_s[...]
        _acc_store(dlg_ref, jnp.sum(dvn * xh, axis=0, keepdims=True), first)
        _acc_store(dlb_ref, jnp.sum(dvn, axis=0, keepdims=True), first)
        dgv = _ln_bwd_vals(dvn, xh, rstd, lng)
        duv_ref[:, :GW] = (dgu_s[...] * _gelu_grad(u)).astype(BF16)
        duv_ref[:, GW:] = (dgv * _gelu_grad(v)).astype(BF16)

    return _hosted(
        body,
        name=name,
        sched=sched,
        grid=(T // tm,),
        in_specs=[
            pl.BlockSpec((tm, GW), lambda i: (i, 0)),
            pl.BlockSpec((tm, GW), lambda i: (i, 1)),
            pl.BlockSpec((tm, D), lambda i: (i, 0)),
            pl.BlockSpec((D, GW), lambda i: (0, 0)),
            pl.BlockSpec((1, GW), lambda i: (0, 0)),
            pl.BlockSpec((1, GW), lambda i: (0, 0)),
            pl.BlockSpec((G, CHUNK, CHUNK), lambda i: (0, 0, 0)),
            pl.BlockSpec((CHUNK, G), lambda i: (0, 0)),
            pl.BlockSpec(memory_space=pl.ANY),
        ],
        out_specs=[
            pl.BlockSpec((tm, 2 * GW), lambda i: (i, 0)),
            pl.BlockSpec((G, CHUNK, CHUNK), lambda i: (0, 0, 0)),
            pl.BlockSpec((CHUNK, G), lambda i: (0, 0)),
            pl.BlockSpec((1, GW), lambda i: (0, 0)),
            pl.BlockSpec((1, GW), lambda i: (0, 0)),
        ],
        out_shape=[
            jax.ShapeDtypeStruct(dP.shape, BF16),
            jax.ShapeDtypeStruct((G, CHUNK, CHUNK), F32),
            jax.ShapeDtypeStruct((CHUNK, G), F32),
            jax.ShapeDtypeStruct((1, GW), F32),
            jax.ShapeDtypeStruct((1, GW), F32),
        ],
        scratch_shapes=[pltpu.VMEM((tm, GW), F32), pltpu.VMEM((tm, GW), F32)],
        input_output_aliases={8: 0},
        compiler_params=_params(
            ("arbitrary",), [((tm, GW), F32)] * 3 + [((tm, D), BF16), ((D, GW), BF16)], 2 * _nbytes((tm, GW), F32)
        ),
    )(P, P, dya_b, Ga, lnv_g, lnv_b, wm, bsT, dP)


def _swap_halves(x):
    w = x.shape[1]
    lane = lax.broadcasted_iota(jnp.int32, x.shape, 1)
    return jnp.where((lane % HEAD_DIM) < ROPE_HALF, pltpu.roll(x, w - ROPE_HALF, 1), pltpu.roll(x, ROPE_HALF, 1))


def _lane_tile(t, width):
    reps = width // LANES
    return t if reps == 1 else jnp.tile(t, (1, reps))


def _rope(x, cos2, sin2):
    w = x.shape[1]
    return x * _lane_tile(cos2, w) + _swap_halves(x) * _lane_tile(sin2, w)


def _rope_bwd(g, cos2, sin2):
    w = g.shape[1]
    return g * _lane_tile(cos2, w) - _swap_halves(g) * _lane_tile(sin2, w)


PAIR = LANES // HEAD_DIM


def _swa_valid(i):
    s = lax.broadcasted_iota(jnp.int32, (2 * CHUNK, CHUNK), 0)
    t = lax.broadcasted_iota(jnp.int32, (2 * CHUNK, CHUNK), 1)
    cur = jnp.logical_and(s >= CHUNK, s - CHUNK <= t)
    prev = jnp.logical_and(jnp.logical_and(s < CHUNK, s > t), i > 0)
    return jnp.logical_or(cur, prev)


def _half_variants(x, odd):
    lane = lax.broadcasted_iota(jnp.int32, x.shape, 1)
    if odd:
        hi = jnp.where(lane >= HEAD_DIM, x, jnp.zeros_like(x))
        return pltpu.roll(hi, HEAD_DIM, 1), hi
    lo = jnp.where(lane < HEAD_DIM, x, jnp.zeros_like(x))
    return lo, pltpu.roll(lo, HEAD_DIM, 1)


def _swa_probs_t(kx, qp, sink, valid):
    s = jnp.where(valid, _dot(kx, qp, "nt") * (HEAD_DIM**-0.5), -jnp.inf)
    m = jnp.maximum(jnp.max(s, axis=0, keepdims=True), sink)
    e = jnp.exp(s - m)
    e_s = jnp.exp(sink - m)
    den = jnp.sum(e, axis=0, keepdims=True) + e_s
    return e / den, e_s / den


def _swa_load(q_ref, kc_ref, kp_ref, vc_ref, vp_ref, cc, sc, cp, sp):
    qr = _rope(q_ref[...], cc, sc).astype(BF16)
    kcat = jnp.concatenate([_rope(kp_ref[...], cp, sp), _rope(kc_ref[...], cc, sc)], axis=0)
    vcat = jnp.concatenate([vp_ref[...], vc_ref[...]], axis=0)
    return qr, kcat, vcat


def _swa_specs(d):
    _, off_q, off_k, off_va, _, _ = _offsets(d)
    assert off_q % d.AW == 0 and off_k % d.KVW == 0 and off_va % d.KVW == 0
    prev = lambda i: jnp.maximum(i - 1, 0)
    return [
        pl.BlockSpec(memory_space=pltpu.SMEM),
        pl.BlockSpec((CHUNK, d.AW), lambda i, o=off_q // d.AW: (i, o)),
        pl.BlockSpec((CHUNK, d.KVW), lambda i, o=off_k // d.KVW: (i, o)),
        pl.BlockSpec((CHUNK, d.KVW), lambda i, o=off_k // d.KVW: (prev(i), o)),
        pl.BlockSpec((CHUNK, d.KVW), lambda i, o=off_va // d.KVW: (i, o)),
        pl.BlockSpec((CHUNK, d.KVW), lambda i, o=off_va // d.KVW: (prev(i), o)),
        pl.BlockSpec((CHUNK, LANES), lambda i: (i, 0)),
        pl.BlockSpec((CHUNK, LANES), lambda i: (i, 0)),
        pl.BlockSpec((CHUNK, LANES), lambda i: (prev(i), 0)),
        pl.BlockSpec((CHUNK, LANES), lambda i: (prev(i), 0)),
    ]


def _swa_fwd(d, P, cos2, sin2, sinks, *, name, sched=None):
    T, AW, HQ, HKV = d.T, d.AW, d.HQ, d.HKV
    grp = HQ // HKV
    assert grp % PAIR == 0 and HKV % PAIR == 0

    def body(sink_ref, q_ref, kc_ref, kp_ref, vc_ref, vp_ref, cc_ref, sc_ref, cp_ref, sp_ref, o_ref):
        i = pl.program_id(0)
        qr, kcat, vcat = _swa_load(q_ref, kc_ref, kp_ref, vc_ref, vp_ref, cc_ref[...], sc_ref[...], cp_ref[...], sp_ref[...])
        valid = _swa_valid(i)
        for kv in range(HKV):
            col = slice((kv // PAIR) * LANES, (kv // PAIR + 1) * LANES)
            kx = [t.astype(BF16) for t in _half_variants(kcat[:, col], kv % PAIR)]
            vx = [t.astype(BF16) for t in _half_variants(vcat[:, col], kv % PAIR)]
            for pp in range(grp // PAIR):
                p = kv * (grp // PAIR) + pp
                qs = slice(p * LANES, (p + 1) * LANES)
                out = jnp.zeros((CHUNK, LANES), F32)
                for half in range(PAIR):
                    pt, _ = _swa_probs_t(kx[half], qr[:, qs], sink_ref[PAIR * p + half], valid)
                    out = out + _dot(pt.astype(BF16), vx[half], "tn")
                o_ref[:, qs] = out.astype(BF16)

    return _hosted(
        body,
        name=name,
        sched=sched,
        grid=(T // CHUNK,),
        in_specs=_swa_specs(d),
        out_specs=pl.BlockSpec((CHUNK, AW), lambda i: (i, 0)),
        out_shape=jax.ShapeDtypeStruct((T, AW), BF16),
        compiler_params=_params(("arbitrary",), [((CHUNK, AW), F32)] * 3),
    )(sinks, P, P, P, P, P, cos2, sin2, cos2, sin2)


def _swa_bwd(d, P, dyb_b, Gb, cos2, sin2, sinks, dP, *, name, sched=None):
    T, D, AW, KVW, HQ, HKV = d.T, d.D, d.AW, d.KVW, d.HQ, d.HKV
    grp = HQ // HKV
    nb = T // CHUNK
    scale = HEAD_DIM**-0.5
    off_q = _offsets(d)[1]
    assert grp % PAIR == 0 and HKV % PAIR == 0 and off_q % AW == 0

    def body(sink_ref, q_ref, kc_ref, kp_ref, vc_ref, vp_ref, cc_ref, sc_ref, cp_ref, sp_ref, dyb_ref, gb_ref, dp_in,
             dq_ref, dkv_ref, dsk_ref, acc_s, dq_s, dk_s, dv_s):
        i = pl.program_id(0)
        cc, sc, cp, sp = cc_ref[...], sc_ref[...], cp_ref[...], sp_ref[...]
        datt = _dot(dyb_ref[...], gb_ref[...], "nn").astype(BF16)
        qr, kcat, vcat = _swa_load(q_ref, kc_ref, kp_ref, vc_ref, vp_ref, cc, sc, cp, sp)
        valid = _swa_valid(i)
        lane1 = lax.broadcasted_iota(jnp.int32, (1, LANES), 1)
        lane2 = lax.broadcasted_iota(jnp.int32, (2 * CHUNK, LANES), 1)
        dsk = jnp.zeros((1, LANES), F32)
        for kvp in range(HKV // PAIR):
            col = slice(kvp * LANES, (kvp + 1) * LANES)
            dk_col = jnp.zeros((2 * CHUNK, LANES), F32)
            dv_col = jnp.zeros((2 * CHUNK, LANES), F32)
            for odd in range(PAIR):
                kv = kvp * PAIR + odd
                kx = [t.astype(BF16) for t in _half_variants(kcat[:, col], odd)]
                vx = [t.astype(BF16) for t in _half_variants(vcat[:, col], odd)]
                dk_half = [jnp.zeros((2 * CHUNK, LANES), F32) for _ in range(PAIR)]
                dv_half = [jnp.zeros((2 * CHUNK, LANES), F32) for _ in range(PAIR)]
                for pp in range(grp // PAIR):
                    p = kv * (grp // PAIR) + pp
                    qs = slice(p * LANES, (p + 1) * LANES)
                    qp, dop = qr[:, qs], datt[:, qs]
                    dq = jnp.zeros((CHUNK, LANES), F32)
                    for half in range(PAIR):
                        h = PAIR * p + half
                        pt, p_s = _swa_probs_t(kx[half], qp, sink_ref[h], valid)
                        dpt = _dot(vx[half], dop, "nt")
                        rs = jnp.sum(pt * dpt, axis=0, keepdims=True)
                        dst = (pt * (dpt - rs) * scale).astype(BF16)
                        dsk = dsk + jnp.where(lane1 == h, -jnp.sum(p_s * rs, axis=1, keepdims=True), 0.0)
                        dq = dq + _dot(dst, kx[half], "tn")
                        dk_half[half] = dk_half[half] + _dot(dst, qp, "nn")
                        dv_half[half] = dv_half[half] + _dot(pt.astype(BF16), dop, "nn")
                    dq_s[:, qs] = dq
                for acc_half, is_k in ((dk_half, True), (dv_half, False)):
                    lo = jnp.where(lane2 < HEAD_DIM, acc_half[0], 0.0)
                    hi = jnp.where(lane2 >= HEAD_DIM, acc_half[1], 0.0)
                    both = (pltpu.roll(lo, HEAD_DIM, 1) + hi) if odd else (lo + pltpu.roll(hi, HEAD_DIM, 1))
                    if is_k:
                        dk_col = dk_col + both
                    else:
                        dv_col = dv_col + both
            dk_s[:, col] = dk_col
            dv_s[:, col] = dv_col
        dq_ref[...] = _rope_bwd(dq_s[...], cc, sc).astype(BF16)
        cur = pl.ds(pl.multiple_of(i * CHUNK, CHUNK), CHUNK)
        acc_s[cur, :KVW] = _rope_bwd(dk_s[CHUNK:, :], cc, sc)
        acc_s[cur, KVW:] = dv_s[CHUNK:, :]

        @pl.when(i > 0)
        def _():
            prv = pl.ds(pl.multiple_of((i - 1) * CHUNK, CHUNK), CHUNK)
            acc_s[prv, :KVW] += _rope_bwd(dk_s[:CHUNK, :], cp, sp)
            acc_s[prv, KVW:] += dv_s[:CHUNK, :]

        _acc_store(dsk_ref, dsk, i == 0)

        @pl.when(i == nb - 1)
        def _():
            dkv_ref[...] = acc_s[...].astype(BF16)

    return _hosted(
        body,
        name=name,
        sched=sched,
        grid=(nb,),
        in_specs=_swa_specs(d) + [pl.BlockSpec((CHUNK, D), lambda i: (i, 0)), pl.BlockSpec((D, AW), lambda i: (0, 0)),
                                   pl.BlockSpec(memory_space=pl.ANY)],
        out_specs=[
            pl.BlockSpec((CHUNK, AW), lambda i, o=off_q // AW: (i, o)),
            pl.BlockSpec((T, 2 * KVW), lambda i: (0, 0)),
            pl.BlockSpec((1, LANES), lambda i: (0, 0)),
        ],
        out_shape=[
            jax.ShapeDtypeStruct(dP.shape, BF16),
            jax.ShapeDtypeStruct((T, 2 * KVW), BF16),
            jax.ShapeDtypeStruct((1, LANES), F32),
        ],
        input_output_aliases={12: 0},
        scratch_shapes=[
            pltpu.VMEM((T, 2 * KVW), F32),
            pltpu.VMEM((CHUNK, AW), F32),
            pltpu.VMEM((2 * CHUNK, KVW), F32),
            pltpu.VMEM((2 * CHUNK, KVW), F32),
        ],
        compiler_params=_params(
            ("arbitrary",), [((CHUNK, AW), F32)] * 3 + [((D, AW), BF16), ((T, 2 * KVW), BF16)], _nbytes((T, 2 * KVW), F32)
        ),
    )(sinks, P, P, P, P, P, cos2, sin2, cos2, sin2, dyb_b, Gb, dP)


def _place_cols(dst, src, off, *, name, sched=None):
    T, w = src.shape
    tm, tw = _tile(T, 512), _tile(w, 512)
    assert off % tw == 0

    def body(src_ref, dst_in, out_ref):
        out_ref[...] = src_ref[...]

    return _hosted(
        body,
        name=name,
        sched=sched,
        grid=(T // tm, w // tw),
        in_specs=[pl.BlockSpec((tm, tw), lambda i, j: (i, j)), pl.BlockSpec(memory_space=pl.ANY)],
        out_specs=pl.BlockSpec((tm, tw), lambda i, j, o=off // tw: (i, j + o)),
        out_shape=jax.ShapeDtypeStruct(dst.shape, dst.dtype),
        input_output_aliases={1: 0},
        compiler_params=_params(("arbitrary", "arbitrary"), [((tm, tw), dst.dtype)] * 2),
    )(src, dst)


def _branch_merge(d, sgu, att, Ga, Gb, P, b_gate, *, name, sched=None):
    T, D, GW, AW = d.T, d.D, d.GW, d.AW
    _, _, _, _, off_ga, off_gb = _offsets(d)
    tm, tn = _tile(T, 1024), _tile(D, 512)
    assert off_ga % tn == 0 and off_gb % tn == 0

    def body(s_ref, a_ref, ga_w, gb_w, ga_ref, gb_ref, bga_ref, bgb_ref, ya_ref, yb_ref, mg_ref):
        ya = _dot(s_ref[...], ga_w[...], "nt")
        yb = _dot(a_ref[...], gb_w[...], "nt")
        ya_ref[...] = ya
        yb_ref[...] = yb
        sa = jax.nn.sigmoid(ga_ref[...] + bga_ref[...])
        sb = jax.nn.sigmoid(gb_ref[...] + bgb_ref[...])
        mg_ref[...] = (sa * ya + sb * yb).astype(BF16)

    tile = pl.BlockSpec((tm, tn), lambda j, i: (i, j))
    return _hosted(
        body,
        name=name,
        sched=sched,
        grid=(D // tn, T // tm),
        in_specs=[
            pl.BlockSpec((tm, GW), lambda j, i: (i, 0)),
            pl.BlockSpec((tm, AW), lambda j, i: (i, 0)),
            pl.BlockSpec((tn, GW), lambda j, i: (j, 0)),
            pl.BlockSpec((tn, AW), lambda j, i: (j, 0)),
            pl.BlockSpec((tm, tn), lambda j, i, o=off_ga // tn: (i, j + o)),
            pl.BlockSpec((tm, tn), lambda j, i, o=off_gb // tn: (i, j + o)),
            pl.BlockSpec((1, tn), lambda j, i: (0, j)),
            pl.BlockSpec((1, tn), lambda j, i, o=D // tn: (0, j + o)),
        ],
        out_specs=[tile, tile, tile],
        out_shape=[jax.ShapeDtypeStruct((T, D), F32), jax.ShapeDtypeStruct((T, D), F32), jax.ShapeDtypeStruct((T, D), BF16)],
        compiler_params=_params(
            ("arbitrary", "arbitrary"),
            [((tm, GW), BF16), ((tm, AW), BF16), ((tn, GW), BF16), ((tn, AW), BF16)] + [((tm, tn), F32)] * 5,
        ),
    )(sgu, att, Ga, Gb, P, P, b_gate, b_gate)


def _xattn_probs(qh, kh):
    s = _dot(qh, kh, "nt") * (X_HEAD_DIM**-0.5)
    e = jnp.exp(s - jnp.max(s, axis=1, keepdims=True))
    return e / jnp.sum(e, axis=1, keepdims=True)


def _xattn_fwd(d, X, Xb, kv, Gxq, Gxo, ln_g, ln_b, *, name, sched=None):
    T, D, XW, XH, MEM = d.T, d.D, d.XW, d.XH, d.MEM
    tm = _tile(T, 256)

    def body(x_ref, xb_ref, kv_ref, wq_ref, wo_ref, g_ref, b_ref, y_ref, yb_ref, xh_ref, rs_ref, q_ref, o_ref, o_s):
        q = _dot(xb_ref[...], wq_ref[...], "nn").astype(BF16)
        q_ref[...] = q
        for h in range(XH):
            hs = slice(h * X_HEAD_DIM, (h + 1) * X_HEAD_DIM)
            p = _xattn_probs(q[:, hs], kv_ref[:, hs]).astype(BF16)
            o_s[:, hs] = _dot(p, kv_ref[:, XW + h * X_HEAD_DIM : XW + (h + 1) * X_HEAD_DIM], "nn").astype(BF16)
        o = o_s[...]
        o_ref[...] = o
        y, xhat, rstd = _ln_fwd(ALPHA * x_ref[...] + _dot(o, wo_ref[...], "nt"), g_ref[...], b_ref[...])
        y_ref[...] = y
        yb_ref[...] = y.astype(BF16)
        xh_ref[...] = xhat
        rs_ref[...] = rstd

    row = pl.BlockSpec((tm, D), lambda i: (i, 0))
    nar = pl.BlockSpec((tm, XW), lambda i: (i, 0))
    vec = pl.BlockSpec((1, D), lambda i: (0, 0))
    return _hosted(
        body,
        name=name,
        sched=sched,
        grid=(T // tm,),
        in_specs=[
            row,
            row,
            pl.BlockSpec((MEM, 2 * XW), lambda i: (0, 0)),
            pl.BlockSpec((D, XW), lambda i: (0, 0)),
            pl.BlockSpec((D, XW), lambda i: (0, 0)),
            vec,
            vec,
        ],
        out_specs=[row, row, row, pl.BlockSpec((tm, 1), lambda i: (i, 0)), nar, nar],
        out_shape=[
            jax.ShapeDtypeStruct((T, D), F32),
            jax.ShapeDtypeStruct((T, D), BF16),
            jax.ShapeDtypeStruct((T, D), F32),
            jax.ShapeDtypeStruct((T, 1), F32),
            jax.ShapeDtypeStruct((T, XW), BF16),
            jax.ShapeDtypeStruct((T, XW), BF16),
        ],
        scratch_shapes=[pltpu.VMEM((tm, XW), BF16)],
        compiler_params=_params(("arbitrary",), [((tm, D), F32)] * 4 + [((D, XW), BF16)] * 2),
    )(X, Xb, kv, Gxq, Gxo, ln_g, ln_b)


def _xattn_bwd(d, dz, dzb, q_b, kv, Gxq, Gxo, xhat, rstd, ln_g, *, name, sched=None):
    T, D, XW, XH, MEM = d.T, d.D, d.XW, d.XH, d.MEM
    tm = _tile(T, 256)
    scale = X_HEAD_DIM**-0.5

    def body(dz_ref, dzb_ref, q_ref, kv_ref, wq_ref, wo_ref, xh_ref, rs_ref, g_ref, dx_ref, dxb_ref, dg_ref, db_ref, dq_ref, dkv_ref, dq_s):
        first = pl.program_id(0) == 0
        do = _dot(dzb_ref[...], wo_ref[...], "nn").astype(BF16)
        for h in range(XH):
            hs = slice(h * X_HEAD_DIM, (h + 1) * X_HEAD_DIM)
            vs = slice(XW + h * X_HEAD_DIM, XW + (h + 1) * X_HEAD_DIM)
            kh, vh, qh, doh = kv_ref[:, hs], kv_ref[:, vs], q_ref[:, hs], do[:, hs]
            p = _xattn_probs(qh, kh)
            dp = _dot(doh, vh, "nt")
            ds = (p * (dp - jnp.sum(p * dp, axis=1, keepdims=True)) * scale).astype(BF16)
            dq_s[:, hs] = _dot(ds, kh, "nn").astype(BF16)
            _acc_store(dkv_ref.at[h], _dot(ds, qh, "tn"), first)
            _acc_store(dkv_ref.at[XH + h], _dot(p.astype(BF16), doh, "tn"), first)
        dq = dq_s[...]
        dq_ref[...] = dq
        dy = ALPHA * dz_ref[...] + _dot(dq, wq_ref[...], "nt")
        xh = xh_ref[...]
        dz_in = _ln_bwd_vals(dy, xh, rs_ref[...], g_ref[...])
        dx_ref[...] = dz_in
        dxb_ref[...] = dz_in.astype(BF16)
        _acc_store(dg_ref, jnp.sum(dy * xh, axis=0, keepdims=True), first)
        _acc_store(db_ref, jnp.sum(dy, axis=0, keepdims=True), first)

    row = pl.BlockSpec((tm, D), lambda i: (i, 0))
    nar = pl.BlockSpec((tm, XW), lambda i: (i, 0))
    vec = pl.BlockSpec((1, D), lambda i: (0, 0))
    return _hosted(
        body,
        name=name,
        sched=sched,
        grid=(T // tm,),
        in_specs=[
            row,
            row,
            nar,
            pl.BlockSpec((MEM, 2 * XW), lambda i: (0, 0)),
            pl.BlockSpec((D, XW), lambda i: (0, 0)),
            pl.BlockSpec((D, XW), lambda i: (0, 0)),
            row,
            pl.BlockSpec((tm, 1), lambda i: (i, 0)),
            vec,
        ],
        out_specs=[row, row, vec, vec, nar, pl.BlockSpec((2 * XH, MEM, X_HEAD_DIM), lambda i: (0, 0, 0))],
        out_shape=[
            jax.ShapeDtypeStruct((T, D), F32),
            jax.ShapeDtypeStruct((T, D), BF16),
            jax.ShapeDtypeStruct((1, D), F32),
            jax.ShapeDtypeStruct((1, D), F32),
            jax.ShapeDtypeStruct((T, XW), BF16),
            jax.ShapeDtypeStruct((2 * XH, MEM, X_HEAD_DIM), F32),
        ],
        scratch_shapes=[pltpu.VMEM((tm, XW), BF16)],
        compiler_params=_params(("arbitrary",), [((tm, D), F32)] * 5 + [((D, XW), BF16)] * 2),
    )(dz, dzb, q_b, kv, Gxq, Gxo, xhat, rstd, ln_g)


def _resid_ln_outs():
    return (("mn", F32), ("mn", BF16), ("mn", F32), ("m1", F32))


def _layer_fwd(d, X, Xb, memb, G, S, cos2, sin2, tag, sched=None):
    D = d.D
    mm = functools.partial(_matmul, sched=sched)
    P = mm(Xb, G["in"], "nt", name=f"proj_{tag}", tm=1024, tn=_tile(d.IN, 1280))
    sgu = _gmlp_fwd(d, P, S["ln_v_g"], S["ln_v_b"], S["w_s"], S["b_sT"], name=f"gmlp_fwd_{tag}", sched=sched)
    att = _swa_fwd(d, P, cos2, sin2, S["sinks"], name=f"swa_fwd_{tag}", sched=sched)
    ya, yb, merged = _branch_merge(d, sgu, att, G["a"], G["b"], P, S["b_gate"], name=f"merge_{tag}", sched=sched)
    X1, X1b, xh1, rs1 = mm(
        merged, G["o"], "nn", name=f"oproj_ln_{tag}", tn=D, tk=1024, out_defs=_resid_ln_outs(), epilogue=_ep_resid_ln,
        extras=((X, "mn", 0), (S["ln1_g"], "row", 0), (S["ln1_b"], "row", 0)),
    )
    kv = mm(memb, G["xkv"], "nn", name=f"xkv_{tag}", out_defs=(("mn", BF16),))
    X2, X2b, xh2, rs2, q_b, o_b = _xattn_fwd(d, X1, X1b, kv, G["xq"], G["xo"], S["ln2_g"], S["ln2_b"], name=f"xattn_fwd_{tag}", sched=sched)
    H, A = mm(X2b, G["up"], "nt", name=f"up_{tag}", tm=1024, out_defs=(("mn", F32), ("mn", BF16)), epilogue=_ep_relu2)
    X3, X3b, xh3, rs3 = mm(
        A, G["down"], "nn", name=f"down_ln_{tag}", tn=D, tk=1024, out_defs=_resid_ln_outs(), epilogue=_ep_resid_ln,
        extras=((X2, "mn", 0), (S["ln3_g"], "row", 0), (S["ln3_b"], "row", 0)),
    )
    saved = dict(Xb=Xb, P=P, sgu=sgu, att=att, ya=ya, yb=yb, merged=merged, X1b=X1b, xh1=xh1, rs1=rs1, kv=kv, q_b=q_b,
                 o_b=o_b, X2b=X2b, xh2=xh2, rs2=rs2, H=H, A=A, xh3=xh3, rs3=rs3)
    return X3, X3b, saved


def _layer_bwd(d, dXout, sv, memb, G, S, cos2, sin2, tag, sched=None, on_grad=None, target=None):
    D = d.D
    mm = functools.partial(_matmul, sched=sched)
    emit = on_grad if on_grad is not None else (lambda n, g: None)
    wide = dict(tn=D)
    _, _, off_k, _, off_ga, off_gb = _offsets(d)
    bf = (("mn", BF16),)
    dz3, dz3b, dg3, db3, *loss = _ln_bwd(dXout, sv["xh3"], sv["rs3"], S["ln3_g"], name=f"ln3_bwd_{tag}", sched=sched, target=target)
    dHb = mm(dz3b, G["down"], "nt", name=f"down_dx_{tag}", tm=1024, out_defs=bf, epilogue=_ep_relu2_bwd, extras=((sv["H"], "mn", 0),))
    g_down = mm(sv["A"], dz3b, "tn", name=f"down_dw_{tag}", **wide)
    emit("down", g_down)
    ln_bwd_outs = (("mn", F32), ("mn", BF16), ("acc", F32), ("acc", F32))
    dz2, dz2b, dg2, db2 = mm(
        dHb, G["up"], "nn", name=f"up_dx_{tag}", tn=D, tk=1024, out_defs=ln_bwd_outs, epilogue=_ep_resid_ln_bwd,
        extras=((dz3, "mn", 0), (sv["xh2"], "mn", 0), (sv["rs2"], "m1", 0), (S["ln2_g"], "row", 0)),
    )
    g_up = mm(dHb, sv["X2b"], "tn", name=f"up_dw_{tag}", **wide)
    emit("up", g_up)
    dz1, dz1b, dg1, db1, dq_b, dkv = _xattn_bwd(d, dz2, dz2b, sv["q_b"], sv["kv"], G["xq"], G["xo"], sv["xh1"], sv["rs1"], S["ln1_g"],
                                               name=f"xattn_bwd_{tag}", sched=sched)
    g_xo = mm(dz2b, sv["o_b"], "tn", name=f"xo_dw_{tag}")
    emit("xo", g_xo)
    g_xq = mm(sv["X1b"], dq_b, "tn", name=f"xq_dw_{tag}")
    emit("xq", g_xq)
    dkv_b = dkv.transpose(1, 0, 2).reshape(d.MEM, 2 * d.XW).astype(BF16)
    g_xkv = mm(memb, dkv_b, "tn", name=f"xkv_dw_{tag}")
    emit("xkv", g_xkv)
    dya_b, dyb_b, dP, dgb_b, dbga, dbgb = mm(
        dz1b, G["o"], "nt", name=f"oproj_dx_{tag}", tm=1024, tn=512,
        out_defs=(("mn", BF16), ("mn", BF16), ("cols", BF16, d.IN, off_ga), ("mn", BF16), ("acc", F32), ("acc", F32)),
        epilogue=_ep_gate_bwd,
        extras=((sv["P"], "mn", off_ga), (sv["P"], "mn", off_gb), (sv["ya"], "mn", 0), (sv["yb"], "mn", 0),
                (S["b_gate"], "row", 0), (S["b_gate"], "row", D)),
    )
    g_o = mm(sv["merged"], dz1b, "tn", name=f"oproj_dw_{tag}", **wide)
    emit("o", g_o)
    dP = _place_cols(dP, dgb_b, off_gb, name=f"place_dgb_{tag}")
    dP, dwm, dbsT, dlvg, dlvb = _gmlp_bwd(d, sv["P"], dya_b, G["a"], S["ln_v_g"], S["ln_v_b"], S["w_s"], S["b_sT"], dP, name=f"gmlp_bwd_{tag}", sched=sched)
    g_a = mm(dya_b, sv["sgu"], "tn", name=f"bra_dw_{tag}")
    emit("a", g_a)
    dP, dkvr_b, dsk = _swa_bwd(d, sv["P"], dyb_b, G["b"], cos2, sin2, S["sinks"], dP, name=f"swa_bwd_{tag}", sched=sched)
    g_b = mm(dyb_b, sv["att"], "tn", name=f"brb_dw_{tag}")
    emit("b", g_b)
    dP = _place_cols(dP, dkvr_b, off_k, name=f"place_dkv_{tag}")
    small = dict(b_gate=jnp.concatenate([dbga, dbgb], axis=1), ln_v_g=dlvg, ln_v_b=dlvb, w_s=dwm, b_sT=dbsT, sinks=dsk,
                 ln1_g=dg1, ln1_b=db1, ln2_g=dg2, ln2_b=db2, ln3_g=dg3, ln3_b=db3)
    emit("small", small)
    g_in = mm(dP, sv["Xb"], "tn", name=f"proj_dw_{tag}", **wide)
    emit("in", g_in)
    dXin = mm(dP, G["in"], "nn", name=f"proj_dx_{tag}", tn=D, tk=_tile(d.IN, 1920), epilogue=_ep_resid, extras=((dz1, "mn", 0),))
    big = {"in": g_in, "a": g_a, "b": g_b, "o": g_o, "xq": g_xq, "xkv": g_xkv, "xo": g_xo, "up": g_up, "down": g_down}
    return dXin, big, small, (loss[0] if loss else None)


def _rope_tables(T):
    inv = 1.0 / (ROPE_THETA ** (jnp.arange(0, HEAD_DIM, 2, dtype=F32) / HEAD_DIM))
    ang = jnp.arange(T, dtype=F32)[:, None] * inv[None, :]
    cos, sin = jnp.cos(ang), jnp.sin(ang)
    reps = LANES // HEAD_DIM
    return jnp.concatenate([cos, cos] * reps, axis=1), jnp.concatenate([-sin, sin] * reps, axis=1)


def _local_step(d, x, mem, target, Gs, Ss, sched=None, on_grad=None):
    cos2, sin2 = _rope_tables(d.T)
    memb = mem.astype(BF16)
    X, Xb = x, x.astype(BF16)
    saved = []
    for l in range(DEPTH):
        X, Xb, sv = _layer_fwd(d, X, Xb, memb, Gs[l], Ss[l], cos2, sin2, f"l{l}", sched)
        saved.append(sv)
    bigs, smalls = [None] * DEPTH, [None] * DEPTH
    dX, loss = X, None
    for l in reversed(range(DEPTH)):
        emit = None if on_grad is None else functools.partial(on_grad, l)
        head = target if l == DEPTH - 1 else None
        dX, bigs[l], smalls[l], got = _layer_bwd(d, dX, saved[l], memb, Gs[l], Ss[l], cos2, sin2, f"l{l}", sched, emit, head)
        loss = got if got is not None else loss
    return loss, dX, bigs, smalls


def _place():
    x, y, c = lax.axis_index("x"), lax.axis_index("y"), lax.axis_index("c")
    return x, y, c, [(1 - x, y), (x, 1 - y), (1 - x, 1 - y)]


def _all_gather(shards, *, name, sched=None):
    n = len(shards)
    any_spec = pl.BlockSpec(memory_space=pl.ANY)

    def body(*refs):
        ins, outs = refs[:n], refs[n : 2 * n]
        send_sems, recv_sems, local_sems = refs[2 * n :]
        x, y, c, chips = _place()
        me, sibling = (x, y, c), (x, y, 1 - c)

        def rows(w, p):
            r = ins[w].shape[0]
            return outs[w].at[pl.ds((4 * p[0] + 2 * p[1] + p[2]) * r, r), :]

        def copy(w, k, block, to, src=None):
            return pltpu.make_async_remote_copy(
                src_ref=rows(w, block) if src is None else src, dst_ref=rows(w, block),
                send_sem=send_sems.at[7 * w + k], recv_sem=recv_sems.at[7 * w + k], device_id=to, device_id_type=MESH)

        mine = [pltpu.make_async_copy(ins[w], rows(w, me), local_sems.at[w]) for w in range(n)]
        for cp in mine:
            cp.start()
        first = []
        for w in range(n):
            first.append(copy(w, 0, me, sibling, src=ins[w]))
            first += [copy(w, 1 + j, me, (*chip, c), src=ins[w]) for j, chip in enumerate(chips)]
        for cp in first:
            cp.start()
        passed = []
        for j, chip in enumerate(chips):
            for w in range(n):
                copy(w, 1 + j, (*chip, c), me).wait_recv()
                fwd = copy(w, 4 + j, (*chip, c), sibling)
                fwd.start()
                passed.append(fwd)
        for w in range(n):
            copy(w, 0, sibling, me).wait_recv()
            for j, chip in enumerate(chips):
                copy(w, 4 + j, (*chip, 1 - c), me).wait_recv()
        for cp in first + passed:
            cp.wait_send()
        for cp in mine:
            cp.wait()

    return _hosted(
        body,
        name=name,
        sched=sched,
        in_specs=[any_spec] * n,
        out_specs=[any_spec] * n,
        out_shape=[jax.ShapeDtypeStruct((N_DEV * s.shape[0], s.shape[1]), s.dtype) for s in shards],
        scratch_shapes=[pltpu.SemaphoreType.DMA((7 * n,)), pltpu.SemaphoreType.DMA((7 * n,)), pltpu.SemaphoreType.DMA((n,))],
    )(*shards)


class _Task:
    def __init__(self, ins, out_shapes, n_remote, n_local, plan, done, aliases=None, sibling=False, chips=False):
        self.ins, self.out_shapes, self.n_remote, self.n_local = list(ins), list(out_shapes), n_remote, n_local
        self.plan, self.done, self.aliases = plan, done, dict(aliases or {})
        self.sibling, self.chips = sibling, chips


class _Sched:
    def __init__(self):
        self.pending = {}

    def at(self, host, task):
        self.pending.setdefault(host, []).append(task)

    def take(self, host):
        return self.pending.pop(host, [])


def _in_hbm(out_shape):
    if isinstance(out_shape, (list, tuple)):
        return [_in_hbm(s) for s in out_shape]
    return pltpu.HBM(out_shape.shape, out_shape.dtype)


def _hosted(body, *, name, sched=None, tasks=(), grid=(), in_specs=None, out_specs=None, out_shape=(), scratch_shapes=(),
            compiler_params=None, input_output_aliases=None):
    tasks = list(tasks) + (sched.take(name) if sched is not None else [])
    scratch_shapes = list(scratch_shapes)
    kwargs = dict(name=name)
    core_aliases = dict(input_output_aliases or {})
    if grid:
        kwargs["grid"] = grid
    if compiler_params is not None:
        kwargs["compiler_params"] = compiler_params
    if not tasks:
        if in_specs is not None:
            kwargs.update(in_specs=in_specs, out_specs=out_specs)
        return pl.pallas_call(body, out_shape=_in_hbm(out_shape), scratch_shapes=scratch_shapes, input_output_aliases=core_aliases, **kwargs)
    assert in_specs is not None and out_specs is not None, name
    with_sibling, with_chips = any(t.sibling for t in tasks), any(t.chips for t in tasks)
    collective_id = {(True, False): 1, (False, True): 2, (True, True): 3}[(with_sibling, with_chips)]
    kwargs["compiler_params"] = dataclasses.replace(
        compiler_params if compiler_params is not None else pltpu.CompilerParams(), collective_id=collective_id)
    single = not isinstance(out_shape, (list, tuple))
    core_shape = [out_shape] if single else list(out_shape)
    core_specs = [out_specs] if single else list(out_specs)
    in_specs = list(in_specs)
    n_in, n_out, n_scr = len(in_specs), len(core_shape), len(scratch_shapes)
    t_ins = [a for t in tasks for a in t.ins]
    t_outs = [s for t in tasks for s in t.out_shapes]
    n_rem = sum(t.n_remote for t in tasks)
    n_loc = sum(t.n_local for t in tasks)
    aliases, pos_in, pos_out = dict(core_aliases), n_in, n_out
    for t in tasks:
        for a, b in t.aliases.items():
            aliases[pos_in + a] = pos_out + b
        pos_in += len(t.ins)
        pos_out += len(t.out_shapes)
    any_spec = pl.BlockSpec(memory_space=pl.ANY)

    def wrapped(*refs):
        core_in, t_in = refs[:n_in], refs[n_in : n_in + len(t_ins)]
        o0 = n_in + len(t_ins)
        core_out, t_out = refs[o0 : o0 + n_out], refs[o0 + n_out : o0 + n_out + len(t_outs)]
        s0 = o0 + n_out + len(t_outs)
        core_scr = refs[s0 : s0 + n_scr]
        send_sems, recv_sems, local_sems = refs[s0 + n_scr :]
        remote, local = [], []
        pi = po = pr = pq = 0
        for t in tasks:
            r, q = t.plan(t_in[pi : pi + len(t.ins)], t_out[po : po + len(t.out_shapes)], send_sems, recv_sems, local_sems, pr, pq)
            assert len(r) == t.n_remote and len(q) == t.n_local
            remote += r
            local += q
            pi, po, pr, pq = pi + len(t.ins), po + len(t.out_shapes), pr + t.n_remote, pq + t.n_local

        def start():
            x, y, c, chips = _place()
            peers = ([(x, y, 1 - c)] if with_sibling else []) + ([(*chip, c) for chip in chips] if with_chips else [])
            barrier = pltpu.get_barrier_semaphore()
            for peer in peers:
                pl.semaphore_signal(barrier, inc=1, device_id=peer, device_id_type=MESH)
            pl.semaphore_wait(barrier, len(peers))
            for cp in local + remote:
                cp.start()

        def finish():
            for cp in remote + local:
                cp.wait()

        if grid:
            ids = [pl.program_id(a) for a in range(len(grid))]
            first = functools.reduce(jnp.logical_and, [i == 0 for i in ids])
            last = functools.reduce(jnp.logical_and, [i == g - 1 for i, g in zip(ids, grid)])
            pl.when(first)(start)
            body(*core_in, *core_out, *core_scr)
            pl.when(last)(finish)
        else:
            start()
            body(*core_in, *core_out, *core_scr)
            finish()

    call = pl.pallas_call(
        wrapped,
        in_specs=in_specs + [any_spec] * len(t_ins),
        out_specs=core_specs + [any_spec] * len(t_outs),
        out_shape=_in_hbm(core_shape + t_outs),
        scratch_shapes=scratch_shapes
        + [pltpu.SemaphoreType.DMA((n_rem,)), pltpu.SemaphoreType.DMA((n_rem,)), pltpu.SemaphoreType.DMA((max(n_loc, 1),))],
        input_output_aliases=aliases,
        **kwargs,
    )

    def run(*operands):
        outs = call(*operands, *t_ins)
        po = n_out
        for t in tasks:
            t.done(list(outs[po : po + len(t.out_shapes)]))
            po += len(t.out_shapes)
        return outs[0] if single else list(outs[:n_out])

    return run


def _comm_only(tasks, *, name):
    _hosted(lambda: None, name=name, tasks=tasks, in_specs=[], out_shape=[], out_specs=[])()


def _rows(ref, block, n_blocks):
    r = ref.shape[0] // n_blocks
    return ref.at[pl.ds(block * r, r), :]


def _remote(src, dst, send_sems, recv_sems, k, to):
    return pltpu.make_async_remote_copy(src_ref=src, dst_ref=dst, send_sem=send_sems.at[k], recv_sem=recv_sems.at[k],
                                        device_id=to, device_id_type=MESH)


def _gather_stage1(shards, done, part=(0, 1, 1), into=None):
    n = len(shards)
    k0, k1, n_parts = part

    def plan(ins, outs, send_sems, recv_sems, local_sems, pr, pq):
        x, y, c, chips = _place()
        mine = 4 * x + 2 * y + c
        remote, local = [], []
        for w in range(n):
            r = shards[w].shape[0]
            rp = r // n_parts
            src = ins[w].at[pl.ds(k0 * rp, (k1 - k0) * rp), :]
            dst = outs[w].at[pl.ds(mine * r + k0 * rp, (k1 - k0) * rp), :]
            local.append(pltpu.make_async_copy(src, dst, local_sems.at[pq + w]))
            for j, to in enumerate([(x, y, 1 - c)] + [(*chip, c) for chip in chips]):
                remote.append(_remote(src, dst, send_sems, recv_sems, pr + 4 * w + j, to))
        return remote, local

    shapes = [jax.ShapeDtypeStruct((N_DEV * s.shape[0], s.shape[1]), s.dtype) for s in shards]
    if into is None:
        return _Task(shards, shapes, 4 * n, n, plan, done, sibling=True, chips=True)
    return _Task(list(shards) + list(into), shapes, 4 * n, n, plan, done, aliases={n + w: w for w in range(n)}, sibling=True, chips=True)


def _gather_stage2(partial, done):
    n = len(partial)

    def plan(ins, outs, send_sems, recv_sems, local_sems, pr, pq):
        x, y, c, chips = _place()
        remote = []
        for w in range(n):
            for j, chip in enumerate(chips):
                rows = _rows(outs[w], 4 * chip[0] + 2 * chip[1] + c, N_DEV)
                remote.append(_remote(rows, rows, send_sems, recv_sems, pr + 3 * w + j, (x, y, 1 - c)))
        return remote, []

    shapes = [jax.ShapeDtypeStruct(p.shape, p.dtype) for p in partial]
    return _Task(partial, shapes, 3 * n, 0, plan, done, aliases={w: w for w in range(n)}, sibling=True)


def _sibling_stage(grads, done):
    n = len(grads)

    def plan(ins, outs, send_sems, recv_sems, local_sems, pr, pq):
        x, y, c, _ = _place()
        remote = []
        for w in range(n):
            for k in range(4):
                src = _rows(ins[w], 4 * (k // 2) + 2 * (k % 2) + (1 - c), N_DEV)
                remote.append(_remote(src, _rows(outs[w], k, 4), send_sems, recv_sems, pr + 4 * w + k, (x, y, 1 - c)))
        return remote, []

    shapes = [jax.ShapeDtypeStruct((g.shape[0] // 2, g.shape[1]), g.dtype) for g in grads]
    return _Task(grads, shapes, 4 * n, 0, plan, done, sibling=True)


def _chips_stage(sends, done, part=(0, 1), into=None):
    n = len(sends)
    k, n_parts = part

    def plan(ins, outs, send_sems, recv_sems, local_sems, pr, pq):
        x, y, c, chips = _place()
        remote = []
        for w in range(n):
            r = sends[w].shape[0] // 3
            rp = r // n_parts
            for j, chip in enumerate(chips):
                rows = pl.ds(j * r + k * rp, rp)
                remote.append(_remote(ins[w].at[rows, :], outs[w].at[rows, :], send_sems, recv_sems, pr + 3 * w + j, (*chip, c)))
        return remote, []

    shapes = [jax.ShapeDtypeStruct(s.shape, s.dtype) for s in sends]
    if into is None:
        return _Task(sends, shapes, 3 * n, 0, plan, done, chips=True)
    return _Task(list(sends) + list(into), shapes, 3 * n, 0, plan, done, aliases={n + w: w for w in range(n)}, chips=True)


def _pair_sum(grad, got, place, *, name):
    R, C = grad.shape
    r = R // N_DEV
    tr = _tile(r, 256, 16)
    nt = r // tr

    def chip_of(s, pr):
        fx = jnp.where((s == 1) | (s == 3), 1, 0)
        fy = jnp.where(s >= 2, 1, 0)
        return pr[0] ^ fx, pr[1] ^ fy

    def grad_map(s, t, pr):
        kx, ky = chip_of(s, pr)
        return ((4 * kx + 2 * ky + pr[2]) * nt + t, 0)

    def got_map(s, t, pr):
        kx, ky = chip_of(s, pr)
        return ((2 * kx + ky) * nt + t, 0)

    def body(pr, g_ref, r_ref, own_ref, send_ref):
        s = pl.program_id(0)
        tot = g_ref[...] + r_ref[...]

        @pl.when(s == 0)
        def _():
            own_ref[...] = tot

        @pl.when(s > 0)
        def _():
            send_ref[...] = tot.astype(BF16)

    return pl.pallas_call(
        body,
        name=name,
        grid_spec=pltpu.PrefetchScalarGridSpec(
            num_scalar_prefetch=1,
            grid=(4, nt),
            in_specs=[pl.BlockSpec((tr, C), grad_map), pl.BlockSpec((tr, C), got_map)],
            out_specs=[
                pl.BlockSpec((tr, C), lambda s, t, pr: (jnp.where(s == 0, t, nt - 1), 0)),
                pl.BlockSpec((tr, C), lambda s, t, pr: (jnp.where(s == 0, 0, (s - 1) * nt + t), 0)),
            ],
        ),
        out_shape=[jax.ShapeDtypeStruct((r, C), F32), jax.ShapeDtypeStruct((3 * r, C), BF16)],
        compiler_params=_params(("arbitrary", "arbitrary"), [((tr, C), F32)] * 4),
    )(place, grad, got)


def _adam_vals(w, g, m, v):
    m = ADAM_B1 * m + (1.0 - ADAM_B1) * g
    v = ADAM_B2 * v + (1.0 - ADAM_B2) * (g * g)
    m_hat = m / (1.0 - ADAM_B1**ADAM_STEP)
    v_hat = v / (1.0 - ADAM_B2**ADAM_STEP)
    delta = -ADAM_LR * (m_hat / (jnp.sqrt(v_hat) + ADAM_EPS) + ADAM_WD * w)
    return delta, m, v


def _adam_big(w3, m3, v3, own, got, layer, transposed, prev, *, name, sched=None):
    _, A, B = w3.shape
    ta = _tile(A, 256, 16)
    nt = A // ta
    b_pad = (-B) % LANES

    def body(*refs):
        w_ref, m_ref, v_ref, own_ref, g1_ref, g2_ref, g3_ref = refs[:7]
        g_ref, d_ref, nm_ref, nv_ref = refs[-4:]
        g = ((own_ref[...] + g1_ref[...].astype(F32)) + g2_ref[...].astype(F32)) + g3_ref[...].astype(F32)
        if transposed:
            if b_pad:
                g = jnp.concatenate([g, jnp.zeros((b_pad, ta), F32)], axis=0)
            g = g.T[:, :B]
        g_ref[...] = g
        d_ref[...], nm_ref[...], nv_ref[...] = _adam_vals(w_ref[...], g, m_ref[...], v_ref[...])

    nat = pl.BlockSpec((None, ta, B), lambda t: (layer, t, 0))
    if transposed:
        parts = [pl.BlockSpec((B, ta), lambda t: (0, t))] + [pl.BlockSpec((B, ta), lambda t, j=j: (j, t)) for j in range(3)]
    else:
        parts = [pl.BlockSpec((ta, B), lambda t: (t, 0))] + [pl.BlockSpec((ta, B), lambda t, j=j: (j * nt + t, 0)) for j in range(3)]
    keep = [] if prev is None else list(prev)
    outs = _hosted(
        body,
        name=name,
        sched=sched,
        grid=(nt,),
        in_specs=[nat, nat, nat] + parts + [pl.BlockSpec(memory_space=pl.ANY)] * len(keep),
        out_specs=[nat] * 4,
        out_shape=[jax.ShapeDtypeStruct(w3.shape, F32)] * 4,
        input_output_aliases={7 + k: k for k in range(len(keep))},
        compiler_params=_params(("arbitrary",), [((ta, B), F32)] * 10),
    )(w3, m3, v3, own, got, got, got, *keep)
    return list(outs)


def _adam_small(w, m, v, parts, *, name, sched=None):
    R = w.shape[0]

    def body(w_ref, m_ref, v_ref, p_ref, g_ref, d_ref, nm_ref, nv_ref):
        g = p_ref[pl.ds(0, R), :]
        for k in range(1, N_DEV):
            g = g + p_ref[pl.ds(k * R, R), :]
        g_ref[...] = g
        d_ref[...], nm_ref[...], nv_ref[...] = _adam_vals(w_ref[...], g, m_ref[...], v_ref[...])

    return _hosted(
        body,
        name=name,
        sched=sched,
        out_shape=[jax.ShapeDtypeStruct((R, LANES), F32)] * 4,
        compiler_params=_params(None, [((R, LANES), F32)] * 16),
    )(w, m, v, parts)


SMALL_NAMES = ("b_gate", "ln_v_g", "ln_v_b", "w_s", "b_s", "sinks", "ln1_g", "ln1_b", "ln2_g", "ln2_b", "ln3_g", "ln3_b")
PACK_ALIGN = 8 * LANES


def _pack(arrays):
    flat = []
    for a in arrays:
        a = a.reshape(-1)
        flat.append(jnp.pad(a, (0, (-a.shape[0]) % PACK_ALIGN)))
    return jnp.concatenate(flat).reshape(-1, LANES)


def _unpack(packed, shapes):
    flat = packed.reshape(-1)
    out, pos = [], 0
    for s in shapes:
        n = 1
        for e in s:
            n *= e
        out.append(flat[pos : pos + n].reshape(s))
        pos += n + (-n) % PACK_ALIGN
    return out


BIG = (("in", "w_in", True), ("a", "w_br_a", True), ("b", "w_br_b", True), ("o", "w_o", False), ("xq", "w_xq", False),
       ("xkv", "w_xkv", False), ("xo", "w_xo", True), ("up", "w_up", True), ("down", "w_down", False))


SMALL_BIG = ("a", "b", "o", "xq", "xkv", "xo")
GATHER_PLAN = (
    (0, ("in",), None, None),
    (0, SMALL_BIG, ("proj_l0",), "swa_fwd_l0"),
    (0, ("up",), ("swa_fwd_l0", "merge_l0"), "oproj_ln_l0"),
    (0, ("down",), ("oproj_ln_l0", "xattn_fwd_l0"), "up_l0"),
    (1, ("in",), ("up_l0",), "down_ln_l0"),
    (1, SMALL_BIG, ("down_ln_l0",), "proj_l1"),
    (1, ("up",), ("down_ln_l0", ("proj_l1", 3)), "swa_fwd_l1"),
    (1, ("down",), (("swa_fwd_l1", 2), "merge_l1", "oproj_ln_l1"), "xattn_fwd_l1"),
)
EARLY_SMALL_BIG = ("o", "xq", "xkv", "xo")
LATE_SMALL_BIG = ("a", "b")
GRAD_HOSTS = {
    1: {"down": ("xattn_bwd_l1", "swa_bwd_l1"), "up": ("xattn_bwd_l1", ("gmlp_bwd_l1", "proj_dw_l1")),
        **{g: ("gmlp_bwd_l1", "proj_dx_l1") for g in EARLY_SMALL_BIG}, **{g: ("proj_dw_l1", "proj_dx_l1") for g in LATE_SMALL_BIG},
        "in": ("proj_dx_l1", ("down_dx_l0", "down_dw_l0"))},
    0: {"down": ("xattn_bwd_l0", ("oproj_dx_l0", "gmlp_bwd_l0")), "up": ("xattn_bwd_l0", "swa_bwd_l0"),
        **{g: ("gmlp_bwd_l0", "proj_dw_l0") for g in EARLY_SMALL_BIG}, **{g: ("proj_dw_l0", "proj_dx_l0") for g in LATE_SMALL_BIG},
        "in": (None, "proj_dx_l0")},
}
SMALL_GRAD_HOSTS = ("proj_dw_l0", "proj_dx_l0")


def kernel(x, mem, w_in, b_gate, ln_v_g, ln_v_b, w_s, b_s, sinks, w_br_a, w_br_b, w_o, ln1_g, ln1_b, w_xq, w_xkv, w_xo, ln2_g, ln2_b, w_up, w_down, ln3_g, ln3_b, loss_target, m_w_in, m_b_gate, m_ln_v_g, m_ln_v_b, m_w_s, m_b_s, m_sinks, m_w_br_a, m_w_br_b, m_w_o, m_ln1_g, m_ln1_b, m_w_xq, m_w_xkv, m_w_xo, m_ln2_g, m_ln2_b, m_w_up, m_w_down, m_ln3_g, m_ln3_b, v_w_in, v_b_gate, v_ln_v_g, v_ln_v_b, v_w_s, v_b_s, v_sinks, v_w_br_a, v_w_br_b, v_w_o, v_ln1_g, v_ln1_b, v_w_xq, v_w_xkv, v_w_xo, v_ln2_g, v_ln2_b, v_w_up, v_w_down, v_ln3_g, v_ln3_b):
    given = dict(locals())
    T, D = x.shape[1], x.shape[2]
    IN, GW, AW, XW, DFF = w_in.shape[2] * N_DEV, ln_v_g.shape[1], w_br_b.shape[1], w_xq.shape[2], w_up.shape[2] * N_DEV
    KVW = (IN - 2 * GW - AW - 2 * D) // 2
    d = Dims(T=T, D=D, IN=IN, GW=GW, G=GW // GROUP_DIM, AW=AW, KVW=KVW, HQ=AW // HEAD_DIM, HKV=KVW // HEAD_DIM, XW=XW,
             XH=XW // X_HEAD_DIM, MEM=mem.shape[1], DFF=DFF)
    place = jnp.stack([lax.axis_index("x"), lax.axis_index("y"), lax.axis_index("c")]).astype(jnp.int32)

    sched = _Sched()
    big_of = {g: (p, tr) for g, p, tr in BIG}

    def shard(l, g):
        p, tr = big_of[g]
        return (given[p][l].T if tr else given[p][l]).astype(BF16)

    Gs = [{}, {}]

    def plan_gather(l, names, hosts1, host2):
        shards = [shard(l, g) for g in names]
        if hosts1 is None:
            Gs[l].update(zip(names, _all_gather(shards, name=f"gather_{names[0]}_l{l}")))
            return

        spans = [(h, 1) if isinstance(h, str) else h for h in hosts1]
        total = sum(cnt for _, cnt in spans)

        def register(i, k0, into):
            host, cnt = spans[i]

            def done(outs):
                if i + 1 < len(spans):
                    register(i + 1, k0 + cnt, outs)
                else:
                    sched.at(host2, _gather_stage2(outs, lambda full: Gs[l].update(zip(names, full))))

            sched.at(host, _gather_stage1(shards, done, part=(k0, k0 + cnt, total), into=into))

        register(0, 0, None)

    for entry in GATHER_PLAN:
        plan_gather(*entry)
    Ss = []
    for l in range(DEPTH):
        S = {n: given[n][l].reshape(1, -1) for n in SMALL_NAMES if n not in ("w_s", "b_s", "sinks")}
        S["w_s"], S["b_sT"], S["sinks"] = w_s[l], b_s[l].T, sinks[l]
        Ss.append(S)

    owns, recvs = {}, {}

    smalls_seen = {}
    gathered_small = []

    def pack_small(src):
        parts = []
        for l in range(DEPTH):
            for n in SMALL_NAMES:
                a = src[l]["b_sT"].T if n == "b_s" else src[l][n]
                parts.append(a[0, : d.HQ] if n == "sinks" else a)
        return _pack(parts)

    def on_small(l, small):
        smalls_seen[l] = small
        if len(smalls_seen) == DEPTH:
            host1, host2 = SMALL_GRAD_HOSTS
            sched.at(host1, _gather_stage1([pack_small(smalls_seen)], lambda part: sched.at(
                host2, _gather_stage2(part, lambda full: gathered_small.append(full[0])))))

    def on_grad(l, g, grad):
        if g == "small":
            return on_small(l, grad)
        sib_host, chips_host = GRAD_HOSTS[l][g]

        def after_sibling(got):
            owns[(l, g)], send = _pair_sum(grad, got[0], place, name=f"grad_pair_sum_{g}_l{l}")
            hosts = chips_host if isinstance(chips_host, tuple) else (chips_host,)

            def register(k, into):
                def done(r):
                    if k + 1 < len(hosts):
                        register(k + 1, r)
                    else:
                        recvs[(l, g)] = r[0]

                task = _chips_stage([send], done, part=(k, len(hosts)), into=into)
                if hosts[k] is None:
                    _comm_only([task], name=f"grad_chips_{g}_l{l}")
                else:
                    sched.at(hosts[k], task)

            register(0, None)

        task = _sibling_stage([grad], after_sibling)
        if sib_host is None:
            _comm_only([task], name=f"grad_sibling_{g}_l{l}")
        else:
            sched.at(sib_host, task)

    loss, dX, bigs, smalls = _local_step(d, x[0], mem[0], loss_target[0], Gs, Ss, sched, on_grad)
    loss = lax.psum(loss[0, 0], ("x", "y", "c"))

    assert not sched.pending, sorted(sched.pending)

    def viewed(p, tr):
        return tr and given[p].shape[2] % LANES != 0

    res = {p: None for _, p, _ in BIG}
    for l in reversed(range(DEPTH)):
        for g, p, tr in BIG:
            view = viewed(p, tr)
            w3, m3, v3 = ((jnp.swapaxes(a, 1, 2) if view else a) for a in (given[p], given["m_" + p], given["v_" + p]))
            res[p] = _adam_big(w3, m3, v3, owns[(l, g)], recvs[(l, g)], l, tr and not view, res[p], name=f"adamw_{g}_l{l}")
    for g, p, tr in BIG:
        if viewed(p, tr):
            res[p] = [jnp.swapaxes(a, 1, 2) for a in res[p]]

    shapes = [given[n].shape[1:] for n in SMALL_NAMES]
    pw, pm, pv = (_pack([given[pre + n][l] for l in range(DEPTH) for n in SMALL_NAMES]) for pre in ("", "m_", "v_"))
    small_out = [_unpack(o, shapes * DEPTH) for o in _adam_small(pw, pm, pv, gathered_small[0], name="adamw_small")]

    names = ["w_in", "b_gate", "ln_v_g", "ln_v_b", "w_s", "b_s", "sinks", "w_br_a", "w_br_b", "w_o", "ln1_g", "ln1_b", "w_xq",
             "w_xkv", "w_xo", "ln2_g", "ln2_b", "w_up", "w_down", "ln3_g", "ln3_b"]
    out = [loss, dX[None]]
    for kind in range(4):
        for n in names:
            if n in SMALL_NAMES:
                i = SMALL_NAMES.index(n)
                out.append(jnp.stack([small_out[kind][l * len(SMALL_NAMES) + i] for l in range(DEPTH)]))
            else:
                out.append(res[n][kind])
    return tuple(out)
```

```python
import collections
import dataclasses
import functools

import jax
import jax.numpy as jnp
from jax import lax
from jax.experimental import pallas as pl
from jax.experimental.pallas import tpu as pltpu

F32 = jnp.float32
BF16 = jnp.bfloat16
MESH = pl.DeviceIdType.MESH

N_DEV = 8
DEPTH = 2
CHUNK = 128
HEAD_DIM = 64
ROPE_HALF = HEAD_DIM // 2
X_HEAD_DIM = 128
GROUP_DIM = 128
LANES = 128
ROPE_THETA = 10000.0
LN_EPS = 1e-5
ALPHA = (2 * DEPTH) ** 0.25
ADAM_LR = 0.001
ADAM_B1 = 0.9
ADAM_B2 = 0.999
ADAM_EPS = 1e-08
ADAM_WD = 0.01
ADAM_STEP = 10
VMEM_BYTES = 64 * 1024 * 1024
VMEM_TEMP_BYTES = 20 * 1024 * 1024

Dims = collections.namedtuple("Dims", "T D IN GW G AW KVW HQ HKV XW XH MEM DFF")


def _offsets(d):
    off_v = d.GW
    off_q = 2 * d.GW
    off_k = off_q + d.AW
    off_va = off_k + d.KVW
    off_ga = off_va + d.KVW
    off_gb = off_ga + d.D
    return off_v, off_q, off_k, off_va, off_ga, off_gb


def _tile(dim, pref, align=LANES):
    if dim <= pref:
        return dim
    t = pref - pref % align
    while t >= align:
        if dim % t == 0:
            return t
        t -= align
    return dim


def _nbytes(shape, dtype):
    n = 1
    for s in shape:
        n *= s
    return n * jnp.dtype(dtype).itemsize


def _params(sem, blocks, scratch=0):
    need = 2 * sum(_nbytes(s, t) for s, t in blocks) + scratch + VMEM_TEMP_BYTES
    limit = min(max(need, 32 * 1024 * 1024), VMEM_BYTES - 4 * 1024 * 1024)
    return pltpu.CompilerParams(dimension_semantics=sem, vmem_limit_bytes=limit)


def _dot(a, b, kind):
    dn = {"nn": (((1,), (0,)), ((), ())), "nt": (((1,), (1,)), ((), ())), "tn": (((0,), (0,)), ((), ()))}[kind]
    return lax.dot_general(a, b, dn, preferred_element_type=F32)


def _gelu(x):
    c = 0.7978845608028654
    t = jnp.tanh(c * (x + 0.044715 * (x * x * x)))
    return 0.5 * x * (1.0 + t)


def _gelu_grad(x):
    c = 0.7978845608028654
    x2 = x * x
    t = jnp.tanh(c * (x + 0.044715 * (x2 * x)))
    return 0.5 * (1.0 + t) + 0.5 * x * (1.0 - t * t) * (c * (1.0 + 3.0 * 0.044715 * x2))


def _ln_fwd(z, g, b):
    mu = jnp.mean(z, axis=-1, keepdims=True)
    zc = z - mu
    var = jnp.mean(zc * zc, axis=-1, keepdims=True)
    rstd = lax.rsqrt(var + LN_EPS)
    xhat = zc * rstd
    return xhat * g + b, xhat, rstd


def _ln_bwd_vals(dy, xhat, rstd, g):
    gd = dy * g
    m1 = jnp.mean(gd, axis=-1, keepdims=True)
    m2 = jnp.mean(gd * xhat, axis=-1, keepdims=True)
    return rstd * (gd - m1 - xhat * m2)


def _acc_store(ref, val, first):
    @pl.when(first)
    def _():
        ref[...] = val

    @pl.when(jnp.logical_not(first))
    def _():
        ref[...] += val


def _matmul(a, b, kind, *, name, tm=512, tn=1024, tk=2048, out_defs=(("mn", F32),), epilogue=None, extras=(), sched=None):
    if kind == "nn":
        (M, K), (K2, N) = a.shape, b.shape
    elif kind == "nt":
        (M, K), (N, K2) = a.shape, b.shape
    else:
        (K, M), (K2, N) = a.shape, b.shape
    assert K == K2, (a.shape, b.shape, kind)
    tm, tn, tk = _tile(M, tm), _tile(N, tn), _tile(K, tk)
    nk = K // tk
    blocks = []

    def spec(shape, imap, dtype):
        blocks.append((shape, dtype))
        return pl.BlockSpec(shape, imap)

    if kind == "tn":
        a_spec = spec((tk, tm), lambda j, i, k: (k, i), a.dtype)
    else:
        a_spec = spec((tm, tk), lambda j, i, k: (i, k), a.dtype)
    if kind == "nt":
        b_spec = spec((tn, tk), lambda j, i, k: (j, k), b.dtype)
    else:
        b_spec = spec((tk, tn), lambda j, i, k: (k, j), b.dtype)
    in_specs = [a_spec, b_spec]
    operands = [a, b]
    for arr, ekind, off in extras:
        if ekind == "mn":
            assert off % tn == 0
            in_specs.append(spec((tm, tn), lambda j, i, k, o=off // tn: (i, j + o), arr.dtype))
        elif ekind == "row":
            assert off % tn == 0
            in_specs.append(spec((1, tn), lambda j, i, k, o=off // tn: (0, j + o), arr.dtype))
        else:
            in_specs.append(spec((tm, 1), lambda j, i, k: (i, 0), arr.dtype))
        operands.append(arr)
    out_shape, out_specs = [], []
    for od in out_defs:
        okind, dt = od[0], od[1]
        if okind == "mn":
            out_shape.append(jax.ShapeDtypeStruct((M, N), dt))
            out_specs.append(spec((tm, tn), lambda j, i, k: (i, j), dt))
        elif okind == "cols":
            assert od[3] % tn == 0
            out_shape.append(jax.ShapeDtypeStruct((M, od[2]), dt))
            out_specs.append(spec((tm, tn), lambda j, i, k, o=od[3] // tn: (i, j + o), dt))
        elif okind == "m1":
            assert N == tn
            out_shape.append(jax.ShapeDtypeStruct((M, 1), dt))
            out_specs.append(spec((tm, 1), lambda j, i, k: (i, 0), dt))
        else:
            out_shape.append(jax.ShapeDtypeStruct((1, N), F32))
            out_specs.append(spec((1, tn), lambda j, i, k: (0, j), F32))
    n_ex, n_out = len(extras), len(out_defs)
    ep = epilogue if epilogue is not None else (lambda acc: (acc,))
    in_place = nk > 1 and epilogue is None and tuple(out_defs) == (("mn", F32),)

    def body(*refs):
        a_ref, b_ref = refs[0], refs[1]
        ex_refs = refs[2 : 2 + n_ex]
        out_refs = refs[2 + n_ex : 2 + n_ex + n_out]
        i = pl.program_id(1)
        k = pl.program_id(2)

        def product():
            return _dot(a_ref[...], b_ref[...], kind)

        def finish(acc):
            vals = ep(acc, *[r[...] for r in ex_refs])
            for od, r, v in zip(out_defs, out_refs, vals):
                if od[0] == "acc":
                    _acc_store(r, v, i == 0)
                else:
                    r[...] = v.astype(od[1])

        if nk == 1:
            finish(product())
            return
        acc_ref = out_refs[0] if in_place else refs[-1]

        @pl.when(k == 0)
        def _():
            acc_ref[...] = product()

        if in_place:
            @pl.when(k > 0)
            def _():
                acc_ref[...] += product()
        else:
            if nk > 2:
                @pl.when(jnp.logical_and(k > 0, k < nk - 1))
                def _():
                    acc_ref[...] += product()

            @pl.when(k == nk - 1)
            def _():
                finish(acc_ref[...] + product())

    scratch = [pltpu.VMEM((tm, tn), F32)] if nk > 1 and not in_place else []
    outs = _hosted(
        body,
        name=name,
        sched=sched,
        grid=(N // tn, M // tm, nk),
        in_specs=in_specs,
        out_specs=out_specs,
        out_shape=out_shape,
        scratch_shapes=scratch,
        compiler_params=_params(("arbitrary", "arbitrary", "arbitrary"), blocks, _nbytes((tm, tn), F32) if scratch else 0),
    )(*operands)
    return outs if len(outs) > 1 else outs[0]


def _ep_resid_ln(acc, resid, g, b):
    y, xhat, rstd = _ln_fwd(ALPHA * resid + acc, g, b)
    return y, y, xhat, rstd


def _ep_resid(acc, resid):
    return (ALPHA * resid + acc,)


def _ep_resid_ln_bwd(acc, resid, xhat, rstd, g):
    dy = ALPHA * resid + acc
    dz = _ln_bwd_vals(dy, xhat, rstd, g)
    return dz, dz, jnp.sum(dy * xhat, axis=0, keepdims=True), jnp.sum(dy, axis=0, keepdims=True)


def _ep_relu2(acc):
    r = jnp.maximum(acc, 0.0)
    return acc, r * r


def _ep_relu2_bwd(acc, h):
    return (acc * (2.0 * jnp.maximum(h, 0.0)),)


def _ep_gate_bwd(acc, ga, gb, ya, yb, bga, bgb):
    sa = jax.nn.sigmoid(ga + bga)
    sb = jax.nn.sigmoid(gb + bgb)
    dga = acc * ya * (sa * (1.0 - sa))
    dgb = acc * yb * (sb * (1.0 - sb))
    return (acc * sa, acc * sb, dga, dgb, jnp.sum(dga, axis=0, keepdims=True), jnp.sum(dgb, axis=0, keepdims=True))


def _ln_bwd(dy, xhat, rstd, g, *, name, sched=None, target=None):
    T, D = dy.shape
    tm = _tile(T, 256)
    head = target is not None

    def body(*refs):
        dy_ref, xh_ref, rs_ref, g_ref = refs[:4]
        dz_ref, dzb_ref, dg_ref, db_ref = refs[4 + head : 8 + head]
        first = pl.program_id(0) == 0
        dyv, xh = dy_ref[...], xh_ref[...]
        if head:
            err = dyv - refs[4][...]
            dyv = err * (1.0 / D)
            part = 0.5 * jnp.sum(jnp.sum(err * err, axis=1, keepdims=True) * (1.0 / D), axis=0, keepdims=True)
            _acc_store(refs[-1], part, first)
        dz = _ln_bwd_vals(dyv, xh, rs_ref[...], g_ref[...])
        dz_ref[...] = dz
        dzb_ref[...] = dz.astype(BF16)
        _acc_store(dg_ref, jnp.sum(dyv * xh, axis=0, keepdims=True), first)
        _acc_store(db_ref, jnp.sum(dyv, axis=0, keepdims=True), first)

    row = pl.BlockSpec((tm, D), lambda i: (i, 0))
    vec = pl.BlockSpec((1, D), lambda i: (0, 0))
    one = pl.BlockSpec((1, 1), lambda i: (0, 0))
    return _hosted(
        body,
        name=name,
        sched=sched,
        grid=(T // tm,),
        in_specs=[row, row, pl.BlockSpec((tm, 1), lambda i: (i, 0)), vec] + [row] * head,
        out_specs=[row, row, vec, vec] + [one] * head,
        out_shape=[
            jax.ShapeDtypeStruct((T, D), F32),
            jax.ShapeDtypeStruct((T, D), BF16),
            jax.ShapeDtypeStruct((1, D), F32),
            jax.ShapeDtypeStruct((1, D), F32),
        ] + [jax.ShapeDtypeStruct((1, 1), F32)] * head,
        compiler_params=_params(("arbitrary",), [((tm, D), F32)] * (4 + head)),
    )(*([dy, xhat, rstd, g] + [target] * head))


def _masked_mix_weights(wm_ref, G):
    r = lax.broadcasted_iota(jnp.int32, (CHUNK, CHUNK), 0)
    c = lax.broadcasted_iota(jnp.int32, (CHUNK, CHUNK), 1)
    return [jnp.where(c <= r, wm_ref[g], 0.0).astype(BF16) for g in range(G)]


def _gmlp_fwd(d, P, lnv_g, lnv_b, wm, bsT, *, name, sched=None):
    T, GW, G = d.T, d.GW, d.G
    tm = _tile(T, 256)
    nc = tm // CHUNK

    def body(u_ref, v_ref, g_ref, b_ref, wm_ref, bs_ref, o_ref):
        gu = _gelu(u_ref[...])
        vn, _, _ = _ln_fwd(_gelu(v_ref[...]), g_ref[...], b_ref[...])
        vn = vn.astype(BF16)
        wl = _masked_mix_weights(wm_ref, G)
        for c in range(nc):
            rows = slice(c * CHUNK, (c + 1) * CHUNK)
            for g in range(G):
                cols = slice(g * GROUP_DIM, (g + 1) * GROUP_DIM)
                mixed = _dot(wl[g], vn[rows, cols], "nn") + bs_ref[:, g : g + 1]
                o_ref[rows, cols] = (gu[rows, cols] * mixed).astype(BF16)

    return _hosted(
        body,
        name=name,
        sched=sched,
        grid=(T // tm,),
        in_specs=[
            pl.BlockSpec((tm, GW), lambda i: (i, 0)),
            pl.BlockSpec((tm, GW), lambda i: (i, 1)),
            pl.BlockSpec((1, GW), lambda i: (0, 0)),
            pl.BlockSpec((1, GW), lambda i: (0, 0)),
            pl.BlockSpec((G, CHUNK, CHUNK), lambda i: (0, 0, 0)),
            pl.BlockSpec((CHUNK, G), lambda i: (0, 0)),
        ],
        out_specs=pl.BlockSpec((tm, GW), lambda i: (i, 0)),
        out_shape=jax.ShapeDtypeStruct((T, GW), BF16),
        compiler_params=_params(("arbitrary",), [((tm, GW), F32)] * 3),
    )(P, P, lnv_g, lnv_b, wm, bsT)


def _gmlp_bwd(d, P, dya_b, Ga, lnv_g, lnv_b, wm, bsT, dP, *, name, sched=None):
    T, D, GW, G = d.T, d.D, d.GW, d.G
    tm = _tile(T, 256)
    nc = tm // CHUNK

    def body(u_ref, v_ref, dya_ref, ga_ref, g_ref, b_ref, wm_ref, bs_ref, dp_in, duv_ref, dwm_ref, dbs_ref, dlg_ref, dlb_ref, dgu_s, dvn_s):
        first = pl.program_id(0) == 0
        dsgu = _dot(dya_ref[...], ga_ref[...], "nn")
        u, v = u_ref[...], v_ref[...]
        gu = _gelu(u)
        lng = g_ref[...]
        vn, xh, rstd = _ln_fwd(_gelu(v), lng, b_ref[...])
        vn = vn.astype(BF16)
        wl = _masked_mix_weights(wm_ref, G)
        r = lax.broadcasted_iota(jnp.int32, (CHUNK, CHUNK), 0)
        cc = lax.broadcasted_iota(jnp.int32, (CHUNK, CHUNK), 1)
        lane_g = lax.broadcasted_iota(jnp.int32, (CHUNK, G), 1)
        dbs = jnp.zeros((CHUNK, G), F32)
        for g in range(G):
            cols = slice(g * GROUP_DIM, (g + 1) * GROUP_DIM)
            dw = jnp.zeros((CHUNK, CHUNK), F32)
            for c in range(nc):
                rows = slice(c * CHUNK, (c + 1) * CHUNK)
                mixed = _dot(wl[g], vn[rows, cols], "nn") + bs_ref[:, g : g + 1]
                ds = dsgu[rows, cols]
                dgu_s[rows, cols] = ds * mixed
                dmx = ds * gu[rows, cols]
                dbs = dbs + jnp.where(lane_g == g, jnp.sum(dmx, axis=1, keepdims=True), 0.0)
                dmx = dmx.astype(BF16)
                dw = dw + _dot(dmx, vn[rows, cols], "nt")
                dvn_s[rows, cols] = _dot(wl[g], dmx, "tn")
            _acc_store(dwm_ref.at[g], jnp.where(cc <= r, dw, 0.0), first)
        _acc_store(dbs_ref, dbs, first)
        dvn = dvn_s[...]
        _acc_store(dlg_ref, jnp.sum(dvn * xh, axis=0, keepdims=True), first)
        _acc_store(dlb_ref, jnp.sum(dvn, axis=0, keepdims=True), first)
        dgv = _ln_bwd_vals(dvn, xh, rstd, lng)
        duv_ref[:, :GW] = (dgu_s[...] * _gelu_grad(u)).astype(BF16)
        duv_ref[:, GW:] = (dgv * _gelu_grad(v)).astype(BF16)

    return _hosted(
        body,
        name=name,
        sched=sched,
        grid=(T // tm,),
        in_specs=[
            pl.BlockSpec((tm, GW), lambda i: (i, 0)),
            pl.BlockSpec((tm, GW), lambda i: (i, 1)),
            pl.BlockSpec((tm, D), lambda i: (i, 0)),
            pl.BlockSpec((D, GW), lambda i: (0, 0)),
            pl.BlockSpec((1, GW), lambda i: (0, 0)),
            pl.BlockSpec((1, GW), lambda i: (0, 0)),
            pl.BlockSpec((G, CHUNK, CHUNK), lambda i: (0, 0, 0)),
            pl.BlockSpec((CHUNK, G), lambda i: (0, 0)),
            pl.BlockSpec(memory_space=pl.ANY),
        ],
        out_specs=[
            pl.BlockSpec((tm, 2 * GW), lambda i: (i, 0)),
            pl.BlockSpec((G, CHUNK, CHUNK), lambda i: (0, 0, 0)),
            pl.BlockSpec((CHUNK, G), lambda i: (0, 0)),
            pl.BlockSpec((1, GW), lambda i: (0, 0)),
            pl.BlockSpec((1, GW), lambda i: (0, 0)),
        ],
        out_shape=[
            jax.ShapeDtypeStruct(dP.shape, BF16),
            jax.ShapeDtypeStruct((G, CHUNK, CHUNK), F32),
            jax.ShapeDtypeStruct((CHUNK, G), F32),
            jax.ShapeDtypeStruct((1, GW), F32),
            jax.ShapeDtypeStruct((1, GW), F32),
        ],
        scratch_shapes=[pltpu.VMEM((tm, GW), F32), pltpu.VMEM((tm, GW), F32)],
        input_output_aliases={8: 0},
        compiler_params=_params(
            ("arbitrary",), [((tm, GW), F32)] * 3 + [((tm, D), BF16), ((D, GW), BF16)], 2 * _nbytes((tm, GW), F32)
        ),
    )(P, P, dya_b, Ga, lnv_g, lnv_b, wm, bsT, dP)


def _swap_halves(x):
    w = x.shape[1]
    lane = lax.broadcasted_iota(jnp.int32, x.shape, 1)
    return jnp.where((lane % HEAD_DIM) < ROPE_HALF, pltpu.roll(x, w - ROPE_HALF, 1), pltpu.roll(x, ROPE_HALF, 1))


def _lane_tile(t, width):
    reps = width // LANES
    return t if reps == 1 else jnp.tile(t, (1, reps))


def _rope(x, cos2, sin2):
    w = x.shape[1]
    return x * _lane_tile(cos2, w) + _swap_halves(x) * _lane_tile(sin2, w)


def _rope_bwd(g, cos2, sin2):
    w = g.shape[1]
    return g * _lane_tile(cos2, w) - _swap_halves(g) * _lane_tile(sin2, w)


PAIR = LANES // HEAD_DIM


def _swa_valid(i):
    s = lax.broadcasted_iota(jnp.int32, (2 * CHUNK, CHUNK), 0)
    t = lax.broadcasted_iota(jnp.int32, (2 * CHUNK, CHUNK), 1)
    cur = jnp.logical_and(s >= CHUNK, s - CHUNK <= t)
    prev = jnp.logical_and(jnp.logical_and(s < CHUNK, s > t), i > 0)
    return jnp.logical_or(cur, prev)


def _half_variants(x, odd):
    lane = lax.broadcasted_iota(jnp.int32, x.shape, 1)
    if odd:
        hi = jnp.where(lane >= HEAD_DIM, x, jnp.zeros_like(x))
        return pltpu.roll(hi, HEAD_DIM, 1), hi
    lo = jnp.where(lane < HEAD_DIM, x, jnp.zeros_like(x))
    return lo, pltpu.roll(lo, HEAD_DIM, 1)


def _swa_probs_t(kx, qp, sink, valid):
    s = jnp.where(valid, _dot(kx, qp, "nt") * (HEAD_DIM**-0.5), -jnp.inf)
    m = jnp.maximum(jnp.max(s, axis=0, keepdims=True), sink)
    e = jnp.exp(s - m)
    e_s = jnp.exp(sink - m)
    den = jnp.sum(e, axis=0, keepdims=True) + e_s
    return e / den, e_s / den


def _swa_load(q_ref, kc_ref, kp_ref, vc_ref, vp_ref, cc, sc, cp, sp):
    qr = _rope(q_ref[...], cc, sc).astype(BF16)
    kcat = jnp.concatenate([_rope(kp_ref[...], cp, sp), _rope(kc_ref[...], cc, sc)], axis=0)
    vcat = jnp.concatenate([vp_ref[...], vc_ref[...]], axis=0)
    return qr, kcat, vcat


def _swa_specs(d):
    _, off_q, off_k, off_va, _, _ = _offsets(d)
    assert off_q % d.AW == 0 and off_k % d.KVW == 0 and off_va % d.KVW == 0
    prev = lambda i: jnp.maximum(i - 1, 0)
    return [
        pl.BlockSpec(memory_space=pltpu.SMEM),
        pl.BlockSpec((CHUNK, d.AW), lambda i, o=off_q // d.AW: (i, o)),
        pl.BlockSpec((CHUNK, d.KVW), lambda i, o=off_k // d.KVW: (i, o)),
        pl.BlockSpec((CHUNK, d.KVW), lambda i, o=off_k // d.KVW: (prev(i), o)),
        pl.BlockSpec((CHUNK, d.KVW), lambda i, o=off_va // d.KVW: (i, o)),
        pl.BlockSpec((CHUNK, d.KVW), lambda i, o=off_va // d.KVW: (prev(i), o)),
        pl.BlockSpec((CHUNK, LANES), lambda i: (i, 0)),
        pl.BlockSpec((CHUNK, LANES), lambda i: (i, 0)),
        pl.BlockSpec((CHUNK, LANES), lambda i: (prev(i), 0)),
        pl.BlockSpec((CHUNK, LANES), lambda i: (prev(i), 0)),
    ]


def _swa_fwd(d, P, cos2, sin2, sinks, *, name, sched=None):
    T, AW, HQ, HKV = d.T, d.AW, d.HQ, d.HKV
    grp = HQ // HKV
    assert grp % PAIR == 0 and HKV % PAIR == 0

    def body(sink_ref, q_ref, kc_ref, kp_ref, vc_ref, vp_ref, cc_ref, sc_ref, cp_ref, sp_ref, o_ref):
        i = pl.program_id(0)
        qr, kcat, vcat = _swa_load(q_ref, kc_ref, kp_ref, vc_ref, vp_ref, cc_ref[...], sc_ref[...], cp_ref[...], sp_ref[...])
        valid = _swa_valid(i)
        for kv in range(HKV):
            col = slice((kv // PAIR) * LANES, (kv // PAIR + 1) * LANES)
            kx = [t.astype(BF16) for t in _half_variants(kcat[:, col], kv % PAIR)]
            vx = [t.astype(BF16) for t in _half_variants(vcat[:, col], kv % PAIR)]
            for pp in range(grp // PAIR):
                p = kv * (grp // PAIR) + pp
                qs = slice(p * LANES, (p + 1) * LANES)
                out = jnp.zeros((CHUNK, LANES), F32)
                for half in range(PAIR):
                    pt, _ = _swa_probs_t(kx[half], qr[:, qs], sink_ref[PAIR * p + half], valid)
                    out = out + _dot(pt.astype(BF16), vx[half], "tn")
                o_ref[:, qs] = out.astype(BF16)

    return _hosted(
        body,
        name=name,
        sched=sched,
        grid=(T // CHUNK,),
        in_specs=_swa_specs(d),
        out_specs=pl.BlockSpec((CHUNK, AW), lambda i: (i, 0)),
        out_shape=jax.ShapeDtypeStruct((T, AW), BF16),
        compiler_params=_params(("arbitrary",), [((CHUNK, AW), F32)] * 3),
    )(sinks, P, P, P, P, P, cos2, sin2, cos2, sin2)


def _swa_bwd(d, P, dyb_b, Gb, cos2, sin2, sinks, dP, *, name, sched=None):
    T, D, AW, KVW, HQ, HKV = d.T, d.D, d.AW, d.KVW, d.HQ, d.HKV
    grp = HQ // HKV
    nb = T // CHUNK
    scale = HEAD_DIM**-0.5
    off_q = _offsets(d)[1]
    assert grp % PAIR == 0 and HKV % PAIR == 0 and off_q % AW == 0

    def body(sink_ref, q_ref, kc_ref, kp_ref, vc_ref, vp_ref, cc_ref, sc_ref, cp_ref, sp_ref, dyb_ref, gb_ref, dp_in,
             dq_ref, dkv_ref, dsk_ref, acc_s, dq_s, dk_s, dv_s):
        i = pl.program_id(0)
        cc, sc, cp, sp = cc_ref[...], sc_ref[...], cp_ref[...], sp_ref[...]
        datt = _dot(dyb_ref[...], gb_ref[...], "nn").astype(BF16)
        qr, kcat, vcat = _swa_load(q_ref, kc_ref, kp_ref, vc_ref, vp_ref, cc, sc, cp, sp)
        valid = _swa_valid(i)
        lane1 = lax.broadcasted_iota(jnp.int32, (1, LANES), 1)
        lane2 = lax.broadcasted_iota(jnp.int32, (2 * CHUNK, LANES), 1)
        dsk = jnp.zeros((1, LANES), F32)
        for kvp in range(HKV // PAIR):
            col = slice(kvp * LANES, (kvp + 1) * LANES)
            dk_col = jnp.zeros((2 * CHUNK, LANES), F32)
            dv_col = jnp.zeros((2 * CHUNK, LANES), F32)
            for odd in range(PAIR):
                kv = kvp * PAIR + odd
                kx = [t.astype(BF16) for t in _half_variants(kcat[:, col], odd)]
                vx = [t.astype(BF16) for t in _half_variants(vcat[:, col], odd)]
                dk_half = [jnp.zeros((2 * CHUNK, LANES), F32) for _ in range(PAIR)]
                dv_half = [jnp.zeros((2 * CHUNK, LANES), F32) for _ in range(PAIR)]
                for pp in range(grp // PAIR):
                    p = kv * (grp // PAIR) + pp
                    qs = slice(p * LANES, (p + 1) * LANES)
                    qp, dop = qr[:, qs], datt[:, qs]
                    dq = jnp.zeros((CHUNK, LANES), F32)
                    for half in range(PAIR):
                        h = PAIR * p + half
                        pt, p_s = _swa_probs_t(kx[half], qp, sink_ref[h], valid)
                        dpt = _dot(vx[half], dop, "nt")
                        rs = jnp.sum(pt * dpt, axis=0, keepdims=True)
                        dst = (pt * (dpt - rs) * scale).astype(BF16)
                        dsk = dsk + jnp.where(lane1 == h, -jnp.sum(p_s * rs, axis=1, keepdims=True), 0.0)
                        dq = dq + _dot(dst, kx[half], "tn")
                        dk_half[half] = dk_half[half] + _dot(dst, qp, "nn")
                        dv_half[half] = dv_half[half] + _dot(pt.astype(BF16), dop, "nn")
                    dq_s[:, qs] = dq
                for acc_half, is_k in ((dk_half, True), (dv_half, False)):
                    lo = jnp.where(lane2 < HEAD_DIM, acc_half[0], 0.0)
                    hi = jnp.where(lane2 >= HEAD_DIM, acc_half[1], 0.0)
                    both = (pltpu.roll(lo, HEAD_DIM, 1) + hi) if odd else (lo + pltpu.roll(hi, HEAD_DIM, 1))
                    if is_k:
                        dk_col = dk_col + both
                    else:
                        dv_col = dv_col + both
            dk_s[:, col] = dk_col
            dv_s[:, col] = dv_col
        dq_ref[...] = _rope_bwd(dq_s[...], cc, sc).astype(BF16)
        cur = pl.ds(pl.multiple_of(i * CHUNK, CHUNK), CHUNK)
        acc_s[cur, :KVW] = _rope_bwd(dk_s[CHUNK:, :], cc, sc)
        acc_s[cur, KVW:] = dv_s[CHUNK:, :]

        @pl.when(i > 0)
        def _():
            prv = pl.ds(pl.multiple_of((i - 1) * CHUNK, CHUNK), CHUNK)
            acc_s[prv, :KVW] += _rope_bwd(dk_s[:CHUNK, :], cp, sp)
            acc_s[prv, KVW:] += dv_s[:CHUNK, :]

        _acc_store(dsk_ref, dsk, i == 0)

        @pl.when(i == nb - 1)
        def _():
            dkv_ref[...] = acc_s[...].astype(BF16)

    return _hosted(
        body,
        name=name,
        sched=sched,
        grid=(nb,),
        in_specs=_swa_specs(d) + [pl.BlockSpec((CHUNK, D), lambda i: (i, 0)), pl.BlockSpec((D, AW), lambda i: (0, 0)),
                                   pl.BlockSpec(memory_space=pl.ANY)],
        out_specs=[
            pl.BlockSpec((CHUNK, AW), lambda i, o=off_q // AW: (i, o)),
            pl.BlockSpec((T, 2 * KVW), lambda i: (0, 0)),
            pl.BlockSpec((1, LANES), lambda i: (0, 0)),
        ],
        out_shape=[
            jax.ShapeDtypeStruct(dP.shape, BF16),
            jax.ShapeDtypeStruct((T, 2 * KVW), BF16),
            jax.ShapeDtypeStruct((1, LANES), F32),
        ],
        input_output_aliases={12: 0},
        scratch_shapes=[
            pltpu.VMEM((T, 2 * KVW), F32),
            pltpu.VMEM((CHUNK, AW), F32),
            pltpu.VMEM((2 * CHUNK, KVW), F32),
            pltpu.VMEM((2 * CHUNK, KVW), F32),
        ],
        compiler_params=_params(
            ("arbitrary",), [((CHUNK, AW), F32)] * 3 + [((D, AW), BF16), ((T, 2 * KVW), BF16)], _nbytes((T, 2 * KVW), F32)
        ),
    )(sinks, P, P, P, P, P, cos2, sin2, cos2, sin2, dyb_b, Gb, dP)


def _place_cols(dst, src, off, *, name, sched=None):
    T, w = src.shape
    tm, tw = _tile(T, 512), _tile(w, 512)
    assert off % tw == 0

    def body(src_ref, dst_in, out_ref):
        out_ref[...] = src_ref[...]

    return _hosted(
        body,
        name=name,
        sched=sched,
        grid=(T // tm, w // tw),
        in_specs=[pl.BlockSpec((tm, tw), lambda i, j: (i, j)), pl.BlockSpec(memory_space=pl.ANY)],
        out_specs=pl.BlockSpec((tm, tw), lambda i, j, o=off // tw: (i, j + o)),
        out_shape=jax.ShapeDtypeStruct(dst.shape, dst.dtype),
        input_output_aliases={1: 0},
        compiler_params=_params(("arbitrary", "arbitrary"), [((tm, tw), dst.dtype)] * 2),
    )(src, dst)


def _branch_merge(d, sgu, att, Ga, Gb, P, b_gate, *, name, sched=None):
    T, D, GW, AW = d.T, d.D, d.GW, d.AW
    _, _, _, _, off_ga, off_gb = _offsets(d)
    tm, tn = _tile(T, 1024), _tile(D, 512)
    assert off_ga % tn == 0 and off_gb % tn == 0

    def body(s_ref, a_ref, ga_w, gb_w, ga_ref, gb_ref, bga_ref, bgb_ref, ya_ref, yb_ref, mg_ref):
        ya = _dot(s_ref[...], ga_w[...], "nt")
        yb = _dot(a_ref[...], gb_w[...], "nt")
        ya_ref[...] = ya
        yb_ref[...] = yb
        sa = jax.nn.sigmoid(ga_ref[...] + bga_ref[...])
        sb = jax.nn.sigmoid(gb_ref[...] + bgb_ref[...])
        mg_ref[...] = (sa * ya + sb * yb).astype(BF16)

    tile = pl.BlockSpec((tm, tn), lambda j, i: (i, j))
    return _hosted(
        body,
        name=name,
        sched=sched,
        grid=(D // tn, T // tm),
        in_specs=[
            pl.BlockSpec((tm, GW), lambda j, i: (i, 0)),
            pl.BlockSpec((tm, AW), lambda j, i: (i, 0)),
            pl.BlockSpec((tn, GW), lambda j, i: (j, 0)),
            pl.BlockSpec((tn, AW), lambda j, i: (j, 0)),
            pl.BlockSpec((tm, tn), lambda j, i, o=off_ga // tn: (i, j + o)),
            pl.BlockSpec((tm, tn), lambda j, i, o=off_gb // tn: (i, j + o)),
            pl.BlockSpec((1, tn), lambda j, i: (0, j)),
            pl.BlockSpec((1, tn), lambda j, i, o=D // tn: (0, j + o)),
        ],
        out_specs=[tile, tile, tile],
        out_shape=[jax.ShapeDtypeStruct((T, D), F32), jax.ShapeDtypeStruct((T, D), F32), jax.ShapeDtypeStruct((T, D), BF16)],
        compiler_params=_params(
            ("arbitrary", "arbitrary"),
            [((tm, GW), BF16), ((tm, AW), BF16), ((tn, GW), BF16), ((tn, AW), BF16)] + [((tm, tn), F32)] * 5,
        ),
    )(sgu, att, Ga, Gb, P, P, b_gate, b_gate)


def _xattn_probs(qh, kh):
    s = _dot(qh, kh, "nt") * (X_HEAD_DIM**-0.5)
    e = jnp.exp(s - jnp.max(s, axis=1, keepdims=True))
    return e / jnp.sum(e, axis=1, keepdims=True)


def _xattn_fwd(d, X, Xb, kv, Gxq, Gxo, ln_g, ln_b, *, name, sched=None):
    T, D, XW, XH, MEM = d.T, d.D, d.XW, d.XH, d.MEM
    tm = _tile(T, 256)

    def body(x_ref, xb_ref, kv_ref, wq_ref, wo_ref, g_ref, b_ref, y_ref, yb_ref, xh_ref, rs_ref, q_ref, o_ref, o_s):
        q = _dot(xb_ref[...], wq_ref[...], "nn").astype(BF16)
        q_ref[...] = q
        for h in range(XH):
            hs = slice(h * X_HEAD_DIM, (h + 1) * X_HEAD_DIM)
            p = _xattn_probs(q[:, hs], kv_ref[:, hs]).astype(BF16)
            o_s[:, hs] = _dot(p, kv_ref[:, XW + h * X_HEAD_DIM : XW + (h + 1) * X_HEAD_DIM], "nn").astype(BF16)
        o = o_s[...]
        o_ref[...] = o
        y, xhat, rstd = _ln_fwd(ALPHA * x_ref[...] + _dot(o, wo_ref[...], "nt"), g_ref[...], b_ref[...])
        y_ref[...] = y
        yb_ref[...] = y.astype(BF16)
        xh_ref[...] = xhat
        rs_ref[...] = rstd

    row = pl.BlockSpec((tm, D), lambda i: (i, 0))
    nar = pl.BlockSpec((tm, XW), lambda i: (i, 0))
    vec = pl.BlockSpec((1, D), lambda i: (0, 0))
    return _hosted(
        body,
        name=name,
        sched=sched,
        grid=(T // tm,),
        in_specs=[
            row,
            row,
            pl.BlockSpec((MEM, 2 * XW), lambda i: (0, 0)),
            pl.BlockSpec((D, XW), lambda i: (0, 0)),
            pl.BlockSpec((D, XW), lambda i: (0, 0)),
            vec,
            vec,
        ],
        out_specs=[row, row, row, pl.BlockSpec((tm, 1), lambda i: (i, 0)), nar, nar],
        out_shape=[
            jax.ShapeDtypeStruct((T, D), F32),
            jax.ShapeDtypeStruct((T, D), BF16),
            jax.ShapeDtypeStruct((T, D), F32),
            jax.ShapeDtypeStruct((T, 1), F32),
            jax.ShapeDtypeStruct((T, XW), BF16),
            jax.ShapeDtypeStruct((T, XW), BF16),
        ],
        scratch_shapes=[pltpu.VMEM((tm, XW), BF16)],
        compiler_params=_params(("arbitrary",), [((tm, D), F32)] * 4 + [((D, XW), BF16)] * 2),
    )(X, Xb, kv, Gxq, Gxo, ln_g, ln_b)


def _xattn_bwd(d, dz, dzb, q_b, kv, Gxq, Gxo, xhat, rstd, ln_g, *, name, sched=None):
    T, D, XW, XH, MEM = d.T, d.D, d.XW, d.XH, d.MEM
    tm = _tile(T, 256)
    scale = X_HEAD_DIM**-0.5

    def body(dz_ref, dzb_ref, q_ref, kv_ref, wq_ref, wo_ref, xh_ref, rs_ref, g_ref, dx_ref, dxb_ref, dg_ref, db_ref, dq_ref, dkv_ref, dq_s):
        first = pl.program_id(0) == 0
        do = _dot(dzb_ref[...], wo_ref[...], "nn").astype(BF16)
        for h in range(XH):
            hs = slice(h * X_HEAD_DIM, (h + 1) * X_HEAD_DIM)
            vs = slice(XW + h * X_HEAD_DIM, XW + (h + 1) * X_HEAD_DIM)
            kh, vh, qh, doh = kv_ref[:, hs], kv_ref[:, vs], q_ref[:, hs], do[:, hs]
            p = _xattn_probs(qh, kh)
            dp = _dot(doh, vh, "nt")
            ds = (p * (dp - jnp.sum(p * dp, axis=1, keepdims=True)) * scale).astype(BF16)
            dq_s[:, hs] = _dot(ds, kh, "nn").astype(BF16)
            _acc_store(dkv_ref.at[h], _dot(ds, qh, "tn"), first)
            _acc_store(dkv_ref.at[XH + h], _dot(p.astype(BF16), doh, "tn"), first)
        dq = dq_s[...]
        dq_ref[...] = dq
        dy = ALPHA * dz_ref[...] + _dot(dq, wq_ref[...], "nt")
        xh = xh_ref[...]
        dz_in = _ln_bwd_vals(dy, xh, rs_ref[...], g_ref[...])
        dx_ref[...] = dz_in
        dxb_ref[...] = dz_in.astype(BF16)
        _acc_store(dg_ref, jnp.sum(dy * xh, axis=0, keepdims=True), first)
        _acc_store(db_ref, jnp.sum(dy, axis=0, keepdims=True), first)

    row = pl.BlockSpec((tm, D), lambda i: (i, 0))
    nar = pl.BlockSpec((tm, XW), lambda i: (i, 0))
    vec = pl.BlockSpec((1, D), lambda i: (0, 0))
    return _hosted(
        body,
        name=name,
        sched=sched,
        grid=(T // tm,),
        in_specs=[
            row,
            row,
            nar,
            pl.BlockSpec((MEM, 2 * XW), lambda i: (0, 0)),
            pl.BlockSpec((D, XW), lambda i: (0, 0)),
            pl.BlockSpec((D, XW), lambda i: (0, 0)),
            row,
            pl.BlockSpec((tm, 1), lambda i: (i, 0)),
            vec,
        ],
        out_specs=[row, row, vec, vec, nar, pl.BlockSpec((2 * XH, MEM, X_HEAD_DIM), lambda i: (0, 0, 0))],
        out_shape=[
            jax.ShapeDtypeStruct((T, D), F32),
            jax.ShapeDtypeStruct((T, D), BF16),
            jax.ShapeDtypeStruct((1, D), F32),
            jax.ShapeDtypeStruct((1, D), F32),
            jax.ShapeDtypeStruct((T, XW), BF16),
            jax.ShapeDtypeStruct((2 * XH, MEM, X_HEAD_DIM), F32),
        ],
        scratch_shapes=[pltpu.VMEM((tm, XW), BF16)],
        compiler_params=_params(("arbitrary",), [((tm, D), F32)] * 5 + [((D, XW), BF16)] * 2),
    )(dz, dzb, q_b, kv, Gxq, Gxo, xhat, rstd, ln_g)


def _resid_ln_outs():
    return (("mn", F32), ("mn", BF16), ("mn", F32), ("m1", F32))


def _layer_fwd(d, X, Xb, memb, G, S, cos2, sin2, tag, sched=None):
    D = d.D
    mm = functools.partial(_matmul, sched=sched)
    P = mm(Xb, G["in"], "nt", name=f"proj_{tag}", tm=1024, tn=_tile(d.IN, 1280))
    sgu = _gmlp_fwd(d, P, S["ln_v_g"], S["ln_v_b"], S["w_s"], S["b_sT"], name=f"gmlp_fwd_{tag}", sched=sched)
    att = _swa_fwd(d, P, cos2, sin2, S["sinks"], name=f"swa_fwd_{tag}", sched=sched)
    ya, yb, merged = _branch_merge(d, sgu, att, G["a"], G["b"], P, S["b_gate"], name=f"merge_{tag}", sched=sched)
    X1, X1b, xh1, rs1 = mm(
        merged, G["o"], "nn", name=f"oproj_ln_{tag}", tn=D, tk=2048, out_defs=_resid_ln_outs(), epilogue=_ep_resid_ln,
        extras=((X, "mn", 0), (S["ln1_g"], "row", 0), (S["ln1_b"], "row", 0)),
    )
    kv = mm(memb, G["xkv"], "nn", name=f"xkv_{tag}", out_defs=(("mn", BF16),))
    X2, X2b, xh2, rs2, q_b, o_b = _xattn_fwd(d, X1, X1b, kv, G["xq"], G["xo"], S["ln2_g"], S["ln2_b"], name=f"xattn_fwd_{tag}", sched=sched)
    H, A = mm(X2b, G["up"], "nt", name=f"up_{tag}", tm=1024, out_defs=(("mn", F32), ("mn", BF16)), epilogue=_ep_relu2)
    X3, X3b, xh3, rs3 = mm(
        A, G["down"], "nn", name=f"down_ln_{tag}", tn=D, tk=2048, out_defs=_resid_ln_outs(), epilogue=_ep_resid_ln,
        extras=((X2, "mn", 0), (S["ln3_g"], "row", 0), (S["ln3_b"], "row", 0)),
    )
    saved = dict(Xb=Xb, P=P, sgu=sgu, att=att, ya=ya, yb=yb, merged=merged, X1b=X1b, xh1=xh1, rs1=rs1, kv=kv, q_b=q_b,
                 o_b=o_b, X2b=X2b, xh2=xh2, rs2=rs2, H=H, A=A, xh3=xh3, rs3=rs3)
    return X3, X3b, saved


def _layer_bwd(d, dXout, sv, memb, G, S, cos2, sin2, tag, sched=None, on_grad=None, target=None):
    D = d.D
    mm = functools.partial(_matmul, sched=sched)
    emit = on_grad if on_grad is not None else (lambda n, g: None)
    wide = dict(tn=D)
    _, _, off_k, _, off_ga, off_gb = _offsets(d)
    bf = (("mn", BF16),)
    dz3, dz3b, dg3, db3, *loss = _ln_bwd(dXout, sv["xh3"], sv["rs3"], S["ln3_g"], name=f"ln3_bwd_{tag}", sched=sched, target=target)
    dHb = mm(dz3b, G["down"], "nt", name=f"down_dx_{tag}", tm=1024, out_defs=bf, epilogue=_ep_relu2_bwd, extras=((sv["H"], "mn", 0),))
    g_down = mm(sv["A"], dz3b, "tn", name=f"down_dw_{tag}", **wide)
    emit("down", g_down)
    ln_bwd_outs = (("mn", F32), ("mn", BF16), ("acc", F32), ("acc", F32))
    dz2, dz2b, dg2, db2 = mm(
        dHb, G["up"], "nn", name=f"up_dx_{tag}", tn=D, tk=1024, out_defs=ln_bwd_outs, epilogue=_ep_resid_ln_bwd,
        extras=((dz3, "mn", 0), (sv["xh2"], "mn", 0), (sv["rs2"], "m1", 0), (S["ln2_g"], "row", 0)),
    )
    g_up = mm(dHb, sv["X2b"], "tn", name=f"up_dw_{tag}", **wide)
    emit("up", g_up)
    dz1, dz1b, dg1, db1, dq_b, dkv = _xattn_bwd(d, dz2, dz2b, sv["q_b"], sv["kv"], G["xq"], G["xo"], sv["xh1"], sv["rs1"], S["ln1_g"],
                                               name=f"xattn_bwd_{tag}", sched=sched)
    g_xo = mm(dz2b, sv["o_b"], "tn", name=f"xo_dw_{tag}")
    emit("xo", g_xo)
    g_xq = mm(sv["X1b"], dq_b, "tn", name=f"xq_dw_{tag}")
    emit("xq", g_xq)
    dkv_b = dkv.transpose(1, 0, 2).reshape(d.MEM, 2 * d.XW).astype(BF16)
    g_xkv = mm(memb, dkv_b, "tn", name=f"xkv_dw_{tag}")
    emit("xkv", g_xkv)
    dya_b, dyb_b, dP, dgb_b, dbga, dbgb = mm(
        dz1b, G["o"], "nt", name=f"oproj_dx_{tag}", tm=1024, tn=512,
        out_defs=(("mn", BF16), ("mn", BF16), ("cols", BF16, d.IN, off_ga), ("mn", BF16), ("acc", F32), ("acc", F32)),
        epilogue=_ep_gate_bwd,
        extras=((sv["P"], "mn", off_ga), (sv["P"], "mn", off_gb), (sv["ya"], "mn", 0), (sv["yb"], "mn", 0),
                (S["b_gate"], "row", 0), (S["b_gate"], "row", D)),
    )
    g_o = mm(sv["merged"], dz1b, "tn", name=f"oproj_dw_{tag}", **wide)
    emit("o", g_o)
    dP = _place_cols(dP, dgb_b, off_gb, name=f"place_dgb_{tag}")
    dP, dwm, dbsT, dlvg, dlvb = _gmlp_bwd(d, sv["P"], dya_b, G["a"], S["ln_v_g"], S["ln_v_b"], S["w_s"], S["b_sT"], dP, name=f"gmlp_bwd_{tag}", sched=sched)
    g_a = mm(dya_b, sv["sgu"], "tn", name=f"bra_dw_{tag}")
    emit("a", g_a)
    dP, dkvr_b, dsk = _swa_bwd(d, sv["P"], dyb_b, G["b"], cos2, sin2, S["sinks"], dP, name=f"swa_bwd_{tag}", sched=sched)
    g_b = mm(dyb_b, sv["att"], "tn", name=f"brb_dw_{tag}")
    emit("b", g_b)
    dP = _place_cols(dP, dkvr_b, off_k, name=f"place_dkv_{tag}")
    small = dict(b_gate=jnp.concatenate([dbga, dbgb], axis=1), ln_v_g=dlvg, ln_v_b=dlvb, w_s=dwm, b_sT=dbsT, sinks=dsk,
                 ln1_g=dg1, ln1_b=db1, ln2_g=dg2, ln2_b=db2, ln3_g=dg3, ln3_b=db3)
    emit("small", small)
    g_in = mm(dP, sv["Xb"], "tn", name=f"proj_dw_{tag}", **wide)
    emit("in", g_in)
    dXin = mm(dP, G["in"], "nn", name=f"proj_dx_{tag}", tn=D, tk=_tile(d.IN, 1920), epilogue=_ep_resid, extras=((dz1, "mn", 0),))
    big = {"in": g_in, "a": g_a, "b": g_b, "o": g_o, "xq": g_xq, "xkv": g_xkv, "xo": g_xo, "up": g_up, "down": g_down}
    return dXin, big, small, (loss[0] if loss else None)


def _rope_tables(T):
    inv = 1.0 / (ROPE_THETA ** (jnp.arange(0, HEAD_DIM, 2, dtype=F32) / HEAD_DIM))
    ang = jnp.arange(T, dtype=F32)[:, None] * inv[None, :]
    cos, sin = jnp.cos(ang), jnp.sin(ang)
    reps = LANES // HEAD_DIM
    return jnp.concatenate([cos, cos] * reps, axis=1), jnp.concatenate([-sin, sin] * reps, axis=1)


def _local_step(d, x, mem, target, Gs, Ss, sched=None, on_grad=None):
    cos2, sin2 = _rope_tables(d.T)
    memb = mem.astype(BF16)
    X, Xb = x, x.astype(BF16)
    saved = []
    for l in range(DEPTH):
        X, Xb, sv = _layer_fwd(d, X, Xb, memb, Gs[l], Ss[l], cos2, sin2, f"l{l}", sched)
        saved.append(sv)
    bigs, smalls = [None] * DEPTH, [None] * DEPTH
    dX, loss = X, None
    for l in reversed(range(DEPTH)):
        emit = None if on_grad is None else functools.partial(on_grad, l)
        head = target if l == DEPTH - 1 else None
        dX, bigs[l], smalls[l], got = _layer_bwd(d, dX, saved[l], memb, Gs[l], Ss[l], cos2, sin2, f"l{l}", sched, emit, head)
        loss = got if got is not None else loss
    return loss, dX, bigs, smalls


def _place():
    x, y, c = lax.axis_index("x"), lax.axis_index("y"), lax.axis_index("c")
    return x, y, c, [(1 - x, y), (x, 1 - y), (1 - x, 1 - y)]


def _all_gather(shards, *, name, sched=None):
    n = len(shards)
    any_spec = pl.BlockSpec(memory_space=pl.ANY)

    def body(*refs):
        ins, outs = refs[:n], refs[n : 2 * n]
        send_sems, recv_sems, local_sems = refs[2 * n :]
        x, y, c, chips = _place()
        me, sibling = (x, y, c), (x, y, 1 - c)

        def rows(w, p):
            r = ins[w].shape[0]
            return outs[w].at[pl.ds((4 * p[0] + 2 * p[1] + p[2]) * r, r), :]

        def copy(w, k, block, to, src=None):
            return pltpu.make_async_remote_copy(
                src_ref=rows(w, block) if src is None else src, dst_ref=rows(w, block),
                send_sem=send_sems.at[7 * w + k], recv_sem=recv_sems.at[7 * w + k], device_id=to, device_id_type=MESH)

        mine = [pltpu.make_async_copy(ins[w], rows(w, me), local_sems.at[w]) for w in range(n)]
        for cp in mine:
            cp.start()
        first = []
        for w in range(n):
            first.append(copy(w, 0, me, sibling, src=ins[w]))
            first += [copy(w, 1 + j, me, (*chip, c), src=ins[w]) for j, chip in enumerate(chips)]
        for cp in first:
            cp.start()
        passed = []
        for j, chip in enumerate(chips):
            for w in range(n):
                copy(w, 1 + j, (*chip, c), me).wait_recv()
                fwd = copy(w, 4 + j, (*chip, c), sibling)
                fwd.start()
                passed.append(fwd)
        for w in range(n):
            copy(w, 0, sibling, me).wait_recv()
            for j, chip in enumerate(chips):
                copy(w, 4 + j, (*chip, 1 - c), me).wait_recv()
        for cp in first + passed:
            cp.wait_send()
        for cp in mine:
            cp.wait()

    return _hosted(
        body,
        name=name,
        sched=sched,
        in_specs=[any_spec] * n,
        out_specs=[any_spec] * n,
        out_shape=[jax.ShapeDtypeStruct((N_DEV * s.shape[0], s.shape[1]), s.dtype) for s in shards],
        scratch_shapes=[pltpu.SemaphoreType.DMA((7 * n,)), pltpu.SemaphoreType.DMA((7 * n,)), pltpu.SemaphoreType.DMA((n,))],
    )(*shards)


class _Task:
    def __init__(self, ins, out_shapes, n_remote, n_local, plan, done, aliases=None, sibling=False, chips=False):
        self.ins, self.out_shapes, self.n_remote, self.n_local = list(ins), list(out_shapes), n_remote, n_local
        self.plan, self.done, self.aliases = plan, done, dict(aliases or {})
        self.sibling, self.chips = sibling, chips


class _Sched:
    def __init__(self):
        self.pending = {}

    def at(self, host, task):
        self.pending.setdefault(host, []).append(task)

    def take(self, host):
        return self.pending.pop(host, [])


def _in_hbm(out_shape):
    if isinstance(out_shape, (list, tuple)):
        return [_in_hbm(s) for s in out_shape]
    return pltpu.HBM(out_shape.shape, out_shape.dtype)


def _hosted(body, *, name, sched=None, tasks=(), grid=(), in_specs=None, out_specs=None, out_shape=(), scratch_shapes=(),
            compiler_params=None, input_output_aliases=None):
    tasks = list(tasks) + (sched.take(name) if sched is not None else [])
    scratch_shapes = list(scratch_shapes)
    kwargs = dict(name=name)
    core_aliases = dict(input_output_aliases or {})
    if grid:
        kwargs["grid"] = grid
    if compiler_params is not None:
        kwargs["compiler_params"] = compiler_params
    if not tasks:
        if in_specs is not None:
            kwargs.update(in_specs=in_specs, out_specs=out_specs)
        return pl.pallas_call(body, out_shape=_in_hbm(out_shape), scratch_shapes=scratch_shapes, input_output_aliases=core_aliases, **kwargs)
    assert in_specs is not None and out_specs is not None, name
    with_sibling, with_chips = any(t.sibling for t in tasks), any(t.chips for t in tasks)
    collective_id = {(True, False): 1, (False, True): 2, (True, True): 3}[(with_sibling, with_chips)]
    kwargs["compiler_params"] = dataclasses.replace(
        compiler_params if compiler_params is not None else pltpu.CompilerParams(), collective_id=collective_id)
    single = not isinstance(out_shape, (list, tuple))
    core_shape = [out_shape] if single else list(out_shape)
    core_specs = [out_specs] if single else list(out_specs)
    in_specs = list(in_specs)
    n_in, n_out, n_scr = len(in_specs), len(core_shape), len(scratch_shapes)
    t_ins = [a for t in tasks for a in t.ins]
    t_outs = [s for t in tasks for s in t.out_shapes]
    n_rem = sum(t.n_remote for t in tasks)
    n_loc = sum(t.n_local for t in tasks)
    aliases, pos_in, pos_out = dict(core_aliases), n_in, n_out
    for t in tasks:
        for a, b in t.aliases.items():
            aliases[pos_in + a] = pos_out + b
        pos_in += len(t.ins)
        pos_out += len(t.out_shapes)
    any_spec = pl.BlockSpec(memory_space=pl.ANY)

    def wrapped(*refs):
        core_in, t_in = refs[:n_in], refs[n_in : n_in + len(t_ins)]
        o0 = n_in + len(t_ins)
        core_out, t_out = refs[o0 : o0 + n_out], refs[o0 + n_out : o0 + n_out + len(t_outs)]
        s0 = o0 + n_out + len(t_outs)
        core_scr = refs[s0 : s0 + n_scr]
        send_sems, recv_sems, local_sems = refs[s0 + n_scr :]
        remote, local = [], []
        pi = po = pr = pq = 0
        for t in tasks:
            r, q = t.plan(t_in[pi : pi + len(t.ins)], t_out[po : po + len(t.out_shapes)], send_sems, recv_sems, local_sems, pr, pq)
            assert len(r) == t.n_remote and len(q) == t.n_local
            remote += r
            local += q
            pi, po, pr, pq = pi + len(t.ins), po + len(t.out_shapes), pr + t.n_remote, pq + t.n_local

        def start():
            x, y, c, chips = _place()
            peers = ([(x, y, 1 - c)] if with_sibling else []) + ([(*chip, c) for chip in chips] if with_chips else [])
            barrier = pltpu.get_barrier_semaphore()
            for peer in peers:
                pl.semaphore_signal(barrier, inc=1, device_id=peer, device_id_type=MESH)
            pl.semaphore_wait(barrier, len(peers))
            for cp in local + remote:
                cp.start()

        def finish():
            for cp in remote + local:
                cp.wait()

        if grid:
            ids = [pl.program_id(a) for a in range(len(grid))]
            first = functools.reduce(jnp.logical_and, [i == 0 for i in ids])
            last = functools.reduce(jnp.logical_and, [i == g - 1 for i, g in zip(ids, grid)])
            pl.when(first)(start)
            body(*core_in, *core_out, *core_scr)
            pl.when(last)(finish)
        else:
            start()
            body(*core_in, *core_out, *core_scr)
            finish()

    call = pl.pallas_call(
        wrapped,
        in_specs=in_specs + [any_spec] * len(t_ins),
        out_specs=core_specs + [any_spec] * len(t_outs),
        out_shape=_in_hbm(core_shape + t_outs),
        scratch_shapes=scratch_shapes
        + [pltpu.SemaphoreType.DMA((n_rem,)), pltpu.SemaphoreType.DMA((n_rem,)), pltpu.SemaphoreType.DMA((max(n_loc, 1),))],
        input_output_aliases=aliases,
        **kwargs,
    )

    def run(*operands):
        outs = call(*operands, *t_ins)
        po = n_out
        for t in tasks:
            t.done(list(outs[po : po + len(t.out_shapes)]))
            po += len(t.out_shapes)
        return outs[0] if single else list(outs[:n_out])

    return run


def _comm_only(tasks, *, name):
    _hosted(lambda: None, name=name, tasks=tasks, in_specs=[], out_shape=[], out_specs=[])()


def _rows(ref, block, n_blocks):
    r = ref.shape[0] // n_blocks
    return ref.at[pl.ds(block * r, r), :]


def _remote(src, dst, send_sems, recv_sems, k, to):
    return pltpu.make_async_remote_copy(src_ref=src, dst_ref=dst, send_sem=send_sems.at[k], recv_sem=recv_sems.at[k],
                                        device_id=to, device_id_type=MESH)


def _gather_stage1(shards, done, part=(0, 1, 1), into=None):
    n = len(shards)
    k0, k1, n_parts = part

    def plan(ins, outs, send_sems, recv_sems, local_sems, pr, pq):
        x, y, c, chips = _place()
        mine = 4 * x + 2 * y + c
        remote, local = [], []
        for w in range(n):
            r = shards[w].shape[0]
            rp = r // n_parts
            src = ins[w].at[pl.ds(k0 * rp, (k1 - k0) * rp), :]
            dst = outs[w].at[pl.ds(mine * r + k0 * rp, (k1 - k0) * rp), :]
            local.append(pltpu.make_async_copy(src, dst, local_sems.at[pq + w]))
            for j, to in enumerate([(x, y, 1 - c)] + [(*chip, c) for chip in chips]):
                remote.append(_remote(src, dst, send_sems, recv_sems, pr + 4 * w + j, to))
        return remote, local

    shapes = [jax.ShapeDtypeStruct((N_DEV * s.shape[0], s.shape[1]), s.dtype) for s in shards]
    if into is None:
        return _Task(shards, shapes, 4 * n, n, plan, done, sibling=True, chips=True)
    return _Task(list(shards) + list(into), shapes, 4 * n, n, plan, done, aliases={n + w: w for w in range(n)}, sibling=True, chips=True)


def _gather_stage2(partial, done):
    n = len(partial)

    def plan(ins, outs, send_sems, recv_sems, local_sems, pr, pq):
        x, y, c, chips = _place()
        remote = []
        for w in range(n):
            for j, chip in enumerate(chips):
                rows = _rows(outs[w], 4 * chip[0] + 2 * chip[1] + c, N_DEV)
                remote.append(_remote(rows, rows, send_sems, recv_sems, pr + 3 * w + j, (x, y, 1 - c)))
        return remote, []

    shapes = [jax.ShapeDtypeStruct(p.shape, p.dtype) for p in partial]
    return _Task(partial, shapes, 3 * n, 0, plan, done, aliases={w: w for w in range(n)}, sibling=True)


def _sibling_stage(grads, done):
    n = len(grads)

    def plan(ins, outs, send_sems, recv_sems, local_sems, pr, pq):
        x, y, c, _ = _place()
        remote = []
        for w in range(n):
            for k in range(4):
                src = _rows(ins[w], 4 * (k // 2) + 2 * (k % 2) + (1 - c), N_DEV)
                remote.append(_remote(src, _rows(outs[w], k, 4), send_sems, recv_sems, pr + 4 * w + k, (x, y, 1 - c)))
        return remote, []

    shapes = [jax.ShapeDtypeStruct((g.shape[0] // 2, g.shape[1]), g.dtype) for g in grads]
    return _Task(grads, shapes, 4 * n, 0, plan, done, sibling=True)


def _chips_stage(sends, done, part=(0, 1), into=None):
    n = len(sends)
    k, n_parts = part

    def plan(ins, outs, send_sems, recv_sems, local_sems, pr, pq):
        x, y, c, chips = _place()
        remote = []
        for w in range(n):
            r = sends[w].shape[0] // 3
            rp = r // n_parts
            for j, chip in enumerate(chips):
                rows = pl.ds(j * r + k * rp, rp)
                remote.append(_remote(ins[w].at[rows, :], outs[w].at[rows, :], send_sems, recv_sems, pr + 3 * w + j, (*chip, c)))
        return remote, []

    shapes = [jax.ShapeDtypeStruct(s.shape, s.dtype) for s in sends]
    if into is None:
        return _Task(sends, shapes, 3 * n, 0, plan, done, chips=True)
    return _Task(list(sends) + list(into), shapes, 3 * n, 0, plan, done, aliases={n + w: w for w in range(n)}, chips=True)


def _pair_sum(grad, got, place, *, name):
    R, C = grad.shape
    r = R // N_DEV
    tr = _tile(r, 256, 16)
    nt = r // tr

    def chip_of(s, pr):
        fx = jnp.where((s == 1) | (s == 3), 1, 0)
        fy = jnp.where(s >= 2, 1, 0)
        return pr[0] ^ fx, pr[1] ^ fy

    def grad_map(s, t, pr):
        kx, ky = chip_of(s, pr)
        return ((4 * kx + 2 * ky + pr[2]) * nt + t, 0)

    def got_map(s, t, pr):
        kx, ky = chip_of(s, pr)
        return ((2 * kx + ky) * nt + t, 0)

    def body(pr, g_ref, r_ref, own_ref, send_ref):
        s = pl.program_id(0)
        tot = g_ref[...] + r_ref[...]

        @pl.when(s == 0)
        def _():
            own_ref[...] = tot

        @pl.when(s > 0)
        def _():
            send_ref[...] = tot.astype(BF16)

    return pl.pallas_call(
        body,
        name=name,
        grid_spec=pltpu.PrefetchScalarGridSpec(
            num_scalar_prefetch=1,
            grid=(4, nt),
            in_specs=[pl.BlockSpec((tr, C), grad_map), pl.BlockSpec((tr, C), got_map)],
            out_specs=[
                pl.BlockSpec((tr, C), lambda s, t, pr: (jnp.where(s == 0, t, nt - 1), 0)),
                pl.BlockSpec((tr, C), lambda s, t, pr: (jnp.where(s == 0, 0, (s - 1) * nt + t), 0)),
            ],
        ),
        out_shape=[jax.ShapeDtypeStruct((r, C), F32), jax.ShapeDtypeStruct((3 * r, C), BF16)],
        compiler_params=_params(("arbitrary", "arbitrary"), [((tr, C), F32)] * 4),
    )(place, grad, got)


def _adam_vals(w, g, m, v):
    m = ADAM_B1 * m + (1.0 - ADAM_B1) * g
    v = ADAM_B2 * v + (1.0 - ADAM_B2) * (g * g)
    m_hat = m / (1.0 - ADAM_B1**ADAM_STEP)
    v_hat = v / (1.0 - ADAM_B2**ADAM_STEP)
    delta = -ADAM_LR * (m_hat / (jnp.sqrt(v_hat) + ADAM_EPS) + ADAM_WD * w)
    return delta, m, v


def _adam_big(w3, m3, v3, own, got, layer, transposed, prev, *, name, sched=None):
    _, A, B = w3.shape
    ta = _tile(A, 256, 16)
    nt = A // ta
    b_pad = (-B) % LANES

    def body(*refs):
        w_ref, m_ref, v_ref, own_ref, g1_ref, g2_ref, g3_ref = refs[:7]
        g_ref, d_ref, nm_ref, nv_ref = refs[-4:]
        g = ((own_ref[...] + g1_ref[...].astype(F32)) + g2_ref[...].astype(F32)) + g3_ref[...].astype(F32)
        if transposed:
            if b_pad:
                g = jnp.concatenate([g, jnp.zeros((b_pad, ta), F32)], axis=0)
            g = g.T[:, :B]
        g_ref[...] = g
        d_ref[...], nm_ref[...], nv_ref[...] = _adam_vals(w_ref[...], g, m_ref[...], v_ref[...])

    nat = pl.BlockSpec((None, ta, B), lambda t: (layer, t, 0))
    if transposed:
        parts = [pl.BlockSpec((B, ta), lambda t: (0, t))] + [pl.BlockSpec((B, ta), lambda t, j=j: (j, t)) for j in range(3)]
    else:
        parts = [pl.BlockSpec((ta, B), lambda t: (t, 0))] + [pl.BlockSpec((ta, B), lambda t, j=j: (j * nt + t, 0)) for j in range(3)]
    keep = [] if prev is None else list(prev)
    outs = _hosted(
        body,
        name=name,
        sched=sched,
        grid=(nt,),
        in_specs=[nat, nat, nat] + parts + [pl.BlockSpec(memory_space=pl.ANY)] * len(keep),
        out_specs=[nat] * 4,
        out_shape=[jax.ShapeDtypeStruct(w3.shape, F32)] * 4,
        input_output_aliases={7 + k: k for k in range(len(keep))},
        compiler_params=_params(("arbitrary",), [((ta, B), F32)] * 10),
    )(w3, m3, v3, own, got, got, got, *keep)
    return list(outs)


def _adam_small(w, m, v, parts, *, name, sched=None):
    R = w.shape[0]

    def body(w_ref, m_ref, v_ref, p_ref, g_ref, d_ref, nm_ref, nv_ref):
        g = p_ref[pl.ds(0, R), :]
        for k in range(1, N_DEV):
            g = g + p_ref[pl.ds(k * R, R), :]
        g_ref[...] = g
        d_ref[...], nm_ref[...], nv_ref[...] = _adam_vals(w_ref[...], g, m_ref[...], v_ref[...])

    return _hosted(
        body,
        name=name,
        sched=sched,
        out_shape=[jax.ShapeDtypeStruct((R, LANES), F32)] * 4,
        compiler_params=_params(None, [((R, LANES), F32)] * 16),
    )(w, m, v, parts)


SMALL_NAMES = ("b_gate", "ln_v_g", "ln_v_b", "w_s", "b_s", "sinks", "ln1_g", "ln1_b", "ln2_g", "ln2_b", "ln3_g", "ln3_b")
PACK_ALIGN = 8 * LANES


def _pack(arrays):
    flat = []
    for a in arrays:
        a = a.reshape(-1)
        flat.append(jnp.pad(a, (0, (-a.shape[0]) % PACK_ALIGN)))
    return jnp.concatenate(flat).reshape(-1, LANES)


def _unpack(packed, shapes):
    flat = packed.reshape(-1)
    out, pos = [], 0
    for s in shapes:
        n = 1
        for e in s:
            n *= e
        out.append(flat[pos : pos + n].reshape(s))
        pos += n + (-n) % PACK_ALIGN
    return out


BIG = (("in", "w_in", True), ("a", "w_br_a", True), ("b", "w_br_b", True), ("o", "w_o", False), ("xq", "w_xq", False),
       ("xkv", "w_xkv", False), ("xo", "w_xo", True), ("up", "w_up", True), ("down", "w_down", False))


SMALL_BIG = ("a", "b", "o", "xq", "xkv", "xo")
GATHER_PLAN = (
    (0, ("in",), None, None),
    (0, SMALL_BIG, ("proj_l0",), "swa_fwd_l0"),
    (0, ("up",), ("swa_fwd_l0", "merge_l0"), "oproj_ln_l0"),
    (0, ("down",), ("oproj_ln_l0", "xattn_fwd_l0"), "up_l0"),
    (1, ("in",), ("up_l0",), "down_ln_l0"),
    (1, SMALL_BIG, ("down_ln_l0",), "proj_l1"),
    (1, ("up",), ("down_ln_l0", ("proj_l1", 3)), "swa_fwd_l1"),
    (1, ("down",), (("swa_fwd_l1", 2), "merge_l1", "oproj_ln_l1"), "xattn_fwd_l1"),
)
EARLY_SMALL_BIG = ("o", "xq", "xkv", "xo")
LATE_SMALL_BIG = ("a", "b")
GRAD_HOSTS = {
    1: {"down": ("xattn_bwd_l1", "swa_bwd_l1"), "up": ("xattn_bwd_l1", ("gmlp_bwd_l1", "proj_dw_l1")),
        **{g: ("gmlp_bwd_l1", "proj_dx_l1") for g in EARLY_SMALL_BIG}, **{g: ("proj_dw_l1", "proj_dx_l1") for g in LATE_SMALL_BIG},
        "in": ("proj_dx_l1", ("down_dx_l0", "down_dw_l0"))},
    0: {"down": ("xattn_bwd_l0", ("oproj_dx_l0", "gmlp_bwd_l0")), "up": ("xattn_bwd_l0", "swa_bwd_l0"),
        **{g: ("gmlp_bwd_l0", "proj_dw_l0") for g in EARLY_SMALL_BIG}, **{g: ("proj_dw_l0", "proj_dx_l0") for g in LATE_SMALL_BIG},
        "in": (None, "proj_dx_l0")},
}
SMALL_GRAD_HOSTS = ("proj_dw_l0", "proj_dx_l0")


def kernel(x, mem, w_in, b_gate, ln_v_g, ln_v_b, w_s, b_s, sinks, w_br_a, w_br_b, w_o, ln1_g, ln1_b, w_xq, w_xkv, w_xo, ln2_g, ln2_b, w_up, w_down, ln3_g, ln3_b, loss_target, m_w_in, m_b_gate, m_ln_v_g, m_ln_v_b, m_w_s, m_b_s, m_sinks, m_w_br_a, m_w_br_b, m_w_o, m_ln1_g, m_ln1_b, m_w_xq, m_w_xkv, m_w_xo, m_ln2_g, m_ln2_b, m_w_up, m_w_down, m_ln3_g, m_ln3_b, v_w_in, v_b_gate, v_ln_v_g, v_ln_v_b, v_w_s, v_b_s, v_sinks, v_w_br_a, v_w_br_b, v_w_o, v_ln1_g, v_ln1_b, v_w_xq, v_w_xkv, v_w_xo, v_ln2_g, v_ln2_b, v_w_up, v_w_down, v_ln3_g, v_ln3_b):
    given = dict(locals())
    T, D = x.shape[1], x.shape[2]
    IN, GW, AW, XW, DFF = w_in.shape[2] * N_DEV, ln_v_g.shape[1], w_br_b.shape[1], w_xq.shape[2], w_up.shape[2] * N_DEV
    KVW = (IN - 2 * GW - AW - 2 * D) // 2
    d = Dims(T=T, D=D, IN=IN, GW=GW, G=GW // GROUP_DIM, AW=AW, KVW=KVW, HQ=AW // HEAD_DIM, HKV=KVW // HEAD_DIM, XW=XW,
             XH=XW // X_HEAD_DIM, MEM=mem.shape[1], DFF=DFF)
    place = jnp.stack([lax.axis_index("x"), lax.axis_index("y"), lax.axis_index("c")]).astype(jnp.int32)

    sched = _Sched()
    big_of = {g: (p, tr) for g, p, tr in BIG}

    def shard(l, g):
        p, tr = big_of[g]
        return (given[p][l].T if tr else given[p][l]).astype(BF16)

    Gs = [{}, {}]

    def plan_gather(l, names, hosts1, host2):
        shards = [shard(l, g) for g in names]
        if hosts1 is None:
            Gs[l].update(zip(names, _all_gather(shards, name=f"gather_{names[0]}_l{l}")))
            return

        spans = [(h, 1) if isinstance(h, str) else h for h in hosts1]
        total = sum(cnt for _, cnt in spans)

        def register(i, k0, into):
            host, cnt = spans[i]

            def done(outs):
                if i + 1 < len(spans):
                    register(i + 1, k0 + cnt, outs)
                else:
                    sched.at(host2, _gather_stage2(outs, lambda full: Gs[l].update(zip(names, full))))

            sched.at(host, _gather_stage1(shards, done, part=(k0, k0 + cnt, total), into=into))

        register(0, 0, None)

    for entry in GATHER_PLAN:
        plan_gather(*entry)
    Ss = []
    for l in range(DEPTH):
        S = {n: given[n][l].reshape(1, -1) for n in SMALL_NAMES if n not in ("w_s", "b_s", "sinks")}
        S["w_s"], S["b_sT"], S["sinks"] = w_s[l], b_s[l].T, sinks[l]
        Ss.append(S)

    owns, recvs = {}, {}

    smalls_seen = {}
    gathered_small = []

    def pack_small(src):
        parts = []
        for l in range(DEPTH):
            for n in SMALL_NAMES:
                a = src[l]["b_sT"].T if n == "b_s" else src[l][n]
                parts.append(a[0, : d.HQ] if n == "sinks" else a)
        return _pack(parts)

    def on_small(l, small):
        smalls_seen[l] = small
        if len(smalls_seen) == DEPTH:
            host1, host2 = SMALL_GRAD_HOSTS
            sched.at(host1, _gather_stage1([pack_small(smalls_seen)], lambda part: sched.at(
                host2, _gather_stage2(part, lambda full: gathered_small.append(full[0])))))

    def on_grad(l, g, grad):
        if g == "small":
            return on_small(l, grad)
        sib_host, chips_host = GRAD_HOSTS[l][g]

        def after_sibling(got):
            owns[(l, g)], send = _pair_sum(grad, got[0], place, name=f"grad_pair_sum_{g}_l{l}")
            hosts = chips_host if isinstance(chips_host, tuple) else (chips_host,)

            def register(k, into):
                def done(r):
                    if k + 1 < len(hosts):
                        register(k + 1, r)
                    else:
                        recvs[(l, g)] = r[0]

                task = _chips_stage([send], done, part=(k, len(hosts)), into=into)
                if hosts[k] is None:
                    _comm_only([task], name=f"grad_chips_{g}_l{l}")
                else:
                    sched.at(hosts[k], task)

            register(0, None)

        task = _sibling_stage([grad], after_sibling)
        if sib_host is None:
            _comm_only([task], name=f"grad_sibling_{g}_l{l}")
        else:
            sched.at(sib_host, task)

    loss, dX, bigs, smalls = _local_step(d, x[0], mem[0], loss_target[0], Gs, Ss, sched, on_grad)
    loss = lax.psum(loss[0, 0], ("x", "y", "c"))

    assert not sched.pending, sorted(sched.pending)

    def viewed(p, tr):
        return tr and given[p].shape[2] % LANES != 0

    res = {p: None for _, p, _ in BIG}
    for l in reversed(range(DEPTH)):
        for g, p, tr in BIG:
            view = viewed(p, tr)
            w3, m3, v3 = ((jnp.swapaxes(a, 1, 2) if view else a) for a in (given[p], given["m_" + p], given["v_" + p]))
            res[p] = _adam_big(w3, m3, v3, owns[(l, g)], recvs[(l, g)], l, tr and not view, res[p], name=f"adamw_{g}_l{l}")
    for g, p, tr in BIG:
        if viewed(p, tr):
            res[p] = [jnp.swapaxes(a, 1, 2) for a in res[p]]

    shapes = [given[n].shape[1:] for n in SMALL_NAMES]
    pw, pm, pv = (_pack([given[pre + n][l] for l in range(DEPTH) for n in SMALL_NAMES]) for pre in ("", "m_", "v_"))
    small_out = [_unpack(o, shapes * DEPTH) for o in _adam_small(pw, pm, pv, gathered_small[0], name="adamw_small")]

    names = ["w_in", "b_gate", "ln_v_g", "ln_v_b", "w_s", "b_s", "sinks", "w_br_a", "w_br_b", "w_o", "ln1_g", "ln1_b", "w_xq",
             "w_xkv", "w_xo", "ln2_g", "ln2_b", "w_up", "w_down", "ln3_g", "ln3_b"]
    out = [loss, dX[None]]
    for kind in range(4):
        for n in names:
            if n in SMALL_NAMES:
                i = SMALL_NAMES.index(n)
                out.append(jnp.stack([small_out[kind][l * len(SMALL_NAMES) + i] for l in range(DEPTH)]))
            else:
                out.append(res[n][kind])
    return tuple(out)
```

```python
import collections
import dataclasses
import functools

import jax
import jax.numpy as jnp
from jax import lax
from jax.experimental import pallas as pl
from jax.experimental.pallas import tpu as pltpu

F32 = jnp.float32
BF16 = jnp.bfloat16
MESH = pl.DeviceIdType.MESH

N_DEV = 8
DEPTH = 2
CHUNK = 128
HEAD_DIM = 64
ROPE_HALF = HEAD_DIM // 2
X_HEAD_DIM = 128
GROUP_DIM = 128
LANES = 128
ROPE_THETA = 10000.0
LN_EPS = 1e-5
ALPHA = (2 * DEPTH) ** 0.25
ADAM_LR = 0.001
ADAM_B1 = 0.9
ADAM_B2 = 0.999
ADAM_EPS = 1e-08
ADAM_WD = 0.01
ADAM_STEP = 10
VMEM_BYTES = 64 * 1024 * 1024
VMEM_TEMP_BYTES = 20 * 1024 * 1024

Dims = collections.namedtuple("Dims", "T D IN GW G AW KVW HQ HKV XW XH MEM DFF")


def _offsets(d):
    off_v = d.GW
    off_q = 2 * d.GW
    off_k = off_q + d.AW
    off_va = off_k + d.KVW
    off_ga = off_va + d.KVW
    off_gb = off_ga + d.D
    return off_v, off_q, off_k, off_va, off_ga, off_gb


def _tile(dim, pref, align=LANES):
    if dim <= pref:
        return dim
    t = pref - pref % align
    while t >= align:
        if dim % t == 0:
            return t
        t -= align
    return dim


def _nbytes(shape, dtype):
    n = 1
    for s in shape:
        n *= s
    return n * jnp.dtype(dtype).itemsize


def _params(sem, blocks, scratch=0):
    need = 2 * sum(_nbytes(s, t) for s, t in blocks) + scratch + VMEM_TEMP_BYTES
    limit = min(max(need, 32 * 1024 * 1024), VMEM_BYTES - 4 * 1024 * 1024)
    return pltpu.CompilerParams(dimension_semantics=sem, vmem_limit_bytes=limit)


def _dot(a, b, kind):
    dn = {"nn": (((1,), (0,)), ((), ())), "nt": (((1,), (1,)), ((), ())), "tn": (((0,), (0,)), ((), ()))}[kind]
    return lax.dot_general(a, b, dn, preferred_element_type=F32)


def _gelu(x):
    c = 0.7978845608028654
    t = jnp.tanh(c * (x + 0.044715 * (x * x * x)))
    return 0.5 * x * (1.0 + t)


def _gelu_grad(x):
    c = 0.7978845608028654
    x2 = x * x
    t = jnp.tanh(c * (x + 0.044715 * (x2 * x)))
    return 0.5 * (1.0 + t) + 0.5 * x * (1.0 - t * t) * (c * (1.0 + 3.0 * 0.044715 * x2))


def _ln_fwd(z, g, b):
    mu = jnp.mean(z, axis=-1, keepdims=True)
    zc = z - mu
    var = jnp.mean(zc * zc, axis=-1, keepdims=True)
    rstd = lax.rsqrt(var + LN_EPS)
    xhat = zc * rstd
    return xhat * g + b, xhat, rstd


def _ln_bwd_vals(dy, xhat, rstd, g):
    gd = dy * g
    m1 = jnp.mean(gd, axis=-1, keepdims=True)
    m2 = jnp.mean(gd * xhat, axis=-1, keepdims=True)
    return rstd * (gd - m1 - xhat * m2)


def _acc_store(ref, val, first):
    @pl.when(first)
    def _():
        ref[...] = val

    @pl.when(jnp.logical_not(first))
    def _():
        ref[...] += val


def _matmul(a, b, kind, *, name, tm=512, tn=1024, tk=2048, out_defs=(("mn", F32),), epilogue=None, extras=(), sched=None):
    if kind == "nn":
        (M, K), (K2, N) = a.shape, b.shape
    elif kind == "nt":
        (M, K), (N, K2) = a.shape, b.shape
    else:
        (K, M), (K2, N) = a.shape, b.shape
    assert K == K2, (a.shape, b.shape, kind)
    tm, tn, tk = _tile(M, tm), _tile(N, tn), _tile(K, tk)
    nk = K // tk
    blocks = []

    def spec(shape, imap, dtype):
        blocks.append((shape, dtype))
        return pl.BlockSpec(shape, imap)

    if kind == "tn":
        a_spec = spec((tk, tm), lambda j, i, k: (k, i), a.dtype)
    else:
        a_spec = spec((tm, tk), lambda j, i, k: (i, k), a.dtype)
    if kind == "nt":
        b_spec = spec((tn, tk), lambda j, i, k: (j, k), b.dtype)
    else:
        b_spec = spec((tk, tn), lambda j, i, k: (k, j), b.dtype)
    in_specs = [a_spec, b_spec]
    operands = [a, b]
    for arr, ekind, off in extras:
        if ekind == "mn":
            assert off % tn == 0
            in_specs.append(spec((tm, tn), lambda j, i, k, o=off // tn: (i, j + o), arr.dtype))
        elif ekind == "row":
            assert off % tn == 0
            in_specs.append(spec((1, tn), lambda j, i, k, o=off // tn: (0, j + o), arr.dtype))
        else:
            in_specs.append(spec((tm, 1), lambda j, i, k: (i, 0), arr.dtype))
        operands.append(arr)
    out_shape, out_specs = [], []
    for od in out_defs:
        okind, dt = od[0], od[1]
        if okind == "mn":
            out_shape.append(jax.ShapeDtypeStruct((M, N), dt))
            out_specs.append(spec((tm, tn), lambda j, i, k: (i, j), dt))
        elif okind == "cols":
            assert od[3] % tn == 0
            out_shape.append(jax.ShapeDtypeStruct((M, od[2]), dt))
            out_specs.append(spec((tm, tn), lambda j, i, k, o=od[3] // tn: (i, j + o), dt))
        elif okind == "m1":
            assert N == tn
            out_shape.append(jax.ShapeDtypeStruct((M, 1), dt))
            out_specs.append(spec((tm, 1), lambda j, i, k: (i, 0), dt))
        else:
            out_shape.append(jax.ShapeDtypeStruct((1, N), F32))
            out_specs.append(spec((1, tn), lambda j, i, k: (0, j), F32))
    n_ex, n_out = len(extras), len(out_defs)
    ep = epilogue if epilogue is not None else (lambda acc: (acc,))
    in_place = nk > 1 and epilogue is None and tuple(out_defs) == (("mn", F32),)

    def body(*refs):
        a_ref, b_ref = refs[0], refs[1]
        ex_refs = refs[2 : 2 + n_ex]
        out_refs = refs[2 + n_ex : 2 + n_ex + n_out]
        i = pl.program_id(1)
        k = pl.program_id(2)

        def product():
            return _dot(a_ref[...], b_ref[...], kind)

        def finish(acc):
            vals = ep(acc, *[r[...] for r in ex_refs])
            for od, r, v in zip(out_defs, out_refs, vals):
                if od[0] == "acc":
                    _acc_store(r, v, i == 0)
                else:
                    r[...] = v.astype(od[1])

        if nk == 1:
            finish(product())
            return
        acc_ref = out_refs[0] if in_place else refs[-1]

        @pl.when(k == 0)
        def _():
            acc_ref[...] = product()

        if in_place:
            @pl.when(k > 0)
            def _():
                acc_ref[...] += product()
        else:
            if nk > 2:
                @pl.when(jnp.logical_and(k > 0, k < nk - 1))
                def _():
                    acc_ref[...] += product()

            @pl.when(k == nk - 1)
            def _():
                finish(acc_ref[...] + product())

    scratch = [pltpu.VMEM((tm, tn), F32)] if nk > 1 and not in_place else []
    outs = _hosted(
        body,
        name=name,
        sched=sched,
        grid=(N // tn, M // tm, nk),
        in_specs=in_specs,
        out_specs=out_specs,
        out_shape=out_shape,
        scratch_shapes=scratch,
        compiler_params=_params(("arbitrary", "arbitrary", "arbitrary"), blocks, _nbytes((tm, tn), F32) if scratch else 0),
    )(*operands)
    return outs if len(outs) > 1 else outs[0]


def _ep_resid_ln(acc, resid, g, b):
    y, xhat, rstd = _ln_fwd(ALPHA * resid + acc, g, b)
    return y, y, xhat, rstd


def _ep_resid(acc, resid):
    return (ALPHA * resid + acc,)


def _ep_resid_ln_bwd(acc, resid, xhat, rstd, g):
    dy = ALPHA * resid + acc
    dz = _ln_bwd_vals(dy, xhat, rstd, g)
    return dz, dz, jnp.sum(dy * xhat, axis=0, keepdims=True), jnp.sum(dy, axis=0, keepdims=True)


def _ep_relu2(acc):
    r = jnp.maximum(acc, 0.0)
    return acc, r * r


def _ep_relu2_bwd(acc, h):
    return (acc * (2.0 * jnp.maximum(h, 0.0)),)


def _ep_gate_bwd(acc, ga, gb, ya, yb, bga, bgb):
    sa = jax.nn.sigmoid(ga + bga)
    sb = jax.nn.sigmoid(gb + bgb)
    dga = acc * ya * (sa * (1.0 - sa))
    dgb = acc * yb * (sb * (1.0 - sb))
    return (acc * sa, acc * sb, dga, dgb, jnp.sum(dga, axis=0, keepdims=True), jnp.sum(dgb, axis=0, keepdims=True))


def _ln_bwd(dy, xhat, rstd, g, *, name, sched=None, target=None):
    T, D = dy.shape
    tm = _tile(T, 256)
    head = target is not None

    def body(*refs):
        dy_ref, xh_ref, rs_ref, g_ref = refs[:4]
        dz_ref, dzb_ref, dg_ref, db_ref = refs[4 + head : 8 + head]
        first = pl.program_id(0) == 0
        dyv, xh = dy_ref[...], xh_ref[...]
        if head:
            err = dyv - refs[4][...]
            dyv = err * (1.0 / D)
            part = 0.5 * jnp.sum(jnp.sum(err * err, axis=1, keepdims=True) * (1.0 / D), axis=0, keepdims=True)
            _acc_store(refs[-1], part, first)
        dz = _ln_bwd_vals(dyv, xh, rs_ref[...], g_ref[...])
        dz_ref[...] = dz
        dzb_ref[...] = dz.astype(BF16)
        _acc_store(dg_ref, jnp.sum(dyv * xh, axis=0, keepdims=True), first)
        _acc_store(db_ref, jnp.sum(dyv, axis=0, keepdims=True), first)

    row = pl.BlockSpec((tm, D), lambda i: (i, 0))
    vec = pl.BlockSpec((1, D), lambda i: (0, 0))
    one = pl.BlockSpec((1, 1), lambda i: (0, 0))
    return _hosted(
        body,
        name=name,
        sched=sched,
        grid=(T // tm,),
        in_specs=[row, row, pl.BlockSpec((tm, 1), lambda i: (i, 0)), vec] + [row] * head,
        out_specs=[row, row, vec, vec] + [one] * head,
        out_shape=[
            jax.ShapeDtypeStruct((T, D), F32),
            jax.ShapeDtypeStruct((T, D), BF16),
            jax.ShapeDtypeStruct((1, D), F32),
            jax.ShapeDtypeStruct((1, D), F32),
        ] + [jax.ShapeDtypeStruct((1, 1), F32)] * head,
        compiler_params=_params(("arbitrary",), [((tm, D), F32)] * (4 + head)),
    )(*([dy, xhat, rstd, g] + [target] * head))


def _masked_mix_weights(wm_ref, G):
    r = lax.broadcasted_iota(jnp.int32, (CHUNK, CHUNK), 0)
    c = lax.broadcasted_iota(jnp.int32, (CHUNK, CHUNK), 1)
    return [jnp.where(c <= r, wm_ref[g], 0.0).astype(BF16) for g in range(G)]


def _gmlp_fwd(d, P, lnv_g, lnv_b, wm, bsT, *, name, sched=None):
    T, GW, G = d.T, d.GW, d.G
    tm = _tile(T, 256)
    nc = tm // CHUNK

    def body(u_ref, v_ref, g_ref, b_ref, wm_ref, bs_ref, o_ref):
        gu = _gelu(u_ref[...])
        vn, _, _ = _ln_fwd(_gelu(v_ref[...]), g_ref[...], b_ref[...])
        vn = vn.astype(BF16)
        wl = _masked_mix_weights(wm_ref, G)
        for c in range(nc):
            rows = slice(c * CHUNK, (c + 1) * CHUNK)
            for g in range(G):
                cols = slice(g * GROUP_DIM, (g + 1) * GROUP_DIM)
                mixed = _dot(wl[g], vn[rows, cols], "nn") + bs_ref[:, g : g + 1]
                o_ref[rows, cols] = (gu[rows, cols] * mixed).astype(BF16)

    return _hosted(
        body,
        name=name,
        sched=sched,
        grid=(T // tm,),
        in_specs=[
            pl.BlockSpec((tm, GW), lambda i: (i, 0)),
            pl.BlockSpec((tm, GW), lambda i: (i, 1)),
            pl.BlockSpec((1, GW), lambda i: (0, 0)),
            pl.BlockSpec((1, GW), lambda i: (0, 0)),
            pl.BlockSpec((G, CHUNK, CHUNK), lambda i: (0, 0, 0)),
            pl.BlockSpec((CHUNK, G), lambda i: (0, 0)),
        ],
        out_specs=pl.BlockSpec((tm, GW), lambda i: (i, 0)),
        out_shape=jax.ShapeDtypeStruct((T, GW), BF16),
        compiler_params=_params(("arbitrary",), [((tm, GW), F32)] * 3),
    )(P, P, lnv_g, lnv_b, wm, bsT)


def _gmlp_bwd(d, P, dya_b, Ga, lnv_g, lnv_b, wm, bsT, dP, *, name, sched=None):
    T, D, GW, G = d.T, d.D, d.GW, d.G
    tm = _tile(T, 256)
    nc = tm // CHUNK

    def body(u_ref, v_ref, dya_ref, ga_ref, g_ref, b_ref, wm_ref, bs_ref, dp_in, duv_ref, dwm_ref, dbs_ref, dlg_ref, dlb_ref, dgu_s, dvn_s):
        first = pl.program_id(0) == 0
        dsgu = _dot(dya_ref[...], ga_ref[...], "nn")
        u, v = u_ref[...], v_ref[...]
        gu = _gelu(u)
        lng = g_ref[...]
        vn, xh, rstd = _ln_fwd(_gelu(v), lng, b_ref[...])
        vn = vn.astype(BF16)
        wl = _masked_mix_weights(wm_ref, G)
        r = lax.broadcasted_iota(jnp.int32, (CHUNK, CHUNK), 0)
        cc = lax.broadcasted_iota(jnp.int32, (CHUNK, CHUNK), 1)
        lane_g = lax.broadcasted_iota(jnp.int32, (CHUNK, G), 1)
        dbs = jnp.zeros((CHUNK, G), F32)
        for g in range(G):
            cols = slice(g * GROUP_DIM, (g + 1) * GROUP_DIM)
            dw = jnp.zeros((CHUNK, CHUNK), F32)
            for c in range(nc):
                rows = slice(c * CHUNK, (c + 1) * CHUNK)
                mixed = _dot(wl[g], vn[rows, cols], "nn") + bs_ref[:, g : g + 1]
                ds = dsgu[rows, cols]
                dgu_s[rows, cols] = ds * mixed
                dmx = ds * gu[rows, cols]
                dbs = dbs + jnp.where(lane_g == g, jnp.sum(dmx, axis=1, keepdims=True), 0.0)
                dmx = dmx.astype(BF16)
                dw = dw + _dot(dmx, vn[rows, cols], "nt")
                dvn_s[rows, cols] = _dot(wl[g], dmx, "tn")
            _acc_store(dwm_ref.at[g], jnp.where(cc <= r, dw, 0.0), first)
        _acc_store(dbs_ref, dbs, first)
        dvn = dvn_s[...]
        _acc_store(dlg_ref, jnp.sum(dvn * xh, axis=0, keepdims=True), first)
        _acc_store(dlb_ref, jnp.sum(dvn, axis=0, keepdims=True), first)
        dgv = _ln_bwd_vals(dvn, xh, rstd, lng)
        duv_ref[:, :GW] = (dgu_s[...] * _gelu_grad(u)).astype(BF16)
        duv_ref[:, GW:] = (dgv * _gelu_grad(v)).astype(BF16)

    return _hosted(
        body,
        name=name,
        sched=sched,
        grid=(T // tm,),
        in_specs=[
            pl.BlockSpec((tm, GW), lambda i: (i, 0)),
            pl.BlockSpec((tm, GW), lambda i: (i, 1)),
            pl.BlockSpec((tm, D), lambda i: (i, 0)),
            pl.BlockSpec((D, GW), lambda i: (0, 0)),
            pl.BlockSpec((1, GW), lambda i: (0, 0)),
            pl.BlockSpec((1, GW), lambda i: (0, 0)),
            pl.BlockSpec((G, CHUNK, CHUNK), lambda i: (0, 0, 0)),
            pl.BlockSpec((CHUNK, G), lambda i: (0, 0)),
            pl.BlockSpec(memory_space=pl.ANY),
        ],
        out_specs=[
            pl.BlockSpec((tm, 2 * GW), lambda i: (i, 0)),
            pl.BlockSpec((G, CHUNK, CHUNK), lambda i: (0, 0, 0)),
            pl.BlockSpec((CHUNK, G), lambda i: (0, 0)),
            pl.BlockSpec((1, GW), lambda i: (0, 0)),
            pl.BlockSpec((1, GW), lambda i: (0, 0)),
        ],
        out_shape=[
            jax.ShapeDtypeStruct(dP.shape, BF16),
            jax.ShapeDtypeStruct((G, CHUNK, CHUNK), F32),
            jax.ShapeDtypeStruct((CHUNK, G), F32),
            jax.ShapeDtypeStruct((1, GW), F32),
            jax.ShapeDtypeStruct((1, GW), F32),
        ],
        scratch_shapes=[pltpu.VMEM((tm, GW), F32), pltpu.VMEM((tm, GW), F32)],
        input_output_aliases={8: 0},
        compiler_params=_params(
            ("arbitrary",), [((tm, GW), F32)] * 3 + [((tm, D), BF16), ((D, GW), BF16)], 2 * _nbytes((tm, GW), F32)
        ),
    )(P, P, dya_b, Ga, lnv_g, lnv_b, wm, bsT, dP)


def _swap_halves(x):
    w = x.shape[1]
    lane = lax.broadcasted_iota(jnp.int32, x.shape, 1)
    return jnp.where((lane % HEAD_DIM) < ROPE_HALF, pltpu.roll(x, w - ROPE_HALF, 1), pltpu.roll(x, ROPE_HALF, 1))


def _lane_tile(t, width):
    reps = width // LANES
    return t if reps == 1 else jnp.tile(t, (1, reps))


def _rope(x, cos2, sin2):
    w = x.shape[1]
    return x * _lane_tile(cos2, w) + _swap_halves(x) * _lane_tile(sin2, w)


def _rope_bwd(g, cos2, sin2):
    w = g.shape[1]
    return g * _lane_tile(cos2, w) - _swap_halves(g) * _lane_tile(sin2, w)


PAIR = LANES // HEAD_DIM


def _swa_valid(i):
    s = lax.broadcasted_iota(jnp.int32, (2 * CHUNK, CHUNK), 0)
    t = lax.broadcasted_iota(jnp.int32, (2 * CHUNK, CHUNK), 1)
    cur = jnp.logical_and(s >= CHUNK, s - CHUNK <= t)
    prev = jnp.logical_and(jnp.logical_and(s < CHUNK, s > t), i > 0)
    return jnp.logical_or(cur, prev)


def _half_variants(x, odd):
    lane = lax.broadcasted_iota(jnp.int32, x.shape, 1)
    if odd:
        hi = jnp.where(lane >= HEAD_DIM, x, jnp.zeros_like(x))
        return pltpu.roll(hi, HEAD_DIM, 1), hi
    lo = jnp.where(lane < HEAD_DIM, x, jnp.zeros_like(x))
    return lo, pltpu.roll(lo, HEAD_DIM, 1)


def _swa_probs_t(kx, qp, sink, valid):
    s = jnp.where(valid, _dot(kx, qp, "nt") * (HEAD_DIM**-0.5), -jnp.inf)
    m = jnp.maximum(jnp.max(s, axis=0, keepdims=True), sink)
    e = jnp.exp(s - m)
    e_s = jnp.exp(sink - m)
    den = jnp.sum(e, axis=0, keepdims=True) + e_s
    return e / den, e_s / den


def _swa_load(q_ref, kc_ref, kp_ref, vc_ref, vp_ref, cc, sc, cp, sp):
    qr = _rope(q_ref[...], cc, sc).astype(BF16)
    kcat = jnp.concatenate([_rope(kp_ref[...], cp, sp), _rope(kc_ref[...], cc, sc)], axis=0)
    vcat = jnp.concatenate([vp_ref[...], vc_ref[...]], axis=0)
    return qr, kcat, vcat


def _swa_specs(d):
    _, off_q, off_k, off_va, _, _ = _offsets(d)
    assert off_q % d.AW == 0 and off_k % d.KVW == 0 and off_va % d.KVW == 0
    prev = lambda i: jnp.maximum(i - 1, 0)
    return [
        pl.BlockSpec(memory_space=pltpu.SMEM),
        pl.BlockSpec((CHUNK, d.AW), lambda i, o=off_q // d.AW: (i, o)),
        pl.BlockSpec((CHUNK, d.KVW), lambda i, o=off_k // d.KVW: (i, o)),
        pl.BlockSpec((CHUNK, d.KVW), lambda i, o=off_k // d.KVW: (prev(i), o)),
        pl.BlockSpec((CHUNK, d.KVW), lambda i, o=off_va // d.KVW: (i, o)),
        pl.BlockSpec((CHUNK, d.KVW), lambda i, o=off_va // d.KVW: (prev(i), o)),
        pl.BlockSpec((CHUNK, LANES), lambda i: (i, 0)),
        pl.BlockSpec((CHUNK, LANES), lambda i: (i, 0)),
        pl.BlockSpec((CHUNK, LANES), lambda i: (prev(i), 0)),
        pl.BlockSpec((CHUNK, LANES), lambda i: (prev(i), 0)),
    ]


def _swa_fwd(d, P, cos2, sin2, sinks, *, name, sched=None):
    T, AW, HQ, HKV = d.T, d.AW, d.HQ, d.HKV
    grp = HQ // HKV
    assert grp % PAIR == 0 and HKV % PAIR == 0

    def body(sink_ref, q_ref, kc_ref, kp_ref, vc_ref, vp_ref, cc_ref, sc_ref, cp_ref, sp_ref, o_ref):
        i = pl.program_id(0)
        qr, kcat, vcat = _swa_load(q_ref, kc_ref, kp_ref, vc_ref, vp_ref, cc_ref[...], sc_ref[...], cp_ref[...], sp_ref[...])
        valid = _swa_valid(i)
        for kv in range(HKV):
            col = slice((kv // PAIR) * LANES, (kv // PAIR + 1) * LANES)
            kx = [t.astype(BF16) for t in _half_variants(kcat[:, col], kv % PAIR)]
            vx = [t.astype(BF16) for t in _half_variants(vcat[:, col], kv % PAIR)]
            for pp in range(grp // PAIR):
                p = kv * (grp // PAIR) + pp
                qs = slice(p * LANES, (p + 1) * LANES)
                out = jnp.zeros((CHUNK, LANES), F32)
                for half in range(PAIR):
                    pt, _ = _swa_probs_t(kx[half], qr[:, qs], sink_ref[PAIR * p + half], valid)
                    out = out + _dot(pt.astype(BF16), vx[half], "tn")
                o_ref[:, qs] = out.astype(BF16)

    return _hosted(
        body,
        name=name,
        sched=sched,
        grid=(T // CHUNK,),
        in_specs=_swa_specs(d),
        out_specs=pl.BlockSpec((CHUNK, AW), lambda i: (i, 0)),
        out_shape=jax.ShapeDtypeStruct((T, AW), BF16),
        compiler_params=_params(("arbitrary",), [((CHUNK, AW), F32)] * 3),
    )(sinks, P, P, P, P, P, cos2, sin2, cos2, sin2)


def _swa_bwd(d, P, dyb_b, Gb, cos2, sin2, sinks, dP, *, name, sched=None):
    T, D, AW, KVW, HQ, HKV = d.T, d.D, d.AW, d.KVW, d.HQ, d.HKV
    grp = HQ // HKV
    nb = T // CHUNK
    scale = HEAD_DIM**-0.5
    off_q = _offsets(d)[1]
    assert grp % PAIR == 0 and HKV % PAIR == 0 and off_q % AW == 0

    def body(sink_ref, q_ref, kc_ref, kp_ref, vc_ref, vp_ref, cc_ref, sc_ref, cp_ref, sp_ref, dyb_ref, gb_ref, dp_in,
             dq_ref, dkv_ref, dsk_ref, acc_s, dq_s, dk_s, dv_s):
        i = pl.program_id(0)
        cc, sc, cp, sp = cc_ref[...], sc_ref[...], cp_ref[...], sp_ref[...]
        datt = _dot(dyb_ref[...], gb_ref[...], "nn").astype(BF16)
        qr, kcat, vcat = _swa_load(q_ref, kc_ref, kp_ref, vc_ref, vp_ref, cc, sc, cp, sp)
        valid = _swa_valid(i)
        lane1 = lax.broadcasted_iota(jnp.int32, (1, LANES), 1)
        lane2 = lax.broadcasted_iota(jnp.int32, (2 * CHUNK, LANES), 1)
        dsk = jnp.zeros((1, LANES), F32)
        for kvp in range(HKV // PAIR):
            col = slice(kvp * LANES, (kvp + 1) * LANES)
            dk_col = jnp.zeros((2 * CHUNK, LANES), F32)
            dv_col = jnp.zeros((2 * CHUNK, LANES), F32)
            for odd in range(PAIR):
                kv = kvp * PAIR + odd
                kx = [t.astype(BF16) for t in _half_variants(kcat[:, col], odd)]
                vx = [t.astype(BF16) for t in _half_variants(vcat[:, col], odd)]
                dk_half = [jnp.zeros((2 * CHUNK, LANES), F32) for _ in range(PAIR)]
                dv_half = [jnp.zeros((2 * CHUNK, LANES), F32) for _ in range(PAIR)]
                for pp in range(grp // PAIR):
                    p = kv * (grp // PAIR) + pp
                    qs = slice(p * LANES, (p + 1) * LANES)
                    qp, dop = qr[:, qs], datt[:, qs]
                    dq = jnp.zeros((CHUNK, LANES), F32)
                    for half in range(PAIR):
                        h = PAIR * p + half
                        pt, p_s = _swa_probs_t(kx[half], qp, sink_ref[h], valid)
                        dpt = _dot(vx[half], dop, "nt")
                        rs = jnp.sum(pt * dpt, axis=0, keepdims=True)
                        dst = (pt * (dpt - rs) * scale).astype(BF16)
                        dsk = dsk + jnp.where(lane1 == h, -jnp.sum(p_s * rs, axis=1, keepdims=True), 0.0)
                        dq = dq + _dot(dst, kx[half], "tn")
                        dk_half[half] = dk_half[half] + _dot(dst, qp, "nn")
                        dv_half[half] = dv_half[half] + _dot(pt.astype(BF16), dop, "nn")
                    dq_s[:, qs] = dq
                for acc_half, is_k in ((dk_half, True), (dv_half, False)):
                    lo = jnp.where(lane2 < HEAD_DIM, acc_half[0], 0.0)
                    hi = jnp.where(lane2 >= HEAD_DIM, acc_half[1], 0.0)
                    both = (pltpu.roll(lo, HEAD_DIM, 1) + hi) if odd else (lo + pltpu.roll(hi, HEAD_DIM, 1))
                    if is_k:
                        dk_col = dk_col + both
                    else:
                        dv_col = dv_col + both
            dk_s[:, col] = dk_col
            dv_s[:, col] = dv_col
        dq_ref[...] = _rope_bwd(dq_s[...], cc, sc).astype(BF16)
        cur = pl.ds(pl.multiple_of(i * CHUNK, CHUNK), CHUNK)
        acc_s[cur, :KVW] = _rope_bwd(dk_s[CHUNK:, :], cc, sc)
        acc_s[cur, KVW:] = dv_s[CHUNK:, :]

        @pl.when(i > 0)
        def _():
            prv = pl.ds(pl.multiple_of((i - 1) * CHUNK, CHUNK), CHUNK)
            acc_s[prv, :KVW] += _rope_bwd(dk_s[:CHUNK, :], cp, sp)
            acc_s[prv, KVW:] += dv_s[:CHUNK, :]

        _acc_store(dsk_ref, dsk, i == 0)

        @pl.when(i == nb - 1)
        def _():
            dkv_ref[...] = acc_s[...].astype(BF16)

    return _hosted(
        body,
        name=name,
        sched=sched,
        grid=(nb,),
        in_specs=_swa_specs(d) + [pl.BlockSpec((CHUNK, D), lambda i: (i, 0)), pl.BlockSpec((D, AW), lambda i: (0, 0)),
                                   pl.BlockSpec(memory_space=pl.ANY)],
        out_specs=[
            pl.BlockSpec((CHUNK, AW), lambda i, o=off_q // AW: (i, o)),
            pl.BlockSpec((T, 2 * KVW), lambda i: (0, 0)),
            pl.BlockSpec((1, LANES), lambda i: (0, 0)),
        ],
        out_shape=[
            jax.ShapeDtypeStruct(dP.shape, BF16),
            jax.ShapeDtypeStruct((T, 2 * KVW), BF16),
            jax.ShapeDtypeStruct((1, LANES), F32),
        ],
        input_output_aliases={12: 0},
        scratch_shapes=[
            pltpu.VMEM((T, 2 * KVW), F32),
            pltpu.VMEM((CHUNK, AW), F32),
            pltpu.VMEM((2 * CHUNK, KVW), F32),
            pltpu.VMEM((2 * CHUNK, KVW), F32),
        ],
        compiler_params=_params(
            ("arbitrary",), [((CHUNK, AW), F32)] * 3 + [((D, AW), BF16), ((T, 2 * KVW), BF16)], _nbytes((T, 2 * KVW), F32)
        ),
    )(sinks, P, P, P, P, P, cos2, sin2, cos2, sin2, dyb_b, Gb, dP)


def _place_cols(dst, src, off, *, name, sched=None):
    T, w = src.shape
    tm, tw = _tile(T, 512), _tile(w, 512)
    assert off % tw == 0

    def body(src_ref, dst_in, out_ref):
        out_ref[...] = src_ref[...]

    return _hosted(
        body,
        name=name,
        sched=sched,
        grid=(T // tm, w // tw),
        in_specs=[pl.BlockSpec((tm, tw), lambda i, j: (i, j)), pl.BlockSpec(memory_space=pl.ANY)],
        out_specs=pl.BlockSpec((tm, tw), lambda i, j, o=off // tw: (i, j + o)),
        out_shape=jax.ShapeDtypeStruct(dst.shape, dst.dtype),
        input_output_aliases={1: 0},
        compiler_params=_params(("arbitrary", "arbitrary"), [((tm, tw), dst.dtype)] * 2),
    )(src, dst)


def _branch_merge(d, sgu, att, Ga, Gb, P, b_gate, *, name, sched=None):
    T, D, GW, AW = d.T, d.D, d.GW, d.AW
    _, _, _, _, off_ga, off_gb = _offsets(d)
    tm, tn = _tile(T, 1024), _tile(D, 512)
    assert off_ga % tn == 0 and off_gb % tn == 0

    def body(s_ref, a_ref, ga_w, gb_w, ga_ref, gb_ref, bga_ref, bgb_ref, ya_ref, yb_ref, mg_ref):
        ya = _dot(s_ref[...], ga_w[...], "nt")
        yb = _dot(a_ref[...], gb_w[...], "nt")
        ya_ref[...] = ya
        yb_ref[...] = yb
        sa = jax.nn.sigmoid(ga_ref[...] + bga_ref[...])
        sb = jax.nn.sigmoid(gb_ref[...] + bgb_ref[...])
        mg_ref[...] = (sa * ya + sb * yb).astype(BF16)

    tile = pl.BlockSpec((tm, tn), lambda j, i: (i, j))
    return _hosted(
        body,
        name=name,
        sched=sched,
        grid=(D // tn, T // tm),
        in_specs=[
            pl.BlockSpec((tm, GW), lambda j, i: (i, 0)),
            pl.BlockSpec((tm, AW), lambda j, i: (i, 0)),
            pl.BlockSpec((tn, GW), lambda j, i: (j, 0)),
            pl.BlockSpec((tn, AW), lambda j, i: (j, 0)),
            pl.BlockSpec((tm, tn), lambda j, i, o=off_ga // tn: (i, j + o)),
            pl.BlockSpec((tm, tn), lambda j, i, o=off_gb // tn: (i, j + o)),
            pl.BlockSpec((1, tn), lambda j, i: (0, j)),
            pl.BlockSpec((1, tn), lambda j, i, o=D // tn: (0, j + o)),
        ],
        out_specs=[tile, tile, tile],
        out_shape=[jax.ShapeDtypeStruct((T, D), F32), jax.ShapeDtypeStruct((T, D), F32), jax.ShapeDtypeStruct((T, D), BF16)],
        compiler_params=_params(
            ("arbitrary", "arbitrary"),
            [((tm, GW), BF16), ((tm, AW), BF16), ((tn, GW), BF16), ((tn, AW), BF16)] + [((tm, tn), F32)] * 5,
        ),
    )(sgu, att, Ga, Gb, P, P, b_gate, b_gate)


def _xattn_probs(qh, kh):
    s = _dot(qh, kh, "nt") * (X_HEAD_DIM**-0.5)
    e = jnp.exp(s - jnp.max(s, axis=1, keepdims=True))
    return e / jnp.sum(e, axis=1, keepdims=True)


def _xattn_fwd(d, X, Xb, kv, Gxq, Gxo, ln_g, ln_b, *, name, sched=None):
    T, D, XW, XH, MEM = d.T, d.D, d.XW, d.XH, d.MEM
    tm = _tile(T, 256)

    def body(x_ref, xb_ref, kv_ref, wq_ref, wo_ref, g_ref, b_ref, y_ref, yb_ref, xh_ref, rs_ref, q_ref, o_ref, o_s):
        q = _dot(xb_ref[...], wq_ref[...], "nn").astype(BF16)
        q_ref[...] = q
        for h in range(XH):
            hs = slice(h * X_HEAD_DIM, (h + 1) * X_HEAD_DIM)
            p = _xattn_probs(q[:, hs], kv_ref[:, hs]).astype(BF16)
            o_s[:, hs] = _dot(p, kv_ref[:, XW + h * X_HEAD_DIM : XW + (h + 1) * X_HEAD_DIM], "nn").astype(BF16)
        o = o_s[...]
        o_ref[...] = o
        y, xhat, rstd = _ln_fwd(ALPHA * x_ref[...] + _dot(o, wo_ref[...], "nt"), g_ref[...], b_ref[...])
        y_ref[...] = y
        yb_ref[...] = y.astype(BF16)
        xh_ref[...] = xhat
        rs_ref[...] = rstd

    row = pl.BlockSpec((tm, D), lambda i: (i, 0))
    nar = pl.BlockSpec((tm, XW), lambda i: (i, 0))
    vec = pl.BlockSpec((1, D), lambda i: (0, 0))
    return _hosted(
        body,
        name=name,
        sched=sched,
        grid=(T // tm,),
        in_specs=[
            row,
            row,
            pl.BlockSpec((MEM, 2 * XW), lambda i: (0, 0)),
            pl.BlockSpec((D, XW), lambda i: (0, 0)),
            pl.BlockSpec((D, XW), lambda i: (0, 0)),
            vec,
            vec,
        ],
        out_specs=[row, row, row, pl.BlockSpec((tm, 1), lambda i: (i, 0)), nar, nar],
        out_shape=[
            jax.ShapeDtypeStruct((T, D), F32),
            jax.ShapeDtypeStruct((T, D), BF16),
            jax.ShapeDtypeStruct((T, D), F32),
            jax.ShapeDtypeStruct((T, 1), F32),
            jax.ShapeDtypeStruct((T, XW), BF16),
            jax.ShapeDtypeStruct((T, XW), BF16),
        ],
        scratch_shapes=[pltpu.VMEM((tm, XW), BF16)],
        compiler_params=_params(("arbitrary",), [((tm, D), F32)] * 4 + [((D, XW), BF16)] * 2),
    )(X, Xb, kv, Gxq, Gxo, ln_g, ln_b)


def _xattn_bwd(d, dz, dzb, q_b, kv, Gxq, Gxo, xhat, rstd, ln_g, *, name, sched=None):
    T, D, XW, XH, MEM = d.T, d.D, d.XW, d.XH, d.MEM
    tm = _tile(T, 256)
    scale = X_HEAD_DIM**-0.5

    def body(dz_ref, dzb_ref, q_ref, kv_ref, wq_ref, wo_ref, xh_ref, rs_ref, g_ref, dx_ref, dxb_ref, dg_ref, db_ref, dq_ref, dkv_ref, dq_s):
        first = pl.program_id(0) == 0
        do = _dot(dzb_ref[...], wo_ref[...], "nn").astype(BF16)
        for h in range(XH):
            hs = slice(h * X_HEAD_DIM, (h + 1) * X_HEAD_DIM)
            vs = slice(XW + h * X_HEAD_DIM, XW + (h + 1) * X_HEAD_DIM)
            kh, vh, qh, doh = kv_ref[:, hs], kv_ref[:, vs], q_ref[:, hs], do[:, hs]
            p = _xattn_probs(qh, kh)
            dp = _dot(doh, vh, "nt")
            ds = (p * (dp - jnp.sum(p * dp, axis=1, keepdims=True)) * scale).astype(BF16)
            dq_s[:, hs] = _dot(ds, kh, "nn").astype(BF16)
            _acc_store(dkv_ref.at[h], _dot(ds, qh, "tn"), first)
            _acc_store(dkv_ref.at[XH + h], _dot(p.astype(BF16), doh, "tn"), first)
        dq = dq_s[...]
        dq_ref[...] = dq
        dy = ALPHA * dz_ref[...] + _dot(dq, wq_ref[...], "nt")
        xh = xh_ref[...]
        dz_in = _ln_bwd_vals(dy, xh, rs_ref[...], g_ref[...])
        dx_ref[...] = dz_in
        dxb_ref[...] = dz_in.astype(BF16)
        _acc_store(dg_ref, jnp.sum(dy * xh, axis=0, keepdims=True), first)
        _acc_store(db_ref, jnp.sum(dy, axis=0, keepdims=True), first)

    row = pl.BlockSpec((tm, D), lambda i: (i, 0))
    nar = pl.BlockSpec((tm, XW), lambda i: (i, 0))
    vec = pl.BlockSpec((1, D), lambda i: (0, 0))
    return _hosted(
        body,
        name=name,
        sched=sched,
        grid=(T // tm,),
        in_specs=[
            row,
            row,
            nar,
            pl.BlockSpec((MEM, 2 * XW), lambda i: (0, 0)),
            pl.BlockSpec((D, XW), lambda i: (0, 0)),
            pl.BlockSpec((D, XW), lambda i: (0, 0)),
            row,
            pl.BlockSpec((tm, 1), lambda i: (i, 0)),
            vec,
        ],
        out_specs=[row, row, vec, vec, nar, pl.BlockSpec((2 * XH, MEM, X_HEAD_DIM), lambda i: (0, 0, 0))],
        out_shape=[
            jax.ShapeDtypeStruct((T, D), F32),
            jax.ShapeDtypeStruct((T, D), BF16),
            jax.ShapeDtypeStruct((1, D), F32),
            jax.ShapeDtypeStruct((1, D), F32),
            jax.ShapeDtypeStruct((T, XW), BF16),
            jax.ShapeDtypeStruct((2 * XH, MEM, X_HEAD_DIM), F32),
        ],
        scratch_shapes=[pltpu.VMEM((tm, XW), BF16)],
        compiler_params=_params(("arbitrary",), [((tm, D), F32)] * 5 + [((D, XW), BF16)] * 2),
    )(dz, dzb, q_b, kv, Gxq, Gxo, xhat, rstd, ln_g)


def _resid_ln_outs():
    return (("mn", F32), ("mn", BF16), ("mn", F32), ("m1", F32))


def _layer_fwd(d, X, Xb, memb, G, S, cos2, sin2, tag, sched=None):
    D = d.D
    mm = functools.partial(_matmul, sched=sched)
    P = mm(Xb, G["in"], "nt", name=f"proj_{tag}", tm=1024, tn=_tile(d.IN, 1280))
    sgu = _gmlp_fwd(d, P, S["ln_v_g"], S["ln_v_b"], S["w_s"], S["b_sT"], name=f"gmlp_fwd_{tag}", sched=sched)
    att = _swa_fwd(d, P, cos2, sin2, S["sinks"], name=f"swa_fwd_{tag}", sched=sched)
    ya, yb, merged = _branch_merge(d, sgu, att, G["a"], G["b"], P, S["b_gate"], name=f"merge_{tag}", sched=sched)
    X1, X1b, xh1, rs1 = mm(
        merged, G["o"], "nn", name=f"oproj_ln_{tag}", tn=D, tk=2048, out_defs=_resid_ln_outs(), epilogue=_ep_resid_ln,
        extras=((X, "mn", 0), (S["ln1_g"], "row", 0), (S["ln1_b"], "row", 0)),
    )
    kv = mm(memb, G["xkv"], "nn", name=f"xkv_{tag}", out_defs=(("mn", BF16),))
    X2, X2b, xh2, rs2, q_b, o_b = _xattn_fwd(d, X1, X1b, kv, G["xq"], G["xo"], S["ln2_g"], S["ln2_b"], name=f"xattn_fwd_{tag}", sched=sched)
    H, A = mm(X2b, G["up"], "nt", name=f"up_{tag}", tm=1024, out_defs=(("mn", F32), ("mn", BF16)), epilogue=_ep_relu2)
    X3, X3b, xh3, rs3 = mm(
        A, G["down"], "nn", name=f"down_ln_{tag}", tn=D, tk=2048, out_defs=_resid_ln_outs(), epilogue=_ep_resid_ln,
        extras=((X2, "mn", 0), (S["ln3_g"], "row", 0), (S["ln3_b"], "row", 0)),
    )
    saved = dict(Xb=Xb, P=P, sgu=sgu, att=att, ya=ya, yb=yb, merged=merged, X1b=X1b, xh1=xh1, rs1=rs1, kv=kv, q_b=q_b,
                 o_b=o_b, X2b=X2b, xh2=xh2, rs2=rs2, H=H, A=A, xh3=xh3, rs3=rs3)
    return X3, X3b, saved


def _layer_bwd(d, dXout, sv, memb, G, S, cos2, sin2, tag, sched=None, on_grad=None, target=None):
    D = d.D
    mm = functools.partial(_matmul, sched=sched)
    emit = on_grad if on_grad is not None else (lambda n, g: None)
    wide = dict(tn=D)
    _, _, off_k, _, off_ga, off_gb = _offsets(d)
    bf = (("mn", BF16),)
    dz3, dz3b, dg3, db3, *loss = _ln_bwd(dXout, sv["xh3"], sv["rs3"], S["ln3_g"], name=f"ln3_bwd_{tag}", sched=sched, target=target)
    dHb = mm(dz3b, G["down"], "nt", name=f"down_dx_{tag}", tm=1024, out_defs=bf, epilogue=_ep_relu2_bwd, extras=((sv["H"], "mn", 0),))
    g_down = mm(sv["A"], dz3b, "tn", name=f"down_dw_{tag}", **wide)
    emit("down", g_down)
    ln_bwd_outs = (("mn", F32), ("mn", BF16), ("acc", F32), ("acc", F32))
    dz2, dz2b, dg2, db2 = mm(
        dHb, G["up"], "nn", name=f"up_dx_{tag}", tn=D, tk=2048, out_defs=ln_bwd_outs, epilogue=_ep_resid_ln_bwd,
        extras=((dz3, "mn", 0), (sv["xh2"], "mn", 0), (sv["rs2"], "m1", 0), (S["ln2_g"], "row", 0)),
    )
    g_up = mm(dHb, sv["X2b"], "tn", name=f"up_dw_{tag}", **wide)
    emit("up", g_up)
    dz1, dz1b, dg1, db1, dq_b, dkv = _xattn_bwd(d, dz2, dz2b, sv["q_b"], sv["kv"], G["xq"], G["xo"], sv["xh1"], sv["rs1"], S["ln1_g"],
                                               name=f"xattn_bwd_{tag}", sched=sched)
    g_xo = mm(dz2b, sv["o_b"], "tn", name=f"xo_dw_{tag}")
    emit("xo", g_xo)
    g_xq = mm(sv["X1b"], dq_b, "tn", name=f"xq_dw_{tag}")
    emit("xq", g_xq)
    dkv_b = dkv.transpose(1, 0, 2).reshape(d.MEM, 2 * d.XW).astype(BF16)
    g_xkv = mm(memb, dkv_b, "tn", name=f"xkv_dw_{tag}")
    emit("xkv", g_xkv)
    dya_b, dyb_b, dP, dgb_b, dbga, dbgb = mm(
        dz1b, G["o"], "nt", name=f"oproj_dx_{tag}", tm=1024, tn=512,
        out_defs=(("mn", BF16), ("mn", BF16), ("cols", BF16, d.IN, off_ga), ("mn", BF16), ("acc", F32), ("acc", F32)),
        epilogue=_ep_gate_bwd,
        extras=((sv["P"], "mn", off_ga), (sv["P"], "mn", off_gb), (sv["ya"], "mn", 0), (sv["yb"], "mn", 0),
                (S["b_gate"], "row", 0), (S["b_gate"], "row", D)),
    )
    g_o = mm(sv["merged"], dz1b, "tn", name=f"oproj_dw_{tag}", **wide)
    emit("o", g_o)
    dP = _place_cols(dP, dgb_b, off_gb, name=f"place_dgb_{tag}")
    dP, dwm, dbsT, dlvg, dlvb = _gmlp_bwd(d, sv["P"], dya_b, G["a"], S["ln_v_g"], S["ln_v_b"], S["w_s"], S["b_sT"], dP, name=f"gmlp_bwd_{tag}", sched=sched)
    g_a = mm(dya_b, sv["sgu"], "tn", name=f"bra_dw_{tag}")
    emit("a", g_a)
    dP, dkvr_b, dsk = _swa_bwd(d, sv["P"], dyb_b, G["b"], cos2, sin2, S["sinks"], dP, name=f"swa_bwd_{tag}", sched=sched)
    g_b = mm(dyb_b, sv["att"], "tn", name=f"brb_dw_{tag}")
    emit("b", g_b)
    dP = _place_cols(dP, dkvr_b, off_k, name=f"place_dkv_{tag}")
    small = dict(b_gate=jnp.concatenate([dbga, dbgb], axis=1), ln_v_g=dlvg, ln_v_b=dlvb, w_s=dwm, b_sT=dbsT, sinks=dsk,
                 ln1_g=dg1, ln1_b=db1, ln2_g=dg2, ln2_b=db2, ln3_g=dg3, ln3_b=db3)
    emit("small", small)
    g_in = mm(dP, sv["Xb"], "tn", name=f"proj_dw_{tag}", **wide)
    emit("in", g_in)
    dXin = mm(dP, G["in"], "nn", name=f"proj_dx_{tag}", tn=D, tk=_tile(d.IN, 1920), epilogue=_ep_resid, extras=((dz1, "mn", 0),))
    big = {"in": g_in, "a": g_a, "b": g_b, "o": g_o, "xq": g_xq, "xkv": g_xkv, "xo": g_xo, "up": g_up, "down": g_down}
    return dXin, big, small, (loss[0] if loss else None)


def _rope_tables(T):
    inv = 1.0 / (ROPE_THETA ** (jnp.arange(0, HEAD_DIM, 2, dtype=F32) / HEAD_DIM))
    ang = jnp.arange(T, dtype=F32)[:, None] * inv[None, :]
    cos, sin = jnp.cos(ang), jnp.sin(ang)
    reps = LANES // HEAD_DIM
    return jnp.concatenate([cos, cos] * reps, axis=1), jnp.concatenate([-sin, sin] * reps, axis=1)


def _local_step(d, x, mem, target, Gs, Ss, sched=None, on_grad=None):
    cos2, sin2 = _rope_tables(d.T)
    memb = mem.astype(BF16)
    X, Xb = x, x.astype(BF16)
    saved = []
    for l in range(DEPTH):
        X, Xb, sv = _layer_fwd(d, X, Xb, memb, Gs[l], Ss[l], cos2, sin2, f"l{l}", sched)
        saved.append(sv)
    bigs, smalls = [None] * DEPTH, [None] * DEPTH
    dX, loss = X, None
    for l in reversed(range(DEPTH)):
        emit = None if on_grad is None else functools.partial(on_grad, l)
        head = target if l == DEPTH - 1 else None
        dX, bigs[l], smalls[l], got = _layer_bwd(d, dX, saved[l], memb, Gs[l], Ss[l], cos2, sin2, f"l{l}", sched, emit, head)
        loss = got if got is not None else loss
    return loss, dX, bigs, smalls


def _place():
    x, y, c = lax.axis_index("x"), lax.axis_index("y"), lax.axis_index("c")
    return x, y, c, [(1 - x, y), (x, 1 - y), (1 - x, 1 - y)]


def _all_gather(shards, *, name, sched=None):
    n = len(shards)
    any_spec = pl.BlockSpec(memory_space=pl.ANY)

    def body(*refs):
        ins, outs = refs[:n], refs[n : 2 * n]
        send_sems, recv_sems, local_sems = refs[2 * n :]
        x, y, c, chips = _place()
        me, sibling = (x, y, c), (x, y, 1 - c)

        def rows(w, p):
            r = ins[w].shape[0]
            return outs[w].at[pl.ds((4 * p[0] + 2 * p[1] + p[2]) * r, r), :]

        def copy(w, k, block, to, src=None):
            return pltpu.make_async_remote_copy(
                src_ref=rows(w, block) if src is None else src, dst_ref=rows(w, block),
                send_sem=send_sems.at[7 * w + k], recv_sem=recv_sems.at[7 * w + k], device_id=to, device_id_type=MESH)

        mine = [pltpu.make_async_copy(ins[w], rows(w, me), local_sems.at[w]) for w in range(n)]
        for cp in mine:
            cp.start()
        first = []
        for w in range(n):
            first.append(copy(w, 0, me, sibling, src=ins[w]))
            first += [copy(w, 1 + j, me, (*chip, c), src=ins[w]) for j, chip in enumerate(chips)]
        for cp in first:
            cp.start()
        passed = []
        for j, chip in enumerate(chips):
            for w in range(n):
                copy(w, 1 + j, (*chip, c), me).wait_recv()
                fwd = copy(w, 4 + j, (*chip, c), sibling)
                fwd.start()
                passed.append(fwd)
        for w in range(n):
            copy(w, 0, sibling, me).wait_recv()
            for j, chip in enumerate(chips):
                copy(w, 4 + j, (*chip, 1 - c), me).wait_recv()
        for cp in first + passed:
            cp.wait_send()
        for cp in mine:
            cp.wait()

    return _hosted(
        body,
        name=name,
        sched=sched,
        in_specs=[any_spec] * n,
        out_specs=[any_spec] * n,
        out_shape=[jax.ShapeDtypeStruct((N_DEV * s.shape[0], s.shape[1]), s.dtype) for s in shards],
        scratch_shapes=[pltpu.SemaphoreType.DMA((7 * n,)), pltpu.SemaphoreType.DMA((7 * n,)), pltpu.SemaphoreType.DMA((n,))],
    )(*shards)


class _Task:
    def __init__(self, ins, out_shapes, n_remote, n_local, plan, done, aliases=None, sibling=False, chips=False):
        self.ins, self.out_shapes, self.n_remote, self.n_local = list(ins), list(out_shapes), n_remote, n_local
        self.plan, self.done, self.aliases = plan, done, dict(aliases or {})
        self.sibling, self.chips = sibling, chips


class _Sched:
    def __init__(self):
        self.pending = {}

    def at(self, host, task):
        self.pending.setdefault(host, []).append(task)

    def take(self, host):
        return self.pending.pop(host, [])


def _in_hbm(out_shape):
    if isinstance(out_shape, (list, tuple)):
        return [_in_hbm(s) for s in out_shape]
    return pltpu.HBM(out_shape.shape, out_shape.dtype)


def _hosted(body, *, name, sched=None, tasks=(), grid=(), in_specs=None, out_specs=None, out_shape=(), scratch_shapes=(),
            compiler_params=None, input_output_aliases=None):
    tasks = list(tasks) + (sched.take(name) if sched is not None else [])
    scratch_shapes = list(scratch_shapes)
    kwargs = dict(name=name)
    core_aliases = dict(input_output_aliases or {})
    if grid:
        kwargs["grid"] = grid
    if compiler_params is not None:
        kwargs["compiler_params"] = compiler_params
    if not tasks:
        if in_specs is not None:
            kwargs.update(in_specs=in_specs, out_specs=out_specs)
        return pl.pallas_call(body, out_shape=_in_hbm(out_shape), scratch_shapes=scratch_shapes, input_output_aliases=core_aliases, **kwargs)
    assert in_specs is not None and out_specs is not None, name
    with_sibling, with_chips = any(t.sibling for t in tasks), any(t.chips for t in tasks)
    collective_id = {(True, False): 1, (False, True): 2, (True, True): 3}[(with_sibling, with_chips)]
    kwargs["compiler_params"] = dataclasses.replace(
        compiler_params if compiler_params is not None else pltpu.CompilerParams(), collective_id=collective_id)
    single = not isinstance(out_shape, (list, tuple))
    core_shape = [out_shape] if single else list(out_shape)
    core_specs = [out_specs] if single else list(out_specs)
    in_specs = list(in_specs)
    n_in, n_out, n_scr = len(in_specs), len(core_shape), len(scratch_shapes)
    t_ins = [a for t in tasks for a in t.ins]
    t_outs = [s for t in tasks for s in t.out_shapes]
    n_rem = sum(t.n_remote for t in tasks)
    n_loc = sum(t.n_local for t in tasks)
    aliases, pos_in, pos_out = dict(core_aliases), n_in, n_out
    for t in tasks:
        for a, b in t.aliases.items():
            aliases[pos_in + a] = pos_out + b
        pos_in += len(t.ins)
        pos_out += len(t.out_shapes)
    any_spec = pl.BlockSpec(memory_space=pl.ANY)

    def wrapped(*refs):
        core_in, t_in = refs[:n_in], refs[n_in : n_in + len(t_ins)]
        o0 = n_in + len(t_ins)
        core_out, t_out = refs[o0 : o0 + n_out], refs[o0 + n_out : o0 + n_out + len(t_outs)]
        s0 = o0 + n_out + len(t_outs)
        core_scr = refs[s0 : s0 + n_scr]
        send_sems, recv_sems, local_sems = refs[s0 + n_scr :]
        remote, local = [], []
        pi = po = pr = pq = 0
        for t in tasks:
            r, q = t.plan(t_in[pi : pi + len(t.ins)], t_out[po : po + len(t.out_shapes)], send_sems, recv_sems, local_sems, pr, pq)
            assert len(r) == t.n_remote and len(q) == t.n_local
            remote += r
            local += q
            pi, po, pr, pq = pi + len(t.ins), po + len(t.out_shapes), pr + t.n_remote, pq + t.n_local

        def start():
            x, y, c, chips = _place()
            peers = ([(x, y, 1 - c)] if with_sibling else []) + ([(*chip, c) for chip in chips] if with_chips else [])
            barrier = pltpu.get_barrier_semaphore()
            for peer in peers:
                pl.semaphore_signal(barrier, inc=1, device_id=peer, device_id_type=MESH)
            pl.semaphore_wait(barrier, len(peers))
            for cp in local + remote:
                cp.start()

        def finish():
            for cp in remote + local:
                cp.wait()

        if grid:
            ids = [pl.program_id(a) for a in range(len(grid))]
            first = functools.reduce(jnp.logical_and, [i == 0 for i in ids])
            last = functools.reduce(jnp.logical_and, [i == g - 1 for i, g in zip(ids, grid)])
            pl.when(first)(start)
            body(*core_in, *core_out, *core_scr)
            pl.when(last)(finish)
        else:
            start()
            body(*core_in, *core_out, *core_scr)
            finish()

    call = pl.pallas_call(
        wrapped,
        in_specs=in_specs + [any_spec] * len(t_ins),
        out_specs=core_specs + [any_spec] * len(t_outs),
        out_shape=_in_hbm(core_shape + t_outs),
        scratch_shapes=scratch_shapes
        + [pltpu.SemaphoreType.DMA((n_rem,)), pltpu.SemaphoreType.DMA((n_rem,)), pltpu.SemaphoreType.DMA((max(n_loc, 1),))],
        input_output_aliases=aliases,
        **kwargs,
    )

    def run(*operands):
        outs = call(*operands, *t_ins)
        po = n_out
        for t in tasks:
            t.done(list(outs[po : po + len(t.out_shapes)]))
            po += len(t.out_shapes)
        return outs[0] if single else list(outs[:n_out])

    return run


def _comm_only(tasks, *, name):
    _hosted(lambda: None, name=name, tasks=tasks, in_specs=[], out_shape=[], out_specs=[])()


def _rows(ref, block, n_blocks):
    r = ref.shape[0] // n_blocks
    return ref.at[pl.ds(block * r, r), :]


def _remote(src, dst, send_sems, recv_sems, k, to):
    return pltpu.make_async_remote_copy(src_ref=src, dst_ref=dst, send_sem=send_sems.at[k], recv_sem=recv_sems.at[k],
                                        device_id=to, device_id_type=MESH)


def _gather_stage1(shards, done, part=(0, 1, 1), into=None):
    n = len(shards)
    k0, k1, n_parts = part

    def plan(ins, outs, send_sems, recv_sems, local_sems, pr, pq):
        x, y, c, chips = _place()
        mine = 4 * x + 2 * y + c
        remote, local = [], []
        for w in range(n):
            r = shards[w].shape[0]
            rp = r // n_parts
            src = ins[w].at[pl.ds(k0 * rp, (k1 - k0) * rp), :]
            dst = outs[w].at[pl.ds(mine * r + k0 * rp, (k1 - k0) * rp), :]
            local.append(pltpu.make_async_copy(src, dst, local_sems.at[pq + w]))
            for j, to in enumerate([(x, y, 1 - c)] + [(*chip, c) for chip in chips]):
                remote.append(_remote(src, dst, send_sems, recv_sems, pr + 4 * w + j, to))
        return remote, local

    shapes = [jax.ShapeDtypeStruct((N_DEV * s.shape[0], s.shape[1]), s.dtype) for s in shards]
    if into is None:
        return _Task(shards, shapes, 4 * n, n, plan, done, sibling=True, chips=True)
    return _Task(list(shards) + list(into), shapes, 4 * n, n, plan, done, aliases={n + w: w for w in range(n)}, sibling=True, chips=True)


def _gather_stage2(partial, done):
    n = len(partial)

    def plan(ins, outs, send_sems, recv_sems, local_sems, pr, pq):
        x, y, c, chips = _place()
        remote = []
        for w in range(n):
            for j, chip in enumerate(chips):
                rows = _rows(outs[w], 4 * chip[0] + 2 * chip[1] + c, N_DEV)
                remote.append(_remote(rows, rows, send_sems, recv_sems, pr + 3 * w + j, (x, y, 1 - c)))
        return remote, []

    shapes = [jax.ShapeDtypeStruct(p.shape, p.dtype) for p in partial]
    return _Task(partial, shapes, 3 * n, 0, plan, done, aliases={w: w for w in range(n)}, sibling=True)


def _sibling_stage(grads, done):
    n = len(grads)

    def plan(ins, outs, send_sems, recv_sems, local_sems, pr, pq):
        x, y, c, _ = _place()
        remote = []
        for w in range(n):
            for k in range(4):
                src = _rows(ins[w], 4 * (k // 2) + 2 * (k % 2) + (1 - c), N_DEV)
                remote.append(_remote(src, _rows(outs[w], k, 4), send_sems, recv_sems, pr + 4 * w + k, (x, y, 1 - c)))
        return remote, []

    shapes = [jax.ShapeDtypeStruct((g.shape[0] // 2, g.shape[1]), g.dtype) for g in grads]
    return _Task(grads, shapes, 4 * n, 0, plan, done, sibling=True)


def _chips_stage(sends, done, part=(0, 1), into=None):
    n = len(sends)
    k, n_parts = part

    def plan(ins, outs, send_sems, recv_sems, local_sems, pr, pq):
        x, y, c, chips = _place()
        remote = []
        for w in range(n):
            r = sends[w].shape[0] // 3
            rp = r // n_parts
            for j, chip in enumerate(chips):
                rows = pl.ds(j * r + k * rp, rp)
                remote.append(_remote(ins[w].at[rows, :], outs[w].at[rows, :], send_sems, recv_sems, pr + 3 * w + j, (*chip, c)))
        return remote, []

    shapes = [jax.ShapeDtypeStruct(s.shape, s.dtype) for s in sends]
    if into is None:
        return _Task(sends, shapes, 3 * n, 0, plan, done, chips=True)
    return _Task(list(sends) + list(into), shapes, 3 * n, 0, plan, done, aliases={n + w: w for w in range(n)}, chips=True)


def _pair_sum(grad, got, place, *, name):
    R, C = grad.shape
    r = R // N_DEV
    tr = _tile(r, 256, 16)
    nt = r // tr

    def chip_of(s, pr):
        fx = jnp.where((s == 1) | (s == 3), 1, 0)
        fy = jnp.where(s >= 2, 1, 0)
        return pr[0] ^ fx, pr[1] ^ fy

    def grad_map(s, t, pr):
        kx, ky = chip_of(s, pr)
        return ((4 * kx + 2 * ky + pr[2]) * nt + t, 0)

    def got_map(s, t, pr):
        kx, ky = chip_of(s, pr)
        return ((2 * kx + ky) * nt + t, 0)

    def body(pr, g_ref, r_ref, own_ref, send_ref):
        s = pl.program_id(0)
        tot = g_ref[...] + r_ref[...]

        @pl.when(s == 0)
        def _():
            own_ref[...] = tot

        @pl.when(s > 0)
        def _():
            send_ref[...] = tot.astype(BF16)

    return pl.pallas_call(
        body,
        name=name,
        grid_spec=pltpu.PrefetchScalarGridSpec(
            num_scalar_prefetch=1,
            grid=(4, nt),
            in_specs=[pl.BlockSpec((tr, C), grad_map), pl.BlockSpec((tr, C), got_map)],
            out_specs=[
                pl.BlockSpec((tr, C), lambda s, t, pr: (jnp.where(s == 0, t, nt - 1), 0)),
                pl.BlockSpec((tr, C), lambda s, t, pr: (jnp.where(s == 0, 0, (s - 1) * nt + t), 0)),
            ],
        ),
        out_shape=[jax.ShapeDtypeStruct((r, C), F32), jax.ShapeDtypeStruct((3 * r, C), BF16)],
        compiler_params=_params(("arbitrary", "arbitrary"), [((tr, C), F32)] * 4),
    )(place, grad, got)


def _adam_vals(w, g, m, v):
    m = ADAM_B1 * m + (1.0 - ADAM_B1) * g
    v = ADAM_B2 * v + (1.0 - ADAM_B2) * (g * g)
    m_hat = m / (1.0 - ADAM_B1**ADAM_STEP)
    v_hat = v / (1.0 - ADAM_B2**ADAM_STEP)
    delta = -ADAM_LR * (m_hat / (jnp.sqrt(v_hat) + ADAM_EPS) + ADAM_WD * w)
    return delta, m, v


def _adam_big(w3, m3, v3, own, got, layer, transposed, prev, *, name, sched=None):
    _, A, B = w3.shape
    ta = _tile(A, 256, 16)
    nt = A // ta
    b_pad = (-B) % LANES

    def body(*refs):
        w_ref, m_ref, v_ref, own_ref, g1_ref, g2_ref, g3_ref = refs[:7]
        g_ref, d_ref, nm_ref, nv_ref = refs[-4:]
        g = ((own_ref[...] + g1_ref[...].astype(F32)) + g2_ref[...].astype(F32)) + g3_ref[...].astype(F32)
        if transposed:
            if b_pad:
                g = jnp.concatenate([g, jnp.zeros((b_pad, ta), F32)], axis=0)
            g = g.T[:, :B]
        g_ref[...] = g
        d_ref[...], nm_ref[...], nv_ref[...] = _adam_vals(w_ref[...], g, m_ref[...], v_ref[...])

    nat = pl.BlockSpec((None, ta, B), lambda t: (layer, t, 0))
    if transposed:
        parts = [pl.BlockSpec((B, ta), lambda t: (0, t))] + [pl.BlockSpec((B, ta), lambda t, j=j: (j, t)) for j in range(3)]
    else:
        parts = [pl.BlockSpec((ta, B), lambda t: (t, 0))] + [pl.BlockSpec((ta, B), lambda t, j=j: (j * nt + t, 0)) for j in range(3)]
    keep = [] if prev is None else list(prev)
    outs = _hosted(
        body,
        name=name,
        sched=sched,
        grid=(nt,),
        in_specs=[nat, nat, nat] + parts + [pl.BlockSpec(memory_space=pl.ANY)] * len(keep),
        out_specs=[nat] * 4,
        out_shape=[jax.ShapeDtypeStruct(w3.shape, F32)] * 4,
        input_output_aliases={7 + k: k for k in range(len(keep))},
        compiler_params=_params(("arbitrary",), [((ta, B), F32)] * 10),
    )(w3, m3, v3, own, got, got, got, *keep)
    return list(outs)


def _adam_small(w, m, v, parts, *, name, sched=None):
    R = w.shape[0]

    def body(w_ref, m_ref, v_ref, p_ref, g_ref, d_ref, nm_ref, nv_ref):
        g = p_ref[pl.ds(0, R), :]
        for k in range(1, N_DEV):
            g = g + p_ref[pl.ds(k * R, R), :]
        g_ref[...] = g
        d_ref[...], nm_ref[...], nv_ref[...] = _adam_vals(w_ref[...], g, m_ref[...], v_ref[...])

    return _hosted(
        body,
        name=name,
        sched=sched,
        out_shape=[jax.ShapeDtypeStruct((R, LANES), F32)] * 4,
        compiler_params=_params(None, [((R, LANES), F32)] * 16),
    )(w, m, v, parts)


SMALL_NAMES = ("b_gate", "ln_v_g", "ln_v_b", "w_s", "b_s", "sinks", "ln1_g", "ln1_b", "ln2_g", "ln2_b", "ln3_g", "ln3_b")
PACK_ALIGN = 8 * LANES


def _pack(arrays):
    flat = []
    for a in arrays:
        a = a.reshape(-1)
        flat.append(jnp.pad(a, (0, (-a.shape[0]) % PACK_ALIGN)))
    return jnp.concatenate(flat).reshape(-1, LANES)


def _unpack(packed, shapes):
    flat = packed.reshape(-1)
    out, pos = [], 0
    for s in shapes:
        n = 1
        for e in s:
            n *= e
        out.append(flat[pos : pos + n].reshape(s))
        pos += n + (-n) % PACK_ALIGN
    return out


BIG = (("in", "w_in", True), ("a", "w_br_a", True), ("b", "w_br_b", True), ("o", "w_o", False), ("xq", "w_xq", False),
       ("xkv", "w_xkv", False), ("xo", "w_xo", True), ("up", "w_up", True), ("down", "w_down", False))


SMALL_BIG = ("a", "b", "o", "xq", "xkv", "xo")
GATHER_PLAN = (
    (0, ("in",), None, None),
    (0, SMALL_BIG, ("proj_l0",), "swa_fwd_l0"),
    (0, ("up",), ("swa_fwd_l0", "merge_l0"), "oproj_ln_l0"),
    (0, ("down",), ("oproj_ln_l0", "xattn_fwd_l0"), "up_l0"),
    (1, ("in",), ("up_l0",), "down_ln_l0"),
    (1, SMALL_BIG, ("down_ln_l0",), "proj_l1"),
    (1, ("up",), ("down_ln_l0", ("proj_l1", 3)), "swa_fwd_l1"),
    (1, ("down",), (("swa_fwd_l1", 2), "merge_l1", "oproj_ln_l1"), "xattn_fwd_l1"),
)
EARLY_SMALL_BIG = ("o", "xq", "xkv", "xo")
LATE_SMALL_BIG = ("a", "b")
GRAD_HOSTS = {
    1: {"down": ("xattn_bwd_l1", "swa_bwd_l1"), "up": ("xattn_bwd_l1", ("gmlp_bwd_l1", "proj_dw_l1")),
        **{g: ("gmlp_bwd_l1", "proj_dx_l1") for g in EARLY_SMALL_BIG}, **{g: ("proj_dw_l1", "proj_dx_l1") for g in LATE_SMALL_BIG},
        "in": ("proj_dx_l1", ("down_dx_l0", "down_dw_l0"))},
    0: {"down": ("xattn_bwd_l0", ("oproj_dx_l0", "gmlp_bwd_l0")), "up": ("xattn_bwd_l0", "swa_bwd_l0"),
        **{g: ("gmlp_bwd_l0", "proj_dw_l0") for g in EARLY_SMALL_BIG}, **{g: ("proj_dw_l0", "proj_dx_l0") for g in LATE_SMALL_BIG},
        "in": (None, "proj_dx_l0")},
}
SMALL_GRAD_HOSTS = ("proj_dw_l0", "proj_dx_l0")


def kernel(x, mem, w_in, b_gate, ln_v_g, ln_v_b, w_s, b_s, sinks, w_br_a, w_br_b, w_o, ln1_g, ln1_b, w_xq, w_xkv, w_xo, ln2_g, ln2_b, w_up, w_down, ln3_g, ln3_b, loss_target, m_w_in, m_b_gate, m_ln_v_g, m_ln_v_b, m_w_s, m_b_s, m_sinks, m_w_br_a, m_w_br_b, m_w_o, m_ln1_g, m_ln1_b, m_w_xq, m_w_xkv, m_w_xo, m_ln2_g, m_ln2_b, m_w_up, m_w_down, m_ln3_g, m_ln3_b, v_w_in, v_b_gate, v_ln_v_g, v_ln_v_b, v_w_s, v_b_s, v_sinks, v_w_br_a, v_w_br_b, v_w_o, v_ln1_g, v_ln1_b, v_w_xq, v_w_xkv, v_w_xo, v_ln2_g, v_ln2_b, v_w_up, v_w_down, v_ln3_g, v_ln3_b):
    given = dict(locals())
    T, D = x.shape[1], x.shape[2]
    IN, GW, AW, XW, DFF = w_in.shape[2] * N_DEV, ln_v_g.shape[1], w_br_b.shape[1], w_xq.shape[2], w_up.shape[2] * N_DEV
    KVW = (IN - 2 * GW - AW - 2 * D) // 2
    d = Dims(T=T, D=D, IN=IN, GW=GW, G=GW // GROUP_DIM, AW=AW, KVW=KVW, HQ=AW // HEAD_DIM, HKV=KVW // HEAD_DIM, XW=XW,
             XH=XW // X_HEAD_DIM, MEM=mem.shape[1], DFF=DFF)
    place = jnp.stack([lax.axis_index("x"), lax.axis_index("y"), lax.axis_index("c")]).astype(jnp.int32)

    sched = _Sched()
    big_of = {g: (p, tr) for g, p, tr in BIG}

    def shard(l, g):
        p, tr = big_of[g]
        return (given[p][l].T if tr else given[p][l]).astype(BF16)

    Gs = [{}, {}]

    def plan_gather(l, names, hosts1, host2):
        shards = [shard(l, g) for g in names]
        if hosts1 is None:
            Gs[l].update(zip(names, _all_gather(shards, name=f"gather_{names[0]}_l{l}")))
            return

        spans = [(h, 1) if isinstance(h, str) else h for h in hosts1]
        total = sum(cnt for _, cnt in spans)

        def register(i, k0, into):
            host, cnt = spans[i]

            def done(outs):
                if i + 1 < len(spans):
                    register(i + 1, k0 + cnt, outs)
                else:
                    sched.at(host2, _gather_stage2(outs, lambda full: Gs[l].update(zip(names, full))))

            sched.at(host, _gather_stage1(shards, done, part=(k0, k0 + cnt, total), into=into))

        register(0, 0, None)

    for entry in GATHER_PLAN:
        plan_gather(*entry)
    Ss = []
    for l in range(DEPTH):
        S = {n: given[n][l].reshape(1, -1) for n in SMALL_NAMES if n not in ("w_s", "b_s", "sinks")}
        S["w_s"], S["b_sT"], S["sinks"] = w_s[l], b_s[l].T, sinks[l]
        Ss.append(S)

    owns, recvs = {}, {}

    smalls_seen = {}
    gathered_small = []

    def pack_small(src):
        parts = []
        for l in range(DEPTH):
            for n in SMALL_NAMES:
                a = src[l]["b_sT"].T if n == "b_s" else src[l][n]
                parts.append(a[0, : d.HQ] if n == "sinks" else a)
        return _pack(parts)

    def on_small(l, small):
        smalls_seen[l] = small
        if len(smalls_seen) == DEPTH:
            host1, host2 = SMALL_GRAD_HOSTS
            sched.at(host1, _gather_stage1([pack_small(smalls_seen)], lambda part: sched.at(
                host2, _gather_stage2(part, lambda full: gathered_small.append(full[0])))))

    def on_grad(l, g, grad):
        if g == "small":
            return on_small(l, grad)
        sib_host, chips_host = GRAD_HOSTS[l][g]

        def after_sibling(got):
            owns[(l, g)], send = _pair_sum(grad, got[0], place, name=f"grad_pair_sum_{g}_l{l}")
            hosts = chips_host if isinstance(chips_host, tuple) else (chips_host,)

            def register(k, into):
                def done(r):
                    if k + 1 < len(hosts):
                        register(k + 1, r)
                    else:
                        recvs[(l, g)] = r[0]

                task = _chips_stage([send], done, part=(k, len(hosts)), into=into)
                if hosts[k] is None:
                    _comm_only([task], name=f"grad_chips_{g}_l{l}")
                else:
                    sched.at(hosts[k], task)

            register(0, None)

        task = _sibling_stage([grad], after_sibling)
        if sib_host is None:
            _comm_only([task], name=f"grad_sibling_{g}_l{l}")
        else:
            sched.at(sib_host, task)

    loss, dX, bigs, smalls = _local_step(d, x[0], mem[0], loss_target[0], Gs, Ss, sched, on_grad)
    loss = lax.psum(loss[0, 0], ("x", "y", "c"))

    assert not sched.pending, sorted(sched.pending)

    def viewed(p, tr):
        return tr and given[p].shape[2] % LANES != 0

    res = {p: None for _, p, _ in BIG}
    for l in reversed(range(DEPTH)):
        for g, p, tr in BIG:
            view = viewed(p, tr)
            w3, m3, v3 = ((jnp.swapaxes(a, 1, 2) if view else a) for a in (given[p], given["m_" + p], given["v_" + p]))
            res[p] = _adam_big(w3, m3, v3, owns[(l, g)], recvs[(l, g)], l, tr and not view, res[p], name=f"adamw_{g}_l{l}")
    for g, p, tr in BIG:
        if viewed(p, tr):
            res[p] = [jnp.swapaxes(a, 1, 2) for a in res[p]]

    shapes = [given[n].shape[1:] for n in SMALL_NAMES]
    pw, pm, pv = (_pack([given[pre + n][l] for l in range(DEPTH) for n in SMALL_NAMES]) for pre in ("", "m_", "v_"))
    small_out = [_unpack(o, shapes * DEPTH) for o in _adam_small(pw, pm, pv, gathered_small[0], name="adamw_small")]

    names = ["w_in", "b_gate", "ln_v_g", "ln_v_b", "w_s", "b_s", "sinks", "w_br_a", "w_br_b", "w_o", "ln1_g", "ln1_b", "w_xq",
             "w_xkv", "w_xo", "ln2_g", "ln2_b", "w_up", "w_down", "ln3_g", "ln3_b"]
    out = [loss, dX[None]]
    for kind in range(4):
        for n in names:
            if n in SMALL_NAMES:
                i = SMALL_NAMES.index(n)
                out.append(jnp.stack([small_out[kind][l * len(SMALL_NAMES) + i] for l in range(DEPTH)]))
            else:
                out.append(res[n][kind])
    return tuple(out)
```

```python
import collections
import dataclasses
import functools

import jax
import jax.numpy as jnp
from jax import lax
from jax.experimental import pallas as pl
from jax.experimental.pallas import tpu as pltpu

F32 = jnp.float32
BF16 = jnp.bfloat16
MESH = pl.DeviceIdType.MESH

N_DEV = 8
DEPTH = 2
CHUNK = 128
HEAD_DIM = 64
ROPE_HALF = HEAD_DIM // 2
X_HEAD_DIM = 128
GROUP_DIM = 128
LANES = 128
ROPE_THETA = 10000.0
LN_EPS = 1e-5
ALPHA = (2 * DEPTH) ** 0.25
ADAM_LR = 0.001
ADAM_B1 = 0.9
ADAM_B2 = 0.999
ADAM_EPS = 1e-08
ADAM_WD = 0.01
ADAM_STEP = 10
VMEM_BYTES = 64 * 1024 * 1024
VMEM_TEMP_BYTES = 20 * 1024 * 1024

Dims = collections.namedtuple("Dims", "T D IN GW G AW KVW HQ HKV XW XH MEM DFF")


def _offsets(d):
    off_v = d.GW
    off_q = 2 * d.GW
    off_k = off_q + d.AW
    off_va = off_k + d.KVW
    off_ga = off_va + d.KVW
    off_gb = off_ga + d.D
    return off_v, off_q, off_k, off_va, off_ga, off_gb


def _tile(dim, pref, align=LANES):
    if dim <= pref:
        return dim
    t = pref - pref % align
    while t >= align:
        if dim % t == 0:
            return t
        t -= align
    return dim


def _nbytes(shape, dtype):
    n = 1
    for s in shape:
        n *= s
    return n * jnp.dtype(dtype).itemsize


def _params(sem, blocks, scratch=0):
    need = 2 * sum(_nbytes(s, t) for s, t in blocks) + scratch + VMEM_TEMP_BYTES
    limit = min(max(need, 32 * 1024 * 1024), VMEM_BYTES - 4 * 1024 * 1024)
    return pltpu.CompilerParams(dimension_semantics=sem, vmem_limit_bytes=limit)


def _dot(a, b, kind):
    dn = {"nn": (((1,), (0,)), ((), ())), "nt": (((1,), (1,)), ((), ())), "tn": (((0,), (0,)), ((), ()))}[kind]
    return lax.dot_general(a, b, dn, preferred_element_type=F32)


def _gelu(x):
    c = 0.7978845608028654
    t = jnp.tanh(c * (x + 0.044715 * (x * x * x)))
    return 0.5 * x * (1.0 + t)


def _gelu_grad(x):
    c = 0.7978845608028654
    x2 = x * x
    t = jnp.tanh(c * (x + 0.044715 * (x2 * x)))
    return 0.5 * (1.0 + t) + 0.5 * x * (1.0 - t * t) * (c * (1.0 + 3.0 * 0.044715 * x2))


def _ln_fwd(z, g, b):
    mu = jnp.mean(z, axis=-1, keepdims=True)
    zc = z - mu
    var = jnp.mean(zc * zc, axis=-1, keepdims=True)
    rstd = lax.rsqrt(var + LN_EPS)
    xhat = zc * rstd
    return xhat * g + b, xhat, rstd


def _ln_bwd_vals(dy, xhat, rstd, g):
    gd = dy * g
    m1 = jnp.mean(gd, axis=-1, keepdims=True)
    m2 = jnp.mean(gd * xhat, axis=-1, keepdims=True)
    return rstd * (gd - m1 - xhat * m2)


def _acc_store(ref, val, first):
    @pl.when(first)
    def _():
        ref[...] = val

    @pl.when(jnp.logical_not(first))
    def _():
        ref[...] += val


def _matmul(a, b, kind, *, name, tm=512, tn=1024, tk=2048, out_defs=(("mn", F32),), epilogue=None, extras=(), sched=None):
    if kind == "nn":
        (M, K), (K2, N) = a.shape, b.shape
    elif kind == "nt":
        (M, K), (N, K2) = a.shape, b.shape
    else:
        (K, M), (K2, N) = a.shape, b.shape
    assert K == K2, (a.shape, b.shape, kind)
    tm, tn, tk = _tile(M, tm), _tile(N, tn), _tile(K, tk)
    nk = K // tk
    blocks = []

    def spec(shape, imap, dtype):
        blocks.append((shape, dtype))
        return pl.BlockSpec(shape, imap)

    if kind == "tn":
        a_spec = spec((tk, tm), lambda j, i, k: (k, i), a.dtype)
    else:
        a_spec = spec((tm, tk), lambda j, i, k: (i, k), a.dtype)
    if kind == "nt":
        b_spec = spec((tn, tk), lambda j, i, k: (j, k), b.dtype)
    else:
        b_spec = spec((tk, tn), lambda j, i, k: (k, j), b.dtype)
    in_specs = [a_spec, b_spec]
    operands = [a, b]
    for arr, ekind, off in extras:
        if ekind == "mn":
            assert off % tn == 0
            in_specs.append(spec((tm, tn), lambda j, i, k, o=off // tn: (i, j + o), arr.dtype))
        elif ekind == "row":
            assert off % tn == 0
            in_specs.append(spec((1, tn), lambda j, i, k, o=off // tn: (0, j + o), arr.dtype))
        else:
            in_specs.append(spec((tm, 1), lambda j, i, k: (i, 0), arr.dtype))
        operands.append(arr)
    out_shape, out_specs = [], []
    for od in out_defs:
        okind, dt = od[0], od[1]
        if okind == "mn":
            out_shape.append(jax.ShapeDtypeStruct((M, N), dt))
            out_specs.append(spec((tm, tn), lambda j, i, k: (i, j), dt))
        elif okind == "cols":
            assert od[3] % tn == 0
            out_shape.append(jax.ShapeDtypeStruct((M, od[2]), dt))
            out_specs.append(spec((tm, tn), lambda j, i, k, o=od[3] // tn: (i, j + o), dt))
        elif okind == "m1":
            assert N == tn
            out_shape.append(jax.ShapeDtypeStruct((M, 1), dt))
            out_specs.append(spec((tm, 1), lambda j, i, k: (i, 0), dt))
        else:
            out_shape.append(jax.ShapeDtypeStruct((1, N), F32))
            out_specs.append(spec((1, tn), lambda j, i, k: (0, j), F32))
    n_ex, n_out = len(extras), len(out_defs)
    ep = epilogue if epilogue is not None else (lambda acc: (acc,))
    in_place = nk > 1 and epilogue is None and tuple(out_defs) == (("mn", F32),)

    def body(*refs):
        a_ref, b_ref = refs[0], refs[1]
        ex_refs = refs[2 : 2 + n_ex]
        out_refs = refs[2 + n_ex : 2 + n_ex + n_out]
        i = pl.program_id(1)
        k = pl.program_id(2)

        def product():
            return _dot(a_ref[...], b_ref[...], kind)

        def finish(acc):
            vals = ep(acc, *[r[...] for r in ex_refs])
            for od, r, v in zip(out_defs, out_refs, vals):
                if od[0] == "acc":
                    _acc_store(r, v, i == 0)
                else:
                    r[...] = v.astype(od[1])

        if nk == 1:
            finish(product())
            return
        acc_ref = out_refs[0] if in_place else refs[-1]

        @pl.when(k == 0)
        def _():
            acc_ref[...] = product()

        if in_place:
            @pl.when(k > 0)
            def _():
                acc_ref[...] += product()
        else:
            if nk > 2:
                @pl.when(jnp.logical_and(k > 0, k < nk - 1))
                def _():
                    acc_ref[...] += product()

            @pl.when(k == nk - 1)
            def _():
                finish(acc_ref[...] + product())

    scratch = [pltpu.VMEM((tm, tn), F32)] if nk > 1 and not in_place else []
    outs = _hosted(
        body,
        name=name,
        sched=sched,
        grid=(N // tn, M // tm, nk),
        in_specs=in_specs,
        out_specs=out_specs,
        out_shape=out_shape,
        scratch_shapes=scratch,
        compiler_params=_params(("arbitrary", "arbitrary", "arbitrary"), blocks, _nbytes((tm, tn), F32) if scratch else 0),
    )(*operands)
    return outs if len(outs) > 1 else outs[0]


def _ep_resid_ln(acc, resid, g, b):
    y, xhat, rstd = _ln_fwd(ALPHA * resid + acc, g, b)
    return y, y, xhat, rstd


def _ep_resid(acc, resid):
    return (ALPHA * resid + acc,)


def _ep_resid_ln_bwd(acc, resid, xhat, rstd, g):
    dy = ALPHA * resid + acc
    dz = _ln_bwd_vals(dy, xhat, rstd, g)
    return dz, dz, jnp.sum(dy * xhat, axis=0, keepdims=True), jnp.sum(dy, axis=0, keepdims=True)


def _ep_relu2(acc):
    r = jnp.maximum(acc, 0.0)
    return acc, r * r


def _ep_relu2_bwd(acc, h):
    return (acc * (2.0 * jnp.maximum(h, 0.0)),)


def _ep_gate_bwd(acc, ga, gb, ya, yb, bga, bgb):
    sa = jax.nn.sigmoid(ga + bga)
    sb = jax.nn.sigmoid(gb + bgb)
    dga = acc * ya * (sa * (1.0 - sa))
    dgb = acc * yb * (sb * (1.0 - sb))
    return (acc * sa, acc * sb, dga, dgb, jnp.sum(dga, axis=0, keepdims=True), jnp.sum(dgb, axis=0, keepdims=True))


def _ln_bwd(dy, xhat, rstd, g, *, name, sched=None, target=None):
    T, D = dy.shape
    tm = _tile(T, 256)
    head = target is not None

    def body(*refs):
        dy_ref, xh_ref, rs_ref, g_ref = refs[:4]
        dz_ref, dzb_ref, dg_ref, db_ref = refs[4 + head : 8 + head]
        first = pl.program_id(0) == 0
        dyv, xh = dy_ref[...], xh_ref[...]
        if head:
            err = dyv - refs[4][...]
            dyv = err * (1.0 / D)
            part = 0.5 * jnp.sum(jnp.sum(err * err, axis=1, keepdims=True) * (1.0 / D), axis=0, keepdims=True)
            _acc_store(refs[-1], part, first)
        dz = _ln_bwd_vals(dyv, xh, rs_ref[...], g_ref[...])
        dz_ref[...] = dz
        dzb_ref[...] = dz.astype(BF16)
        _acc_store(dg_ref, jnp.sum(dyv * xh, axis=0, keepdims=True), first)
        _acc_store(db_ref, jnp.sum(dyv, axis=0, keepdims=True), first)

    row = pl.BlockSpec((tm, D), lambda i: (i, 0))
    vec = pl.BlockSpec((1, D), lambda i: (0, 0))
    one = pl.BlockSpec((1, 1), lambda i: (0, 0))
    return _hosted(
        body,
        name=name,
        sched=sched,
        grid=(T // tm,),
        in_specs=[row, row, pl.BlockSpec((tm, 1), lambda i: (i, 0)), vec] + [row] * head,
        out_specs=[row, row, vec, vec] + [one] * head,
        out_shape=[
            jax.ShapeDtypeStruct((T, D), F32),
            jax.ShapeDtypeStruct((T, D), BF16),
            jax.ShapeDtypeStruct((1, D), F32),
            jax.ShapeDtypeStruct((1, D), F32),
        ] + [jax.ShapeDtypeStruct((1, 1), F32)] * head,
        compiler_params=_params(("arbitrary",), [((tm, D), F32)] * (4 + head)),
    )(*([dy, xhat, rstd, g] + [target] * head))


def _masked_mix_weights(wm_ref, G):
    r = lax.broadcasted_iota(jnp.int32, (CHUNK, CHUNK), 0)
    c = lax.broadcasted_iota(jnp.int32, (CHUNK, CHUNK), 1)
    return [jnp.where(c <= r, wm_ref[g], 0.0).astype(BF16) for g in range(G)]


def _gmlp_fwd(d, P, lnv_g, lnv_b, wm, bsT, *, name, sched=None):
    T, GW, G = d.T, d.GW, d.G
    tm = _tile(T, 256)
    nc = tm // CHUNK

    def body(u_ref, v_ref, g_ref, b_ref, wm_ref, bs_ref, o_ref):
        gu = _gelu(u_ref[...])
        vn, _, _ = _ln_fwd(_gelu(v_ref[...]), g_ref[...], b_ref[...])
        vn = vn.astype(BF16)
        wl = _masked_mix_weights(wm_ref, G)
        for c in range(nc):
            rows = slice(c * CHUNK, (c + 1) * CHUNK)
            for g in range(G):
                cols = slice(g * GROUP_DIM, (g + 1) * GROUP_DIM)
                mixed = _dot(wl[g], vn[rows, cols], "nn") + bs_ref[:, g : g + 1]
                o_ref[rows, cols] = (gu[rows, cols] * mixed).astype(BF16)

    return _hosted(
        body,
        name=name,
        sched=sched,
        grid=(T // tm,),
        in_specs=[
            pl.BlockSpec((tm, GW), lambda i: (i, 0)),
            pl.BlockSpec((tm, GW), lambda i: (i, 1)),
            pl.BlockSpec((1, GW), lambda i: (0, 0)),
            pl.BlockSpec((1, GW), lambda i: (0, 0)),
            pl.BlockSpec((G, CHUNK, CHUNK), lambda i: (0, 0, 0)),
            pl.BlockSpec((CHUNK, G), lambda i: (0, 0)),
        ],
        out_specs=pl.BlockSpec((tm, GW), lambda i: (i, 0)),
        out_shape=jax.ShapeDtypeStruct((T, GW), BF16),
        compiler_params=_params(("arbitrary",), [((tm, GW), F32)] * 3),
    )(P, P, lnv_g, lnv_b, wm, bsT)


def _gmlp_bwd(d, P, dya_b, Ga, lnv_g, lnv_b, wm, bsT, dP, *, name, sched=None):
    T, D, GW, G = d.T, d.D, d.GW, d.G
    tm = _tile(T, 256)
    nc = tm // CHUNK

    def body(u_ref, v_ref, dya_ref, ga_ref, g_ref, b_ref, wm_ref, bs_ref, dp_in, duv_ref, dwm_ref, dbs_ref, dlg_ref, dlb_ref, dgu_s, dvn_s):
        first = pl.program_id(0) == 0
        dsgu = _dot(dya_ref[...], ga_ref[...], "nn")
        u, v = u_ref[...], v_ref[...]
        gu = _gelu(u)
        lng = g_ref[...]
        vn, xh, rstd = _ln_fwd(_gelu(v), lng, b_ref[...])
        vn = vn.astype(BF16)
        wl = _masked_mix_weights(wm_ref, G)
        r = lax.broadcasted_iota(jnp.int32, (CHUNK, CHUNK), 0)
        cc = lax.broadcasted_iota(jnp.int32, (CHUNK, CHUNK), 1)
        lane_g = lax.broadcasted_iota(jnp.int32, (CHUNK, G), 1)
        dbs = jnp.zeros((CHUNK, G), F32)
        for g in range(G):
            cols = slice(g * GROUP_DIM, (g + 1) * GROUP_DIM)
            dw = jnp.zeros((CHUNK, CHUNK), F32)
            for c in range(nc):
                rows = slice(c * CHUNK, (c + 1) * CHUNK)
                mixed = _dot(wl[g], vn[rows, cols], "nn") + bs_ref[:, g : g + 1]
                ds = dsgu[rows, cols]
                dgu_s[rows, cols] = ds * mixed
                dmx = ds * gu[rows, cols]
                dbs = dbs + jnp.where(lane_g == g, jnp.sum(dmx, axis=1, keepdims=True), 0.0)
                dmx = dmx.astype(BF16)
                dw = dw + _dot(dmx, vn[rows, cols], "nt")
                dvn_s[rows, cols] = _dot(wl[g], dmx, "tn")
            _acc_store(dwm_ref.at[g], jnp.where(cc <= r, dw, 0.0), first)
        _acc_store(dbs_ref, dbs, first)
        dvn = dvn_s[...]
        _acc_store(dlg_ref, jnp.sum(dvn * xh, axis=0, keepdims=True), first)
        _acc_store(dlb_ref, jnp.sum(dvn, axis=0, keepdims=True), first)
        dgv = _ln_bwd_vals(dvn, xh, rstd, lng)
        duv_ref[:, :GW] = (dgu_s[...] * _gelu_grad(u)).astype(BF16)
        duv_ref[:, GW:] = (dgv * _gelu_grad(v)).astype(BF16)

    return _hosted(
        body,
        name=name,
        sched=sched,
        grid=(T // tm,),
        in_specs=[
            pl.BlockSpec((tm, GW), lambda i: (i, 0)),
            pl.BlockSpec((tm, GW), lambda i: (i, 1)),
            pl.BlockSpec((tm, D), lambda i: (i, 0)),
            pl.BlockSpec((D, GW), lambda i: (0, 0)),
            pl.BlockSpec((1, GW), lambda i: (0, 0)),
            pl.BlockSpec((1, GW), lambda i: (0, 0)),
            pl.BlockSpec((G, CHUNK, CHUNK), lambda i: (0, 0, 0)),
            pl.BlockSpec((CHUNK, G), lambda i: (0, 0)),
            pl.BlockSpec(memory_space=pl.ANY),
        ],
        out_specs=[
            pl.BlockSpec((tm, 2 * GW), lambda i: (i, 0)),
            pl.BlockSpec((G, CHUNK, CHUNK), lambda i: (0, 0, 0)),
            pl.BlockSpec((CHUNK, G), lambda i: (0, 0)),
            pl.BlockSpec((1, GW), lambda i: (0, 0)),
            pl.BlockSpec((1, GW), lambda i: (0, 0)),
        ],
        out_shape=[
            jax.ShapeDtypeStruct(dP.shape, BF16),
            jax.ShapeDtypeStruct((G, CHUNK, CHUNK), F32),
            jax.ShapeDtypeStruct((CHUNK, G), F32),
            jax.ShapeDtypeStruct((1, GW), F32),
            jax.ShapeDtypeStruct((1, GW), F32),
        ],
        scratch_shapes=[pltpu.VMEM((tm, GW), F32), pltpu.VMEM((tm, GW), F32)],
        input_output_aliases={8: 0},
        compiler_params=_params(
            ("arbitrary",), [((tm, GW), F32)] * 3 + [((tm, D), BF16), ((D, GW), BF16)], 2 * _nbytes((tm, GW), F32)
        ),
    )(P, P, dya_b, Ga, lnv_g, lnv_b, wm, bsT, dP)


def _swap_halves(x):
    w = x.shape[1]
    lane = lax.broadcasted_iota(jnp.int32, x.shape, 1)
    return jnp.where((lane % HEAD_DIM) < ROPE_HALF, pltpu.roll(x, w - ROPE_HALF, 1), pltpu.roll(x, ROPE_HALF, 1))


def _lane_tile(t, width):
    reps = width // LANES
    return t if reps == 1 else jnp.tile(t, (1, reps))


def _rope(x, cos2, sin2):
    w = x.shape[1]
    return x * _lane_tile(cos2, w) + _swap_halves(x) * _lane_tile(sin2, w)


def _rope_bwd(g, cos2, sin2):
    w = g.shape[1]
    return g * _lane_tile(cos2, w) - _swap_halves(g) * _lane_tile(sin2, w)


PAIR = LANES // HEAD_DIM


def _swa_valid(i):
    s = lax.broadcasted_iota(jnp.int32, (2 * CHUNK, CHUNK), 0)
    t = lax.broadcasted_iota(jnp.int32, (2 * CHUNK, CHUNK), 1)
    cur = jnp.logical_and(s >= CHUNK, s - CHUNK <= t)
    prev = jnp.logical_and(jnp.logical_and(s < CHUNK, s > t), i > 0)
    return jnp.logical_or(cur, prev)


def _half_variants(x, odd):
    lane = lax.broadcasted_iota(jnp.int32, x.shape, 1)
    if odd:
        hi = jnp.where(lane >= HEAD_DIM, x, jnp.zeros_like(x))
        return pltpu.roll(hi, HEAD_DIM, 1), hi
    lo = jnp.where(lane < HEAD_DIM, x, jnp.zeros_like(x))
    return lo, pltpu.roll(lo, HEAD_DIM, 1)


def _swa_probs_t(kx, qp, sink, valid):
    s = jnp.where(valid, _dot(kx, qp, "nt") * (HEAD_DIM**-0.5), -jnp.inf)
    m = jnp.maximum(jnp.max(s, axis=0, keepdims=True), sink)
    e = jnp.exp(s - m)
    e_s = jnp.exp(sink - m)
    den = jnp.sum(e, axis=0, keepdims=True) + e_s
    return e / den, e_s / den


def _swa_load(q_ref, kc_ref, kp_ref, vc_ref, vp_ref, cc, sc, cp, sp):
    qr = _rope(q_ref[...], cc, sc).astype(BF16)
    kcat = jnp.concatenate([_rope(kp_ref[...], cp, sp), _rope(kc_ref[...], cc, sc)], axis=0)
    vcat = jnp.concatenate([vp_ref[...], vc_ref[...]], axis=0)
    return qr, kcat, vcat


def _swa_specs(d):
    _, off_q, off_k, off_va, _, _ = _offsets(d)
    assert off_q % d.AW == 0 and off_k % d.KVW == 0 and off_va % d.KVW == 0
    prev = lambda i: jnp.maximum(i - 1, 0)
    return [
        pl.BlockSpec(memory_space=pltpu.SMEM),
        pl.BlockSpec((CHUNK, d.AW), lambda i, o=off_q // d.AW: (i, o)),
        pl.BlockSpec((CHUNK, d.KVW), lambda i, o=off_k // d.KVW: (i, o)),
        pl.BlockSpec((CHUNK, d.KVW), lambda i, o=off_k // d.KVW: (prev(i), o)),
        pl.BlockSpec((CHUNK, d.KVW), lambda i, o=off_va // d.KVW: (i, o)),
        pl.BlockSpec((CHUNK, d.KVW), lambda i, o=off_va // d.KVW: (prev(i), o)),
        pl.BlockSpec((CHUNK, LANES), lambda i: (i, 0)),
        pl.BlockSpec((CHUNK, LANES), lambda i: (i, 0)),
        pl.BlockSpec((CHUNK, LANES), lambda i: (prev(i), 0)),
        pl.BlockSpec((CHUNK, LANES), lambda i: (prev(i), 0)),
    ]


def _swa_fwd(d, P, cos2, sin2, sinks, *, name, sched=None):
    T, AW, HQ, HKV = d.T, d.AW, d.HQ, d.HKV
    grp = HQ // HKV
    assert grp % PAIR == 0 and HKV % PAIR == 0

    def body(sink_ref, q_ref, kc_ref, kp_ref, vc_ref, vp_ref, cc_ref, sc_ref, cp_ref, sp_ref, o_ref):
        i = pl.program_id(0)
        qr, kcat, vcat = _swa_load(q_ref, kc_ref, kp_ref, vc_ref, vp_ref, cc_ref[...], sc_ref[...], cp_ref[...], sp_ref[...])
        valid = _swa_valid(i)
        for kv in range(HKV):
            col = slice((kv // PAIR) * LANES, (kv // PAIR + 1) * LANES)
            kx = [t.astype(BF16) for t in _half_variants(kcat[:, col], kv % PAIR)]
            vx = [t.astype(BF16) for t in _half_variants(vcat[:, col], kv % PAIR)]
            for pp in range(grp // PAIR):
                p = kv * (grp // PAIR) + pp
                qs = slice(p * LANES, (p + 1) * LANES)
                out = jnp.zeros((CHUNK, LANES), F32)
                for half in range(PAIR):
                    pt, _ = _swa_probs_t(kx[half], qr[:, qs], sink_ref[PAIR * p + half], valid)
                    out = out + _dot(pt.astype(BF16), vx[half], "tn")
                o_ref[:, qs] = out.astype(BF16)

    return _hosted(
        body,
        name=name,
        sched=sched,
        grid=(T // CHUNK,),
        in_specs=_swa_specs(d),
        out_specs=pl.BlockSpec((CHUNK, AW), lambda i: (i, 0)),
        out_shape=jax.ShapeDtypeStruct((T, AW), BF16),
        compiler_params=_params(("arbitrary",), [((CHUNK, AW), F32)] * 3),
    )(sinks, P, P, P, P, P, cos2, sin2, cos2, sin2)


def _swa_bwd(d, P, dyb_b, Gb, cos2, sin2, sinks, dP, *, name, sched=None):
    T, D, AW, KVW, HQ, HKV = d.T, d.D, d.AW, d.KVW, d.HQ, d.HKV
    grp = HQ // HKV
    nb = T // CHUNK
    scale = HEAD_DIM**-0.5
    off_q = _offsets(d)[1]
    assert grp % PAIR == 0 and HKV % PAIR == 0 and off_q % AW == 0

    def body(sink_ref, q_ref, kc_ref, kp_ref, vc_ref, vp_ref, cc_ref, sc_ref, cp_ref, sp_ref, dyb_ref, gb_ref, dp_in,
             dq_ref, dkv_ref, dsk_ref, acc_s, dq_s, dk_s, dv_s):
        i = pl.program_id(0)
        cc, sc, cp, sp = cc_ref[...], sc_ref[...], cp_ref[...], sp_ref[...]
        datt = _dot(dyb_ref[...], gb_ref[...], "nn").astype(BF16)
        qr, kcat, vcat = _swa_load(q_ref, kc_ref, kp_ref, vc_ref, vp_ref, cc, sc, cp, sp)
        valid = _swa_valid(i)
        lane1 = lax.broadcasted_iota(jnp.int32, (1, LANES), 1)
        lane2 = lax.broadcasted_iota(jnp.int32, (2 * CHUNK, LANES), 1)
        dsk = jnp.zeros((1, LANES), F32)
        for kvp in range(HKV // PAIR):
            col = slice(kvp * LANES, (kvp + 1) * LANES)
            dk_col = jnp.zeros((2 * CHUNK, LANES), F32)
            dv_col = jnp.zeros((2 * CHUNK, LANES), F32)
            for odd in range(PAIR):
                kv = kvp * PAIR + odd
                kx = [t.astype(BF16) for t in _half_variants(kcat[:, col], odd)]
                vx = [t.astype(BF16) for t in _half_variants(vcat[:, col], odd)]
                dk_half = [jnp.zeros((2 * CHUNK, LANES), F32) for _ in range(PAIR)]
                dv_half = [jnp.zeros((2 * CHUNK, LANES), F32) for _ in range(PAIR)]
                for pp in range(grp // PAIR):
                    p = kv * (grp // PAIR) + pp
                    qs = slice(p * LANES, (p + 1) * LANES)
                    qp, dop = qr[:, qs], datt[:, qs]
                    dq = jnp.zeros((CHUNK, LANES), F32)
                    for half in range(PAIR):
                        h = PAIR * p + half
                        pt, p_s = _swa_probs_t(kx[half], qp, sink_ref[h], valid)
                        dpt = _dot(vx[half], dop, "nt")
                        rs = jnp.sum(pt * dpt, axis=0, keepdims=True)
                        dst = (pt * (dpt - rs) * scale).astype(BF16)
                        dsk = dsk + jnp.where(lane1 == h, -jnp.sum(p_s * rs, axis=1, keepdims=True), 0.0)
                        dq = dq + _dot(dst, kx[half], "tn")
                        dk_half[half] = dk_half[half] + _dot(dst, qp, "nn")
                        dv_half[half] = dv_half[half] + _dot(pt.astype(BF16), dop, "nn")
                    dq_s[:, qs] = dq
                for acc_half, is_k in ((dk_half, True), (dv_half, False)):
                    lo = jnp.where(lane2 < HEAD_DIM, acc_half[0], 0.0)
                    hi = jnp.where(lane2 >= HEAD_DIM, acc_half[1], 0.0)
                    both = (pltpu.roll(lo, HEAD_DIM, 1) + hi) if odd else (lo + pltpu.roll(hi, HEAD_DIM, 1))
                    if is_k:
                        dk_col = dk_col + both
                    else:
                        dv_col = dv_col + both
            dk_s[:, col] = dk_col
            dv_s[:, col] = dv_col
        dq_ref[...] = _rope_bwd(dq_s[...], cc, sc).astype(BF16)
        cur = pl.ds(pl.multiple_of(i * CHUNK, CHUNK), CHUNK)
        acc_s[cur, :KVW] = _rope_bwd(dk_s[CHUNK:, :], cc, sc)
        acc_s[cur, KVW:] = dv_s[CHUNK:, :]

        @pl.when(i > 0)
        def _():
            prv = pl.ds(pl.multiple_of((i - 1) * CHUNK, CHUNK), CHUNK)
            acc_s[prv, :KVW] += _rope_bwd(dk_s[:CHUNK, :], cp, sp)
            acc_s[prv, KVW:] += dv_s[:CHUNK, :]

        _acc_store(dsk_ref, dsk, i == 0)

        @pl.when(i == nb - 1)
        def _():
            dkv_ref[...] = acc_s[...].astype(BF16)

    return _hosted(
        body,
        name=name,
        sched=sched,
        grid=(nb,),
        in_specs=_swa_specs(d) + [pl.BlockSpec((CHUNK, D), lambda i: (i, 0)), pl.BlockSpec((D, AW), lambda i: (0, 0)),
                                   pl.BlockSpec(memory_space=pl.ANY)],
        out_specs=[
            pl.BlockSpec((CHUNK, AW), lambda i, o=off_q // AW: (i, o)),
            pl.BlockSpec((T, 2 * KVW), lambda i: (0, 0)),
            pl.BlockSpec((1, LANES), lambda i: (0, 0)),
        ],
        out_shape=[
            jax.ShapeDtypeStruct(dP.shape, BF16),
            jax.ShapeDtypeStruct((T, 2 * KVW), BF16),
            jax.ShapeDtypeStruct((1, LANES), F32),
        ],
        input_output_aliases={12: 0},
        scratch_shapes=[
            pltpu.VMEM((T, 2 * KVW), F32),
            pltpu.VMEM((CHUNK, AW), F32),
            pltpu.VMEM((2 * CHUNK, KVW), F32),
            pltpu.VMEM((2 * CHUNK, KVW), F32),
        ],
        compiler_params=_params(
            ("arbitrary",), [((CHUNK, AW), F32)] * 3 + [((D, AW), BF16), ((T, 2 * KVW), BF16)], _nbytes((T, 2 * KVW), F32)
        ),
    )(sinks, P, P, P, P, P, cos2, sin2, cos2, sin2, dyb_b, Gb, dP)


def _place_cols(dst, src, off, *, name, sched=None):
    T, w = src.shape
    tm, tw = _tile(T, 512), _tile(w, 512)
    assert off % tw == 0

    def body(src_ref, dst_in, out_ref):
        out_ref[...] = src_ref[...]

    return _hosted(
        body,
        name=name,
        sched=sched,
        grid=(T // tm, w // tw),
        in_specs=[pl.BlockSpec((tm, tw), lambda i, j: (i, j)), pl.BlockSpec(memory_space=pl.ANY)],
        out_specs=pl.BlockSpec((tm, tw), lambda i, j, o=off // tw: (i, j + o)),
        out_shape=jax.ShapeDtypeStruct(dst.shape, dst.dtype),
        input_output_aliases={1: 0},
        compiler_params=_params(("arbitrary", "arbitrary"), [((tm, tw), dst.dtype)] * 2),
    )(src, dst)


def _branch_merge(d, sgu, att, Ga, Gb, P, b_gate, *, name, sched=None):
    T, D, GW, AW = d.T, d.D, d.GW, d.AW
    _, _, _, _, off_ga, off_gb = _offsets(d)
    tm, tn = _tile(T, 1024), _tile(D, 512)
    assert off_ga % tn == 0 and off_gb % tn == 0

    def body(s_ref, a_ref, ga_w, gb_w, ga_ref, gb_ref, bga_ref, bgb_ref, ya_ref, yb_ref, mg_ref):
        ya = _dot(s_ref[...], ga_w[...], "nt")
        yb = _dot(a_ref[...], gb_w[...], "nt")
        ya_ref[...] = ya
        yb_ref[...] = yb
        sa = jax.nn.sigmoid(ga_ref[...] + bga_ref[...])
        sb = jax.nn.sigmoid(gb_ref[...] + bgb_ref[...])
        mg_ref[...] = (sa * ya + sb * yb).astype(BF16)

    tile = pl.BlockSpec((tm, tn), lambda j, i: (i, j))
    return _hosted(
        body,
        name=name,
        sched=sched,
        grid=(D // tn, T // tm),
        in_specs=[
            pl.BlockSpec((tm, GW), lambda j, i: (i, 0)),
            pl.BlockSpec((tm, AW), lambda j, i: (i, 0)),
            pl.BlockSpec((tn, GW), lambda j, i: (j, 0)),
            pl.BlockSpec((tn, AW), lambda j, i: (j, 0)),
            pl.BlockSpec((tm, tn), lambda j, i, o=off_ga // tn: (i, j + o)),
            pl.BlockSpec((tm, tn), lambda j, i, o=off_gb // tn: (i, j + o)),
            pl.BlockSpec((1, tn), lambda j, i: (0, j)),
            pl.BlockSpec((1, tn), lambda j, i, o=D // tn: (0, j + o)),
        ],
        out_specs=[tile, tile, tile],
        out_shape=[jax.ShapeDtypeStruct((T, D), F32), jax.ShapeDtypeStruct((T, D), F32), jax.ShapeDtypeStruct((T, D), BF16)],
        compiler_params=_params(
            ("arbitrary", "arbitrary"),
            [((tm, GW), BF16), ((tm, AW), BF16), ((tn, GW), BF16), ((tn, AW), BF16)] + [((tm, tn), F32)] * 5,
        ),
    )(sgu, att, Ga, Gb, P, P, b_gate, b_gate)


def _xattn_probs(qh, kh):
    s = _dot(qh, kh, "nt") * (X_HEAD_DIM**-0.5)
    e = jnp.exp(s - jnp.max(s, axis=1, keepdims=True))
    return e / jnp.sum(e, axis=1, keepdims=True)


def _xattn_fwd(d, X, Xb, kv, Gxq, Gxo, ln_g, ln_b, *, name, sched=None):
    T, D, XW, XH, MEM = d.T, d.D, d.XW, d.XH, d.MEM
    tm = _tile(T, 256)

    def body(x_ref, xb_ref, kv_ref, wq_ref, wo_ref, g_ref, b_ref, y_ref, yb_ref, xh_ref, rs_ref, q_ref, o_ref, o_s):
        q = _dot(xb_ref[...], wq_ref[...], "nn").astype(BF16)
        q_ref[...] = q
        for h in range(XH):
            hs = slice(h * X_HEAD_DIM, (h + 1) * X_HEAD_DIM)
            p = _xattn_probs(q[:, hs], kv_ref[:, hs]).astype(BF16)
            o_s[:, hs] = _dot(p, kv_ref[:, XW + h * X_HEAD_DIM : XW + (h + 1) * X_HEAD_DIM], "nn").astype(BF16)
        o = o_s[...]
        o_ref[...] = o
        y, xhat, rstd = _ln_fwd(ALPHA * x_ref[...] + _dot(o, wo_ref[...], "nt"), g_ref[...], b_ref[...])
        y_ref[...] = y
        yb_ref[...] = y.astype(BF16)
        xh_ref[...] = xhat
        rs_ref[...] = rstd

    row = pl.BlockSpec((tm, D), lambda i: (i, 0))
    nar = pl.BlockSpec((tm, XW), lambda i: (i, 0))
    vec = pl.BlockSpec((1, D), lambda i: (0, 0))
    return _hosted(
        body,
        name=name,
        sched=sched,
        grid=(T // tm,),
        in_specs=[
            row,
            row,
            pl.BlockSpec((MEM, 2 * XW), lambda i: (0, 0)),
            pl.BlockSpec((D, XW), lambda i: (0, 0)),
            pl.BlockSpec((D, XW), lambda i: (0, 0)),
            vec,
            vec,
        ],
        out_specs=[row, row, row, pl.BlockSpec((tm, 1), lambda i: (i, 0)), nar, nar],
        out_shape=[
            jax.ShapeDtypeStruct((T, D), F32),
            jax.ShapeDtypeStruct((T, D), BF16),
            jax.ShapeDtypeStruct((T, D), F32),
            jax.ShapeDtypeStruct((T, 1), F32),
            jax.ShapeDtypeStruct((T, XW), BF16),
            jax.ShapeDtypeStruct((T, XW), BF16),
        ],
        scratch_shapes=[pltpu.VMEM((tm, XW), BF16)],
        compiler_params=_params(("arbitrary",), [((tm, D), F32)] * 4 + [((D, XW), BF16)] * 2),
    )(X, Xb, kv, Gxq, Gxo, ln_g, ln_b)


def _xattn_bwd(d, dz, dzb, q_b, kv, Gxq, Gxo, xhat, rstd, ln_g, *, name, sched=None):
    T, D, XW, XH, MEM = d.T, d.D, d.XW, d.XH, d.MEM
    tm = _tile(T, 256)
    scale = X_HEAD_DIM**-0.5

    def body(dz_ref, dzb_ref, q_ref, kv_ref, wq_ref, wo_ref, xh_ref, rs_ref, g_ref, dx_ref, dxb_ref, dg_ref, db_ref, dq_ref, dkv_ref, dq_s):
        first = pl.program_id(0) == 0
        do = _dot(dzb_ref[...], wo_ref[...], "nn").astype(BF16)
        for h in range(XH):
            hs = slice(h * X_HEAD_DIM, (h + 1) * X_HEAD_DIM)
            vs = slice(XW + h * X_HEAD_DIM, XW + (h + 1) * X_HEAD_DIM)
            kh, vh, qh, doh = kv_ref[:, hs], kv_ref[:, vs], q_ref[:, hs], do[:, hs]
            p = _xattn_probs(qh, kh)
            dp = _dot(doh, vh, "nt")
            ds = (p * (dp - jnp.sum(p * dp, axis=1, keepdims=True)) * scale).astype(BF16)
            dq_s[:, hs] = _dot(ds, kh, "nn").astype(BF16)
            _acc_store(dkv_ref.at[h], _dot(ds, qh, "tn"), first)
            _acc_store(dkv_ref.at[XH + h], _dot(p.astype(BF16), doh, "tn"), first)
        dq = dq_s[...]
        dq_ref[...] = dq
        dy = ALPHA * dz_ref[...] + _dot(dq, wq_ref[...], "nt")
        xh = xh_ref[...]
        dz_in = _ln_bwd_vals(dy, xh, rs_ref[...], g_ref[...])
        dx_ref[...] = dz_in
        dxb_ref[...] = dz_in.astype(BF16)
        _acc_store(dg_ref, jnp.sum(dy * xh, axis=0, keepdims=True), first)
        _acc_store(db_ref, jnp.sum(dy, axis=0, keepdims=True), first)

    row = pl.BlockSpec((tm, D), lambda i: (i, 0))
    nar = pl.BlockSpec((tm, XW), lambda i: (i, 0))
    vec = pl.BlockSpec((1, D), lambda i: (0, 0))
    return _hosted(
        body,
        name=name,
        sched=sched,
        grid=(T // tm,),
        in_specs=[
            row,
            row,
            nar,
            pl.BlockSpec((MEM, 2 * XW), lambda i: (0, 0)),
            pl.BlockSpec((D, XW), lambda i: (0, 0)),
            pl.BlockSpec((D, XW), lambda i: (0, 0)),
            row,
            pl.BlockSpec((tm, 1), lambda i: (i, 0)),
            vec,
        ],
        out_specs=[row, row, vec, vec, nar, pl.BlockSpec((2 * XH, MEM, X_HEAD_DIM), lambda i: (0, 0, 0))],
        out_shape=[
            jax.ShapeDtypeStruct((T, D), F32),
            jax.ShapeDtypeStruct((T, D), BF16),
            jax.ShapeDtypeStruct((1, D), F32),
            jax.ShapeDtypeStruct((1, D), F32),
            jax.ShapeDtypeStruct((T, XW), BF16),
            jax.ShapeDtypeStruct((2 * XH, MEM, X_HEAD_DIM), F32),
        ],
        scratch_shapes=[pltpu.VMEM((tm, XW), BF16)],
        compiler_params=_params(("arbitrary",), [((tm, D), F32)] * 5 + [((D, XW), BF16)] * 2),
    )(dz, dzb, q_b, kv, Gxq, Gxo, xhat, rstd, ln_g)


def _resid_ln_outs():
    return (("mn", F32), ("mn", BF16), ("mn", F32), ("m1", F32))


def _layer_fwd(d, X, Xb, memb, G, S, cos2, sin2, tag, sched=None):
    D = d.D
    mm = functools.partial(_matmul, sched=sched)
    P = mm(Xb, G["in"], "nt", name=f"proj_{tag}", tm=1024, tn=_tile(d.IN, 1280))
    sgu = _gmlp_fwd(d, P, S["ln_v_g"], S["ln_v_b"], S["w_s"], S["b_sT"], name=f"gmlp_fwd_{tag}", sched=sched)
    att = _swa_fwd(d, P, cos2, sin2, S["sinks"], name=f"swa_fwd_{tag}", sched=sched)
    ya, yb, merged = _branch_merge(d, sgu, att, G["a"], G["b"], P, S["b_gate"], name=f"merge_{tag}", sched=sched)
    X1, X1b, xh1, rs1 = mm(
        merged, G["o"], "nn", name=f"oproj_ln_{tag}", tn=D, tk=2048, out_defs=_resid_ln_outs(), epilogue=_ep_resid_ln,
        extras=((X, "mn", 0), (S["ln1_g"], "row", 0), (S["ln1_b"], "row", 0)),
    )
    kv = mm(memb, G["xkv"], "nn", name=f"xkv_{tag}", out_defs=(("mn", BF16),))
    X2, X2b, xh2, rs2, q_b, o_b = _xattn_fwd(d, X1, X1b, kv, G["xq"], G["xo"], S["ln2_g"], S["ln2_b"], name=f"xattn_fwd_{tag}", sched=sched)
    H, A = mm(X2b, G["up"], "nt", name=f"up_{tag}", tm=1024, out_defs=(("mn", F32), ("mn", BF16)), epilogue=_ep_relu2)
    X3, X3b, xh3, rs3 = mm(
        A, G["down"], "nn", name=f"down_ln_{tag}", tn=D, tk=2048, out_defs=_resid_ln_outs(), epilogue=_ep_resid_ln,
        extras=((X2, "mn", 0), (S["ln3_g"], "row", 0), (S["ln3_b"], "row", 0)),
    )
    saved = dict(Xb=Xb, P=P, sgu=sgu, att=att, ya=ya, yb=yb, merged=merged, X1b=X1b, xh1=xh1, rs1=rs1, kv=kv, q_b=q_b,
                 o_b=o_b, X2b=X2b, xh2=xh2, rs2=rs2, H=H, A=A, xh3=xh3, rs3=rs3)
    return X3, X3b, saved


def _layer_bwd(d, dXout, sv, memb, G, S, cos2, sin2, tag, sched=None, on_grad=None, target=None):
    D = d.D
    mm = functools.partial(_matmul, sched=sched)
    emit = on_grad if on_grad is not None else (lambda n, g: None)
    wide = dict(tn=D)
    _, _, off_k, _, off_ga, off_gb = _offsets(d)
    bf = (("mn", BF16),)
    dz3, dz3b, dg3, db3, *loss = _ln_bwd(dXout, sv["xh3"], sv["rs3"], S["ln3_g"], name=f"ln3_bwd_{tag}", sched=sched, target=target)
    dHb = mm(dz3b, G["down"], "nt", name=f"down_dx_{tag}", tm=1024, out_defs=bf, epilogue=_ep_relu2_bwd, extras=((sv["H"], "mn", 0),))
    g_down = mm(sv["A"], dz3b, "tn", name=f"down_dw_{tag}", **wide)
    emit("down", g_down)
    ln_bwd_outs = (("mn", F32), ("mn", BF16), ("acc", F32), ("acc", F32))
    dz2, dz2b, dg2, db2 = mm(
        dHb, G["up"], "nn", name=f"up_dx_{tag}", tn=D, tk=2048, out_defs=ln_bwd_outs, epilogue=_ep_resid_ln_bwd,
        extras=((dz3, "mn", 0), (sv["xh2"], "mn", 0), (sv["rs2"], "m1", 0), (S["ln2_g"], "row", 0)),
    )
    g_up = mm(dHb, sv["X2b"], "tn", name=f"up_dw_{tag}", **wide)
    emit("up", g_up)
    dz1, dz1b, dg1, db1, dq_b, dkv = _xattn_bwd(d, dz2, dz2b, sv["q_b"], sv["kv"], G["xq"], G["xo"], sv["xh1"], sv["rs1"], S["ln1_g"],
                                               name=f"xattn_bwd_{tag}", sched=sched)
    g_xo = mm(dz2b, sv["o_b"], "tn", name=f"xo_dw_{tag}")
    emit("xo", g_xo)
    g_xq = mm(sv["X1b"], dq_b, "tn", name=f"xq_dw_{tag}")
    emit("xq", g_xq)
    dkv_b = dkv.transpose(1, 0, 2).reshape(d.MEM, 2 * d.XW).astype(BF16)
    g_xkv = mm(memb, dkv_b, "tn", name=f"xkv_dw_{tag}")
    emit("xkv", g_xkv)
    dya_b, dyb_b, dP, dgb_b, dbga, dbgb = mm(
        dz1b, G["o"], "nt", name=f"oproj_dx_{tag}", tm=1024, tn=512,
        out_defs=(("mn", BF16), ("mn", BF16), ("cols", BF16, d.IN, off_ga), ("mn", BF16), ("acc", F32), ("acc", F32)),
        epilogue=_ep_gate_bwd,
        extras=((sv["P"], "mn", off_ga), (sv["P"], "mn", off_gb), (sv["ya"], "mn", 0), (sv["yb"], "mn", 0),
                (S["b_gate"], "row", 0), (S["b_gate"], "row", D)),
    )
    g_o = mm(sv["merged"], dz1b, "tn", name=f"oproj_dw_{tag}", **wide)
    emit("o", g_o)
    dP = _place_cols(dP, dgb_b, off_gb, name=f"place_dgb_{tag}")
    dP, dwm, dbsT, dlvg, dlvb = _gmlp_bwd(d, sv["P"], dya_b, G["a"], S["ln_v_g"], S["ln_v_b"], S["w_s"], S["b_sT"], dP, name=f"gmlp_bwd_{tag}", sched=sched)
    g_a = mm(dya_b, sv["sgu"], "tn", name=f"bra_dw_{tag}")
    emit("a", g_a)
    dP, dkvr_b, dsk = _swa_bwd(d, sv["P"], dyb_b, G["b"], cos2, sin2, S["sinks"], dP, name=f"swa_bwd_{tag}", sched=sched)
    g_b = mm(dyb_b, sv["att"], "tn", name=f"brb_dw_{tag}")
    emit("b", g_b)
    dP = _place_cols(dP, dkvr_b, off_k, name=f"place_dkv_{tag}")
    small = dict(b_gate=jnp.concatenate([dbga, dbgb], axis=1), ln_v_g=dlvg, ln_v_b=dlvb, w_s=dwm, b_sT=dbsT, sinks=dsk,
                 ln1_g=dg1, ln1_b=db1, ln2_g=dg2, ln2_b=db2, ln3_g=dg3, ln3_b=db3)
    emit("small", small)
    g_in = mm(dP, sv["Xb"], "tn", name=f"proj_dw_{tag}", **wide)
    emit("in", g_in)
    dXin = mm(dP, G["in"], "nn", name=f"proj_dx_{tag}", tn=D, tk=_tile(d.IN, 2560), epilogue=_ep_resid, extras=((dz1, "mn", 0),))
    big = {"in": g_in, "a": g_a, "b": g_b, "o": g_o, "xq": g_xq, "xkv": g_xkv, "xo": g_xo, "up": g_up, "down": g_down}
    return dXin, big, small, (loss[0] if loss else None)


def _rope_tables(T):
    inv = 1.0 / (ROPE_THETA ** (jnp.arange(0, HEAD_DIM, 2, dtype=F32) / HEAD_DIM))
    ang = jnp.arange(T, dtype=F32)[:, None] * inv[None, :]
    cos, sin = jnp.cos(ang), jnp.sin(ang)
    reps = LANES // HEAD_DIM
    return jnp.concatenate([cos, cos] * reps, axis=1), jnp.concatenate([-sin, sin] * reps, axis=1)


def _local_step(d, x, mem, target, Gs, Ss, sched=None, on_grad=None):
    cos2, sin2 = _rope_tables(d.T)
    memb = mem.astype(BF16)
    X, Xb = x, x.astype(BF16)
    saved = []
    for l in range(DEPTH):
        X, Xb, sv = _layer_fwd(d, X, Xb, memb, Gs[l], Ss[l], cos2, sin2, f"l{l}", sched)
        saved.append(sv)
    bigs, smalls = [None] * DEPTH, [None] * DEPTH
    dX, loss = X, None
    for l in reversed(range(DEPTH)):
        emit = None if on_grad is None else functools.partial(on_grad, l)
        head = target if l == DEPTH - 1 else None
        dX, bigs[l], smalls[l], got = _layer_bwd(d, dX, saved[l], memb, Gs[l], Ss[l], cos2, sin2, f"l{l}", sched, emit, head)
        loss = got if got is not None else loss
    return loss, dX, bigs, smalls


def _place():
    x, y, c = lax.axis_index("x"), lax.axis_index("y"), lax.axis_index("c")
    return x, y, c, [(1 - x, y), (x, 1 - y), (1 - x, 1 - y)]


def _all_gather(shards, *, name, sched=None):
    n = len(shards)
    any_spec = pl.BlockSpec(memory_space=pl.ANY)

    def body(*refs):
        ins, outs = refs[:n], refs[n : 2 * n]
        send_sems, recv_sems, local_sems = refs[2 * n :]
        x, y, c, chips = _place()
        me, sibling = (x, y, c), (x, y, 1 - c)

        def rows(w, p):
            r = ins[w].shape[0]
            return outs[w].at[pl.ds((4 * p[0] + 2 * p[1] + p[2]) * r, r), :]

        def copy(w, k, block, to, src=None):
            return pltpu.make_async_remote_copy(
                src_ref=rows(w, block) if src is None else src, dst_ref=rows(w, block),
                send_sem=send_sems.at[7 * w + k], recv_sem=recv_sems.at[7 * w + k], device_id=to, device_id_type=MESH)

        mine = [pltpu.make_async_copy(ins[w], rows(w, me), local_sems.at[w]) for w in range(n)]
        for cp in mine:
            cp.start()
        first = []
        for w in range(n):
            first.append(copy(w, 0, me, sibling, src=ins[w]))
            first += [copy(w, 1 + j, me, (*chip, c), src=ins[w]) for j, chip in enumerate(chips)]
        for cp in first:
            cp.start()
        passed = []
        for j, chip in enumerate(chips):
            for w in range(n):
                copy(w, 1 + j, (*chip, c), me).wait_recv()
                fwd = copy(w, 4 + j, (*chip, c), sibling)
                fwd.start()
                passed.append(fwd)
        for w in range(n):
            copy(w, 0, sibling, me).wait_recv()
            for j, chip in enumerate(chips):
                copy(w, 4 + j, (*chip, 1 - c), me).wait_recv()
        for cp in first + passed:
            cp.wait_send()
        for cp in mine:
            cp.wait()

    return _hosted(
        body,
        name=name,
        sched=sched,
        in_specs=[any_spec] * n,
        out_specs=[any_spec] * n,
        out_shape=[jax.ShapeDtypeStruct((N_DEV * s.shape[0], s.shape[1]), s.dtype) for s in shards],
        scratch_shapes=[pltpu.SemaphoreType.DMA((7 * n,)), pltpu.SemaphoreType.DMA((7 * n,)), pltpu.SemaphoreType.DMA((n,))],
    )(*shards)


class _Task:
    def __init__(self, ins, out_shapes, n_remote, n_local, plan, done, aliases=None, sibling=False, chips=False):
        self.ins, self.out_shapes, self.n_remote, self.n_local = list(ins), list(out_shapes), n_remote, n_local
        self.plan, self.done, self.aliases = plan, done, dict(aliases or {})
        self.sibling, self.chips = sibling, chips


class _Sched:
    def __init__(self):
        self.pending = {}

    def at(self, host, task):
        self.pending.setdefault(host, []).append(task)

    def take(self, host):
        return self.pending.pop(host, [])


def _in_hbm(out_shape):
    if isinstance(out_shape, (list, tuple)):
        return [_in_hbm(s) for s in out_shape]
    return pltpu.HBM(out_shape.shape, out_shape.dtype)


def _hosted(body, *, name, sched=None, tasks=(), grid=(), in_specs=None, out_specs=None, out_shape=(), scratch_shapes=(),
            compiler_params=None, input_output_aliases=None):
    tasks = list(tasks) + (sched.take(name) if sched is not None else [])
    scratch_shapes = list(scratch_shapes)
    kwargs = dict(name=name)
    core_aliases = dict(input_output_aliases or {})
    if grid:
        kwargs["grid"] = grid
    if compiler_params is not None:
        kwargs["compiler_params"] = compiler_params
    if not tasks:
        if in_specs is not None:
            kwargs.update(in_specs=in_specs, out_specs=out_specs)
        return pl.pallas_call(body, out_shape=_in_hbm(out_shape), scratch_shapes=scratch_shapes, input_output_aliases=core_aliases, **kwargs)
    assert in_specs is not None and out_specs is not None, name
    with_sibling, with_chips = any(t.sibling for t in tasks), any(t.chips for t in tasks)
    collective_id = {(True, False): 1, (False, True): 2, (True, True): 3}[(with_sibling, with_chips)]
    kwargs["compiler_params"] = dataclasses.replace(
        compiler_params if compiler_params is not None else pltpu.CompilerParams(), collective_id=collective_id)
    single = not isinstance(out_shape, (list, tuple))
    core_shape = [out_shape] if single else list(out_shape)
    core_specs = [out_specs] if single else list(out_specs)
    in_specs = list(in_specs)
    n_in, n_out, n_scr = len(in_specs), len(core_shape), len(scratch_shapes)
    t_ins = [a for t in tasks for a in t.ins]
    t_outs = [s for t in tasks for s in t.out_shapes]
    n_rem = sum(t.n_remote for t in tasks)
    n_loc = sum(t.n_local for t in tasks)
    aliases, pos_in, pos_out = dict(core_aliases), n_in, n_out
    for t in tasks:
        for a, b in t.aliases.items():
            aliases[pos_in + a] = pos_out + b
        pos_in += len(t.ins)
        pos_out += len(t.out_shapes)
    any_spec = pl.BlockSpec(memory_space=pl.ANY)

    def wrapped(*refs):
        core_in, t_in = refs[:n_in], refs[n_in : n_in + len(t_ins)]
        o0 = n_in + len(t_ins)
        core_out, t_out = refs[o0 : o0 + n_out], refs[o0 + n_out : o0 + n_out + len(t_outs)]
        s0 = o0 + n_out + len(t_outs)
        core_scr = refs[s0 : s0 + n_scr]
        send_sems, recv_sems, local_sems = refs[s0 + n_scr :]
        remote, local = [], []
        pi = po = pr = pq = 0
        for t in tasks:
            r, q = t.plan(t_in[pi : pi + len(t.ins)], t_out[po : po + len(t.out_shapes)], send_sems, recv_sems, local_sems, pr, pq)
            assert len(r) == t.n_remote and len(q) == t.n_local
            remote += r
            local += q
            pi, po, pr, pq = pi + len(t.ins), po + len(t.out_shapes), pr + t.n_remote, pq + t.n_local

        def start():
            x, y, c, chips = _place()
            peers = ([(x, y, 1 - c)] if with_sibling else []) + ([(*chip, c) for chip in chips] if with_chips else [])
            barrier = pltpu.get_barrier_semaphore()
            for peer in peers:
                pl.semaphore_signal(barrier, inc=1, device_id=peer, device_id_type=MESH)
            pl.semaphore_wait(barrier, len(peers))
            for cp in local + remote:
                cp.start()

        def finish():
            for cp in remote + local:
                cp.wait()

        if grid:
            ids = [pl.program_id(a) for a in range(len(grid))]
            first = functools.reduce(jnp.logical_and, [i == 0 for i in ids])
            last = functools.reduce(jnp.logical_and, [i == g - 1 for i, g in zip(ids, grid)])
            pl.when(first)(start)
            body(*core_in, *core_out, *core_scr)
            pl.when(last)(finish)
        else:
            start()
            body(*core_in, *core_out, *core_scr)
            finish()

    call = pl.pallas_call(
        wrapped,
        in_specs=in_specs + [any_spec] * len(t_ins),
        out_specs=core_specs + [any_spec] * len(t_outs),
        out_shape=_in_hbm(core_shape + t_outs),
        scratch_shapes=scratch_shapes
        + [pltpu.SemaphoreType.DMA((n_rem,)), pltpu.SemaphoreType.DMA((n_rem,)), pltpu.SemaphoreType.DMA((max(n_loc, 1),))],
        input_output_aliases=aliases,
        **kwargs,
    )

    def run(*operands):
        outs = call(*operands, *t_ins)
        po = n_out
        for t in tasks:
            t.done(list(outs[po : po + len(t.out_shapes)]))
            po += len(t.out_shapes)
        return outs[0] if single else list(outs[:n_out])

    return run


def _comm_only(tasks, *, name):
    _hosted(lambda: None, name=name, tasks=tasks, in_specs=[], out_shape=[], out_specs=[])()


def _rows(ref, block, n_blocks):
    r = ref.shape[0] // n_blocks
    return ref.at[pl.ds(block * r, r), :]


def _remote(src, dst, send_sems, recv_sems, k, to):
    return pltpu.make_async_remote_copy(src_ref=src, dst_ref=dst, send_sem=send_sems.at[k], recv_sem=recv_sems.at[k],
                                        device_id=to, device_id_type=MESH)


def _gather_stage1(shards, done, part=(0, 1, 1), into=None):
    n = len(shards)
    k0, k1, n_parts = part

    def plan(ins, outs, send_sems, recv_sems, local_sems, pr, pq):
        x, y, c, chips = _place()
        mine = 4 * x + 2 * y + c
        remote, local = [], []
        for w in range(n):
            r = shards[w].shape[0]
            rp = r // n_parts
            src = ins[w].at[pl.ds(k0 * rp, (k1 - k0) * rp), :]
            dst = outs[w].at[pl.ds(mine * r + k0 * rp, (k1 - k0) * rp), :]
            local.append(pltpu.make_async_copy(src, dst, local_sems.at[pq + w]))
            for j, to in enumerate([(x, y, 1 - c)] + [(*chip, c) for chip in chips]):
                remote.append(_remote(src, dst, send_sems, recv_sems, pr + 4 * w + j, to))
        return remote, local

    shapes = [jax.ShapeDtypeStruct((N_DEV * s.shape[0], s.shape[1]), s.dtype) for s in shards]
    if into is None:
        return _Task(shards, shapes, 4 * n, n, plan, done, sibling=True, chips=True)
    return _Task(list(shards) + list(into), shapes, 4 * n, n, plan, done, aliases={n + w: w for w in range(n)}, sibling=True, chips=True)


def _gather_stage2(partial, done):
    n = len(partial)

    def plan(ins, outs, send_sems, recv_sems, local_sems, pr, pq):
        x, y, c, chips = _place()
        remote = []
        for w in range(n):
            for j, chip in enumerate(chips):
                rows = _rows(outs[w], 4 * chip[0] + 2 * chip[1] + c, N_DEV)
                remote.append(_remote(rows, rows, send_sems, recv_sems, pr + 3 * w + j, (x, y, 1 - c)))
        return remote, []

    shapes = [jax.ShapeDtypeStruct(p.shape, p.dtype) for p in partial]
    return _Task(partial, shapes, 3 * n, 0, plan, done, aliases={w: w for w in range(n)}, sibling=True)


def _sibling_stage(grads, done):
    n = len(grads)

    def plan(ins, outs, send_sems, recv_sems, local_sems, pr, pq):
        x, y, c, _ = _place()
        remote = []
        for w in range(n):
            for k in range(4):
                src = _rows(ins[w], 4 * (k // 2) + 2 * (k % 2) + (1 - c), N_DEV)
                remote.append(_remote(src, _rows(outs[w], k, 4), send_sems, recv_sems, pr + 4 * w + k, (x, y, 1 - c)))
        return remote, []

    shapes = [jax.ShapeDtypeStruct((g.shape[0] // 2, g.shape[1]), g.dtype) for g in grads]
    return _Task(grads, shapes, 4 * n, 0, plan, done, sibling=True)


def _chips_stage(sends, done, part=(0, 1), into=None):
    n = len(sends)
    k, n_parts = part

    def plan(ins, outs, send_sems, recv_sems, local_sems, pr, pq):
        x, y, c, chips = _place()
        remote = []
        for w in range(n):
            r = sends[w].shape[0] // 3
            rp = r // n_parts
            for j, chip in enumerate(chips):
                rows = pl.ds(j * r + k * rp, rp)
                remote.append(_remote(ins[w].at[rows, :], outs[w].at[rows, :], send_sems, recv_sems, pr + 3 * w + j, (*chip, c)))
        return remote, []

    shapes = [jax.ShapeDtypeStruct(s.shape, s.dtype) for s in sends]
    if into is None:
        return _Task(sends, shapes, 3 * n, 0, plan, done, chips=True)
    return _Task(list(sends) + list(into), shapes, 3 * n, 0, plan, done, aliases={n + w: w for w in range(n)}, chips=True)


def _pair_sum(grad, got, place, *, name):
    R, C = grad.shape
    r = R // N_DEV
    tr = _tile(r, 256, 16)
    nt = r // tr

    def chip_of(s, pr):
        fx = jnp.where((s == 1) | (s == 3), 1, 0)
        fy = jnp.where(s >= 2, 1, 0)
        return pr[0] ^ fx, pr[1] ^ fy

    def grad_map(s, t, pr):
        kx, ky = chip_of(s, pr)
        return ((4 * kx + 2 * ky + pr[2]) * nt + t, 0)

    def got_map(s, t, pr):
        kx, ky = chip_of(s, pr)
        return ((2 * kx + ky) * nt + t, 0)

    def body(pr, g_ref, r_ref, own_ref, send_ref):
        s = pl.program_id(0)
        tot = g_ref[...] + r_ref[...]

        @pl.when(s == 0)
        def _():
            own_ref[...] = tot

        @pl.when(s > 0)
        def _():
            send_ref[...] = tot.astype(BF16)

    return pl.pallas_call(
        body,
        name=name,
        grid_spec=pltpu.PrefetchScalarGridSpec(
            num_scalar_prefetch=1,
            grid=(4, nt),
            in_specs=[pl.BlockSpec((tr, C), grad_map), pl.BlockSpec((tr, C), got_map)],
            out_specs=[
                pl.BlockSpec((tr, C), lambda s, t, pr: (jnp.where(s == 0, t, nt - 1), 0)),
                pl.BlockSpec((tr, C), lambda s, t, pr: (jnp.where(s == 0, 0, (s - 1) * nt + t), 0)),
            ],
        ),
        out_shape=[jax.ShapeDtypeStruct((r, C), F32), jax.ShapeDtypeStruct((3 * r, C), BF16)],
        compiler_params=_params(("arbitrary", "arbitrary"), [((tr, C), F32)] * 4),
    )(place, grad, got)


def _adam_vals(w, g, m, v):
    m = ADAM_B1 * m + (1.0 - ADAM_B1) * g
    v = ADAM_B2 * v + (1.0 - ADAM_B2) * (g * g)
    m_hat = m / (1.0 - ADAM_B1**ADAM_STEP)
    v_hat = v / (1.0 - ADAM_B2**ADAM_STEP)
    delta = -ADAM_LR * (m_hat / (jnp.sqrt(v_hat) + ADAM_EPS) + ADAM_WD * w)
    return delta, m, v


def _adam_big(w3, m3, v3, own, got, layer, transposed, prev, *, name, sched=None):
    _, A, B = w3.shape
    ta = _tile(A, 256, 16)
    nt = A // ta
    b_pad = (-B) % LANES

    def body(*refs):
        w_ref, m_ref, v_ref, own_ref, g1_ref, g2_ref, g3_ref = refs[:7]
        g_ref, d_ref, nm_ref, nv_ref = refs[-4:]
        g = ((own_ref[...] + g1_ref[...].astype(F32)) + g2_ref[...].astype(F32)) + g3_ref[...].astype(F32)
        if transposed:
            if b_pad:
                g = jnp.concatenate([g, jnp.zeros((b_pad, ta), F32)], axis=0)
            g = g.T[:, :B]
        g_ref[...] = g
        d_ref[...], nm_ref[...], nv_ref[...] = _adam_vals(w_ref[...], g, m_ref[...], v_ref[...])

    nat = pl.BlockSpec((None, ta, B), lambda t: (layer, t, 0))
    if transposed:
        parts = [pl.BlockSpec((B, ta), lambda t: (0, t))] + [pl.BlockSpec((B, ta), lambda t, j=j: (j, t)) for j in range(3)]
    else:
        parts = [pl.BlockSpec((ta, B), lambda t: (t, 0))] + [pl.BlockSpec((ta, B), lambda t, j=j: (j * nt + t, 0)) for j in range(3)]
    keep = [] if prev is None else list(prev)
    outs = _hosted(
        body,
        name=name,
        sched=sched,
        grid=(nt,),
        in_specs=[nat, nat, nat] + parts + [pl.BlockSpec(memory_space=pl.ANY)] * len(keep),
        out_specs=[nat] * 4,
        out_shape=[jax.ShapeDtypeStruct(w3.shape, F32)] * 4,
        input_output_aliases={7 + k: k for k in range(len(keep))},
        compiler_params=_params(("arbitrary",), [((ta, B), F32)] * 10),
    )(w3, m3, v3, own, got, got, got, *keep)
    return list(outs)


def _adam_small(w, m, v, parts, *, name, sched=None):
    R = w.shape[0]

    def body(w_ref, m_ref, v_ref, p_ref, g_ref, d_ref, nm_ref, nv_ref):
        g = p_ref[pl.ds(0, R), :]
        for k in range(1, N_DEV):
            g = g + p_ref[pl.ds(k * R, R), :]
        g_ref[...] = g
        d_ref[...], nm_ref[...], nv_ref[...] = _adam_vals(w_ref[...], g, m_ref[...], v_ref[...])

    return _hosted(
        body,
        name=name,
        sched=sched,
        out_shape=[jax.ShapeDtypeStruct((R, LANES), F32)] * 4,
        compiler_params=_params(None, [((R, LANES), F32)] * 16),
    )(w, m, v, parts)


SMALL_NAMES = ("b_gate", "ln_v_g", "ln_v_b", "w_s", "b_s", "sinks", "ln1_g", "ln1_b", "ln2_g", "ln2_b", "ln3_g", "ln3_b")
PACK_ALIGN = 8 * LANES


def _pack(arrays):
    flat = []
    for a in arrays:
        a = a.reshape(-1)
        flat.append(jnp.pad(a, (0, (-a.shape[0]) % PACK_ALIGN)))
    return jnp.concatenate(flat).reshape(-1, LANES)


def _unpack(packed, shapes):
    flat = packed.reshape(-1)
    out, pos = [], 0
    for s in shapes:
        n = 1
        for e in s:
            n *= e
        out.append(flat[pos : pos + n].reshape(s))
        pos += n + (-n) % PACK_ALIGN
    return out


BIG = (("in", "w_in", True), ("a", "w_br_a", True), ("b", "w_br_b", True), ("o", "w_o", False), ("xq", "w_xq", False),
       ("xkv", "w_xkv", False), ("xo", "w_xo", True), ("up", "w_up", True), ("down", "w_down", False))


SMALL_BIG = ("a", "b", "o", "xq", "xkv", "xo")
GATHER_PLAN = (
    (0, ("in",), None, None),
    (0, SMALL_BIG, ("proj_l0",), "swa_fwd_l0"),
    (0, ("up",), ("swa_fwd_l0", "merge_l0"), "oproj_ln_l0"),
    (0, ("down",), ("oproj_ln_l0", "xattn_fwd_l0"), "up_l0"),
    (1, ("in",), ("up_l0",), "down_ln_l0"),
    (1, SMALL_BIG, ("down_ln_l0",), "proj_l1"),
    (1, ("up",), ("down_ln_l0", ("proj_l1", 3)), "swa_fwd_l1"),
    (1, ("down",), (("swa_fwd_l1", 2), "merge_l1", "oproj_ln_l1"), "xattn_fwd_l1"),
)
EARLY_SMALL_BIG = ("o", "xq", "xkv", "xo")
LATE_SMALL_BIG = ("a", "b")
GRAD_HOSTS = {
    1: {"down": ("xattn_bwd_l1", "swa_bwd_l1"), "up": ("xattn_bwd_l1", ("gmlp_bwd_l1", "proj_dw_l1")),
        **{g: ("gmlp_bwd_l1", "proj_dx_l1") for g in EARLY_SMALL_BIG}, **{g: ("proj_dw_l1", "proj_dx_l1") for g in LATE_SMALL_BIG},
        "in": ("proj_dx_l1", ("down_dx_l0", "down_dw_l0"))},
    0: {"down": ("xattn_bwd_l0", ("oproj_dx_l0", "gmlp_bwd_l0")), "up": ("xattn_bwd_l0", "swa_bwd_l0"),
        **{g: ("gmlp_bwd_l0", "proj_dw_l0") for g in EARLY_SMALL_BIG}, **{g: ("proj_dw_l0", "proj_dx_l0") for g in LATE_SMALL_BIG},
        "in": (None, "proj_dx_l0")},
}
SMALL_GRAD_HOSTS = ("proj_dw_l0", "proj_dx_l0")


def kernel(x, mem, w_in, b_gate, ln_v_g, ln_v_b, w_s, b_s, sinks, w_br_a, w_br_b, w_o, ln1_g, ln1_b, w_xq, w_xkv, w_xo, ln2_g, ln2_b, w_up, w_down, ln3_g, ln3_b, loss_target, m_w_in, m_b_gate, m_ln_v_g, m_ln_v_b, m_w_s, m_b_s, m_sinks, m_w_br_a, m_w_br_b, m_w_o, m_ln1_g, m_ln1_b, m_w_xq, m_w_xkv, m_w_xo, m_ln2_g, m_ln2_b, m_w_up, m_w_down, m_ln3_g, m_ln3_b, v_w_in, v_b_gate, v_ln_v_g, v_ln_v_b, v_w_s, v_b_s, v_sinks, v_w_br_a, v_w_br_b, v_w_o, v_ln1_g, v_ln1_b, v_w_xq, v_w_xkv, v_w_xo, v_ln2_g, v_ln2_b, v_w_up, v_w_down, v_ln3_g, v_ln3_b):
    given = dict(locals())
    T, D = x.shape[1], x.shape[2]
    IN, GW, AW, XW, DFF = w_in.shape[2] * N_DEV, ln_v_g.shape[1], w_br_b.shape[1], w_xq.shape[2], w_up.shape[2] * N_DEV
    KVW = (IN - 2 * GW - AW - 2 * D) // 2
    d = Dims(T=T, D=D, IN=IN, GW=GW, G=GW // GROUP_DIM, AW=AW, KVW=KVW, HQ=AW // HEAD_DIM, HKV=KVW // HEAD_DIM, XW=XW,
             XH=XW // X_HEAD_DIM, MEM=mem.shape[1], DFF=DFF)
    place = jnp.stack([lax.axis_index("x"), lax.axis_index("y"), lax.axis_index("c")]).astype(jnp.int32)

    sched = _Sched()
    big_of = {g: (p, tr) for g, p, tr in BIG}

    def shard(l, g):
        p, tr = big_of[g]
        return (given[p][l].T if tr else given[p][l]).astype(BF16)

    Gs = [{}, {}]

    def plan_gather(l, names, hosts1, host2):
        shards = [shard(l, g) for g in names]
        if hosts1 is None:
            Gs[l].update(zip(names, _all_gather(shards, name=f"gather_{names[0]}_l{l}")))
            return

        spans = [(h, 1) if isinstance(h, str) else h for h in hosts1]
        total = sum(cnt for _, cnt in spans)

        def register(i, k0, into):
            host, cnt = spans[i]

            def done(outs):
                if i + 1 < len(spans):
                    register(i + 1, k0 + cnt, outs)
                else:
                    sched.at(host2, _gather_stage2(outs, lambda full: Gs[l].update(zip(names, full))))

            sched.at(host, _gather_stage1(shards, done, part=(k0, k0 + cnt, total), into=into))

        register(0, 0, None)

    for entry in GATHER_PLAN:
        plan_gather(*entry)
    Ss = []
    for l in range(DEPTH):
        S = {n: given[n][l].reshape(1, -1) for n in SMALL_NAMES if n not in ("w_s", "b_s", "sinks")}
        S["w_s"], S["b_sT"], S["sinks"] = w_s[l], b_s[l].T, sinks[l]
        Ss.append(S)

    owns, recvs = {}, {}

    smalls_seen = {}
    gathered_small = []

    def pack_small(src):
        parts = []
        for l in range(DEPTH):
            for n in SMALL_NAMES:
                a = src[l]["b_sT"].T if n == "b_s" else src[l][n]
                parts.append(a[0, : d.HQ] if n == "sinks" else a)
        return _pack(parts)

    def on_small(l, small):
        smalls_seen[l] = small
        if len(smalls_seen) == DEPTH:
            host1, host2 = SMALL_GRAD_HOSTS
            sched.at(host1, _gather_stage1([pack_small(smalls_seen)], lambda part: sched.at(
                host2, _gather_stage2(part, lambda full: gathered_small.append(full[0])))))

    def on_grad(l, g, grad):
        if g == "small":
            return on_small(l, grad)
        sib_host, chips_host = GRAD_HOSTS[l][g]

        def after_sibling(got):
            owns[(l, g)], send = _pair_sum(grad, got[0], place, name=f"grad_pair_sum_{g}_l{l}")
            hosts = chips_host if isinstance(chips_host, tuple) else (chips_host,)

            def register(k, into):
                def done(r):
                    if k + 1 < len(hosts):
                        register(k + 1, r)
                    else:
                        recvs[(l, g)] = r[0]

                task = _chips_stage([send], done, part=(k, len(hosts)), into=into)
                if hosts[k] is None:
                    _comm_only([task], name=f"grad_chips_{g}_l{l}")
                else:
                    sched.at(hosts[k], task)

            register(0, None)

        task = _sibling_stage([grad], after_sibling)
        if sib_host is None:
            _comm_only([task], name=f"grad_sibling_{g}_l{l}")
        else:
            sched.at(sib_host, task)

    loss, dX, bigs, smalls = _local_step(d, x[0], mem[0], loss_target[0], Gs, Ss, sched, on_grad)
    loss = lax.psum(loss[0, 0], ("x", "y", "c"))

    assert not sched.pending, sorted(sched.pending)

    def viewed(p, tr):
        return tr and given[p].shape[2] % LANES != 0

    res = {p: None for _, p, _ in BIG}
    for l in reversed(range(DEPTH)):
        for g, p, tr in BIG:
            view = viewed(p, tr)
            w3, m3, v3 = ((jnp.swapaxes(a, 1, 2) if view else a) for a in (given[p], given["m_" + p], given["v_" + p]))
            res[p] = _adam_big(w3, m3, v3, owns[(l, g)], recvs[(l, g)], l, tr and not view, res[p], name=f"adamw_{g}_l{l}")
    for g, p, tr in BIG:
        if viewed(p, tr):
            res[p] = [jnp.swapaxes(a, 1, 2) for a in res[p]]

    shapes = [given[n].shape[1:] for n in SMALL_NAMES]
    pw, pm, pv = (_pack([given[pre + n][l] for l in range(DEPTH) for n in SMALL_NAMES]) for pre in ("", "m_", "v_"))
    small_out = [_unpack(o, shapes * DEPTH) for o in _adam_small(pw, pm, pv, gathered_small[0], name="adamw_small")]

    names = ["w_in", "b_gate", "ln_v_g", "ln_v_b", "w_s", "b_s", "sinks", "w_br_a", "w_br_b", "w_o", "ln1_g", "ln1_b", "w_xq",
             "w_xkv", "w_xo", "ln2_g", "ln2_b", "w_up", "w_down", "ln3_g", "ln3_b"]
    out = [loss, dX[None]]
    for kind in range(4):
        for n in names:
            if n in SMALL_NAMES:
                i = SMALL_NAMES.index(n)
                out.append(jnp.stack([small_out[kind][l * len(SMALL_NAMES) + i] for l in range(DEPTH)]))
            else:
                out.append(res[n][kind])
    return tuple(out)
```
